```python
import jax, jax.numpy as jnp
from jax import lax
import numpy as np

D_MODEL = 1024
BATCH = 8
SEQ = 4096
DEPTH = 1

CTX_LEN = 256
GRID_W = 64
NA_HEADS = 8
NA_HEAD_DIM = 64
NA_WIDTH = NA_HEADS * NA_HEAD_DIM
WIN_ROWS = 8
WIN_COLS = 16
GLA_HEADS = 4
GLA_DK = 128
GLA_DV = 256
GLA_QK_WIDTH = GLA_HEADS * GLA_DK
GLA_V_WIDTH = GLA_HEADS * GLA_DV
GLA_GATE_RANK = 16
GLA_TAU = 16.0
GLA_CHUNK = 64
ROPE_BASE = 10000.0
PROJ_SIZES = (NA_WIDTH, NA_WIDTH, NA_WIDTH, GLA_QK_WIDTH, GLA_QK_WIDTH, GLA_V_WIDTH, GLA_V_WIDTH, GLA_GATE_RANK, GLA_GATE_RANK, D_MODEL, D_MODEL)
PROJ_WIDTH = sum(PROJ_SIZES)
N_EXPERTS = 256
TOP_K = 8
N_GROUPS = 8
TOPK_GROUPS = 4
EXPERT_FF = 256
SHARED_FF = 256
ROUTED_SCALE = 2.5
MOE_BLOCK = 128
DEEPNORM_ALPHA = (2 * DEPTH) ** 0.25
DEEPNORM_BETA = (8 * DEPTH) ** -0.25
EPS = 1e-6

kernel_name = 'hybrid_natten_gla_moe_dit_layer'


def layer_norm(x, g, b):
    xf = x.astype(jnp.float32)
    mu = xf.mean(-1, keepdims=True)
    var = jnp.square(xf - mu).mean(-1, keepdims=True)
    return ((xf - mu) * lax.rsqrt(var + EPS)).astype(x.dtype) * g + b


def split_heads(t, n_heads):
    b, t_len, _ = t.shape
    return t.reshape(b, t_len, n_heads, -1).transpose(0, 2, 1, 3)


def merge_heads(t):
    b, h, t_len, d = t.shape
    return t.transpose(0, 2, 1, 3).reshape(b, t_len, h * d)


def swiglu(x, wg, wu, wd):
    return (jax.nn.silu(x @ wg) * (x @ wu)) @ wd


def rope_2d(x, row_pos, col_pos):
    half = x.shape[-1] // 2
    quarter = half // 2
    inv_freq = ROPE_BASE ** (-jnp.arange(quarter, dtype=jnp.float32) / quarter)

    def rotate(xh, pos):
        ang = pos[:, None] * inv_freq[None, :]
        cos = jnp.cos(ang).astype(x.dtype)
        sin = jnp.sin(ang).astype(x.dtype)
        x1, x2 = xh[..., :quarter], xh[..., quarter:]
        return jnp.concatenate([x1 * cos - x2 * sin, x2 * cos + x1 * sin], axis=-1)

    return jnp.concatenate([rotate(x[..., :half], row_pos), rotate(x[..., half:], col_pos)], axis=-1)


def neighbourhood_attention(q, k, v, k_ctx, v_ctx, rpb):
    b, h, rows, w, dh = q.shape
    wr = min(WIN_ROWS, rows)
    col_start = np.clip(np.arange(w) - WIN_COLS // 2, 0, w - WIN_COLS)
    col_idx = col_start[:, None] + np.arange(WIN_COLS)[None, :]
    col_rel = col_idx - np.arange(w)[:, None] + (WIN_COLS - 1)
    rpb_cols = rpb[:, :, col_rel]
    n_loc = wr * WIN_COLS

    def one_row(r):
        start = jnp.clip(r - wr // 2, 0, rows - wr)
        q_r = lax.dynamic_index_in_dim(q, r, axis=2, keepdims=False)
        k_win = lax.dynamic_slice_in_dim(k, start, wr, axis=2)[:, :, :, col_idx]
        v_win = lax.dynamic_slice_in_dim(v, start, wr, axis=2)[:, :, :, col_idx]
        row_rel = start + jnp.arange(wr) - r + (WIN_ROWS - 1)
        bias = jnp.take(rpb_cols, row_rel, axis=1).transpose(0, 2, 1, 3)
        s_loc = jnp.einsum('bhqd,bhrqcd->bhqrc', q_r, k_win) + bias[None]
        s_ctx = jnp.einsum('bhqd,bhkd->bhqk', q_r, k_ctx)
        s = jnp.concatenate([s_loc.reshape(b, h, w, n_loc), s_ctx], axis=-1)
        p = jax.nn.softmax(s.astype(jnp.float32), axis=-1).astype(q.dtype)
        p_loc = p[..., :n_loc].reshape(b, h, w, wr, WIN_COLS)
        return (jnp.einsum('bhqrc,bhrqcd->bhqd', p_loc, v_win)
                + jnp.einsum('bhqk,bhkd->bhqd', p[..., n_loc:], v_ctx))

    out = lax.map(one_row, jnp.arange(rows))
    return out.transpose(1, 0, 3, 2, 4).reshape(b, rows * w, h * dh)


def context_attention(q, k, v):
    s = jnp.einsum('bhqd,bhkd->bhqk', q, k)
    p = jax.nn.softmax(s.astype(jnp.float32), axis=-1).astype(q.dtype)
    return merge_heads(jnp.einsum('bhqk,bhkd->bhqd', p, v))


def gla_log_decay(lr, w2, b2):
    return split_heads(jax.nn.log_sigmoid((lr @ w2 + b2).astype(jnp.float32)) / GLA_TAU, GLA_HEADS)


def gla_chunked(q, k, v, log_a, s0):
    b, h, t_len, dk = q.shape
    dv = v.shape[-1]
    n_chunks = t_len // GLA_CHUNK

    def to_chunks(t):
        return t.astype(jnp.float32).reshape(b, h, n_chunks, GLA_CHUNK, t.shape[-1]).transpose(2, 0, 1, 3, 4)

    causal = jnp.tril(jnp.ones((GLA_CHUNK, GLA_CHUNK), dtype=bool))[:, :, None]

    def chunk_step(state, xs):
        qc, kc, vc, ac = xs
        cum = jnp.cumsum(ac, axis=-2)
        pair = jnp.exp(jnp.where(causal, cum[..., :, None, :] - cum[..., None, :, :], -jnp.inf))
        scores = jnp.einsum('bhid,bhjd,bhijd->bhij', qc, kc, pair)
        out = (jnp.einsum('bhij,bhje->bhie', scores, vc)
               + jnp.einsum('bhid,bhde->bhie', qc * jnp.exp(cum), state))
        total = cum[..., -1:, :]
        state = (jnp.exp(total)[..., 0, :, None] * state
                 + jnp.einsum('bhjd,bhje->bhde', kc * jnp.exp(total - cum), vc))
        return state, out

    state, out = lax.scan(chunk_step, s0, (to_chunks(q), to_chunks(k), to_chunks(v), to_chunks(log_a)))
    return out.transpose(1, 2, 0, 3, 4).reshape(b, h, t_len, dv).astype(v.dtype), state


def gla_final_state(k, v, log_a):
    cum = jnp.cumsum(log_a.astype(jnp.float32), axis=2)
    return jnp.einsum('bhtd,bhte->bhde', k.astype(jnp.float32) * jnp.exp(cum[:, :, -1:, :] - cum), v.astype(jnp.float32))


def gla_readout(o_f, o_b, g, norm_w):
    o = (o_f + o_b).transpose(0, 2, 1, 3).astype(jnp.float32)
    o = o * lax.rsqrt(jnp.mean(jnp.square(o), axis=-1, keepdims=True) + EPS)
    b, t_len = o.shape[:2]
    return o.reshape(b, t_len, -1).astype(g.dtype) * norm_w * jax.nn.silu(g)


def branch_merge(out_a, out_b, gate_a, gate_b, w_br_a, w_br_b, w_out):
    y = jax.nn.sigmoid(gate_a) * (out_a @ w_br_a) + jax.nn.sigmoid(gate_b) * (out_b @ w_br_b)
    return y @ w_out


def token_mixer(h_lat, h_ctx, w_in, rpb, w_dec_f, b_dec_f, w_dec_b, b_dec_b, norm_w, w_br_a, w_br_b, w_out, need_ctx):
    b, t_len, _ = h_lat.shape
    rows = t_len // GRID_W
    splits = np.cumsum(PROJ_SIZES)[:-1].tolist()
    qa, ka, va, qb, kb, vb, gb, lrf, lrb, gate_a, gate_b = jnp.split(h_lat @ w_in, splits, axis=-1)
    qa_c, ka_c, va_c, qb_c, kb_c, vb_c, gb_c, lrf_c, lrb_c, gate_a_c, gate_b_c = jnp.split(h_ctx @ w_in, splits, axis=-1)

    na_scale = NA_HEAD_DIM ** -0.5

    def to_grid(t):
        return split_heads(t, NA_HEADS).reshape(b, NA_HEADS, rows, GRID_W, NA_HEAD_DIM)

    ka_ch, va_ch = split_heads(ka_c, NA_HEADS), split_heads(va_c, NA_HEADS)
    out_a = neighbourhood_attention(to_grid(qa) * na_scale, to_grid(ka), to_grid(va), ka_ch, va_ch, rpb)

    pos = jnp.arange(t_len)
    row_pos = (pos // GRID_W).astype(jnp.float32)
    col_pos = (pos % GRID_W).astype(jnp.float32)
    gla_scale = GLA_DK ** -0.5
    qb_l = rope_2d(split_heads(qb, GLA_HEADS), row_pos, col_pos) * gla_scale
    kb_l = rope_2d(split_heads(kb, GLA_HEADS), row_pos, col_pos)
    vb_l = split_heads(vb, GLA_HEADS)
    lgf_l = gla_log_decay(lrf, w_dec_f, b_dec_f)
    lgb_l = gla_log_decay(lrb, w_dec_b, b_dec_b)
    kb_ch, vb_ch = split_heads(kb_c, GLA_HEADS), split_heads(vb_c, GLA_HEADS)
    lgf_c = gla_log_decay(lrf_c, w_dec_f, b_dec_f)
    lgb_c = gla_log_decay(lrb_c, w_dec_b, b_dec_b)

    def flip(t):
        return jnp.flip(t, axis=2)

    if need_ctx:
        qb_ch = split_heads(qb_c, GLA_HEADS) * gla_scale
        s0 = jnp.zeros((b, GLA_HEADS, GLA_DK, GLA_DV), jnp.float32)
        o_cf, s_f = gla_chunked(qb_ch, kb_ch, vb_ch, lgf_c, s0)
        o_cb, s_b = gla_chunked(flip(qb_ch), flip(kb_ch), flip(vb_ch), flip(lgb_c), s0)
    else:
        s_f = gla_final_state(kb_ch, vb_ch, lgf_c)
        s_b = gla_final_state(flip(kb_ch), flip(vb_ch), flip(lgb_c))
    o_lf, _ = gla_chunked(qb_l, kb_l, vb_l, lgf_l, s_f)
    o_lb, _ = gla_chunked(flip(qb_l), flip(kb_l), flip(vb_l), flip(lgb_l), s_b)
    out_b = gla_readout(o_lf, flip(o_lb), gb, norm_w)

    y_lat = branch_merge(out_a, out_b, gate_a, gate_b, w_br_a, w_br_b, w_out)
    if not need_ctx:
        return y_lat, None
    out_a_c = context_attention(split_heads(qa_c, NA_HEADS) * na_scale, ka_ch, va_ch)
    out_b_c = gla_readout(o_cf, flip(o_cb), gb_c, norm_w)
    y_ctx = branch_merge(out_a_c, out_b_c, gate_a_c, gate_b_c, w_br_a, w_br_b, w_out)
    return y_lat, y_ctx


def moe_ffn(h, router_w, router_bias, w_gate, w_up, w_down, sh_gate, sh_up, sh_down):
    n, d = h.shape
    n_exp = w_gate.shape[0]
    scores = jax.nn.sigmoid(h.astype(jnp.float32) @ router_w.astype(jnp.float32))
    biased = scores + router_bias.astype(jnp.float32)
    group_score = lax.top_k(biased.reshape(n, N_GROUPS, n_exp // N_GROUPS), 2)[0].sum(-1)
    _, top_groups = lax.top_k(group_score, TOPK_GROUPS)
    group_keep = jnp.any(top_groups[:, :, None] == jnp.arange(N_GROUPS)[None, None, :], axis=1)
    masked = jnp.where(jnp.repeat(group_keep, n_exp // N_GROUPS, axis=1), biased, -jnp.inf)
    _, top_e = lax.top_k(masked, TOP_K)
    top_s = jnp.take_along_axis(scores, top_e, axis=1)
    top_w = (top_s / top_s.sum(-1, keepdims=True) * ROUTED_SCALE).astype(h.dtype)

    nk = n * TOP_K
    n_blocks = -(-nk // MOE_BLOCK) + n_exp
    e_flat = top_e.reshape(-1)
    order = jnp.argsort(e_flat)
    e_sorted = e_flat[order]
    tok_sorted = (order // TOP_K).astype(jnp.int32)
    w_sorted = top_w.reshape(-1)[order]
    counts = jnp.bincount(e_flat, length=n_exp)
    padded = (counts + MOE_BLOCK - 1) // MOE_BLOCK * MOE_BLOCK
    start = jnp.cumsum(counts) - counts
    pad_end = jnp.cumsum(padded)
    pad_start = pad_end - padded
    dest = pad_start[e_sorted] + jnp.arange(nk) - start[e_sorted]
    slot_tok = jnp.full((n_blocks * MOE_BLOCK,), n, jnp.int32).at[dest].set(tok_sorted)
    slot_w = jnp.zeros((n_blocks * MOE_BLOCK,), h.dtype).at[dest].set(w_sorted)
    block_exp = jnp.clip(jnp.searchsorted(pad_end, jnp.arange(n_blocks) * MOE_BLOCK, side='right'), 0, n_exp - 1)
    h_pad = jnp.concatenate([h, jnp.zeros((1, d), h.dtype)], axis=0)

    def block_step(acc, blk):
        idx, wts, e = blk
        y = swiglu(h_pad[idx], w_gate[e], w_up[e], w_down[e]) * wts[:, None]
        return acc.at[idx].add(y), None

    routed, _ = lax.scan(block_step, jnp.zeros((n + 1, d), h.dtype),
                         (slot_tok.reshape(n_blocks, MOE_BLOCK), slot_w.reshape(n_blocks, MOE_BLOCK), block_exp))
    return swiglu(h, sh_gate, sh_up, sh_down) + routed[:n]


def setup_inputs(seed: int = 0) -> dict:
    key = jax.random.key(seed)
    ks = jax.random.split(key, 28)

    def nrm(k, shape, scale):
        return jax.random.normal(k, shape, jnp.float32) * scale

    L, D = DEPTH, D_MODEL
    return {
        'x': nrm(ks[0], (BATCH, SEQ, D), 1.0),
        'c': nrm(ks[1], (BATCH, D), 1.0),
        'ctx': nrm(ks[2], (BATCH, CTX_LEN, D), 1.0),
        'c_ctx': nrm(ks[3], (D,), 1.0),
        'w_mod': nrm(ks[4], (L, D, 6 * D), D ** -0.5),
        'b_mod': nrm(ks[5], (L, 6 * D), 0.02),
        'w_in': nrm(ks[6], (L, D, PROJ_WIDTH), D ** -0.5),
        'na_rpb': nrm(ks[7], (L, NA_HEADS, 2 * WIN_ROWS - 1, 2 * WIN_COLS - 1), 0.1),
        'gla_w_decay_f': nrm(ks[8], (L, GLA_GATE_RANK, GLA_QK_WIDTH), GLA_GATE_RANK ** -0.5),
        'gla_b_decay_f': nrm(ks[9], (L, GLA_QK_WIDTH), 0.1),
        'gla_w_decay_b': nrm(ks[10], (L, GLA_GATE_RANK, GLA_QK_WIDTH), GLA_GATE_RANK ** -0.5),
        'gla_b_decay_b': nrm(ks[11], (L, GLA_QK_WIDTH), 0.1),
        'gla_norm_w': 1.0 + nrm(ks[12], (L, GLA_V_WIDTH), 0.02),
        'w_branch_a': nrm(ks[13], (L, NA_WIDTH, D), NA_WIDTH ** -0.5 * DEEPNORM_BETA),
        'w_branch_b': nrm(ks[14], (L, GLA_V_WIDTH, D), GLA_V_WIDTH ** -0.5 * DEEPNORM_BETA),
        'w_out': nrm(ks[15], (L, D, D), D ** -0.5 * DEEPNORM_BETA),
        'ln1_g': 1.0 + nrm(ks[16], (L, D), 0.02),
        'ln1_b': nrm(ks[17], (L, D), 0.02),
        'router_w': nrm(ks[18], (L, D, N_EXPERTS), D ** -0.5),
        'router_bias': nrm(ks[19], (L, N_EXPERTS), 0.01),
        'exp_w_gate': nrm(ks[20], (L, N_EXPERTS, D, EXPERT_FF), D ** -0.5),
        'exp_w_up': nrm(ks[21], (L, N_EXPERTS, D, EXPERT_FF), D ** -0.5),
        'exp_w_down': nrm(ks[22], (L, N_EXPERTS, EXPERT_FF, D), EXPERT_FF ** -0.5 * DEEPNORM_BETA),
        'sh_w_gate': nrm(ks[23], (L, D, SHARED_FF), D ** -0.5),
        'sh_w_up': nrm(ks[24], (L, D, SHARED_FF), D ** -0.5),
        'sh_w_down': nrm(ks[25], (L, SHARED_FF, D), SHARED_FF ** -0.5 * DEEPNORM_BETA),
        'ln2_g': 1.0 + nrm(ks[26], (L, D), 0.02),
        'ln2_b': nrm(ks[27], (L, D), 0.02),
    }


def reference(x, c, ctx, c_ctx, w_mod, b_mod, w_in, na_rpb, gla_w_decay_f, gla_b_decay_f, gla_w_decay_b, gla_b_decay_b,
              gla_norm_w, w_branch_a, w_branch_b, w_out, ln1_g, ln1_b, router_w, router_bias, exp_w_gate, exp_w_up,
              exp_w_down, sh_w_gate, sh_w_up, sh_w_down, ln2_g, ln2_b):
    b, t_len, d = x.shape
    ctx_len = ctx.shape[1]
    for layer in range(DEPTH):
        need_ctx = layer < DEPTH - 1
        mod = jax.nn.silu(c) @ w_mod[layer] + b_mod[layer]
        mod_c = jax.nn.silu(c_ctx) @ w_mod[layer] + b_mod[layer]
        sh1, sc1, g1, sh2, sc2, g2 = [m[:, None, :] for m in jnp.split(mod, 6, axis=-1)]
        sh1c, sc1c, g1c, sh2c, sc2c, g2c = jnp.split(mod_c, 6)

        y_lat, y_ctx = token_mixer(x * (1.0 + sc1) + sh1, ctx * (1.0 + sc1c) + sh1c, w_in[layer], na_rpb[layer],
                                   gla_w_decay_f[layer], gla_b_decay_f[layer], gla_w_decay_b[layer], gla_b_decay_b[layer],
                                   gla_norm_w[layer], w_branch_a[layer], w_branch_b[layer], w_out[layer], need_ctx)
        x = layer_norm(DEEPNORM_ALPHA * x + g1 * y_lat, ln1_g[layer], ln1_b[layer])
        h2 = x * (1.0 + sc2) + sh2
        moe_args = (router_w[layer], router_bias[layer], exp_w_gate[layer], exp_w_up[layer], exp_w_down[layer],
                    sh_w_gate[layer], sh_w_up[layer], sh_w_down[layer])
        if need_ctx:
            ctx = layer_norm(DEEPNORM_ALPHA * ctx + g1c * y_ctx, ln1_g[layer], ln1_b[layer])
            h2c = ctx * (1.0 + sc2c) + sh2c
            f = moe_ffn(jnp.concatenate([h2c.reshape(-1, d), h2.reshape(-1, d)], axis=0), *moe_args)
            f_ctx = f[:b * ctx_len].reshape(b, ctx_len, d)
            f_lat = f[b * ctx_len:].reshape(b, t_len, d)
            ctx = layer_norm(DEEPNORM_ALPHA * ctx + g2c * f_ctx, ln2_g[layer], ln2_b[layer])
        else:
            f_lat = moe_ffn(h2.reshape(-1, d), *moe_args).reshape(b, t_len, d)
        x = layer_norm(DEEPNORM_ALPHA * x + g2 * f_lat, ln2_g[layer], ln2_b[layer])
    return x
```

```python
import functools

import numpy as np
import jax
import jax.numpy as jnp
from jax import lax
from jax.experimental import pallas as pl
from jax.experimental.pallas import tpu as pltpu

F32 = jnp.float32
BF16 = jnp.bfloat16
HIGHEST = lax.Precision.HIGHEST

GRID_W = 64
NA_HEADS = 8
NA_HEAD_DIM = 64
NA_WIDTH = NA_HEADS * NA_HEAD_DIM
WIN_ROWS = 8
WIN_COLS = 16
GLA_HEADS = 4
GLA_DK = 128
GLA_DV = 256
GLA_QK_WIDTH = GLA_HEADS * GLA_DK
GLA_V_WIDTH = GLA_HEADS * GLA_DV
GLA_GATE_RANK = 16
GLA_TAU = 16.0
ROPE_BASE = 10000.0
N_EXPERTS = 256
TOP_K = 8
N_GROUPS = 8
TOPK_GROUPS = 4
GROUP_SIZE = N_EXPERTS // N_GROUPS
ROUTED_SCALE = 2.5
DEPTH = 1
DEEPNORM_ALPHA = (2 * DEPTH) ** 0.25
EPS = 1e-6

LANES = 128
NA_ROW_BLOCK = 4
NA_UNION_ROWS = NA_ROW_BLOCK + WIN_ROWS - 1
GLA_CHUNK = 256
GLA_DIAG = 16
EXPERT_BLOCK = 256
NEG_BIG = -1e30
VMEM_LIMIT = 56 * 1024 * 1024


def _params(n_axes, vmem=VMEM_LIMIT):
    return pltpu.CompilerParams(dimension_semantics=("arbitrary",) * n_axes, vmem_limit_bytes=vmem)


def _sigmoid(v):
    return 1.0 / (1.0 + jnp.exp(-v))


def _silu(v):
    return v * _sigmoid(v)


def _mod_kernel(c_ref, w_ref, b_ref, o_ref):
    o_ref[...] = jnp.dot(_silu(c_ref[...]), w_ref[...], preferred_element_type=F32, precision=HIGHEST) + b_ref[...]


def _modulation(c_all, w_mod, b_mod):
    rows, d = c_all.shape
    n = w_mod.shape[1]
    bn = 512
    return pl.pallas_call(
        _mod_kernel,
        out_shape=jax.ShapeDtypeStruct((rows, n), F32),
        grid=(n // bn,),
        in_specs=[pl.BlockSpec((rows, d), lambda j: (0, 0)),
                  pl.BlockSpec((d, bn), lambda j: (0, j)),
                  pl.BlockSpec((1, bn), lambda j: (0, j))],
        out_specs=pl.BlockSpec((rows, bn), lambda j: (0, j)),
        compiler_params=_params(1),
        name="modulation",
    )(c_all, w_mod, b_mod.reshape(1, n))


def _swap32(v):
    lane = lax.broadcasted_iota(jnp.int32, v.shape, 1)
    return jnp.where((lane % 64) < 32, pltpu.roll(v, 96, 1), pltpu.roll(v, 32, 1))


def _proj_kernel(plan, has_rope, *refs):
    x_ref, sc_ref, sh_ref, w_ref = refs[:4]
    pos = 4
    if has_rope:
        cos_ref, sin_ref = refs[4:6]
        pos = 6
    out_refs = refs[pos:]
    h = (x_ref[...] * (1.0 + sc_ref[0]) + sh_ref[0]).astype(BF16)
    col = 0
    for out_ref, (width, kinds) in zip(out_refs, plan):
        for j, kind in enumerate(kinds):
            cw = width // len(kinds)
            c0 = j * cw
            acc = jnp.dot(h, w_ref[:, col + c0:col + c0 + cw], preferred_element_type=F32)
            if kind[0] == "scale":
                acc = acc * kind[1]
            elif kind[0] == "rope":
                cos, sin = cos_ref[...], sin_ref[...]
                pieces = []
                for p in range(cw // LANES):
                    v = acc[:, p * LANES:(p + 1) * LANES]
                    pieces.append((v * cos + _swap32(v) * sin) * kind[1])
                acc = jnp.concatenate(pieces, axis=1)
            out_ref[:, c0:c0 + cw] = acc.astype(out_ref.dtype)
        col += width


def _projection(x2d, sc, sh, w, plan, out_dtypes, tile, tiles_per_mod, rope=None):
    n, d = x2d.shape
    in_specs = [pl.BlockSpec((tile, d), lambda i: (i, 0)),
                pl.BlockSpec((1, 1, d), lambda i: (i // tiles_per_mod, 0, 0)),
                pl.BlockSpec((1, 1, d), lambda i: (i // tiles_per_mod, 0, 0)),
                pl.BlockSpec(w.shape, lambda i: (0, 0), pipeline_mode=pl.Buffered(1))]
    args = [x2d, sc, sh, w]
    if rope is not None:
        in_specs += [pl.BlockSpec((tile, LANES), lambda i: (i % tiles_per_mod, 0))] * 2
        args += list(rope)
    out_shape = [jax.ShapeDtypeStruct((n, wd), dt) for (wd, _), dt in zip(plan, out_dtypes)]
    out_specs = [pl.BlockSpec((tile, wd), lambda i: (i, 0)) for (wd, _) in plan]
    return pl.pallas_call(
        functools.partial(_proj_kernel, plan, rope is not None),
        out_shape=out_shape,
        grid=(n // tile,),
        in_specs=in_specs,
        out_specs=out_specs,
        compiler_params=_params(1),
        name="in_proj" if rope is not None else "ctx_proj",
    )(*args)


def _rope_tables(t_len):
    half = GLA_DK // 2
    quarter = half // 2
    inv_freq = ROPE_BASE ** (-jnp.arange(quarter, dtype=F32) / quarter)
    pos = jnp.arange(t_len)
    row_ang = (pos // GRID_W).astype(F32)[:, None] * inv_freq[None, :]
    col_ang = (pos % GRID_W).astype(F32)[:, None] * inv_freq[None, :]
    cr, sr, cc, sn = jnp.cos(row_ang), jnp.sin(row_ang), jnp.cos(col_ang), jnp.sin(col_ang)
    cos = jnp.concatenate([cr, cr, cc, cc], axis=1)
    sin = jnp.concatenate([-sr, sr, -sn, sn], axis=1)
    return cos, sin


def _na_bias_tables(rpb):
    rb, ur, w = NA_ROW_BLOCK, NA_UNION_ROWS, GRID_W
    i = np.arange(rb)[:, None, None, None]
    c = np.arange(w)[None, :, None, None]
    j = np.arange(ur)[None, None, :, None]
    kc = np.arange(w)[None, None, None, :]
    col_start = np.clip(c - WIN_COLS // 2, 0, w - WIN_COLS)
    col_ok = (kc >= col_start) & (kc < col_start + WIN_COLS)
    col_rel = np.clip(kc - c + (WIN_COLS - 1), 0, 2 * WIN_COLS - 2)
    tables = []
    for lo, rel in ((0 * i, j - i + (WIN_ROWS - 1)),
                    (i, j - i + (WIN_ROWS - 1) - WIN_ROWS // 2),
                    (0 * i + (ur - WIN_ROWS), j - i)):
        ok = col_ok & (j >= lo) & (j < lo + WIN_ROWS)
        rel = np.clip(rel, 0, 2 * WIN_ROWS - 2)
        ok = np.broadcast_to(ok, (rb, w, ur, w)).reshape(rb * w, ur * w)
        rel_b = np.broadcast_to(rel, (rb, w, ur, w)).reshape(rb * w, ur * w)
        col_b = np.broadcast_to(col_rel, (rb, w, ur, w)).reshape(rb * w, ur * w)
        tables.append(jnp.where(ok[None], rpb[:, rel_b, col_b], NEG_BIG))
    return jnp.stack(tables)


def _na_kernel(rows, q_ref, k_ref, v_ref, kc_ref, vc_ref, bias_ref, o_ref):
    rb = pl.program_id(1)
    ustart = jnp.clip(rb * NA_ROW_BLOCK - WIN_ROWS // 2, 0, rows - NA_UNION_ROWS)
    k0 = pl.multiple_of(ustart * GRID_W, GRID_W)
    nk = NA_UNION_ROWS * GRID_W
    nt = (((1,), (1,)), ((), ()))
    lane = lax.broadcasted_iota(jnp.int32, (NA_ROW_BLOCK * GRID_W, LANES), 1)
    for p in range(NA_WIDTH // LANES):
        ls = slice(p * LANES, (p + 1) * LANES)
        q = q_ref[:, ls]
        ku = k_ref[pl.ds(k0, nk), ls]
        vu = v_ref[pl.ds(k0, nk), ls]
        kc = kc_ref[:, ls]
        vc = vc_ref[:, ls]
        out = jnp.zeros(q.shape, F32)
        for hh in range(LANES // NA_HEAD_DIM):
            mine = (lane // NA_HEAD_DIM) == hh
            qm = jnp.where(mine, q, jnp.zeros_like(q))
            s_loc = lax.dot_general(qm, ku, nt, preferred_element_type=F32) + bias_ref[p * 2 + hh]
            s_ctx = lax.dot_general(qm, kc, nt, preferred_element_type=F32)
            m = jnp.maximum(jnp.max(s_loc, axis=1, keepdims=True), jnp.max(s_ctx, axis=1, keepdims=True))
            p_loc = jnp.exp(s_loc - m)
            p_ctx = jnp.exp(s_ctx - m)
            denom = jnp.sum(p_loc, axis=1, keepdims=True) + jnp.sum(p_ctx, axis=1, keepdims=True)
            o = (jnp.dot(p_loc.astype(BF16), vu, preferred_element_type=F32)
                 + jnp.dot(p_ctx.astype(BF16), vc, preferred_element_type=F32)) / denom
            out = jnp.where(mine, o, out)
        o_ref[:, ls] = out.astype(o_ref.dtype)


def _na_attention(na_qkv, na_kv_ctx, bias_tables, batch, t_len, ctx_len):
    rows = t_len // GRID_W
    n_rb = rows // NA_ROW_BLOCK
    tq = NA_ROW_BLOCK * GRID_W
    qkv = na_qkv.reshape(batch, t_len, 3 * NA_WIDTH)
    kvc = na_kv_ctx.reshape(batch, ctx_len, 2 * NA_WIDTH)

    def bias_idx(b, r):
        return (jnp.where(r == 0, 0, jnp.where(r == n_rb - 1, 2, 1)), 0, 0, 0)

    return pl.pallas_call(
        functools.partial(_na_kernel, rows),
        out_shape=jax.ShapeDtypeStruct((batch, t_len, NA_WIDTH), BF16),
        grid=(batch, n_rb),
        in_specs=[pl.BlockSpec((None, tq, NA_WIDTH), lambda b, r: (b, r, 0)),
                  pl.BlockSpec((None, t_len, NA_WIDTH), lambda b, r: (b, 0, 1)),
                  pl.BlockSpec((None, t_len, NA_WIDTH), lambda b, r: (b, 0, 2)),
                  pl.BlockSpec((None, ctx_len, NA_WIDTH), lambda b, r: (b, 0, 0)),
                  pl.BlockSpec((None, ctx_len, NA_WIDTH), lambda b, r: (b, 0, 1)),
                  pl.BlockSpec((None,) + bias_tables.shape[1:], bias_idx)],
        out_specs=pl.BlockSpec((None, tq, NA_WIDTH), lambda b, r: (b, r, 0)),
        compiler_params=_params(2),
        name="na_attention",
    )(qkv, qkv, qkv, kvc, kvc, bias_tables)


def _log_decay(lr, w2, b2):
    z = jnp.dot(lr, w2, preferred_element_type=F32) + b2
    return (jnp.minimum(z, 0.0) - jnp.log(1.0 + jnp.exp(-jnp.abs(z)))) * (1.0 / GLA_TAU)


def _gla_masks(c, reverse):
    i = lax.broadcasted_iota(jnp.int32, (c, c), 0)
    j = lax.broadcasted_iota(jnp.int32, (c, c), 1)
    if reverse:
        i, j = j, i
    masks = []
    s = c // 2
    while s >= GLA_DIAG:
        masks.append(((i // (2 * s)) == (j // (2 * s))) & ((i % (2 * s)) >= s) & ((j % (2 * s)) < s))
        s //= 2
    masks.append(((i // GLA_DIAG) == (j // GLA_DIAG)) & (j <= i))
    return masks


def _block_refs(cum, s, reverse, diag):
    c = cum.shape[0]
    span = s if diag else 2 * s
    parts = []
    for p in range(c // span):
        if diag:
            r = p * span + (span - 1 if reverse else 0)
        else:
            r = p * span + (s - 1 if reverse else s)
        parts.append(jnp.broadcast_to(cum[r:r + 1, :], (span, cum.shape[1])))
    return jnp.concatenate(parts, axis=0)


def _cumsum_rows(a, reverse):
    c = a.shape[0]
    i = lax.broadcasted_iota(jnp.int32, (c, c), 0)
    j = lax.broadcasted_iota(jnp.int32, (c, c), 1)
    tri = jnp.where((j >= i) if reverse else (j <= i), 1.0, 0.0).astype(F32)
    return jnp.dot(tri, a, preferred_element_type=F32, precision=HIGHEST)


def _gla_chunk(q, k, v, a, state_t, reverse, want_out):
    c = k.shape[0]
    nt = (((1,), (1,)), ((), ()))
    cum = _cumsum_rows(a, reverse)
    last = 0 if reverse else c - 1
    total = cum[last:last + 1, :]
    out = None
    if want_out:
        q_in = (q * jnp.exp(cum)).astype(BF16)
        out = lax.dot_general(q_in, state_t.astype(BF16), nt, preferred_element_type=F32)
        masks = _gla_masks(c, reverse)
        scores = jnp.zeros((c, c), F32)
        s = c // 2
        level = 0
        while True:
            diag = s < GLA_DIAG
            d = cum - _block_refs(cum, GLA_DIAG if diag else s, reverse, diag)
            if diag:
                qs = q * jnp.exp(d)
                ks = k * jnp.exp(-d)
            else:
                qs = q * jnp.exp(jnp.minimum(d, 0.0))
                ks = k * jnp.exp(jnp.minimum(-d, 0.0))
            att = lax.dot_general(qs.astype(BF16), ks.astype(BF16), nt, preferred_element_type=F32)
            scores = scores + jnp.where(masks[level], att, 0.0)
            if diag:
                break
            s //= 2
            level += 1
        out = out + jnp.dot(scores.astype(BF16), v, preferred_element_type=F32)
    k_out = (k * jnp.exp(total - cum)).astype(BF16)
    upd = lax.dot_general(v, k_out, (((0,), (0,)), ((), ())), preferred_element_type=F32)
    return out, jnp.exp(total) * state_t + upd


def _gla_kernel(qf_ref, kf_ref, vf_ref, lrf_ref, qb_ref, kb_ref, vb_ref, lrb_ref,
                kc_ref, vc_ref, lrc_ref, wf_ref, bf_ref, wb_ref, bb_ref,
                of_ref, ob_ref, sf_ref, sb_ref):
    c = pl.program_id(2)

    @pl.when(c == 0)
    def _():
        kc = kc_ref[...].astype(F32)
        vc = vc_ref[...]
        lrc = lrc_ref[...]
        zero = jnp.zeros(sf_ref.shape, F32)
        _, sf_ref[...] = _gla_chunk(None, kc, vc, _log_decay(lrc, wf_ref[...], bf_ref[...]), zero, False, False)
        _, sb_ref[...] = _gla_chunk(None, kc, vc, _log_decay(lrc, wb_ref[...], bb_ref[...]), zero, True, False)

    @pl.when(c > 0)
    def _():
        o, sf_ref[...] = _gla_chunk(qf_ref[...].astype(F32), kf_ref[...].astype(F32), vf_ref[...],
                                    _log_decay(lrf_ref[...], wf_ref[...], bf_ref[...]), sf_ref[...], False, True)
        of_ref[...] = o.astype(of_ref.dtype)
        o, sb_ref[...] = _gla_chunk(qb_ref[...].astype(F32), kb_ref[...].astype(F32), vb_ref[...],
                                    _log_decay(lrb_ref[...], wb_ref[...], bb_ref[...]), sb_ref[...], True, True)
        ob_ref[...] = o.astype(ob_ref.dtype)


def _gla(gla_qk, vb, lr, k_ctx, v_ctx, lr_ctx, wf, bf, wb, bb, batch, t_len, ctx_len):
    nc = t_len // GLA_CHUNK
    h = GLA_HEADS
    qk = gla_qk.reshape(batch, t_len, 2 * GLA_QK_WIDTH)
    v3 = vb.reshape(batch, t_len, GLA_V_WIDTH)
    lr3 = lr.reshape(batch, t_len, LANES)
    kc3 = k_ctx.reshape(batch, ctx_len, GLA_QK_WIDTH)
    vc3 = v_ctx.reshape(batch, ctx_len, GLA_V_WIDTH)
    lrc3 = lr_ctx.reshape(batch, ctx_len, LANES)

    def fwd(c):
        return jnp.maximum(c - 1, 0)

    def bwd(c):
        return nc - 1 - jnp.maximum(c - 1, 0)

    cq = (None, GLA_CHUNK, GLA_DK)
    cv = (None, GLA_CHUNK, GLA_DV)
    in_specs = [
        pl.BlockSpec(cq, lambda b, hh, c: (b, fwd(c), hh)),
        pl.BlockSpec(cq, lambda b, hh, c: (b, fwd(c), h + hh)),
        pl.BlockSpec(cv, lambda b, hh, c: (b, fwd(c), hh)),
        pl.BlockSpec(cq, lambda b, hh, c: (b, fwd(c), 0)),
        pl.BlockSpec(cq, lambda b, hh, c: (b, bwd(c), hh)),
        pl.BlockSpec(cq, lambda b, hh, c: (b, bwd(c), h + hh)),
        pl.BlockSpec(cv, lambda b, hh, c: (b, bwd(c), hh)),
        pl.BlockSpec(cq, lambda b, hh, c: (b, bwd(c), 0)),
        pl.BlockSpec((None, ctx_len, GLA_DK), lambda b, hh, c: (b, 0, hh)),
        pl.BlockSpec((None, ctx_len, GLA_DV), lambda b, hh, c: (b, 0, hh)),
        pl.BlockSpec((None, ctx_len, LANES), lambda b, hh, c: (b, 0, 0)),
        pl.BlockSpec((None, LANES, GLA_DK), lambda b, hh, c: (hh, 0, 0)),
        pl.BlockSpec((None, 1, GLA_DK), lambda b, hh, c: (hh, 0, 0)),
        pl.BlockSpec((None, LANES, GLA_DK), lambda b, hh, c: (hh, 0, 0)),
        pl.BlockSpec((None, 1, GLA_DK), lambda b, hh, c: (hh, 0, 0)),
    ]
    out_specs = [pl.BlockSpec(cv, lambda b, hh, c: (b, fwd(c), hh)),
                 pl.BlockSpec(cv, lambda b, hh, c: (b, bwd(c), hh))]
    out_shape = [jax.ShapeDtypeStruct((batch, t_len, GLA_V_WIDTH), BF16)] * 2
    return pl.pallas_call(
        _gla_kernel,
        out_shape=out_shape,
        grid=(batch, h, nc + 1),
        in_specs=in_specs,
        out_specs=out_specs,
        scratch_shapes=[pltpu.VMEM((GLA_DV, GLA_DK), F32), pltpu.VMEM((GLA_DV, GLA_DK), F32)],
        compiler_params=_params(3),
        name="gla",
    )(qk, qk, v3, lr3, qk, qk, v3, lr3, kc3, vc3, lrc3, wf, bf, wb, bb)


def _decay_weights(w_dec, b_dec, row0):
    w = w_dec.reshape(GLA_GATE_RANK, GLA_HEADS, GLA_DK).transpose(1, 0, 2)
    wp = jnp.zeros((GLA_HEADS, LANES, GLA_DK), F32).at[:, row0:row0 + GLA_GATE_RANK, :].set(w)
    return wp, b_dec.reshape(GLA_HEADS, 1, GLA_DK)


def _layer_norm(v, g, b):
    mu = jnp.mean(v, axis=1, keepdims=True)
    var = jnp.mean(jnp.square(v - mu), axis=1, keepdims=True)
    return (v - mu) * lax.rsqrt(var + EPS) * g + b


def _merge_kernel(oa_ref, of_ref, ob_ref, gb_ref, gates_ref, x_ref, g1_ref, sc2_ref, sh2_ref,
                  wa_ref, wb_ref, wo_ref, nw_ref, lg_ref, lb_ref, rw_ref,
                  x1_ref, h2_ref, sc_ref):
    d = x_ref.shape[1]
    o = of_ref[...].astype(F32) + ob_ref[...].astype(F32)
    pieces = []
    for hh in range(GLA_HEADS):
        oh = o[:, hh * GLA_DV:(hh + 1) * GLA_DV]
        pieces.append(oh * lax.rsqrt(jnp.mean(jnp.square(oh), axis=1, keepdims=True) + EPS))
    out_b = jnp.concatenate(pieces, axis=1) * nw_ref[...] * _silu(gb_ref[...].astype(F32))
    ya = jnp.dot(oa_ref[...], wa_ref[...], preferred_element_type=F32)
    yb = jnp.dot(out_b.astype(BF16), wb_ref[...], preferred_element_type=F32)
    y = _sigmoid(gates_ref[:, :d].astype(F32)) * ya + _sigmoid(gates_ref[:, d:].astype(F32)) * yb
    y2 = jnp.dot(y.astype(BF16), wo_ref[...], preferred_element_type=F32)
    x1 = _layer_norm(DEEPNORM_ALPHA * x_ref[...] + g1_ref[0] * y2, lg_ref[...], lb_ref[...])
    x1_ref[...] = x1
    h2 = x1 * (1.0 + sc2_ref[0]) + sh2_ref[0]
    h2_ref[...] = h2
    logits_t = lax.dot_general(rw_ref[...], h2.astype(BF16), (((1,), (1,)), ((), ())), preferred_element_type=F32)
    sc_ref[...] = _sigmoid(logits_t)


def _merge(out_a, o_f, o_b, gb, gates, x2d, g1, sc2, sh2, wa, wb, wo, nw, lg, lb, rw_t, tile, tiles_per_batch):
    n, d = x2d.shape
    row = lambda i: (i, 0)
    mod = lambda i: (i // tiles_per_batch, 0, 0)
    full = lambda i: (0, 0)

    def const(a):
        return pl.BlockSpec(a.shape, full, pipeline_mode=pl.Buffered(1))

    in_specs = [pl.BlockSpec((tile, NA_WIDTH), row), pl.BlockSpec((tile, GLA_V_WIDTH), row),
                pl.BlockSpec((tile, GLA_V_WIDTH), row), pl.BlockSpec((tile, GLA_V_WIDTH), row),
                pl.BlockSpec((tile, 2 * d), row), pl.BlockSpec((tile, d), row),
                pl.BlockSpec((1, 1, d), mod), pl.BlockSpec((1, 1, d), mod), pl.BlockSpec((1, 1, d), mod),
                const(wa), const(wb), const(wo), const(nw), const(lg), const(lb), const(rw_t)]
    out_shape = [jax.ShapeDtypeStruct((n, d), F32), jax.ShapeDtypeStruct((n, d), F32),
                 jax.ShapeDtypeStruct((N_EXPERTS, n), F32)]
    out_specs = [pl.BlockSpec((tile, d), row), pl.BlockSpec((tile, d), row),
                 pl.BlockSpec((N_EXPERTS, tile), lambda i: (0, i))]
    return pl.pallas_call(
        _merge_kernel, out_shape=out_shape, grid=(n // tile,), in_specs=in_specs, out_specs=out_specs,
        compiler_params=_params(1), name="merge_ln1_router",
    )(out_a, o_f, o_b, gb, gates, x2d, g1, sc2, sh2, wa, wb, wo, nw, lg, lb, rw_t)


def _first_argmax(vals, idx, n):
    m = jnp.max(vals, axis=0, keepdims=True)
    first = jnp.min(jnp.where(vals == m, idx, float(n)), axis=0, keepdims=True)
    return m, first


def _route_kernel(sc_ref, bias_ref, e_ref, w_ref, rank_ref, cnt_ref, carry_ref):
    step = pl.program_id(0)
    tr = sc_ref.shape[1]

    @pl.when(step == 0)
    def _():
        carry_ref[...] = jnp.zeros(carry_ref.shape, F32)

    scores = sc_ref[...]
    biased = scores + bias_ref[...]
    eidx = lax.broadcasted_iota(jnp.int32, (N_EXPERTS, tr), 0).astype(F32)
    lidx = lax.broadcasted_iota(jnp.int32, (GROUP_SIZE, tr), 0).astype(F32)
    gidx = lax.broadcasted_iota(jnp.int32, (N_GROUPS, tr), 0).astype(F32)
    gs = []
    for g in range(N_GROUPS):
        blk = biased[g * GROUP_SIZE:(g + 1) * GROUP_SIZE]
        m1, first = _first_argmax(blk, lidx, GROUP_SIZE)
        m2 = jnp.max(jnp.where(lidx == first, -jnp.inf, blk), axis=0, keepdims=True)
        gs.append(m1 + m2)
    cur = jnp.concatenate(gs, axis=0)
    keep = jnp.zeros((N_GROUPS, tr), F32)
    for _ in range(TOPK_GROUPS):
        _, first = _first_argmax(cur, gidx, N_GROUPS)
        sel = gidx == first
        keep = jnp.where(sel, 1.0, keep)
        cur = jnp.where(sel, -jnp.inf, cur)
    keep_e = jnp.concatenate([jnp.broadcast_to(keep[g:g + 1], (GROUP_SIZE, tr)) for g in range(N_GROUPS)], axis=0)
    masked = jnp.where(keep_e > 0.5, biased, -jnp.inf)
    chosen = jnp.zeros((N_EXPERTS, tr), F32)
    tops, topi = [], []
    for _ in range(TOP_K):
        _, first = _first_argmax(masked, eidx, N_EXPERTS)
        sel = eidx == first
        tops.append(jnp.sum(jnp.where(sel, scores, 0.0), axis=0, keepdims=True))
        topi.append(first)
        chosen = jnp.where(sel, 1.0, chosen)
        masked = jnp.where(sel, -jnp.inf, masked)
    top_s = jnp.concatenate(tops, axis=0)
    top_i = jnp.concatenate(topi, axis=0)
    e_ref[...] = top_i.astype(jnp.int32)
    w_ref[...] = top_s / jnp.sum(top_s, axis=0, keepdims=True) * ROUTED_SCALE
    r = lax.broadcasted_iota(jnp.int32, (tr, tr), 0)
    cidx = lax.broadcasted_iota(jnp.int32, (tr, tr), 1)
    before = jnp.where(r < cidx, 1.0, 0.0).astype(BF16)
    prior = jnp.dot(chosen.astype(BF16), before, preferred_element_type=F32) + carry_ref[...]
    ranks = [jnp.sum(jnp.where(eidx == topi[kk], prior, 0.0), axis=0, keepdims=True) for kk in range(TOP_K)]
    rank_ref[...] = jnp.concatenate(ranks, axis=0).astype(jnp.int32)
    carry_ref[...] = carry_ref[...] + jnp.sum(chosen, axis=1, keepdims=True)
    cnt_ref[...] = jnp.broadcast_to(carry_ref[...], cnt_ref.shape).astype(jnp.int32)


def _route(scores_t, router_bias, tile):
    n = scores_t.shape[1]
    col = lambda i: (0, i)
    out_shape = [jax.ShapeDtypeStruct((TOP_K, n), jnp.int32), jax.ShapeDtypeStruct((TOP_K, n), F32),
                 jax.ShapeDtypeStruct((TOP_K, n), jnp.int32), jax.ShapeDtypeStruct((N_EXPERTS, LANES), jnp.int32)]
    return pl.pallas_call(
        _route_kernel, out_shape=out_shape, grid=(n // tile,),
        in_specs=[pl.BlockSpec((N_EXPERTS, tile), col), pl.BlockSpec((N_EXPERTS, 1), lambda i: (0, 0))],
        out_specs=[pl.BlockSpec((TOP_K, tile), col), pl.BlockSpec((TOP_K, tile), col),
                   pl.BlockSpec((TOP_K, tile), col), pl.BlockSpec((N_EXPERTS, LANES), lambda i: (0, 0))],
        scratch_shapes=[pltpu.VMEM((N_EXPERTS, 1), F32)],
        compiler_params=_params(1), name="route",
    )(scores_t, router_bias.reshape(N_EXPERTS, 1))


def _row_copy(src, s, dst, t, sem):
    return pltpu.make_async_copy(src.at[pl.ds(s, 1)], dst.at[pl.ds(t, 1)], sem)


def _dispatch_kernel(tile, dest_ref, h_ref, xs_ref, sem):
    t0 = pl.program_id(0) * tile

    def issue(t, carry):
        for kk in range(TOP_K):
            _row_copy(h_ref, t0 + t, xs_ref, dest_ref[kk, t], sem).start()
        return carry

    lax.fori_loop(0, tile, issue, 0)

    def drain(t, carry):
        for kk in range(TOP_K):
            _row_copy(h_ref, t0 + t, xs_ref, dest_ref[kk, t], sem).wait()
        return carry

    lax.fori_loop(0, tile, drain, 0)


def _dispatch(dest, h2, n_slots, tile):
    n, d = h2.shape
    return pl.pallas_call(
        functools.partial(_dispatch_kernel, tile),
        out_shape=jax.ShapeDtypeStruct((n_slots, d), h2.dtype),
        grid=(n // tile,),
        in_specs=[pl.BlockSpec((TOP_K, tile), lambda i: (0, i), memory_space=pltpu.SMEM),
                  pl.BlockSpec(memory_space=pl.ANY)],
        out_specs=pl.BlockSpec(memory_space=pl.ANY),
        scratch_shapes=[pltpu.SemaphoreType.DMA],
        compiler_params=_params(1), name="dispatch",
    )(dest, h2)


def _expert_kernel(bexp_ref, bvalid_ref, nused_ref, xs_ref, wg_ref, wu_ref, wd_ref, y_ref, wgb, wub, wdb):
    i = pl.program_id(0)

    @pl.when(i < nused_ref[0])
    def _():
        prev = bexp_ref[jnp.maximum(i - 1, 0)]

        @pl.when((i == 0) | (bexp_ref[i] != prev))
        def _():
            wgb[...] = wg_ref[...].astype(BF16)
            wub[...] = wu_ref[...].astype(BF16)
            wdb[...] = wd_ref[...].astype(BF16)

        row = lax.broadcasted_iota(jnp.int32, xs_ref.shape, 0)
        x = jnp.where(row < bvalid_ref[i], xs_ref[...], 0.0).astype(BF16)
        g = jnp.dot(x, wgb[...], preferred_element_type=F32)
        u = jnp.dot(x, wub[...], preferred_element_type=F32)
        y_ref[...] = jnp.dot((_silu(g) * u).astype(BF16), wdb[...], preferred_element_type=F32)


def _experts(block_exp, block_valid, n_used, xs, wg, wu, wd):
    n_slots, d = xs.shape
    ff = wg.shape[2]
    n_blocks = n_slots // EXPERT_BLOCK

    def blk(i, be, bv, nu):
        return (jnp.minimum(i, nu[0] - 1), 0)

    def wsel(i, be, bv, nu):
        return (be[jnp.minimum(i, nu[0] - 1)], 0, 0)

    grid_spec = pltpu.PrefetchScalarGridSpec(
        num_scalar_prefetch=3, grid=(n_blocks,),
        in_specs=[pl.BlockSpec((EXPERT_BLOCK, d), blk),
                  pl.BlockSpec((None, d, ff), wsel), pl.BlockSpec((None, d, ff), wsel),
                  pl.BlockSpec((None, ff, d), wsel)],
        out_specs=pl.BlockSpec((EXPERT_BLOCK, d), blk),
        scratch_shapes=[pltpu.VMEM((d, ff), BF16), pltpu.VMEM((d, ff), BF16), pltpu.VMEM((ff, d), BF16)])
    return pl.pallas_call(
        _expert_kernel, out_shape=jax.ShapeDtypeStruct((n_slots, d), F32), grid_spec=grid_spec,
        compiler_params=_params(1), name="experts",
    )(block_exp, block_valid, n_used, xs, wg, wu, wd)


def _combine_kernel(tile, dest_ref, y_ref, w_ref, h_ref, x1_ref, g2_ref, sg_ref, su_ref, sd_ref, lg_ref, lb_ref,
                    o_ref, buf, sem):
    def issue(t, carry):
        for kk in range(TOP_K):
            _row_copy(y_ref, dest_ref[kk, t], buf.at[kk], t, sem).start()
        return carry

    lax.fori_loop(0, tile, issue, 0)
    hb = h_ref[...].astype(BF16)
    g = jnp.dot(hb, sg_ref[...], preferred_element_type=F32)
    u = jnp.dot(hb, su_ref[...], preferred_element_type=F32)
    f = jnp.dot((_silu(g) * u).astype(BF16), sd_ref[...], preferred_element_type=F32)

    def drain(t, carry):
        for kk in range(TOP_K):
            _row_copy(y_ref, dest_ref[kk, t], buf.at[kk], t, sem).wait()
        return carry

    lax.fori_loop(0, tile, drain, 0)
    w = w_ref[...]
    for kk in range(TOP_K):
        f = f + w[:, kk:kk + 1] * buf[kk]
    o_ref[...] = _layer_norm(DEEPNORM_ALPHA * x1_ref[...] + g2_ref[0] * f, lg_ref[...], lb_ref[...])


def _combine(dest, y, top_w, h2, x1, g2, sg, su, sd, lg, lb, tile, tiles_per_batch):
    n, d = h2.shape
    row = lambda i: (i, 0)
    full = lambda i: (0, 0)

    def const(a):
        return pl.BlockSpec(a.shape, full, pipeline_mode=pl.Buffered(1))

    return pl.pallas_call(
        functools.partial(_combine_kernel, tile),
        out_shape=jax.ShapeDtypeStruct((n, d), F32),
        grid=(n // tile,),
        in_specs=[pl.BlockSpec((TOP_K, tile), lambda i: (0, i), memory_space=pltpu.SMEM),
                  pl.BlockSpec(memory_space=pl.ANY),
                  pl.BlockSpec((tile, TOP_K), row), pl.BlockSpec((tile, d), row), pl.BlockSpec((tile, d), row),
                  pl.BlockSpec((1, 1, d), lambda i: (i // tiles_per_batch, 0, 0)),
                  const(sg), const(su), const(sd), const(lg), const(lb)],
        out_specs=pl.BlockSpec((tile, d), row),
        scratch_shapes=[pltpu.VMEM((TOP_K, tile, d), F32), pltpu.SemaphoreType.DMA],
        compiler_params=_params(1), name="combine_shared_ln2",
    )(dest, y, top_w, h2, x1, g2, sg, su, sd, lg, lb)


def kernel(x, c, ctx, c_ctx, w_mod, b_mod, w_in, na_rpb, gla_w_decay_f, gla_b_decay_f, gla_w_decay_b, gla_b_decay_b,
           gla_norm_w, w_branch_a, w_branch_b, w_out, ln1_g, ln1_b, router_w, router_bias, exp_w_gate, exp_w_up,
           exp_w_down, sh_w_gate, sh_w_up, sh_w_down, ln2_g, ln2_b):
    batch, t_len, d = x.shape
    ctx_len = ctx.shape[1]
    n = batch * t_len
    assert w_mod.shape[0] == DEPTH == 1
    assert t_len % GLA_CHUNK == 0 and ctx_len == GLA_CHUNK and (t_len // GRID_W) % NA_ROW_BLOCK == 0

    mod_rows = 16
    c_all = jnp.zeros((mod_rows, d), F32).at[:batch].set(c).at[batch].set(c_ctx)
    mod = _modulation(c_all, w_mod[0], b_mod[0])
    sh1, sc1, g1, sh2, sc2, g2 = [mod[:batch, j * d:(j + 1) * d].reshape(batch, 1, d) for j in range(6)]
    sh1c = mod[batch:batch + 1, 0:d].reshape(1, 1, d)
    sc1c = mod[batch:batch + 1, d:2 * d].reshape(1, 1, d)

    offs = np.cumsum((0, NA_WIDTH, NA_WIDTH, NA_WIDTH, GLA_QK_WIDTH, GLA_QK_WIDTH, GLA_V_WIDTH, GLA_V_WIDTH,
                      GLA_GATE_RANK, GLA_GATE_RANK, d, d))
    qa, ka, va, qb, kb, vbc, gbc, lrf, lrb, ga, gbt = [w_in[0][:, offs[j]:offs[j + 1]] for j in range(11)]
    lr_cols = jnp.concatenate([lrf, lrb, jnp.zeros((d, LANES - 2 * GLA_GATE_RANK), F32)], axis=1)
    w_lat = jnp.concatenate([qa, ka, va, qb, kb, vbc, gbc, ga, gbt, lr_cols], axis=1).astype(BF16)
    w_ctx = jnp.concatenate([ka, va, kb, vbc, lr_cols], axis=1).astype(BF16)
    plain = ("plain",)
    lat_plan = ((3 * NA_WIDTH, (("scale", NA_HEAD_DIM ** -0.5), plain, plain)),
                (2 * GLA_QK_WIDTH, (("rope", GLA_DK ** -0.5), ("rope", 1.0))),
                (GLA_V_WIDTH, (plain, plain)), (GLA_V_WIDTH, (plain, plain)),
                (2 * d, (plain,) * 4), (LANES, (plain,)))
    ctx_plan = ((2 * NA_WIDTH, (plain, plain)), (GLA_QK_WIDTH, (plain,)), (GLA_V_WIDTH, (plain, plain)),
                (LANES, (plain,)))
    tile = 256
    x2d = x.reshape(n, d)
    na_qkv, gla_qk, vb, gb, gates, lr = _projection(
        x2d, sc1, sh1, w_lat, lat_plan, (BF16, BF16, BF16, BF16, BF16, F32), tile, t_len // tile,
        rope=_rope_tables(t_len))
    na_kv_c, k_c, v_c, lr_c = _projection(
        ctx.reshape(batch * ctx_len, d), sc1c, sh1c, w_ctx, ctx_plan, (BF16, BF16, BF16, F32), tile,
        batch * ctx_len // tile)

    out_a = _na_attention(na_qkv, na_kv_c, _na_bias_tables(na_rpb[0]), batch, t_len, ctx_len)

    wf, bf = _decay_weights(gla_w_decay_f[0], gla_b_decay_f[0], 0)
    wb, bb = _decay_weights(gla_w_decay_b[0], gla_b_decay_b[0], GLA_GATE_RANK)
    o_f, o_b = _gla(gla_qk, vb, lr, k_c, v_c, lr_c, wf, bf, wb, bb, batch, t_len, ctx_len)

    x1, h2, scores_t = _merge(
        out_a.reshape(n, NA_WIDTH), o_f.reshape(n, GLA_V_WIDTH), o_b.reshape(n, GLA_V_WIDTH), gb, gates, x2d,
        g1, sc2, sh2, w_branch_a[0].astype(BF16), w_branch_b[0].astype(BF16), w_out[0].astype(BF16),
        gla_norm_w[0].reshape(1, -1), ln1_g[0].reshape(1, d), ln1_b[0].reshape(1, d),
        router_w[0].T.astype(BF16), tile, t_len // tile)

    top_e, top_w, rank, counts = _route(scores_t, router_bias[0], 512)

    counts = counts[:, 0]
    n_blocks = n * TOP_K // EXPERT_BLOCK + N_EXPERTS
    blocks_per = (counts + EXPERT_BLOCK - 1) // EXPERT_BLOCK
    blk_end = jnp.cumsum(blocks_per)
    blk_start = blk_end - blocks_per
    dest = blk_start[top_e] * EXPERT_BLOCK + rank
    bidx = jnp.arange(n_blocks, dtype=jnp.int32)
    block_exp = jnp.clip(jnp.searchsorted(blk_end, bidx, side="right"), 0, N_EXPERTS - 1).astype(jnp.int32)
    block_valid = jnp.clip(counts[block_exp] - (bidx - blk_start[block_exp]) * EXPERT_BLOCK, 0, EXPERT_BLOCK)
    n_used = blk_end[-1:].astype(jnp.int32)

    xs = _dispatch(dest, h2, n_blocks * EXPERT_BLOCK, tile)
    y = _experts(block_exp, block_valid.astype(jnp.int32), n_used, xs, exp_w_gate[0], exp_w_up[0], exp_w_down[0])
    out = _combine(dest, y, top_w.T, h2, x1, g2, sh_w_gate[0].astype(BF16), sh_w_up[0].astype(BF16),
                   sh_w_down[0].astype(BF16), ln2_g[0].reshape(1, d), ln2_b[0].reshape(1, d), tile, t_len // tile)
    return out.reshape(batch, t_len, d)
```

```python
import functools

import numpy as np
import jax
import jax.numpy as jnp
from jax import lax
from jax.experimental import pallas as pl
from jax.experimental.pallas import tpu as pltpu

F32 = jnp.float32
BF16 = jnp.bfloat16
HIGHEST = lax.Precision.HIGHEST

GRID_W = 64
NA_HEADS = 8
NA_HEAD_DIM = 64
NA_WIDTH = NA_HEADS * NA_HEAD_DIM
WIN_ROWS = 8
WIN_COLS = 16
GLA_HEADS = 4
GLA_DK = 128
GLA_DV = 256
GLA_QK_WIDTH = GLA_HEADS * GLA_DK
GLA_V_WIDTH = GLA_HEADS * GLA_DV
GLA_GATE_RANK = 16
GLA_TAU = 16.0
ROPE_BASE = 10000.0
N_EXPERTS = 256
TOP_K = 8
N_GROUPS = 8
TOPK_GROUPS = 4
GROUP_SIZE = N_EXPERTS // N_GROUPS
ROUTED_SCALE = 2.5
DEPTH = 1
DEEPNORM_ALPHA = (2 * DEPTH) ** 0.25
EPS = 1e-6

LANES = 128
NA_ROW_BLOCK = 4
NA_UNION_ROWS = NA_ROW_BLOCK + WIN_ROWS - 1
GLA_CHUNK = 256
GLA_DIAG = 16
EXPERT_BLOCK = 256
NEG_BIG = -1e30
VMEM_LIMIT = 56 * 1024 * 1024


def _params(n_axes, vmem=VMEM_LIMIT):
    return pltpu.CompilerParams(dimension_semantics=("arbitrary",) * n_axes, vmem_limit_bytes=vmem)


def _sigmoid(v):
    return 1.0 / (1.0 + jnp.exp(-v))


def _silu(v):
    return v * _sigmoid(v)


def _mod_kernel(c_ref, w_ref, b_ref, o_ref):
    o_ref[...] = jnp.dot(_silu(c_ref[...]), w_ref[...], preferred_element_type=F32, precision=HIGHEST) + b_ref[...]


def _modulation(c_all, w_mod, b_mod):
    rows, d = c_all.shape
    n = w_mod.shape[1]
    bn = 512
    return pl.pallas_call(
        _mod_kernel,
        out_shape=jax.ShapeDtypeStruct((rows, n), F32),
        grid=(n // bn,),
        in_specs=[pl.BlockSpec((rows, d), lambda j: (0, 0)),
                  pl.BlockSpec((d, bn), lambda j: (0, j)),
                  pl.BlockSpec((1, bn), lambda j: (0, j))],
        out_specs=pl.BlockSpec((rows, bn), lambda j: (0, j)),
        compiler_params=_params(1),
        name="modulation",
    )(c_all, w_mod, b_mod.reshape(1, n))


def _swap32(v):
    lane = lax.broadcasted_iota(jnp.int32, v.shape, 1)
    return jnp.where((lane % 64) < 32, pltpu.roll(v, 96, 1), pltpu.roll(v, 32, 1))


def _proj_kernel(plan, has_rope, *refs):
    x_ref, sc_ref, sh_ref, w_ref = refs[:4]
    pos = 4
    if has_rope:
        cos_ref, sin_ref = refs[4:6]
        pos = 6
    out_refs = refs[pos:]
    h = (x_ref[...] * (1.0 + sc_ref[0]) + sh_ref[0]).astype(BF16)
    col = 0
    for out_ref, (width, kinds) in zip(out_refs, plan):
        for j, kind in enumerate(kinds):
            cw = width // len(kinds)
            c0 = j * cw
            acc = jnp.dot(h, w_ref[:, col + c0:col + c0 + cw], preferred_element_type=F32)
            if kind[0] == "scale":
                acc = acc * kind[1]
            elif kind[0] == "rope":
                cos, sin = cos_ref[...], sin_ref[...]
                pieces = []
                for p in range(cw // LANES):
                    v = acc[:, p * LANES:(p + 1) * LANES]
                    pieces.append((v * cos + _swap32(v) * sin) * kind[1])
                acc = jnp.concatenate(pieces, axis=1)
            out_ref[:, c0:c0 + cw] = acc.astype(out_ref.dtype)
        col += width


def _projection(x2d, sc, sh, w, plan, out_dtypes, tile, tiles_per_mod, rope=None):
    n, d = x2d.shape
    in_specs = [pl.BlockSpec((tile, d), lambda i: (i, 0)),
                pl.BlockSpec((1, 1, d), lambda i: (i // tiles_per_mod, 0, 0)),
                pl.BlockSpec((1, 1, d), lambda i: (i // tiles_per_mod, 0, 0)),
                pl.BlockSpec(w.shape, lambda i: (0, 0), pipeline_mode=pl.Buffered(1))]
    args = [x2d, sc, sh, w]
    if rope is not None:
        in_specs += [pl.BlockSpec((tile, LANES), lambda i: (i % tiles_per_mod, 0))] * 2
        args += list(rope)
    out_shape = [jax.ShapeDtypeStruct((n, wd), dt) for (wd, _), dt in zip(plan, out_dtypes)]
    out_specs = [pl.BlockSpec((tile, wd), lambda i: (i, 0)) for (wd, _) in plan]
    return pl.pallas_call(
        functools.partial(_proj_kernel, plan, rope is not None),
        out_shape=out_shape,
        grid=(n // tile,),
        in_specs=in_specs,
        out_specs=out_specs,
        compiler_params=_params(1),
        name="in_proj" if rope is not None else "ctx_proj",
    )(*args)


def _rope_tables(t_len):
    half = GLA_DK // 2
    quarter = half // 2
    inv_freq = ROPE_BASE ** (-jnp.arange(quarter, dtype=F32) / quarter)
    pos = jnp.arange(t_len)
    row_ang = (pos // GRID_W).astype(F32)[:, None] * inv_freq[None, :]
    col_ang = (pos % GRID_W).astype(F32)[:, None] * inv_freq[None, :]
    cr, sr, cc, sn = jnp.cos(row_ang), jnp.sin(row_ang), jnp.cos(col_ang), jnp.sin(col_ang)
    cos = jnp.concatenate([cr, cr, cc, cc], axis=1)
    sin = jnp.concatenate([-sr, sr, -sn, sn], axis=1)
    return cos, sin


def _na_bias_tables(rpb):
    rb, ur, w = NA_ROW_BLOCK, NA_UNION_ROWS, GRID_W
    heads = rpb.shape[0]
    pad = jnp.pad(rpb, ((0, 0), (0, 0), (w, w)))
    toep = jnp.stack([pad[:, :, w + WIN_COLS - 1 - c:2 * w + WIN_COLS - 1 - c] for c in range(w)], axis=2)
    c = np.arange(w)[:, None]
    kc = np.arange(w)[None, :]
    col_start = np.clip(c - WIN_COLS // 2, 0, w - WIN_COLS)
    col_ok = (kc >= col_start) & (kc < col_start + WIN_COLS)
    toep = jnp.where(col_ok[None, None], toep, NEG_BIG)
    neg = jnp.full((heads, w, w), NEG_BIG, F32)
    half = WIN_ROWS // 2
    tables = []
    for lo, off in ((lambda i: 0, WIN_ROWS - 1), (lambda i: i, WIN_ROWS - 1 - half), (lambda i: ur - WIN_ROWS, 0)):
        rows_ = []
        for i in range(rb):
            blocks = [toep[:, j - i + off] if lo(i) <= j < lo(i) + WIN_ROWS else neg for j in range(ur)]
            rows_.append(jnp.concatenate(blocks, axis=2))
        tables.append(jnp.concatenate(rows_, axis=1))
    return jnp.stack(tables)


def _na_kernel(rows, q_ref, k_ref, v_ref, kc_ref, vc_ref, bias_ref, o_ref):
    rb = pl.program_id(1)
    ustart = jnp.clip(rb * NA_ROW_BLOCK - WIN_ROWS // 2, 0, rows - NA_UNION_ROWS)
    k0 = pl.multiple_of(ustart * GRID_W, GRID_W)
    nk = NA_UNION_ROWS * GRID_W
    nt = (((1,), (1,)), ((), ()))
    lane = lax.broadcasted_iota(jnp.int32, (NA_ROW_BLOCK * GRID_W, LANES), 1)
    for p in range(NA_WIDTH // LANES):
        ls = slice(p * LANES, (p + 1) * LANES)
        q = q_ref[:, ls]
        ku = k_ref[pl.ds(k0, nk), ls]
        vu = v_ref[pl.ds(k0, nk), ls]
        kc = kc_ref[:, ls]
        vc = vc_ref[:, ls]
        out = jnp.zeros(q.shape, F32)
        for hh in range(LANES // NA_HEAD_DIM):
            mine = (lane // NA_HEAD_DIM) == hh
            qm = jnp.where(mine, q, jnp.zeros_like(q))
            s_loc = lax.dot_general(qm, ku, nt, preferred_element_type=F32) + bias_ref[p * 2 + hh]
            s_ctx = lax.dot_general(qm, kc, nt, preferred_element_type=F32)
            m = jnp.maximum(jnp.max(s_loc, axis=1, keepdims=True), jnp.max(s_ctx, axis=1, keepdims=True))
            p_loc = jnp.exp(s_loc - m)
            p_ctx = jnp.exp(s_ctx - m)
            denom = jnp.sum(p_loc, axis=1, keepdims=True) + jnp.sum(p_ctx, axis=1, keepdims=True)
            o = (jnp.dot(p_loc.astype(BF16), vu, preferred_element_type=F32)
                 + jnp.dot(p_ctx.astype(BF16), vc, preferred_element_type=F32)) / denom
            out = jnp.where(mine, o, out)
        o_ref[:, ls] = out.astype(o_ref.dtype)


def _na_attention(na_qkv, na_kv_ctx, bias_tables, batch, t_len, ctx_len):
    rows = t_len // GRID_W
    n_rb = rows // NA_ROW_BLOCK
    tq = NA_ROW_BLOCK * GRID_W
    qkv = na_qkv.reshape(batch, t_len, 3 * NA_WIDTH)
    kvc = na_kv_ctx.reshape(batch, ctx_len, 2 * NA_WIDTH)

    def bias_idx(b, r):
        return (jnp.where(r == 0, 0, jnp.where(r == n_rb - 1, 2, 1)), 0, 0, 0)

    return pl.pallas_call(
        functools.partial(_na_kernel, rows),
        out_shape=jax.ShapeDtypeStruct((batch, t_len, NA_WIDTH), BF16),
        grid=(batch, n_rb),
        in_specs=[pl.BlockSpec((None, tq, NA_WIDTH), lambda b, r: (b, r, 0)),
                  pl.BlockSpec((None, t_len, NA_WIDTH), lambda b, r: (b, 0, 1)),
                  pl.BlockSpec((None, t_len, NA_WIDTH), lambda b, r: (b, 0, 2)),
                  pl.BlockSpec((None, ctx_len, NA_WIDTH), lambda b, r: (b, 0, 0)),
                  pl.BlockSpec((None, ctx_len, NA_WIDTH), lambda b, r: (b, 0, 1)),
                  pl.BlockSpec((None,) + bias_tables.shape[1:], bias_idx)],
        out_specs=pl.BlockSpec((None, tq, NA_WIDTH), lambda b, r: (b, r, 0)),
        compiler_params=_params(2),
        name="na_attention",
    )(qkv, qkv, qkv, kvc, kvc, bias_tables)


def _log_decay(lr, w2, b2):
    z = jnp.dot(lr, w2, preferred_element_type=F32) + b2
    return (jnp.minimum(z, 0.0) - jnp.log(1.0 + jnp.exp(-jnp.abs(z)))) * (1.0 / GLA_TAU)


def _gla_masks(c, reverse):
    i = lax.broadcasted_iota(jnp.int32, (c, c), 0)
    j = lax.broadcasted_iota(jnp.int32, (c, c), 1)
    if reverse:
        i, j = j, i
    masks = []
    s = c // 2
    while s >= GLA_DIAG:
        masks.append(((i // (2 * s)) == (j // (2 * s))) & ((i % (2 * s)) >= s) & ((j % (2 * s)) < s))
        s //= 2
    masks.append(((i // GLA_DIAG) == (j // GLA_DIAG)) & (j <= i))
    return masks


def _block_refs(cum, s, reverse, diag):
    c = cum.shape[0]
    span = s if diag else 2 * s
    parts = []
    for p in range(c // span):
        if diag:
            r = p * span + (span - 1 if reverse else 0)
        else:
            r = p * span + (s - 1 if reverse else s)
        parts.append(jnp.broadcast_to(cum[r:r + 1, :], (span, cum.shape[1])))
    return jnp.concatenate(parts, axis=0)


def _cumsum_rows(a, reverse):
    c = a.shape[0]
    i = lax.broadcasted_iota(jnp.int32, (c, c), 0)
    j = lax.broadcasted_iota(jnp.int32, (c, c), 1)
    tri = jnp.where((j >= i) if reverse else (j <= i), 1.0, 0.0).astype(F32)
    return jnp.dot(tri, a, preferred_element_type=F32, precision=HIGHEST)


def _gla_chunk(q, k, v, a, state_t, reverse, want_out):
    c = k.shape[0]
    nt = (((1,), (1,)), ((), ()))
    cum = _cumsum_rows(a, reverse)
    last = 0 if reverse else c - 1
    total = cum[last:last + 1, :]
    out = None
    if want_out:
        q_in = (q * jnp.exp(cum)).astype(BF16)
        out = lax.dot_general(q_in, state_t.astype(BF16), nt, preferred_element_type=F32)
        masks = _gla_masks(c, reverse)
        scores = jnp.zeros((c, c), F32)
        s = c // 2
        level = 0
        while True:
            diag = s < GLA_DIAG
            d = cum - _block_refs(cum, GLA_DIAG if diag else s, reverse, diag)
            if diag:
                qs = q * jnp.exp(d)
                ks = k * jnp.exp(-d)
            else:
                qs = q * jnp.exp(jnp.minimum(d, 0.0))
                ks = k * jnp.exp(jnp.minimum(-d, 0.0))
            att = lax.dot_general(qs.astype(BF16), ks.astype(BF16), nt, preferred_element_type=F32)
            scores = scores + jnp.where(masks[level], att, 0.0)
            if diag:
                break
            s //= 2
            level += 1
        out = out + jnp.dot(scores.astype(BF16), v, preferred_element_type=F32)
    k_out = (k * jnp.exp(total - cum)).astype(BF16)
    upd = lax.dot_general(v, k_out, (((0,), (0,)), ((), ())), preferred_element_type=F32)
    return out, jnp.exp(total) * state_t + upd


def _gla_kernel(qf_ref, kf_ref, vf_ref, lrf_ref, qb_ref, kb_ref, vb_ref, lrb_ref,
                kc_ref, vc_ref, lrc_ref, wf_ref, bf_ref, wb_ref, bb_ref,
                of_ref, ob_ref, sf_ref, sb_ref):
    c = pl.program_id(2)

    @pl.when(c == 0)
    def _():
        kc = kc_ref[...].astype(F32)
        vc = vc_ref[...]
        lrc = lrc_ref[...]
        zero = jnp.zeros(sf_ref.shape, F32)
        _, sf_ref[...] = _gla_chunk(None, kc, vc, _log_decay(lrc, wf_ref[...], bf_ref[...]), zero, False, False)
        _, sb_ref[...] = _gla_chunk(None, kc, vc, _log_decay(lrc, wb_ref[...], bb_ref[...]), zero, True, False)

    @pl.when(c > 0)
    def _():
        o, sf_ref[...] = _gla_chunk(qf_ref[...].astype(F32), kf_ref[...].astype(F32), vf_ref[...],
                                    _log_decay(lrf_ref[...], wf_ref[...], bf_ref[...]), sf_ref[...], False, True)
        of_ref[...] = o.astype(of_ref.dtype)
        o, sb_ref[...] = _gla_chunk(qb_ref[...].astype(F32), kb_ref[...].astype(F32), vb_ref[...],
                                    _log_decay(lrb_ref[...], wb_ref[...], bb_ref[...]), sb_ref[...], True, True)
        ob_ref[...] = o.astype(ob_ref.dtype)


def _gla(gla_qk, vb, lr, k_ctx, v_ctx, lr_ctx, wf, bf, wb, bb, batch, t_len, ctx_len):
    nc = t_len // GLA_CHUNK
    h = GLA_HEADS
    qk = gla_qk.reshape(batch, t_len, 2 * GLA_QK_WIDTH)
    v3 = vb.reshape(batch, t_len, GLA_V_WIDTH)
    lr3 = lr.reshape(batch, t_len, LANES)
    kc3 = k_ctx.reshape(batch, ctx_len, GLA_QK_WIDTH)
    vc3 = v_ctx.reshape(batch, ctx_len, GLA_V_WIDTH)
    lrc3 = lr_ctx.reshape(batch, ctx_len, LANES)

    def fwd(c):
        return jnp.maximum(c - 1, 0)

    def bwd(c):
        return nc - 1 - jnp.maximum(c - 1, 0)

    cq = (None, GLA_CHUNK, GLA_DK)
    cv = (None, GLA_CHUNK, GLA_DV)
    in_specs = [
        pl.BlockSpec(cq, lambda b, hh, c: (b, fwd(c), hh)),
        pl.BlockSpec(cq, lambda b, hh, c: (b, fwd(c), h + hh)),
        pl.BlockSpec(cv, lambda b, hh, c: (b, fwd(c), hh)),
        pl.BlockSpec(cq, lambda b, hh, c: (b, fwd(c), 0)),
        pl.BlockSpec(cq, lambda b, hh, c: (b, bwd(c), hh)),
        pl.BlockSpec(cq, lambda b, hh, c: (b, bwd(c), h + hh)),
        pl.BlockSpec(cv, lambda b, hh, c: (b, bwd(c), hh)),
        pl.BlockSpec(cq, lambda b, hh, c: (b, bwd(c), 0)),
        pl.BlockSpec((None, ctx_len, GLA_DK), lambda b, hh, c: (b, 0, hh)),
        pl.BlockSpec((None, ctx_len, GLA_DV), lambda b, hh, c: (b, 0, hh)),
        pl.BlockSpec((None, ctx_len, LANES), lambda b, hh, c: (b, 0, 0)),
        pl.BlockSpec((None, LANES, GLA_DK), lambda b, hh, c: (hh, 0, 0)),
        pl.BlockSpec((None, 1, GLA_DK), lambda b, hh, c: (hh, 0, 0)),
        pl.BlockSpec((None, LANES, GLA_DK), lambda b, hh, c: (hh, 0, 0)),
        pl.BlockSpec((None, 1, GLA_DK), lambda b, hh, c: (hh, 0, 0)),
    ]
    out_specs = [pl.BlockSpec(cv, lambda b, hh, c: (b, fwd(c), hh)),
                 pl.BlockSpec(cv, lambda b, hh, c: (b, bwd(c), hh))]
    out_shape = [jax.ShapeDtypeStruct((batch, t_len, GLA_V_WIDTH), BF16)] * 2
    return pl.pallas_call(
        _gla_kernel,
        out_shape=out_shape,
        grid=(batch, h, nc + 1),
        in_specs=in_specs,
        out_specs=out_specs,
        scratch_shapes=[pltpu.VMEM((GLA_DV, GLA_DK), F32), pltpu.VMEM((GLA_DV, GLA_DK), F32)],
        compiler_params=_params(3),
        name="gla",
    )(qk, qk, v3, lr3, qk, qk, v3, lr3, kc3, vc3, lrc3, wf, bf, wb, bb)


def _decay_weights(w_dec, b_dec, row0):
    w = w_dec.reshape(GLA_GATE_RANK, GLA_HEADS, GLA_DK).transpose(1, 0, 2)
    wp = jnp.zeros((GLA_HEADS, LANES, GLA_DK), F32).at[:, row0:row0 + GLA_GATE_RANK, :].set(w)
    return wp, b_dec.reshape(GLA_HEADS, 1, GLA_DK)


def _layer_norm(v, g, b):
    mu = jnp.mean(v, axis=1, keepdims=True)
    var = jnp.mean(jnp.square(v - mu), axis=1, keepdims=True)
    return (v - mu) * lax.rsqrt(var + EPS) * g + b


def _merge_kernel(oa_ref, of_ref, ob_ref, gb_ref, gates_ref, x_ref, g1_ref, sc2_ref, sh2_ref,
                  wa_ref, wb_ref, wo_ref, nw_ref, lg_ref, lb_ref, rw_ref,
                  x1_ref, h2_ref, sc_ref):
    d = x_ref.shape[1]
    o = of_ref[...].astype(F32) + ob_ref[...].astype(F32)
    pieces = []
    for hh in range(GLA_HEADS):
        oh = o[:, hh * GLA_DV:(hh + 1) * GLA_DV]
        pieces.append(oh * lax.rsqrt(jnp.mean(jnp.square(oh), axis=1, keepdims=True) + EPS))
    out_b = jnp.concatenate(pieces, axis=1) * nw_ref[...] * _silu(gb_ref[...].astype(F32))
    ya = jnp.dot(oa_ref[...], wa_ref[...], preferred_element_type=F32)
    yb = jnp.dot(out_b.astype(BF16), wb_ref[...], preferred_element_type=F32)
    y = _sigmoid(gates_ref[:, :d].astype(F32)) * ya + _sigmoid(gates_ref[:, d:].astype(F32)) * yb
    y2 = jnp.dot(y.astype(BF16), wo_ref[...], preferred_element_type=F32)
    x1 = _layer_norm(DEEPNORM_ALPHA * x_ref[...] + g1_ref[0] * y2, lg_ref[...], lb_ref[...])
    x1_ref[...] = x1
    h2 = x1 * (1.0 + sc2_ref[0]) + sh2_ref[0]
    h2_ref[...] = h2
    logits_t = lax.dot_general(rw_ref[...], h2.astype(BF16), (((1,), (1,)), ((), ())), preferred_element_type=F32)
    sc_ref[...] = _sigmoid(logits_t)


def _merge(out_a, o_f, o_b, gb, gates, x2d, g1, sc2, sh2, wa, wb, wo, nw, lg, lb, rw_t, tile, tiles_per_batch):
    n, d = x2d.shape
    row = lambda i: (i, 0)
    mod = lambda i: (i // tiles_per_batch, 0, 0)
    full = lambda i: (0, 0)

    def const(a):
        return pl.BlockSpec(a.shape, full, pipeline_mode=pl.Buffered(1))

    in_specs = [pl.BlockSpec((tile, NA_WIDTH), row), pl.BlockSpec((tile, GLA_V_WIDTH), row),
                pl.BlockSpec((tile, GLA_V_WIDTH), row), pl.BlockSpec((tile, GLA_V_WIDTH), row),
                pl.BlockSpec((tile, 2 * d), row), pl.BlockSpec((tile, d), row),
                pl.BlockSpec((1, 1, d), mod), pl.BlockSpec((1, 1, d), mod), pl.BlockSpec((1, 1, d), mod),
                const(wa), const(wb), const(wo), const(nw), const(lg), const(lb), const(rw_t)]
    out_shape = [jax.ShapeDtypeStruct((n, d), F32), jax.ShapeDtypeStruct((n, d), F32),
                 jax.ShapeDtypeStruct((N_EXPERTS, n), F32)]
    out_specs = [pl.BlockSpec((tile, d), row), pl.BlockSpec((tile, d), row),
                 pl.BlockSpec((N_EXPERTS, tile), lambda i: (0, i))]
    return pl.pallas_call(
        _merge_kernel, out_shape=out_shape, grid=(n // tile,), in_specs=in_specs, out_specs=out_specs,
        compiler_params=_params(1), name="merge_ln1_router",
    )(out_a, o_f, o_b, gb, gates, x2d, g1, sc2, sh2, wa, wb, wo, nw, lg, lb, rw_t)


def _first_argmax(vals, idx, n):
    m = jnp.max(vals, axis=0, keepdims=True)
    first = jnp.min(jnp.where(vals == m, idx, float(n)), axis=0, keepdims=True)
    return m, first


def _route_kernel(sc_ref, bias_ref, e_ref, w_ref, rank_ref, cnt_ref, carry_ref):
    step = pl.program_id(0)
    tr = sc_ref.shape[1]

    @pl.when(step == 0)
    def _():
        carry_ref[...] = jnp.zeros(carry_ref.shape, F32)

    scores = sc_ref[...]
    biased = scores + bias_ref[...]
    eidx = lax.broadcasted_iota(jnp.int32, (N_EXPERTS, tr), 0).astype(F32)
    lidx = lax.broadcasted_iota(jnp.int32, (GROUP_SIZE, tr), 0).astype(F32)
    gidx = lax.broadcasted_iota(jnp.int32, (N_GROUPS, tr), 0).astype(F32)
    gs = []
    for g in range(N_GROUPS):
        blk = biased[g * GROUP_SIZE:(g + 1) * GROUP_SIZE]
        m1, first = _first_argmax(blk, lidx, GROUP_SIZE)
        m2 = jnp.max(jnp.where(lidx == first, -jnp.inf, blk), axis=0, keepdims=True)
        gs.append(m1 + m2)
    cur = jnp.concatenate(gs, axis=0)
    keep = jnp.zeros((N_GROUPS, tr), F32)
    for _ in range(TOPK_GROUPS):
        _, first = _first_argmax(cur, gidx, N_GROUPS)
        sel = gidx == first
        keep = jnp.where(sel, 1.0, keep)
        cur = jnp.where(sel, -jnp.inf, cur)
    keep_e = jnp.concatenate([jnp.broadcast_to(keep[g:g + 1], (GROUP_SIZE, tr)) for g in range(N_GROUPS)], axis=0)
    masked = jnp.where(keep_e > 0.5, biased, -jnp.inf)
    chosen = jnp.zeros((N_EXPERTS, tr), F32)
    tops, topi = [], []
    for _ in range(TOP_K):
        _, first = _first_argmax(masked, eidx, N_EXPERTS)
        sel = eidx == first
        tops.append(jnp.sum(jnp.where(sel, scores, 0.0), axis=0, keepdims=True))
        topi.append(first)
        chosen = jnp.where(sel, 1.0, chosen)
        masked = jnp.where(sel, -jnp.inf, masked)
    top_s = jnp.concatenate(tops, axis=0)
    top_i = jnp.concatenate(topi, axis=0)
    e_ref[...] = top_i.astype(jnp.int32)
    w_ref[...] = top_s / jnp.sum(top_s, axis=0, keepdims=True) * ROUTED_SCALE
    r = lax.broadcasted_iota(jnp.int32, (tr, tr), 0)
    cidx = lax.broadcasted_iota(jnp.int32, (tr, tr), 1)
    before = jnp.where(r < cidx, 1.0, 0.0).astype(BF16)
    prior = jnp.dot(chosen.astype(BF16), before, preferred_element_type=F32) + carry_ref[...]
    ranks = [jnp.sum(jnp.where(eidx == topi[kk], prior, 0.0), axis=0, keepdims=True) for kk in range(TOP_K)]
    rank_ref[...] = jnp.concatenate(ranks, axis=0).astype(jnp.int32)
    carry_ref[...] = carry_ref[...] + jnp.sum(chosen, axis=1, keepdims=True)
    cnt_ref[...] = jnp.broadcast_to(carry_ref[...], cnt_ref.shape).astype(jnp.int32)


def _route(scores_t, router_bias, tile):
    n = scores_t.shape[1]
    col = lambda i: (0, i)
    out_shape = [jax.ShapeDtypeStruct((TOP_K, n), jnp.int32), jax.ShapeDtypeStruct((TOP_K, n), F32),
                 jax.ShapeDtypeStruct((TOP_K, n), jnp.int32), jax.ShapeDtypeStruct((N_EXPERTS, LANES), jnp.int32)]
    return pl.pallas_call(
        _route_kernel, out_shape=out_shape, grid=(n // tile,),
        in_specs=[pl.BlockSpec((N_EXPERTS, tile), col), pl.BlockSpec((N_EXPERTS, 1), lambda i: (0, 0))],
        out_specs=[pl.BlockSpec((TOP_K, tile), col), pl.BlockSpec((TOP_K, tile), col),
                   pl.BlockSpec((TOP_K, tile), col), pl.BlockSpec((N_EXPERTS, LANES), lambda i: (0, 0))],
        scratch_shapes=[pltpu.VMEM((N_EXPERTS, 1), F32)],
        compiler_params=_params(1), name="route",
    )(scores_t, router_bias.reshape(N_EXPERTS, 1))


def _row_copy(src, s, dst, t, sem):
    return pltpu.make_async_copy(src.at[pl.ds(s, 1)], dst.at[pl.ds(t, 1)], sem)


def _dispatch_kernel(tile, dest_ref, h_ref, xs_ref, sem):
    def issue(t, carry):
        for kk in range(TOP_K):
            _row_copy(h_ref, t, xs_ref, dest_ref[kk, t], sem).start(priority=kk % 2)
        return carry

    lax.fori_loop(0, tile, issue, 0)

    def drain(t, carry):
        for kk in range(TOP_K):
            _row_copy(h_ref, t, xs_ref, dest_ref[kk, t], sem).wait()
        return carry

    lax.fori_loop(0, tile, drain, 0)


def _dispatch(dest, h2, n_slots, tile):
    n, d = h2.shape
    return pl.pallas_call(
        functools.partial(_dispatch_kernel, tile),
        out_shape=jax.ShapeDtypeStruct((n_slots, d), h2.dtype),
        grid=(n // tile,),
        in_specs=[pl.BlockSpec((TOP_K, tile), lambda i: (0, i), memory_space=pltpu.SMEM),
                  pl.BlockSpec((tile, d), lambda i: (i, 0))],
        out_specs=pl.BlockSpec(memory_space=pl.ANY),
        scratch_shapes=[pltpu.SemaphoreType.DMA],
        compiler_params=_params(1), name="dispatch",
    )(dest, h2)


def _expert_kernel(bexp_ref, bvalid_ref, nused_ref, xs_ref, wg_ref, wu_ref, wd_ref, y_ref, wgb, wub, wdb):
    i = pl.program_id(0)

    @pl.when(i < nused_ref[0])
    def _():
        prev = bexp_ref[jnp.maximum(i - 1, 0)]

        @pl.when((i == 0) | (bexp_ref[i] != prev))
        def _():
            wgb[...] = wg_ref[...].astype(BF16)
            wub[...] = wu_ref[...].astype(BF16)
            wdb[...] = wd_ref[...].astype(BF16)

        row = lax.broadcasted_iota(jnp.int32, xs_ref.shape, 0)
        x = jnp.where(row < bvalid_ref[i], xs_ref[...], 0.0).astype(BF16)
        g = jnp.dot(x, wgb[...], preferred_element_type=F32)
        u = jnp.dot(x, wub[...], preferred_element_type=F32)
        y_ref[...] = jnp.dot((_silu(g) * u).astype(BF16), wdb[...], preferred_element_type=F32)


def _experts(block_exp, block_valid, n_used, xs, wg, wu, wd):
    n_slots, d = xs.shape
    ff = wg.shape[2]
    n_blocks = n_slots // EXPERT_BLOCK

    def blk(i, be, bv, nu):
        return (jnp.minimum(i, nu[0] - 1), 0)

    def wsel(i, be, bv, nu):
        return (be[jnp.minimum(i, nu[0] - 1)], 0, 0)

    grid_spec = pltpu.PrefetchScalarGridSpec(
        num_scalar_prefetch=3, grid=(n_blocks,),
        in_specs=[pl.BlockSpec((EXPERT_BLOCK, d), blk),
                  pl.BlockSpec((None, d, ff), wsel), pl.BlockSpec((None, d, ff), wsel),
                  pl.BlockSpec((None, ff, d), wsel)],
        out_specs=pl.BlockSpec((EXPERT_BLOCK, d), blk),
        scratch_shapes=[pltpu.VMEM((d, ff), BF16), pltpu.VMEM((d, ff), BF16), pltpu.VMEM((ff, d), BF16)])
    return pl.pallas_call(
        _expert_kernel, out_shape=jax.ShapeDtypeStruct((n_slots, d), F32), grid_spec=grid_spec,
        compiler_params=_params(1), name="experts",
    )(block_exp, block_valid, n_used, xs, wg, wu, wd)


def _combine_kernel(tile, dest_ref, y_ref, w_ref, h_ref, x1_ref, g2_ref, sg_ref, su_ref, sd_ref, lg_ref, lb_ref,
                    o_ref, buf, sem):
    def issue(t, carry):
        for kk in range(TOP_K):
            _row_copy(y_ref, dest_ref[kk, t], buf.at[kk], t, sem).start(priority=kk % 2)
        return carry

    lax.fori_loop(0, tile, issue, 0)
    hb = h_ref[...].astype(BF16)
    g = jnp.dot(hb, sg_ref[...], preferred_element_type=F32)
    u = jnp.dot(hb, su_ref[...], preferred_element_type=F32)
    f = jnp.dot((_silu(g) * u).astype(BF16), sd_ref[...], preferred_element_type=F32)

    def drain(t, carry):
        for kk in range(TOP_K):
            _row_copy(y_ref, dest_ref[kk, t], buf.at[kk], t, sem).wait()
        return carry

    lax.fori_loop(0, tile, drain, 0)
    w = w_ref[...]
    for kk in range(TOP_K):
        f = f + w[:, kk:kk + 1] * buf[kk]
    o_ref[...] = _layer_norm(DEEPNORM_ALPHA * x1_ref[...] + g2_ref[0] * f, lg_ref[...], lb_ref[...])


def _combine(dest, y, top_w, h2, x1, g2, sg, su, sd, lg, lb, tile, tiles_per_batch):
    n, d = h2.shape
    row = lambda i: (i, 0)
    full = lambda i: (0, 0)

    def const(a):
        return pl.BlockSpec(a.shape, full, pipeline_mode=pl.Buffered(1))

    return pl.pallas_call(
        functools.partial(_combine_kernel, tile),
        out_shape=jax.ShapeDtypeStruct((n, d), F32),
        grid=(n // tile,),
        in_specs=[pl.BlockSpec((TOP_K, tile), lambda i: (0, i), memory_space=pltpu.SMEM),
                  pl.BlockSpec(memory_space=pl.ANY),
                  pl.BlockSpec((tile, TOP_K), row), pl.BlockSpec((tile, d), row), pl.BlockSpec((tile, d), row),
                  pl.BlockSpec((1, 1, d), lambda i: (i // tiles_per_batch, 0, 0)),
                  const(sg), const(su), const(sd), const(lg), const(lb)],
        out_specs=pl.BlockSpec((tile, d), row),
        scratch_shapes=[pltpu.VMEM((TOP_K, tile, d), F32), pltpu.SemaphoreType.DMA],
        compiler_params=_params(1), name="combine_shared_ln2",
    )(dest, y, top_w, h2, x1, g2, sg, su, sd, lg, lb)


def kernel(x, c, ctx, c_ctx, w_mod, b_mod, w_in, na_rpb, gla_w_decay_f, gla_b_decay_f, gla_w_decay_b, gla_b_decay_b,
           gla_norm_w, w_branch_a, w_branch_b, w_out, ln1_g, ln1_b, router_w, router_bias, exp_w_gate, exp_w_up,
           exp_w_down, sh_w_gate, sh_w_up, sh_w_down, ln2_g, ln2_b):
    batch, t_len, d = x.shape
    ctx_len = ctx.shape[1]
    n = batch * t_len
    assert w_mod.shape[0] == DEPTH == 1
    assert t_len % GLA_CHUNK == 0 and ctx_len == GLA_CHUNK and (t_len // GRID_W) % NA_ROW_BLOCK == 0

    mod_rows = 16
    c_all = jnp.zeros((mod_rows, d), F32).at[:batch].set(c).at[batch].set(c_ctx)
    mod = _modulation(c_all, w_mod[0], b_mod[0])
    sh1, sc1, g1, sh2, sc2, g2 = [mod[:batch, j * d:(j + 1) * d].reshape(batch, 1, d) for j in range(6)]
    sh1c = mod[batch:batch + 1, 0:d].reshape(1, 1, d)
    sc1c = mod[batch:batch + 1, d:2 * d].reshape(1, 1, d)

    offs = np.cumsum((0, NA_WIDTH, NA_WIDTH, NA_WIDTH, GLA_QK_WIDTH, GLA_QK_WIDTH, GLA_V_WIDTH, GLA_V_WIDTH,
                      GLA_GATE_RANK, GLA_GATE_RANK, d, d))
    qa, ka, va, qb, kb, vbc, gbc, lrf, lrb, ga, gbt = [w_in[0][:, offs[j]:offs[j + 1]] for j in range(11)]
    lr_cols = jnp.concatenate([lrf, lrb, jnp.zeros((d, LANES - 2 * GLA_GATE_RANK), F32)], axis=1)
    w_lat = jnp.concatenate([qa, ka, va, qb, kb, vbc, gbc, ga, gbt, lr_cols], axis=1).astype(BF16)
    w_ctx = jnp.concatenate([ka, va, kb, vbc, lr_cols], axis=1).astype(BF16)
    plain = ("plain",)
    lat_plan = ((3 * NA_WIDTH, (("scale", NA_HEAD_DIM ** -0.5), plain, plain)),
                (2 * GLA_QK_WIDTH, (("rope", GLA_DK ** -0.5), ("rope", 1.0))),
                (GLA_V_WIDTH, (plain, plain)), (GLA_V_WIDTH, (plain, plain)),
                (2 * d, (plain,) * 4), (LANES, (plain,)))
    ctx_plan = ((2 * NA_WIDTH, (plain, plain)), (GLA_QK_WIDTH, (plain,)), (GLA_V_WIDTH, (plain, plain)),
                (LANES, (plain,)))
    tile = 256
    x2d = x.reshape(n, d)
    na_qkv, gla_qk, vb, gb, gates, lr = _projection(
        x2d, sc1, sh1, w_lat, lat_plan, (BF16, BF16, BF16, BF16, BF16, F32), tile, t_len // tile,
        rope=_rope_tables(t_len))
    na_kv_c, k_c, v_c, lr_c = _projection(
        ctx.reshape(batch * ctx_len, d), sc1c, sh1c, w_ctx, ctx_plan, (BF16, BF16, BF16, F32), tile,
        batch * ctx_len // tile)

    out_a = _na_attention(na_qkv, na_kv_c, _na_bias_tables(na_rpb[0]), batch, t_len, ctx_len)

    wf, bf = _decay_weights(gla_w_decay_f[0], gla_b_decay_f[0], 0)
    wb, bb = _decay_weights(gla_w_decay_b[0], gla_b_decay_b[0], GLA_GATE_RANK)
    o_f, o_b = _gla(gla_qk, vb, lr, k_c, v_c, lr_c, wf, bf, wb, bb, batch, t_len, ctx_len)

    x1, h2, scores_t = _merge(
        out_a.reshape(n, NA_WIDTH), o_f.reshape(n, GLA_V_WIDTH), o_b.reshape(n, GLA_V_WIDTH), gb, gates, x2d,
        g1, sc2, sh2, w_branch_a[0].astype(BF16), w_branch_b[0].astype(BF16), w_out[0].astype(BF16),
        gla_norm_w[0].reshape(1, -1), ln1_g[0].reshape(1, d), ln1_b[0].reshape(1, d),
        router_w[0].T.astype(BF16), tile, t_len // tile)

    top_e, top_w, rank, counts = _route(scores_t, router_bias[0], 512)

    counts = counts[:, 0]
    n_blocks = n * TOP_K // EXPERT_BLOCK + N_EXPERTS
    blocks_per = (counts + EXPERT_BLOCK - 1) // EXPERT_BLOCK
    blk_end = jnp.cumsum(blocks_per)
    blk_start = blk_end - blocks_per
    dest = blk_start[top_e] * EXPERT_BLOCK + rank
    bidx = jnp.arange(n_blocks, dtype=jnp.int32)
    block_exp = jnp.minimum(jnp.sum(blk_end[None, :] <= bidx[:, None], axis=1), N_EXPERTS - 1).astype(jnp.int32)
    block_valid = jnp.clip(counts[block_exp] - (bidx - blk_start[block_exp]) * EXPERT_BLOCK, 0, EXPERT_BLOCK)
    n_used = blk_end[-1:].astype(jnp.int32)

    xs = _dispatch(dest, h2, n_blocks * EXPERT_BLOCK, tile)
    y = _experts(block_exp, block_valid.astype(jnp.int32), n_used, xs, exp_w_gate[0], exp_w_up[0], exp_w_down[0])
    out = _combine(dest, y, top_w.T, h2, x1, g2, sh_w_gate[0].astype(BF16), sh_w_up[0].astype(BF16),
                   sh_w_down[0].astype(BF16), ln2_g[0].reshape(1, d), ln2_b[0].reshape(1, d), tile, t_len // tile)
    return out.reshape(batch, t_len, d)
```

```python
import functools

import numpy as np
import jax
import jax.numpy as jnp
from jax import lax
from jax.experimental import pallas as pl
from jax.experimental.pallas import tpu as pltpu

F32 = jnp.float32
BF16 = jnp.bfloat16
U32 = jnp.uint32
HIGHEST = lax.Precision.HIGHEST

GRID_W = 64
NA_HEADS = 8
NA_HEAD_DIM = 64
NA_WIDTH = NA_HEADS * NA_HEAD_DIM
WIN_ROWS = 8
WIN_COLS = 16
GLA_HEADS = 4
GLA_DK = 128
GLA_DV = 256
GLA_QK_WIDTH = GLA_HEADS * GLA_DK
GLA_V_WIDTH = GLA_HEADS * GLA_DV
GLA_GATE_RANK = 16
GLA_TAU = 16.0
ROPE_BASE = 10000.0
N_EXPERTS = 256
TOP_K = 8
N_GROUPS = 8
TOPK_GROUPS = 4
GROUP_SIZE = N_EXPERTS // N_GROUPS
ROUTED_SCALE = 2.5
DEPTH = 1
DEEPNORM_ALPHA = (2 * DEPTH) ** 0.25
EPS = 1e-6

LANES = 128
NA_ROW_BLOCK = 4
NA_UNION_ROWS = NA_ROW_BLOCK + WIN_ROWS - 1
GLA_CHUNK = 256
GLA_DIAG = 16
EXPERT_BLOCK = 256
NEG_BIG = -1e30
VMEM_LIMIT = 56 * 1024 * 1024


def _params(n_axes, vmem=VMEM_LIMIT):
    return pltpu.CompilerParams(dimension_semantics=("arbitrary",) * n_axes, vmem_limit_bytes=vmem)


def _sigmoid(v):
    return 1.0 / (1.0 + jnp.exp(-v))


def _silu(v):
    return v * _sigmoid(v)


def _pack_pairs(v):
    m = v.shape[1] // 2
    lo = lax.bitcast_convert_type(v[:, :m].astype(BF16).astype(F32), U32) >> 16
    hi = lax.bitcast_convert_type(v[:, m:].astype(BF16).astype(F32), U32) & jnp.uint32(0xFFFF0000)
    return lo | hi


def _unpack_pairs(p):
    lo = lax.bitcast_convert_type(p << 16, F32)
    hi = lax.bitcast_convert_type(p & jnp.uint32(0xFFFF0000), F32)
    return jnp.concatenate([lo, hi], axis=1)


def _mod_kernel(c_ref, w_ref, b_ref, o_ref):
    o_ref[...] = jnp.dot(_silu(c_ref[...]), w_ref[...], preferred_element_type=F32, precision=HIGHEST) + b_ref[...]


def _modulation(c_all, w_mod, b_mod):
    rows, d = c_all.shape
    n = w_mod.shape[1]
    bn = 512
    return pl.pallas_call(
        _mod_kernel,
        out_shape=jax.ShapeDtypeStruct((rows, n), F32),
        grid=(n // bn,),
        in_specs=[pl.BlockSpec((rows, d), lambda j: (0, 0)),
                  pl.BlockSpec((d, bn), lambda j: (0, j)),
                  pl.BlockSpec((1, bn), lambda j: (0, j))],
        out_specs=pl.BlockSpec((rows, bn), lambda j: (0, j)),
        compiler_params=_params(1),
        name="modulation",
    )(c_all, w_mod, b_mod.reshape(1, n))


def _swap32(v):
    lane = lax.broadcasted_iota(jnp.int32, v.shape, 1)
    return jnp.where((lane % 64) < 32, pltpu.roll(v, 96, 1), pltpu.roll(v, 32, 1))


def _proj_kernel(plan, has_rope, *refs):
    x_ref, sc_ref, sh_ref, w_ref = refs[:4]
    pos = 4
    if has_rope:
        cos_ref, sin_ref = refs[4:6]
        pos = 6
    out_refs = refs[pos:]
    h = (x_ref[...] * (1.0 + sc_ref[0]) + sh_ref[0]).astype(BF16)
    col = 0
    for out_ref, (width, kinds) in zip(out_refs, plan):
        for j, kind in enumerate(kinds):
            cw = width // len(kinds)
            c0 = j * cw
            acc = jnp.dot(h, w_ref[:, col + c0:col + c0 + cw], preferred_element_type=F32)
            if kind[0] == "scale":
                acc = acc * kind[1]
            elif kind[0] == "rope":
                cos, sin = cos_ref[...], sin_ref[...]
                pieces = []
                for p in range(cw // LANES):
                    v = acc[:, p * LANES:(p + 1) * LANES]
                    pieces.append((v * cos + _swap32(v) * sin) * kind[1])
                acc = jnp.concatenate(pieces, axis=1)
            out_ref[:, c0:c0 + cw] = acc.astype(out_ref.dtype)
        col += width


def _projection(x2d, sc, sh, w, plan, out_dtypes, tile, tiles_per_mod, rope=None):
    n, d = x2d.shape
    in_specs = [pl.BlockSpec((tile, d), lambda i: (i, 0)),
                pl.BlockSpec((1, 1, d), lambda i: (i // tiles_per_mod, 0, 0)),
                pl.BlockSpec((1, 1, d), lambda i: (i // tiles_per_mod, 0, 0)),
                pl.BlockSpec(w.shape, lambda i: (0, 0), pipeline_mode=pl.Buffered(1))]
    args = [x2d, sc, sh, w]
    if rope is not None:
        in_specs += [pl.BlockSpec((tile, LANES), lambda i: (i % tiles_per_mod, 0))] * 2
        args += list(rope)
    out_shape = [jax.ShapeDtypeStruct((n, wd), dt) for (wd, _), dt in zip(plan, out_dtypes)]
    out_specs = [pl.BlockSpec((tile, wd), lambda i: (i, 0)) for (wd, _) in plan]
    return pl.pallas_call(
        functools.partial(_proj_kernel, plan, rope is not None),
        out_shape=out_shape,
        grid=(n // tile,),
        in_specs=in_specs,
        out_specs=out_specs,
        compiler_params=_params(1),
        name="in_proj" if rope is not None else "ctx_proj",
    )(*args)


def _rope_tables(t_len):
    half = GLA_DK // 2
    quarter = half // 2
    inv_freq = ROPE_BASE ** (-jnp.arange(quarter, dtype=F32) / quarter)
    pos = jnp.arange(t_len)
    row_ang = (pos // GRID_W).astype(F32)[:, None] * inv_freq[None, :]
    col_ang = (pos % GRID_W).astype(F32)[:, None] * inv_freq[None, :]
    cr, sr, cc, sn = jnp.cos(row_ang), jnp.sin(row_ang), jnp.cos(col_ang), jnp.sin(col_ang)
    cos = jnp.concatenate([cr, cr, cc, cc], axis=1)
    sin = jnp.concatenate([-sr, sr, -sn, sn], axis=1)
    return cos, sin


def _na_bias_tables(rpb):
    rb, ur, w = NA_ROW_BLOCK, NA_UNION_ROWS, GRID_W
    heads = rpb.shape[0]
    pad = jnp.pad(rpb, ((0, 0), (0, 0), (w, w)))
    toep = jnp.stack([pad[:, :, w + WIN_COLS - 1 - c:2 * w + WIN_COLS - 1 - c] for c in range(w)], axis=2)
    c = np.arange(w)[:, None]
    kc = np.arange(w)[None, :]
    col_start = np.clip(c - WIN_COLS // 2, 0, w - WIN_COLS)
    col_ok = (kc >= col_start) & (kc < col_start + WIN_COLS)
    toep = jnp.where(col_ok[None, None], toep, NEG_BIG)
    neg = jnp.full((heads, w, w), NEG_BIG, F32)
    half = WIN_ROWS // 2
    tables = []
    for lo, off in ((lambda i: 0, WIN_ROWS - 1), (lambda i: i, WIN_ROWS - 1 - half), (lambda i: ur - WIN_ROWS, 0)):
        rows_ = []
        for i in range(rb):
            blocks = [toep[:, j - i + off] if lo(i) <= j < lo(i) + WIN_ROWS else neg for j in range(ur)]
            rows_.append(jnp.concatenate(blocks, axis=2))
        tables.append(jnp.concatenate(rows_, axis=1))
    return jnp.stack(tables)


def _na_kernel(rows, q_ref, k_ref, v_ref, kc_ref, vc_ref, bias_ref, o_ref):
    rb = pl.program_id(1)
    ustart = jnp.clip(rb * NA_ROW_BLOCK - WIN_ROWS // 2, 0, rows - NA_UNION_ROWS)
    k0 = pl.multiple_of(ustart * GRID_W, GRID_W)
    nk = NA_UNION_ROWS * GRID_W
    nt = (((1,), (1,)), ((), ()))
    lane = lax.broadcasted_iota(jnp.int32, (NA_ROW_BLOCK * GRID_W, LANES), 1)
    for p in range(NA_WIDTH // LANES):
        ls = slice(p * LANES, (p + 1) * LANES)
        q = q_ref[:, ls]
        ku = k_ref[pl.ds(k0, nk), ls]
        vu = v_ref[pl.ds(k0, nk), ls]
        kc = kc_ref[:, ls]
        vc = vc_ref[:, ls]
        out = jnp.zeros(q.shape, F32)
        for hh in range(LANES // NA_HEAD_DIM):
            mine = (lane // NA_HEAD_DIM) == hh
            qm = jnp.where(mine, q, jnp.zeros_like(q))
            s_loc = lax.dot_general(qm, ku, nt, preferred_element_type=F32) + bias_ref[p * 2 + hh]
            s_ctx = lax.dot_general(qm, kc, nt, preferred_element_type=F32)
            m = jnp.maximum(jnp.max(s_loc, axis=1, keepdims=True), jnp.max(s_ctx, axis=1, keepdims=True))
            p_loc = jnp.exp(s_loc - m)
            p_ctx = jnp.exp(s_ctx - m)
            denom = jnp.sum(p_loc, axis=1, keepdims=True) + jnp.sum(p_ctx, axis=1, keepdims=True)
            o = (jnp.dot(p_loc.astype(BF16), vu, preferred_element_type=F32)
                 + jnp.dot(p_ctx.astype(BF16), vc, preferred_element_type=F32)) / denom
            out = jnp.where(mine, o, out)
        o_ref[:, ls] = out.astype(o_ref.dtype)


def _na_attention(na_qkv, na_kv_ctx, bias_tables, batch, t_len, ctx_len):
    rows = t_len // GRID_W
    n_rb = rows // NA_ROW_BLOCK
    tq = NA_ROW_BLOCK * GRID_W
    qkv = na_qkv.reshape(batch, t_len, 3 * NA_WIDTH)
    kvc = na_kv_ctx.reshape(batch, ctx_len, 2 * NA_WIDTH)

    def bias_idx(b, r):
        return (jnp.where(r == 0, 0, jnp.where(r == n_rb - 1, 2, 1)), 0, 0, 0)

    return pl.pallas_call(
        functools.partial(_na_kernel, rows),
        out_shape=jax.ShapeDtypeStruct((batch, t_len, NA_WIDTH), BF16),
        grid=(batch, n_rb),
        in_specs=[pl.BlockSpec((None, tq, NA_WIDTH), lambda b, r: (b, r, 0)),
                  pl.BlockSpec((None, t_len, NA_WIDTH), lambda b, r: (b, 0, 1)),
                  pl.BlockSpec((None, t_len, NA_WIDTH), lambda b, r: (b, 0, 2)),
                  pl.BlockSpec((None, ctx_len, NA_WIDTH), lambda b, r: (b, 0, 0)),
                  pl.BlockSpec((None, ctx_len, NA_WIDTH), lambda b, r: (b, 0, 1)),
                  pl.BlockSpec((None,) + bias_tables.shape[1:], bias_idx)],
        out_specs=pl.BlockSpec((None, tq, NA_WIDTH), lambda b, r: (b, r, 0)),
        compiler_params=_params(2),
        name="na_attention",
    )(qkv, qkv, qkv, kvc, kvc, bias_tables)


def _log_decay(lr, w2, b2):
    z = jnp.dot(lr.astype(BF16), w2.astype(BF16), preferred_element_type=F32) + b2
    return (jnp.minimum(z, 0.0) - jnp.log(1.0 + jnp.exp(-jnp.abs(z)))) * (1.0 / GLA_TAU)


def _gla_constants(c):
    tris, masks = [], []
    for reverse in (False, True):
        i = np.arange(c)[:, None]
        j = np.arange(c)[None, :]
        tris.append((j >= i) if reverse else (j <= i))
        if reverse:
            i, j = j, i
        level = []
        s = c // 2
        while s >= GLA_DIAG:
            level.append(((i // (2 * s)) == (j // (2 * s))) & ((i % (2 * s)) >= s) & ((j % (2 * s)) < s))
            s //= 2
        level.append(((i // GLA_DIAG) == (j // GLA_DIAG)) & (j <= i))
        masks.append(np.stack(level))
    return jnp.asarray(np.stack(tris), BF16), jnp.asarray(np.stack(masks), F32)


def _block_refs(cum, s, reverse, diag):
    c = cum.shape[0]
    span = s if diag else 2 * s
    parts = []
    for p in range(c // span):
        if diag:
            r = p * span + (span - 1 if reverse else 0)
        else:
            r = p * span + (s - 1 if reverse else s)
        parts.append(jnp.broadcast_to(cum[r:r + 1, :], (span, cum.shape[1])))
    return jnp.concatenate(parts, axis=0)


def _cumsum_rows(a, tri):
    hi = a.astype(BF16)
    rest = a - hi.astype(F32)
    mid = rest.astype(BF16)
    lo = (rest - mid.astype(F32)).astype(BF16)
    parts = jnp.dot(tri, jnp.concatenate([hi, mid, lo], axis=1), preferred_element_type=F32)
    w = a.shape[1]
    return parts[:, :w] + parts[:, w:2 * w] + parts[:, 2 * w:]


def _gla_chunk(q, k, v, a, state_t, tri, mask_ref, reverse, want_out):
    c = k.shape[0]
    nt = (((1,), (1,)), ((), ()))
    cum = _cumsum_rows(a, tri)
    last = 0 if reverse else c - 1
    total = cum[last:last + 1, :]
    out = None
    if want_out:
        q_in = (q * jnp.exp(cum)).astype(BF16)
        out = lax.dot_general(q_in, state_t.astype(BF16), nt, preferred_element_type=F32)
        scores = jnp.zeros((c, c), F32)
        s = c // 2
        level = 0
        while True:
            diag = s < GLA_DIAG
            d = cum - _block_refs(cum, GLA_DIAG if diag else s, reverse, diag)
            if diag:
                qs = q * jnp.exp(d)
                ks = k * jnp.exp(-d)
            else:
                qs = q * jnp.exp(jnp.minimum(d, 0.0))
                ks = k * jnp.exp(jnp.minimum(-d, 0.0))
            att = lax.dot_general(qs.astype(BF16), ks.astype(BF16), nt, preferred_element_type=F32)
            scores = scores + att * mask_ref[level]
            if diag:
                break
            s //= 2
            level += 1
        out = out + jnp.dot(scores.astype(BF16), v, preferred_element_type=F32)
    k_out = (k * jnp.exp(total - cum)).astype(BF16)
    upd = lax.dot_general(v, k_out, (((0,), (0,)), ((), ())), preferred_element_type=F32)
    return out, jnp.exp(total) * state_t + upd


def _gla_kernel(qf_ref, kf_ref, vf_ref, lrf_ref, qb_ref, kb_ref, vb_ref, lrb_ref,
                kc_ref, vc_ref, lrc_ref, wf_ref, bf_ref, wb_ref, bb_ref, tri_ref, mask_ref,
                of_ref, ob_ref, sf_ref, sb_ref):
    c = pl.program_id(2)
    fm, bm = mask_ref.at[0], mask_ref.at[1]

    @pl.when(c == 0)
    def _():
        kc = kc_ref[...].astype(F32)
        vc = vc_ref[...]
        lrc = lrc_ref[...]
        zero = jnp.zeros(sf_ref.shape, F32)
        _, sf_ref[...] = _gla_chunk(None, kc, vc, _log_decay(lrc, wf_ref[...], bf_ref[...]), zero,
                                    tri_ref[0], fm, False, False)
        _, sb_ref[...] = _gla_chunk(None, kc, vc, _log_decay(lrc, wb_ref[...], bb_ref[...]), zero,
                                    tri_ref[1], bm, True, False)

    @pl.when(c > 0)
    def _():
        o, sf_ref[...] = _gla_chunk(qf_ref[...].astype(F32), kf_ref[...].astype(F32), vf_ref[...],
                                    _log_decay(lrf_ref[...], wf_ref[...], bf_ref[...]), sf_ref[...],
                                    tri_ref[0], fm, False, True)
        of_ref[...] = o.astype(of_ref.dtype)
        o, sb_ref[...] = _gla_chunk(qb_ref[...].astype(F32), kb_ref[...].astype(F32), vb_ref[...],
                                    _log_decay(lrb_ref[...], wb_ref[...], bb_ref[...]), sb_ref[...],
                                    tri_ref[1], bm, True, True)
        ob_ref[...] = o.astype(ob_ref.dtype)


def _gla(gla_qk, vb, lr, k_ctx, v_ctx, lr_ctx, wf, bf, wb, bb, batch, t_len, ctx_len):
    nc = t_len // GLA_CHUNK
    h = GLA_HEADS
    qk = gla_qk.reshape(batch, t_len, 2 * GLA_QK_WIDTH)
    v3 = vb.reshape(batch, t_len, GLA_V_WIDTH)
    lr3 = lr.reshape(batch, t_len, LANES)
    kc3 = k_ctx.reshape(batch, ctx_len, GLA_QK_WIDTH)
    vc3 = v_ctx.reshape(batch, ctx_len, GLA_V_WIDTH)
    lrc3 = lr_ctx.reshape(batch, ctx_len, LANES)
    tri, masks = _gla_constants(GLA_CHUNK)

    def fwd(c):
        return jnp.maximum(c - 1, 0)

    def bwd(c):
        return nc - 1 - jnp.maximum(c - 1, 0)

    def const(a):
        return pl.BlockSpec(a.shape, lambda b, hh, c: (0,) * a.ndim, pipeline_mode=pl.Buffered(1))

    cq = (None, GLA_CHUNK, GLA_DK)
    cv = (None, GLA_CHUNK, GLA_DV)
    in_specs = [
        pl.BlockSpec(cq, lambda b, hh, c: (b, fwd(c), hh)),
        pl.BlockSpec(cq, lambda b, hh, c: (b, fwd(c), h + hh)),
        pl.BlockSpec(cv, lambda b, hh, c: (b, fwd(c), hh)),
        pl.BlockSpec(cq, lambda b, hh, c: (b, fwd(c), 0)),
        pl.BlockSpec(cq, lambda b, hh, c: (b, bwd(c), hh)),
        pl.BlockSpec(cq, lambda b, hh, c: (b, bwd(c), h + hh)),
        pl.BlockSpec(cv, lambda b, hh, c: (b, bwd(c), hh)),
        pl.BlockSpec(cq, lambda b, hh, c: (b, bwd(c), 0)),
        pl.BlockSpec((None, ctx_len, GLA_DK), lambda b, hh, c: (b, 0, hh)),
        pl.BlockSpec((None, ctx_len, GLA_DV), lambda b, hh, c: (b, 0, hh)),
        pl.BlockSpec((None, ctx_len, LANES), lambda b, hh, c: (b, 0, 0)),
        pl.BlockSpec((None, LANES, GLA_DK), lambda b, hh, c: (hh, 0, 0)),
        pl.BlockSpec((None, 1, GLA_DK), lambda b, hh, c: (hh, 0, 0)),
        pl.BlockSpec((None, LANES, GLA_DK), lambda b, hh, c: (hh, 0, 0)),
        pl.BlockSpec((None, 1, GLA_DK), lambda b, hh, c: (hh, 0, 0)),
        const(tri), const(masks),
    ]
    out_specs = [pl.BlockSpec(cv, lambda b, hh, c: (b, fwd(c), hh)),
                 pl.BlockSpec(cv, lambda b, hh, c: (b, bwd(c), hh))]
    out_shape = [jax.ShapeDtypeStruct((batch, t_len, GLA_V_WIDTH), BF16)] * 2
    return pl.pallas_call(
        _gla_kernel,
        out_shape=out_shape,
        grid=(batch, h, nc + 1),
        in_specs=in_specs,
        out_specs=out_specs,
        scratch_shapes=[pltpu.VMEM((GLA_DV, GLA_DK), F32), pltpu.VMEM((GLA_DV, GLA_DK), F32)],
        compiler_params=_params(3),
        name="gla",
    )(qk, qk, v3, lr3, qk, qk, v3, lr3, kc3, vc3, lrc3, wf, bf, wb, bb, tri, masks)


def _decay_weights(w_dec, b_dec, row0):
    w = w_dec.reshape(GLA_GATE_RANK, GLA_HEADS, GLA_DK).transpose(1, 0, 2)
    wp = jnp.zeros((GLA_HEADS, LANES, GLA_DK), F32).at[:, row0:row0 + GLA_GATE_RANK, :].set(w)
    return wp, b_dec.reshape(GLA_HEADS, 1, GLA_DK)


def _layer_norm(v, g, b):
    mu = jnp.mean(v, axis=1, keepdims=True)
    var = jnp.mean(jnp.square(v - mu), axis=1, keepdims=True)
    return (v - mu) * lax.rsqrt(var + EPS) * g + b


def _merge_kernel(oa_ref, of_ref, ob_ref, gb_ref, gates_ref, x_ref, g1_ref, sc2_ref, sh2_ref,
                  wa_ref, wb_ref, wo_ref, nw_ref, lg_ref, lb_ref, rw_ref,
                  x1_ref, h2_ref, sc_ref):
    d = x_ref.shape[1]
    o = of_ref[...].astype(F32) + ob_ref[...].astype(F32)
    pieces = []
    for hh in range(GLA_HEADS):
        oh = o[:, hh * GLA_DV:(hh + 1) * GLA_DV]
        pieces.append(oh * lax.rsqrt(jnp.mean(jnp.square(oh), axis=1, keepdims=True) + EPS))
    out_b = jnp.concatenate(pieces, axis=1) * nw_ref[...] * _silu(gb_ref[...].astype(F32))
    ya = jnp.dot(oa_ref[...], wa_ref[...], preferred_element_type=F32)
    yb = jnp.dot(out_b.astype(BF16), wb_ref[...], preferred_element_type=F32)
    y = _sigmoid(gates_ref[:, :d].astype(F32)) * ya + _sigmoid(gates_ref[:, d:].astype(F32)) * yb
    y2 = jnp.dot(y.astype(BF16), wo_ref[...], preferred_element_type=F32)
    x1 = _layer_norm(DEEPNORM_ALPHA * x_ref[...] + g1_ref[0] * y2, lg_ref[...], lb_ref[...])
    x1_ref[...] = x1
    h2 = x1 * (1.0 + sc2_ref[0]) + sh2_ref[0]
    h2_ref[...] = _pack_pairs(h2)
    logits_t = lax.dot_general(rw_ref[...], h2.astype(BF16), (((1,), (1,)), ((), ())), preferred_element_type=F32)
    sc_ref[...] = _sigmoid(logits_t)


def _merge(out_a, o_f, o_b, gb, gates, x2d, g1, sc2, sh2, wa, wb, wo, nw, lg, lb, rw_t, tile, tiles_per_batch):
    n, d = x2d.shape
    row = lambda i: (i, 0)
    mod = lambda i: (i // tiles_per_batch, 0, 0)
    full = lambda i: (0, 0)

    def const(a):
        return pl.BlockSpec(a.shape, full, pipeline_mode=pl.Buffered(1))

    in_specs = [pl.BlockSpec((tile, NA_WIDTH), row), pl.BlockSpec((tile, GLA_V_WIDTH), row),
                pl.BlockSpec((tile, GLA_V_WIDTH), row), pl.BlockSpec((tile, GLA_V_WIDTH), row),
                pl.BlockSpec((tile, 2 * d), row), pl.BlockSpec((tile, d), row),
                pl.BlockSpec((1, 1, d), mod), pl.BlockSpec((1, 1, d), mod), pl.BlockSpec((1, 1, d), mod),
                const(wa), const(wb), const(wo), const(nw), const(lg), const(lb), const(rw_t)]
    out_shape = [jax.ShapeDtypeStruct((n, d), F32), jax.ShapeDtypeStruct((n, d // 2), U32),
                 jax.ShapeDtypeStruct((N_EXPERTS, n), F32)]
    out_specs = [pl.BlockSpec((tile, d), row), pl.BlockSpec((tile, d // 2), row),
                 pl.BlockSpec((N_EXPERTS, tile), lambda i: (0, i))]
    return pl.pallas_call(
        _merge_kernel, out_shape=out_shape, grid=(n // tile,), in_specs=in_specs, out_specs=out_specs,
        compiler_params=_params(1), name="merge_ln1_router",
    )(out_a, o_f, o_b, gb, gates, x2d, g1, sc2, sh2, wa, wb, wo, nw, lg, lb, rw_t)


def _first_argmax(vals, idx, n):
    m = jnp.max(vals, axis=0, keepdims=True)
    first = jnp.min(jnp.where(vals == m, idx, float(n)), axis=0, keepdims=True)
    return m, first


def _route_kernel(sc_ref, bias_ref, e_ref, w_ref, rank_ref, cnt_ref, carry_ref):
    step = pl.program_id(0)
    tr = sc_ref.shape[1]

    @pl.when(step == 0)
    def _():
        carry_ref[...] = jnp.zeros(carry_ref.shape, F32)

    scores = sc_ref[...]
    biased = scores + bias_ref[...]
    eidx = lax.broadcasted_iota(jnp.int32, (N_EXPERTS, tr), 0).astype(F32)
    lidx = lax.broadcasted_iota(jnp.int32, (GROUP_SIZE, tr), 0).astype(F32)
    gidx = lax.broadcasted_iota(jnp.int32, (N_GROUPS, tr), 0).astype(F32)
    gs = []
    for g in range(N_GROUPS):
        blk = biased[g * GROUP_SIZE:(g + 1) * GROUP_SIZE]
        m1, first = _first_argmax(blk, lidx, GROUP_SIZE)
        m2 = jnp.max(jnp.where(lidx == first, -jnp.inf, blk), axis=0, keepdims=True)
        gs.append(m1 + m2)
    cur = jnp.concatenate(gs, axis=0)
    keep = jnp.zeros((N_GROUPS, tr), F32)
    for _ in range(TOPK_GROUPS):
        _, first = _first_argmax(cur, gidx, N_GROUPS)
        sel = gidx == first
        keep = jnp.where(sel, 1.0, keep)
        cur = jnp.where(sel, -jnp.inf, cur)
    keep_e = jnp.concatenate([jnp.broadcast_to(keep[g:g + 1], (GROUP_SIZE, tr)) for g in range(N_GROUPS)], axis=0)
    masked = jnp.where(keep_e > 0.5, biased, -jnp.inf)
    chosen = jnp.zeros((N_EXPERTS, tr), F32)
    tops, topi = [], []
    for _ in range(TOP_K):
        _, first = _first_argmax(masked, eidx, N_EXPERTS)
        sel = eidx == first
        tops.append(jnp.sum(jnp.where(sel, scores, 0.0), axis=0, keepdims=True))
        topi.append(first)
        chosen = jnp.where(sel, 1.0, chosen)
        masked = jnp.where(sel, -jnp.inf, masked)
    top_s = jnp.concatenate(tops, axis=0)
    top_i = jnp.concatenate(topi, axis=0)
    e_ref[...] = top_i.astype(jnp.int32)
    w_ref[...] = top_s / jnp.sum(top_s, axis=0, keepdims=True) * ROUTED_SCALE
    r = lax.broadcasted_iota(jnp.int32, (tr, tr), 0)
    cidx = lax.broadcasted_iota(jnp.int32, (tr, tr), 1)
    before = jnp.where(r < cidx, 1.0, 0.0).astype(BF16)
    prior = jnp.dot(chosen.astype(BF16), before, preferred_element_type=F32) + carry_ref[...]
    ranks = [jnp.sum(jnp.where(eidx == topi[kk], prior, 0.0), axis=0, keepdims=True) for kk in range(TOP_K)]
    rank_ref[...] = jnp.concatenate(ranks, axis=0).astype(jnp.int32)
    carry_ref[...] = carry_ref[...] + jnp.sum(chosen, axis=1, keepdims=True)
    cnt_ref[...] = jnp.broadcast_to(carry_ref[...], cnt_ref.shape).astype(jnp.int32)


def _route(scores_t, router_bias, tile):
    n = scores_t.shape[1]
    col = lambda i: (0, i)
    out_shape = [jax.ShapeDtypeStruct((TOP_K, n), jnp.int32), jax.ShapeDtypeStruct((TOP_K, n), F32),
                 jax.ShapeDtypeStruct((TOP_K, n), jnp.int32), jax.ShapeDtypeStruct((N_EXPERTS, LANES), jnp.int32)]
    return pl.pallas_call(
        _route_kernel, out_shape=out_shape, grid=(n // tile,),
        in_specs=[pl.BlockSpec((N_EXPERTS, tile), col), pl.BlockSpec((N_EXPERTS, 1), lambda i: (0, 0))],
        out_specs=[pl.BlockSpec((TOP_K, tile), col), pl.BlockSpec((TOP_K, tile), col),
                   pl.BlockSpec((TOP_K, tile), col), pl.BlockSpec((N_EXPERTS, LANES), lambda i: (0, 0))],
        scratch_shapes=[pltpu.VMEM((N_EXPERTS, 1), F32)],
        compiler_params=_params(1), name="route",
    )(scores_t, router_bias.reshape(N_EXPERTS, 1))


def _slots_kernel(e_ref, rank_ref, start_ref, dest_ref):
    tr = e_ref.shape[1]
    eidx = lax.broadcasted_iota(jnp.int32, (N_EXPERTS, tr), 0)
    e = e_ref[...]
    start = start_ref[...]
    rows = [jnp.sum(jnp.where(eidx == e[kk:kk + 1], start, 0.0), axis=0, keepdims=True) for kk in range(TOP_K)]
    dest_ref[...] = jnp.concatenate(rows, axis=0).astype(jnp.int32) + rank_ref[...]


def _slots(top_e, rank, start_rows, tile):
    n = top_e.shape[1]
    col = lambda i: (0, i)
    return pl.pallas_call(
        _slots_kernel, out_shape=jax.ShapeDtypeStruct((TOP_K, n), jnp.int32), grid=(n // tile,),
        in_specs=[pl.BlockSpec((TOP_K, tile), col), pl.BlockSpec((TOP_K, tile), col),
                  pl.BlockSpec((N_EXPERTS, 1), lambda i: (0, 0))],
        out_specs=pl.BlockSpec((TOP_K, tile), col),
        compiler_params=_params(1), name="slots",
    )(top_e, rank, start_rows)


def _row_copy(src, s, dst, t, sem):
    return pltpu.make_async_copy(src.at[pl.ds(s, 1)], dst.at[pl.ds(t, 1)], sem)


def _dispatch_kernel(tile, dest_ref, h_ref, xs_ref, sem):
    def issue(t, carry):
        for kk in range(TOP_K):
            _row_copy(h_ref, t, xs_ref, dest_ref[kk, t], sem).start(priority=kk % 2)
        return carry

    lax.fori_loop(0, tile, issue, 0)

    def drain(t, carry):
        for kk in range(TOP_K):
            _row_copy(h_ref, t, xs_ref, dest_ref[kk, t], sem).wait()
        return carry

    lax.fori_loop(0, tile, drain, 0)


def _dispatch(dest, h2p, n_slots, tile):
    n, dp = h2p.shape
    return pl.pallas_call(
        functools.partial(_dispatch_kernel, tile),
        out_shape=jax.ShapeDtypeStruct((n_slots, dp), h2p.dtype),
        grid=(n // tile,),
        in_specs=[pl.BlockSpec((TOP_K, tile), lambda i: (0, i), memory_space=pltpu.SMEM),
                  pl.BlockSpec((tile, dp), lambda i: (i, 0))],
        out_specs=pl.BlockSpec(memory_space=pl.ANY),
        scratch_shapes=[pltpu.SemaphoreType.DMA],
        compiler_params=_params(1), name="dispatch",
    )(dest, h2p)


def _expert_kernel(bexp_ref, bvalid_ref, nused_ref, xs_ref, wg_ref, wu_ref, wd_ref, y_ref, wgb, wub, wdb):
    i = pl.program_id(0)

    @pl.when(i < nused_ref[0])
    def _():
        prev = bexp_ref[jnp.maximum(i - 1, 0)]

        @pl.when((i == 0) | (bexp_ref[i] != prev))
        def _():
            wgb[...] = wg_ref[...].astype(BF16)
            wub[...] = wu_ref[...].astype(BF16)
            wdb[...] = wd_ref[...].astype(BF16)

        row = lax.broadcasted_iota(jnp.int32, xs_ref.shape, 0)
        x = _unpack_pairs(jnp.where(row < bvalid_ref[i], xs_ref[...], jnp.uint32(0))).astype(BF16)
        g = jnp.dot(x, wgb[...], preferred_element_type=F32)
        u = jnp.dot(x, wub[...], preferred_element_type=F32)
        y_ref[...] = _pack_pairs(jnp.dot((_silu(g) * u).astype(BF16), wdb[...], preferred_element_type=F32))


def _experts(block_exp, block_valid, n_used, xs, wg, wu, wd):
    n_slots, dp = xs.shape
    d, ff = wg.shape[1], wg.shape[2]
    n_blocks = n_slots // EXPERT_BLOCK

    def blk(i, be, bv, nu):
        return (jnp.minimum(i, nu[0] - 1), 0)

    def wsel(i, be, bv, nu):
        return (be[jnp.minimum(i, nu[0] - 1)], 0, 0)

    grid_spec = pltpu.PrefetchScalarGridSpec(
        num_scalar_prefetch=3, grid=(n_blocks,),
        in_specs=[pl.BlockSpec((EXPERT_BLOCK, dp), blk),
                  pl.BlockSpec((None, d, ff), wsel), pl.BlockSpec((None, d, ff), wsel),
                  pl.BlockSpec((None, ff, d), wsel)],
        out_specs=pl.BlockSpec((EXPERT_BLOCK, dp), blk),
        scratch_shapes=[pltpu.VMEM((d, ff), BF16), pltpu.VMEM((d, ff), BF16), pltpu.VMEM((ff, d), BF16)])
    return pl.pallas_call(
        _expert_kernel, out_shape=jax.ShapeDtypeStruct((n_slots, dp), U32), grid_spec=grid_spec,
        compiler_params=_params(1), name="experts",
    )(block_exp, block_valid, n_used, xs, wg, wu, wd)


def _combine_kernel(tile, dest_ref, y_ref, w_ref, h_ref, x1_ref, g2_ref, sg_ref, su_ref, sd_ref, lg_ref, lb_ref,
                    o_ref, buf, sem):
    def issue(t, carry):
        for kk in range(TOP_K):
            _row_copy(y_ref, dest_ref[kk, t], buf.at[kk], t, sem).start(priority=kk % 2)
        return carry

    lax.fori_loop(0, tile, issue, 0)
    hb = _unpack_pairs(h_ref[...]).astype(BF16)
    g = jnp.dot(hb, sg_ref[...], preferred_element_type=F32)
    u = jnp.dot(hb, su_ref[...], preferred_element_type=F32)
    f = jnp.dot((_silu(g) * u).astype(BF16), sd_ref[...], preferred_element_type=F32)

    def drain(t, carry):
        for kk in range(TOP_K):
            _row_copy(y_ref, dest_ref[kk, t], buf.at[kk], t, sem).wait()
        return carry

    lax.fori_loop(0, tile, drain, 0)
    w = w_ref[...]
    for kk in range(TOP_K):
        f = f + w[:, kk:kk + 1] * _unpack_pairs(buf[kk])
    o_ref[...] = _layer_norm(DEEPNORM_ALPHA * x1_ref[...] + g2_ref[0] * f, lg_ref[...], lb_ref[...])


def _combine(dest, y, top_w, h2p, x1, g2, sg, su, sd, lg, lb, tile, tiles_per_batch):
    n, d = x1.shape
    dp = h2p.shape[1]
    row = lambda i: (i, 0)
    full = lambda i: (0, 0)

    def const(a):
        return pl.BlockSpec(a.shape, full, pipeline_mode=pl.Buffered(1))

    return pl.pallas_call(
        functools.partial(_combine_kernel, tile),
        out_shape=jax.ShapeDtypeStruct((n, d), F32),
        grid=(n // tile,),
        in_specs=[pl.BlockSpec((TOP_K, tile), lambda i: (0, i), memory_space=pltpu.SMEM),
                  pl.BlockSpec(memory_space=pl.ANY),
                  pl.BlockSpec((tile, TOP_K), row), pl.BlockSpec((tile, dp), row), pl.BlockSpec((tile, d), row),
                  pl.BlockSpec((1, 1, d), lambda i: (i // tiles_per_batch, 0, 0)),
                  const(sg), const(su), const(sd), const(lg), const(lb)],
        out_specs=pl.BlockSpec((tile, d), row),
        scratch_shapes=[pltpu.VMEM((TOP_K, tile, dp), U32), pltpu.SemaphoreType.DMA],
        compiler_params=_params(1), name="combine_shared_ln2",
    )(dest, y, top_w, h2p, x1, g2, sg, su, sd, lg, lb)


def kernel(x, c, ctx, c_ctx, w_mod, b_mod, w_in, na_rpb, gla_w_decay_f, gla_b_decay_f, gla_w_decay_b, gla_b_decay_b,
           gla_norm_w, w_branch_a, w_branch_b, w_out, ln1_g, ln1_b, router_w, router_bias, exp_w_gate, exp_w_up,
           exp_w_down, sh_w_gate, sh_w_up, sh_w_down, ln2_g, ln2_b):
    batch, t_len, d = x.shape
    ctx_len = ctx.shape[1]
    n = batch * t_len
    assert w_mod.shape[0] == DEPTH == 1
    assert t_len % GLA_CHUNK == 0 and ctx_len == GLA_CHUNK and (t_len // GRID_W) % NA_ROW_BLOCK == 0

    mod_rows = 16
    c_all = jnp.zeros((mod_rows, d), F32).at[:batch].set(c).at[batch].set(c_ctx)
    mod = _modulation(c_all, w_mod[0], b_mod[0])
    sh1, sc1, g1, sh2, sc2, g2 = [mod[:batch, j * d:(j + 1) * d].reshape(batch, 1, d) for j in range(6)]
    sh1c = mod[batch:batch + 1, 0:d].reshape(1, 1, d)
    sc1c = mod[batch:batch + 1, d:2 * d].reshape(1, 1, d)

    offs = np.cumsum((0, NA_WIDTH, NA_WIDTH, NA_WIDTH, GLA_QK_WIDTH, GLA_QK_WIDTH, GLA_V_WIDTH, GLA_V_WIDTH,
                      GLA_GATE_RANK, GLA_GATE_RANK, d, d))
    qa, ka, va, qb, kb, vbc, gbc, lrf, lrb, ga, gbt = [w_in[0][:, offs[j]:offs[j + 1]] for j in range(11)]
    lr_cols = jnp.concatenate([lrf, lrb, jnp.zeros((d, LANES - 2 * GLA_GATE_RANK), F32)], axis=1)
    w_lat = jnp.concatenate([qa, ka, va, qb, kb, vbc, gbc, ga, gbt, lr_cols], axis=1).astype(BF16)
    w_ctx = jnp.concatenate([ka, va, kb, vbc, lr_cols], axis=1).astype(BF16)
    plain = ("plain",)
    lat_plan = ((3 * NA_WIDTH, (("scale", NA_HEAD_DIM ** -0.5), plain, plain)),
                (2 * GLA_QK_WIDTH, (("rope", GLA_DK ** -0.5), ("rope", 1.0))),
                (GLA_V_WIDTH, (plain, plain)), (GLA_V_WIDTH, (plain, plain)),
                (2 * d, (plain,) * 4), (LANES, (plain,)))
    ctx_plan = ((2 * NA_WIDTH, (plain, plain)), (GLA_QK_WIDTH, (plain,)), (GLA_V_WIDTH, (plain, plain)),
                (LANES, (plain,)))
    tile = 256
    x2d = x.reshape(n, d)
    na_qkv, gla_qk, vb, gb, gates, lr = _projection(
        x2d, sc1, sh1, w_lat, lat_plan, (BF16, BF16, BF16, BF16, BF16, F32), tile, t_len // tile,
        rope=_rope_tables(t_len))
    na_kv_c, k_c, v_c, lr_c = _projection(
        ctx.reshape(batch * ctx_len, d), sc1c, sh1c, w_ctx, ctx_plan, (BF16, BF16, BF16, F32), tile,
        batch * ctx_len // tile)

    out_a = _na_attention(na_qkv, na_kv_c, _na_bias_tables(na_rpb[0]), batch, t_len, ctx_len)

    wf, bf = _decay_weights(gla_w_decay_f[0], gla_b_decay_f[0], 0)
    wb, bb = _decay_weights(gla_w_decay_b[0], gla_b_decay_b[0], GLA_GATE_RANK)
    o_f, o_b = _gla(gla_qk, vb, lr, k_c, v_c, lr_c, wf, bf, wb, bb, batch, t_len, ctx_len)

    x1, h2p, scores_t = _merge(
        out_a.reshape(n, NA_WIDTH), o_f.reshape(n, GLA_V_WIDTH), o_b.reshape(n, GLA_V_WIDTH), gb, gates, x2d,
        g1, sc2, sh2, w_branch_a[0].astype(BF16), w_branch_b[0].astype(BF16), w_out[0].astype(BF16),
        gla_norm_w[0].reshape(1, -1), ln1_g[0].reshape(1, d), ln1_b[0].reshape(1, d),
        router_w[0].T.astype(BF16), tile, t_len // tile)

    top_e, top_w, rank, counts = _route(scores_t, router_bias[0], 512)

    counts = counts[:, 0]
    n_blocks = n * TOP_K // EXPERT_BLOCK + N_EXPERTS
    blocks_per = (counts + EXPERT_BLOCK - 1) // EXPERT_BLOCK
    blk_end = jnp.cumsum(blocks_per)
    blk_start = blk_end - blocks_per
    dest = _slots(top_e, rank, (blk_start * EXPERT_BLOCK).astype(F32).reshape(N_EXPERTS, 1), 512)
    bidx = jnp.arange(n_blocks, dtype=jnp.int32)
    block_exp = jnp.minimum(jnp.sum(blk_end[None, :] <= bidx[:, None], axis=1), N_EXPERTS - 1).astype(jnp.int32)
    block_valid = jnp.clip(counts[block_exp] - (bidx - blk_start[block_exp]) * EXPERT_BLOCK, 0, EXPERT_BLOCK)
    n_used = blk_end[-1:].astype(jnp.int32)

    xs = _dispatch(dest, h2p, n_blocks * EXPERT_BLOCK, tile)
    y = _experts(block_exp, block_valid.astype(jnp.int32), n_used, xs, exp_w_gate[0], exp_w_up[0], exp_w_down[0])
    out = _combine(dest, y, top_w.T, h2p, x1, g2, sh_w_gate[0].astype(BF16), sh_w_up[0].astype(BF16),
                   sh_w_down[0].astype(BF16), ln2_g[0].reshape(1, d), ln2_b[0].reshape(1, d), tile, t_len // tile)
    return out.reshape(batch, t_len, d)
```

```python
import functools

import numpy as np
import jax
import jax.numpy as jnp
from jax import lax
from jax.experimental import pallas as pl
from jax.experimental.pallas import tpu as pltpu

F32 = jnp.float32
BF16 = jnp.bfloat16
U32 = jnp.uint32
HIGHEST = lax.Precision.HIGHEST

GRID_W = 64
NA_HEADS = 8
NA_HEAD_DIM = 64
NA_WIDTH = NA_HEADS * NA_HEAD_DIM
WIN_ROWS = 8
WIN_COLS = 16
GLA_HEADS = 4
GLA_DK = 128
GLA_DV = 256
GLA_QK_WIDTH = GLA_HEADS * GLA_DK
GLA_V_WIDTH = GLA_HEADS * GLA_DV
GLA_GATE_RANK = 16
GLA_TAU = 16.0
LOG2E = 1.4426950408889634
ROPE_BASE = 10000.0
N_EXPERTS = 256
TOP_K = 8
N_GROUPS = 8
TOPK_GROUPS = 4
GROUP_SIZE = N_EXPERTS // N_GROUPS
ROUTED_SCALE = 2.5
DEPTH = 1
DEEPNORM_ALPHA = (2 * DEPTH) ** 0.25
EPS = 1e-6

LANES = 128
NA_ROW_BLOCK = 4
NA_UNION_ROWS = NA_ROW_BLOCK + WIN_ROWS - 1
GLA_CHUNK = 256
GLA_DIAG = 16
GLA_HEADS_PER_STEP = 4
EXPERT_BLOCK = 256
NEG_BIG = -1e30
VMEM_LIMIT = 56 * 1024 * 1024


def _params(n_axes, vmem=VMEM_LIMIT):
    return pltpu.CompilerParams(dimension_semantics=("arbitrary",) * n_axes, vmem_limit_bytes=vmem)


def _sigmoid(v):
    return 1.0 / (1.0 + jnp.exp(-v))


def _silu(v):
    return v * _sigmoid(v)


def _pack_pairs(v):
    m = v.shape[1] // 2
    lo = lax.bitcast_convert_type(v[:, :m].astype(BF16).astype(F32), U32) >> 16
    hi = lax.bitcast_convert_type(v[:, m:].astype(BF16).astype(F32), U32) & jnp.uint32(0xFFFF0000)
    return lo | hi


def _unpack_pairs(p):
    lo = lax.bitcast_convert_type(p << 16, F32)
    hi = lax.bitcast_convert_type(p & jnp.uint32(0xFFFF0000), F32)
    return jnp.concatenate([lo, hi], axis=1)


def _mod_kernel(c_ref, w_ref, b_ref, o_ref):
    o_ref[...] = jnp.dot(_silu(c_ref[...]), w_ref[...], preferred_element_type=F32, precision=HIGHEST) + b_ref[...]


def _modulation(c_all, w_mod, b_mod):
    rows, d = c_all.shape
    n = w_mod.shape[1]
    bn = 512
    return pl.pallas_call(
        _mod_kernel,
        out_shape=jax.ShapeDtypeStruct((rows, n), F32),
        grid=(n // bn,),
        in_specs=[pl.BlockSpec((rows, d), lambda j: (0, 0)),
                  pl.BlockSpec((d, bn), lambda j: (0, j)),
                  pl.BlockSpec((1, bn), lambda j: (0, j))],
        out_specs=pl.BlockSpec((rows, bn), lambda j: (0, j)),
        compiler_params=_params(1),
        name="modulation",
    )(c_all, w_mod, b_mod.reshape(1, n))


def _swap32(v):
    lane = lax.broadcasted_iota(jnp.int32, v.shape, 1)
    return jnp.where((lane % 64) < 32, pltpu.roll(v, 96, 1), pltpu.roll(v, 32, 1))


def _proj_kernel(plan, has_rope, *refs):
    x_ref, sc_ref, sh_ref, w_ref = refs[:4]
    pos = 4
    if has_rope:
        cos_ref, sin_ref = refs[4:6]
        pos = 6
    out_refs = refs[pos:]
    h = (x_ref[...] * (1.0 + sc_ref[0]) + sh_ref[0]).astype(BF16)
    col = 0
    for out_ref, (width, kinds) in zip(out_refs, plan):
        for j, kind in enumerate(kinds):
            cw = width // len(kinds)
            c0 = j * cw
            acc = jnp.dot(h, w_ref[:, col + c0:col + c0 + cw], preferred_element_type=F32)
            if kind[0] == "scale":
                acc = acc * kind[1]
            elif kind[0] == "rope":
                cos, sin = cos_ref[...], sin_ref[...]
                pieces = []
                for p in range(cw // LANES):
                    v = acc[:, p * LANES:(p + 1) * LANES]
                    pieces.append((v * cos + _swap32(v) * sin) * kind[1])
                acc = jnp.concatenate(pieces, axis=1)
            out_ref[:, c0:c0 + cw] = acc.astype(out_ref.dtype)
        col += width


def _projection(x2d, sc, sh, w, plan, out_dtypes, tile, tiles_per_mod, rope=None):
    n, d = x2d.shape
    in_specs = [pl.BlockSpec((tile, d), lambda i: (i, 0)),
                pl.BlockSpec((1, 1, d), lambda i: (i // tiles_per_mod, 0, 0)),
                pl.BlockSpec((1, 1, d), lambda i: (i // tiles_per_mod, 0, 0)),
                pl.BlockSpec(w.shape, lambda i: (0, 0), pipeline_mode=pl.Buffered(1))]
    args = [x2d, sc, sh, w]
    if rope is not None:
        in_specs += [pl.BlockSpec((tile, LANES), lambda i: (i % tiles_per_mod, 0))] * 2
        args += list(rope)
    out_shape = [jax.ShapeDtypeStruct((n, wd), dt) for (wd, _), dt in zip(plan, out_dtypes)]
    out_specs = [pl.BlockSpec((tile, wd), lambda i: (i, 0)) for (wd, _) in plan]
    return pl.pallas_call(
        functools.partial(_proj_kernel, plan, rope is not None),
        out_shape=out_shape,
        grid=(n // tile,),
        in_specs=in_specs,
        out_specs=out_specs,
        compiler_params=_params(1),
        name="in_proj" if rope is not None else "ctx_proj",
    )(*args)


def _rope_tables(t_len):
    half = GLA_DK // 2
    quarter = half // 2
    inv_freq = ROPE_BASE ** (-jnp.arange(quarter, dtype=F32) / quarter)
    pos = jnp.arange(t_len)
    row_ang = (pos // GRID_W).astype(F32)[:, None] * inv_freq[None, :]
    col_ang = (pos % GRID_W).astype(F32)[:, None] * inv_freq[None, :]
    cr, sr, cc, sn = jnp.cos(row_ang), jnp.sin(row_ang), jnp.cos(col_ang), jnp.sin(col_ang)
    cos = jnp.concatenate([cr, cr, cc, cc], axis=1)
    sin = jnp.concatenate([-sr, sr, -sn, sn], axis=1)
    return cos, sin


def _na_bias_tables(rpb):
    rb, ur, w = NA_ROW_BLOCK, NA_UNION_ROWS, GRID_W
    heads = rpb.shape[0]
    pad = jnp.pad(rpb, ((0, 0), (0, 0), (w, w)))
    toep = jnp.stack([pad[:, :, w + WIN_COLS - 1 - c:2 * w + WIN_COLS - 1 - c] for c in range(w)], axis=2)
    c = np.arange(w)[:, None]
    kc = np.arange(w)[None, :]
    col_start = np.clip(c - WIN_COLS // 2, 0, w - WIN_COLS)
    col_ok = (kc >= col_start) & (kc < col_start + WIN_COLS)
    toep = jnp.where(col_ok[None, None], toep, NEG_BIG)
    neg = jnp.full((heads, w, w), NEG_BIG, F32)
    half = WIN_ROWS // 2
    tables = []
    for lo, off in ((lambda i: 0, WIN_ROWS - 1), (lambda i: i, WIN_ROWS - 1 - half), (lambda i: ur - WIN_ROWS, 0)):
        rows_ = []
        for i in range(rb):
            blocks = [toep[:, j - i + off] if lo(i) <= j < lo(i) + WIN_ROWS else neg for j in range(ur)]
            rows_.append(jnp.concatenate(blocks, axis=2))
        tables.append(jnp.concatenate(rows_, axis=1))
    return jnp.stack(tables)


def _na_kernel(rows, q_ref, k_ref, v_ref, kc_ref, vc_ref, bias_ref, o_ref):
    rb = pl.program_id(1)
    ustart = jnp.clip(rb * NA_ROW_BLOCK - WIN_ROWS // 2, 0, rows - NA_UNION_ROWS)
    k0 = pl.multiple_of(ustart * GRID_W, GRID_W)
    nk = NA_UNION_ROWS * GRID_W
    nt = (((1,), (1,)), ((), ()))
    lane = lax.broadcasted_iota(jnp.int32, (NA_ROW_BLOCK * GRID_W, LANES), 1)
    for p in range(NA_WIDTH // LANES):
        ls = slice(p * LANES, (p + 1) * LANES)
        q = q_ref[:, ls]
        ku = k_ref[pl.ds(k0, nk), ls]
        vu = v_ref[pl.ds(k0, nk), ls]
        kc = kc_ref[:, ls]
        vc = vc_ref[:, ls]
        out = jnp.zeros(q.shape, F32)
        for hh in range(LANES // NA_HEAD_DIM):
            mine = (lane // NA_HEAD_DIM) == hh
            qm = jnp.where(mine, q, jnp.zeros_like(q))
            s_loc = lax.dot_general(qm, ku, nt, preferred_element_type=F32) + bias_ref[p * 2 + hh]
            s_ctx = lax.dot_general(qm, kc, nt, preferred_element_type=F32)
            m = jnp.maximum(jnp.max(s_loc, axis=1, keepdims=True), jnp.max(s_ctx, axis=1, keepdims=True))
            p_loc = jnp.exp(s_loc - m)
            p_ctx = jnp.exp(s_ctx - m)
            denom = jnp.sum(p_loc, axis=1, keepdims=True) + jnp.sum(p_ctx, axis=1, keepdims=True)
            o = (jnp.dot(p_loc.astype(BF16), vu, preferred_element_type=F32)
                 + jnp.dot(p_ctx.astype(BF16), vc, preferred_element_type=F32)) / denom
            out = jnp.where(mine, o, out)
        o_ref[:, ls] = out.astype(o_ref.dtype)


def _na_attention(na_qkv, na_kv_ctx, bias_tables, batch, t_len, ctx_len):
    rows = t_len // GRID_W
    n_rb = rows // NA_ROW_BLOCK
    tq = NA_ROW_BLOCK * GRID_W
    qkv = na_qkv.reshape(batch, t_len, 3 * NA_WIDTH)
    kvc = na_kv_ctx.reshape(batch, ctx_len, 2 * NA_WIDTH)

    def bias_idx(b, r):
        return (jnp.where(r == 0, 0, jnp.where(r == n_rb - 1, 2, 1)), 0, 0, 0)

    return pl.pallas_call(
        functools.partial(_na_kernel, rows),
        out_shape=jax.ShapeDtypeStruct((batch, t_len, NA_WIDTH), BF16),
        grid=(batch, n_rb),
        in_specs=[pl.BlockSpec((None, tq, NA_WIDTH), lambda b, r: (b, r, 0)),
                  pl.BlockSpec((None, t_len, NA_WIDTH), lambda b, r: (b, 0, 1)),
                  pl.BlockSpec((None, t_len, NA_WIDTH), lambda b, r: (b, 0, 2)),
                  pl.BlockSpec((None, ctx_len, NA_WIDTH), lambda b, r: (b, 0, 0)),
                  pl.BlockSpec((None, ctx_len, NA_WIDTH), lambda b, r: (b, 0, 1)),
                  pl.BlockSpec((None,) + bias_tables.shape[1:], bias_idx)],
        out_specs=pl.BlockSpec((None, tq, NA_WIDTH), lambda b, r: (b, r, 0)),
        compiler_params=_params(2),
        name="na_attention",
    )(qkv, qkv, qkv, kvc, kvc, bias_tables)


def _log2_decay(lr, w2, b2):
    z = (jnp.dot(lr.astype(BF16), w2.astype(BF16), preferred_element_type=F32) + b2) * LOG2E
    return (jnp.minimum(z, 0.0) - jnp.log2(1.0 + jnp.exp2(-jnp.abs(z)))) * (1.0 / GLA_TAU)


def _gla_constants(c):
    tris, masks = [], []
    for reverse in (False, True):
        i = np.arange(c)[:, None]
        j = np.arange(c)[None, :]
        tris.append((j >= i) if reverse else (j <= i))
        i = np.arange(c // 2)[:, None]
        j = np.arange(c // 2)[None, :]
        if reverse:
            i, j = j, i
        level = []
        s = c // 4
        while s >= GLA_DIAG:
            level.append(((i // (2 * s)) == (j // (2 * s))) & ((i % (2 * s)) >= s) & ((j % (2 * s)) < s))
            s //= 2
        level.append(((i // GLA_DIAG) == (j // GLA_DIAG)) & (j <= i))
        masks.append(np.stack(level))
    return jnp.asarray(np.stack(tris), BF16), jnp.asarray(np.stack(masks), F32)


def _block_refs(cum, s, reverse, diag):
    c = cum.shape[0]
    span = s if diag else 2 * s
    parts = []
    for p in range(c // span):
        if diag:
            r = p * span + (span - 1 if reverse else 0)
        else:
            r = p * span + (s - 1 if reverse else s)
        parts.append(jnp.broadcast_to(cum[r:r + 1, :], (span, cum.shape[1])))
    return jnp.concatenate(parts, axis=0)


def _cumsum_rows(a, tri):
    hi = a.astype(BF16)
    rest = a - hi.astype(F32)
    mid = rest.astype(BF16)
    lo = (rest - mid.astype(F32)).astype(BF16)
    parts = jnp.dot(tri, jnp.concatenate([hi, mid, lo], axis=1), preferred_element_type=F32)
    w = a.shape[1]
    return parts[:, :w] + parts[:, w:2 * w] + parts[:, 2 * w:]


def _gla_chunk(q, k, v, a, state_t, tri, mask_ref, reverse, want_out):
    c = k.shape[0]
    hc = c // 2
    nt = (((1,), (1,)), ((), ()))
    cum = _cumsum_rows(a, tri)
    last = 0 if reverse else c - 1
    total = cum[last:last + 1, :]
    out = None
    if want_out:
        q_in = (q * jnp.exp2(cum)).astype(BF16)
        out = lax.dot_general(q_in, state_t.astype(BF16), nt, preferred_element_type=F32)
        halves = (slice(hc, c), slice(0, hc)) if reverse else (slice(0, hc), slice(hc, c))
        early, late = halves

        def scaled(s, diag):
            d = cum - _block_refs(cum, GLA_DIAG if diag else s, reverse, diag)
            if diag:
                return (q * jnp.exp2(d)).astype(BF16), (k * jnp.exp2(-d)).astype(BF16)
            e = jnp.exp2(-jnp.abs(d))
            return (q * e).astype(BF16), (k * e).astype(BF16)

        qs, ks = scaled(hc, False)
        cross = lax.dot_general(qs[late], ks[early], nt, preferred_element_type=F32)
        inner = [jnp.zeros((hc, hc), F32), jnp.zeros((hc, hc), F32)]
        s = hc // 2
        level = 0
        while True:
            diag = s < GLA_DIAG
            qs, ks = scaled(s, diag)
            for x, rows in enumerate(halves):
                att = lax.dot_general(qs[rows], ks[rows], nt, preferred_element_type=F32)
                inner[x] = inner[x] + att * mask_ref[level]
            if diag:
                break
            s //= 2
            level += 1
        out_early = jnp.dot(inner[0].astype(BF16), v[early], preferred_element_type=F32)
        out_late = jnp.dot(jnp.concatenate([cross, inner[1]], axis=1).astype(BF16),
                           jnp.concatenate([v[early], v[late]], axis=0), preferred_element_type=F32)
        intra = [out_late, out_early] if reverse else [out_early, out_late]
        out = out + jnp.concatenate(intra, axis=0)
    k_out = (k * jnp.exp2(total - cum)).astype(BF16)
    upd = lax.dot_general(v, k_out, (((0,), (0,)), ((), ())), preferred_element_type=F32)
    return out, jnp.exp2(total) * state_t + upd


def _gla_kernel(qf_ref, kf_ref, vf_ref, lrf_ref, qb_ref, kb_ref, vb_ref, lrb_ref,
                kc_ref, vc_ref, lrc_ref, wf_ref, bf_ref, wb_ref, bb_ref, tri_ref, mask_ref,
                of_ref, ob_ref, sf_ref, sb_ref):
    c = pl.program_id(2)
    fm, bm = mask_ref.at[0], mask_ref.at[1]
    heads = sf_ref.shape[0]

    @pl.when(c == 0)
    def _():
        lrc = lrc_ref[...]
        zero = jnp.zeros(sf_ref.shape[1:], F32)
        for h in range(heads):
            kc = kc_ref[:, h * GLA_DK:(h + 1) * GLA_DK].astype(F32)
            vc = vc_ref[:, h * GLA_DV:(h + 1) * GLA_DV]
            _, sf_ref[h] = _gla_chunk(None, kc, vc, _log2_decay(lrc, wf_ref[h], bf_ref[h]), zero,
                                      tri_ref[0], fm, False, False)
            _, sb_ref[h] = _gla_chunk(None, kc, vc, _log2_decay(lrc, wb_ref[h], bb_ref[h]), zero,
                                      tri_ref[1], bm, True, False)

    @pl.when(c > 0)
    def _():
        lrf, lrb = lrf_ref[...], lrb_ref[...]
        for h in range(heads):
            ks = slice(h * GLA_DK, (h + 1) * GLA_DK)
            vs = slice(h * GLA_DV, (h + 1) * GLA_DV)
            o, sf_ref[h] = _gla_chunk(qf_ref[:, ks].astype(F32), kf_ref[:, ks].astype(F32), vf_ref[:, vs],
                                      _log2_decay(lrf, wf_ref[h], bf_ref[h]), sf_ref[h],
                                      tri_ref[0], fm, False, True)
            of_ref[:, vs] = o.astype(of_ref.dtype)
            o, sb_ref[h] = _gla_chunk(qb_ref[:, ks].astype(F32), kb_ref[:, ks].astype(F32), vb_ref[:, vs],
                                      _log2_decay(lrb, wb_ref[h], bb_ref[h]), sb_ref[h],
                                      tri_ref[1], bm, True, True)
            ob_ref[:, vs] = o.astype(ob_ref.dtype)


def _gla(gla_qk, vb, lr, k_ctx, v_ctx, lr_ctx, wf, bf, wb, bb, batch, t_len, ctx_len):
    nc = t_len // GLA_CHUNK
    h = GLA_HEADS
    qk = gla_qk.reshape(batch, t_len, 2 * GLA_QK_WIDTH)
    v3 = vb.reshape(batch, t_len, GLA_V_WIDTH)
    lr3 = lr.reshape(batch, t_len, LANES)
    kc3 = k_ctx.reshape(batch, ctx_len, GLA_QK_WIDTH)
    vc3 = v_ctx.reshape(batch, ctx_len, GLA_V_WIDTH)
    lrc3 = lr_ctx.reshape(batch, ctx_len, LANES)
    tri, masks = _gla_constants(GLA_CHUNK)

    def fwd(c):
        return jnp.maximum(c - 1, 0)

    def bwd(c):
        return nc - 1 - jnp.maximum(c - 1, 0)

    def const(a):
        return pl.BlockSpec(a.shape, lambda b, hh, c: (0,) * a.ndim, pipeline_mode=pl.Buffered(1))

    hp = GLA_HEADS_PER_STEP
    groups = h // hp
    cq = (None, GLA_CHUNK, hp * GLA_DK)
    cv = (None, GLA_CHUNK, hp * GLA_DV)
    cl = (None, GLA_CHUNK, LANES)
    in_specs = [
        pl.BlockSpec(cq, lambda b, g, c: (b, fwd(c), g)),
        pl.BlockSpec(cq, lambda b, g, c: (b, fwd(c), groups + g)),
        pl.BlockSpec(cv, lambda b, g, c: (b, fwd(c), g)),
        pl.BlockSpec(cl, lambda b, g, c: (b, fwd(c), 0)),
        pl.BlockSpec(cq, lambda b, g, c: (b, bwd(c), g)),
        pl.BlockSpec(cq, lambda b, g, c: (b, bwd(c), groups + g)),
        pl.BlockSpec(cv, lambda b, g, c: (b, bwd(c), g)),
        pl.BlockSpec(cl, lambda b, g, c: (b, bwd(c), 0)),
        pl.BlockSpec((None, ctx_len, hp * GLA_DK), lambda b, g, c: (b, 0, g)),
        pl.BlockSpec((None, ctx_len, hp * GLA_DV), lambda b, g, c: (b, 0, g)),
        pl.BlockSpec((None, ctx_len, LANES), lambda b, g, c: (b, 0, 0)),
        pl.BlockSpec((hp, LANES, GLA_DK), lambda b, g, c: (g, 0, 0)),
        pl.BlockSpec((hp, 1, GLA_DK), lambda b, g, c: (g, 0, 0)),
        pl.BlockSpec((hp, LANES, GLA_DK), lambda b, g, c: (g, 0, 0)),
        pl.BlockSpec((hp, 1, GLA_DK), lambda b, g, c: (g, 0, 0)),
        const(tri), const(masks),
    ]
    out_specs = [pl.BlockSpec(cv, lambda b, g, c: (b, fwd(c), g)),
                 pl.BlockSpec(cv, lambda b, g, c: (b, bwd(c), g))]
    out_shape = [jax.ShapeDtypeStruct((batch, t_len, GLA_V_WIDTH), BF16)] * 2
    return pl.pallas_call(
        _gla_kernel,
        out_shape=out_shape,
        grid=(batch, groups, nc + 1),
        in_specs=in_specs,
        out_specs=out_specs,
        scratch_shapes=[pltpu.VMEM((hp, GLA_DV, GLA_DK), F32), pltpu.VMEM((hp, GLA_DV, GLA_DK), F32)],
        compiler_params=_params(3),
        name="gla",
    )(qk, qk, v3, lr3, qk, qk, v3, lr3, kc3, vc3, lrc3, wf, bf, wb, bb, tri, masks)


def _decay_weights(w_dec, b_dec, row0):
    w = w_dec.reshape(GLA_GATE_RANK, GLA_HEADS, GLA_DK).transpose(1, 0, 2)
    wp = jnp.zeros((GLA_HEADS, LANES, GLA_DK), F32).at[:, row0:row0 + GLA_GATE_RANK, :].set(w)
    return wp, b_dec.reshape(GLA_HEADS, 1, GLA_DK)


def _layer_norm(v, g, b):
    mu = jnp.mean(v, axis=1, keepdims=True)
    var = jnp.mean(jnp.square(v - mu), axis=1, keepdims=True)
    return (v - mu) * lax.rsqrt(var + EPS) * g + b


def _merge_kernel(oa_ref, of_ref, ob_ref, gb_ref, gates_ref, x_ref, g1_ref, sc2_ref, sh2_ref,
                  wa_ref, wb_ref, wo_ref, nw_ref, lg_ref, lb_ref, rw_ref,
                  x1_ref, h2_ref, sc_ref):
    d = x_ref.shape[1]
    o = of_ref[...].astype(F32) + ob_ref[...].astype(F32)
    pieces = []
    for hh in range(GLA_HEADS):
        oh = o[:, hh * GLA_DV:(hh + 1) * GLA_DV]
        pieces.append(oh * lax.rsqrt(jnp.mean(jnp.square(oh), axis=1, keepdims=True) + EPS))
    out_b = jnp.concatenate(pieces, axis=1) * nw_ref[...] * _silu(gb_ref[...].astype(F32))
    ya = jnp.dot(oa_ref[...], wa_ref[...], preferred_element_type=F32)
    yb = jnp.dot(out_b.astype(BF16), wb_ref[...], preferred_element_type=F32)
    y = _sigmoid(gates_ref[:, :d].astype(F32)) * ya + _sigmoid(gates_ref[:, d:].astype(F32)) * yb
    y2 = jnp.dot(y.astype(BF16), wo_ref[...], preferred_element_type=F32)
    x1 = _layer_norm(DEEPNORM_ALPHA * x_ref[...] + g1_ref[0] * y2, lg_ref[...], lb_ref[...])
    x1_ref[...] = x1
    h2 = x1 * (1.0 + sc2_ref[0]) + sh2_ref[0]
    h2_ref[...] = _pack_pairs(h2)
    logits_t = lax.dot_general(rw_ref[...], h2.astype(BF16), (((1,), (1,)), ((), ())), preferred_element_type=F32)
    sc_ref[...] = _sigmoid(logits_t)


def _merge(out_a, o_f, o_b, gb, gates, x2d, g1, sc2, sh2, wa, wb, wo, nw, lg, lb, rw_t, tile, tiles_per_batch):
    n, d = x2d.shape
    row = lambda i: (i, 0)
    mod = lambda i: (i // tiles_per_batch, 0, 0)
    full = lambda i: (0, 0)

    def const(a):
        return pl.BlockSpec(a.shape, full, pipeline_mode=pl.Buffered(1))

    in_specs = [pl.BlockSpec((tile, NA_WIDTH), row), pl.BlockSpec((tile, GLA_V_WIDTH), row),
                pl.BlockSpec((tile, GLA_V_WIDTH), row), pl.BlockSpec((tile, GLA_V_WIDTH), row),
                pl.BlockSpec((tile, 2 * d), row), pl.BlockSpec((tile, d), row),
                pl.BlockSpec((1, 1, d), mod), pl.BlockSpec((1, 1, d), mod), pl.BlockSpec((1, 1, d), mod),
                const(wa), const(wb), const(wo), const(nw), const(lg), const(lb), const(rw_t)]
    out_shape = [jax.ShapeDtypeStruct((n, d), F32), jax.ShapeDtypeStruct((n, d // 2), U32),
                 jax.ShapeDtypeStruct((N_EXPERTS, n), F32)]
    out_specs = [pl.BlockSpec((tile, d), row), pl.BlockSpec((tile, d // 2), row),
                 pl.BlockSpec((N_EXPERTS, tile), lambda i: (0, i))]
    return pl.pallas_call(
        _merge_kernel, out_shape=out_shape, grid=(n // tile,), in_specs=in_specs, out_specs=out_specs,
        compiler_params=_params(1), name="merge_ln1_router",
    )(out_a, o_f, o_b, gb, gates, x2d, g1, sc2, sh2, wa, wb, wo, nw, lg, lb, rw_t)


def _first_argmax(vals, idx, n):
    m = jnp.max(vals, axis=0, keepdims=True)
    first = jnp.min(jnp.where(vals == m, idx, float(n)), axis=0, keepdims=True)
    return m, first


def _route_kernel(sc_ref, bias_ref, e_ref, w_ref, rank_ref, cnt_ref, carry_ref):
    step = pl.program_id(0)
    tr = sc_ref.shape[1]

    @pl.when(step == 0)
    def _():
        carry_ref[...] = jnp.zeros(carry_ref.shape, F32)

    scores = sc_ref[...]
    biased = scores + bias_ref[...]
    eidx = lax.broadcasted_iota(jnp.int32, (N_EXPERTS, tr), 0).astype(F32)
    lidx = lax.broadcasted_iota(jnp.int32, (GROUP_SIZE, tr), 0).astype(F32)
    gidx = lax.broadcasted_iota(jnp.int32, (N_GROUPS, tr), 0).astype(F32)
    gs = []
    for g in range(N_GROUPS):
        blk = biased[g * GROUP_SIZE:(g + 1) * GROUP_SIZE]
        m1, first = _first_argmax(blk, lidx, GROUP_SIZE)
        m2 = jnp.max(jnp.where(lidx == first, -jnp.inf, blk), axis=0, keepdims=True)
        gs.append(m1 + m2)
    cur = jnp.concatenate(gs, axis=0)
    keep = jnp.zeros((N_GROUPS, tr), F32)
    for _ in range(TOPK_GROUPS):
        _, first = _first_argmax(cur, gidx, N_GROUPS)
        sel = gidx == first
        keep = jnp.where(sel, 1.0, keep)
        cur = jnp.where(sel, -jnp.inf, cur)
    keep_e = jnp.concatenate([jnp.broadcast_to(keep[g:g + 1], (GROUP_SIZE, tr)) for g in range(N_GROUPS)], axis=0)
    masked = jnp.where(keep_e > 0.5, biased, -jnp.inf)
    chosen = jnp.zeros((N_EXPERTS, tr), F32)
    tops, topi = [], []
    for _ in range(TOP_K):
        _, first = _first_argmax(masked, eidx, N_EXPERTS)
        sel = eidx == first
        tops.append(jnp.sum(jnp.where(sel, scores, 0.0), axis=0, keepdims=True))
        topi.append(first)
        chosen = jnp.where(sel, 1.0, chosen)
        masked = jnp.where(sel, -jnp.inf, masked)
    top_s = jnp.concatenate(tops, axis=0)
    top_i = jnp.concatenate(topi, axis=0)
    e_ref[...] = top_i.astype(jnp.int32)
    w_ref[...] = top_s / jnp.sum(top_s, axis=0, keepdims=True) * ROUTED_SCALE
    r = lax.broadcasted_iota(jnp.int32, (tr, tr), 0)
    cidx = lax.broadcasted_iota(jnp.int32, (tr, tr), 1)
    before = jnp.where(r < cidx, 1.0, 0.0).astype(BF16)
    prior = jnp.dot(chosen.astype(BF16), before, preferred_element_type=F32) + carry_ref[...]
    ranks = [jnp.sum(jnp.where(eidx == topi[kk], prior, 0.0), axis=0, keepdims=True) for kk in range(TOP_K)]
    rank_ref[...] = jnp.concatenate(ranks, axis=0).astype(jnp.int32)
    carry_ref[...] = carry_ref[...] + jnp.sum(chosen, axis=1, keepdims=True)
    cnt_ref[...] = jnp.broadcast_to(carry_ref[...], cnt_ref.shape).astype(jnp.int32)


def _route(scores_t, router_bias, tile):
    n = scores_t.shape[1]
    col = lambda i: (0, i)
    out_shape = [jax.ShapeDtypeStruct((TOP_K, n), jnp.int32), jax.ShapeDtypeStruct((TOP_K, n), F32),
                 jax.ShapeDtypeStruct((TOP_K, n), jnp.int32), jax.ShapeDtypeStruct((N_EXPERTS, LANES), jnp.int32)]
    return pl.pallas_call(
        _route_kernel, out_shape=out_shape, grid=(n // tile,),
        in_specs=[pl.BlockSpec((N_EXPERTS, tile), col), pl.BlockSpec((N_EXPERTS, 1), lambda i: (0, 0))],
        out_specs=[pl.BlockSpec((TOP_K, tile), col), pl.BlockSpec((TOP_K, tile), col),
                   pl.BlockSpec((TOP_K, tile), col), pl.BlockSpec((N_EXPERTS, LANES), lambda i: (0, 0))],
        scratch_shapes=[pltpu.VMEM((N_EXPERTS, 1), F32)],
        compiler_params=_params(1), name="route",
    )(scores_t, router_bias.reshape(N_EXPERTS, 1))


def _slots_kernel(e_ref, rank_ref, start_ref, dest_ref):
    tr = e_ref.shape[1]
    eidx = lax.broadcasted_iota(jnp.int32, (N_EXPERTS, tr), 0)
    e = e_ref[...]
    start = start_ref[...]
    rows = [jnp.sum(jnp.where(eidx == e[kk:kk + 1], start, 0.0), axis=0, keepdims=True) for kk in range(TOP_K)]
    dest_ref[...] = jnp.concatenate(rows, axis=0).astype(jnp.int32) + rank_ref[...]


def _slots(top_e, rank, start_rows, tile):
    n = top_e.shape[1]
    col = lambda i: (0, i)
    return pl.pallas_call(
        _slots_kernel, out_shape=jax.ShapeDtypeStruct((TOP_K, n), jnp.int32), grid=(n // tile,),
        in_specs=[pl.BlockSpec((TOP_K, tile), col), pl.BlockSpec((TOP_K, tile), col),
                  pl.BlockSpec((N_EXPERTS, 1), lambda i: (0, 0))],
        out_specs=pl.BlockSpec((TOP_K, tile), col),
        compiler_params=_params(1), name="slots",
    )(top_e, rank, start_rows)


def _row_copy(src, s, dst, t, sem):
    return pltpu.make_async_copy(src.at[pl.ds(s, 1)], dst.at[pl.ds(t, 1)], sem)


def _dispatch_kernel(tile, dest_ref, h_ref, xs_ref, sem):
    def issue(t, carry):
        for kk in range(TOP_K):
            _row_copy(h_ref, t, xs_ref, dest_ref[kk, t], sem).start(priority=kk % 2)
        return carry

    lax.fori_loop(0, tile, issue, 0)

    def drain(t, carry):
        for kk in range(TOP_K):
            _row_copy(h_ref, t, xs_ref, dest_ref[kk, t], sem).wait()
        return carry

    lax.fori_loop(0, tile, drain, 0)


def _dispatch(dest, h2p, n_slots, tile):
    n, dp = h2p.shape
    return pl.pallas_call(
        functools.partial(_dispatch_kernel, tile),
        out_shape=jax.ShapeDtypeStruct((n_slots, dp), h2p.dtype),
        grid=(n // tile,),
        in_specs=[pl.BlockSpec((TOP_K, tile), lambda i: (0, i), memory_space=pltpu.SMEM),
                  pl.BlockSpec((tile, dp), lambda i: (i, 0))],
        out_specs=pl.BlockSpec(memory_space=pl.ANY),
        scratch_shapes=[pltpu.SemaphoreType.DMA],
        compiler_params=_params(1), name="dispatch",
    )(dest, h2p)


def _expert_kernel(start_ref, cnt_ref, xs_ref, wg_ref, wu_ref, wd_ref, y_ref,
                   wgb, wub, wdb, xbuf, ybuf, sem_in, sem_out):
    e = pl.program_id(0)
    bm = EXPERT_BLOCK
    cnt = cnt_ref[e]
    nb = (cnt + bm - 1) // bm
    row0 = start_ref[e] * bm

    def x_copy(j, slot):
        return pltpu.make_async_copy(xs_ref.at[pl.ds(row0 + j * bm, bm)], xbuf.at[slot], sem_in.at[slot])

    def y_copy(j, slot):
        return pltpu.make_async_copy(ybuf.at[slot], y_ref.at[pl.ds(row0 + j * bm, bm)], sem_out.at[slot])

    @pl.when(nb > 0)
    def _():
        x_copy(0, 0).start()

    wgb[...] = wg_ref[...].astype(BF16)
    wub[...] = wu_ref[...].astype(BF16)
    wdb[...] = wd_ref[...].astype(BF16)
    row = lax.broadcasted_iota(jnp.int32, (bm, xbuf.shape[2]), 0)

    def block(j, carry):
        slot = j % 2
        x_copy(j, slot).wait()

        @pl.when(j + 1 < nb)
        def _():
            x_copy(j + 1, 1 - slot).start()

        @pl.when(j >= 2)
        def _():
            y_copy(j - 2, slot).wait()

        x = _unpack_pairs(jnp.where(row < cnt - j * bm, xbuf[slot], jnp.uint32(0))).astype(BF16)
        g = jnp.dot(x, wgb[...], preferred_element_type=F32)
        u = jnp.dot(x, wub[...], preferred_element_type=F32)
        ybuf[slot] = _pack_pairs(jnp.dot((_silu(g) * u).astype(BF16), wdb[...], preferred_element_type=F32))
        y_copy(j, slot).start()
        return carry

    lax.fori_loop(0, nb, block, 0)

    @pl.when(nb >= 2)
    def _():
        y_copy(nb - 2, nb % 2).wait()

    @pl.when(nb >= 1)
    def _():
        y_copy(nb - 1, (nb - 1) % 2).wait()


def _experts(blk_start, counts, xs, wg, wu, wd):
    n_slots, dp = xs.shape
    n_exp, d, ff = wg.shape
    bm = EXPERT_BLOCK
    wsel = lambda e, st, ct: (e, 0, 0)
    grid_spec = pltpu.PrefetchScalarGridSpec(
        num_scalar_prefetch=2, grid=(n_exp,),
        in_specs=[pl.BlockSpec(memory_space=pl.ANY),
                  pl.BlockSpec((None, d, ff), wsel), pl.BlockSpec((None, d, ff), wsel),
                  pl.BlockSpec((None, ff, d), wsel)],
        out_specs=pl.BlockSpec(memory_space=pl.ANY),
        scratch_shapes=[pltpu.VMEM((d, ff), BF16), pltpu.VMEM((d, ff), BF16), pltpu.VMEM((ff, d), BF16),
                        pltpu.VMEM((2, bm, dp), U32), pltpu.VMEM((2, bm, dp), U32),
                        pltpu.SemaphoreType.DMA((2,)), pltpu.SemaphoreType.DMA((2,))])
    return pl.pallas_call(
        _expert_kernel, out_shape=jax.ShapeDtypeStruct((n_slots, dp), U32), grid_spec=grid_spec,
        compiler_params=_params(1), name="experts",
    )(blk_start, counts, xs, wg, wu, wd)


def _combine_kernel(tile, dest_ref, y_ref, w_ref, h_ref, x1_ref, g2_ref, sg_ref, su_ref, sd_ref, lg_ref, lb_ref,
                    o_ref, buf, sem):
    def issue(t, carry):
        for kk in range(TOP_K):
            _row_copy(y_ref, dest_ref[kk, t], buf.at[kk], t, sem).start(priority=kk % 2)
        return carry

    lax.fori_loop(0, tile, issue, 0)
    hb = _unpack_pairs(h_ref[...]).astype(BF16)
    g = jnp.dot(hb, sg_ref[...], preferred_element_type=F32)
    u = jnp.dot(hb, su_ref[...], preferred_element_type=F32)
    f = jnp.dot((_silu(g) * u).astype(BF16), sd_ref[...], preferred_element_type=F32)

    def drain(t, carry):
        for kk in range(TOP_K):
            _row_copy(y_ref, dest_ref[kk, t], buf.at[kk], t, sem).wait()
        return carry

    lax.fori_loop(0, tile, drain, 0)
    w = w_ref[...]
    for kk in range(TOP_K):
        f = f + w[:, kk:kk + 1] * _unpack_pairs(buf[kk])
    o_ref[...] = _layer_norm(DEEPNORM_ALPHA * x1_ref[...] + g2_ref[0] * f, lg_ref[...], lb_ref[...])


def _combine(dest, y, top_w, h2p, x1, g2, sg, su, sd, lg, lb, tile, tiles_per_batch):
    n, d = x1.shape
    dp = h2p.shape[1]
    row = lambda i: (i, 0)
    full = lambda i: (0, 0)

    def const(a):
        return pl.BlockSpec(a.shape, full, pipeline_mode=pl.Buffered(1))

    return pl.pallas_call(
        functools.partial(_combine_kernel, tile),
        out_shape=jax.ShapeDtypeStruct((n, d), F32),
        grid=(n // tile,),
        in_specs=[pl.BlockSpec((TOP_K, tile), lambda i: (0, i), memory_space=pltpu.SMEM),
                  pl.BlockSpec(memory_space=pl.ANY),
                  pl.BlockSpec((tile, TOP_K), row), pl.BlockSpec((tile, dp), row), pl.BlockSpec((tile, d), row),
                  pl.BlockSpec((1, 1, d), lambda i: (i // tiles_per_batch, 0, 0)),
                  const(sg), const(su), const(sd), const(lg), const(lb)],
        out_specs=pl.BlockSpec((tile, d), row),
        scratch_shapes=[pltpu.VMEM((TOP_K, tile, dp), U32), pltpu.SemaphoreType.DMA],
        compiler_params=_params(1), name="combine_shared_ln2",
    )(dest, y, top_w, h2p, x1, g2, sg, su, sd, lg, lb)


def kernel(x, c, ctx, c_ctx, w_mod, b_mod, w_in, na_rpb, gla_w_decay_f, gla_b_decay_f, gla_w_decay_b, gla_b_decay_b,
           gla_norm_w, w_branch_a, w_branch_b, w_out, ln1_g, ln1_b, router_w, router_bias, exp_w_gate, exp_w_up,
           exp_w_down, sh_w_gate, sh_w_up, sh_w_down, ln2_g, ln2_b):
    batch, t_len, d = x.shape
    ctx_len = ctx.shape[1]
    n = batch * t_len
    assert w_mod.shape[0] == DEPTH == 1
    assert t_len % GLA_CHUNK == 0 and ctx_len == GLA_CHUNK and (t_len // GRID_W) % NA_ROW_BLOCK == 0

    mod_rows = 16
    c_all = jnp.zeros((mod_rows, d), F32).at[:batch].set(c).at[batch].set(c_ctx)
    mod = _modulation(c_all, w_mod[0], b_mod[0])
    sh1, sc1, g1, sh2, sc2, g2 = [mod[:batch, j * d:(j + 1) * d].reshape(batch, 1, d) for j in range(6)]
    sh1c = mod[batch:batch + 1, 0:d].reshape(1, 1, d)
    sc1c = mod[batch:batch + 1, d:2 * d].reshape(1, 1, d)

    offs = np.cumsum((0, NA_WIDTH, NA_WIDTH, NA_WIDTH, GLA_QK_WIDTH, GLA_QK_WIDTH, GLA_V_WIDTH, GLA_V_WIDTH,
                      GLA_GATE_RANK, GLA_GATE_RANK, d, d))
    qa, ka, va, qb, kb, vbc, gbc, lrf, lrb, ga, gbt = [w_in[0][:, offs[j]:offs[j + 1]] for j in range(11)]
    lr_cols = jnp.concatenate([lrf, lrb, jnp.zeros((d, LANES - 2 * GLA_GATE_RANK), F32)], axis=1)
    w_lat = jnp.concatenate([qa, ka, va, qb, kb, vbc, gbc, ga, gbt, lr_cols], axis=1).astype(BF16)
    w_ctx = jnp.concatenate([ka, va, kb, vbc, lr_cols], axis=1).astype(BF16)
    plain = ("plain",)
    lat_plan = ((3 * NA_WIDTH, (("scale", NA_HEAD_DIM ** -0.5), plain, plain)),
                (2 * GLA_QK_WIDTH, (("rope", GLA_DK ** -0.5), ("rope", 1.0))),
                (GLA_V_WIDTH, (plain, plain)), (GLA_V_WIDTH, (plain, plain)),
                (2 * d, (plain,) * 4), (LANES, (plain,)))
    ctx_plan = ((2 * NA_WIDTH, (plain, plain)), (GLA_QK_WIDTH, (plain,)), (GLA_V_WIDTH, (plain, plain)),
                (LANES, (plain,)))
    tile = 256
    x2d = x.reshape(n, d)
    na_qkv, gla_qk, vb, gb, gates, lr = _projection(
        x2d, sc1, sh1, w_lat, lat_plan, (BF16, BF16, BF16, BF16, BF16, F32), tile, t_len // tile,
        rope=_rope_tables(t_len))
    na_kv_c, k_c, v_c, lr_c = _projection(
        ctx.reshape(batch * ctx_len, d), sc1c, sh1c, w_ctx, ctx_plan, (BF16, BF16, BF16, F32), tile,
        batch * ctx_len // tile)

    out_a = _na_attention(na_qkv, na_kv_c, _na_bias_tables(na_rpb[0]), batch, t_len, ctx_len)

    wf, bf = _decay_weights(gla_w_decay_f[0], gla_b_decay_f[0], 0)
    wb, bb = _decay_weights(gla_w_decay_b[0], gla_b_decay_b[0], GLA_GATE_RANK)
    o_f, o_b = _gla(gla_qk, vb, lr, k_c, v_c, lr_c, wf, bf, wb, bb, batch, t_len, ctx_len)

    x1, h2p, scores_t = _merge(
        out_a.reshape(n, NA_WIDTH), o_f.reshape(n, GLA_V_WIDTH), o_b.reshape(n, GLA_V_WIDTH), gb, gates, x2d,
        g1, sc2, sh2, w_branch_a[0].astype(BF16), w_branch_b[0].astype(BF16), w_out[0].astype(BF16),
        gla_norm_w[0].reshape(1, -1), ln1_g[0].reshape(1, d), ln1_b[0].reshape(1, d),
        router_w[0].T.astype(BF16), tile, t_len // tile)

    top_e, top_w, rank, counts = _route(scores_t, router_bias[0], 512)

    counts = counts[:, 0]
    n_blocks = n * TOP_K // EXPERT_BLOCK + N_EXPERTS
    blocks_per = (counts + EXPERT_BLOCK - 1) // EXPERT_BLOCK
    blk_end = jnp.cumsum(blocks_per)
    blk_start = blk_end - blocks_per
    dest = _slots(top_e, rank, (blk_start * EXPERT_BLOCK).astype(F32).reshape(N_EXPERTS, 1), 512)

    xs = _dispatch(dest, h2p, n_blocks * EXPERT_BLOCK, tile)
    y = _experts(blk_start.astype(jnp.int32), counts, xs, exp_w_gate[0], exp_w_up[0], exp_w_down[0])
    out = _combine(dest, y, top_w.T, h2p, x1, g2, sh_w_gate[0].astype(BF16), sh_w_up[0].astype(BF16),
                   sh_w_down[0].astype(BF16), ln2_g[0].reshape(1, d), ln2_b[0].reshape(1, d), tile, t_len // tile)
    return out.reshape(batch, t_len, d)
```

```python
import functools

import numpy as np
import jax
import jax.numpy as jnp
from jax import lax
from jax.experimental import pallas as pl
from jax.experimental.pallas import tpu as pltpu

F32 = jnp.float32
BF16 = jnp.bfloat16
U32 = jnp.uint32
HIGHEST = lax.Precision.HIGHEST

GRID_W = 64
NA_HEADS = 8
NA_HEAD_DIM = 64
NA_WIDTH = NA_HEADS * NA_HEAD_DIM
WIN_ROWS = 8
WIN_COLS = 16
GLA_HEADS = 4
GLA_DK = 128
GLA_DV = 256
GLA_QK_WIDTH = GLA_HEADS * GLA_DK
GLA_V_WIDTH = GLA_HEADS * GLA_DV
GLA_GATE_RANK = 16
GLA_TAU = 16.0
LOG2E = 1.4426950408889634
ROPE_BASE = 10000.0
N_EXPERTS = 256
TOP_K = 8
N_GROUPS = 8
TOPK_GROUPS = 4
GROUP_SIZE = N_EXPERTS // N_GROUPS
ROUTED_SCALE = 2.5
DEPTH = 1
DEEPNORM_ALPHA = (2 * DEPTH) ** 0.25
EPS = 1e-6

LANES = 128
NA_ROW_BLOCK = 4
NA_UNION_ROWS = NA_ROW_BLOCK + WIN_ROWS - 1
GLA_CHUNK = 256
GLA_DIAG = 16
GLA_HEADS_PER_STEP = 4
EXPERT_BLOCK = 256
EXPERT_RING = 4
NEG_BIG = -1e30
VMEM_LIMIT = 56 * 1024 * 1024


def _params(n_axes, vmem=VMEM_LIMIT):
    return pltpu.CompilerParams(dimension_semantics=("arbitrary",) * n_axes, vmem_limit_bytes=vmem)


def _sigmoid(v):
    return 1.0 / (1.0 + jnp.exp(-v))


def _silu(v):
    return v * _sigmoid(v)


def _pack_pairs(v):
    m = v.shape[1] // 2
    lo = lax.bitcast_convert_type(v[:, :m].astype(BF16).astype(F32), U32) >> 16
    hi = lax.bitcast_convert_type(v[:, m:].astype(BF16).astype(F32), U32) & jnp.uint32(0xFFFF0000)
    return lo | hi


def _unpack_pairs(p):
    lo = lax.bitcast_convert_type(p << 16, F32)
    hi = lax.bitcast_convert_type(p & jnp.uint32(0xFFFF0000), F32)
    return jnp.concatenate([lo, hi], axis=1)


def _mod_kernel(c_ref, w_ref, b_ref, o_ref):
    o_ref[...] = jnp.dot(_silu(c_ref[...]), w_ref[...], preferred_element_type=F32, precision=HIGHEST) + b_ref[...]


def _modulation(c_all, w_mod, b_mod):
    rows, d = c_all.shape
    n = w_mod.shape[1]
    bn = 512
    return pl.pallas_call(
        _mod_kernel,
        out_shape=jax.ShapeDtypeStruct((rows, n), F32),
        grid=(n // bn,),
        in_specs=[pl.BlockSpec((rows, d), lambda j: (0, 0)),
                  pl.BlockSpec((d, bn), lambda j: (0, j)),
                  pl.BlockSpec((1, bn), lambda j: (0, j))],
        out_specs=pl.BlockSpec((rows, bn), lambda j: (0, j)),
        compiler_params=_params(1),
        name="modulation",
    )(c_all, w_mod, b_mod.reshape(1, n))


def _swap32(v):
    lane = lax.broadcasted_iota(jnp.int32, v.shape, 1)
    return jnp.where((lane % 64) < 32, pltpu.roll(v, 96, 1), pltpu.roll(v, 32, 1))


def _proj_kernel(plan, has_rope, *refs):
    x_ref, sc_ref, sh_ref, w_ref = refs[:4]
    pos = 4
    if has_rope:
        cos_ref, sin_ref = refs[4:6]
        pos = 6
    out_refs = refs[pos:]
    h = (x_ref[...] * (1.0 + sc_ref[0]) + sh_ref[0]).astype(BF16)
    col = 0
    for out_ref, (width, kinds) in zip(out_refs, plan):
        for j, kind in enumerate(kinds):
            cw = width // len(kinds)
            c0 = j * cw
            acc = jnp.dot(h, w_ref[:, col + c0:col + c0 + cw], preferred_element_type=F32)
            if kind[0] == "scale":
                acc = acc * kind[1]
            elif kind[0] == "rope":
                cos, sin = cos_ref[...], sin_ref[...]
                pieces = []
                for p in range(cw // LANES):
                    v = acc[:, p * LANES:(p + 1) * LANES]
                    pieces.append((v * cos + _swap32(v) * sin) * kind[1])
                acc = jnp.concatenate(pieces, axis=1)
            out_ref[:, c0:c0 + cw] = acc.astype(out_ref.dtype)
        col += width


def _projection(x2d, sc, sh, w, plan, out_dtypes, tile, tiles_per_mod, rope=None):
    n, d = x2d.shape
    in_specs = [pl.BlockSpec((tile, d), lambda i: (i, 0)),
                pl.BlockSpec((1, 1, d), lambda i: (i // tiles_per_mod, 0, 0)),
                pl.BlockSpec((1, 1, d), lambda i: (i // tiles_per_mod, 0, 0)),
                pl.BlockSpec(w.shape, lambda i: (0, 0), pipeline_mode=pl.Buffered(1))]
    args = [x2d, sc, sh, w]
    if rope is not None:
        in_specs += [pl.BlockSpec((tile, LANES), lambda i: (i % tiles_per_mod, 0))] * 2
        args += list(rope)
    out_shape = [jax.ShapeDtypeStruct((n, wd), dt) for (wd, _), dt in zip(plan, out_dtypes)]
    out_specs = [pl.BlockSpec((tile, wd), lambda i: (i, 0)) for (wd, _) in plan]
    return pl.pallas_call(
        functools.partial(_proj_kernel, plan, rope is not None),
        out_shape=out_shape,
        grid=(n // tile,),
        in_specs=in_specs,
        out_specs=out_specs,
        compiler_params=_params(1),
        name="in_proj" if rope is not None else "ctx_proj",
    )(*args)


def _rope_tables(t_len):
    half = GLA_DK // 2
    quarter = half // 2
    inv_freq = ROPE_BASE ** (-jnp.arange(quarter, dtype=F32) / quarter)
    pos = jnp.arange(t_len)
    row_ang = (pos // GRID_W).astype(F32)[:, None] * inv_freq[None, :]
    col_ang = (pos % GRID_W).astype(F32)[:, None] * inv_freq[None, :]
    cr, sr, cc, sn = jnp.cos(row_ang), jnp.sin(row_ang), jnp.cos(col_ang), jnp.sin(col_ang)
    cos = jnp.concatenate([cr, cr, cc, cc], axis=1)
    sin = jnp.concatenate([-sr, sr, -sn, sn], axis=1)
    return cos, sin


def _na_bias_tables(rpb):
    rb, ur, w = NA_ROW_BLOCK, NA_UNION_ROWS, GRID_W
    heads = rpb.shape[0]
    pad = jnp.pad(rpb, ((0, 0), (0, 0), (w, w)))
    toep = jnp.stack([pad[:, :, w + WIN_COLS - 1 - c:2 * w + WIN_COLS - 1 - c] for c in range(w)], axis=2)
    c = np.arange(w)[:, None]
    kc = np.arange(w)[None, :]
    col_start = np.clip(c - WIN_COLS // 2, 0, w - WIN_COLS)
    col_ok = (kc >= col_start) & (kc < col_start + WIN_COLS)
    toep = jnp.where(col_ok[None, None], toep, NEG_BIG)
    neg = jnp.full((heads, w, w), NEG_BIG, F32)
    half = WIN_ROWS // 2
    tables = []
    for lo, off in ((lambda i: 0, WIN_ROWS - 1), (lambda i: i, WIN_ROWS - 1 - half), (lambda i: ur - WIN_ROWS, 0)):
        rows_ = []
        for i in range(rb):
            blocks = [toep[:, j - i + off] if lo(i) <= j < lo(i) + WIN_ROWS else neg for j in range(ur)]
            rows_.append(jnp.concatenate(blocks, axis=2))
        tables.append(jnp.concatenate(rows_, axis=1))
    return jnp.stack(tables)


def _na_kernel(rows, q_ref, k_ref, v_ref, kc_ref, vc_ref, bias_ref, o_ref):
    rb = pl.program_id(1)
    ustart = jnp.clip(rb * NA_ROW_BLOCK - WIN_ROWS // 2, 0, rows - NA_UNION_ROWS)
    k0 = pl.multiple_of(ustart * GRID_W, GRID_W)
    nk = NA_UNION_ROWS * GRID_W
    nt = (((1,), (1,)), ((), ()))
    lane = lax.broadcasted_iota(jnp.int32, (NA_ROW_BLOCK * GRID_W, LANES), 1)
    for p in range(NA_WIDTH // LANES):
        ls = slice(p * LANES, (p + 1) * LANES)
        q = q_ref[:, ls]
        ku = k_ref[pl.ds(k0, nk), ls]
        vu = v_ref[pl.ds(k0, nk), ls]
        kc = kc_ref[:, ls]
        vc = vc_ref[:, ls]
        out = jnp.zeros(q.shape, F32)
        for hh in range(LANES // NA_HEAD_DIM):
            mine = (lane // NA_HEAD_DIM) == hh
            qm = jnp.where(mine, q, jnp.zeros_like(q))
            s_loc = lax.dot_general(qm, ku, nt, preferred_element_type=F32) + bias_ref[p * 2 + hh]
            s_ctx = lax.dot_general(qm, kc, nt, preferred_element_type=F32)
            m = jnp.maximum(jnp.max(s_loc, axis=1, keepdims=True), jnp.max(s_ctx, axis=1, keepdims=True))
            p_loc = jnp.exp(s_loc - m)
            p_ctx = jnp.exp(s_ctx - m)
            denom = jnp.sum(p_loc, axis=1, keepdims=True) + jnp.sum(p_ctx, axis=1, keepdims=True)
            o = (jnp.dot(p_loc.astype(BF16), vu, preferred_element_type=F32)
                 + jnp.dot(p_ctx.astype(BF16), vc, preferred_element_type=F32)) / denom
            out = jnp.where(mine, o, out)
        o_ref[:, ls] = out.astype(o_ref.dtype)


def _na_attention(na_qkv, na_kv_ctx, bias_tables, batch, t_len, ctx_len):
    rows = t_len // GRID_W
    n_rb = rows // NA_ROW_BLOCK
    tq = NA_ROW_BLOCK * GRID_W
    qkv = na_qkv.reshape(batch, t_len, 3 * NA_WIDTH)
    kvc = na_kv_ctx.reshape(batch, ctx_len, 2 * NA_WIDTH)

    def bias_idx(b, r):
        return (jnp.where(r == 0, 0, jnp.where(r == n_rb - 1, 2, 1)), 0, 0, 0)

    return pl.pallas_call(
        functools.partial(_na_kernel, rows),
        out_shape=jax.ShapeDtypeStruct((batch, t_len, NA_WIDTH), BF16),
        grid=(batch, n_rb),
        in_specs=[pl.BlockSpec((None, tq, NA_WIDTH), lambda b, r: (b, r, 0)),
                  pl.BlockSpec((None, t_len, NA_WIDTH), lambda b, r: (b, 0, 1)),
                  pl.BlockSpec((None, t_len, NA_WIDTH), lambda b, r: (b, 0, 2)),
                  pl.BlockSpec((None, ctx_len, NA_WIDTH), lambda b, r: (b, 0, 0)),
                  pl.BlockSpec((None, ctx_len, NA_WIDTH), lambda b, r: (b, 0, 1)),
                  pl.BlockSpec((None,) + bias_tables.shape[1:], bias_idx)],
        out_specs=pl.BlockSpec((None, tq, NA_WIDTH), lambda b, r: (b, r, 0)),
        compiler_params=_params(2),
        name="na_attention",
    )(qkv, qkv, qkv, kvc, kvc, bias_tables)


def _log2_decay(lr, w2, b2):
    z = (jnp.dot(lr.astype(BF16), w2.astype(BF16), preferred_element_type=F32) + b2) * LOG2E
    return (jnp.minimum(z, 0.0) - jnp.log2(1.0 + jnp.exp2(-jnp.abs(z)))) * (1.0 / GLA_TAU)


def _gla_constants(c):
    tris, masks = [], []
    for reverse in (False, True):
        i = np.arange(c)[:, None]
        j = np.arange(c)[None, :]
        tris.append((j >= i) if reverse else (j <= i))
        i = np.arange(c // 2)[:, None]
        j = np.arange(c // 2)[None, :]
        if reverse:
            i, j = j, i
        level = []
        s = c // 4
        while s >= GLA_DIAG:
            level.append(((i // (2 * s)) == (j // (2 * s))) & ((i % (2 * s)) >= s) & ((j % (2 * s)) < s))
            s //= 2
        level.append(((i // GLA_DIAG) == (j // GLA_DIAG)) & (j <= i))
        masks.append(np.stack(level))
    return jnp.asarray(np.stack(tris), BF16), jnp.asarray(np.stack(masks), F32)


def _block_refs(cum, s, reverse, diag):
    c = cum.shape[0]
    span = s if diag else 2 * s
    parts = []
    for p in range(c // span):
        if diag:
            r = p * span + (span - 1 if reverse else 0)
        else:
            r = p * span + (s - 1 if reverse else s)
        parts.append(jnp.broadcast_to(cum[r:r + 1, :], (span, cum.shape[1])))
    return jnp.concatenate(parts, axis=0)


def _cumsum_rows(a, tri):
    hi = a.astype(BF16)
    rest = a - hi.astype(F32)
    mid = rest.astype(BF16)
    lo = (rest - mid.astype(F32)).astype(BF16)
    parts = jnp.dot(tri, jnp.concatenate([hi, mid, lo], axis=1), preferred_element_type=F32)
    w = a.shape[1]
    return parts[:, :w] + parts[:, w:2 * w] + parts[:, 2 * w:]


def _gla_chunk(q, k, v, a, state_t, tri, mask_ref, reverse, want_out):
    c = k.shape[0]
    hc = c // 2
    nt = (((1,), (1,)), ((), ()))
    cum = _cumsum_rows(a, tri)
    last = 0 if reverse else c - 1
    total = cum[last:last + 1, :]
    out = None
    if want_out:
        q_in = (q * jnp.exp2(cum)).astype(BF16)
        out = lax.dot_general(q_in, state_t.astype(BF16), nt, preferred_element_type=F32)
        halves = (slice(hc, c), slice(0, hc)) if reverse else (slice(0, hc), slice(hc, c))
        early, late = halves

        def scaled(s, diag):
            d = cum - _block_refs(cum, GLA_DIAG if diag else s, reverse, diag)
            if diag:
                return (q * jnp.exp2(d)).astype(BF16), (k * jnp.exp2(-d)).astype(BF16)
            e = jnp.exp2(-jnp.abs(d))
            return (q * e).astype(BF16), (k * e).astype(BF16)

        qs, ks = scaled(hc, False)
        cross = lax.dot_general(qs[late], ks[early], nt, preferred_element_type=F32)
        inner = [jnp.zeros((hc, hc), F32), jnp.zeros((hc, hc), F32)]
        s = hc // 2
        level = 0
        while True:
            diag = s < GLA_DIAG
            qs, ks = scaled(s, diag)
            for x, rows in enumerate(halves):
                att = lax.dot_general(qs[rows], ks[rows], nt, preferred_element_type=F32)
                inner[x] = inner[x] + att * mask_ref[level]
            if diag:
                break
            s //= 2
            level += 1
        out_early = jnp.dot(inner[0].astype(BF16), v[early], preferred_element_type=F32)
        out_late = jnp.dot(jnp.concatenate([cross, inner[1]], axis=1).astype(BF16),
                           jnp.concatenate([v[early], v[late]], axis=0), preferred_element_type=F32)
        intra = [out_late, out_early] if reverse else [out_early, out_late]
        out = out + jnp.concatenate(intra, axis=0)
    k_out = (k * jnp.exp2(total - cum)).astype(BF16)
    upd = lax.dot_general(v, k_out, (((0,), (0,)), ((), ())), preferred_element_type=F32)
    return out, jnp.exp2(total) * state_t + upd


def _gla_kernel(qf_ref, kf_ref, vf_ref, lrf_ref, qb_ref, kb_ref, vb_ref, lrb_ref,
                kc_ref, vc_ref, lrc_ref, wf_ref, bf_ref, wb_ref, bb_ref, tri_ref, mask_ref,
                of_ref, ob_ref, sf_ref, sb_ref):
    c = pl.program_id(2)
    fm, bm = mask_ref.at[0], mask_ref.at[1]
    heads = sf_ref.shape[0]

    @pl.when(c == 0)
    def _():
        lrc = lrc_ref[...]
        zero = jnp.zeros(sf_ref.shape[1:], F32)
        for h in range(heads):
            kc = kc_ref[:, h * GLA_DK:(h + 1) * GLA_DK].astype(F32)
            vc = vc_ref[:, h * GLA_DV:(h + 1) * GLA_DV]
            _, sf_ref[h] = _gla_chunk(None, kc, vc, _log2_decay(lrc, wf_ref[h], bf_ref[h]), zero,
                                      tri_ref[0], fm, False, False)
            _, sb_ref[h] = _gla_chunk(None, kc, vc, _log2_decay(lrc, wb_ref[h], bb_ref[h]), zero,
                                      tri_ref[1], bm, True, False)

    @pl.when(c > 0)
    def _():
        lrf, lrb = lrf_ref[...], lrb_ref[...]
        for h in range(heads):
            ks = slice(h * GLA_DK, (h + 1) * GLA_DK)
            vs = slice(h * GLA_DV, (h + 1) * GLA_DV)
            o, sf_ref[h] = _gla_chunk(qf_ref[:, ks].astype(F32), kf_ref[:, ks].astype(F32), vf_ref[:, vs],
                                      _log2_decay(lrf, wf_ref[h], bf_ref[h]), sf_ref[h],
                                      tri_ref[0], fm, False, True)
            of_ref[:, vs] = o.astype(of_ref.dtype)
            o, sb_ref[h] = _gla_chunk(qb_ref[:, ks].astype(F32), kb_ref[:, ks].astype(F32), vb_ref[:, vs],
                                      _log2_decay(lrb, wb_ref[h], bb_ref[h]), sb_ref[h],
                                      tri_ref[1], bm, True, True)
            ob_ref[:, vs] = o.astype(ob_ref.dtype)


def _gla(gla_qk, vb, lr, k_ctx, v_ctx, lr_ctx, wf, bf, wb, bb, batch, t_len, ctx_len):
    nc = t_len // GLA_CHUNK
    h = GLA_HEADS
    qk = gla_qk.reshape(batch, t_len, 2 * GLA_QK_WIDTH)
    v3 = vb.reshape(batch, t_len, GLA_V_WIDTH)
    lr3 = lr.reshape(batch, t_len, LANES)
    kc3 = k_ctx.reshape(batch, ctx_len, GLA_QK_WIDTH)
    vc3 = v_ctx.reshape(batch, ctx_len, GLA_V_WIDTH)
    lrc3 = lr_ctx.reshape(batch, ctx_len, LANES)
    tri, masks = _gla_constants(GLA_CHUNK)

    def fwd(c):
        return jnp.maximum(c - 1, 0)

    def bwd(c):
        return nc - 1 - jnp.maximum(c - 1, 0)

    def const(a):
        return pl.BlockSpec(a.shape, lambda b, hh, c: (0,) * a.ndim, pipeline_mode=pl.Buffered(1))

    hp = GLA_HEADS_PER_STEP
    groups = h // hp
    cq = (None, GLA_CHUNK, hp * GLA_DK)
    cv = (None, GLA_CHUNK, hp * GLA_DV)
    cl = (None, GLA_CHUNK, LANES)
    in_specs = [
        pl.BlockSpec(cq, lambda b, g, c: (b, fwd(c), g)),
        pl.BlockSpec(cq, lambda b, g, c: (b, fwd(c), groups + g)),
        pl.BlockSpec(cv, lambda b, g, c: (b, fwd(c), g)),
        pl.BlockSpec(cl, lambda b, g, c: (b, fwd(c), 0)),
        pl.BlockSpec(cq, lambda b, g, c: (b, bwd(c), g)),
        pl.BlockSpec(cq, lambda b, g, c: (b, bwd(c), groups + g)),
        pl.BlockSpec(cv, lambda b, g, c: (b, bwd(c), g)),
        pl.BlockSpec(cl, lambda b, g, c: (b, bwd(c), 0)),
        pl.BlockSpec((None, ctx_len, hp * GLA_DK), lambda b, g, c: (b, 0, g)),
        pl.BlockSpec((None, ctx_len, hp * GLA_DV), lambda b, g, c: (b, 0, g)),
        pl.BlockSpec((None, ctx_len, LANES), lambda b, g, c: (b, 0, 0)),
        pl.BlockSpec((hp, LANES, GLA_DK), lambda b, g, c: (g, 0, 0)),
        pl.BlockSpec((hp, 1, GLA_DK), lambda b, g, c: (g, 0, 0)),
        pl.BlockSpec((hp, LANES, GLA_DK), lambda b, g, c: (g, 0, 0)),
        pl.BlockSpec((hp, 1, GLA_DK), lambda b, g, c: (g, 0, 0)),
        const(tri), const(masks),
    ]
    out_specs = [pl.BlockSpec(cv, lambda b, g, c: (b, fwd(c), g)),
                 pl.BlockSpec(cv, lambda b, g, c: (b, bwd(c), g))]
    out_shape = [jax.ShapeDtypeStruct((batch, t_len, GLA_V_WIDTH), BF16)] * 2
    return pl.pallas_call(
        _gla_kernel,
        out_shape=out_shape,
        grid=(batch, groups, nc + 1),
        in_specs=in_specs,
        out_specs=out_specs,
        scratch_shapes=[pltpu.VMEM((hp, GLA_DV, GLA_DK), F32), pltpu.VMEM((hp, GLA_DV, GLA_DK), F32)],
        compiler_params=_params(3),
        name="gla",
    )(qk, qk, v3, lr3, qk, qk, v3, lr3, kc3, vc3, lrc3, wf, bf, wb, bb, tri, masks)


def _decay_weights(w_dec, b_dec, row0):
    w = w_dec.reshape(GLA_GATE_RANK, GLA_HEADS, GLA_DK).transpose(1, 0, 2)
    wp = jnp.zeros((GLA_HEADS, LANES, GLA_DK), F32).at[:, row0:row0 + GLA_GATE_RANK, :].set(w)
    return wp, b_dec.reshape(GLA_HEADS, 1, GLA_DK)


def _layer_norm(v, g, b):
    mu = jnp.mean(v, axis=1, keepdims=True)
    var = jnp.mean(jnp.square(v - mu), axis=1, keepdims=True)
    return (v - mu) * lax.rsqrt(var + EPS) * g + b


def _merge_kernel(oa_ref, of_ref, ob_ref, gb_ref, gates_ref, x_ref, g1_ref, sc2_ref, sh2_ref,
                  wa_ref, wb_ref, wo_ref, nw_ref, lg_ref, lb_ref, rw_ref,
                  x1_ref, h2_ref, sc_ref):
    d = x_ref.shape[1]
    o = of_ref[...].astype(F32) + ob_ref[...].astype(F32)
    pieces = []
    for hh in range(GLA_HEADS):
        oh = o[:, hh * GLA_DV:(hh + 1) * GLA_DV]
        pieces.append(oh * lax.rsqrt(jnp.mean(jnp.square(oh), axis=1, keepdims=True) + EPS))
    out_b = jnp.concatenate(pieces, axis=1) * nw_ref[...] * _silu(gb_ref[...].astype(F32))
    ya = jnp.dot(oa_ref[...], wa_ref[...], preferred_element_type=F32)
    yb = jnp.dot(out_b.astype(BF16), wb_ref[...], preferred_element_type=F32)
    y = _sigmoid(gates_ref[:, :d].astype(F32)) * ya + _sigmoid(gates_ref[:, d:].astype(F32)) * yb
    y2 = jnp.dot(y.astype(BF16), wo_ref[...], preferred_element_type=F32)
    x1 = _layer_norm(DEEPNORM_ALPHA * x_ref[...] + g1_ref[0] * y2, lg_ref[...], lb_ref[...])
    x1_ref[...] = x1
    h2 = x1 * (1.0 + sc2_ref[0]) + sh2_ref[0]
    h2_ref[...] = _pack_pairs(h2)
    logits_t = lax.dot_general(rw_ref[...], h2.astype(BF16), (((1,), (1,)), ((), ())), preferred_element_type=F32)
    sc_ref[...] = _sigmoid(logits_t)


def _merge(out_a, o_f, o_b, gb, gates, x2d, g1, sc2, sh2, wa, wb, wo, nw, lg, lb, rw_t, tile, tiles_per_batch):
    n, d = x2d.shape
    row = lambda i: (i, 0)
    mod = lambda i: (i // tiles_per_batch, 0, 0)
    full = lambda i: (0, 0)

    def const(a):
        return pl.BlockSpec(a.shape, full, pipeline_mode=pl.Buffered(1))

    in_specs = [pl.BlockSpec((tile, NA_WIDTH), row), pl.BlockSpec((tile, GLA_V_WIDTH), row),
                pl.BlockSpec((tile, GLA_V_WIDTH), row), pl.BlockSpec((tile, GLA_V_WIDTH), row),
                pl.BlockSpec((tile, 2 * d), row), pl.BlockSpec((tile, d), row),
                pl.BlockSpec((1, 1, d), mod), pl.BlockSpec((1, 1, d), mod), pl.BlockSpec((1, 1, d), mod),
                const(wa), const(wb), const(wo), const(nw), const(lg), const(lb), const(rw_t)]
    out_shape = [jax.ShapeDtypeStruct((n, d), F32), jax.ShapeDtypeStruct((n, d // 2), U32),
                 jax.ShapeDtypeStruct((N_EXPERTS, n), F32)]
    out_specs = [pl.BlockSpec((tile, d), row), pl.BlockSpec((tile, d // 2), row),
                 pl.BlockSpec((N_EXPERTS, tile), lambda i: (0, i))]
    return pl.pallas_call(
        _merge_kernel, out_shape=out_shape, grid=(n // tile,), in_specs=in_specs, out_specs=out_specs,
        compiler_params=_params(1), name="merge_ln1_router",
    )(out_a, o_f, o_b, gb, gates, x2d, g1, sc2, sh2, wa, wb, wo, nw, lg, lb, rw_t)


def _first_argmax(vals, idx, n):
    m = jnp.max(vals, axis=0, keepdims=True)
    first = jnp.min(jnp.where(vals == m, idx, float(n)), axis=0, keepdims=True)
    return m, first


def _route_kernel(sc_ref, bias_ref, e_ref, w_ref, rank_ref, cnt_ref, carry_ref):
    step = pl.program_id(0)
    tr = sc_ref.shape[1]

    @pl.when(step == 0)
    def _():
        carry_ref[...] = jnp.zeros(carry_ref.shape, F32)

    scores = sc_ref[...]
    biased = scores + bias_ref[...]
    eidx = lax.broadcasted_iota(jnp.int32, (N_EXPERTS, tr), 0).astype(F32)
    lidx = lax.broadcasted_iota(jnp.int32, (GROUP_SIZE, tr), 0).astype(F32)
    gidx = lax.broadcasted_iota(jnp.int32, (N_GROUPS, tr), 0).astype(F32)
    gs = []
    for g in range(N_GROUPS):
        blk = biased[g * GROUP_SIZE:(g + 1) * GROUP_SIZE]
        m1, first = _first_argmax(blk, lidx, GROUP_SIZE)
        m2 = jnp.max(jnp.where(lidx == first, -jnp.inf, blk), axis=0, keepdims=True)
        gs.append(m1 + m2)
    cur = jnp.concatenate(gs, axis=0)
    keep = jnp.zeros((N_GROUPS, tr), F32)
    for _ in range(TOPK_GROUPS):
        _, first = _first_argmax(cur, gidx, N_GROUPS)
        sel = gidx == first
        keep = jnp.where(sel, 1.0, keep)
        cur = jnp.where(sel, -jnp.inf, cur)
    keep_e = jnp.concatenate([jnp.broadcast_to(keep[g:g + 1], (GROUP_SIZE, tr)) for g in range(N_GROUPS)], axis=0)
    masked = jnp.where(keep_e > 0.5, biased, -jnp.inf)
    chosen = jnp.zeros((N_EXPERTS, tr), F32)
    tops, topi = [], []
    for _ in range(TOP_K):
        _, first = _first_argmax(masked, eidx, N_EXPERTS)
        sel = eidx == first
        tops.append(jnp.sum(jnp.where(sel, scores, 0.0), axis=0, keepdims=True))
        topi.append(first)
        chosen = jnp.where(sel, 1.0, chosen)
        masked = jnp.where(sel, -jnp.inf, masked)
    top_s = jnp.concatenate(tops, axis=0)
    top_i = jnp.concatenate(topi, axis=0)
    e_ref[...] = top_i.astype(jnp.int32)
    w_ref[...] = top_s / jnp.sum(top_s, axis=0, keepdims=True) * ROUTED_SCALE
    r = lax.broadcasted_iota(jnp.int32, (tr, tr), 0)
    cidx = lax.broadcasted_iota(jnp.int32, (tr, tr), 1)
    before = jnp.where(r < cidx, 1.0, 0.0).astype(BF16)
    prior = jnp.dot(chosen.astype(BF16), before, preferred_element_type=F32) + carry_ref[...]
    ranks = [jnp.sum(jnp.where(eidx == topi[kk], prior, 0.0), axis=0, keepdims=True) for kk in range(TOP_K)]
    rank_ref[...] = jnp.concatenate(ranks, axis=0).astype(jnp.int32)
    carry_ref[...] = carry_ref[...] + jnp.sum(chosen, axis=1, keepdims=True)
    cnt_ref[...] = jnp.broadcast_to(carry_ref[...], cnt_ref.shape).astype(jnp.int32)


def _route(scores_t, router_bias, tile):
    n = scores_t.shape[1]
    col = lambda i: (0, i)
    out_shape = [jax.ShapeDtypeStruct((TOP_K, n), jnp.int32), jax.ShapeDtypeStruct((TOP_K, n), F32),
                 jax.ShapeDtypeStruct((TOP_K, n), jnp.int32), jax.ShapeDtypeStruct((N_EXPERTS, LANES), jnp.int32)]
    return pl.pallas_call(
        _route_kernel, out_shape=out_shape, grid=(n // tile,),
        in_specs=[pl.BlockSpec((N_EXPERTS, tile), col), pl.BlockSpec((N_EXPERTS, 1), lambda i: (0, 0))],
        out_specs=[pl.BlockSpec((TOP_K, tile), col), pl.BlockSpec((TOP_K, tile), col),
                   pl.BlockSpec((TOP_K, tile), col), pl.BlockSpec((N_EXPERTS, LANES), lambda i: (0, 0))],
        scratch_shapes=[pltpu.VMEM((N_EXPERTS, 1), F32)],
        compiler_params=_params(1), name="route",
    )(scores_t, router_bias.reshape(N_EXPERTS, 1))


def _slots_kernel(e_ref, rank_ref, start_ref, dest_ref):
    tr = e_ref.shape[1]
    eidx = lax.broadcasted_iota(jnp.int32, (N_EXPERTS, tr), 0)
    e = e_ref[...]
    start = start_ref[...]
    rows = [jnp.sum(jnp.where(eidx == e[kk:kk + 1], start, 0.0), axis=0, keepdims=True) for kk in range(TOP_K)]
    dest_ref[...] = jnp.concatenate(rows, axis=0).astype(jnp.int32) + rank_ref[...]


def _slots(top_e, rank, start_rows, tile):
    n = top_e.shape[1]
    col = lambda i: (0, i)
    return pl.pallas_call(
        _slots_kernel, out_shape=jax.ShapeDtypeStruct((TOP_K, n), jnp.int32), grid=(n // tile,),
        in_specs=[pl.BlockSpec((TOP_K, tile), col), pl.BlockSpec((TOP_K, tile), col),
                  pl.BlockSpec((N_EXPERTS, 1), lambda i: (0, 0))],
        out_specs=pl.BlockSpec((TOP_K, tile), col),
        compiler_params=_params(1), name="slots",
    )(top_e, rank, start_rows)


def _row_copy(src, s, dst, t, sem):
    return pltpu.make_async_copy(src.at[pl.ds(s, 1)], dst.at[pl.ds(t, 1)], sem)


def _dispatch_kernel(tile, dest_ref, h_ref, xs_ref, sem):
    def issue(t, carry):
        for kk in range(TOP_K):
            _row_copy(h_ref, t, xs_ref, dest_ref[kk, t], sem).start(priority=kk % 2)
        return carry

    lax.fori_loop(0, tile, issue, 0)

    def drain(t, carry):
        for kk in range(TOP_K):
            _row_copy(h_ref, t, xs_ref, dest_ref[kk, t], sem).wait()
        return carry

    lax.fori_loop(0, tile, drain, 0)


def _dispatch(dest, h2p, n_slots, tile):
    n, dp = h2p.shape
    return pl.pallas_call(
        functools.partial(_dispatch_kernel, tile),
        out_shape=jax.ShapeDtypeStruct((n_slots, dp), h2p.dtype),
        grid=(n // tile,),
        in_specs=[pl.BlockSpec((TOP_K, tile), lambda i: (0, i), memory_space=pltpu.SMEM),
                  pl.BlockSpec((tile, dp), lambda i: (i, 0))],
        out_specs=pl.BlockSpec(memory_space=pl.ANY),
        scratch_shapes=[pltpu.SemaphoreType.DMA],
        compiler_params=_params(1), name="dispatch",
    )(dest, h2p)


def _expert_kernel(start_ref, cnt_ref, nused_ref, xs_ref, wg_ref, wu_ref, wd_ref, y_ref,
                   wgb, wub, wdb, xbuf, ybuf, sem_in, sem_out):
    e = pl.program_id(0)
    bm, ring = EXPERT_BLOCK, EXPERT_RING
    cnt = cnt_ref[e]
    nb = (cnt + bm - 1) // bm
    g0 = start_ref[e]
    n_used = nused_ref[0]

    def x_copy(g):
        slot = g % ring
        return pltpu.make_async_copy(xs_ref.at[pl.ds(g * bm, bm)], xbuf.at[slot], sem_in.at[slot])

    def y_copy(g):
        slot = g % ring
        return pltpu.make_async_copy(ybuf.at[slot], y_ref.at[pl.ds(g * bm, bm)], sem_out.at[slot])

    @pl.when(e == 0)
    def _():
        for g in range(ring):
            @pl.when(g < n_used)
            def _():
                x_copy(g).start()

    wgb[...] = wg_ref[...].astype(BF16)
    wub[...] = wu_ref[...].astype(BF16)
    wdb[...] = wd_ref[...].astype(BF16)
    row = lax.broadcasted_iota(jnp.int32, (bm, xbuf.shape[2]), 0)

    def block(j, carry):
        g = g0 + j
        slot = g % ring
        x_copy(g).wait()

        @pl.when(g >= ring)
        def _():
            y_copy(g - ring).wait()

        x = _unpack_pairs(jnp.where(row < cnt - j * bm, xbuf[slot], jnp.uint32(0))).astype(BF16)
        gate = jnp.dot(x, wgb[...], preferred_element_type=F32)
        up = jnp.dot(x, wub[...], preferred_element_type=F32)
        ybuf[slot] = _pack_pairs(jnp.dot((_silu(gate) * up).astype(BF16), wdb[...], preferred_element_type=F32))
        y_copy(g).start()

        @pl.when(g + ring < n_used)
        def _():
            x_copy(g + ring).start()
        return carry

    lax.fori_loop(0, nb, block, 0)

    @pl.when(e == pl.num_programs(0) - 1)
    def _():
        for back in range(1, ring + 1):
            @pl.when(n_used - back >= 0)
            def _():
                y_copy(n_used - back).wait()


def _experts(blk_start, counts, n_used, xs, wg, wu, wd):
    n_slots, dp = xs.shape
    n_exp, d, ff = wg.shape
    bm, ring = EXPERT_BLOCK, EXPERT_RING
    wsel = lambda e, st, ct, nu: (e, 0, 0)
    grid_spec = pltpu.PrefetchScalarGridSpec(
        num_scalar_prefetch=3, grid=(n_exp,),
        in_specs=[pl.BlockSpec(memory_space=pl.ANY),
                  pl.BlockSpec((None, d, ff), wsel), pl.BlockSpec((None, d, ff), wsel),
                  pl.BlockSpec((None, ff, d), wsel)],
        out_specs=pl.BlockSpec(memory_space=pl.ANY),
        scratch_shapes=[pltpu.VMEM((d, ff), BF16), pltpu.VMEM((d, ff), BF16), pltpu.VMEM((ff, d), BF16),
                        pltpu.VMEM((ring, bm, dp), U32), pltpu.VMEM((ring, bm, dp), U32),
                        pltpu.SemaphoreType.DMA((ring,)), pltpu.SemaphoreType.DMA((ring,))])
    return pl.pallas_call(
        _expert_kernel, out_shape=jax.ShapeDtypeStruct((n_slots, dp), U32), grid_spec=grid_spec,
        compiler_params=_params(1), name="experts",
    )(blk_start, counts, n_used, xs, wg, wu, wd)


def _combine_kernel(tile, dest_ref, next_ref, y_ref, w_ref, h_ref, x1_ref, g2_ref, sg_ref, su_ref, sd_ref,
                    lg_ref, lb_ref, o_ref, buf, sem):
    i = pl.program_id(0)
    cur = i % 2

    def gather(idx_ref, slot, wait):
        def body(t, carry):
            for kk in range(TOP_K):
                cp = _row_copy(y_ref, idx_ref[kk, t], buf.at[slot, kk], t, sem.at[slot])
                if wait:
                    cp.wait()
                else:
                    cp.start(priority=kk % 2)
            return carry

        lax.fori_loop(0, tile, body, 0)

    @pl.when(i == 0)
    def _():
        gather(dest_ref, 0, False)

    @pl.when(i + 1 < pl.num_programs(0))
    def _():
        gather(next_ref, 1 - cur, False)

    hb = _unpack_pairs(h_ref[...]).astype(BF16)
    g = jnp.dot(hb, sg_ref[...], preferred_element_type=F32)
    u = jnp.dot(hb, su_ref[...], preferred_element_type=F32)
    f = jnp.dot((_silu(g) * u).astype(BF16), sd_ref[...], preferred_element_type=F32)
    gather(dest_ref, cur, True)
    w = w_ref[...]
    for kk in range(TOP_K):
        f = f + w[:, kk:kk + 1] * _unpack_pairs(buf[cur, kk])
    o_ref[...] = _layer_norm(DEEPNORM_ALPHA * x1_ref[...] + g2_ref[0] * f, lg_ref[...], lb_ref[...])


def _combine(dest, y, top_w, h2p, x1, g2, sg, su, sd, lg, lb, tile, tiles_per_batch):
    n, d = x1.shape
    dp = h2p.shape[1]
    row = lambda i: (i, 0)
    full = lambda i: (0, 0)

    def const(a):
        return pl.BlockSpec(a.shape, full, pipeline_mode=pl.Buffered(1))

    steps = n // tile
    return pl.pallas_call(
        functools.partial(_combine_kernel, tile),
        out_shape=jax.ShapeDtypeStruct((n, d), F32),
        grid=(steps,),
        in_specs=[pl.BlockSpec((TOP_K, tile), lambda i: (0, i), memory_space=pltpu.SMEM),
                  pl.BlockSpec((TOP_K, tile), lambda i: (0, jnp.minimum(i + 1, steps - 1)), memory_space=pltpu.SMEM),
                  pl.BlockSpec(memory_space=pl.ANY),
                  pl.BlockSpec((tile, TOP_K), row), pl.BlockSpec((tile, dp), row), pl.BlockSpec((tile, d), row),
                  pl.BlockSpec((1, 1, d), lambda i: (i // tiles_per_batch, 0, 0)),
                  const(sg), const(su), const(sd), const(lg), const(lb)],
        out_specs=pl.BlockSpec((tile, d), row),
        scratch_shapes=[pltpu.VMEM((2, TOP_K, tile, dp), U32), pltpu.SemaphoreType.DMA((2,))],
        compiler_params=_params(1), name="combine_shared_ln2",
    )(dest, dest, y, top_w, h2p, x1, g2, sg, su, sd, lg, lb)


def kernel(x, c, ctx, c_ctx, w_mod, b_mod, w_in, na_rpb, gla_w_decay_f, gla_b_decay_f, gla_w_decay_b, gla_b_decay_b,
           gla_norm_w, w_branch_a, w_branch_b, w_out, ln1_g, ln1_b, router_w, router_bias, exp_w_gate, exp_w_up,
           exp_w_down, sh_w_gate, sh_w_up, sh_w_down, ln2_g, ln2_b):
    batch, t_len, d = x.shape
    ctx_len = ctx.shape[1]
    n = batch * t_len
    assert w_mod.shape[0] == DEPTH == 1
    assert t_len % GLA_CHUNK == 0 and ctx_len == GLA_CHUNK and (t_len // GRID_W) % NA_ROW_BLOCK == 0

    mod_rows = 16
    c_all = jnp.zeros((mod_rows, d), F32).at[:batch].set(c).at[batch].set(c_ctx)
    mod = _modulation(c_all, w_mod[0], b_mod[0])
    sh1, sc1, g1, sh2, sc2, g2 = [mod[:batch, j * d:(j + 1) * d].reshape(batch, 1, d) for j in range(6)]
    sh1c = mod[batch:batch + 1, 0:d].reshape(1, 1, d)
    sc1c = mod[batch:batch + 1, d:2 * d].reshape(1, 1, d)

    offs = np.cumsum((0, NA_WIDTH, NA_WIDTH, NA_WIDTH, GLA_QK_WIDTH, GLA_QK_WIDTH, GLA_V_WIDTH, GLA_V_WIDTH,
                      GLA_GATE_RANK, GLA_GATE_RANK, d, d))
    qa, ka, va, qb, kb, vbc, gbc, lrf, lrb, ga, gbt = [w_in[0][:, offs[j]:offs[j + 1]] for j in range(11)]
    lr_cols = jnp.concatenate([lrf, lrb, jnp.zeros((d, LANES - 2 * GLA_GATE_RANK), F32)], axis=1)
    w_lat = jnp.concatenate([qa, ka, va, qb, kb, vbc, gbc, ga, gbt, lr_cols], axis=1).astype(BF16)
    w_ctx = jnp.concatenate([ka, va, kb, vbc, lr_cols], axis=1).astype(BF16)
    plain = ("plain",)
    lat_plan = ((3 * NA_WIDTH, (("scale", NA_HEAD_DIM ** -0.5), plain, plain)),
                (2 * GLA_QK_WIDTH, (("rope", GLA_DK ** -0.5), ("rope", 1.0))),
                (GLA_V_WIDTH, (plain, plain)), (GLA_V_WIDTH, (plain, plain)),
                (2 * d, (plain,) * 4), (LANES, (plain,)))
    ctx_plan = ((2 * NA_WIDTH, (plain, plain)), (GLA_QK_WIDTH, (plain,)), (GLA_V_WIDTH, (plain, plain)),
                (LANES, (plain,)))
    tile = 256
    x2d = x.reshape(n, d)
    na_qkv, gla_qk, vb, gb, gates, lr = _projection(
        x2d, sc1, sh1, w_lat, lat_plan, (BF16, BF16, BF16, BF16, BF16, F32), tile, t_len // tile,
        rope=_rope_tables(t_len))
    na_kv_c, k_c, v_c, lr_c = _projection(
        ctx.reshape(batch * ctx_len, d), sc1c, sh1c, w_ctx, ctx_plan, (BF16, BF16, BF16, F32), tile,
        batch * ctx_len // tile)

    out_a = _na_attention(na_qkv, na_kv_c, _na_bias_tables(na_rpb[0]), batch, t_len, ctx_len)

    wf, bf = _decay_weights(gla_w_decay_f[0], gla_b_decay_f[0], 0)
    wb, bb = _decay_weights(gla_w_decay_b[0], gla_b_decay_b[0], GLA_GATE_RANK)
    o_f, o_b = _gla(gla_qk, vb, lr, k_c, v_c, lr_c, wf, bf, wb, bb, batch, t_len, ctx_len)

    x1, h2p, scores_t = _merge(
        out_a.reshape(n, NA_WIDTH), o_f.reshape(n, GLA_V_WIDTH), o_b.reshape(n, GLA_V_WIDTH), gb, gates, x2d,
        g1, sc2, sh2, w_branch_a[0].astype(BF16), w_branch_b[0].astype(BF16), w_out[0].astype(BF16),
        gla_norm_w[0].reshape(1, -1), ln1_g[0].reshape(1, d), ln1_b[0].reshape(1, d),
        router_w[0].T.astype(BF16), tile, t_len // tile)

    top_e, top_w, rank, counts = _route(scores_t, router_bias[0], 512)

    counts = counts[:, 0]
    n_blocks = n * TOP_K // EXPERT_BLOCK + N_EXPERTS
    blocks_per = (counts + EXPERT_BLOCK - 1) // EXPERT_BLOCK
    blk_end = jnp.cumsum(blocks_per)
    blk_start = blk_end - blocks_per
    dest = _slots(top_e, rank, (blk_start * EXPERT_BLOCK).astype(F32).reshape(N_EXPERTS, 1), 512)

    xs = _dispatch(dest, h2p, n_blocks * EXPERT_BLOCK, tile)
    y = _experts(blk_start.astype(jnp.int32), counts, blk_end[-1:].astype(jnp.int32), xs,
                 exp_w_gate[0], exp_w_up[0], exp_w_down[0])
    out = _combine(dest, y, top_w.T, h2p, x1, g2, sh_w_gate[0].astype(BF16), sh_w_up[0].astype(BF16),
                   sh_w_down[0].astype(BF16), ln2_g[0].reshape(1, d), ln2_b[0].reshape(1, d), tile, t_len // tile)
    return out.reshape(batch, t_len, d)
```

```python
import functools

import numpy as np
import jax
import jax.numpy as jnp
from jax import lax
from jax.experimental import pallas as pl
from jax.experimental.pallas import tpu as pltpu
from jax.experimental.pallas import tpu_sc as plsc

F32 = jnp.float32
BF16 = jnp.bfloat16
U32 = jnp.uint32
HIGHEST = lax.Precision.HIGHEST

GRID_W = 64
NA_HEADS = 8
NA_HEAD_DIM = 64
NA_WIDTH = NA_HEADS * NA_HEAD_DIM
WIN_ROWS = 8
WIN_COLS = 16
GLA_HEADS = 4
GLA_DK = 128
GLA_DV = 256
GLA_QK_WIDTH = GLA_HEADS * GLA_DK
GLA_V_WIDTH = GLA_HEADS * GLA_DV
GLA_GATE_RANK = 16
GLA_TAU = 16.0
LOG2E = 1.4426950408889634
ROPE_BASE = 10000.0
N_EXPERTS = 256
TOP_K = 8
N_GROUPS = 8
TOPK_GROUPS = 4
GROUP_SIZE = N_EXPERTS // N_GROUPS
ROUTED_SCALE = 2.5
DEPTH = 1
DEEPNORM_ALPHA = (2 * DEPTH) ** 0.25
EPS = 1e-6

LANES = 128
NA_ROW_BLOCK = 4
NA_UNION_ROWS = NA_ROW_BLOCK + WIN_ROWS - 1
GLA_CHUNK = 256
GLA_DIAG = 16
GLA_HEADS_PER_STEP = 4
EXPERT_BLOCK = 256
EXPERT_RING = 4
SC_CORES = 2
SC_SUBCORES = 16
SC_GATHER_ROWS = 128
NEG_BIG = -1e30
VMEM_LIMIT = 56 * 1024 * 1024


def _params(n_axes, vmem=VMEM_LIMIT):
    return pltpu.CompilerParams(dimension_semantics=("arbitrary",) * n_axes, vmem_limit_bytes=vmem)


def _sigmoid(v):
    return 1.0 / (1.0 + jnp.exp(-v))


def _silu(v):
    return v * _sigmoid(v)


def _pack_pairs(v):
    m = v.shape[1] // 2
    lo = lax.bitcast_convert_type(v[:, :m].astype(BF16).astype(F32), U32) >> 16
    hi = lax.bitcast_convert_type(v[:, m:].astype(BF16).astype(F32), U32) & jnp.uint32(0xFFFF0000)
    return lo | hi


def _unpack_pairs(p):
    lo = lax.bitcast_convert_type(p << 16, F32)
    hi = lax.bitcast_convert_type(p & jnp.uint32(0xFFFF0000), F32)
    return jnp.concatenate([lo, hi], axis=1)


def _mod_kernel(c_ref, w_ref, b_ref, o_ref):
    o_ref[...] = jnp.dot(_silu(c_ref[...]), w_ref[...], preferred_element_type=F32, precision=HIGHEST) + b_ref[...]


def _modulation(c_all, w_mod, b_mod):
    rows, d = c_all.shape
    n = w_mod.shape[1]
    bn = 512
    return pl.pallas_call(
        _mod_kernel,
        out_shape=jax.ShapeDtypeStruct((rows, n), F32),
        grid=(n // bn,),
        in_specs=[pl.BlockSpec((rows, d), lambda j: (0, 0)),
                  pl.BlockSpec((d, bn), lambda j: (0, j)),
                  pl.BlockSpec((1, bn), lambda j: (0, j))],
        out_specs=pl.BlockSpec((rows, bn), lambda j: (0, j)),
        compiler_params=_params(1),
        name="modulation",
    )(c_all, w_mod, b_mod.reshape(1, n))


def _swap32(v):
    lane = lax.broadcasted_iota(jnp.int32, v.shape, 1)
    return jnp.where((lane % 64) < 32, pltpu.roll(v, 96, 1), pltpu.roll(v, 32, 1))


def _proj_kernel(plan, has_rope, *refs):
    x_ref, sc_ref, sh_ref, w_ref = refs[:4]
    pos = 4
    if has_rope:
        cos_ref, sin_ref = refs[4:6]
        pos = 6
    out_refs = refs[pos:]
    h = (x_ref[...] * (1.0 + sc_ref[0]) + sh_ref[0]).astype(BF16)
    col = 0
    for out_ref, (width, kinds) in zip(out_refs, plan):
        for j, kind in enumerate(kinds):
            cw = width // len(kinds)
            c0 = j * cw
            acc = jnp.dot(h, w_ref[:, col + c0:col + c0 + cw], preferred_element_type=F32)
            if kind[0] == "scale":
                acc = acc * kind[1]
            elif kind[0] == "rope":
                cos, sin = cos_ref[...], sin_ref[...]
                pieces = []
                for p in range(cw // LANES):
                    v = acc[:, p * LANES:(p + 1) * LANES]
                    pieces.append((v * cos + _swap32(v) * sin) * kind[1])
                acc = jnp.concatenate(pieces, axis=1)
            out_ref[:, c0:c0 + cw] = acc.astype(out_ref.dtype)
        col += width


def _projection(x2d, sc, sh, w, plan, out_dtypes, tile, tiles_per_mod, rope=None):
    n, d = x2d.shape
    in_specs = [pl.BlockSpec((tile, d), lambda i: (i, 0)),
                pl.BlockSpec((1, 1, d), lambda i: (i // tiles_per_mod, 0, 0)),
                pl.BlockSpec((1, 1, d), lambda i: (i // tiles_per_mod, 0, 0)),
                pl.BlockSpec(w.shape, lambda i: (0, 0), pipeline_mode=pl.Buffered(1))]
    args = [x2d, sc, sh, w]
    if rope is not None:
        in_specs += [pl.BlockSpec((tile, LANES), lambda i: (i % tiles_per_mod, 0))] * 2
        args += list(rope)
    out_shape = [jax.ShapeDtypeStruct((n, wd), dt) for (wd, _), dt in zip(plan, out_dtypes)]
    out_specs = [pl.BlockSpec((tile, wd), lambda i: (i, 0)) for (wd, _) in plan]
    return pl.pallas_call(
        functools.partial(_proj_kernel, plan, rope is not None),
        out_shape=out_shape,
        grid=(n // tile,),
        in_specs=in_specs,
        out_specs=out_specs,
        compiler_params=_params(1),
        name="in_proj" if rope is not None else "ctx_proj",
    )(*args)


def _rope_tables(t_len):
    half = GLA_DK // 2
    quarter = half // 2
    inv_freq = ROPE_BASE ** (-jnp.arange(quarter, dtype=F32) / quarter)
    pos = jnp.arange(t_len)
    row_ang = (pos // GRID_W).astype(F32)[:, None] * inv_freq[None, :]
    col_ang = (pos % GRID_W).astype(F32)[:, None] * inv_freq[None, :]
    cr, sr, cc, sn = jnp.cos(row_ang), jnp.sin(row_ang), jnp.cos(col_ang), jnp.sin(col_ang)
    cos = jnp.concatenate([cr, cr, cc, cc], axis=1)
    sin = jnp.concatenate([-sr, sr, -sn, sn], axis=1)
    return cos, sin


def _na_bias_tables(rpb):
    rb, ur, w = NA_ROW_BLOCK, NA_UNION_ROWS, GRID_W
    heads = rpb.shape[0]
    pad = jnp.pad(rpb, ((0, 0), (0, 0), (w, w)))
    toep = jnp.stack([pad[:, :, w + WIN_COLS - 1 - c:2 * w + WIN_COLS - 1 - c] for c in range(w)], axis=2)
    c = np.arange(w)[:, None]
    kc = np.arange(w)[None, :]
    col_start = np.clip(c - WIN_COLS // 2, 0, w - WIN_COLS)
    col_ok = (kc >= col_start) & (kc < col_start + WIN_COLS)
    toep = jnp.where(col_ok[None, None], toep, NEG_BIG)
    neg = jnp.full((heads, w, w), NEG_BIG, F32)
    half = WIN_ROWS // 2
    tables = []
    for lo, off in ((lambda i: 0, WIN_ROWS - 1), (lambda i: i, WIN_ROWS - 1 - half), (lambda i: ur - WIN_ROWS, 0)):
        rows_ = []
        for i in range(rb):
            blocks = [toep[:, j - i + off] if lo(i) <= j < lo(i) + WIN_ROWS else neg for j in range(ur)]
            rows_.append(jnp.concatenate(blocks, axis=2))
        tables.append(jnp.concatenate(rows_, axis=1))
    return jnp.stack(tables)


def _na_kernel(rows, q_ref, k_ref, v_ref, kc_ref, vc_ref, bias_ref, o_ref):
    rb = pl.program_id(1)
    ustart = jnp.clip(rb * NA_ROW_BLOCK - WIN_ROWS // 2, 0, rows - NA_UNION_ROWS)
    k0 = pl.multiple_of(ustart * GRID_W, GRID_W)
    nk = NA_UNION_ROWS * GRID_W
    nt = (((1,), (1,)), ((), ()))
    lane = lax.broadcasted_iota(jnp.int32, (NA_ROW_BLOCK * GRID_W, LANES), 1)
    for p in range(NA_WIDTH // LANES):
        ls = slice(p * LANES, (p + 1) * LANES)
        q = q_ref[:, ls]
        ku = k_ref[pl.ds(k0, nk), ls]
        vu = v_ref[pl.ds(k0, nk), ls]
        kc = kc_ref[:, ls]
        vc = vc_ref[:, ls]
        out = jnp.zeros(q.shape, F32)
        for hh in range(LANES // NA_HEAD_DIM):
            mine = (lane // NA_HEAD_DIM) == hh
            qm = jnp.where(mine, q, jnp.zeros_like(q))
            s_loc = lax.dot_general(qm, ku, nt, preferred_element_type=F32) + bias_ref[p * 2 + hh]
            s_ctx = lax.dot_general(qm, kc, nt, preferred_element_type=F32)
            m = jnp.maximum(jnp.max(s_loc, axis=1, keepdims=True), jnp.max(s_ctx, axis=1, keepdims=True))
            p_loc = jnp.exp(s_loc - m)
            p_ctx = jnp.exp(s_ctx - m)
            denom = jnp.sum(p_loc, axis=1, keepdims=True) + jnp.sum(p_ctx, axis=1, keepdims=True)
            o = (jnp.dot(p_loc.astype(BF16), vu, preferred_element_type=F32)
                 + jnp.dot(p_ctx.astype(BF16), vc, preferred_element_type=F32)) / denom
            out = jnp.where(mine, o, out)
        o_ref[:, ls] = out.astype(o_ref.dtype)


def _na_attention(na_qkv, na_kv_ctx, bias_tables, batch, t_len, ctx_len):
    rows = t_len // GRID_W
    n_rb = rows // NA_ROW_BLOCK
    tq = NA_ROW_BLOCK * GRID_W
    qkv = na_qkv.reshape(batch, t_len, 3 * NA_WIDTH)
    kvc = na_kv_ctx.reshape(batch, ctx_len, 2 * NA_WIDTH)

    def bias_idx(b, r):
        return (jnp.where(r == 0, 0, jnp.where(r == n_rb - 1, 2, 1)), 0, 0, 0)

    return pl.pallas_call(
        functools.partial(_na_kernel, rows),
        out_shape=jax.ShapeDtypeStruct((batch, t_len, NA_WIDTH), BF16),
        grid=(batch, n_rb),
        in_specs=[pl.BlockSpec((None, tq, NA_WIDTH), lambda b, r: (b, r, 0)),
                  pl.BlockSpec((None, t_len, NA_WIDTH), lambda b, r: (b, 0, 1)),
                  pl.BlockSpec((None, t_len, NA_WIDTH), lambda b, r: (b, 0, 2)),
                  pl.BlockSpec((None, ctx_len, NA_WIDTH), lambda b, r: (b, 0, 0)),
                  pl.BlockSpec((None, ctx_len, NA_WIDTH), lambda b, r: (b, 0, 1)),
                  pl.BlockSpec((None,) + bias_tables.shape[1:], bias_idx)],
        out_specs=pl.BlockSpec((None, tq, NA_WIDTH), lambda b, r: (b, r, 0)),
        compiler_params=_params(2),
        name="na_attention",
    )(qkv, qkv, qkv, kvc, kvc, bias_tables)


def _log2_decay(lr, w2, b2):
    z = (jnp.dot(lr.astype(BF16), w2.astype(BF16), preferred_element_type=F32) + b2) * LOG2E
    return (jnp.minimum(z, 0.0) - jnp.log2(1.0 + jnp.exp2(-jnp.abs(z)))) * (1.0 / GLA_TAU)


def _gla_constants(c):
    tris, masks = [], []
    for reverse in (False, True):
        i = np.arange(c)[:, None]
        j = np.arange(c)[None, :]
        tris.append((j >= i) if reverse else (j <= i))
        i = np.arange(c // 2)[:, None]
        j = np.arange(c // 2)[None, :]
        if reverse:
            i, j = j, i
        level = []
        s = c // 4
        while s >= GLA_DIAG:
            level.append(((i // (2 * s)) == (j // (2 * s))) & ((i % (2 * s)) >= s) & ((j % (2 * s)) < s))
            s //= 2
        level.append(((i // GLA_DIAG) == (j // GLA_DIAG)) & (j <= i))
        masks.append(np.stack(level))
    return jnp.asarray(np.stack(tris), BF16), jnp.asarray(np.stack(masks), F32)


def _block_refs(cum, s, reverse, diag):
    c = cum.shape[0]
    span = s if diag else 2 * s
    parts = []
    for p in range(c // span):
        if diag:
            r = p * span + (span - 1 if reverse else 0)
        else:
            r = p * span + (s - 1 if reverse else s)
        parts.append(jnp.broadcast_to(cum[r:r + 1, :], (span, cum.shape[1])))
    return jnp.concatenate(parts, axis=0)


def _cumsum_rows(a, tri):
    hi = a.astype(BF16)
    rest = a - hi.astype(F32)
    mid = rest.astype(BF16)
    lo = (rest - mid.astype(F32)).astype(BF16)
    parts = jnp.dot(tri, jnp.concatenate([hi, mid, lo], axis=1), preferred_element_type=F32)
    w = a.shape[1]
    return parts[:, :w] + parts[:, w:2 * w] + parts[:, 2 * w:]


def _gla_chunk(q, k, v, a, state_t, tri, mask_ref, reverse, want_out):
    c = k.shape[0]
    hc = c // 2
    nt = (((1,), (1,)), ((), ()))
    cum = _cumsum_rows(a, tri)
    last = 0 if reverse else c - 1
    total = cum[last:last + 1, :]
    out = None
    if want_out:
        q_in = (q * jnp.exp2(cum)).astype(BF16)
        out = lax.dot_general(q_in, state_t.astype(BF16), nt, preferred_element_type=F32)
        halves = (slice(hc, c), slice(0, hc)) if reverse else (slice(0, hc), slice(hc, c))
        early, late = halves

        def scaled(s, diag):
            d = cum - _block_refs(cum, GLA_DIAG if diag else s, reverse, diag)
            if diag:
                return (q * jnp.exp2(d)).astype(BF16), (k * jnp.exp2(-d)).astype(BF16)
            e = jnp.exp2(-jnp.abs(d))
            return (q * e).astype(BF16), (k * e).astype(BF16)

        qs, ks = scaled(hc, False)
        cross = lax.dot_general(qs[late], ks[early], nt, preferred_element_type=F32)
        inner = [jnp.zeros((hc, hc), F32), jnp.zeros((hc, hc), F32)]
        s = hc // 2
        level = 0
        while True:
            diag = s < GLA_DIAG
            qs, ks = scaled(s, diag)
            for x, rows in enumerate(halves):
                att = lax.dot_general(qs[rows], ks[rows], nt, preferred_element_type=F32)
                inner[x] = inner[x] + att * mask_ref[level]
            if diag:
                break
            s //= 2
            level += 1
        out_early = jnp.dot(inner[0].astype(BF16), v[early], preferred_element_type=F32)
        out_late = jnp.dot(jnp.concatenate([cross, inner[1]], axis=1).astype(BF16),
                           jnp.concatenate([v[early], v[late]], axis=0), preferred_element_type=F32)
        intra = [out_late, out_early] if reverse else [out_early, out_late]
        out = out + jnp.concatenate(intra, axis=0)
    k_out = (k * jnp.exp2(total - cum)).astype(BF16)
    upd = lax.dot_general(v, k_out, (((0,), (0,)), ((), ())), preferred_element_type=F32)
    return out, jnp.exp2(total) * state_t + upd


def _gla_kernel(qf_ref, kf_ref, vf_ref, lrf_ref, qb_ref, kb_ref, vb_ref, lrb_ref,
                kc_ref, vc_ref, lrc_ref, wf_ref, bf_ref, wb_ref, bb_ref, tri_ref, mask_ref,
                of_ref, ob_ref, sf_ref, sb_ref):
    c = pl.program_id(2)
    fm, bm = mask_ref.at[0], mask_ref.at[1]
    heads = sf_ref.shape[0]

    @pl.when(c == 0)
    def _():
        lrc = lrc_ref[...]
        zero = jnp.zeros(sf_ref.shape[1:], F32)
        for h in range(heads):
            kc = kc_ref[:, h * GLA_DK:(h + 1) * GLA_DK].astype(F32)
            vc = vc_ref[:, h * GLA_DV:(h + 1) * GLA_DV]
            _, sf_ref[h] = _gla_chunk(None, kc, vc, _log2_decay(lrc, wf_ref[h], bf_ref[h]), zero,
                                      tri_ref[0], fm, False, False)
            _, sb_ref[h] = _gla_chunk(None, kc, vc, _log2_decay(lrc, wb_ref[h], bb_ref[h]), zero,
                                      tri_ref[1], bm, True, False)

    @pl.when(c > 0)
    def _():
        lrf, lrb = lrf_ref[...], lrb_ref[...]
        for h in range(heads):
            ks = slice(h * GLA_DK, (h + 1) * GLA_DK)
            vs = slice(h * GLA_DV, (h + 1) * GLA_DV)
            o, sf_ref[h] = _gla_chunk(qf_ref[:, ks].astype(F32), kf_ref[:, ks].astype(F32), vf_ref[:, vs],
                                      _log2_decay(lrf, wf_ref[h], bf_ref[h]), sf_ref[h],
                                      tri_ref[0], fm, False, True)
            of_ref[:, vs] = o.astype(of_ref.dtype)
            o, sb_ref[h] = _gla_chunk(qb_ref[:, ks].astype(F32), kb_ref[:, ks].astype(F32), vb_ref[:, vs],
                                      _log2_decay(lrb, wb_ref[h], bb_ref[h]), sb_ref[h],
                                      tri_ref[1], bm, True, True)
            ob_ref[:, vs] = o.astype(ob_ref.dtype)


def _gla(gla_qk, vb, lr, k_ctx, v_ctx, lr_ctx, wf, bf, wb, bb, batch, t_len, ctx_len):
    nc = t_len // GLA_CHUNK
    h = GLA_HEADS
    qk = gla_qk.reshape(batch, t_len, 2 * GLA_QK_WIDTH)
    v3 = vb.reshape(batch, t_len, GLA_V_WIDTH)
    lr3 = lr.reshape(batch, t_len, LANES)
    kc3 = k_ctx.reshape(batch, ctx_len, GLA_QK_WIDTH)
    vc3 = v_ctx.reshape(batch, ctx_len, GLA_V_WIDTH)
    lrc3 = lr_ctx.reshape(batch, ctx_len, LANES)
    tri, masks = _gla_constants(GLA_CHUNK)

    def fwd(c):
        return jnp.maximum(c - 1, 0)

    def bwd(c):
        return nc - 1 - jnp.maximum(c - 1, 0)

    def const(a):
        return pl.BlockSpec(a.shape, lambda b, hh, c: (0,) * a.ndim, pipeline_mode=pl.Buffered(1))

    hp = GLA_HEADS_PER_STEP
    groups = h // hp
    cq = (None, GLA_CHUNK, hp * GLA_DK)
    cv = (None, GLA_CHUNK, hp * GLA_DV)
    cl = (None, GLA_CHUNK, LANES)
    in_specs = [
        pl.BlockSpec(cq, lambda b, g, c: (b, fwd(c), g)),
        pl.BlockSpec(cq, lambda b, g, c: (b, fwd(c), groups + g)),
        pl.BlockSpec(cv, lambda b, g, c: (b, fwd(c), g)),
        pl.BlockSpec(cl, lambda b, g, c: (b, fwd(c), 0)),
        pl.BlockSpec(cq, lambda b, g, c: (b, bwd(c), g)),
        pl.BlockSpec(cq, lambda b, g, c: (b, bwd(c), groups + g)),
        pl.BlockSpec(cv, lambda b, g, c: (b, bwd(c), g)),
        pl.BlockSpec(cl, lambda b, g, c: (b, bwd(c), 0)),
        pl.BlockSpec((None, ctx_len, hp * GLA_DK), lambda b, g, c: (b, 0, g)),
        pl.BlockSpec((None, ctx_len, hp * GLA_DV), lambda b, g, c: (b, 0, g)),
        pl.BlockSpec((None, ctx_len, LANES), lambda b, g, c: (b, 0, 0)),
        pl.BlockSpec((hp, LANES, GLA_DK), lambda b, g, c: (g, 0, 0)),
        pl.BlockSpec((hp, 1, GLA_DK), lambda b, g, c: (g, 0, 0)),
        pl.BlockSpec((hp, LANES, GLA_DK), lambda b, g, c: (g, 0, 0)),
        pl.BlockSpec((hp, 1, GLA_DK), lambda b, g, c: (g, 0, 0)),
        const(tri), const(masks),
    ]
    out_specs = [pl.BlockSpec(cv, lambda b, g, c: (b, fwd(c), g)),
                 pl.BlockSpec(cv, lambda b, g, c: (b, bwd(c), g))]
    out_shape = [jax.ShapeDtypeStruct((batch, t_len, GLA_V_WIDTH), BF16)] * 2
    return pl.pallas_call(
        _gla_kernel,
        out_shape=out_shape,
        grid=(batch, groups, nc + 1),
        in_specs=in_specs,
        out_specs=out_specs,
        scratch_shapes=[pltpu.VMEM((hp, GLA_DV, GLA_DK), F32), pltpu.VMEM((hp, GLA_DV, GLA_DK), F32)],
        compiler_params=_params(3),
        name="gla",
    )(qk, qk, v3, lr3, qk, qk, v3, lr3, kc3, vc3, lrc3, wf, bf, wb, bb, tri, masks)


def _decay_weights(w_dec, b_dec, row0):
    w = w_dec.reshape(GLA_GATE_RANK, GLA_HEADS, GLA_DK).transpose(1, 0, 2)
    wp = jnp.zeros((GLA_HEADS, LANES, GLA_DK), F32).at[:, row0:row0 + GLA_GATE_RANK, :].set(w)
    return wp, b_dec.reshape(GLA_HEADS, 1, GLA_DK)


def _layer_norm(v, g, b):
    mu = jnp.mean(v, axis=1, keepdims=True)
    var = jnp.mean(jnp.square(v - mu), axis=1, keepdims=True)
    return (v - mu) * lax.rsqrt(var + EPS) * g + b


def _merge_kernel(oa_ref, of_ref, ob_ref, gb_ref, gates_ref, x_ref, g1_ref, sc2_ref, sh2_ref,
                  wa_ref, wb_ref, wo_ref, nw_ref, lg_ref, lb_ref, rw_ref,
                  x1_ref, h2_ref, sc_ref):
    d = x_ref.shape[1]
    o = of_ref[...].astype(F32) + ob_ref[...].astype(F32)
    pieces = []
    for hh in range(GLA_HEADS):
        oh = o[:, hh * GLA_DV:(hh + 1) * GLA_DV]
        pieces.append(oh * lax.rsqrt(jnp.mean(jnp.square(oh), axis=1, keepdims=True) + EPS))
    out_b = jnp.concatenate(pieces, axis=1) * nw_ref[...] * _silu(gb_ref[...].astype(F32))
    ya = jnp.dot(oa_ref[...], wa_ref[...], preferred_element_type=F32)
    yb = jnp.dot(out_b.astype(BF16), wb_ref[...], preferred_element_type=F32)
    y = _sigmoid(gates_ref[:, :d].astype(F32)) * ya + _sigmoid(gates_ref[:, d:].astype(F32)) * yb
    y2 = jnp.dot(y.astype(BF16), wo_ref[...], preferred_element_type=F32)
    x1 = _layer_norm(DEEPNORM_ALPHA * x_ref[...] + g1_ref[0] * y2, lg_ref[...], lb_ref[...])
    x1_ref[...] = x1
    h2 = x1 * (1.0 + sc2_ref[0]) + sh2_ref[0]
    h2_ref[...] = _pack_pairs(h2)
    logits_t = lax.dot_general(rw_ref[...], h2.astype(BF16), (((1,), (1,)), ((), ())), preferred_element_type=F32)
    sc_ref[...] = _sigmoid(logits_t)


def _merge(out_a, o_f, o_b, gb, gates, x2d, g1, sc2, sh2, wa, wb, wo, nw, lg, lb, rw_t, tile, tiles_per_batch):
    n, d = x2d.shape
    row = lambda i: (i, 0)
    mod = lambda i: (i // tiles_per_batch, 0, 0)
    full = lambda i: (0, 0)

    def const(a):
        return pl.BlockSpec(a.shape, full, pipeline_mode=pl.Buffered(1))

    in_specs = [pl.BlockSpec((tile, NA_WIDTH), row), pl.BlockSpec((tile, GLA_V_WIDTH), row),
                pl.BlockSpec((tile, GLA_V_WIDTH), row), pl.BlockSpec((tile, GLA_V_WIDTH), row),
                pl.BlockSpec((tile, 2 * d), row), pl.BlockSpec((tile, d), row),
                pl.BlockSpec((1, 1, d), mod), pl.BlockSpec((1, 1, d), mod), pl.BlockSpec((1, 1, d), mod),
                const(wa), const(wb), const(wo), const(nw), const(lg), const(lb), const(rw_t)]
    out_shape = [jax.ShapeDtypeStruct((n, d), F32), jax.ShapeDtypeStruct((n, d // 2), U32),
                 jax.ShapeDtypeStruct((N_EXPERTS, n), F32)]
    out_specs = [pl.BlockSpec((tile, d), row), pl.BlockSpec((tile, d // 2), row),
                 pl.BlockSpec((N_EXPERTS, tile), lambda i: (0, i))]
    return pl.pallas_call(
        _merge_kernel, out_shape=out_shape, grid=(n // tile,), in_specs=in_specs, out_specs=out_specs,
        compiler_params=_params(1), name="merge_ln1_router",
    )(out_a, o_f, o_b, gb, gates, x2d, g1, sc2, sh2, wa, wb, wo, nw, lg, lb, rw_t)


def _first_argmax(vals, idx, n):
    m = jnp.max(vals, axis=0, keepdims=True)
    first = jnp.min(jnp.where(vals == m, idx, float(n)), axis=0, keepdims=True)
    return m, first


def _route_kernel(sc_ref, bias_ref, e_ref, w_ref, rank_ref, cnt_ref, carry_ref):
    step = pl.program_id(0)
    tr = sc_ref.shape[1]

    @pl.when(step == 0)
    def _():
        carry_ref[...] = jnp.zeros(carry_ref.shape, F32)

    scores = sc_ref[...]
    biased = scores + bias_ref[...]
    eidx = lax.broadcasted_iota(jnp.int32, (N_EXPERTS, tr), 0).astype(F32)
    lidx = lax.broadcasted_iota(jnp.int32, (GROUP_SIZE, tr), 0).astype(F32)
    gidx = lax.broadcasted_iota(jnp.int32, (N_GROUPS, tr), 0).astype(F32)
    gs = []
    for g in range(N_GROUPS):
        blk = biased[g * GROUP_SIZE:(g + 1) * GROUP_SIZE]
        m1, first = _first_argmax(blk, lidx, GROUP_SIZE)
        m2 = jnp.max(jnp.where(lidx == first, -jnp.inf, blk), axis=0, keepdims=True)
        gs.append(m1 + m2)
    cur = jnp.concatenate(gs, axis=0)
    keep = jnp.zeros((N_GROUPS, tr), F32)
    for _ in range(TOPK_GROUPS):
        _, first = _first_argmax(cur, gidx, N_GROUPS)
        sel = gidx == first
        keep = jnp.where(sel, 1.0, keep)
        cur = jnp.where(sel, -jnp.inf, cur)
    keep_e = jnp.concatenate([jnp.broadcast_to(keep[g:g + 1], (GROUP_SIZE, tr)) for g in range(N_GROUPS)], axis=0)
    masked = jnp.where(keep_e > 0.5, biased, -jnp.inf)
    chosen = jnp.zeros((N_EXPERTS, tr), F32)
    tops, topi = [], []
    for _ in range(TOP_K):
        _, first = _first_argmax(masked, eidx, N_EXPERTS)
        sel = eidx == first
        tops.append(jnp.sum(jnp.where(sel, scores, 0.0), axis=0, keepdims=True))
        topi.append(first)
        chosen = jnp.where(sel, 1.0, chosen)
        masked = jnp.where(sel, -jnp.inf, masked)
    top_s = jnp.concatenate(tops, axis=0)
    top_i = jnp.concatenate(topi, axis=0)
    e_ref[...] = top_i.astype(jnp.int32)
    w_ref[...] = top_s / jnp.sum(top_s, axis=0, keepdims=True) * ROUTED_SCALE
    r = lax.broadcasted_iota(jnp.int32, (tr, tr), 0)
    cidx = lax.broadcasted_iota(jnp.int32, (tr, tr), 1)
    before = jnp.where(r < cidx, 1.0, 0.0).astype(BF16)
    prior = jnp.dot(chosen.astype(BF16), before, preferred_element_type=F32) + carry_ref[...]
    ranks = [jnp.sum(jnp.where(eidx == topi[kk], prior, 0.0), axis=0, keepdims=True) for kk in range(TOP_K)]
    rank_ref[...] = jnp.concatenate(ranks, axis=0).astype(jnp.int32)
    carry_ref[...] = carry_ref[...] + jnp.sum(chosen, axis=1, keepdims=True)
    cnt_ref[...] = jnp.broadcast_to(carry_ref[...], cnt_ref.shape).astype(jnp.int32)


def _route(scores_t, router_bias, tile):
    n = scores_t.shape[1]
    col = lambda i: (0, i)
    out_shape = [jax.ShapeDtypeStruct((TOP_K, n), jnp.int32), jax.ShapeDtypeStruct((TOP_K, n), F32),
                 jax.ShapeDtypeStruct((TOP_K, n), jnp.int32), jax.ShapeDtypeStruct((N_EXPERTS, LANES), jnp.int32)]
    return pl.pallas_call(
        _route_kernel, out_shape=out_shape, grid=(n // tile,),
        in_specs=[pl.BlockSpec((N_EXPERTS, tile), col), pl.BlockSpec((N_EXPERTS, 1), lambda i: (0, 0))],
        out_specs=[pl.BlockSpec((TOP_K, tile), col), pl.BlockSpec((TOP_K, tile), col),
                   pl.BlockSpec((TOP_K, tile), col), pl.BlockSpec((N_EXPERTS, LANES), lambda i: (0, 0))],
        scratch_shapes=[pltpu.VMEM((N_EXPERTS, 1), F32)],
        compiler_params=_params(1), name="route",
    )(scores_t, router_bias.reshape(N_EXPERTS, 1))


def _slots_kernel(e_ref, rank_ref, start_ref, dest_ref):
    tr = e_ref.shape[1]
    eidx = lax.broadcasted_iota(jnp.int32, (N_EXPERTS, tr), 0)
    e = e_ref[...]
    start = start_ref[...]
    rows = [jnp.sum(jnp.where(eidx == e[kk:kk + 1], start, 0.0), axis=0, keepdims=True) for kk in range(TOP_K)]
    dest_ref[...] = jnp.concatenate(rows, axis=0).astype(jnp.int32) + rank_ref[...]


def _slots(top_e, rank, start_rows, tile):
    n = top_e.shape[1]
    col = lambda i: (0, i)
    return pl.pallas_call(
        _slots_kernel, out_shape=jax.ShapeDtypeStruct((TOP_K, n), jnp.int32), grid=(n // tile,),
        in_specs=[pl.BlockSpec((TOP_K, tile), col), pl.BlockSpec((TOP_K, tile), col),
                  pl.BlockSpec((N_EXPERTS, 1), lambda i: (0, 0))],
        out_specs=pl.BlockSpec((TOP_K, tile), col),
        compiler_params=_params(1), name="slots",
    )(top_e, rank, start_rows)


def _row_copy(src, s, dst, t, sem):
    return pltpu.make_async_copy(src.at[pl.ds(s, 1)], dst.at[pl.ds(t, 1)], sem)


def _dispatch_kernel(tile, dest_ref, h_ref, xs_ref, sem):
    def issue(t, carry):
        for kk in range(TOP_K):
            _row_copy(h_ref, t, xs_ref, dest_ref[kk, t], sem).start(priority=kk % 2)
        return carry

    lax.fori_loop(0, tile, issue, 0)

    def drain(t, carry):
        for kk in range(TOP_K):
            _row_copy(h_ref, t, xs_ref, dest_ref[kk, t], sem).wait()
        return carry

    lax.fori_loop(0, tile, drain, 0)


def _dispatch(dest, h2p, n_slots, tile):
    n, dp = h2p.shape
    return pl.pallas_call(
        functools.partial(_dispatch_kernel, tile),
        out_shape=jax.ShapeDtypeStruct((n_slots, dp), h2p.dtype),
        grid=(n // tile,),
        in_specs=[pl.BlockSpec((TOP_K, tile), lambda i: (0, i), memory_space=pltpu.SMEM),
                  pl.BlockSpec((tile, dp), lambda i: (i, 0))],
        out_specs=pl.BlockSpec(memory_space=pl.ANY),
        scratch_shapes=[pltpu.SemaphoreType.DMA],
        compiler_params=_params(1), name="dispatch",
    )(dest, h2p)


def _expert_kernel(start_ref, cnt_ref, nused_ref, xs_ref, wg_ref, wu_ref, wd_ref, y_ref,
                   wgb, wub, wdb, xbuf, ybuf, sem_in, sem_out):
    e = pl.program_id(0)
    bm, ring = EXPERT_BLOCK, EXPERT_RING
    cnt = cnt_ref[e]
    nb = (cnt + bm - 1) // bm
    g0 = start_ref[e]
    n_used = nused_ref[0]

    def x_copy(g):
        slot = g % ring
        return pltpu.make_async_copy(xs_ref.at[pl.ds(g * bm, bm)], xbuf.at[slot], sem_in.at[slot])

    def y_copy(g):
        slot = g % ring
        return pltpu.make_async_copy(ybuf.at[slot], y_ref.at[pl.ds(g * bm, bm)], sem_out.at[slot])

    @pl.when(e == 0)
    def _():
        for g in range(ring):
            @pl.when(g < n_used)
            def _():
                x_copy(g).start()

    wgb[...] = wg_ref[...].astype(BF16)
    wub[...] = wu_ref[...].astype(BF16)
    wdb[...] = wd_ref[...].astype(BF16)
    row = lax.broadcasted_iota(jnp.int32, (bm, xbuf.shape[2]), 0)

    def block(j, carry):
        g = g0 + j
        slot = g % ring
        x_copy(g).wait()

        @pl.when(g >= ring)
        def _():
            y_copy(g - ring).wait()

        x = _unpack_pairs(jnp.where(row < cnt - j * bm, xbuf[slot], jnp.uint32(0))).astype(BF16)
        gate = jnp.dot(x, wgb[...], preferred_element_type=F32)
        up = jnp.dot(x, wub[...], preferred_element_type=F32)
        ybuf[slot] = _pack_pairs(jnp.dot((_silu(gate) * up).astype(BF16), wdb[...], preferred_element_type=F32))
        y_copy(g).start()

        @pl.when(g + ring < n_used)
        def _():
            x_copy(g + ring).start()
        return carry

    lax.fori_loop(0, nb, block, 0)

    @pl.when(e == pl.num_programs(0) - 1)
    def _():
        for back in range(1, ring + 1):
            @pl.when(n_used - back >= 0)
            def _():
                y_copy(n_used - back).wait()


def _experts(blk_start, counts, n_used, xs, wg, wu, wd):
    n_slots, dp = xs.shape
    n_exp, d, ff = wg.shape
    bm, ring = EXPERT_BLOCK, EXPERT_RING
    wsel = lambda e, st, ct, nu: (e, 0, 0)
    grid_spec = pltpu.PrefetchScalarGridSpec(
        num_scalar_prefetch=3, grid=(n_exp,),
        in_specs=[pl.BlockSpec(memory_space=pl.ANY),
                  pl.BlockSpec((None, d, ff), wsel), pl.BlockSpec((None, d, ff), wsel),
                  pl.BlockSpec((None, ff, d), wsel)],
        out_specs=pl.BlockSpec(memory_space=pl.ANY),
        scratch_shapes=[pltpu.VMEM((d, ff), BF16), pltpu.VMEM((d, ff), BF16), pltpu.VMEM((ff, d), BF16),
                        pltpu.VMEM((ring, bm, dp), U32), pltpu.VMEM((ring, bm, dp), U32),
                        pltpu.SemaphoreType.DMA((ring,)), pltpu.SemaphoreType.DMA((ring,))])
    return pl.pallas_call(
        _expert_kernel, out_shape=jax.ShapeDtypeStruct((n_slots, dp), U32), grid_spec=grid_spec,
        compiler_params=_params(1), name="experts",
    )(blk_start, counts, n_used, xs, wg, wu, wd)


def _sc_gather_rows(table, idx):
    n_idx = idx.shape[0]
    dp = table.shape[1]
    workers = SC_CORES * SC_SUBCORES
    per_worker = n_idx // workers
    chunk = SC_GATHER_ROWS
    mesh = plsc.VectorSubcoreMesh(core_axis_name="c", subcore_axis_name="s",
                                  num_cores=SC_CORES, num_subcores=SC_SUBCORES)

    @functools.partial(
        pl.kernel, mesh=mesh, out_type=jax.ShapeDtypeStruct((n_idx, dp), table.dtype),
        scratch_types=[pltpu.VMEM((chunk,), jnp.int32), pltpu.VMEM((chunk, dp), table.dtype),
                       pltpu.SemaphoreType.DMA],
        name="sc_gather_rows")
    def gather(table_hbm, idx_hbm, out_hbm, idx_v, rows_v, sem):
        base = (lax.axis_index("s") * SC_CORES + lax.axis_index("c")) * per_worker

        @pl.loop(0, per_worker // chunk)
        def _(j):
            off = base + j * chunk
            pltpu.sync_copy(idx_hbm.at[pl.ds(off, chunk)], idx_v)
            pltpu.async_copy(table_hbm.at[idx_v], rows_v, sem).wait()
            pltpu.sync_copy(rows_v, out_hbm.at[pl.ds(off, chunk)])

    return gather(table, idx)


def _combine_kernel(tile, yt_ref, w_ref, h_ref, x1_ref, g2_ref, sg_ref, su_ref, sd_ref, lg_ref, lb_ref, o_ref):
    hb = _unpack_pairs(h_ref[...]).astype(BF16)
    g = jnp.dot(hb, sg_ref[...], preferred_element_type=F32)
    u = jnp.dot(hb, su_ref[...], preferred_element_type=F32)
    f = jnp.dot((_silu(g) * u).astype(BF16), sd_ref[...], preferred_element_type=F32)
    w = w_ref[...]
    for kk in range(TOP_K):
        f = f + w[:, kk:kk + 1] * _unpack_pairs(yt_ref[kk * tile:(kk + 1) * tile, :])
    o_ref[...] = _layer_norm(DEEPNORM_ALPHA * x1_ref[...] + g2_ref[0] * f, lg_ref[...], lb_ref[...])


def _combine(y_tok, top_w, h2p, x1, g2, sg, su, sd, lg, lb, tile, tiles_per_batch):
    n, d = x1.shape
    dp = h2p.shape[1]
    row = lambda i: (i, 0)
    full = lambda i: (0, 0)

    def const(a):
        return pl.BlockSpec(a.shape, full, pipeline_mode=pl.Buffered(1))

    return pl.pallas_call(
        functools.partial(_combine_kernel, tile),
        out_shape=jax.ShapeDtypeStruct((n, d), F32),
        grid=(n // tile,),
        in_specs=[pl.BlockSpec((TOP_K * tile, dp), row),
                  pl.BlockSpec((tile, TOP_K), row), pl.BlockSpec((tile, dp), row), pl.BlockSpec((tile, d), row),
                  pl.BlockSpec((1, 1, d), lambda i: (i // tiles_per_batch, 0, 0)),
                  const(sg), const(su), const(sd), const(lg), const(lb)],
        out_specs=pl.BlockSpec((tile, d), row),
        compiler_params=_params(1), name="combine_shared_ln2",
    )(y_tok, top_w, h2p, x1, g2, sg, su, sd, lg, lb)


def kernel(x, c, ctx, c_ctx, w_mod, b_mod, w_in, na_rpb, gla_w_decay_f, gla_b_decay_f, gla_w_decay_b, gla_b_decay_b,
           gla_norm_w, w_branch_a, w_branch_b, w_out, ln1_g, ln1_b, router_w, router_bias, exp_w_gate, exp_w_up,
           exp_w_down, sh_w_gate, sh_w_up, sh_w_down, ln2_g, ln2_b):
    batch, t_len, d = x.shape
    ctx_len = ctx.shape[1]
    n = batch * t_len
    assert w_mod.shape[0] == DEPTH == 1
    assert t_len % GLA_CHUNK == 0 and ctx_len == GLA_CHUNK and (t_len // GRID_W) % NA_ROW_BLOCK == 0

    mod_rows = 16
    c_all = jnp.zeros((mod_rows, d), F32).at[:batch].set(c).at[batch].set(c_ctx)
    mod = _modulation(c_all, w_mod[0], b_mod[0])
    sh1, sc1, g1, sh2, sc2, g2 = [mod[:batch, j * d:(j + 1) * d].reshape(batch, 1, d) for j in range(6)]
    sh1c = mod[batch:batch + 1, 0:d].reshape(1, 1, d)
    sc1c = mod[batch:batch + 1, d:2 * d].reshape(1, 1, d)

    offs = np.cumsum((0, NA_WIDTH, NA_WIDTH, NA_WIDTH, GLA_QK_WIDTH, GLA_QK_WIDTH, GLA_V_WIDTH, GLA_V_WIDTH,
                      GLA_GATE_RANK, GLA_GATE_RANK, d, d))
    qa, ka, va, qb, kb, vbc, gbc, lrf, lrb, ga, gbt = [w_in[0][:, offs[j]:offs[j + 1]] for j in range(11)]
    lr_cols = jnp.concatenate([lrf, lrb, jnp.zeros((d, LANES - 2 * GLA_GATE_RANK), F32)], axis=1)
    w_lat = jnp.concatenate([qa, ka, va, qb, kb, vbc, gbc, ga, gbt, lr_cols], axis=1).astype(BF16)
    w_ctx = jnp.concatenate([ka, va, kb, vbc, lr_cols], axis=1).astype(BF16)
    plain = ("plain",)
    lat_plan = ((3 * NA_WIDTH, (("scale", NA_HEAD_DIM ** -0.5), plain, plain)),
                (2 * GLA_QK_WIDTH, (("rope", GLA_DK ** -0.5), ("rope", 1.0))),
                (GLA_V_WIDTH, (plain, plain)), (GLA_V_WIDTH, (plain, plain)),
                (2 * d, (plain,) * 4), (LANES, (plain,)))
    ctx_plan = ((2 * NA_WIDTH, (plain, plain)), (GLA_QK_WIDTH, (plain,)), (GLA_V_WIDTH, (plain, plain)),
                (LANES, (plain,)))
    tile = 256
    x2d = x.reshape(n, d)
    na_qkv, gla_qk, vb, gb, gates, lr = _projection(
        x2d, sc1, sh1, w_lat, lat_plan, (BF16, BF16, BF16, BF16, BF16, F32), tile, t_len // tile,
        rope=_rope_tables(t_len))
    na_kv_c, k_c, v_c, lr_c = _projection(
        ctx.reshape(batch * ctx_len, d), sc1c, sh1c, w_ctx, ctx_plan, (BF16, BF16, BF16, F32), tile,
        batch * ctx_len // tile)

    out_a = _na_attention(na_qkv, na_kv_c, _na_bias_tables(na_rpb[0]), batch, t_len, ctx_len)

    wf, bf = _decay_weights(gla_w_decay_f[0], gla_b_decay_f[0], 0)
    wb, bb = _decay_weights(gla_w_decay_b[0], gla_b_decay_b[0], GLA_GATE_RANK)
    o_f, o_b = _gla(gla_qk, vb, lr, k_c, v_c, lr_c, wf, bf, wb, bb, batch, t_len, ctx_len)

    x1, h2p, scores_t = _merge(
        out_a.reshape(n, NA_WIDTH), o_f.reshape(n, GLA_V_WIDTH), o_b.reshape(n, GLA_V_WIDTH), gb, gates, x2d,
        g1, sc2, sh2, w_branch_a[0].astype(BF16), w_branch_b[0].astype(BF16), w_out[0].astype(BF16),
        gla_norm_w[0].reshape(1, -1), ln1_g[0].reshape(1, d), ln1_b[0].reshape(1, d),
        router_w[0].T.astype(BF16), tile, t_len // tile)

    top_e, top_w, rank, counts = _route(scores_t, router_bias[0], 512)

    counts = counts[:, 0]
    n_blocks = n * TOP_K // EXPERT_BLOCK + N_EXPERTS
    blocks_per = (counts + EXPERT_BLOCK - 1) // EXPERT_BLOCK
    blk_end = jnp.cumsum(blocks_per)
    blk_start = blk_end - blocks_per
    dest = _slots(top_e, rank, (blk_start * EXPERT_BLOCK).astype(F32).reshape(N_EXPERTS, 1), 512)

    xs = _dispatch(dest, h2p, n_blocks * EXPERT_BLOCK, tile)
    y = _experts(blk_start.astype(jnp.int32), counts, blk_end[-1:].astype(jnp.int32), xs,
                 exp_w_gate[0], exp_w_up[0], exp_w_down[0])
    dest_tok = dest.reshape(TOP_K, n // tile, tile).transpose(1, 0, 2).reshape(n * TOP_K)
    y_tok = _sc_gather_rows(y, dest_tok)
    out = _combine(y_tok, top_w.T, h2p, x1, g2, sh_w_gate[0].astype(BF16), sh_w_up[0].astype(BF16),
                   sh_w_down[0].astype(BF16), ln2_g[0].reshape(1, d), ln2_b[0].reshape(1, d), tile, t_len // tile)
    return out.reshape(batch, t_len, d)
```

```python
import functools

import numpy as np
import jax
import jax.numpy as jnp
from jax import lax
from jax.experimental import pallas as pl
from jax.experimental.pallas import tpu as pltpu
from jax.experimental.pallas import tpu_sc as plsc

F32 = jnp.float32
BF16 = jnp.bfloat16
U32 = jnp.uint32
HIGHEST = lax.Precision.HIGHEST

GRID_W = 64
NA_HEADS = 8
NA_HEAD_DIM = 64
NA_WIDTH = NA_HEADS * NA_HEAD_DIM
WIN_ROWS = 8
WIN_COLS = 16
GLA_HEADS = 4
GLA_DK = 128
GLA_DV = 256
GLA_QK_WIDTH = GLA_HEADS * GLA_DK
GLA_V_WIDTH = GLA_HEADS * GLA_DV
GLA_GATE_RANK = 16
GLA_TAU = 16.0
LOG2E = 1.4426950408889634
ROPE_BASE = 10000.0
N_EXPERTS = 256
TOP_K = 8
N_GROUPS = 8
TOPK_GROUPS = 4
GROUP_SIZE = N_EXPERTS // N_GROUPS
ROUTED_SCALE = 2.5
DEPTH = 1
DEEPNORM_ALPHA = (2 * DEPTH) ** 0.25
EPS = 1e-6

LANES = 128
NA_ROW_BLOCK = 4
NA_UNION_ROWS = NA_ROW_BLOCK + WIN_ROWS - 1
GLA_CHUNK = 256
GLA_DIAG = 16
GLA_HEADS_PER_STEP = 4
EXPERT_BLOCK = 256
EXPERT_RING = 4
SC_CORES = 2
SC_SUBCORES = 16
SC_GATHER_ROWS = 128
NEG_BIG = -1e30
VMEM_LIMIT = 56 * 1024 * 1024


def _params(n_axes, vmem=VMEM_LIMIT):
    return pltpu.CompilerParams(dimension_semantics=("arbitrary",) * n_axes, vmem_limit_bytes=vmem)


def _sigmoid(v):
    return 1.0 / (1.0 + jnp.exp(-v))


def _silu(v):
    return v * _sigmoid(v)


def _pack_pairs(v):
    m = v.shape[1] // 2
    lo = lax.bitcast_convert_type(v[:, :m].astype(BF16).astype(F32), U32) >> 16
    hi = lax.bitcast_convert_type(v[:, m:].astype(BF16).astype(F32), U32) & jnp.uint32(0xFFFF0000)
    return lo | hi


def _unpack_pairs(p):
    lo = lax.bitcast_convert_type(p << 16, F32)
    hi = lax.bitcast_convert_type(p & jnp.uint32(0xFFFF0000), F32)
    return jnp.concatenate([lo, hi], axis=1)


def _mod_kernel(c_ref, w_ref, b_ref, o_ref):
    o_ref[...] = jnp.dot(_silu(c_ref[...]), w_ref[...], preferred_element_type=F32, precision=HIGHEST) + b_ref[...]


def _modulation(c_all, w_mod, b_mod):
    rows, d = c_all.shape
    n = w_mod.shape[1]
    bn = 512
    return pl.pallas_call(
        _mod_kernel,
        out_shape=jax.ShapeDtypeStruct((rows, n), F32),
        grid=(n // bn,),
        in_specs=[pl.BlockSpec((rows, d), lambda j: (0, 0)),
                  pl.BlockSpec((d, bn), lambda j: (0, j)),
                  pl.BlockSpec((1, bn), lambda j: (0, j))],
        out_specs=pl.BlockSpec((rows, bn), lambda j: (0, j)),
        compiler_params=_params(1),
        name="modulation",
    )(c_all, w_mod, b_mod.reshape(1, n))


def _swap32(v):
    lane = lax.broadcasted_iota(jnp.int32, v.shape, 1)
    return jnp.where((lane % 64) < 32, pltpu.roll(v, 96, 1), pltpu.roll(v, 32, 1))


def _proj_kernel(plan, has_rope, *refs):
    x_ref, sc_ref, sh_ref, w_ref = refs[:4]
    pos = 4
    if has_rope:
        cos_ref, sin_ref = refs[4:6]
        pos = 6
    out_refs = refs[pos:]
    h = (x_ref[...] * (1.0 + sc_ref[0]) + sh_ref[0]).astype(BF16)
    col = 0
    for out_ref, (width, kinds) in zip(out_refs, plan):
        for j, kind in enumerate(kinds):
            cw = width // len(kinds)
            c0 = j * cw
            acc = jnp.dot(h, w_ref[:, col + c0:col + c0 + cw], preferred_element_type=F32)
            if kind[0] == "scale":
                acc = acc * kind[1]
            elif kind[0] == "rope":
                cos, sin = cos_ref[...], sin_ref[...]
                pieces = []
                for p in range(cw // LANES):
                    v = acc[:, p * LANES:(p + 1) * LANES]
                    pieces.append((v * cos + _swap32(v) * sin) * kind[1])
                acc = jnp.concatenate(pieces, axis=1)
            out_ref[:, c0:c0 + cw] = acc.astype(out_ref.dtype)
        col += width


def _projection(x2d, sc, sh, w, plan, out_dtypes, tile, tiles_per_mod, rope=None):
    n, d = x2d.shape
    in_specs = [pl.BlockSpec((tile, d), lambda i: (i, 0)),
                pl.BlockSpec((1, 1, d), lambda i: (i // tiles_per_mod, 0, 0)),
                pl.BlockSpec((1, 1, d), lambda i: (i // tiles_per_mod, 0, 0)),
                pl.BlockSpec(w.shape, lambda i: (0, 0), pipeline_mode=pl.Buffered(1))]
    args = [x2d, sc, sh, w]
    if rope is not None:
        in_specs += [pl.BlockSpec((tile, LANES), lambda i: (i % tiles_per_mod, 0))] * 2
        args += list(rope)
    out_shape = [jax.ShapeDtypeStruct((n, wd), dt) for (wd, _), dt in zip(plan, out_dtypes)]
    out_specs = [pl.BlockSpec((tile, wd), lambda i: (i, 0)) for (wd, _) in plan]
    return pl.pallas_call(
        functools.partial(_proj_kernel, plan, rope is not None),
        out_shape=out_shape,
        grid=(n // tile,),
        in_specs=in_specs,
        out_specs=out_specs,
        compiler_params=_params(1),
        name="in_proj" if rope is not None else "ctx_proj",
    )(*args)


def _rope_tables(t_len):
    half = GLA_DK // 2
    quarter = half // 2
    inv_freq = ROPE_BASE ** (-jnp.arange(quarter, dtype=F32) / quarter)
    pos = jnp.arange(t_len)
    row_ang = (pos // GRID_W).astype(F32)[:, None] * inv_freq[None, :]
    col_ang = (pos % GRID_W).astype(F32)[:, None] * inv_freq[None, :]
    cr, sr, cc, sn = jnp.cos(row_ang), jnp.sin(row_ang), jnp.cos(col_ang), jnp.sin(col_ang)
    cos = jnp.concatenate([cr, cr, cc, cc], axis=1)
    sin = jnp.concatenate([-sr, sr, -sn, sn], axis=1)
    return cos, sin


def _na_bias_tables(rpb):
    rb, ur, w = NA_ROW_BLOCK, NA_UNION_ROWS, GRID_W
    heads = rpb.shape[0]
    pad = jnp.pad(rpb, ((0, 0), (0, 0), (w, w)))
    toep = jnp.stack([pad[:, :, w + WIN_COLS - 1 - c:2 * w + WIN_COLS - 1 - c] for c in range(w)], axis=2)
    c = np.arange(w)[:, None]
    kc = np.arange(w)[None, :]
    col_start = np.clip(c - WIN_COLS // 2, 0, w - WIN_COLS)
    col_ok = (kc >= col_start) & (kc < col_start + WIN_COLS)
    toep = jnp.where(col_ok[None, None], toep, NEG_BIG)
    neg = jnp.full((heads, w, w), NEG_BIG, F32)
    half = WIN_ROWS // 2
    tables = []
    for lo, off in ((lambda i: 0, WIN_ROWS - 1), (lambda i: i, WIN_ROWS - 1 - half), (lambda i: ur - WIN_ROWS, 0)):
        rows_ = []
        for i in range(rb):
            blocks = [toep[:, j - i + off] if lo(i) <= j < lo(i) + WIN_ROWS else neg for j in range(ur)]
            rows_.append(jnp.concatenate(blocks, axis=2))
        tables.append(jnp.concatenate(rows_, axis=1))
    return jnp.stack(tables)


def _na_kernel(rows, q_ref, k_ref, v_ref, kc_ref, vc_ref, bias_ref, o_ref):
    rb = pl.program_id(1)
    ustart = jnp.clip(rb * NA_ROW_BLOCK - WIN_ROWS // 2, 0, rows - NA_UNION_ROWS)
    k0 = pl.multiple_of(ustart * GRID_W, GRID_W)
    nk = NA_UNION_ROWS * GRID_W
    nt = (((1,), (1,)), ((), ()))
    lane = lax.broadcasted_iota(jnp.int32, (NA_ROW_BLOCK * GRID_W, LANES), 1)
    for p in range(NA_WIDTH // LANES):
        ls = slice(p * LANES, (p + 1) * LANES)
        q = q_ref[:, ls]
        ku = k_ref[pl.ds(k0, nk), ls]
        vu = v_ref[pl.ds(k0, nk), ls]
        kc = kc_ref[:, ls]
        vc = vc_ref[:, ls]
        out = jnp.zeros(q.shape, F32)
        for hh in range(LANES // NA_HEAD_DIM):
            mine = (lane // NA_HEAD_DIM) == hh
            qm = jnp.where(mine, q, jnp.zeros_like(q))
            s_loc = lax.dot_general(qm, ku, nt, preferred_element_type=F32) + bias_ref[p * 2 + hh]
            s_ctx = lax.dot_general(qm, kc, nt, preferred_element_type=F32)
            m = jnp.maximum(jnp.max(s_loc, axis=1, keepdims=True), jnp.max(s_ctx, axis=1, keepdims=True))
            p_loc = jnp.exp(s_loc - m)
            p_ctx = jnp.exp(s_ctx - m)
            denom = jnp.sum(p_loc, axis=1, keepdims=True) + jnp.sum(p_ctx, axis=1, keepdims=True)
            o = (jnp.dot(p_loc.astype(BF16), vu, preferred_element_type=F32)
                 + jnp.dot(p_ctx.astype(BF16), vc, preferred_element_type=F32)) / denom
            out = jnp.where(mine, o, out)
        o_ref[:, ls] = out.astype(o_ref.dtype)


def _na_attention(na_qkv, na_kv_ctx, bias_tables, batch, t_len, ctx_len):
    rows = t_len // GRID_W
    n_rb = rows // NA_ROW_BLOCK
    tq = NA_ROW_BLOCK * GRID_W
    qkv = na_qkv.reshape(batch, t_len, 3 * NA_WIDTH)
    kvc = na_kv_ctx.reshape(batch, ctx_len, 2 * NA_WIDTH)

    def bias_idx(b, r):
        return (jnp.where(r == 0, 0, jnp.where(r == n_rb - 1, 2, 1)), 0, 0, 0)

    return pl.pallas_call(
        functools.partial(_na_kernel, rows),
        out_shape=jax.ShapeDtypeStruct((batch, t_len, NA_WIDTH), BF16),
        grid=(batch, n_rb),
        in_specs=[pl.BlockSpec((None, tq, NA_WIDTH), lambda b, r: (b, r, 0)),
                  pl.BlockSpec((None, t_len, NA_WIDTH), lambda b, r: (b, 0, 1)),
                  pl.BlockSpec((None, t_len, NA_WIDTH), lambda b, r: (b, 0, 2)),
                  pl.BlockSpec((None, ctx_len, NA_WIDTH), lambda b, r: (b, 0, 0)),
                  pl.BlockSpec((None, ctx_len, NA_WIDTH), lambda b, r: (b, 0, 1)),
                  pl.BlockSpec((None,) + bias_tables.shape[1:], bias_idx)],
        out_specs=pl.BlockSpec((None, tq, NA_WIDTH), lambda b, r: (b, r, 0)),
        compiler_params=_params(2),
        name="na_attention",
    )(qkv, qkv, qkv, kvc, kvc, bias_tables)


def _log2_decay(lr, w2, b2):
    z = (jnp.dot(lr.astype(BF16), w2.astype(BF16), preferred_element_type=F32) + b2) * LOG2E
    return (jnp.minimum(z, 0.0) - jnp.log2(1.0 + jnp.exp2(-jnp.abs(z)))) * (1.0 / GLA_TAU)


def _gla_constants(c):
    tris, masks = [], []
    for reverse in (False, True):
        i = np.arange(c)[:, None]
        j = np.arange(c)[None, :]
        tris.append((j >= i) if reverse else (j <= i))
        i = np.arange(c // 2)[:, None]
        j = np.arange(c // 2)[None, :]
        if reverse:
            i, j = j, i
        level = []
        s = c // 4
        while s >= GLA_DIAG:
            level.append(((i // (2 * s)) == (j // (2 * s))) & ((i % (2 * s)) >= s) & ((j % (2 * s)) < s))
            s //= 2
        level.append(((i // GLA_DIAG) == (j // GLA_DIAG)) & (j <= i))
        masks.append(np.stack(level))
    return jnp.asarray(np.stack(tris), BF16), jnp.asarray(np.stack(masks), F32)


def _block_refs(cum, s, reverse, diag):
    c = cum.shape[0]
    span = s if diag else 2 * s
    parts = []
    for p in range(c // span):
        if diag:
            r = p * span + (span - 1 if reverse else 0)
        else:
            r = p * span + (s - 1 if reverse else s)
        parts.append(jnp.broadcast_to(cum[r:r + 1, :], (span, cum.shape[1])))
    return jnp.concatenate(parts, axis=0)


def _cumsum_rows(a, tri):
    hi = a.astype(BF16)
    rest = a - hi.astype(F32)
    mid = rest.astype(BF16)
    lo = (rest - mid.astype(F32)).astype(BF16)
    parts = jnp.dot(tri, jnp.concatenate([hi, mid, lo], axis=1), preferred_element_type=F32)
    w = a.shape[1]
    return parts[:, :w] + parts[:, w:2 * w] + parts[:, 2 * w:]


def _gla_chunk(q, k, v, a, state_t, tri, mask_ref, reverse, want_out):
    c = k.shape[0]
    hc = c // 2
    nt = (((1,), (1,)), ((), ()))
    cum = _cumsum_rows(a, tri)
    last = 0 if reverse else c - 1
    total = cum[last:last + 1, :]
    out = None
    if want_out:
        q_in = (q * jnp.exp2(cum)).astype(BF16)
        out = lax.dot_general(q_in, state_t.astype(BF16), nt, preferred_element_type=F32)
        halves = (slice(hc, c), slice(0, hc)) if reverse else (slice(0, hc), slice(hc, c))
        early, late = halves

        def scaled(s, diag):
            d = cum - _block_refs(cum, GLA_DIAG if diag else s, reverse, diag)
            if diag:
                return (q * jnp.exp2(d)).astype(BF16), (k * jnp.exp2(-d)).astype(BF16)
            e = jnp.exp2(-jnp.abs(d))
            return (q * e).astype(BF16), (k * e).astype(BF16)

        qs, ks = scaled(hc, False)
        cross = lax.dot_general(qs[late], ks[early], nt, preferred_element_type=F32)
        inner = [jnp.zeros((hc, hc), F32), jnp.zeros((hc, hc), F32)]
        s = hc // 2
        level = 0
        while True:
            diag = s < GLA_DIAG
            qs, ks = scaled(s, diag)
            for x, rows in enumerate(halves):
                att = lax.dot_general(qs[rows], ks[rows], nt, preferred_element_type=F32)
                inner[x] = inner[x] + att * mask_ref[level]
            if diag:
                break
            s //= 2
            level += 1
        out_early = jnp.dot(inner[0].astype(BF16), v[early], preferred_element_type=F32)
        out_late = jnp.dot(jnp.concatenate([cross, inner[1]], axis=1).astype(BF16),
                           jnp.concatenate([v[early], v[late]], axis=0), preferred_element_type=F32)
        intra = [out_late, out_early] if reverse else [out_early, out_late]
        out = out + jnp.concatenate(intra, axis=0)
    k_out = (k * jnp.exp2(total - cum)).astype(BF16)
    upd = lax.dot_general(v, k_out, (((0,), (0,)), ((), ())), preferred_element_type=F32)
    return out, jnp.exp2(total) * state_t + upd


def _gla_kernel(qf_ref, kf_ref, vf_ref, lrf_ref, qb_ref, kb_ref, vb_ref, lrb_ref,
                kc_ref, vc_ref, lrc_ref, wf_ref, bf_ref, wb_ref, bb_ref, tri_ref, mask_ref,
                of_ref, ob_ref, sf_ref, sb_ref):
    c = pl.program_id(2)
    fm, bm = mask_ref.at[0], mask_ref.at[1]
    heads = sf_ref.shape[0]

    @pl.when(c == 0)
    def _():
        lrc = lrc_ref[...]
        zero = jnp.zeros(sf_ref.shape[1:], F32)
        for h in range(heads):
            kc = kc_ref[:, h * GLA_DK:(h + 1) * GLA_DK].astype(F32)
            vc = vc_ref[:, h * GLA_DV:(h + 1) * GLA_DV]
            _, sf_ref[h] = _gla_chunk(None, kc, vc, _log2_decay(lrc, wf_ref[h], bf_ref[h]), zero,
                                      tri_ref[0], fm, False, False)
            _, sb_ref[h] = _gla_chunk(None, kc, vc, _log2_decay(lrc, wb_ref[h], bb_ref[h]), zero,
                                      tri_ref[1], bm, True, False)

    @pl.when(c > 0)
    def _():
        lrf, lrb = lrf_ref[...], lrb_ref[...]
        for h in range(heads):
            ks = slice(h * GLA_DK, (h + 1) * GLA_DK)
            vs = slice(h * GLA_DV, (h + 1) * GLA_DV)
            o, sf_ref[h] = _gla_chunk(qf_ref[:, ks].astype(F32), kf_ref[:, ks].astype(F32), vf_ref[:, vs],
                                      _log2_decay(lrf, wf_ref[h], bf_ref[h]), sf_ref[h],
                                      tri_ref[0], fm, False, True)
            of_ref[:, vs] = o.astype(of_ref.dtype)
            o, sb_ref[h] = _gla_chunk(qb_ref[:, ks].astype(F32), kb_ref[:, ks].astype(F32), vb_ref[:, vs],
                                      _log2_decay(lrb, wb_ref[h], bb_ref[h]), sb_ref[h],
                                      tri_ref[1], bm, True, True)
            ob_ref[:, vs] = o.astype(ob_ref.dtype)


def _gla(gla_qk, vb, lr, k_ctx, v_ctx, lr_ctx, wf, bf, wb, bb, batch, t_len, ctx_len):
    nc = t_len // GLA_CHUNK
    h = GLA_HEADS
    qk = gla_qk.reshape(batch, t_len, 2 * GLA_QK_WIDTH)
    v3 = vb.reshape(batch, t_len, GLA_V_WIDTH)
    lr3 = lr.reshape(batch, t_len, LANES)
    kc3 = k_ctx.reshape(batch, ctx_len, GLA_QK_WIDTH)
    vc3 = v_ctx.reshape(batch, ctx_len, GLA_V_WIDTH)
    lrc3 = lr_ctx.reshape(batch, ctx_len, LANES)
    tri, masks = _gla_constants(GLA_CHUNK)

    def fwd(c):
        return jnp.maximum(c - 1, 0)

    def bwd(c):
        return nc - 1 - jnp.maximum(c - 1, 0)

    def const(a):
        return pl.BlockSpec(a.shape, lambda b, hh, c: (0,) * a.ndim, pipeline_mode=pl.Buffered(1))

    hp = GLA_HEADS_PER_STEP
    groups = h // hp
    cq = (None, GLA_CHUNK, hp * GLA_DK)
    cv = (None, GLA_CHUNK, hp * GLA_DV)
    cl = (None, GLA_CHUNK, LANES)
    in_specs = [
        pl.BlockSpec(cq, lambda b, g, c: (b, fwd(c), g)),
        pl.BlockSpec(cq, lambda b, g, c: (b, fwd(c), groups + g)),
        pl.BlockSpec(cv, lambda b, g, c: (b, fwd(c), g)),
        pl.BlockSpec(cl, lambda b, g, c: (b, fwd(c), 0)),
        pl.BlockSpec(cq, lambda b, g, c: (b, bwd(c), g)),
        pl.BlockSpec(cq, lambda b, g, c: (b, bwd(c), groups + g)),
        pl.BlockSpec(cv, lambda b, g, c: (b, bwd(c), g)),
        pl.BlockSpec(cl, lambda b, g, c: (b, bwd(c), 0)),
        pl.BlockSpec((None, ctx_len, hp * GLA_DK), lambda b, g, c: (b, 0, g)),
        pl.BlockSpec((None, ctx_len, hp * GLA_DV), lambda b, g, c: (b, 0, g)),
        pl.BlockSpec((None, ctx_len, LANES), lambda b, g, c: (b, 0, 0)),
        pl.BlockSpec((hp, LANES, GLA_DK), lambda b, g, c: (g, 0, 0)),
        pl.BlockSpec((hp, 1, GLA_DK), lambda b, g, c: (g, 0, 0)),
        pl.BlockSpec((hp, LANES, GLA_DK), lambda b, g, c: (g, 0, 0)),
        pl.BlockSpec((hp, 1, GLA_DK), lambda b, g, c: (g, 0, 0)),
        const(tri), const(masks),
    ]
    out_specs = [pl.BlockSpec(cv, lambda b, g, c: (b, fwd(c), g)),
                 pl.BlockSpec(cv, lambda b, g, c: (b, bwd(c), g))]
    out_shape = [jax.ShapeDtypeStruct((batch, t_len, GLA_V_WIDTH), BF16)] * 2
    return pl.pallas_call(
        _gla_kernel,
        out_shape=out_shape,
        grid=(batch, groups, nc + 1),
        in_specs=in_specs,
        out_specs=out_specs,
        scratch_shapes=[pltpu.VMEM((hp, GLA_DV, GLA_DK), F32), pltpu.VMEM((hp, GLA_DV, GLA_DK), F32)],
        compiler_params=_params(3),
        name="gla",
    )(qk, qk, v3, lr3, qk, qk, v3, lr3, kc3, vc3, lrc3, wf, bf, wb, bb, tri, masks)


def _decay_weights(w_dec, b_dec, row0):
    w = w_dec.reshape(GLA_GATE_RANK, GLA_HEADS, GLA_DK).transpose(1, 0, 2)
    wp = jnp.zeros((GLA_HEADS, LANES, GLA_DK), F32).at[:, row0:row0 + GLA_GATE_RANK, :].set(w)
    return wp, b_dec.reshape(GLA_HEADS, 1, GLA_DK)


def _layer_norm(v, g, b):
    mu = jnp.mean(v, axis=1, keepdims=True)
    var = jnp.mean(jnp.square(v - mu), axis=1, keepdims=True)
    return (v - mu) * lax.rsqrt(var + EPS) * g + b


def _merge_kernel(oa_ref, of_ref, ob_ref, gb_ref, gates_ref, x_ref, g1_ref, sc2_ref, sh2_ref,
                  wa_ref, wb_ref, wo_ref, nw_ref, lg_ref, lb_ref, rw_ref,
                  x1_ref, h2_ref, sc_ref):
    d = x_ref.shape[1]
    o = of_ref[...].astype(F32) + ob_ref[...].astype(F32)
    pieces = []
    for hh in range(GLA_HEADS):
        oh = o[:, hh * GLA_DV:(hh + 1) * GLA_DV]
        pieces.append(oh * lax.rsqrt(jnp.mean(jnp.square(oh), axis=1, keepdims=True) + EPS))
    out_b = jnp.concatenate(pieces, axis=1) * nw_ref[...] * _silu(gb_ref[...].astype(F32))
    ya = jnp.dot(oa_ref[...], wa_ref[...], preferred_element_type=F32)
    yb = jnp.dot(out_b.astype(BF16), wb_ref[...], preferred_element_type=F32)
    y = _sigmoid(gates_ref[:, :d].astype(F32)) * ya + _sigmoid(gates_ref[:, d:].astype(F32)) * yb
    y2 = jnp.dot(y.astype(BF16), wo_ref[...], preferred_element_type=F32)
    x1 = _layer_norm(DEEPNORM_ALPHA * x_ref[...] + g1_ref[0] * y2, lg_ref[...], lb_ref[...])
    x1_ref[...] = x1
    h2 = x1 * (1.0 + sc2_ref[0]) + sh2_ref[0]
    h2_ref[...] = _pack_pairs(h2)
    logits_t = lax.dot_general(rw_ref[...], h2.astype(BF16), (((1,), (1,)), ((), ())), preferred_element_type=F32)
    sc_ref[...] = _sigmoid(logits_t)


def _merge(out_a, o_f, o_b, gb, gates, x2d, g1, sc2, sh2, wa, wb, wo, nw, lg, lb, rw_t, tile, tiles_per_batch):
    n, d = x2d.shape
    row = lambda i: (i, 0)
    mod = lambda i: (i // tiles_per_batch, 0, 0)
    full = lambda i: (0, 0)

    def const(a):
        return pl.BlockSpec(a.shape, full, pipeline_mode=pl.Buffered(1))

    in_specs = [pl.BlockSpec((tile, NA_WIDTH), row), pl.BlockSpec((tile, GLA_V_WIDTH), row),
                pl.BlockSpec((tile, GLA_V_WIDTH), row), pl.BlockSpec((tile, GLA_V_WIDTH), row),
                pl.BlockSpec((tile, 2 * d), row), pl.BlockSpec((tile, d), row),
                pl.BlockSpec((1, 1, d), mod), pl.BlockSpec((1, 1, d), mod), pl.BlockSpec((1, 1, d), mod),
                const(wa), const(wb), const(wo), const(nw), const(lg), const(lb), const(rw_t)]
    out_shape = [jax.ShapeDtypeStruct((n, d), F32), jax.ShapeDtypeStruct((n, d // 2), U32),
                 jax.ShapeDtypeStruct((N_EXPERTS, n), F32)]
    out_specs = [pl.BlockSpec((tile, d), row), pl.BlockSpec((tile, d // 2), row),
                 pl.BlockSpec((N_EXPERTS, tile), lambda i: (0, i))]
    return pl.pallas_call(
        _merge_kernel, out_shape=out_shape, grid=(n // tile,), in_specs=in_specs, out_specs=out_specs,
        compiler_params=_params(1), name="merge_ln1_router",
    )(out_a, o_f, o_b, gb, gates, x2d, g1, sc2, sh2, wa, wb, wo, nw, lg, lb, rw_t)


def _first_argmax(vals, idx, n):
    m = jnp.max(vals, axis=0, keepdims=True)
    first = jnp.min(jnp.where(vals == m, idx, float(n)), axis=0, keepdims=True)
    return m, first


def _route_kernel(sc_ref, bias_ref, e_ref, w_ref, rank_ref, cnt_ref, carry_ref):
    step = pl.program_id(0)
    tr = sc_ref.shape[1]

    @pl.when(step == 0)
    def _():
        carry_ref[...] = jnp.zeros(carry_ref.shape, F32)

    scores = sc_ref[...]
    biased = scores + bias_ref[...]
    eidx = lax.broadcasted_iota(jnp.int32, (N_EXPERTS, tr), 0).astype(F32)
    lidx = lax.broadcasted_iota(jnp.int32, (GROUP_SIZE, tr), 0).astype(F32)
    gidx = lax.broadcasted_iota(jnp.int32, (N_GROUPS, tr), 0).astype(F32)
    gs = []
    for g in range(N_GROUPS):
        blk = biased[g * GROUP_SIZE:(g + 1) * GROUP_SIZE]
        m1, first = _first_argmax(blk, lidx, GROUP_SIZE)
        m2 = jnp.max(jnp.where(lidx == first, -jnp.inf, blk), axis=0, keepdims=True)
        gs.append(m1 + m2)
    cur = jnp.concatenate(gs, axis=0)
    keep = jnp.zeros((N_GROUPS, tr), F32)
    for _ in range(TOPK_GROUPS):
        _, first = _first_argmax(cur, gidx, N_GROUPS)
        sel = gidx == first
        keep = jnp.where(sel, 1.0, keep)
        cur = jnp.where(sel, -jnp.inf, cur)
    keep_e = jnp.concatenate([jnp.broadcast_to(keep[g:g + 1], (GROUP_SIZE, tr)) for g in range(N_GROUPS)], axis=0)
    masked = jnp.where(keep_e > 0.5, biased, -jnp.inf)
    chosen = jnp.zeros((N_EXPERTS, tr), F32)
    tops, topi = [], []
    for _ in range(TOP_K):
        _, first = _first_argmax(masked, eidx, N_EXPERTS)
        sel = eidx == first
        tops.append(jnp.sum(jnp.where(sel, scores, 0.0), axis=0, keepdims=True))
        topi.append(first)
        chosen = jnp.where(sel, 1.0, chosen)
        masked = jnp.where(sel, -jnp.inf, masked)
    top_s = jnp.concatenate(tops, axis=0)
    top_i = jnp.concatenate(topi, axis=0)
    e_ref[...] = top_i.astype(jnp.int32)
    w_ref[...] = top_s / jnp.sum(top_s, axis=0, keepdims=True) * ROUTED_SCALE
    r = lax.broadcasted_iota(jnp.int32, (tr, tr), 0)
    cidx = lax.broadcasted_iota(jnp.int32, (tr, tr), 1)
    before = jnp.where(r < cidx, 1.0, 0.0).astype(BF16)
    prior = jnp.dot(chosen.astype(BF16), before, preferred_element_type=F32) + carry_ref[...]
    ranks = [jnp.sum(jnp.where(eidx == topi[kk], prior, 0.0), axis=0, keepdims=True) for kk in range(TOP_K)]
    rank_ref[...] = jnp.concatenate(ranks, axis=0).astype(jnp.int32)
    carry_ref[...] = carry_ref[...] + jnp.sum(chosen, axis=1, keepdims=True)
    cnt_ref[...] = jnp.broadcast_to(carry_ref[...], cnt_ref.shape).astype(jnp.int32)


def _route(scores_t, router_bias, tile):
    n = scores_t.shape[1]
    col = lambda i: (0, i)
    out_shape = [jax.ShapeDtypeStruct((TOP_K, n), jnp.int32), jax.ShapeDtypeStruct((TOP_K, n), F32),
                 jax.ShapeDtypeStruct((TOP_K, n), jnp.int32), jax.ShapeDtypeStruct((N_EXPERTS, LANES), jnp.int32)]
    return pl.pallas_call(
        _route_kernel, out_shape=out_shape, grid=(n // tile,),
        in_specs=[pl.BlockSpec((N_EXPERTS, tile), col), pl.BlockSpec((N_EXPERTS, 1), lambda i: (0, 0))],
        out_specs=[pl.BlockSpec((TOP_K, tile), col), pl.BlockSpec((TOP_K, tile), col),
                   pl.BlockSpec((TOP_K, tile), col), pl.BlockSpec((N_EXPERTS, LANES), lambda i: (0, 0))],
        scratch_shapes=[pltpu.VMEM((N_EXPERTS, 1), F32)],
        compiler_params=_params(1), name="route",
    )(scores_t, router_bias.reshape(N_EXPERTS, 1))


def _slots_kernel(e_ref, rank_ref, start_ref, dest_ref):
    tr = e_ref.shape[1]
    eidx = lax.broadcasted_iota(jnp.int32, (N_EXPERTS, tr), 0)
    e = e_ref[...]
    start = start_ref[...]
    rows = [jnp.sum(jnp.where(eidx == e[kk:kk + 1], start, 0.0), axis=0, keepdims=True) for kk in range(TOP_K)]
    dest_ref[...] = jnp.concatenate(rows, axis=0).astype(jnp.int32) + rank_ref[...]


def _slots(top_e, rank, start_rows, tile):
    n = top_e.shape[1]
    col = lambda i: (0, i)
    return pl.pallas_call(
        _slots_kernel, out_shape=jax.ShapeDtypeStruct((TOP_K, n), jnp.int32), grid=(n // tile,),
        in_specs=[pl.BlockSpec((TOP_K, tile), col), pl.BlockSpec((TOP_K, tile), col),
                  pl.BlockSpec((N_EXPERTS, 1), lambda i: (0, 0))],
        out_specs=pl.BlockSpec((TOP_K, tile), col),
        compiler_params=_params(1), name="slots",
    )(top_e, rank, start_rows)


def _sc_mesh():
    return plsc.VectorSubcoreMesh(core_axis_name="c", subcore_axis_name="s",
                                  num_cores=SC_CORES, num_subcores=SC_SUBCORES)


def _sc_scatter_rows(rows, idx, n_out):
    n, dp = rows.shape
    copies = idx.shape[0] // n
    per_worker = n // (SC_CORES * SC_SUBCORES)
    chunk = SC_GATHER_ROWS

    @functools.partial(
        pl.kernel, mesh=_sc_mesh(), out_type=jax.ShapeDtypeStruct((n_out, dp), rows.dtype),
        scratch_types=[pltpu.VMEM((chunk,), jnp.int32), pltpu.VMEM((chunk, dp), rows.dtype)],
        name="sc_scatter_rows")
    def scatter(rows_hbm, idx_hbm, out_hbm, idx_v, rows_v):
        base = (lax.axis_index("s") * SC_CORES + lax.axis_index("c")) * per_worker

        @pl.loop(0, per_worker // chunk)
        def _(j):
            off = base + j * chunk
            pltpu.sync_copy(rows_hbm.at[pl.ds(off, chunk)], rows_v)
            for k in range(copies):
                pltpu.sync_copy(idx_hbm.at[pl.ds(k * n + off, chunk)], idx_v)
                pltpu.sync_copy(rows_v, out_hbm.at[idx_v])

    return scatter(rows, idx)


def _expert_kernel(start_ref, cnt_ref, nused_ref, xs_ref, wg_ref, wu_ref, wd_ref, y_ref,
                   wgb, wub, wdb, xbuf, ybuf, sem_in, sem_out):
    e = pl.program_id(0)
    bm, ring = EXPERT_BLOCK, EXPERT_RING
    cnt = cnt_ref[e]
    nb = (cnt + bm - 1) // bm
    g0 = start_ref[e]
    n_used = nused_ref[0]

    def x_copy(g):
        slot = g % ring
        return pltpu.make_async_copy(xs_ref.at[pl.ds(g * bm, bm)], xbuf.at[slot], sem_in.at[slot])

    def y_copy(g):
        slot = g % ring
        return pltpu.make_async_copy(ybuf.at[slot], y_ref.at[pl.ds(g * bm, bm)], sem_out.at[slot])

    @pl.when(e == 0)
    def _():
        for g in range(ring):
            @pl.when(g < n_used)
            def _():
                x_copy(g).start()

    wgb[...] = wg_ref[...].astype(BF16)
    wub[...] = wu_ref[...].astype(BF16)
    wdb[...] = wd_ref[...].astype(BF16)
    row = lax.broadcasted_iota(jnp.int32, (bm, xbuf.shape[2]), 0)

    def block(j, carry):
        g = g0 + j
        slot = g % ring
        x_copy(g).wait()

        @pl.when(g >= ring)
        def _():
            y_copy(g - ring).wait()

        x = _unpack_pairs(jnp.where(row < cnt - j * bm, xbuf[slot], jnp.uint32(0))).astype(BF16)
        gate = jnp.dot(x, wgb[...], preferred_element_type=F32)
        up = jnp.dot(x, wub[...], preferred_element_type=F32)
        ybuf[slot] = _pack_pairs(jnp.dot((_silu(gate) * up).astype(BF16), wdb[...], preferred_element_type=F32))
        y_copy(g).start()

        @pl.when(g + ring < n_used)
        def _():
            x_copy(g + ring).start()
        return carry

    lax.fori_loop(0, nb, block, 0)

    @pl.when(e == pl.num_programs(0) - 1)
    def _():
        for back in range(1, ring + 1):
            @pl.when(n_used - back >= 0)
            def _():
                y_copy(n_used - back).wait()


def _experts(blk_start, counts, n_used, xs, wg, wu, wd):
    n_slots, dp = xs.shape
    n_exp, d, ff = wg.shape
    bm, ring = EXPERT_BLOCK, EXPERT_RING
    wsel = lambda e, st, ct, nu: (e, 0, 0)
    grid_spec = pltpu.PrefetchScalarGridSpec(
        num_scalar_prefetch=3, grid=(n_exp,),
        in_specs=[pl.BlockSpec(memory_space=pl.ANY),
                  pl.BlockSpec((None, d, ff), wsel), pl.BlockSpec((None, d, ff), wsel),
                  pl.BlockSpec((None, ff, d), wsel)],
        out_specs=pl.BlockSpec(memory_space=pl.ANY),
        scratch_shapes=[pltpu.VMEM((d, ff), BF16), pltpu.VMEM((d, ff), BF16), pltpu.VMEM((ff, d), BF16),
                        pltpu.VMEM((ring, bm, dp), U32), pltpu.VMEM((ring, bm, dp), U32),
                        pltpu.SemaphoreType.DMA((ring,)), pltpu.SemaphoreType.DMA((ring,))])
    return pl.pallas_call(
        _expert_kernel, out_shape=jax.ShapeDtypeStruct((n_slots, dp), U32), grid_spec=grid_spec,
        compiler_params=_params(1), name="experts",
    )(blk_start, counts, n_used, xs, wg, wu, wd)


def _sc_gather_rows(table, idx):
    n_idx = idx.shape[0]
    dp = table.shape[1]
    workers = SC_CORES * SC_SUBCORES
    per_worker = n_idx // workers
    chunk = SC_GATHER_ROWS

    @functools.partial(
        pl.kernel, mesh=_sc_mesh(), out_type=jax.ShapeDtypeStruct((n_idx, dp), table.dtype),
        scratch_types=[pltpu.VMEM((chunk,), jnp.int32), pltpu.VMEM((chunk, dp), table.dtype),
                       pltpu.SemaphoreType.DMA],
        name="sc_gather_rows")
    def gather(table_hbm, idx_hbm, out_hbm, idx_v, rows_v, sem):
        base = (lax.axis_index("s") * SC_CORES + lax.axis_index("c")) * per_worker

        @pl.loop(0, per_worker // chunk)
        def _(j):
            off = base + j * chunk
            pltpu.sync_copy(idx_hbm.at[pl.ds(off, chunk)], idx_v)
            pltpu.async_copy(table_hbm.at[idx_v], rows_v, sem).wait()
            pltpu.sync_copy(rows_v, out_hbm.at[pl.ds(off, chunk)])

    return gather(table, idx)


def _combine_kernel(tile, yt_ref, w_ref, h_ref, x1_ref, g2_ref, sg_ref, su_ref, sd_ref, lg_ref, lb_ref, o_ref):
    hb = _unpack_pairs(h_ref[...]).astype(BF16)
    g = jnp.dot(hb, sg_ref[...], preferred_element_type=F32)
    u = jnp.dot(hb, su_ref[...], preferred_element_type=F32)
    f = jnp.dot((_silu(g) * u).astype(BF16), sd_ref[...], preferred_element_type=F32)
    w = w_ref[...]
    for kk in range(TOP_K):
        f = f + w[:, kk:kk + 1] * _unpack_pairs(yt_ref[kk * tile:(kk + 1) * tile, :])
    o_ref[...] = _layer_norm(DEEPNORM_ALPHA * x1_ref[...] + g2_ref[0] * f, lg_ref[...], lb_ref[...])


def _combine(y_tok, top_w, h2p, x1, g2, sg, su, sd, lg, lb, tile, tiles_per_batch):
    n, d = x1.shape
    dp = h2p.shape[1]
    row = lambda i: (i, 0)
    full = lambda i: (0, 0)

    def const(a):
        return pl.BlockSpec(a.shape, full, pipeline_mode=pl.Buffered(1))

    return pl.pallas_call(
        functools.partial(_combine_kernel, tile),
        out_shape=jax.ShapeDtypeStruct((n, d), F32),
        grid=(n // tile,),
        in_specs=[pl.BlockSpec((TOP_K * tile, dp), row),
                  pl.BlockSpec((tile, TOP_K), row), pl.BlockSpec((tile, dp), row), pl.BlockSpec((tile, d), row),
                  pl.BlockSpec((1, 1, d), lambda i: (i // tiles_per_batch, 0, 0)),
                  const(sg), const(su), const(sd), const(lg), const(lb)],
        out_specs=pl.BlockSpec((tile, d), row),
        compiler_params=_params(1), name="combine_shared_ln2",
    )(y_tok, top_w, h2p, x1, g2, sg, su, sd, lg, lb)


def kernel(x, c, ctx, c_ctx, w_mod, b_mod, w_in, na_rpb, gla_w_decay_f, gla_b_decay_f, gla_w_decay_b, gla_b_decay_b,
           gla_norm_w, w_branch_a, w_branch_b, w_out, ln1_g, ln1_b, router_w, router_bias, exp_w_gate, exp_w_up,
           exp_w_down, sh_w_gate, sh_w_up, sh_w_down, ln2_g, ln2_b):
    batch, t_len, d = x.shape
    ctx_len = ctx.shape[1]
    n = batch * t_len
    assert w_mod.shape[0] == DEPTH == 1
    assert t_len % GLA_CHUNK == 0 and ctx_len == GLA_CHUNK and (t_len // GRID_W) % NA_ROW_BLOCK == 0

    mod_rows = 16
    c_all = jnp.zeros((mod_rows, d), F32).at[:batch].set(c).at[batch].set(c_ctx)
    mod = _modulation(c_all, w_mod[0], b_mod[0])
    sh1, sc1, g1, sh2, sc2, g2 = [mod[:batch, j * d:(j + 1) * d].reshape(batch, 1, d) for j in range(6)]
    sh1c = mod[batch:batch + 1, 0:d].reshape(1, 1, d)
    sc1c = mod[batch:batch + 1, d:2 * d].reshape(1, 1, d)

    offs = np.cumsum((0, NA_WIDTH, NA_WIDTH, NA_WIDTH, GLA_QK_WIDTH, GLA_QK_WIDTH, GLA_V_WIDTH, GLA_V_WIDTH,
                      GLA_GATE_RANK, GLA_GATE_RANK, d, d))
    qa, ka, va, qb, kb, vbc, gbc, lrf, lrb, ga, gbt = [w_in[0][:, offs[j]:offs[j + 1]] for j in range(11)]
    lr_cols = jnp.concatenate([lrf, lrb, jnp.zeros((d, LANES - 2 * GLA_GATE_RANK), F32)], axis=1)
    w_lat = jnp.concatenate([qa, ka, va, qb, kb, vbc, gbc, ga, gbt, lr_cols], axis=1).astype(BF16)
    w_ctx = jnp.concatenate([ka, va, kb, vbc, lr_cols], axis=1).astype(BF16)
    plain = ("plain",)
    lat_plan = ((3 * NA_WIDTH, (("scale", NA_HEAD_DIM ** -0.5), plain, plain)),
                (2 * GLA_QK_WIDTH, (("rope", GLA_DK ** -0.5), ("rope", 1.0))),
                (GLA_V_WIDTH, (plain, plain)), (GLA_V_WIDTH, (plain, plain)),
                (2 * d, (plain,) * 4), (LANES, (plain,)))
    ctx_plan = ((2 * NA_WIDTH, (plain, plain)), (GLA_QK_WIDTH, (plain,)), (GLA_V_WIDTH, (plain, plain)),
                (LANES, (plain,)))
    tile = 256
    x2d = x.reshape(n, d)
    na_qkv, gla_qk, vb, gb, gates, lr = _projection(
        x2d, sc1, sh1, w_lat, lat_plan, (BF16, BF16, BF16, BF16, BF16, F32), tile, t_len // tile,
        rope=_rope_tables(t_len))
    na_kv_c, k_c, v_c, lr_c = _projection(
        ctx.reshape(batch * ctx_len, d), sc1c, sh1c, w_ctx, ctx_plan, (BF16, BF16, BF16, F32), tile,
        batch * ctx_len // tile)

    out_a = _na_attention(na_qkv, na_kv_c, _na_bias_tables(na_rpb[0]), batch, t_len, ctx_len)

    wf, bf = _decay_weights(gla_w_decay_f[0], gla_b_decay_f[0], 0)
    wb, bb = _decay_weights(gla_w_decay_b[0], gla_b_decay_b[0], GLA_GATE_RANK)
    o_f, o_b = _gla(gla_qk, vb, lr, k_c, v_c, lr_c, wf, bf, wb, bb, batch, t_len, ctx_len)

    x1, h2p, scores_t = _merge(
        out_a.reshape(n, NA_WIDTH), o_f.reshape(n, GLA_V_WIDTH), o_b.reshape(n, GLA_V_WIDTH), gb, gates, x2d,
        g1, sc2, sh2, w_branch_a[0].astype(BF16), w_branch_b[0].astype(BF16), w_out[0].astype(BF16),
        gla_norm_w[0].reshape(1, -1), ln1_g[0].reshape(1, d), ln1_b[0].reshape(1, d),
        router_w[0].T.astype(BF16), tile, t_len // tile)

    top_e, top_w, rank, counts = _route(scores_t, router_bias[0], 512)

    counts = counts[:, 0]
    n_blocks = n * TOP_K // EXPERT_BLOCK + N_EXPERTS
    blocks_per = (counts + EXPERT_BLOCK - 1) // EXPERT_BLOCK
    blk_end = jnp.cumsum(blocks_per)
    blk_start = blk_end - blocks_per
    dest = _slots(top_e, rank, (blk_start * EXPERT_BLOCK).astype(F32).reshape(N_EXPERTS, 1), 512)

    xs = _sc_scatter_rows(h2p, dest.reshape(TOP_K * n), n_blocks * EXPERT_BLOCK)
    y = _experts(blk_start.astype(jnp.int32), counts, blk_end[-1:].astype(jnp.int32), xs,
                 exp_w_gate[0], exp_w_up[0], exp_w_down[0])
    dest_tok = dest.reshape(TOP_K, n // tile, tile).transpose(1, 0, 2).reshape(n * TOP_K)
    y_tok = _sc_gather_rows(y, dest_tok)
    out = _combine(y_tok, top_w.T, h2p, x1, g2, sh_w_gate[0].astype(BF16), sh_w_up[0].astype(BF16),
                   sh_w_down[0].astype(BF16), ln2_g[0].reshape(1, d), ln2_b[0].reshape(1, d), tile, t_len // tile)
    return out.reshape(batch, t_len, d)
```

```python
import functools

import numpy as np
import jax
import jax.numpy as jnp
from jax import lax
from jax.experimental import pallas as pl
from jax.experimental.pallas import tpu as pltpu
from jax.experimental.pallas import tpu_sc as plsc

F32 = jnp.float32
BF16 = jnp.bfloat16
U32 = jnp.uint32
HIGHEST = lax.Precision.HIGHEST

GRID_W = 64
NA_HEADS = 8
NA_HEAD_DIM = 64
NA_WIDTH = NA_HEADS * NA_HEAD_DIM
WIN_ROWS = 8
WIN_COLS = 16
GLA_HEADS = 4
GLA_DK = 128
GLA_DV = 256
GLA_QK_WIDTH = GLA_HEADS * GLA_DK
GLA_V_WIDTH = GLA_HEADS * GLA_DV
GLA_GATE_RANK = 16
GLA_TAU = 16.0
LOG2E = 1.4426950408889634
ROPE_BASE = 10000.0
N_EXPERTS = 256
TOP_K = 8
N_GROUPS = 8
TOPK_GROUPS = 4
GROUP_SIZE = N_EXPERTS // N_GROUPS
ROUTED_SCALE = 2.5
DEPTH = 1
DEEPNORM_ALPHA = (2 * DEPTH) ** 0.25
EPS = 1e-6

LANES = 128
NA_ROW_BLOCK = 4
NA_UNION_ROWS = NA_ROW_BLOCK + WIN_ROWS - 1
GLA_CHUNK = 256
GLA_DIAG = 16
GLA_HEADS_PER_STEP = 4
EXPERT_BLOCK = 256
EXPERT_RING = 4
SC_CORES = 2
SC_SUBCORES = 16
SC_GATHER_ROWS = 128
NEG_BIG = -1e30
VMEM_LIMIT = 56 * 1024 * 1024


def _params(n_axes, vmem=VMEM_LIMIT):
    return pltpu.CompilerParams(dimension_semantics=("arbitrary",) * n_axes, vmem_limit_bytes=vmem)


def _sigmoid(v):
    return 1.0 / (1.0 + jnp.exp(-v))


def _silu(v):
    return v * _sigmoid(v)


def _pack_pairs(v):
    m = v.shape[1] // 2
    lo = lax.bitcast_convert_type(v[:, :m].astype(BF16).astype(F32), U32) >> 16
    hi = lax.bitcast_convert_type(v[:, m:].astype(BF16).astype(F32), U32) & jnp.uint32(0xFFFF0000)
    return lo | hi


def _unpack_pairs(p):
    lo = lax.bitcast_convert_type(p << 16, F32)
    hi = lax.bitcast_convert_type(p & jnp.uint32(0xFFFF0000), F32)
    return jnp.concatenate([lo, hi], axis=1)


def _mod_kernel(c_ref, w_ref, b_ref, o_ref):
    o_ref[...] = jnp.dot(_silu(c_ref[...]), w_ref[...], preferred_element_type=F32, precision=HIGHEST) + b_ref[...]


def _modulation(c_all, w_mod, b_mod):
    rows, d = c_all.shape
    n = w_mod.shape[1]
    bn = 512
    return pl.pallas_call(
        _mod_kernel,
        out_shape=jax.ShapeDtypeStruct((rows, n), F32),
        grid=(n // bn,),
        in_specs=[pl.BlockSpec((rows, d), lambda j: (0, 0)),
                  pl.BlockSpec((d, bn), lambda j: (0, j)),
                  pl.BlockSpec((1, bn), lambda j: (0, j))],
        out_specs=pl.BlockSpec((rows, bn), lambda j: (0, j)),
        compiler_params=_params(1),
        name="modulation",
    )(c_all, w_mod, b_mod.reshape(1, n))


def _swap32(v):
    lane = lax.broadcasted_iota(jnp.int32, v.shape, 1)
    return jnp.where((lane % 64) < 32, pltpu.roll(v, 96, 1), pltpu.roll(v, 32, 1))


GLA_DECAY_WIDTH = 2 * GLA_HEADS * 2 * GLA_DK


def _log2_decay_split(lr, w2, b2, out_ref):
    z = (jnp.dot(lr.astype(BF16), w2, preferred_element_type=F32) + b2) * LOG2E
    a = (jnp.minimum(z, 0.0) - jnp.log2(1.0 + jnp.exp2(-jnp.abs(z)))) * (1.0 / GLA_TAU)
    hi = a.astype(BF16)
    lo = (a - hi.astype(F32)).astype(BF16)
    for p in range(a.shape[1] // GLA_DK):
        src = slice(p * GLA_DK, (p + 1) * GLA_DK)
        out_ref[:, 2 * p * GLA_DK:(2 * p + 1) * GLA_DK] = hi[:, src]
        out_ref[:, (2 * p + 1) * GLA_DK:(2 * p + 2) * GLA_DK] = lo[:, src]


def _proj_kernel(plan, has_rope, *refs):
    x_ref, sc_ref, sh_ref, w_ref, w2_ref, b2_ref = refs[:6]
    pos = 6
    if has_rope:
        cos_ref, sin_ref = refs[6:8]
        pos = 8
    out_refs = refs[pos:]
    h = (x_ref[...] * (1.0 + sc_ref[0]) + sh_ref[0]).astype(BF16)
    col = 0
    for out_ref, (width, kinds) in zip(out_refs, plan):
        for j, kind in enumerate(kinds):
            cw = width // len(kinds)
            c0 = j * cw
            acc = jnp.dot(h, w_ref[:, col + c0:col + c0 + cw], preferred_element_type=F32)
            if kind[0] == "decay":
                _log2_decay_split(acc, w2_ref[...], b2_ref[...], out_ref)
                continue
            if kind[0] == "scale":
                acc = acc * kind[1]
            elif kind[0] == "rope":
                cos, sin = cos_ref[...], sin_ref[...]
                pieces = []
                for p in range(cw // LANES):
                    v = acc[:, p * LANES:(p + 1) * LANES]
                    pieces.append((v * cos + _swap32(v) * sin) * kind[1])
                acc = jnp.concatenate(pieces, axis=1)
            out_ref[:, c0:c0 + cw] = acc.astype(out_ref.dtype)
        col += width


def _projection(x2d, sc, sh, w, w2, b2, plan, out_dtypes, tile, tiles_per_mod, rope=None):
    n, d = x2d.shape
    const = lambda a: pl.BlockSpec(a.shape, lambda i: (0, 0), pipeline_mode=pl.Buffered(1))
    in_specs = [pl.BlockSpec((tile, d), lambda i: (i, 0)),
                pl.BlockSpec((1, 1, d), lambda i: (i // tiles_per_mod, 0, 0)),
                pl.BlockSpec((1, 1, d), lambda i: (i // tiles_per_mod, 0, 0)),
                const(w), const(w2), const(b2)]
    args = [x2d, sc, sh, w, w2, b2]
    if rope is not None:
        in_specs += [pl.BlockSpec((tile, LANES), lambda i: (i % tiles_per_mod, 0))] * 2
        args += list(rope)
    widths = [GLA_DECAY_WIDTH if kinds[0][0] == "decay" else wd for wd, kinds in plan]
    out_shape = [jax.ShapeDtypeStruct((n, wd), dt) for wd, dt in zip(widths, out_dtypes)]
    out_specs = [pl.BlockSpec((tile, wd), lambda i: (i, 0)) for wd in widths]
    return pl.pallas_call(
        functools.partial(_proj_kernel, plan, rope is not None),
        out_shape=out_shape,
        grid=(n // tile,),
        in_specs=in_specs,
        out_specs=out_specs,
        compiler_params=_params(1),
        name="in_proj" if rope is not None else "ctx_proj",
    )(*args)


def _rope_tables(t_len):
    half = GLA_DK // 2
    quarter = half // 2
    inv_freq = ROPE_BASE ** (-jnp.arange(quarter, dtype=F32) / quarter)
    pos = jnp.arange(t_len)
    row_ang = (pos // GRID_W).astype(F32)[:, None] * inv_freq[None, :]
    col_ang = (pos % GRID_W).astype(F32)[:, None] * inv_freq[None, :]
    cr, sr, cc, sn = jnp.cos(row_ang), jnp.sin(row_ang), jnp.cos(col_ang), jnp.sin(col_ang)
    cos = jnp.concatenate([cr, cr, cc, cc], axis=1)
    sin = jnp.concatenate([-sr, sr, -sn, sn], axis=1)
    return cos, sin


def _na_bias_tables(rpb):
    rb, ur, w = NA_ROW_BLOCK, NA_UNION_ROWS, GRID_W
    heads = rpb.shape[0]
    pad = jnp.pad(rpb, ((0, 0), (0, 0), (w, w)))
    toep = jnp.stack([pad[:, :, w + WIN_COLS - 1 - c:2 * w + WIN_COLS - 1 - c] for c in range(w)], axis=2)
    c = np.arange(w)[:, None]
    kc = np.arange(w)[None, :]
    col_start = np.clip(c - WIN_COLS // 2, 0, w - WIN_COLS)
    col_ok = (kc >= col_start) & (kc < col_start + WIN_COLS)
    toep = jnp.where(col_ok[None, None], toep, NEG_BIG)
    neg = jnp.full((heads, w, w), NEG_BIG, F32)
    half = WIN_ROWS // 2
    tables = []
    for lo, off in ((lambda i: 0, WIN_ROWS - 1), (lambda i: i, WIN_ROWS - 1 - half), (lambda i: ur - WIN_ROWS, 0)):
        rows_ = []
        for i in range(rb):
            blocks = [toep[:, j - i + off] if lo(i) <= j < lo(i) + WIN_ROWS else neg for j in range(ur)]
            rows_.append(jnp.concatenate(blocks, axis=2))
        tables.append(jnp.concatenate(rows_, axis=1))
    return jnp.stack(tables)


def _na_kernel(rows, q_ref, k_ref, v_ref, kc_ref, vc_ref, bias_ref, o_ref):
    rb = pl.program_id(1)
    ustart = jnp.clip(rb * NA_ROW_BLOCK - WIN_ROWS // 2, 0, rows - NA_UNION_ROWS)
    k0 = pl.multiple_of(ustart * GRID_W, GRID_W)
    nk = NA_UNION_ROWS * GRID_W
    nt = (((1,), (1,)), ((), ()))
    lane = lax.broadcasted_iota(jnp.int32, (NA_ROW_BLOCK * GRID_W, LANES), 1)
    for p in range(NA_WIDTH // LANES):
        ls = slice(p * LANES, (p + 1) * LANES)
        q = q_ref[:, ls]
        ku = k_ref[pl.ds(k0, nk), ls]
        vu = v_ref[pl.ds(k0, nk), ls]
        kc = kc_ref[:, ls]
        vc = vc_ref[:, ls]
        out = jnp.zeros(q.shape, F32)
        for hh in range(LANES // NA_HEAD_DIM):
            mine = (lane // NA_HEAD_DIM) == hh
            qm = jnp.where(mine, q, jnp.zeros_like(q))
            s_loc = lax.dot_general(qm, ku, nt, preferred_element_type=F32) + bias_ref[p * 2 + hh]
            s_ctx = lax.dot_general(qm, kc, nt, preferred_element_type=F32)
            m = jnp.maximum(jnp.max(s_loc, axis=1, keepdims=True), jnp.max(s_ctx, axis=1, keepdims=True))
            p_loc = jnp.exp(s_loc - m)
            p_ctx = jnp.exp(s_ctx - m)
            denom = jnp.sum(p_loc, axis=1, keepdims=True) + jnp.sum(p_ctx, axis=1, keepdims=True)
            o = (jnp.dot(p_loc.astype(BF16), vu, preferred_element_type=F32)
                 + jnp.dot(p_ctx.astype(BF16), vc, preferred_element_type=F32)) / denom
            out = jnp.where(mine, o, out)
        o_ref[:, ls] = out.astype(o_ref.dtype)


def _na_attention(na_qkv, na_kv_ctx, bias_tables, batch, t_len, ctx_len):
    rows = t_len // GRID_W
    n_rb = rows // NA_ROW_BLOCK
    tq = NA_ROW_BLOCK * GRID_W
    qkv = na_qkv.reshape(batch, t_len, 3 * NA_WIDTH)
    kvc = na_kv_ctx.reshape(batch, ctx_len, 2 * NA_WIDTH)

    def bias_idx(b, r):
        return (jnp.where(r == 0, 0, jnp.where(r == n_rb - 1, 2, 1)), 0, 0, 0)

    return pl.pallas_call(
        functools.partial(_na_kernel, rows),
        out_shape=jax.ShapeDtypeStruct((batch, t_len, NA_WIDTH), BF16),
        grid=(batch, n_rb),
        in_specs=[pl.BlockSpec((None, tq, NA_WIDTH), lambda b, r: (b, r, 0)),
                  pl.BlockSpec((None, t_len, NA_WIDTH), lambda b, r: (b, 0, 1)),
                  pl.BlockSpec((None, t_len, NA_WIDTH), lambda b, r: (b, 0, 2)),
                  pl.BlockSpec((None, ctx_len, NA_WIDTH), lambda b, r: (b, 0, 0)),
                  pl.BlockSpec((None, ctx_len, NA_WIDTH), lambda b, r: (b, 0, 1)),
                  pl.BlockSpec((None,) + bias_tables.shape[1:], bias_idx)],
        out_specs=pl.BlockSpec((None, tq, NA_WIDTH), lambda b, r: (b, r, 0)),
        compiler_params=_params(2),
        name="na_attention",
    )(qkv, qkv, qkv, kvc, kvc, bias_tables)


def _gla_constants(c):
    tris, masks = [], []
    for reverse in (False, True):
        i = np.arange(c)[:, None]
        j = np.arange(c)[None, :]
        tris.append((j >= i) if reverse else (j <= i))
        i = np.arange(c // 2)[:, None]
        j = np.arange(c // 2)[None, :]
        if reverse:
            i, j = j, i
        level = []
        s = c // 4
        while s >= GLA_DIAG:
            level.append(((i // (2 * s)) == (j // (2 * s))) & ((i % (2 * s)) >= s) & ((j % (2 * s)) < s))
            s //= 2
        level.append(((i // GLA_DIAG) == (j // GLA_DIAG)) & (j <= i))
        masks.append(np.stack(level))
    return jnp.asarray(np.stack(tris), BF16), jnp.asarray(np.stack(masks), F32)


def _block_refs(cum, s, reverse, diag):
    c = cum.shape[0]
    span = s if diag else 2 * s
    parts = []
    for p in range(c // span):
        if diag:
            r = p * span + (span - 1 if reverse else 0)
        else:
            r = p * span + (s - 1 if reverse else s)
        parts.append(jnp.broadcast_to(cum[r:r + 1, :], (span, cum.shape[1])))
    return jnp.concatenate(parts, axis=0)


def _cumsum_rows(a_hl, tri):
    parts = jnp.dot(tri, a_hl, preferred_element_type=F32)
    w = a_hl.shape[1] // 2
    return parts[:, :w] + parts[:, w:]


def _gla_chunk(q, k, v, a, state_t, tri, mask_ref, reverse, want_out):
    c = k.shape[0]
    hc = c // 2
    nt = (((1,), (1,)), ((), ()))
    k = k.astype(F32)
    q = q.astype(F32) if want_out else None
    cum = _cumsum_rows(a, tri)
    last = 0 if reverse else c - 1
    total = cum[last:last + 1, :]

    def scale(x, log2_factor):
        return (x * jnp.exp2(log2_factor)).astype(BF16)

    out = None
    if want_out:
        out = lax.dot_general(scale(q, cum), state_t.astype(BF16), nt, preferred_element_type=F32)
        halves = (slice(hc, c), slice(0, hc)) if reverse else (slice(0, hc), slice(hc, c))
        early, late = halves
        r = hc - 1 if reverse else hc
        g = cum[r:r + 1, :]
        cross = lax.dot_general(scale(q[late], cum[late] - g), scale(k[early], g - cum[early]), nt,
                                preferred_element_type=F32)
        inner = []
        for rows in halves:
            cx, qx, kx = cum[rows], q[rows], k[rows]
            acc = jnp.zeros((hc, hc), F32)
            s = hc // 2
            level = 0
            while True:
                diag = s < GLA_DIAG
                d = cx - _block_refs(cx, GLA_DIAG if diag else s, reverse, diag)
                if diag:
                    qs, ks = scale(qx, d), scale(kx, -d)
                else:
                    e = jnp.exp2(-jnp.abs(d))
                    qs, ks = (qx * e).astype(BF16), (kx * e).astype(BF16)
                acc = acc + lax.dot_general(qs, ks, nt, preferred_element_type=F32) * mask_ref[level]
                if diag:
                    break
                s //= 2
                level += 1
            inner.append(acc)
        out_early = jnp.dot(inner[0].astype(BF16), v[early], preferred_element_type=F32)
        out_late = jnp.dot(jnp.concatenate([cross, inner[1]], axis=1).astype(BF16),
                           jnp.concatenate([v[early], v[late]], axis=0), preferred_element_type=F32)
        intra = [out_late, out_early] if reverse else [out_early, out_late]
        out = out + jnp.concatenate(intra, axis=0)
    upd = lax.dot_general(v, scale(k, total - cum), (((0,), (0,)), ((), ())), preferred_element_type=F32)
    return out, jnp.exp2(total) * state_t + upd


def _gla_kernel(qf_ref, kf_ref, vf_ref, af_ref, qb_ref, kb_ref, vb_ref, ab_ref,
                kc_ref, vc_ref, acf_ref, acb_ref, tri_ref, mask_ref,
                of_ref, ob_ref, sf_ref, sb_ref):
    c = pl.program_id(2)
    fm, bm = mask_ref.at[0], mask_ref.at[1]
    heads = sf_ref.shape[0]

    @pl.when(c == 0)
    def _():
        zero = jnp.zeros(sf_ref.shape[1:], F32)
        for h in range(heads):
            kc = kc_ref[:, h * GLA_DK:(h + 1) * GLA_DK]
            vc = vc_ref[:, h * GLA_DV:(h + 1) * GLA_DV]
            hl = slice(2 * h * GLA_DK, 2 * (h + 1) * GLA_DK)
            _, sf_ref[h] = _gla_chunk(None, kc, vc, acf_ref[:, hl], zero, tri_ref[0], fm, False, False)
            _, sb_ref[h] = _gla_chunk(None, kc, vc, acb_ref[:, hl], zero, tri_ref[1], bm, True, False)

    @pl.when(c > 0)
    def _():
        for h in range(heads):
            ks = slice(h * GLA_DK, (h + 1) * GLA_DK)
            vs = slice(h * GLA_DV, (h + 1) * GLA_DV)
            hl = slice(2 * h * GLA_DK, 2 * (h + 1) * GLA_DK)
            o, sf_ref[h] = _gla_chunk(qf_ref[:, ks], kf_ref[:, ks], vf_ref[:, vs], af_ref[:, hl], sf_ref[h],
                                      tri_ref[0], fm, False, True)
            of_ref[:, vs] = o.astype(of_ref.dtype)
            o, sb_ref[h] = _gla_chunk(qb_ref[:, ks], kb_ref[:, ks], vb_ref[:, vs], ab_ref[:, hl], sb_ref[h],
                                      tri_ref[1], bm, True, True)
            ob_ref[:, vs] = o.astype(ob_ref.dtype)


def _gla(gla_qk, vb, decay, k_ctx, v_ctx, decay_ctx, batch, t_len, ctx_len):
    nc = t_len // GLA_CHUNK
    h = GLA_HEADS
    qk = gla_qk.reshape(batch, t_len, 2 * GLA_QK_WIDTH)
    v3 = vb.reshape(batch, t_len, GLA_V_WIDTH)
    a3 = decay.reshape(batch, t_len, GLA_DECAY_WIDTH)
    kc3 = k_ctx.reshape(batch, ctx_len, GLA_QK_WIDTH)
    vc3 = v_ctx.reshape(batch, ctx_len, GLA_V_WIDTH)
    ac3 = decay_ctx.reshape(batch, ctx_len, GLA_DECAY_WIDTH)
    tri, masks = _gla_constants(GLA_CHUNK)

    def fwd(c):
        return jnp.maximum(c - 1, 0)

    def bwd(c):
        return nc - 1 - jnp.maximum(c - 1, 0)

    def const(a):
        return pl.BlockSpec(a.shape, lambda b, hh, c: (0,) * a.ndim, pipeline_mode=pl.Buffered(1))

    hp = GLA_HEADS_PER_STEP
    groups = h // hp
    cq = (None, GLA_CHUNK, hp * GLA_DK)
    cv = (None, GLA_CHUNK, hp * GLA_DV)
    ca = (None, GLA_CHUNK, hp * 2 * GLA_DK)
    cca = (None, ctx_len, hp * 2 * GLA_DK)
    in_specs = [
        pl.BlockSpec(cq, lambda b, g, c: (b, fwd(c), g)),
        pl.BlockSpec(cq, lambda b, g, c: (b, fwd(c), groups + g)),
        pl.BlockSpec(cv, lambda b, g, c: (b, fwd(c), g)),
        pl.BlockSpec(ca, lambda b, g, c: (b, fwd(c), g)),
        pl.BlockSpec(cq, lambda b, g, c: (b, bwd(c), g)),
        pl.BlockSpec(cq, lambda b, g, c: (b, bwd(c), groups + g)),
        pl.BlockSpec(cv, lambda b, g, c: (b, bwd(c), g)),
        pl.BlockSpec(ca, lambda b, g, c: (b, bwd(c), groups + g)),
        pl.BlockSpec((None, ctx_len, hp * GLA_DK), lambda b, g, c: (b, 0, g)),
        pl.BlockSpec((None, ctx_len, hp * GLA_DV), lambda b, g, c: (b, 0, g)),
        pl.BlockSpec(cca, lambda b, g, c: (b, 0, g)),
        pl.BlockSpec(cca, lambda b, g, c: (b, 0, groups + g)),
        const(tri), const(masks),
    ]
    out_specs = [pl.BlockSpec(cv, lambda b, g, c: (b, fwd(c), g)),
                 pl.BlockSpec(cv, lambda b, g, c: (b, bwd(c), g))]
    out_shape = [jax.ShapeDtypeStruct((batch, t_len, GLA_V_WIDTH), BF16)] * 2
    return pl.pallas_call(
        _gla_kernel,
        out_shape=out_shape,
        grid=(batch, groups, nc + 1),
        in_specs=in_specs,
        out_specs=out_specs,
        scratch_shapes=[pltpu.VMEM((hp, GLA_DV, GLA_DK), F32), pltpu.VMEM((hp, GLA_DV, GLA_DK), F32)],
        compiler_params=_params(3),
        name="gla",
    )(qk, qk, v3, a3, qk, qk, v3, a3, kc3, vc3, ac3, ac3, tri, masks)


def _decay_weights(w_f, b_f, w_b, b_b):
    r, width = w_f.shape
    w2 = jnp.zeros((LANES, 2 * width), F32).at[:r, :width].set(w_f).at[r:2 * r, width:].set(w_b)
    return w2.astype(BF16), jnp.concatenate([b_f, b_b]).reshape(1, 2 * width)


def _layer_norm(v, g, b):
    mu = jnp.mean(v, axis=1, keepdims=True)
    var = jnp.mean(jnp.square(v - mu), axis=1, keepdims=True)
    return (v - mu) * lax.rsqrt(var + EPS) * g + b


def _merge_kernel(oa_ref, of_ref, ob_ref, gb_ref, gates_ref, x_ref, g1_ref, sc2_ref, sh2_ref,
                  wa_ref, wb_ref, wo_ref, nw_ref, lg_ref, lb_ref, rw_ref,
                  x1_ref, h2_ref, sc_ref):
    d = x_ref.shape[1]
    o = of_ref[...].astype(F32) + ob_ref[...].astype(F32)
    pieces = []
    for hh in range(GLA_HEADS):
        oh = o[:, hh * GLA_DV:(hh + 1) * GLA_DV]
        pieces.append(oh * lax.rsqrt(jnp.mean(jnp.square(oh), axis=1, keepdims=True) + EPS))
    out_b = jnp.concatenate(pieces, axis=1) * nw_ref[...] * _silu(gb_ref[...].astype(F32))
    ya = jnp.dot(oa_ref[...], wa_ref[...], preferred_element_type=F32)
    yb = jnp.dot(out_b.astype(BF16), wb_ref[...], preferred_element_type=F32)
    y = _sigmoid(gates_ref[:, :d].astype(F32)) * ya + _sigmoid(gates_ref[:, d:].astype(F32)) * yb
    y2 = jnp.dot(y.astype(BF16), wo_ref[...], preferred_element_type=F32)
    x1 = _layer_norm(DEEPNORM_ALPHA * x_ref[...] + g1_ref[0] * y2, lg_ref[...], lb_ref[...])
    x1_ref[...] = x1
    h2 = x1 * (1.0 + sc2_ref[0]) + sh2_ref[0]
    h2_ref[...] = _pack_pairs(h2)
    logits_t = lax.dot_general(rw_ref[...], h2.astype(BF16), (((1,), (1,)), ((), ())), preferred_element_type=F32)
    sc_ref[...] = _sigmoid(logits_t)


def _merge(out_a, o_f, o_b, gb, gates, x2d, g1, sc2, sh2, wa, wb, wo, nw, lg, lb, rw_t, tile, tiles_per_batch):
    n, d = x2d.shape
    row = lambda i: (i, 0)
    mod = lambda i: (i // tiles_per_batch, 0, 0)
    full = lambda i: (0, 0)

    def const(a):
        return pl.BlockSpec(a.shape, full, pipeline_mode=pl.Buffered(1))

    in_specs = [pl.BlockSpec((tile, NA_WIDTH), row), pl.BlockSpec((tile, GLA_V_WIDTH), row),
                pl.BlockSpec((tile, GLA_V_WIDTH), row), pl.BlockSpec((tile, GLA_V_WIDTH), row),
                pl.BlockSpec((tile, 2 * d), row), pl.BlockSpec((tile, d), row),
                pl.BlockSpec((1, 1, d), mod), pl.BlockSpec((1, 1, d), mod), pl.BlockSpec((1, 1, d), mod),
                const(wa), const(wb), const(wo), const(nw), const(lg), const(lb), const(rw_t)]
    out_shape = [jax.ShapeDtypeStruct((n, d), F32), jax.ShapeDtypeStruct((n, d // 2), U32),
                 jax.ShapeDtypeStruct((N_EXPERTS, n), F32)]
    out_specs = [pl.BlockSpec((tile, d), row), pl.BlockSpec((tile, d // 2), row),
                 pl.BlockSpec((N_EXPERTS, tile), lambda i: (0, i))]
    return pl.pallas_call(
        _merge_kernel, out_shape=out_shape, grid=(n // tile,), in_specs=in_specs, out_specs=out_specs,
        compiler_params=_params(1), name="merge_ln1_router",
    )(out_a, o_f, o_b, gb, gates, x2d, g1, sc2, sh2, wa, wb, wo, nw, lg, lb, rw_t)


def _first_argmax(vals, idx, n):
    m = jnp.max(vals, axis=0, keepdims=True)
    first = jnp.min(jnp.where(vals == m, idx, float(n)), axis=0, keepdims=True)
    return m, first


def _route_kernel(sc_ref, bias_ref, e_ref, w_ref, rank_ref, cnt_ref, carry_ref):
    step = pl.program_id(0)
    tr = sc_ref.shape[1]

    @pl.when(step == 0)
    def _():
        carry_ref[...] = jnp.zeros(carry_ref.shape, F32)

    scores = sc_ref[...]
    biased = scores + bias_ref[...]
    eidx = lax.broadcasted_iota(jnp.int32, (N_EXPERTS, tr), 0).astype(F32)
    lidx = lax.broadcasted_iota(jnp.int32, (GROUP_SIZE, tr), 0).astype(F32)
    gidx = lax.broadcasted_iota(jnp.int32, (N_GROUPS, tr), 0).astype(F32)
    gs = []
    for g in range(N_GROUPS):
        blk = biased[g * GROUP_SIZE:(g + 1) * GROUP_SIZE]
        m1, first = _first_argmax(blk, lidx, GROUP_SIZE)
        m2 = jnp.max(jnp.where(lidx == first, -jnp.inf, blk), axis=0, keepdims=True)
        gs.append(m1 + m2)
    cur = jnp.concatenate(gs, axis=0)
    keep = jnp.zeros((N_GROUPS, tr), F32)
    for _ in range(TOPK_GROUPS):
        _, first = _first_argmax(cur, gidx, N_GROUPS)
        sel = gidx == first
        keep = jnp.where(sel, 1.0, keep)
        cur = jnp.where(sel, -jnp.inf, cur)
    keep_e = jnp.concatenate([jnp.broadcast_to(keep[g:g + 1], (GROUP_SIZE, tr)) for g in range(N_GROUPS)], axis=0)
    masked = jnp.where(keep_e > 0.5, biased, -jnp.inf)
    chosen = jnp.zeros((N_EXPERTS, tr), F32)
    tops, topi = [], []
    for _ in range(TOP_K):
        _, first = _first_argmax(masked, eidx, N_EXPERTS)
        sel = eidx == first
        tops.append(jnp.sum(jnp.where(sel, scores, 0.0), axis=0, keepdims=True))
        topi.append(first)
        chosen = jnp.where(sel, 1.0, chosen)
        masked = jnp.where(sel, -jnp.inf, masked)
    top_s = jnp.concatenate(tops, axis=0)
    top_i = jnp.concatenate(topi, axis=0)
    e_ref[...] = top_i.astype(jnp.int32)
    w_ref[...] = top_s / jnp.sum(top_s, axis=0, keepdims=True) * ROUTED_SCALE
    r = lax.broadcasted_iota(jnp.int32, (tr, tr), 0)
    cidx = lax.broadcasted_iota(jnp.int32, (tr, tr), 1)
    before = jnp.where(r < cidx, 1.0, 0.0).astype(BF16)
    prior = jnp.dot(chosen.astype(BF16), before, preferred_element_type=F32) + carry_ref[...]
    ranks = [jnp.sum(jnp.where(eidx == topi[kk], prior, 0.0), axis=0, keepdims=True) for kk in range(TOP_K)]
    rank_ref[...] = jnp.concatenate(ranks, axis=0).astype(jnp.int32)
    carry_ref[...] = carry_ref[...] + jnp.sum(chosen, axis=1, keepdims=True)
    cnt_ref[...] = jnp.broadcast_to(carry_ref[...], cnt_ref.shape).astype(jnp.int32)


def _route(scores_t, router_bias, tile):
    n = scores_t.shape[1]
    col = lambda i: (0, i)
    out_shape = [jax.ShapeDtypeStruct((TOP_K, n), jnp.int32), jax.ShapeDtypeStruct((TOP_K, n), F32),
                 jax.ShapeDtypeStruct((TOP_K, n), jnp.int32), jax.ShapeDtypeStruct((N_EXPERTS, LANES), jnp.int32)]
    return pl.pallas_call(
        _route_kernel, out_shape=out_shape, grid=(n // tile,),
        in_specs=[pl.BlockSpec((N_EXPERTS, tile), col), pl.BlockSpec((N_EXPERTS, 1), lambda i: (0, 0))],
        out_specs=[pl.BlockSpec((TOP_K, tile), col), pl.BlockSpec((TOP_K, tile), col),
                   pl.BlockSpec((TOP_K, tile), col), pl.BlockSpec((N_EXPERTS, LANES), lambda i: (0, 0))],
        scratch_shapes=[pltpu.VMEM((N_EXPERTS, 1), F32)],
        compiler_params=_params(1), name="route",
    )(scores_t, router_bias.reshape(N_EXPERTS, 1))


def _slots_kernel(e_ref, rank_ref, start_ref, dest_ref):
    tr = e_ref.shape[1]
    eidx = lax.broadcasted_iota(jnp.int32, (N_EXPERTS, tr), 0)
    e = e_ref[...]
    start = start_ref[...]
    rows = [jnp.sum(jnp.where(eidx == e[kk:kk + 1], start, 0.0), axis=0, keepdims=True) for kk in range(TOP_K)]
    dest_ref[...] = jnp.concatenate(rows, axis=0).astype(jnp.int32) + rank_ref[...]


def _slots(top_e, rank, start_rows, tile):
    n = top_e.shape[1]
    col = lambda i: (0, i)
    return pl.pallas_call(
        _slots_kernel, out_shape=jax.ShapeDtypeStruct((TOP_K, n), jnp.int32), grid=(n // tile,),
        in_specs=[pl.BlockSpec((TOP_K, tile), col), pl.BlockSpec((TOP_K, tile), col),
                  pl.BlockSpec((N_EXPERTS, 1), lambda i: (0, 0))],
        out_specs=pl.BlockSpec((TOP_K, tile), col),
        compiler_params=_params(1), name="slots",
    )(top_e, rank, start_rows)


def _sc_mesh():
    return plsc.VectorSubcoreMesh(core_axis_name="c", subcore_axis_name="s",
                                  num_cores=SC_CORES, num_subcores=SC_SUBCORES)


def _sc_scatter_rows(rows, idx, n_out):
    n, dp = rows.shape
    copies = idx.shape[0] // n
    per_worker = n // (SC_CORES * SC_SUBCORES)
    chunk = SC_GATHER_ROWS

    @functools.partial(
        pl.kernel, mesh=_sc_mesh(), out_type=jax.ShapeDtypeStruct((n_out, dp), rows.dtype),
        scratch_types=[pltpu.VMEM((chunk,), jnp.int32), pltpu.VMEM((chunk, dp), rows.dtype)],
        name="sc_scatter_rows")
    def scatter(rows_hbm, idx_hbm, out_hbm, idx_v, rows_v):
        base = (lax.axis_index("s") * SC_CORES + lax.axis_index("c")) * per_worker

        @pl.loop(0, per_worker // chunk)
        def _(j):
            off = base + j * chunk
            pltpu.sync_copy(rows_hbm.at[pl.ds(off, chunk)], rows_v)
            for k in range(copies):
                pltpu.sync_copy(idx_hbm.at[pl.ds(k * n + off, chunk)], idx_v)
                pltpu.sync_copy(rows_v, out_hbm.at[idx_v])

    return scatter(rows, idx)


def _expert_kernel(start_ref, cnt_ref, nused_ref, xs_ref, wg_ref, wu_ref, wd_ref, y_ref,
                   wgb, wub, wdb, xbuf, ybuf, sem_in, sem_out):
    e = pl.program_id(0)
    bm, ring = EXPERT_BLOCK, EXPERT_RING
    cnt = cnt_ref[e]
    nb = (cnt + bm - 1) // bm
    g0 = start_ref[e]
    n_used = nused_ref[0]

    def x_copy(g):
        slot = g % ring
        return pltpu.make_async_copy(xs_ref.at[pl.ds(g * bm, bm)], xbuf.at[slot], sem_in.at[slot])

    def y_copy(g):
        slot = g % ring
        return pltpu.make_async_copy(ybuf.at[slot], y_ref.at[pl.ds(g * bm, bm)], sem_out.at[slot])

    @pl.when(e == 0)
    def _():
        for g in range(ring):
            @pl.when(g < n_used)
            def _():
                x_copy(g).start()

    wgb[...] = wg_ref[...].astype(BF16)
    wub[...] = wu_ref[...].astype(BF16)
    wdb[...] = wd_ref[...].astype(BF16)
    row = lax.broadcasted_iota(jnp.int32, (bm, xbuf.shape[2]), 0)

    def block(j, carry):
        g = g0 + j
        slot = g % ring
        x_copy(g).wait()

        @pl.when(g >= ring)
        def _():
            y_copy(g - ring).wait()

        x = _unpack_pairs(jnp.where(row < cnt - j * bm, xbuf[slot], jnp.uint32(0))).astype(BF16)
        gate = jnp.dot(x, wgb[...], preferred_element_type=F32)
        up = jnp.dot(x, wub[...], preferred_element_type=F32)
        ybuf[slot] = _pack_pairs(jnp.dot((_silu(gate) * up).astype(BF16), wdb[...], preferred_element_type=F32))
        y_copy(g).start()

        @pl.when(g + ring < n_used)
        def _():
            x_copy(g + ring).start()
        return carry

    lax.fori_loop(0, nb, block, 0)

    @pl.when(e == pl.num_programs(0) - 1)
    def _():
        for back in range(1, ring + 1):
            @pl.when(n_used - back >= 0)
            def _():
                y_copy(n_used - back).wait()


def _experts(blk_start, counts, n_used, xs, wg, wu, wd):
    n_slots, dp = xs.shape
    n_exp, d, ff = wg.shape
    bm, ring = EXPERT_BLOCK, EXPERT_RING
    wsel = lambda e, st, ct, nu: (e, 0, 0)
    grid_spec = pltpu.PrefetchScalarGridSpec(
        num_scalar_prefetch=3, grid=(n_exp,),
        in_specs=[pl.BlockSpec(memory_space=pl.ANY),
                  pl.BlockSpec((None, d, ff), wsel), pl.BlockSpec((None, d, ff), wsel),
                  pl.BlockSpec((None, ff, d), wsel)],
        out_specs=pl.BlockSpec(memory_space=pl.ANY),
        scratch_shapes=[pltpu.VMEM((d, ff), BF16), pltpu.VMEM((d, ff), BF16), pltpu.VMEM((ff, d), BF16),
                        pltpu.VMEM((ring, bm, dp), U32), pltpu.VMEM((ring, bm, dp), U32),
                        pltpu.SemaphoreType.DMA((ring,)), pltpu.SemaphoreType.DMA((ring,))])
    return pl.pallas_call(
        _expert_kernel, out_shape=jax.ShapeDtypeStruct((n_slots, dp), U32), grid_spec=grid_spec,
        compiler_params=_params(1), name="experts",
    )(blk_start, counts, n_used, xs, wg, wu, wd)


def _sc_gather_rows(table, idx):
    n_idx = idx.shape[0]
    dp = table.shape[1]
    workers = SC_CORES * SC_SUBCORES
    per_worker = n_idx // workers
    chunk = SC_GATHER_ROWS

    @functools.partial(
        pl.kernel, mesh=_sc_mesh(), out_type=jax.ShapeDtypeStruct((n_idx, dp), table.dtype),
        scratch_types=[pltpu.VMEM((chunk,), jnp.int32), pltpu.VMEM((chunk, dp), table.dtype),
                       pltpu.SemaphoreType.DMA],
        name="sc_gather_rows")
    def gather(table_hbm, idx_hbm, out_hbm, idx_v, rows_v, sem):
        base = (lax.axis_index("s") * SC_CORES + lax.axis_index("c")) * per_worker

        @pl.loop(0, per_worker // chunk)
        def _(j):
            off = base + j * chunk
            pltpu.sync_copy(idx_hbm.at[pl.ds(off, chunk)], idx_v)
            pltpu.async_copy(table_hbm.at[idx_v], rows_v, sem).wait()
            pltpu.sync_copy(rows_v, out_hbm.at[pl.ds(off, chunk)])

    return gather(table, idx)


def _combine_kernel(tile, yt_ref, w_ref, h_ref, x1_ref, g2_ref, sg_ref, su_ref, sd_ref, lg_ref, lb_ref, o_ref):
    hb = _unpack_pairs(h_ref[...]).astype(BF16)
    g = jnp.dot(hb, sg_ref[...], preferred_element_type=F32)
    u = jnp.dot(hb, su_ref[...], preferred_element_type=F32)
    f = jnp.dot((_silu(g) * u).astype(BF16), sd_ref[...], preferred_element_type=F32)
    w = w_ref[...]
    for kk in range(TOP_K):
        f = f + w[:, kk:kk + 1] * _unpack_pairs(yt_ref[kk * tile:(kk + 1) * tile, :])
    o_ref[...] = _layer_norm(DEEPNORM_ALPHA * x1_ref[...] + g2_ref[0] * f, lg_ref[...], lb_ref[...])


def _combine(y_tok, top_w, h2p, x1, g2, sg, su, sd, lg, lb, tile, tiles_per_batch):
    n, d = x1.shape
    dp = h2p.shape[1]
    row = lambda i: (i, 0)
    full = lambda i: (0, 0)

    def const(a):
        return pl.BlockSpec(a.shape, full, pipeline_mode=pl.Buffered(1))

    return pl.pallas_call(
        functools.partial(_combine_kernel, tile),
        out_shape=jax.ShapeDtypeStruct((n, d), F32),
        grid=(n // tile,),
        in_specs=[pl.BlockSpec((TOP_K * tile, dp), row),
                  pl.BlockSpec((tile, TOP_K), row), pl.BlockSpec((tile, dp), row), pl.BlockSpec((tile, d), row),
                  pl.BlockSpec((1, 1, d), lambda i: (i // tiles_per_batch, 0, 0)),
                  const(sg), const(su), const(sd), const(lg), const(lb)],
        out_specs=pl.BlockSpec((tile, d), row),
        compiler_params=_params(1), name="combine_shared_ln2",
    )(y_tok, top_w, h2p, x1, g2, sg, su, sd, lg, lb)


def kernel(x, c, ctx, c_ctx, w_mod, b_mod, w_in, na_rpb, gla_w_decay_f, gla_b_decay_f, gla_w_decay_b, gla_b_decay_b,
           gla_norm_w, w_branch_a, w_branch_b, w_out, ln1_g, ln1_b, router_w, router_bias, exp_w_gate, exp_w_up,
           exp_w_down, sh_w_gate, sh_w_up, sh_w_down, ln2_g, ln2_b):
    batch, t_len, d = x.shape
    ctx_len = ctx.shape[1]
    n = batch * t_len
    assert w_mod.shape[0] == DEPTH == 1
    assert t_len % GLA_CHUNK == 0 and ctx_len == GLA_CHUNK and (t_len // GRID_W) % NA_ROW_BLOCK == 0

    mod_rows = 16
    c_all = jnp.zeros((mod_rows, d), F32).at[:batch].set(c).at[batch].set(c_ctx)
    mod = _modulation(c_all, w_mod[0], b_mod[0])
    sh1, sc1, g1, sh2, sc2, g2 = [mod[:batch, j * d:(j + 1) * d].reshape(batch, 1, d) for j in range(6)]
    sh1c = mod[batch:batch + 1, 0:d].reshape(1, 1, d)
    sc1c = mod[batch:batch + 1, d:2 * d].reshape(1, 1, d)

    offs = np.cumsum((0, NA_WIDTH, NA_WIDTH, NA_WIDTH, GLA_QK_WIDTH, GLA_QK_WIDTH, GLA_V_WIDTH, GLA_V_WIDTH,
                      GLA_GATE_RANK, GLA_GATE_RANK, d, d))
    qa, ka, va, qb, kb, vbc, gbc, lrf, lrb, ga, gbt = [w_in[0][:, offs[j]:offs[j + 1]] for j in range(11)]
    lr_cols = jnp.concatenate([lrf, lrb, jnp.zeros((d, LANES - 2 * GLA_GATE_RANK), F32)], axis=1)
    w_lat = jnp.concatenate([lr_cols, qa, ka, va, qb, kb, vbc, gbc, ga, gbt], axis=1).astype(BF16)
    w_ctx = jnp.concatenate([lr_cols, ka, va, kb, vbc], axis=1).astype(BF16)
    plain = ("plain",)
    lat_plan = ((LANES, (("decay",),)),
                (3 * NA_WIDTH, (("scale", NA_HEAD_DIM ** -0.5), plain, plain)),
                (2 * GLA_QK_WIDTH, (("rope", GLA_DK ** -0.5), ("rope", 1.0))),
                (GLA_V_WIDTH, (plain, plain)), (GLA_V_WIDTH, (plain, plain)),
                (2 * d, (plain,) * 4))
    ctx_plan = ((LANES, (("decay",),)),
                (2 * NA_WIDTH, (plain, plain)), (GLA_QK_WIDTH, (plain,)), (GLA_V_WIDTH, (plain, plain)))
    w2, b2 = _decay_weights(gla_w_decay_f[0], gla_b_decay_f[0], gla_w_decay_b[0], gla_b_decay_b[0])
    tile = 256
    x2d = x.reshape(n, d)
    decay, na_qkv, gla_qk, vb, gb, gates = _projection(
        x2d, sc1, sh1, w_lat, w2, b2, lat_plan, (BF16,) * 6, tile, t_len // tile, rope=_rope_tables(t_len))
    decay_c, na_kv_c, k_c, v_c = _projection(
        ctx.reshape(batch * ctx_len, d), sc1c, sh1c, w_ctx, w2, b2, ctx_plan, (BF16,) * 4, tile,
        batch * ctx_len // tile)

    out_a = _na_attention(na_qkv, na_kv_c, _na_bias_tables(na_rpb[0]), batch, t_len, ctx_len)
    o_f, o_b = _gla(gla_qk, vb, decay, k_c, v_c, decay_c, batch, t_len, ctx_len)

    x1, h2p, scores_t = _merge(
        out_a.reshape(n, NA_WIDTH), o_f.reshape(n, GLA_V_WIDTH), o_b.reshape(n, GLA_V_WIDTH), gb, gates, x2d,
        g1, sc2, sh2, w_branch_a[0].astype(BF16), w_branch_b[0].astype(BF16), w_out[0].astype(BF16),
        gla_norm_w[0].reshape(1, -1), ln1_g[0].reshape(1, d), ln1_b[0].reshape(1, d),
        router_w[0].T.astype(BF16), tile, t_len // tile)

    top_e, top_w, rank, counts = _route(scores_t, router_bias[0], 512)

    counts = counts[:, 0]
    n_blocks = n * TOP_K // EXPERT_BLOCK + N_EXPERTS
    blocks_per = (counts + EXPERT_BLOCK - 1) // EXPERT_BLOCK
    blk_end = jnp.cumsum(blocks_per)
    blk_start = blk_end - blocks_per
    dest = _slots(top_e, rank, (blk_start * EXPERT_BLOCK).astype(F32).reshape(N_EXPERTS, 1), 512)

    xs = _sc_scatter_rows(h2p, dest.reshape(TOP_K * n), n_blocks * EXPERT_BLOCK)
    y = _experts(blk_start.astype(jnp.int32), counts, blk_end[-1:].astype(jnp.int32), xs,
                 exp_w_gate[0], exp_w_up[0], exp_w_down[0])
    dest_tok = dest.reshape(TOP_K, n // tile, tile).transpose(1, 0, 2).reshape(n * TOP_K)
    y_tok = _sc_gather_rows(y, dest_tok)
    out = _combine(y_tok, top_w.T, h2p, x1, g2, sh_w_gate[0].astype(BF16), sh_w_up[0].astype(BF16),
                   sh_w_down[0].astype(BF16), ln2_g[0].reshape(1, d), ln2_b[0].reshape(1, d), tile, t_len // tile)
    return out.reshape(batch, t_len, d)
```

```python
import functools

import numpy as np
import jax
import jax.numpy as jnp
from jax import lax
from jax.experimental import pallas as pl
from jax.experimental.pallas import tpu as pltpu
from jax.experimental.pallas import tpu_sc as plsc

F32 = jnp.float32
BF16 = jnp.bfloat16
U32 = jnp.uint32
HIGHEST = lax.Precision.HIGHEST

GRID_W = 64
NA_HEADS = 8
NA_HEAD_DIM = 64
NA_WIDTH = NA_HEADS * NA_HEAD_DIM
WIN_ROWS = 8
WIN_COLS = 16
GLA_HEADS = 4
GLA_DK = 128
GLA_DV = 256
GLA_QK_WIDTH = GLA_HEADS * GLA_DK
GLA_V_WIDTH = GLA_HEADS * GLA_DV
GLA_GATE_RANK = 16
GLA_TAU = 16.0
LOG2E = 1.4426950408889634
ROPE_BASE = 10000.0
N_EXPERTS = 256
TOP_K = 8
N_GROUPS = 8
TOPK_GROUPS = 4
GROUP_SIZE = N_EXPERTS // N_GROUPS
ROUTED_SCALE = 2.5
DEPTH = 1
DEEPNORM_ALPHA = (2 * DEPTH) ** 0.25
EPS = 1e-6

LANES = 128
PROJ_TILE = 512
NA_ROW_BLOCK = 4
NA_UNION_ROWS = NA_ROW_BLOCK + WIN_ROWS - 1
GLA_CHUNK = 256
GLA_DIAG = 16
GLA_HEADS_PER_STEP = 4
EXPERT_BLOCK = 256
EXPERT_SUB = 256
COMBINE_CHUNKS = 4
EXPERT_RING = 4
SC_CORES = 2
SC_SUBCORES = 16
SC_GATHER_ROWS = 128
NEG_BIG = -1e30
VMEM_LIMIT = 56 * 1024 * 1024


def _params(n_axes, vmem=VMEM_LIMIT):
    return pltpu.CompilerParams(dimension_semantics=("arbitrary",) * n_axes, vmem_limit_bytes=vmem)


def _sigmoid(v):
    return 1.0 / (1.0 + jnp.exp(-v))


def _silu(v):
    return v * _sigmoid(v)


def _pack_pairs(v):
    m = v.shape[1] // 2
    lo = lax.bitcast_convert_type(v[:, :m].astype(BF16).astype(F32), U32) >> 16
    hi = lax.bitcast_convert_type(v[:, m:].astype(BF16).astype(F32), U32) & jnp.uint32(0xFFFF0000)
    return lo | hi


def _unpack_pairs(p):
    lo = lax.bitcast_convert_type(p << 16, F32)
    hi = lax.bitcast_convert_type(p & jnp.uint32(0xFFFF0000), F32)
    return jnp.concatenate([lo, hi], axis=1)


def _mod_kernel(c_ref, w_ref, b_ref, o_ref):
    o_ref[...] = jnp.dot(_silu(c_ref[...]), w_ref[...], preferred_element_type=F32, precision=HIGHEST) + b_ref[...]


def _modulation(c_all, w_mod, b_mod):
    rows, d = c_all.shape
    n = w_mod.shape[1]
    bn = 512
    return pl.pallas_call(
        _mod_kernel,
        out_shape=jax.ShapeDtypeStruct((rows, n), F32),
        grid=(n // bn,),
        in_specs=[pl.BlockSpec((rows, d), lambda j: (0, 0)),
                  pl.BlockSpec((d, bn), lambda j: (0, j)),
                  pl.BlockSpec((1, bn), lambda j: (0, j))],
        out_specs=pl.BlockSpec((rows, bn), lambda j: (0, j)),
        compiler_params=_params(1),
        name="modulation",
    )(c_all, w_mod, b_mod.reshape(1, n))


def _swap32(v):
    lane = lax.broadcasted_iota(jnp.int32, v.shape, 1)
    return jnp.where((lane % 64) < 32, pltpu.roll(v, 96, 1), pltpu.roll(v, 32, 1))


GLA_DECAY_WIDTH = 2 * GLA_HEADS * 2 * GLA_DK


def _log2_decay_split(lr, w2, b2, out_ref):
    z = (jnp.dot(lr.astype(BF16), w2, preferred_element_type=F32) + b2) * LOG2E
    a = (jnp.minimum(z, 0.0) - jnp.log2(1.0 + jnp.exp2(-jnp.abs(z)))) * (1.0 / GLA_TAU)
    hi = a.astype(BF16)
    lo = (a - hi.astype(F32)).astype(BF16)
    for p in range(a.shape[1] // GLA_DK):
        src = slice(p * GLA_DK, (p + 1) * GLA_DK)
        out_ref[:, 2 * p * GLA_DK:(2 * p + 1) * GLA_DK] = hi[:, src]
        out_ref[:, (2 * p + 1) * GLA_DK:(2 * p + 2) * GLA_DK] = lo[:, src]


def _proj_kernel(plan, has_rope, *refs):
    x_ref, sc_ref, sh_ref, w_ref, w2_ref, b2_ref = refs[:6]
    pos = 6
    if has_rope:
        cos_ref, sin_ref = refs[6:8]
        pos = 8
    out_refs = refs[pos:]
    h = (x_ref[...] * (1.0 + sc_ref[0]) + sh_ref[0]).astype(BF16)
    col = 0
    for out_ref, (width, kinds) in zip(out_refs, plan):
        for j, kind in enumerate(kinds):
            cw = width // len(kinds)
            c0 = j * cw
            acc = jnp.dot(h, w_ref[:, col + c0:col + c0 + cw], preferred_element_type=F32)
            if kind[0] == "decay":
                _log2_decay_split(acc, w2_ref[...], b2_ref[...], out_ref)
                continue
            if kind[0] == "scale":
                acc = acc * kind[1]
            elif kind[0] == "rope":
                cos, sin = cos_ref[...], sin_ref[...]
                pieces = []
                for p in range(cw // LANES):
                    v = acc[:, p * LANES:(p + 1) * LANES]
                    pieces.append((v * cos + _swap32(v) * sin) * kind[1])
                acc = jnp.concatenate(pieces, axis=1)
            out_ref[:, c0:c0 + cw] = acc.astype(out_ref.dtype)
        col += width


def _projection(x2d, sc, sh, w, w2, b2, plan, out_dtypes, tile, tiles_per_mod, rope=None):
    n, d = x2d.shape
    const = lambda a: pl.BlockSpec(a.shape, lambda i: (0, 0), pipeline_mode=pl.Buffered(1))
    in_specs = [pl.BlockSpec((tile, d), lambda i: (i, 0)),
                pl.BlockSpec((1, 1, d), lambda i: (i // tiles_per_mod, 0, 0)),
                pl.BlockSpec((1, 1, d), lambda i: (i // tiles_per_mod, 0, 0)),
                const(w), const(w2), const(b2)]
    args = [x2d, sc, sh, w, w2, b2]
    if rope is not None:
        in_specs += [pl.BlockSpec((tile, LANES), lambda i: (i % tiles_per_mod, 0))] * 2
        args += list(rope)
    widths = [GLA_DECAY_WIDTH if kinds[0][0] == "decay" else wd for wd, kinds in plan]
    out_shape = [jax.ShapeDtypeStruct((n, wd), dt) for wd, dt in zip(widths, out_dtypes)]
    out_specs = [pl.BlockSpec((tile, wd), lambda i: (i, 0)) for wd in widths]
    return pl.pallas_call(
        functools.partial(_proj_kernel, plan, rope is not None),
        out_shape=out_shape,
        grid=(n // tile,),
        in_specs=in_specs,
        out_specs=out_specs,
        compiler_params=_params(1),
        name="in_proj" if rope is not None else "ctx_proj",
    )(*args)


def _rope_tables(t_len):
    half = GLA_DK // 2
    quarter = half // 2
    inv_freq = ROPE_BASE ** (-jnp.arange(quarter, dtype=F32) / quarter)
    pos = jnp.arange(t_len)
    row_ang = (pos // GRID_W).astype(F32)[:, None] * inv_freq[None, :]
    col_ang = (pos % GRID_W).astype(F32)[:, None] * inv_freq[None, :]
    cr, sr, cc, sn = jnp.cos(row_ang), jnp.sin(row_ang), jnp.cos(col_ang), jnp.sin(col_ang)
    cos = jnp.concatenate([cr, cr, cc, cc], axis=1)
    sin = jnp.concatenate([-sr, sr, -sn, sn], axis=1)
    return cos, sin


def _na_bias_tables(rpb):
    rb, ur, w = NA_ROW_BLOCK, NA_UNION_ROWS, GRID_W
    heads = rpb.shape[0]
    pad = jnp.pad(rpb, ((0, 0), (0, 0), (w, w)))
    toep = jnp.stack([pad[:, :, w + WIN_COLS - 1 - c:2 * w + WIN_COLS - 1 - c] for c in range(w)], axis=2)
    c = np.arange(w)[:, None]
    kc = np.arange(w)[None, :]
    col_start = np.clip(c - WIN_COLS // 2, 0, w - WIN_COLS)
    col_ok = (kc >= col_start) & (kc < col_start + WIN_COLS)
    toep = jnp.where(col_ok[None, None], toep, NEG_BIG)
    neg = jnp.full((heads, w, w), NEG_BIG, F32)
    half = WIN_ROWS // 2
    tables = []
    for lo, off in ((lambda i: 0, WIN_ROWS - 1), (lambda i: i, WIN_ROWS - 1 - half), (lambda i: ur - WIN_ROWS, 0)):
        rows_ = []
        for i in range(rb):
            blocks = [toep[:, j - i + off] if lo(i) <= j < lo(i) + WIN_ROWS else neg for j in range(ur)]
            rows_.append(jnp.concatenate(blocks, axis=2))
        tables.append(jnp.concatenate(rows_, axis=1))
    return jnp.stack(tables)


def _na_kernel(rows, q_ref, k_ref, v_ref, kc_ref, vc_ref, bias_ref, o_ref):
    rb = pl.program_id(1)
    ustart = jnp.clip(rb * NA_ROW_BLOCK - WIN_ROWS // 2, 0, rows - NA_UNION_ROWS)
    k0 = pl.multiple_of(ustart * GRID_W, GRID_W)
    nk = NA_UNION_ROWS * GRID_W
    nt = (((1,), (1,)), ((), ()))
    lane = lax.broadcasted_iota(jnp.int32, (NA_ROW_BLOCK * GRID_W, LANES), 1)
    for p in range(NA_WIDTH // LANES):
        ls = slice(p * LANES, (p + 1) * LANES)
        q = q_ref[:, ls]
        ku = k_ref[pl.ds(k0, nk), ls]
        vu = v_ref[pl.ds(k0, nk), ls]
        kc = kc_ref[:, ls]
        vc = vc_ref[:, ls]
        out = jnp.zeros(q.shape, F32)
        for hh in range(LANES // NA_HEAD_DIM):
            mine = (lane // NA_HEAD_DIM) == hh
            qm = jnp.where(mine, q, jnp.zeros_like(q))
            s_loc = lax.dot_general(qm, ku, nt, preferred_element_type=F32) + bias_ref[p * 2 + hh]
            s_ctx = lax.dot_general(qm, kc, nt, preferred_element_type=F32)
            m = jnp.maximum(jnp.max(s_loc, axis=1, keepdims=True), jnp.max(s_ctx, axis=1, keepdims=True))
            p_loc = jnp.exp(s_loc - m)
            p_ctx = jnp.exp(s_ctx - m)
            denom = jnp.sum(p_loc, axis=1, keepdims=True) + jnp.sum(p_ctx, axis=1, keepdims=True)
            o = (jnp.dot(p_loc.astype(BF16), vu, preferred_element_type=F32)
                 + jnp.dot(p_ctx.astype(BF16), vc, preferred_element_type=F32)) / denom
            out = jnp.where(mine, o, out)
        o_ref[:, ls] = out.astype(o_ref.dtype)


def _na_attention(na_qkv, na_kv_ctx, bias_tables, batch, t_len, ctx_len):
    rows = t_len // GRID_W
    n_rb = rows // NA_ROW_BLOCK
    tq = NA_ROW_BLOCK * GRID_W
    qkv = na_qkv.reshape(batch, t_len, 3 * NA_WIDTH)
    kvc = na_kv_ctx.reshape(batch, ctx_len, 2 * NA_WIDTH)

    def bias_idx(b, r):
        return (jnp.where(r == 0, 0, jnp.where(r == n_rb - 1, 2, 1)), 0, 0, 0)

    return pl.pallas_call(
        functools.partial(_na_kernel, rows),
        out_shape=jax.ShapeDtypeStruct((batch, t_len, NA_WIDTH), BF16),
        grid=(batch, n_rb),
        in_specs=[pl.BlockSpec((None, tq, NA_WIDTH), lambda b, r: (b, r, 0)),
                  pl.BlockSpec((None, t_len, NA_WIDTH), lambda b, r: (b, 0, 1)),
                  pl.BlockSpec((None, t_len, NA_WIDTH), lambda b, r: (b, 0, 2)),
                  pl.BlockSpec((None, ctx_len, NA_WIDTH), lambda b, r: (b, 0, 0)),
                  pl.BlockSpec((None, ctx_len, NA_WIDTH), lambda b, r: (b, 0, 1)),
                  pl.BlockSpec((None,) + bias_tables.shape[1:], bias_idx)],
        out_specs=pl.BlockSpec((None, tq, NA_WIDTH), lambda b, r: (b, r, 0)),
        compiler_params=_params(2),
        name="na_attention",
    )(qkv, qkv, qkv, kvc, kvc, bias_tables)


def _gla_constants(c):
    tris, masks = [], []
    for reverse in (False, True):
        i = np.arange(c)[:, None]
        j = np.arange(c)[None, :]
        tris.append((j >= i) if reverse else (j <= i))
        i = np.arange(c // 2)[:, None]
        j = np.arange(c // 2)[None, :]
        if reverse:
            i, j = j, i
        level = []
        s = c // 4
        while s >= GLA_DIAG:
            level.append(((i // (2 * s)) == (j // (2 * s))) & ((i % (2 * s)) >= s) & ((j % (2 * s)) < s))
            s //= 2
        level.append(((i // GLA_DIAG) == (j // GLA_DIAG)) & (j <= i))
        masks.append(np.stack(level))
    return jnp.asarray(np.stack(tris), BF16), jnp.asarray(np.stack(masks), F32)


def _block_refs(cum, s, reverse, diag):
    c = cum.shape[0]
    span = s if diag else 2 * s
    parts = []
    for p in range(c // span):
        if diag:
            r = p * span + (span - 1 if reverse else 0)
        else:
            r = p * span + (s - 1 if reverse else s)
        parts.append(jnp.broadcast_to(cum[r:r + 1, :], (span, cum.shape[1])))
    return jnp.concatenate(parts, axis=0)


def _cumsum_rows(a_hl, tri):
    parts = jnp.dot(tri, a_hl, preferred_element_type=F32)
    w = a_hl.shape[1] // 2
    return parts[:, :w] + parts[:, w:]


def _gla_chunk(q, k, v, a, state_t, tri, mask_ref, reverse, want_out):
    c = k.shape[0]
    hc = c // 2
    nt = (((1,), (1,)), ((), ()))
    k = k.astype(F32)
    q = q.astype(F32) if want_out else None
    cum = _cumsum_rows(a, tri)
    last = 0 if reverse else c - 1
    total = cum[last:last + 1, :]

    def scale(x, log2_factor):
        return (x * jnp.exp2(log2_factor)).astype(BF16)

    out = None
    if want_out:
        out = lax.dot_general(scale(q, cum), state_t.astype(BF16), nt, preferred_element_type=F32)
        halves = (slice(hc, c), slice(0, hc)) if reverse else (slice(0, hc), slice(hc, c))
        early, late = halves
        r = hc - 1 if reverse else hc
        g = cum[r:r + 1, :]
        cross = lax.dot_general(scale(q[late], cum[late] - g), scale(k[early], g - cum[early]), nt,
                                preferred_element_type=F32)
        inner = []
        for rows in halves:
            cx, qx, kx = cum[rows], q[rows], k[rows]
            acc = jnp.zeros((hc, hc), F32)
            s = hc // 2
            level = 0
            while True:
                diag = s < GLA_DIAG
                d = cx - _block_refs(cx, GLA_DIAG if diag else s, reverse, diag)
                if diag:
                    qs, ks = scale(qx, d), scale(kx, -d)
                else:
                    e = jnp.exp2(-jnp.abs(d))
                    qs, ks = (qx * e).astype(BF16), (kx * e).astype(BF16)
                acc = acc + lax.dot_general(qs, ks, nt, preferred_element_type=F32) * mask_ref[level]
                if diag:
                    break
                s //= 2
                level += 1
            inner.append(acc)
        out_early = jnp.dot(inner[0].astype(BF16), v[early], preferred_element_type=F32)
        out_late = jnp.dot(jnp.concatenate([cross, inner[1]], axis=1).astype(BF16),
                           jnp.concatenate([v[early], v[late]], axis=0), preferred_element_type=F32)
        intra = [out_late, out_early] if reverse else [out_early, out_late]
        out = out + jnp.concatenate(intra, axis=0)
    upd = lax.dot_general(v, scale(k, total - cum), (((0,), (0,)), ((), ())), preferred_element_type=F32)
    return out, jnp.exp2(total) * state_t + upd


def _gla_kernel(qf_ref, kf_ref, vf_ref, af_ref, qb_ref, kb_ref, vb_ref, ab_ref,
                kc_ref, vc_ref, acf_ref, acb_ref, tri_ref, mask_ref,
                of_ref, ob_ref, sf_ref, sb_ref):
    c = pl.program_id(2)
    fm, bm = mask_ref.at[0], mask_ref.at[1]
    heads = sf_ref.shape[0]

    @pl.when(c == 0)
    def _():
        zero = jnp.zeros(sf_ref.shape[1:], F32)
        for h in range(heads):
            kc = kc_ref[:, h * GLA_DK:(h + 1) * GLA_DK]
            vc = vc_ref[:, h * GLA_DV:(h + 1) * GLA_DV]
            hl = slice(2 * h * GLA_DK, 2 * (h + 1) * GLA_DK)
            _, sf_ref[h] = _gla_chunk(None, kc, vc, acf_ref[:, hl], zero, tri_ref[0], fm, False, False)
            _, sb_ref[h] = _gla_chunk(None, kc, vc, acb_ref[:, hl], zero, tri_ref[1], bm, True, False)

    @pl.when(c > 0)
    def _():
        for h in range(heads):
            ks = slice(h * GLA_DK, (h + 1) * GLA_DK)
            vs = slice(h * GLA_DV, (h + 1) * GLA_DV)
            hl = slice(2 * h * GLA_DK, 2 * (h + 1) * GLA_DK)
            o, sf_ref[h] = _gla_chunk(qf_ref[:, ks], kf_ref[:, ks], vf_ref[:, vs], af_ref[:, hl], sf_ref[h],
                                      tri_ref[0], fm, False, True)
            of_ref[:, vs] = o.astype(of_ref.dtype)
            o, sb_ref[h] = _gla_chunk(qb_ref[:, ks], kb_ref[:, ks], vb_ref[:, vs], ab_ref[:, hl], sb_ref[h],
                                      tri_ref[1], bm, True, True)
            ob_ref[:, vs] = o.astype(ob_ref.dtype)


def _gla(gla_qk, vb, decay, k_ctx, v_ctx, decay_ctx, batch, t_len, ctx_len):
    nc = t_len // GLA_CHUNK
    h = GLA_HEADS
    qk = gla_qk.reshape(batch, t_len, 2 * GLA_QK_WIDTH)
    v3 = vb.reshape(batch, t_len, GLA_V_WIDTH)
    a3 = decay.reshape(batch, t_len, GLA_DECAY_WIDTH)
    kc3 = k_ctx.reshape(batch, ctx_len, GLA_QK_WIDTH)
    vc3 = v_ctx.reshape(batch, ctx_len, GLA_V_WIDTH)
    ac3 = decay_ctx.reshape(batch, ctx_len, GLA_DECAY_WIDTH)
    tri, masks = _gla_constants(GLA_CHUNK)

    def fwd(c):
        return jnp.maximum(c - 1, 0)

    def bwd(c):
        return nc - 1 - jnp.maximum(c - 1, 0)

    def const(a):
        return pl.BlockSpec(a.shape, lambda b, hh, c: (0,) * a.ndim, pipeline_mode=pl.Buffered(1))

    hp = GLA_HEADS_PER_STEP
    groups = h // hp
    cq = (None, GLA_CHUNK, hp * GLA_DK)
    cv = (None, GLA_CHUNK, hp * GLA_DV)
    ca = (None, GLA_CHUNK, hp * 2 * GLA_DK)
    cca = (None, ctx_len, hp * 2 * GLA_DK)
    in_specs = [
        pl.BlockSpec(cq, lambda b, g, c: (b, fwd(c), g)),
        pl.BlockSpec(cq, lambda b, g, c: (b, fwd(c), groups + g)),
        pl.BlockSpec(cv, lambda b, g, c: (b, fwd(c), g)),
        pl.BlockSpec(ca, lambda b, g, c: (b, fwd(c), g)),
        pl.BlockSpec(cq, lambda b, g, c: (b, bwd(c), g)),
        pl.BlockSpec(cq, lambda b, g, c: (b, bwd(c), groups + g)),
        pl.BlockSpec(cv, lambda b, g, c: (b, bwd(c), g)),
        pl.BlockSpec(ca, lambda b, g, c: (b, bwd(c), groups + g)),
        pl.BlockSpec((None, ctx_len, hp * GLA_DK), lambda b, g, c: (b, 0, g)),
        pl.BlockSpec((None, ctx_len, hp * GLA_DV), lambda b, g, c: (b, 0, g)),
        pl.BlockSpec(cca, lambda b, g, c: (b, 0, g)),
        pl.BlockSpec(cca, lambda b, g, c: (b, 0, groups + g)),
        const(tri), const(masks),
    ]
    out_specs = [pl.BlockSpec(cv, lambda b, g, c: (b, fwd(c), g)),
                 pl.BlockSpec(cv, lambda b, g, c: (b, bwd(c), g))]
    out_shape = [jax.ShapeDtypeStruct((batch, t_len, GLA_V_WIDTH), BF16)] * 2
    return pl.pallas_call(
        _gla_kernel,
        out_shape=out_shape,
        grid=(batch, groups, nc + 1),
        in_specs=in_specs,
        out_specs=out_specs,
        scratch_shapes=[pltpu.VMEM((hp, GLA_DV, GLA_DK), F32), pltpu.VMEM((hp, GLA_DV, GLA_DK), F32)],
        compiler_params=_params(3),
        name="gla",
    )(qk, qk, v3, a3, qk, qk, v3, a3, kc3, vc3, ac3, ac3, tri, masks)


def _decay_weights(w_f, b_f, w_b, b_b):
    r, width = w_f.shape
    w2 = jnp.zeros((LANES, 2 * width), F32).at[:r, :width].set(w_f).at[r:2 * r, width:].set(w_b)
    return w2.astype(BF16), jnp.concatenate([b_f, b_b]).reshape(1, 2 * width)


def _layer_norm(v, g, b):
    mu = jnp.mean(v, axis=1, keepdims=True)
    var = jnp.mean(jnp.square(v - mu), axis=1, keepdims=True)
    return (v - mu) * lax.rsqrt(var + EPS) * g + b


def _merge_kernel(oa_ref, of_ref, ob_ref, gb_ref, gates_ref, x_ref, g1_ref, sc2_ref, sh2_ref,
                  wa_ref, wb_ref, wo_ref, nw_ref, lg_ref, lb_ref, rw_ref,
                  x1_ref, h2_ref, sc_ref):
    d = x_ref.shape[1]
    o = of_ref[...].astype(F32) + ob_ref[...].astype(F32)
    pieces = []
    for hh in range(GLA_HEADS):
        oh = o[:, hh * GLA_DV:(hh + 1) * GLA_DV]
        pieces.append(oh * lax.rsqrt(jnp.mean(jnp.square(oh), axis=1, keepdims=True) + EPS))
    out_b = jnp.concatenate(pieces, axis=1) * nw_ref[...] * _silu(gb_ref[...].astype(F32))
    ya = jnp.dot(oa_ref[...], wa_ref[...], preferred_element_type=F32)
    yb = jnp.dot(out_b.astype(BF16), wb_ref[...], preferred_element_type=F32)
    y = _sigmoid(gates_ref[:, :d].astype(F32)) * ya + _sigmoid(gates_ref[:, d:].astype(F32)) * yb
    y2 = jnp.dot(y.astype(BF16), wo_ref[...], preferred_element_type=F32)
    x1 = _layer_norm(DEEPNORM_ALPHA * x_ref[...] + g1_ref[0] * y2, lg_ref[...], lb_ref[...])
    x1_ref[...] = x1
    h2 = x1 * (1.0 + sc2_ref[0]) + sh2_ref[0]
    h2_ref[...] = _pack_pairs(h2)
    logits_t = lax.dot_general(rw_ref[...], h2.astype(BF16), (((1,), (1,)), ((), ())), preferred_element_type=F32)
    sc_ref[...] = _sigmoid(logits_t)


def _merge(out_a, o_f, o_b, gb, gates, x2d, g1, sc2, sh2, wa, wb, wo, nw, lg, lb, rw_t, tile, tiles_per_batch):
    n, d = x2d.shape
    row = lambda i: (i, 0)
    mod = lambda i: (i // tiles_per_batch, 0, 0)
    full = lambda i: (0, 0)

    def const(a):
        return pl.BlockSpec(a.shape, full, pipeline_mode=pl.Buffered(1))

    in_specs = [pl.BlockSpec((tile, NA_WIDTH), row), pl.BlockSpec((tile, GLA_V_WIDTH), row),
                pl.BlockSpec((tile, GLA_V_WIDTH), row), pl.BlockSpec((tile, GLA_V_WIDTH), row),
                pl.BlockSpec((tile, 2 * d), row), pl.BlockSpec((tile, d), row),
                pl.BlockSpec((1, 1, d), mod), pl.BlockSpec((1, 1, d), mod), pl.BlockSpec((1, 1, d), mod),
                const(wa), const(wb), const(wo), const(nw), const(lg), const(lb), const(rw_t)]
    out_shape = [jax.ShapeDtypeStruct((n, d), F32), jax.ShapeDtypeStruct((n, d // 2), U32),
                 jax.ShapeDtypeStruct((N_EXPERTS, n), F32)]
    out_specs = [pl.BlockSpec((tile, d), row), pl.BlockSpec((tile, d // 2), row),
                 pl.BlockSpec((N_EXPERTS, tile), lambda i: (0, i))]
    return pl.pallas_call(
        _merge_kernel, out_shape=out_shape, grid=(n // tile,), in_specs=in_specs, out_specs=out_specs,
        compiler_params=_params(1), name="merge_ln1_router",
    )(out_a, o_f, o_b, gb, gates, x2d, g1, sc2, sh2, wa, wb, wo, nw, lg, lb, rw_t)


def _first_argmax(vals, idx, n):
    m = jnp.max(vals, axis=0, keepdims=True)
    first = jnp.min(jnp.where(vals == m, idx, float(n)), axis=0, keepdims=True)
    return m, first


def _route_kernel(sc_ref, bias_ref, e_ref, w_ref, rank_ref, cnt_ref, carry_ref):
    step = pl.program_id(0)
    tr = sc_ref.shape[1]

    @pl.when(step == 0)
    def _():
        carry_ref[...] = jnp.zeros(carry_ref.shape, F32)

    scores = sc_ref[...]
    biased = scores + bias_ref[...]
    eidx = lax.broadcasted_iota(jnp.int32, (N_EXPERTS, tr), 0).astype(F32)
    lidx = lax.broadcasted_iota(jnp.int32, (GROUP_SIZE, tr), 0).astype(F32)
    gidx = lax.broadcasted_iota(jnp.int32, (N_GROUPS, tr), 0).astype(F32)
    gs = []
    for g in range(N_GROUPS):
        blk = biased[g * GROUP_SIZE:(g + 1) * GROUP_SIZE]
        m1, first = _first_argmax(blk, lidx, GROUP_SIZE)
        m2 = jnp.max(jnp.where(lidx == first, -jnp.inf, blk), axis=0, keepdims=True)
        gs.append(m1 + m2)
    cur = jnp.concatenate(gs, axis=0)
    keep = jnp.zeros((N_GROUPS, tr), F32)
    for _ in range(TOPK_GROUPS):
        _, first = _first_argmax(cur, gidx, N_GROUPS)
        sel = gidx == first
        keep = jnp.where(sel, 1.0, keep)
        cur = jnp.where(sel, -jnp.inf, cur)
    keep_e = jnp.concatenate([jnp.broadcast_to(keep[g:g + 1], (GROUP_SIZE, tr)) for g in range(N_GROUPS)], axis=0)
    masked = jnp.where(keep_e > 0.5, biased, -jnp.inf)
    chosen = jnp.zeros((N_EXPERTS, tr), F32)
    tops, topi = [], []
    for _ in range(TOP_K):
        _, first = _first_argmax(masked, eidx, N_EXPERTS)
        sel = eidx == first
        tops.append(jnp.sum(jnp.where(sel, scores, 0.0), axis=0, keepdims=True))
        topi.append(first)
        chosen = jnp.where(sel, 1.0, chosen)
        masked = jnp.where(sel, -jnp.inf, masked)
    top_s = jnp.concatenate(tops, axis=0)
    top_i = jnp.concatenate(topi, axis=0)
    e_ref[...] = top_i.astype(jnp.int32)
    w_ref[...] = top_s / jnp.sum(top_s, axis=0, keepdims=True) * ROUTED_SCALE
    r = lax.broadcasted_iota(jnp.int32, (tr, tr), 0)
    cidx = lax.broadcasted_iota(jnp.int32, (tr, tr), 1)
    before = jnp.where(r < cidx, 1.0, 0.0).astype(BF16)
    prior = jnp.dot(chosen.astype(BF16), before, preferred_element_type=F32) + carry_ref[...]
    ranks = [jnp.sum(jnp.where(eidx == topi[kk], prior, 0.0), axis=0, keepdims=True) for kk in range(TOP_K)]
    rank_ref[...] = jnp.concatenate(ranks, axis=0).astype(jnp.int32)
    carry_ref[...] = carry_ref[...] + jnp.sum(chosen, axis=1, keepdims=True)
    cnt_ref[...] = jnp.broadcast_to(carry_ref[...], cnt_ref.shape).astype(jnp.int32)


def _route(scores_t, router_bias, tile):
    n = scores_t.shape[1]
    col = lambda i: (0, i)
    out_shape = [jax.ShapeDtypeStruct((TOP_K, n), jnp.int32), jax.ShapeDtypeStruct((TOP_K, n), F32),
                 jax.ShapeDtypeStruct((TOP_K, n), jnp.int32), jax.ShapeDtypeStruct((N_EXPERTS, LANES), jnp.int32)]
    return pl.pallas_call(
        _route_kernel, out_shape=out_shape, grid=(n // tile,),
        in_specs=[pl.BlockSpec((N_EXPERTS, tile), col), pl.BlockSpec((N_EXPERTS, 1), lambda i: (0, 0))],
        out_specs=[pl.BlockSpec((TOP_K, tile), col), pl.BlockSpec((TOP_K, tile), col),
                   pl.BlockSpec((TOP_K, tile), col), pl.BlockSpec((N_EXPERTS, LANES), lambda i: (0, 0))],
        scratch_shapes=[pltpu.VMEM((N_EXPERTS, 1), F32)],
        compiler_params=_params(1), name="route",
    )(scores_t, router_bias.reshape(N_EXPERTS, 1))


def _slots_kernel(e_ref, rank_ref, start_ref, dest_ref):
    tr = e_ref.shape[1]
    eidx = lax.broadcasted_iota(jnp.int32, (N_EXPERTS, tr), 0)
    e = e_ref[...]
    start = start_ref[...]
    rows = [jnp.sum(jnp.where(eidx == e[kk:kk + 1], start, 0.0), axis=0, keepdims=True) for kk in range(TOP_K)]
    dest_ref[...] = jnp.concatenate(rows, axis=0).astype(jnp.int32) + rank_ref[...]


def _slots(top_e, rank, start_rows, tile):
    n = top_e.shape[1]
    col = lambda i: (0, i)
    return pl.pallas_call(
        _slots_kernel, out_shape=jax.ShapeDtypeStruct((TOP_K, n), jnp.int32), grid=(n // tile,),
        in_specs=[pl.BlockSpec((TOP_K, tile), col), pl.BlockSpec((TOP_K, tile), col),
                  pl.BlockSpec((N_EXPERTS, 1), lambda i: (0, 0))],
        out_specs=pl.BlockSpec((TOP_K, tile), col),
        compiler_params=_params(1), name="slots",
    )(top_e, rank, start_rows)


def _sc_mesh():
    return plsc.VectorSubcoreMesh(core_axis_name="c", subcore_axis_name="s",
                                  num_cores=SC_CORES, num_subcores=SC_SUBCORES)


def _sc_scatter_rows(rows, idx, n_out):
    n, dp = rows.shape
    copies = idx.shape[0] // n
    per_worker = n // (SC_CORES * SC_SUBCORES)
    chunk = SC_GATHER_ROWS

    @functools.partial(
        pl.kernel, mesh=_sc_mesh(), out_type=jax.ShapeDtypeStruct((n_out, dp), rows.dtype),
        scratch_types=[pltpu.VMEM((chunk,), jnp.int32), pltpu.VMEM((chunk, dp), rows.dtype)],
        name="sc_scatter_rows")
    def scatter(rows_hbm, idx_hbm, out_hbm, idx_v, rows_v):
        base = (lax.axis_index("s") * SC_CORES + lax.axis_index("c")) * per_worker

        @pl.loop(0, per_worker // chunk)
        def _(j):
            off = base + j * chunk
            pltpu.sync_copy(rows_hbm.at[pl.ds(off, chunk)], rows_v)
            for k in range(copies):
                pltpu.sync_copy(idx_hbm.at[pl.ds(k * n + off, chunk)], idx_v)
                pltpu.sync_copy(rows_v, out_hbm.at[idx_v])

    return scatter(rows, idx)


def _expert_kernel(start_ref, cnt_ref, nused_ref, xs_ref, wg_ref, wu_ref, wd_ref, y_ref,
                   wgb, wub, wdb, xbuf, ybuf, sem_in, sem_out):
    e = pl.program_id(0)
    bm, ring = EXPERT_BLOCK, EXPERT_RING
    cnt = cnt_ref[e]
    nb = (cnt + bm - 1) // bm
    g0 = start_ref[e]
    n_used = nused_ref[0]

    def x_copy(g):
        slot = g % ring
        return pltpu.make_async_copy(xs_ref.at[pl.ds(g * bm, bm)], xbuf.at[slot], sem_in.at[slot])

    def y_copy(g):
        slot = g % ring
        return pltpu.make_async_copy(ybuf.at[slot], y_ref.at[pl.ds(g * bm, bm)], sem_out.at[slot])

    @pl.when(e == 0)
    def _():
        for g in range(ring):
            @pl.when(g < n_used)
            def _():
                x_copy(g).start()

    wgb[...] = wg_ref[...].astype(BF16)
    wub[...] = wu_ref[...].astype(BF16)
    wdb[...] = wd_ref[...].astype(BF16)
    row = lax.broadcasted_iota(jnp.int32, (EXPERT_SUB, xbuf.shape[2]), 0)

    def block(j, carry):
        g = g0 + j
        slot = g % ring
        x_copy(g).wait()

        @pl.when(g >= ring)
        def _():
            y_copy(g - ring).wait()

        for sub in range(bm // EXPERT_SUB):
            rows = pl.ds(sub * EXPERT_SUB, EXPERT_SUB)
            valid = cnt - j * bm - sub * EXPERT_SUB
            x = _unpack_pairs(jnp.where(row < valid, xbuf[slot, rows], jnp.uint32(0))).astype(BF16)
            gate = jnp.dot(x, wgb[...], preferred_element_type=F32)
            up = jnp.dot(x, wub[...], preferred_element_type=F32)
            ybuf[slot, rows] = _pack_pairs(
                jnp.dot((_silu(gate) * up).astype(BF16), wdb[...], preferred_element_type=F32))
        y_copy(g).start()

        @pl.when(g + ring < n_used)
        def _():
            x_copy(g + ring).start()
        return carry

    lax.fori_loop(0, nb, block, 0)

    @pl.when(e == pl.num_programs(0) - 1)
    def _():
        for back in range(1, ring + 1):
            @pl.when(n_used - back >= 0)
            def _():
                y_copy(n_used - back).wait()


def _experts(blk_start, counts, n_used, xs, wg, wu, wd):
    n_slots, dp = xs.shape
    n_exp, d, ff = wg.shape
    bm, ring = EXPERT_BLOCK, EXPERT_RING
    wsel = lambda e, st, ct, nu: (e, 0, 0)
    grid_spec = pltpu.PrefetchScalarGridSpec(
        num_scalar_prefetch=3, grid=(n_exp,),
        in_specs=[pl.BlockSpec(memory_space=pl.ANY),
                  pl.BlockSpec((None, d, ff), wsel), pl.BlockSpec((None, d, ff), wsel),
                  pl.BlockSpec((None, ff, d), wsel)],
        out_specs=pl.BlockSpec(memory_space=pl.ANY),
        scratch_shapes=[pltpu.VMEM((d, ff), BF16), pltpu.VMEM((d, ff), BF16), pltpu.VMEM((ff, d), BF16),
                        pltpu.VMEM((ring, bm, dp), U32), pltpu.VMEM((ring, bm, dp), U32),
                        pltpu.SemaphoreType.DMA((ring,)), pltpu.SemaphoreType.DMA((ring,))])
    return pl.pallas_call(
        _expert_kernel, out_shape=jax.ShapeDtypeStruct((n_slots, dp), U32), grid_spec=grid_spec,
        compiler_params=_params(1), name="experts",
    )(blk_start, counts, n_used, xs, wg, wu, wd)


def _sc_gather_rows(table, idx):
    n_idx = idx.shape[0]
    dp = table.shape[1]
    workers = SC_CORES * SC_SUBCORES
    per_worker = n_idx // workers
    chunk = SC_GATHER_ROWS

    @functools.partial(
        pl.kernel, mesh=_sc_mesh(), out_type=jax.ShapeDtypeStruct((n_idx, dp), table.dtype),
        scratch_types=[pltpu.VMEM((chunk,), jnp.int32), pltpu.VMEM((chunk, dp), table.dtype),
                       pltpu.SemaphoreType.DMA],
        name="sc_gather_rows")
    def gather(table_hbm, idx_hbm, out_hbm, idx_v, rows_v, sem):
        base = (lax.axis_index("s") * SC_CORES + lax.axis_index("c")) * per_worker

        @pl.loop(0, per_worker // chunk)
        def _(j):
            off = base + j * chunk
            pltpu.sync_copy(idx_hbm.at[pl.ds(off, chunk)], idx_v)
            pltpu.async_copy(table_hbm.at[idx_v], rows_v, sem).wait()
            pltpu.sync_copy(rows_v, out_hbm.at[pl.ds(off, chunk)])

    return gather(table, idx)


def _combine_kernel(tile, yt_ref, w_ref, h_ref, x1_ref, g2_ref, sg_ref, su_ref, sd_ref, lg_ref, lb_ref, *rest):
    o_ref = rest[-1]
    hb = _unpack_pairs(h_ref[...]).astype(BF16)
    g = jnp.dot(hb, sg_ref[...], preferred_element_type=F32)
    u = jnp.dot(hb, su_ref[...], preferred_element_type=F32)
    f = jnp.dot((_silu(g) * u).astype(BF16), sd_ref[...], preferred_element_type=F32)
    w = w_ref[...]
    for kk in range(TOP_K):
        f = f + w[:, kk:kk + 1] * _unpack_pairs(yt_ref[kk * tile:(kk + 1) * tile, :])
    o_ref[...] = _layer_norm(DEEPNORM_ALPHA * x1_ref[...] + g2_ref[0] * f, lg_ref[...], lb_ref[...])


def _combine(y_tok, top_w, h2p, x1, g2, sg, su, sd, lg, lb, tile, tiles_per_batch, first_tile, prev_out):
    n, d = x1.shape
    dp = h2p.shape[1]
    row = lambda i: (first_tile + i, 0)
    full = lambda i: (0, 0)

    def const(a):
        return pl.BlockSpec(a.shape, full, pipeline_mode=pl.Buffered(1))

    in_specs = [pl.BlockSpec((TOP_K * tile, dp), lambda i: (i, 0)),
                pl.BlockSpec((tile, TOP_K), row), pl.BlockSpec((tile, dp), row), pl.BlockSpec((tile, d), row),
                pl.BlockSpec((1, 1, d), lambda i: ((first_tile + i) // tiles_per_batch, 0, 0)),
                const(sg), const(su), const(sd), const(lg), const(lb)]
    args = [y_tok, top_w, h2p, x1, g2, sg, su, sd, lg, lb]
    aliases = {}
    if prev_out is not None:
        in_specs.append(pl.BlockSpec(memory_space=pl.ANY))
        args.append(prev_out)
        aliases = {len(args) - 1: 0}
    return pl.pallas_call(
        functools.partial(_combine_kernel, tile),
        out_shape=jax.ShapeDtypeStruct((n, d), F32),
        grid=(y_tok.shape[0] // (TOP_K * tile),),
        in_specs=in_specs,
        out_specs=pl.BlockSpec((tile, d), row),
        input_output_aliases=aliases,
        compiler_params=_params(1), name="combine_shared_ln2",
    )(*args)


def kernel(x, c, ctx, c_ctx, w_mod, b_mod, w_in, na_rpb, gla_w_decay_f, gla_b_decay_f, gla_w_decay_b, gla_b_decay_b,
           gla_norm_w, w_branch_a, w_branch_b, w_out, ln1_g, ln1_b, router_w, router_bias, exp_w_gate, exp_w_up,
           exp_w_down, sh_w_gate, sh_w_up, sh_w_down, ln2_g, ln2_b):
    batch, t_len, d = x.shape
    ctx_len = ctx.shape[1]
    n = batch * t_len
    assert w_mod.shape[0] == DEPTH == 1
    assert t_len % GLA_CHUNK == 0 and ctx_len == GLA_CHUNK and (t_len // GRID_W) % NA_ROW_BLOCK == 0

    mod_rows = 16
    c_all = jnp.zeros((mod_rows, d), F32).at[:batch].set(c).at[batch].set(c_ctx)
    mod = _modulation(c_all, w_mod[0], b_mod[0])
    sh1, sc1, g1, sh2, sc2, g2 = [mod[:batch, j * d:(j + 1) * d].reshape(batch, 1, d) for j in range(6)]
    sh1c = mod[batch:batch + 1, 0:d].reshape(1, 1, d)
    sc1c = mod[batch:batch + 1, d:2 * d].reshape(1, 1, d)

    offs = np.cumsum((0, NA_WIDTH, NA_WIDTH, NA_WIDTH, GLA_QK_WIDTH, GLA_QK_WIDTH, GLA_V_WIDTH, GLA_V_WIDTH,
                      GLA_GATE_RANK, GLA_GATE_RANK, d, d))
    qa, ka, va, qb, kb, vbc, gbc, lrf, lrb, ga, gbt = [w_in[0][:, offs[j]:offs[j + 1]] for j in range(11)]
    lr_cols = jnp.concatenate([lrf, lrb, jnp.zeros((d, LANES - 2 * GLA_GATE_RANK), F32)], axis=1)
    w_lat = jnp.concatenate([lr_cols, qa, ka, va, qb, kb, vbc, gbc, ga, gbt], axis=1).astype(BF16)
    w_ctx = jnp.concatenate([lr_cols, ka, va, kb, vbc], axis=1).astype(BF16)
    plain = ("plain",)
    lat_plan = ((LANES, (("decay",),)),
                (3 * NA_WIDTH, (("scale", NA_HEAD_DIM ** -0.5), plain, plain)),
                (2 * GLA_QK_WIDTH, (("rope", GLA_DK ** -0.5), ("rope", 1.0))),
                (GLA_V_WIDTH, (plain, plain)), (GLA_V_WIDTH, (plain, plain)),
                (2 * d, (plain,) * 4))
    ctx_plan = ((LANES, (("decay",),)),
                (2 * NA_WIDTH, (plain, plain)), (GLA_QK_WIDTH, (plain,)), (GLA_V_WIDTH, (plain, plain)))
    w2, b2 = _decay_weights(gla_w_decay_f[0], gla_b_decay_f[0], gla_w_decay_b[0], gla_b_decay_b[0])
    tile = 256
    x2d = x.reshape(n, d)
    decay, na_qkv, gla_qk, vb, gb, gates = _projection(
        x2d, sc1, sh1, w_lat, w2, b2, lat_plan, (BF16,) * 6, PROJ_TILE, t_len // PROJ_TILE,
        rope=_rope_tables(t_len))
    decay_c, na_kv_c, k_c, v_c = _projection(
        ctx.reshape(batch * ctx_len, d), sc1c, sh1c, w_ctx, w2, b2, ctx_plan, (BF16,) * 4, tile,
        batch * ctx_len // tile)

    out_a = _na_attention(na_qkv, na_kv_c, _na_bias_tables(na_rpb[0]), batch, t_len, ctx_len)
    o_f, o_b = _gla(gla_qk, vb, decay, k_c, v_c, decay_c, batch, t_len, ctx_len)

    x1, h2p, scores_t = _merge(
        out_a.reshape(n, NA_WIDTH), o_f.reshape(n, GLA_V_WIDTH), o_b.reshape(n, GLA_V_WIDTH), gb, gates, x2d,
        g1, sc2, sh2, w_branch_a[0].astype(BF16), w_branch_b[0].astype(BF16), w_out[0].astype(BF16),
        gla_norm_w[0].reshape(1, -1), ln1_g[0].reshape(1, d), ln1_b[0].reshape(1, d),
        router_w[0].T.astype(BF16), tile, t_len // tile)

    top_e, top_w, rank, counts = _route(scores_t, router_bias[0], 512)

    counts = counts[:, 0]
    n_blocks = n * TOP_K // EXPERT_BLOCK + N_EXPERTS
    blocks_per = (counts + EXPERT_BLOCK - 1) // EXPERT_BLOCK
    blk_end = jnp.cumsum(blocks_per)
    blk_start = blk_end - blocks_per
    dest = _slots(top_e, rank, (blk_start * EXPERT_BLOCK).astype(F32).reshape(N_EXPERTS, 1), 512)

    xs = _sc_scatter_rows(h2p, dest.reshape(TOP_K * n), n_blocks * EXPERT_BLOCK)
    y = _experts(blk_start.astype(jnp.int32), counts, blk_end[-1:].astype(jnp.int32), xs,
                 exp_w_gate[0], exp_w_up[0], exp_w_down[0])
    dest_tok = dest.reshape(TOP_K, n // tile, tile).transpose(1, 0, 2).reshape(COMBINE_CHUNKS, -1)
    w_rows = top_w.T
    shared = (sh_w_gate[0].astype(BF16), sh_w_up[0].astype(BF16), sh_w_down[0].astype(BF16),
              ln2_g[0].reshape(1, d), ln2_b[0].reshape(1, d))
    tiles_per_chunk = n // tile // COMBINE_CHUNKS
    out = None
    for ci in range(COMBINE_CHUNKS):
        y_tok = _sc_gather_rows(y, dest_tok[ci])
        out = _combine(y_tok, w_rows, h2p, x1, g2, *shared, tile, t_len // tile, ci * tiles_per_chunk, out)
    return out.reshape(batch, t_len, d)
```

```python
import functools

import numpy as np
import jax
import jax.numpy as jnp
from jax import lax
from jax.experimental import pallas as pl
from jax.experimental.pallas import tpu as pltpu
from jax.experimental.pallas import tpu_sc as plsc

F32 = jnp.float32
BF16 = jnp.bfloat16
U32 = jnp.uint32
HIGHEST = lax.Precision.HIGHEST

GRID_W = 64
NA_HEADS = 8
NA_HEAD_DIM = 64
NA_WIDTH = NA_HEADS * NA_HEAD_DIM
WIN_ROWS = 8
WIN_COLS = 16
GLA_HEADS = 4
GLA_DK = 128
GLA_DV = 256
GLA_QK_WIDTH = GLA_HEADS * GLA_DK
GLA_V_WIDTH = GLA_HEADS * GLA_DV
GLA_GATE_RANK = 16
GLA_TAU = 16.0
LOG2E = 1.4426950408889634
ROPE_BASE = 10000.0
N_EXPERTS = 256
TOP_K = 8
N_GROUPS = 8
TOPK_GROUPS = 4
GROUP_SIZE = N_EXPERTS // N_GROUPS
ROUTED_SCALE = 2.5
DEPTH = 1
DEEPNORM_ALPHA = (2 * DEPTH) ** 0.25
EPS = 1e-6

LANES = 128
PROJ_TILE = 512
NA_ROW_BLOCK = 4
NA_UNION_ROWS = NA_ROW_BLOCK + WIN_ROWS - 1
GLA_CHUNK = 256
GLA_DIAG = 16
GLA_HEADS_PER_STEP = 4
EXPERT_BLOCK = 256
COMBINE_CHUNKS = 4
EXPERT_RING = 4
SC_CORES = 2
SC_SUBCORES = 16
SC_GATHER_ROWS = 128
NEG_BIG = -1e30
VMEM_LIMIT = 56 * 1024 * 1024


def _params(n_axes, vmem=VMEM_LIMIT):
    return pltpu.CompilerParams(dimension_semantics=("arbitrary",) * n_axes, vmem_limit_bytes=vmem)


def _sigmoid(v):
    return 1.0 / (1.0 + jnp.exp(-v))


def _silu(v):
    return v * _sigmoid(v)


def _pack_pairs(v):
    m = v.shape[1] // 2
    lo = lax.bitcast_convert_type(v[:, :m].astype(BF16).astype(F32), U32) >> 16
    hi = lax.bitcast_convert_type(v[:, m:].astype(BF16).astype(F32), U32) & jnp.uint32(0xFFFF0000)
    return lo | hi


def _unpack_pairs(p):
    lo = lax.bitcast_convert_type(p << 16, F32)
    hi = lax.bitcast_convert_type(p & jnp.uint32(0xFFFF0000), F32)
    return jnp.concatenate([lo, hi], axis=1)


def _mod_kernel(c_ref, w_ref, b_ref, o_ref):
    o_ref[...] = jnp.dot(_silu(c_ref[...]), w_ref[...], preferred_element_type=F32, precision=HIGHEST) + b_ref[...]


def _modulation(c_all, w_mod, b_mod):
    rows, d = c_all.shape
    n = w_mod.shape[1]
    bn = 512
    return pl.pallas_call(
        _mod_kernel,
        out_shape=jax.ShapeDtypeStruct((rows, n), F32),
        grid=(n // bn,),
        in_specs=[pl.BlockSpec((rows, d), lambda j: (0, 0)),
                  pl.BlockSpec((d, bn), lambda j: (0, j)),
                  pl.BlockSpec((1, bn), lambda j: (0, j))],
        out_specs=pl.BlockSpec((rows, bn), lambda j: (0, j)),
        compiler_params=_params(1),
        name="modulation",
    )(c_all, w_mod, b_mod.reshape(1, n))


def _swap32(v):
    lane = lax.broadcasted_iota(jnp.int32, v.shape, 1)
    return jnp.where((lane % 64) < 32, pltpu.roll(v, 96, 1), pltpu.roll(v, 32, 1))


GLA_DECAY_WIDTH = 2 * GLA_HEADS * 2 * GLA_DK


def _log2_decay_split(lr, w2, b2, out_ref):
    z = (jnp.dot(lr.astype(BF16), w2, preferred_element_type=F32) + b2) * LOG2E
    a = (jnp.minimum(z, 0.0) - jnp.log2(1.0 + jnp.exp2(-jnp.abs(z)))) * (1.0 / GLA_TAU)
    hi = a.astype(BF16)
    lo = (a - hi.astype(F32)).astype(BF16)
    for p in range(a.shape[1] // GLA_DK):
        src = slice(p * GLA_DK, (p + 1) * GLA_DK)
        out_ref[:, 2 * p * GLA_DK:(2 * p + 1) * GLA_DK] = hi[:, src]
        out_ref[:, (2 * p + 1) * GLA_DK:(2 * p + 2) * GLA_DK] = lo[:, src]


def _proj_kernel(plan, has_rope, *refs):
    x_ref, sc_ref, sh_ref, w_ref, w2_ref, b2_ref = refs[:6]
    pos = 6
    if has_rope:
        cos_ref, sin_ref = refs[6:8]
        pos = 8
    out_refs = refs[pos:]
    h = (x_ref[...] * (1.0 + sc_ref[0]) + sh_ref[0]).astype(BF16)
    col = 0
    for out_ref, (width, kinds) in zip(out_refs, plan):
        for j, kind in enumerate(kinds):
            cw = width // len(kinds)
            c0 = j * cw
            acc = jnp.dot(h, w_ref[:, col + c0:col + c0 + cw], preferred_element_type=F32)
            if kind[0] == "decay":
                _log2_decay_split(acc, w2_ref[...], b2_ref[...], out_ref)
                continue
            if kind[0] == "scale":
                acc = acc * kind[1]
            elif kind[0] == "rope":
                cos, sin = cos_ref[...], sin_ref[...]
                pieces = []
                for p in range(cw // LANES):
                    v = acc[:, p * LANES:(p + 1) * LANES]
                    pieces.append((v * cos + _swap32(v) * sin) * kind[1])
                acc = jnp.concatenate(pieces, axis=1)
            out_ref[:, c0:c0 + cw] = acc.astype(out_ref.dtype)
        col += width


def _projection(x2d, sc, sh, w, w2, b2, plan, out_dtypes, tile, tiles_per_mod, rope=None):
    n, d = x2d.shape
    const = lambda a: pl.BlockSpec(a.shape, lambda i: (0, 0), pipeline_mode=pl.Buffered(1))
    in_specs = [pl.BlockSpec((tile, d), lambda i: (i, 0)),
                pl.BlockSpec((1, 1, d), lambda i: (i // tiles_per_mod, 0, 0)),
                pl.BlockSpec((1, 1, d), lambda i: (i // tiles_per_mod, 0, 0)),
                const(w), const(w2), const(b2)]
    args = [x2d, sc, sh, w, w2, b2]
    if rope is not None:
        in_specs += [pl.BlockSpec((tile, LANES), lambda i: (i % tiles_per_mod, 0))] * 2
        args += list(rope)
    widths = [GLA_DECAY_WIDTH if kinds[0][0] == "decay" else wd for wd, kinds in plan]
    out_shape = [jax.ShapeDtypeStruct((n, wd), dt) for wd, dt in zip(widths, out_dtypes)]
    out_specs = [pl.BlockSpec((tile, wd), lambda i: (i, 0)) for wd in widths]
    return pl.pallas_call(
        functools.partial(_proj_kernel, plan, rope is not None),
        out_shape=out_shape,
        grid=(n // tile,),
        in_specs=in_specs,
        out_specs=out_specs,
        compiler_params=_params(1),
        name="in_proj" if rope is not None else "ctx_proj",
    )(*args)


def _rope_tables(t_len):
    half = GLA_DK // 2
    quarter = half // 2
    inv_freq = ROPE_BASE ** (-jnp.arange(quarter, dtype=F32) / quarter)
    pos = jnp.arange(t_len)
    row_ang = (pos // GRID_W).astype(F32)[:, None] * inv_freq[None, :]
    col_ang = (pos % GRID_W).astype(F32)[:, None] * inv_freq[None, :]
    cr, sr, cc, sn = jnp.cos(row_ang), jnp.sin(row_ang), jnp.cos(col_ang), jnp.sin(col_ang)
    cos = jnp.concatenate([cr, cr, cc, cc], axis=1)
    sin = jnp.concatenate([-sr, sr, -sn, sn], axis=1)
    return cos, sin


def _na_bias_tables(rpb):
    rb, ur, w = NA_ROW_BLOCK, NA_UNION_ROWS, GRID_W
    heads = rpb.shape[0]
    pad = jnp.pad(rpb, ((0, 0), (0, 0), (w, w)))
    toep = jnp.stack([pad[:, :, w + WIN_COLS - 1 - c:2 * w + WIN_COLS - 1 - c] for c in range(w)], axis=2)
    c = np.arange(w)[:, None]
    kc = np.arange(w)[None, :]
    col_start = np.clip(c - WIN_COLS // 2, 0, w - WIN_COLS)
    col_ok = (kc >= col_start) & (kc < col_start + WIN_COLS)
    toep = jnp.where(col_ok[None, None], toep, NEG_BIG)
    neg = jnp.full((heads, w, w), NEG_BIG, F32)
    half = WIN_ROWS // 2
    tables = []
    for lo, off in ((lambda i: 0, WIN_ROWS - 1), (lambda i: i, WIN_ROWS - 1 - half), (lambda i: ur - WIN_ROWS, 0)):
        rows_ = []
        for i in range(rb):
            blocks = [toep[:, j - i + off] if lo(i) <= j < lo(i) + WIN_ROWS else neg for j in range(ur)]
            rows_.append(jnp.concatenate(blocks, axis=2))
        tables.append(jnp.concatenate(rows_, axis=1))
    return jnp.stack(tables)


def _na_kernel(rows, q_ref, k_ref, v_ref, kc_ref, vc_ref, bias_ref, o_ref):
    rb = pl.program_id(1)
    ustart = jnp.clip(rb * NA_ROW_BLOCK - WIN_ROWS // 2, 0, rows - NA_UNION_ROWS)
    k0 = pl.multiple_of(ustart * GRID_W, GRID_W)
    nk = NA_UNION_ROWS * GRID_W
    nt = (((1,), (1,)), ((), ()))
    lane = lax.broadcasted_iota(jnp.int32, (NA_ROW_BLOCK * GRID_W, LANES), 1)
    for p in range(NA_WIDTH // LANES):
        ls = slice(p * LANES, (p + 1) * LANES)
        q = q_ref[:, ls]
        ku = k_ref[pl.ds(k0, nk), ls]
        vu = v_ref[pl.ds(k0, nk), ls]
        kc = kc_ref[:, ls]
        vc = vc_ref[:, ls]
        out = jnp.zeros(q.shape, F32)
        for hh in range(LANES // NA_HEAD_DIM):
            mine = (lane // NA_HEAD_DIM) == hh
            qm = jnp.where(mine, q, jnp.zeros_like(q))
            s_loc = lax.dot_general(qm, ku, nt, preferred_element_type=F32) + bias_ref[p * 2 + hh]
            s_ctx = lax.dot_general(qm, kc, nt, preferred_element_type=F32)
            m = jnp.maximum(jnp.max(s_loc, axis=1, keepdims=True), jnp.max(s_ctx, axis=1, keepdims=True))
            p_loc = jnp.exp(s_loc - m)
            p_ctx = jnp.exp(s_ctx - m)
            denom = jnp.sum(p_loc, axis=1, keepdims=True) + jnp.sum(p_ctx, axis=1, keepdims=True)
            o = (jnp.dot(p_loc.astype(BF16), vu, preferred_element_type=F32)
                 + jnp.dot(p_ctx.astype(BF16), vc, preferred_element_type=F32)) / denom
            out = jnp.where(mine, o, out)
        o_ref[:, ls] = out.astype(o_ref.dtype)


def _na_attention(na_qkv, na_kv_ctx, bias_tables, batch, t_len, ctx_len):
    rows = t_len // GRID_W
    n_rb = rows // NA_ROW_BLOCK
    tq = NA_ROW_BLOCK * GRID_W
    qkv = na_qkv.reshape(batch, t_len, 3 * NA_WIDTH)
    kvc = na_kv_ctx.reshape(batch, ctx_len, 2 * NA_WIDTH)

    def bias_idx(b, r):
        return (jnp.where(r == 0, 0, jnp.where(r == n_rb - 1, 2, 1)), 0, 0, 0)

    return pl.pallas_call(
        functools.partial(_na_kernel, rows),
        out_shape=jax.ShapeDtypeStruct((batch, t_len, NA_WIDTH), BF16),
        grid=(batch, n_rb),
        in_specs=[pl.BlockSpec((None, tq, NA_WIDTH), lambda b, r: (b, r, 0)),
                  pl.BlockSpec((None, t_len, NA_WIDTH), lambda b, r: (b, 0, 1)),
                  pl.BlockSpec((None, t_len, NA_WIDTH), lambda b, r: (b, 0, 2)),
                  pl.BlockSpec((None, ctx_len, NA_WIDTH), lambda b, r: (b, 0, 0)),
                  pl.BlockSpec((None, ctx_len, NA_WIDTH), lambda b, r: (b, 0, 1)),
                  pl.BlockSpec((None,) + bias_tables.shape[1:], bias_idx)],
        out_specs=pl.BlockSpec((None, tq, NA_WIDTH), lambda b, r: (b, r, 0)),
        compiler_params=_params(2),
        name="na_attention",
    )(qkv, qkv, qkv, kvc, kvc, bias_tables)


def _gla_constants(c):
    tris, masks = [], []
    for reverse in (False, True):
        i = np.arange(c)[:, None]
        j = np.arange(c)[None, :]
        tris.append((j >= i) if reverse else (j <= i))
        i = np.arange(c // 2)[:, None]
        j = np.arange(c // 2)[None, :]
        if reverse:
            i, j = j, i
        level = []
        s = c // 4
        while s >= GLA_DIAG:
            level.append(((i // (2 * s)) == (j // (2 * s))) & ((i % (2 * s)) >= s) & ((j % (2 * s)) < s))
            s //= 2
        level.append(((i // GLA_DIAG) == (j // GLA_DIAG)) & (j <= i))
        masks.append(np.stack(level))
    return jnp.asarray(np.stack(tris), BF16), jnp.asarray(np.stack(masks), F32)


def _block_refs(cum, s, reverse, diag):
    c = cum.shape[0]
    span = s if diag else 2 * s
    parts = []
    for p in range(c // span):
        if diag:
            r = p * span + (span - 1 if reverse else 0)
        else:
            r = p * span + (s - 1 if reverse else s)
        parts.append(jnp.broadcast_to(cum[r:r + 1, :], (span, cum.shape[1])))
    return jnp.concatenate(parts, axis=0)


def _cumsum_rows(a_hl, tri):
    parts = jnp.dot(tri, a_hl, preferred_element_type=F32)
    w = a_hl.shape[1] // 2
    return parts[:, :w] + parts[:, w:]


def _gla_chunk(q, k, v, a, state_t, tri, mask_ref, reverse, want_out):
    c = k.shape[0]
    hc = c // 2
    nt = (((1,), (1,)), ((), ()))
    k = k.astype(F32)
    q = q.astype(F32) if want_out else None
    cum = _cumsum_rows(a, tri)
    last = 0 if reverse else c - 1
    total = cum[last:last + 1, :]

    def scale(x, log2_factor):
        return (x * jnp.exp2(log2_factor)).astype(BF16)

    out = None
    if want_out:
        out = lax.dot_general(scale(q, cum), state_t.astype(BF16), nt, preferred_element_type=F32)
        halves = (slice(hc, c), slice(0, hc)) if reverse else (slice(0, hc), slice(hc, c))
        early, late = halves
        r = hc - 1 if reverse else hc
        g = cum[r:r + 1, :]
        cross = lax.dot_general(scale(q[late], cum[late] - g), scale(k[early], g - cum[early]), nt,
                                preferred_element_type=F32)
        inner = []
        for rows in halves:
            cx, qx, kx = cum[rows], q[rows], k[rows]
            acc = jnp.zeros((hc, hc), F32)
            s = hc // 2
            level = 0
            while True:
                diag = s < GLA_DIAG
                d = cx - _block_refs(cx, GLA_DIAG if diag else s, reverse, diag)
                if diag:
                    qs, ks = scale(qx, d), scale(kx, -d)
                else:
                    e = jnp.exp2(-jnp.abs(d))
                    qs, ks = (qx * e).astype(BF16), (kx * e).astype(BF16)
                acc = acc + lax.dot_general(qs, ks, nt, preferred_element_type=F32) * mask_ref[level]
                if diag:
                    break
                s //= 2
                level += 1
            inner.append(acc)
        out_early = jnp.dot(inner[0].astype(BF16), v[early], preferred_element_type=F32)
        out_late = jnp.dot(jnp.concatenate([cross, inner[1]], axis=1).astype(BF16),
                           jnp.concatenate([v[early], v[late]], axis=0), preferred_element_type=F32)
        intra = [out_late, out_early] if reverse else [out_early, out_late]
        out = out + jnp.concatenate(intra, axis=0)
    upd = lax.dot_general(v, scale(k, total - cum), (((0,), (0,)), ((), ())), preferred_element_type=F32)
    return out, jnp.exp2(total) * state_t + upd


def _gla_kernel(qf_ref, kf_ref, vf_ref, af_ref, qb_ref, kb_ref, vb_ref, ab_ref,
                kc_ref, vc_ref, acf_ref, acb_ref, tri_ref, mask_ref,
                of_ref, ob_ref, sf_ref, sb_ref):
    c = pl.program_id(2)
    fm, bm = mask_ref.at[0], mask_ref.at[1]
    heads = sf_ref.shape[0]

    @pl.when(c == 0)
    def _():
        zero = jnp.zeros(sf_ref.shape[1:], F32)
        for h in range(heads):
            kc = kc_ref[:, h * GLA_DK:(h + 1) * GLA_DK]
            vc = vc_ref[:, h * GLA_DV:(h + 1) * GLA_DV]
            hl = slice(2 * h * GLA_DK, 2 * (h + 1) * GLA_DK)
            _, sf_ref[h] = _gla_chunk(None, kc, vc, acf_ref[:, hl], zero, tri_ref[0], fm, False, False)
            _, sb_ref[h] = _gla_chunk(None, kc, vc, acb_ref[:, hl], zero, tri_ref[1], bm, True, False)

    @pl.when(c > 0)
    def _():
        for h in range(heads):
            ks = slice(h * GLA_DK, (h + 1) * GLA_DK)
            vs = slice(h * GLA_DV, (h + 1) * GLA_DV)
            hl = slice(2 * h * GLA_DK, 2 * (h + 1) * GLA_DK)
            o, sf_ref[h] = _gla_chunk(qf_ref[:, ks], kf_ref[:, ks], vf_ref[:, vs], af_ref[:, hl], sf_ref[h],
                                      tri_ref[0], fm, False, True)
            of_ref[:, vs] = o.astype(of_ref.dtype)
            o, sb_ref[h] = _gla_chunk(qb_ref[:, ks], kb_ref[:, ks], vb_ref[:, vs], ab_ref[:, hl], sb_ref[h],
                                      tri_ref[1], bm, True, True)
            ob_ref[:, vs] = o.astype(ob_ref.dtype)


def _gla(gla_qk, vb, decay, k_ctx, v_ctx, decay_ctx, batch, t_len, ctx_len):
    nc = t_len // GLA_CHUNK
    h = GLA_HEADS
    qk = gla_qk.reshape(batch, t_len, 2 * GLA_QK_WIDTH)
    v3 = vb.reshape(batch, t_len, GLA_V_WIDTH)
    a3 = decay.reshape(batch, t_len, GLA_DECAY_WIDTH)
    kc3 = k_ctx.reshape(batch, ctx_len, GLA_QK_WIDTH)
    vc3 = v_ctx.reshape(batch, ctx_len, GLA_V_WIDTH)
    ac3 = decay_ctx.reshape(batch, ctx_len, GLA_DECAY_WIDTH)
    tri, masks = _gla_constants(GLA_CHUNK)

    def fwd(c):
        return jnp.maximum(c - 1, 0)

    def bwd(c):
        return nc - 1 - jnp.maximum(c - 1, 0)

    def const(a):
        return pl.BlockSpec(a.shape, lambda b, hh, c: (0,) * a.ndim, pipeline_mode=pl.Buffered(1))

    hp = GLA_HEADS_PER_STEP
    groups = h // hp
    cq = (None, GLA_CHUNK, hp * GLA_DK)
    cv = (None, GLA_CHUNK, hp * GLA_DV)
    ca = (None, GLA_CHUNK, hp * 2 * GLA_DK)
    cca = (None, ctx_len, hp * 2 * GLA_DK)
    in_specs = [
        pl.BlockSpec(cq, lambda b, g, c: (b, fwd(c), g)),
        pl.BlockSpec(cq, lambda b, g, c: (b, fwd(c), groups + g)),
        pl.BlockSpec(cv, lambda b, g, c: (b, fwd(c), g)),
        pl.BlockSpec(ca, lambda b, g, c: (b, fwd(c), g)),
        pl.BlockSpec(cq, lambda b, g, c: (b, bwd(c), g)),
        pl.BlockSpec(cq, lambda b, g, c: (b, bwd(c), groups + g)),
        pl.BlockSpec(cv, lambda b, g, c: (b, bwd(c), g)),
        pl.BlockSpec(ca, lambda b, g, c: (b, bwd(c), groups + g)),
        pl.BlockSpec((None, ctx_len, hp * GLA_DK), lambda b, g, c: (b, 0, g)),
        pl.BlockSpec((None, ctx_len, hp * GLA_DV), lambda b, g, c: (b, 0, g)),
        pl.BlockSpec(cca, lambda b, g, c: (b, 0, g)),
        pl.BlockSpec(cca, lambda b, g, c: (b, 0, groups + g)),
        const(tri), const(masks),
    ]
    out_specs = [pl.BlockSpec(cv, lambda b, g, c: (b, fwd(c), g)),
                 pl.BlockSpec(cv, lambda b, g, c: (b, bwd(c), g))]
    out_shape = [jax.ShapeDtypeStruct((batch, t_len, GLA_V_WIDTH), BF16)] * 2
    return pl.pallas_call(
        _gla_kernel,
        out_shape=out_shape,
        grid=(batch, groups, nc + 1),
        in_specs=in_specs,
        out_specs=out_specs,
        scratch_shapes=[pltpu.VMEM((hp, GLA_DV, GLA_DK), F32), pltpu.VMEM((hp, GLA_DV, GLA_DK), F32)],
        compiler_params=_params(3),
        name="gla",
    )(qk, qk, v3, a3, qk, qk, v3, a3, kc3, vc3, ac3, ac3, tri, masks)


def _decay_weights(w_f, b_f, w_b, b_b):
    r, width = w_f.shape
    w2 = jnp.zeros((LANES, 2 * width), F32).at[:r, :width].set(w_f).at[r:2 * r, width:].set(w_b)
    return w2.astype(BF16), jnp.concatenate([b_f, b_b]).reshape(1, 2 * width)


def _layer_norm(v, g, b):
    mu = jnp.mean(v, axis=1, keepdims=True)
    var = jnp.mean(jnp.square(v - mu), axis=1, keepdims=True)
    return (v - mu) * lax.rsqrt(var + EPS) * g + b


def _merge_kernel(oa_ref, of_ref, ob_ref, gb_ref, gates_ref, x_ref, g1_ref, sc2_ref, sh2_ref,
                  wa_ref, wb_ref, wo_ref, nw_ref, lg_ref, lb_ref, rw_ref,
                  x1_ref, h2_ref, sc_ref):
    d = x_ref.shape[1]
    o = of_ref[...].astype(F32) + ob_ref[...].astype(F32)
    pieces = []
    for hh in range(GLA_HEADS):
        oh = o[:, hh * GLA_DV:(hh + 1) * GLA_DV]
        pieces.append(oh * lax.rsqrt(jnp.mean(jnp.square(oh), axis=1, keepdims=True) + EPS))
    out_b = jnp.concatenate(pieces, axis=1) * nw_ref[...] * _silu(gb_ref[...].astype(F32))
    ya = jnp.dot(oa_ref[...], wa_ref[...], preferred_element_type=F32)
    yb = jnp.dot(out_b.astype(BF16), wb_ref[...], preferred_element_type=F32)
    y = _sigmoid(gates_ref[:, :d].astype(F32)) * ya + _sigmoid(gates_ref[:, d:].astype(F32)) * yb
    y2 = jnp.dot(y.astype(BF16), wo_ref[...], preferred_element_type=F32)
    x1 = _layer_norm(DEEPNORM_ALPHA * x_ref[...] + g1_ref[0] * y2, lg_ref[...], lb_ref[...])
    x1_ref[...] = x1
    h2 = x1 * (1.0 + sc2_ref[0]) + sh2_ref[0]
    h2_ref[...] = _pack_pairs(h2)
    logits_t = lax.dot_general(rw_ref[...], h2.astype(BF16), (((1,), (1,)), ((), ())), preferred_element_type=F32)
    sc_ref[...] = _sigmoid(logits_t)


def _merge(out_a, o_f, o_b, gb, gates, x2d, g1, sc2, sh2, wa, wb, wo, nw, lg, lb, rw_t, tile, tiles_per_batch):
    n, d = x2d.shape
    row = lambda i: (i, 0)
    mod = lambda i: (i // tiles_per_batch, 0, 0)
    full = lambda i: (0, 0)

    def const(a):
        return pl.BlockSpec(a.shape, full, pipeline_mode=pl.Buffered(1))

    in_specs = [pl.BlockSpec((tile, NA_WIDTH), row), pl.BlockSpec((tile, GLA_V_WIDTH), row),
                pl.BlockSpec((tile, GLA_V_WIDTH), row), pl.BlockSpec((tile, GLA_V_WIDTH), row),
                pl.BlockSpec((tile, 2 * d), row), pl.BlockSpec((tile, d), row),
                pl.BlockSpec((1, 1, d), mod), pl.BlockSpec((1, 1, d), mod), pl.BlockSpec((1, 1, d), mod),
                const(wa), const(wb), const(wo), const(nw), const(lg), const(lb), const(rw_t)]
    out_shape = [jax.ShapeDtypeStruct((n, d), F32), jax.ShapeDtypeStruct((n, d // 2), U32),
                 jax.ShapeDtypeStruct((N_EXPERTS, n), F32)]
    out_specs = [pl.BlockSpec((tile, d), row), pl.BlockSpec((tile, d // 2), row),
                 pl.BlockSpec((N_EXPERTS, tile), lambda i: (0, i))]
    return pl.pallas_call(
        _merge_kernel, out_shape=out_shape, grid=(n // tile,), in_specs=in_specs, out_specs=out_specs,
        compiler_params=_params(1), name="merge_ln1_router",
    )(out_a, o_f, o_b, gb, gates, x2d, g1, sc2, sh2, wa, wb, wo, nw, lg, lb, rw_t)


def _first_argmax(vals, idx, n):
    m = jnp.max(vals, axis=0, keepdims=True)
    first = jnp.min(jnp.where(vals == m, idx, float(n)), axis=0, keepdims=True)
    return m, first


def _route_kernel(sc_ref, bias_ref, e_ref, w_ref, rank_ref, cnt_ref, carry_ref):
    step = pl.program_id(0)
    tr = sc_ref.shape[1]

    @pl.when(step == 0)
    def _():
        carry_ref[...] = jnp.zeros(carry_ref.shape, F32)

    scores = sc_ref[...]
    biased = scores + bias_ref[...]
    eidx = lax.broadcasted_iota(jnp.int32, (N_EXPERTS, tr), 0).astype(F32)
    lidx = lax.broadcasted_iota(jnp.int32, (GROUP_SIZE, tr), 0).astype(F32)
    gidx = lax.broadcasted_iota(jnp.int32, (N_GROUPS, tr), 0).astype(F32)
    gs = []
    for g in range(N_GROUPS):
        blk = biased[g * GROUP_SIZE:(g + 1) * GROUP_SIZE]
        m1, first = _first_argmax(blk, lidx, GROUP_SIZE)
        m2 = jnp.max(jnp.where(lidx == first, -jnp.inf, blk), axis=0, keepdims=True)
        gs.append(m1 + m2)
    cur = jnp.concatenate(gs, axis=0)
    keep = jnp.zeros((N_GROUPS, tr), F32)
    for _ in range(TOPK_GROUPS):
        _, first = _first_argmax(cur, gidx, N_GROUPS)
        sel = gidx == first
        keep = jnp.where(sel, 1.0, keep)
        cur = jnp.where(sel, -jnp.inf, cur)
    keep_e = jnp.concatenate([jnp.broadcast_to(keep[g:g + 1], (GROUP_SIZE, tr)) for g in range(N_GROUPS)], axis=0)
    masked = jnp.where(keep_e > 0.5, biased, -jnp.inf)
    chosen = jnp.zeros((N_EXPERTS, tr), F32)
    tops, topi = [], []
    for _ in range(TOP_K):
        _, first = _first_argmax(masked, eidx, N_EXPERTS)
        sel = eidx == first
        tops.append(jnp.sum(jnp.where(sel, scores, 0.0), axis=0, keepdims=True))
        topi.append(first)
        chosen = jnp.where(sel, 1.0, chosen)
        masked = jnp.where(sel, -jnp.inf, masked)
    top_s = jnp.concatenate(tops, axis=0)
    top_i = jnp.concatenate(topi, axis=0)
    e_ref[...] = top_i.astype(jnp.int32)
    w_ref[...] = top_s / jnp.sum(top_s, axis=0, keepdims=True) * ROUTED_SCALE
    r = lax.broadcasted_iota(jnp.int32, (tr, tr), 0)
    cidx = lax.broadcasted_iota(jnp.int32, (tr, tr), 1)
    before = jnp.where(r < cidx, 1.0, 0.0).astype(BF16)
    prior = jnp.dot(chosen.astype(BF16), before, preferred_element_type=F32) + carry_ref[...]
    ranks = [jnp.sum(jnp.where(eidx == topi[kk], prior, 0.0), axis=0, keepdims=True) for kk in range(TOP_K)]
    rank_ref[...] = jnp.concatenate(ranks, axis=0).astype(jnp.int32)
    carry_ref[...] = carry_ref[...] + jnp.sum(chosen, axis=1, keepdims=True)
    cnt_ref[...] = jnp.broadcast_to(carry_ref[...], cnt_ref.shape).astype(jnp.int32)


def _route(scores_t, router_bias, tile):
    n = scores_t.shape[1]
    col = lambda i: (0, i)
    out_shape = [jax.ShapeDtypeStruct((TOP_K, n), jnp.int32), jax.ShapeDtypeStruct((TOP_K, n), F32),
                 jax.ShapeDtypeStruct((TOP_K, n), jnp.int32), jax.ShapeDtypeStruct((N_EXPERTS, LANES), jnp.int32)]
    return pl.pallas_call(
        _route_kernel, out_shape=out_shape, grid=(n // tile,),
        in_specs=[pl.BlockSpec((N_EXPERTS, tile), col), pl.BlockSpec((N_EXPERTS, 1), lambda i: (0, 0))],
        out_specs=[pl.BlockSpec((TOP_K, tile), col), pl.BlockSpec((TOP_K, tile), col),
                   pl.BlockSpec((TOP_K, tile), col), pl.BlockSpec((N_EXPERTS, LANES), lambda i: (0, 0))],
        scratch_shapes=[pltpu.VMEM((N_EXPERTS, 1), F32)],
        compiler_params=_params(1), name="route",
    )(scores_t, router_bias.reshape(N_EXPERTS, 1))


def _slots_kernel(e_ref, rank_ref, start_ref, dest_ref):
    tr = e_ref.shape[1]
    eidx = lax.broadcasted_iota(jnp.int32, (N_EXPERTS, tr), 0)
    e = e_ref[...]
    start = start_ref[...]
    rows = [jnp.sum(jnp.where(eidx == e[kk:kk + 1], start, 0.0), axis=0, keepdims=True) for kk in range(TOP_K)]
    dest_ref[...] = jnp.concatenate(rows, axis=0).astype(jnp.int32) + rank_ref[...]


def _slots(top_e, rank, start_rows, tile):
    n = top_e.shape[1]
    col = lambda i: (0, i)
    return pl.pallas_call(
        _slots_kernel, out_shape=jax.ShapeDtypeStruct((TOP_K, n), jnp.int32), grid=(n // tile,),
        in_specs=[pl.BlockSpec((TOP_K, tile), col), pl.BlockSpec((TOP_K, tile), col),
                  pl.BlockSpec((N_EXPERTS, 1), lambda i: (0, 0))],
        out_specs=pl.BlockSpec((TOP_K, tile), col),
        compiler_params=_params(1), name="slots",
    )(top_e, rank, start_rows)


def _sc_mesh():
    return plsc.VectorSubcoreMesh(core_axis_name="c", subcore_axis_name="s",
                                  num_cores=SC_CORES, num_subcores=SC_SUBCORES)


def _sc_scatter_rows(rows, idx, n_out):
    n, dp = rows.shape
    copies = idx.shape[0] // n
    per_worker = n // (SC_CORES * SC_SUBCORES)
    chunk = SC_GATHER_ROWS

    @functools.partial(
        pl.kernel, mesh=_sc_mesh(), out_type=jax.ShapeDtypeStruct((n_out, dp), rows.dtype),
        scratch_types=[pltpu.VMEM((chunk,), jnp.int32), pltpu.VMEM((chunk, dp), rows.dtype)],
        name="sc_scatter_rows")
    def scatter(rows_hbm, idx_hbm, out_hbm, idx_v, rows_v):
        base = (lax.axis_index("s") * SC_CORES + lax.axis_index("c")) * per_worker

        @pl.loop(0, per_worker // chunk)
        def _(j):
            off = base + j * chunk
            pltpu.sync_copy(rows_hbm.at[pl.ds(off, chunk)], rows_v)
            for k in range(copies):
                pltpu.sync_copy(idx_hbm.at[pl.ds(k * n + off, chunk)], idx_v)
                pltpu.sync_copy(rows_v, out_hbm.at[idx_v])

    return scatter(rows, idx)


def _expert_kernel(start_ref, cnt_ref, nused_ref, xs_ref, wg_ref, wu_ref, wd_ref, y_ref,
                   wgb, wub, wdb, xbuf, ybuf, sem_in, sem_out):
    e = pl.program_id(0)
    bm, ring = EXPERT_BLOCK, EXPERT_RING
    cnt = cnt_ref[e]
    nb = (cnt + bm - 1) // bm
    g0 = start_ref[e]
    n_used = nused_ref[0]

    def x_copy(g):
        slot = g % ring
        return pltpu.make_async_copy(xs_ref.at[pl.ds(g * bm, bm)], xbuf.at[slot], sem_in.at[slot])

    def y_copy(g):
        slot = g % ring
        return pltpu.make_async_copy(ybuf.at[slot], y_ref.at[pl.ds(g * bm, bm)], sem_out.at[slot])

    @pl.when(e == 0)
    def _():
        for g in range(ring):
            @pl.when(g < n_used)
            def _():
                x_copy(g).start()

    wgb[...] = wg_ref[...].astype(BF16)
    wub[...] = wu_ref[...].astype(BF16)
    wdb[...] = wd_ref[...].astype(BF16)
    row = lax.broadcasted_iota(jnp.int32, (bm, xbuf.shape[2]), 0)

    def blocks(j, count):
        for b in range(count):
            g = g0 + j + b
            x_copy(g).wait()

            @pl.when(g >= ring)
            def _():
                y_copy(g - ring).wait()

        for b in range(count):
            slot = (g0 + j + b) % ring
            x = _unpack_pairs(jnp.where(row < cnt - (j + b) * bm, xbuf[slot], jnp.uint32(0))).astype(BF16)
            gate = jnp.dot(x, wgb[...], preferred_element_type=F32)
            up = jnp.dot(x, wub[...], preferred_element_type=F32)
            ybuf[slot] = _pack_pairs(
                jnp.dot((_silu(gate) * up).astype(BF16), wdb[...], preferred_element_type=F32))

        for b in range(count):
            g = g0 + j + b
            y_copy(g).start()

            @pl.when(g + ring < n_used)
            def _():
                x_copy(g + ring).start()

    def pair(p, carry):
        blocks(2 * p, 2)
        return carry

    lax.fori_loop(0, nb // 2, pair, 0)

    @pl.when(nb % 2 == 1)
    def _():
        blocks(nb - 1, 1)

    @pl.when(e == pl.num_programs(0) - 1)
    def _():
        for back in range(1, ring + 1):
            @pl.when(n_used - back >= 0)
            def _():
                y_copy(n_used - back).wait()


def _experts(blk_start, counts, n_used, xs, wg, wu, wd):
    n_slots, dp = xs.shape
    n_exp, d, ff = wg.shape
    bm, ring = EXPERT_BLOCK, EXPERT_RING
    wsel = lambda e, st, ct, nu: (e, 0, 0)
    grid_spec = pltpu.PrefetchScalarGridSpec(
        num_scalar_prefetch=3, grid=(n_exp,),
        in_specs=[pl.BlockSpec(memory_space=pl.ANY),
                  pl.BlockSpec((None, d, ff), wsel), pl.BlockSpec((None, d, ff), wsel),
                  pl.BlockSpec((None, ff, d), wsel)],
        out_specs=pl.BlockSpec(memory_space=pl.ANY),
        scratch_shapes=[pltpu.VMEM((d, ff), BF16), pltpu.VMEM((d, ff), BF16), pltpu.VMEM((ff, d), BF16),
                        pltpu.VMEM((ring, bm, dp), U32), pltpu.VMEM((ring, bm, dp), U32),
                        pltpu.SemaphoreType.DMA((ring,)), pltpu.SemaphoreType.DMA((ring,))])
    return pl.pallas_call(
        _expert_kernel, out_shape=jax.ShapeDtypeStruct((n_slots, dp), U32), grid_spec=grid_spec,
        compiler_params=_params(1), name="experts",
    )(blk_start, counts, n_used, xs, wg, wu, wd)


def _sc_gather_rows(table, idx):
    n_idx = idx.shape[0]
    dp = table.shape[1]
    workers = SC_CORES * SC_SUBCORES
    per_worker = n_idx // workers
    chunk = SC_GATHER_ROWS

    @functools.partial(
        pl.kernel, mesh=_sc_mesh(), out_type=jax.ShapeDtypeStruct((n_idx, dp), table.dtype),
        scratch_types=[pltpu.VMEM((chunk,), jnp.int32), pltpu.VMEM((chunk, dp), table.dtype),
                       pltpu.SemaphoreType.DMA],
        name="sc_gather_rows")
    def gather(table_hbm, idx_hbm, out_hbm, idx_v, rows_v, sem):
        base = (lax.axis_index("s") * SC_CORES + lax.axis_index("c")) * per_worker

        @pl.loop(0, per_worker // chunk)
        def _(j):
            off = base + j * chunk
            pltpu.sync_copy(idx_hbm.at[pl.ds(off, chunk)], idx_v)
            pltpu.async_copy(table_hbm.at[idx_v], rows_v, sem).wait()
            pltpu.sync_copy(rows_v, out_hbm.at[pl.ds(off, chunk)])

    return gather(table, idx)


def _combine_kernel(tile, yt_ref, w_ref, h_ref, x1_ref, g2_ref, sg_ref, su_ref, sd_ref, lg_ref, lb_ref, *rest):
    o_ref = rest[-1]
    hb = _unpack_pairs(h_ref[...]).astype(BF16)
    g = jnp.dot(hb, sg_ref[...], preferred_element_type=F32)
    u = jnp.dot(hb, su_ref[...], preferred_element_type=F32)
    f = jnp.dot((_silu(g) * u).astype(BF16), sd_ref[...], preferred_element_type=F32)
    w = w_ref[...]
    for kk in range(TOP_K):
        f = f + w[:, kk:kk + 1] * _unpack_pairs(yt_ref[kk * tile:(kk + 1) * tile, :])
    o_ref[...] = _layer_norm(DEEPNORM_ALPHA * x1_ref[...] + g2_ref[0] * f, lg_ref[...], lb_ref[...])


def _combine(y_tok, top_w, h2p, x1, g2, sg, su, sd, lg, lb, tile, tiles_per_batch, first_tile, prev_out):
    n, d = x1.shape
    dp = h2p.shape[1]
    row = lambda i: (first_tile + i, 0)
    full = lambda i: (0, 0)

    def const(a):
        return pl.BlockSpec(a.shape, full, pipeline_mode=pl.Buffered(1))

    in_specs = [pl.BlockSpec((TOP_K * tile, dp), lambda i: (i, 0)),
                pl.BlockSpec((tile, TOP_K), row), pl.BlockSpec((tile, dp), row), pl.BlockSpec((tile, d), row),
                pl.BlockSpec((1, 1, d), lambda i: ((first_tile + i) // tiles_per_batch, 0, 0)),
                const(sg), const(su), const(sd), const(lg), const(lb)]
    args = [y_tok, top_w, h2p, x1, g2, sg, su, sd, lg, lb]
    aliases = {}
    if prev_out is not None:
        in_specs.append(pl.BlockSpec(memory_space=pl.ANY))
        args.append(prev_out)
        aliases = {len(args) - 1: 0}
    return pl.pallas_call(
        functools.partial(_combine_kernel, tile),
        out_shape=jax.ShapeDtypeStruct((n, d), F32),
        grid=(y_tok.shape[0] // (TOP_K * tile),),
        in_specs=in_specs,
        out_specs=pl.BlockSpec((tile, d), row),
        input_output_aliases=aliases,
        compiler_params=_params(1), name="combine_shared_ln2",
    )(*args)


def kernel(x, c, ctx, c_ctx, w_mod, b_mod, w_in, na_rpb, gla_w_decay_f, gla_b_decay_f, gla_w_decay_b, gla_b_decay_b,
           gla_norm_w, w_branch_a, w_branch_b, w_out, ln1_g, ln1_b, router_w, router_bias, exp_w_gate, exp_w_up,
           exp_w_down, sh_w_gate, sh_w_up, sh_w_down, ln2_g, ln2_b):
    batch, t_len, d = x.shape
    ctx_len = ctx.shape[1]
    n = batch * t_len
    assert w_mod.shape[0] == DEPTH == 1
    assert t_len % GLA_CHUNK == 0 and ctx_len == GLA_CHUNK and (t_len // GRID_W) % NA_ROW_BLOCK == 0

    mod_rows = 16
    c_all = jnp.zeros((mod_rows, d), F32).at[:batch].set(c).at[batch].set(c_ctx)
    mod = _modulation(c_all, w_mod[0], b_mod[0])
    sh1, sc1, g1, sh2, sc2, g2 = [mod[:batch, j * d:(j + 1) * d].reshape(batch, 1, d) for j in range(6)]
    sh1c = mod[batch:batch + 1, 0:d].reshape(1, 1, d)
    sc1c = mod[batch:batch + 1, d:2 * d].reshape(1, 1, d)

    offs = np.cumsum((0, NA_WIDTH, NA_WIDTH, NA_WIDTH, GLA_QK_WIDTH, GLA_QK_WIDTH, GLA_V_WIDTH, GLA_V_WIDTH,
                      GLA_GATE_RANK, GLA_GATE_RANK, d, d))
    qa, ka, va, qb, kb, vbc, gbc, lrf, lrb, ga, gbt = [w_in[0][:, offs[j]:offs[j + 1]] for j in range(11)]
    lr_cols = jnp.concatenate([lrf, lrb, jnp.zeros((d, LANES - 2 * GLA_GATE_RANK), F32)], axis=1)
    w_lat = jnp.concatenate([lr_cols, qa, ka, va, qb, kb, vbc, gbc, ga, gbt], axis=1).astype(BF16)
    lat_offs = np.cumsum((0, LANES, NA_WIDTH, NA_WIDTH, NA_WIDTH, GLA_QK_WIDTH, GLA_QK_WIDTH, GLA_V_WIDTH))
    w_ctx = jnp.concatenate([w_lat[:, lat_offs[j]:lat_offs[j + 1]] for j in (0, 2, 3, 5, 6)], axis=1)
    plain = ("plain",)
    lat_plan = ((LANES, (("decay",),)),
                (3 * NA_WIDTH, (("scale", NA_HEAD_DIM ** -0.5), plain, plain)),
                (2 * GLA_QK_WIDTH, (("rope", GLA_DK ** -0.5), ("rope", 1.0))),
                (GLA_V_WIDTH, (plain, plain)), (GLA_V_WIDTH, (plain, plain)),
                (2 * d, (plain,) * 4))
    ctx_plan = ((LANES, (("decay",),)),
                (2 * NA_WIDTH, (plain, plain)), (GLA_QK_WIDTH, (plain,)), (GLA_V_WIDTH, (plain, plain)))
    w2, b2 = _decay_weights(gla_w_decay_f[0], gla_b_decay_f[0], gla_w_decay_b[0], gla_b_decay_b[0])
    tile = 256
    x2d = x.reshape(n, d)
    decay, na_qkv, gla_qk, vb, gb, gates = _projection(
        x2d, sc1, sh1, w_lat, w2, b2, lat_plan, (BF16,) * 6, PROJ_TILE, t_len // PROJ_TILE,
        rope=_rope_tables(t_len))
    decay_c, na_kv_c, k_c, v_c = _projection(
        ctx.reshape(batch * ctx_len, d), sc1c, sh1c, w_ctx, w2, b2, ctx_plan, (BF16,) * 4, tile,
        batch * ctx_len // tile)

    out_a = _na_attention(na_qkv, na_kv_c, _na_bias_tables(na_rpb[0]), batch, t_len, ctx_len)
    o_f, o_b = _gla(gla_qk, vb, decay, k_c, v_c, decay_c, batch, t_len, ctx_len)

    x1, h2p, scores_t = _merge(
        out_a.reshape(n, NA_WIDTH), o_f.reshape(n, GLA_V_WIDTH), o_b.reshape(n, GLA_V_WIDTH), gb, gates, x2d,
        g1, sc2, sh2, w_branch_a[0].astype(BF16), w_branch_b[0].astype(BF16), w_out[0].astype(BF16),
        gla_norm_w[0].reshape(1, -1), ln1_g[0].reshape(1, d), ln1_b[0].reshape(1, d),
        router_w[0].T.astype(BF16), tile, t_len // tile)

    top_e, top_w, rank, counts = _route(scores_t, router_bias[0], 512)

    counts = counts[:, 0]
    n_blocks = n * TOP_K // EXPERT_BLOCK + N_EXPERTS
    blocks_per = (counts + EXPERT_BLOCK - 1) // EXPERT_BLOCK
    blk_end = jnp.cumsum(blocks_per)
    blk_start = blk_end - blocks_per
    dest = _slots(top_e, rank, (blk_start * EXPERT_BLOCK).astype(F32).reshape(N_EXPERTS, 1), 512)

    xs = _sc_scatter_rows(h2p, dest.reshape(TOP_K * n), n_blocks * EXPERT_BLOCK)
    y = _experts(blk_start.astype(jnp.int32), counts, blk_end[-1:].astype(jnp.int32), xs,
                 exp_w_gate[0], exp_w_up[0], exp_w_down[0])
    dest_tok = dest.reshape(TOP_K, n // tile, tile).transpose(1, 0, 2).reshape(COMBINE_CHUNKS, -1)
    w_rows = top_w.T
    shared = (sh_w_gate[0].astype(BF16), sh_w_up[0].astype(BF16), sh_w_down[0].astype(BF16),
              ln2_g[0].reshape(1, d), ln2_b[0].reshape(1, d))
    tiles_per_chunk = n // tile // COMBINE_CHUNKS
    out = None
    for ci in range(COMBINE_CHUNKS):
        y_tok = _sc_gather_rows(y, dest_tok[ci])
        out = _combine(y_tok, w_rows, h2p, x1, g2, *shared, tile, t_len // tile, ci * tiles_per_chunk, out)
    return out.reshape(batch, t_len, d)
```

```python
import functools

import numpy as np
import jax
import jax.numpy as jnp
from jax import lax
from jax.experimental import pallas as pl
from jax.experimental.pallas import tpu as pltpu
from jax.experimental.pallas import tpu_sc as plsc

F32 = jnp.float32
BF16 = jnp.bfloat16
U32 = jnp.uint32
HIGHEST = lax.Precision.HIGHEST

GRID_W = 64
NA_HEADS = 8
NA_HEAD_DIM = 64
NA_WIDTH = NA_HEADS * NA_HEAD_DIM
WIN_ROWS = 8
WIN_COLS = 16
GLA_HEADS = 4
GLA_DK = 128
GLA_DV = 256
GLA_QK_WIDTH = GLA_HEADS * GLA_DK
GLA_V_WIDTH = GLA_HEADS * GLA_DV
GLA_GATE_RANK = 16
GLA_TAU = 16.0
LOG2E = 1.4426950408889634
ROPE_BASE = 10000.0
N_EXPERTS = 256
TOP_K = 8
N_GROUPS = 8
TOPK_GROUPS = 4
GROUP_SIZE = N_EXPERTS // N_GROUPS
ROUTED_SCALE = 2.5
DEPTH = 1
DEEPNORM_ALPHA = (2 * DEPTH) ** 0.25
EPS = 1e-6

LANES = 128
PROJ_TILE = 512
NA_ROW_BLOCK = 4
NA_UNION_ROWS = NA_ROW_BLOCK + WIN_ROWS - 1
GLA_CHUNK = 256
GLA_DIAG = 16
GLA_HEADS_PER_STEP = 4
EXPERT_BLOCK = 256
COMBINE_CHUNKS = 8
EXPERT_RING = 4
SC_CORES = 2
SC_SUBCORES = 16
SC_GATHER_ROWS = 128
NEG_BIG = -1e30
VMEM_LIMIT = 56 * 1024 * 1024


def _params(n_axes, vmem=VMEM_LIMIT):
    return pltpu.CompilerParams(dimension_semantics=("arbitrary",) * n_axes, vmem_limit_bytes=vmem)


def _sigmoid(v):
    return 1.0 / (1.0 + jnp.exp(-v))


def _silu(v):
    return v * _sigmoid(v)


def _pack_pairs(v):
    m = v.shape[1] // 2
    lo = lax.bitcast_convert_type(v[:, :m].astype(BF16).astype(F32), U32) >> 16
    hi = lax.bitcast_convert_type(v[:, m:].astype(BF16).astype(F32), U32) & jnp.uint32(0xFFFF0000)
    return lo | hi


def _unpack_pairs(p):
    lo = lax.bitcast_convert_type(p << 16, F32)
    hi = lax.bitcast_convert_type(p & jnp.uint32(0xFFFF0000), F32)
    return jnp.concatenate([lo, hi], axis=1)


def _mod_kernel(c_ref, w_ref, b_ref, o_ref):
    o_ref[...] = jnp.dot(_silu(c_ref[...]), w_ref[...], preferred_element_type=F32, precision=HIGHEST) + b_ref[...]


def _modulation(c_all, w_mod, b_mod):
    rows, d = c_all.shape
    n = w_mod.shape[1]
    bn = 512
    return pl.pallas_call(
        _mod_kernel,
        out_shape=jax.ShapeDtypeStruct((rows, n), F32),
        grid=(n // bn,),
        in_specs=[pl.BlockSpec((rows, d), lambda j: (0, 0)),
                  pl.BlockSpec((d, bn), lambda j: (0, j)),
                  pl.BlockSpec((1, bn), lambda j: (0, j))],
        out_specs=pl.BlockSpec((rows, bn), lambda j: (0, j)),
        compiler_params=_params(1),
        name="modulation",
    )(c_all, w_mod, b_mod.reshape(1, n))


def _swap32(v):
    lane = lax.broadcasted_iota(jnp.int32, v.shape, 1)
    return jnp.where((lane % 64) < 32, pltpu.roll(v, 96, 1), pltpu.roll(v, 32, 1))


GLA_DECAY_WIDTH = 2 * GLA_HEADS * 2 * GLA_DK


def _log2_decay_split(lr, w2, b2, out_ref):
    z = (jnp.dot(lr.astype(BF16), w2, preferred_element_type=F32) + b2) * LOG2E
    a = (jnp.minimum(z, 0.0) - jnp.log2(1.0 + jnp.exp2(-jnp.abs(z)))) * (1.0 / GLA_TAU)
    hi = a.astype(BF16)
    lo = (a - hi.astype(F32)).astype(BF16)
    for p in range(a.shape[1] // GLA_DK):
        src = slice(p * GLA_DK, (p + 1) * GLA_DK)
        out_ref[:, 2 * p * GLA_DK:(2 * p + 1) * GLA_DK] = hi[:, src]
        out_ref[:, (2 * p + 1) * GLA_DK:(2 * p + 2) * GLA_DK] = lo[:, src]


def _proj_kernel(plan, has_rope, *refs):
    x_ref, sc_ref, sh_ref, w_ref, w2_ref, b2_ref = refs[:6]
    pos = 6
    if has_rope:
        cos_ref, sin_ref = refs[6:8]
        pos = 8
    out_refs = refs[pos:]
    h = (x_ref[...] * (1.0 + sc_ref[0]) + sh_ref[0]).astype(BF16)
    col = 0
    for out_ref, (width, kinds) in zip(out_refs, plan):
        for j, kind in enumerate(kinds):
            cw = width // len(kinds)
            c0 = j * cw
            acc = jnp.dot(h, w_ref[:, col + c0:col + c0 + cw], preferred_element_type=F32)
            if kind[0] == "decay":
                _log2_decay_split(acc, w2_ref[...], b2_ref[...], out_ref)
                continue
            if kind[0] == "scale":
                acc = acc * kind[1]
            elif kind[0] == "rope":
                cos, sin = cos_ref[...], sin_ref[...]
                pieces = []
                for p in range(cw // LANES):
                    v = acc[:, p * LANES:(p + 1) * LANES]
                    pieces.append((v * cos + _swap32(v) * sin) * kind[1])
                acc = jnp.concatenate(pieces, axis=1)
            out_ref[:, c0:c0 + cw] = acc.astype(out_ref.dtype)
        col += width


def _projection(x2d, sc, sh, w, w2, b2, plan, out_dtypes, tile, tiles_per_mod, rope=None):
    n, d = x2d.shape
    const = lambda a: pl.BlockSpec(a.shape, lambda i: (0, 0), pipeline_mode=pl.Buffered(1))
    in_specs = [pl.BlockSpec((tile, d), lambda i: (i, 0)),
                pl.BlockSpec((1, 1, d), lambda i: (i // tiles_per_mod, 0, 0)),
                pl.BlockSpec((1, 1, d), lambda i: (i // tiles_per_mod, 0, 0)),
                const(w), const(w2), const(b2)]
    args = [x2d, sc, sh, w, w2, b2]
    if rope is not None:
        in_specs += [pl.BlockSpec((tile, LANES), lambda i: (i % tiles_per_mod, 0))] * 2
        args += list(rope)
    widths = [GLA_DECAY_WIDTH if kinds[0][0] == "decay" else wd for wd, kinds in plan]
    out_shape = [jax.ShapeDtypeStruct((n, wd), dt) for wd, dt in zip(widths, out_dtypes)]
    out_specs = [pl.BlockSpec((tile, wd), lambda i: (i, 0)) for wd in widths]
    return pl.pallas_call(
        functools.partial(_proj_kernel, plan, rope is not None),
        out_shape=out_shape,
        grid=(n // tile,),
        in_specs=in_specs,
        out_specs=out_specs,
        compiler_params=_params(1),
        name="in_proj" if rope is not None else "ctx_proj",
    )(*args)


def _rope_tables(t_len):
    half = GLA_DK // 2
    quarter = half // 2
    f32 = np.float32
    inv_freq = f32(ROPE_BASE) ** (-np.arange(quarter, dtype=f32) / f32(quarter))
    pos = np.arange(t_len)
    row_ang = (pos // GRID_W).astype(f32)[:, None] * inv_freq[None, :]
    col_ang = (pos % GRID_W).astype(f32)[:, None] * inv_freq[None, :]
    cr, sr, cc, sn = np.cos(row_ang), np.sin(row_ang), np.cos(col_ang), np.sin(col_ang)
    cos = np.concatenate([cr, cr, cc, cc], axis=1).astype(f32)
    sin = np.concatenate([-sr, sr, -sn, sn], axis=1).astype(f32)
    return jnp.asarray(cos), jnp.asarray(sin)


def _na_bias_tables(rpb):
    rb, ur, w = NA_ROW_BLOCK, NA_UNION_ROWS, GRID_W
    heads = rpb.shape[0]
    pad = jnp.pad(rpb, ((0, 0), (0, 0), (w, w)))
    toep = jnp.stack([pad[:, :, w + WIN_COLS - 1 - c:2 * w + WIN_COLS - 1 - c] for c in range(w)], axis=2)
    c = np.arange(w)[:, None]
    kc = np.arange(w)[None, :]
    col_start = np.clip(c - WIN_COLS // 2, 0, w - WIN_COLS)
    col_ok = (kc >= col_start) & (kc < col_start + WIN_COLS)
    toep = jnp.where(col_ok[None, None], toep, NEG_BIG)
    neg = jnp.full((heads, w, w), NEG_BIG, F32)
    half = WIN_ROWS // 2
    tables = []
    for lo, off in ((lambda i: 0, WIN_ROWS - 1), (lambda i: i, WIN_ROWS - 1 - half), (lambda i: ur - WIN_ROWS, 0)):
        rows_ = []
        for i in range(rb):
            blocks = [toep[:, j - i + off] if lo(i) <= j < lo(i) + WIN_ROWS else neg for j in range(ur)]
            rows_.append(jnp.concatenate(blocks, axis=2))
        tables.append(jnp.concatenate(rows_, axis=1))
    return jnp.stack(tables)


def _na_kernel(rows, q_ref, k_ref, v_ref, kc_ref, vc_ref, bias_ref, o_ref):
    rb = pl.program_id(1)
    ustart = jnp.clip(rb * NA_ROW_BLOCK - WIN_ROWS // 2, 0, rows - NA_UNION_ROWS)
    k0 = pl.multiple_of(ustart * GRID_W, GRID_W)
    nk = NA_UNION_ROWS * GRID_W
    nt = (((1,), (1,)), ((), ()))
    lane = lax.broadcasted_iota(jnp.int32, (NA_ROW_BLOCK * GRID_W, LANES), 1)
    for p in range(NA_WIDTH // LANES):
        ls = slice(p * LANES, (p + 1) * LANES)
        q = q_ref[:, ls]
        ku = k_ref[pl.ds(k0, nk), ls]
        vu = v_ref[pl.ds(k0, nk), ls]
        kc = kc_ref[:, ls]
        vc = vc_ref[:, ls]
        out = jnp.zeros(q.shape, F32)
        for hh in range(LANES // NA_HEAD_DIM):
            mine = (lane // NA_HEAD_DIM) == hh
            qm = jnp.where(mine, q, jnp.zeros_like(q))
            s_loc = lax.dot_general(qm, ku, nt, preferred_element_type=F32) + bias_ref[p * 2 + hh]
            s_ctx = lax.dot_general(qm, kc, nt, preferred_element_type=F32)
            m = jnp.maximum(jnp.max(s_loc, axis=1, keepdims=True), jnp.max(s_ctx, axis=1, keepdims=True))
            p_loc = jnp.exp(s_loc - m)
            p_ctx = jnp.exp(s_ctx - m)
            denom = jnp.sum(p_loc, axis=1, keepdims=True) + jnp.sum(p_ctx, axis=1, keepdims=True)
            o = (jnp.dot(p_loc.astype(BF16), vu, preferred_element_type=F32)
                 + jnp.dot(p_ctx.astype(BF16), vc, preferred_element_type=F32)) / denom
            out = jnp.where(mine, o, out)
        o_ref[:, ls] = out.astype(o_ref.dtype)


def _na_attention(na_qkv, na_kv_ctx, bias_tables, batch, t_len, ctx_len):
    rows = t_len // GRID_W
    n_rb = rows // NA_ROW_BLOCK
    tq = NA_ROW_BLOCK * GRID_W
    qkv = na_qkv.reshape(batch, t_len, 3 * NA_WIDTH)
    kvc = na_kv_ctx.reshape(batch, ctx_len, 2 * NA_WIDTH)

    def bias_idx(b, r):
        return (jnp.where(r == 0, 0, jnp.where(r == n_rb - 1, 2, 1)), 0, 0, 0)

    return pl.pallas_call(
        functools.partial(_na_kernel, rows),
        out_shape=jax.ShapeDtypeStruct((batch, t_len, NA_WIDTH), BF16),
        grid=(batch, n_rb),
        in_specs=[pl.BlockSpec((None, tq, NA_WIDTH), lambda b, r: (b, r, 0)),
                  pl.BlockSpec((None, t_len, NA_WIDTH), lambda b, r: (b, 0, 1)),
                  pl.BlockSpec((None, t_len, NA_WIDTH), lambda b, r: (b, 0, 2)),
                  pl.BlockSpec((None, ctx_len, NA_WIDTH), lambda b, r: (b, 0, 0)),
                  pl.BlockSpec((None, ctx_len, NA_WIDTH), lambda b, r: (b, 0, 1)),
                  pl.BlockSpec((None,) + bias_tables.shape[1:], bias_idx)],
        out_specs=pl.BlockSpec((None, tq, NA_WIDTH), lambda b, r: (b, r, 0)),
        compiler_params=_params(2),
        name="na_attention",
    )(qkv, qkv, qkv, kvc, kvc, bias_tables)


def _gla_constants(c):
    tris, masks = [], []
    for reverse in (False, True):
        i = np.arange(c)[:, None]
        j = np.arange(c)[None, :]
        tris.append((j >= i) if reverse else (j <= i))
        i = np.arange(c // 2)[:, None]
        j = np.arange(c // 2)[None, :]
        if reverse:
            i, j = j, i
        level = []
        s = c // 4
        while s >= GLA_DIAG:
            level.append(((i // (2 * s)) == (j // (2 * s))) & ((i % (2 * s)) >= s) & ((j % (2 * s)) < s))
            s //= 2
        level.append(((i // GLA_DIAG) == (j // GLA_DIAG)) & (j <= i))
        masks.append(np.stack(level))
    return jnp.asarray(np.stack(tris), BF16), jnp.asarray(np.stack(masks), F32)


def _block_refs(cum, s, reverse, diag):
    c = cum.shape[0]
    span = s if diag else 2 * s
    parts = []
    for p in range(c // span):
        if diag:
            r = p * span + (span - 1 if reverse else 0)
        else:
            r = p * span + (s - 1 if reverse else s)
        parts.append(jnp.broadcast_to(cum[r:r + 1, :], (span, cum.shape[1])))
    return jnp.concatenate(parts, axis=0)


def _cumsum_rows(a_hl, tri):
    parts = jnp.dot(tri, a_hl, preferred_element_type=F32)
    w = a_hl.shape[1] // 2
    return parts[:, :w] + parts[:, w:]


def _gla_chunk(q, k, v, a, state_t, tri, mask_ref, reverse, want_out):
    c = k.shape[0]
    hc = c // 2
    nt = (((1,), (1,)), ((), ()))
    k = k.astype(F32)
    q = q.astype(F32) if want_out else None
    cum = _cumsum_rows(a, tri)
    last = 0 if reverse else c - 1
    total = cum[last:last + 1, :]

    def scale(x, log2_factor):
        return (x * jnp.exp2(log2_factor)).astype(BF16)

    out = None
    if want_out:
        out = lax.dot_general(scale(q, cum), state_t.astype(BF16), nt, preferred_element_type=F32)
        halves = (slice(hc, c), slice(0, hc)) if reverse else (slice(0, hc), slice(hc, c))
        early, late = halves
        r = hc - 1 if reverse else hc
        g = cum[r:r + 1, :]
        cross = lax.dot_general(scale(q[late], cum[late] - g), scale(k[early], g - cum[early]), nt,
                                preferred_element_type=F32)
        inner = []
        for rows in halves:
            cx, qx, kx = cum[rows], q[rows], k[rows]
            acc = jnp.zeros((hc, hc), F32)
            s = hc // 2
            level = 0
            while True:
                diag = s < GLA_DIAG
                d = cx - _block_refs(cx, GLA_DIAG if diag else s, reverse, diag)
                if diag:
                    qs, ks = scale(qx, d), scale(kx, -d)
                else:
                    e = jnp.exp2(-jnp.abs(d))
                    qs, ks = (qx * e).astype(BF16), (kx * e).astype(BF16)
                acc = acc + lax.dot_general(qs, ks, nt, preferred_element_type=F32) * mask_ref[level]
                if diag:
                    break
                s //= 2
                level += 1
            inner.append(acc)
        out_early = jnp.dot(inner[0].astype(BF16), v[early], preferred_element_type=F32)
        out_late = jnp.dot(jnp.concatenate([cross, inner[1]], axis=1).astype(BF16),
                           jnp.concatenate([v[early], v[late]], axis=0), preferred_element_type=F32)
        intra = [out_late, out_early] if reverse else [out_early, out_late]
        out = out + jnp.concatenate(intra, axis=0)
    upd = lax.dot_general(v, scale(k, total - cum), (((0,), (0,)), ((), ())), preferred_element_type=F32)
    return out, jnp.exp2(total) * state_t + upd


def _gla_kernel(qf_ref, kf_ref, vf_ref, af_ref, qb_ref, kb_ref, vb_ref, ab_ref,
                kc_ref, vc_ref, acf_ref, acb_ref, tri_ref, mask_ref,
                of_ref, ob_ref, sf_ref, sb_ref):
    c = pl.program_id(2)
    fm, bm = mask_ref.at[0], mask_ref.at[1]
    heads = sf_ref.shape[0]

    @pl.when(c == 0)
    def _():
        zero = jnp.zeros(sf_ref.shape[1:], F32)
        for h in range(heads):
            kc = kc_ref[:, h * GLA_DK:(h + 1) * GLA_DK]
            vc = vc_ref[:, h * GLA_DV:(h + 1) * GLA_DV]
            hl = slice(2 * h * GLA_DK, 2 * (h + 1) * GLA_DK)
            _, sf_ref[h] = _gla_chunk(None, kc, vc, acf_ref[:, hl], zero, tri_ref[0], fm, False, False)
            _, sb_ref[h] = _gla_chunk(None, kc, vc, acb_ref[:, hl], zero, tri_ref[1], bm, True, False)

    @pl.when(c > 0)
    def _():
        for h in range(heads):
            ks = slice(h * GLA_DK, (h + 1) * GLA_DK)
            vs = slice(h * GLA_DV, (h + 1) * GLA_DV)
            hl = slice(2 * h * GLA_DK, 2 * (h + 1) * GLA_DK)
            o, sf_ref[h] = _gla_chunk(qf_ref[:, ks], kf_ref[:, ks], vf_ref[:, vs], af_ref[:, hl], sf_ref[h],
                                      tri_ref[0], fm, False, True)
            of_ref[:, vs] = o.astype(of_ref.dtype)
            o, sb_ref[h] = _gla_chunk(qb_ref[:, ks], kb_ref[:, ks], vb_ref[:, vs], ab_ref[:, hl], sb_ref[h],
                                      tri_ref[1], bm, True, True)
            ob_ref[:, vs] = o.astype(ob_ref.dtype)


def _gla(gla_qk, vb, decay, k_ctx, v_ctx, decay_ctx, batch, t_len, ctx_len):
    nc = t_len // GLA_CHUNK
    h = GLA_HEADS
    qk = gla_qk.reshape(batch, t_len, 2 * GLA_QK_WIDTH)
    v3 = vb.reshape(batch, t_len, GLA_V_WIDTH)
    a3 = decay.reshape(batch, t_len, GLA_DECAY_WIDTH)
    kc3 = k_ctx.reshape(batch, ctx_len, GLA_QK_WIDTH)
    vc3 = v_ctx.reshape(batch, ctx_len, GLA_V_WIDTH)
    ac3 = decay_ctx.reshape(batch, ctx_len, GLA_DECAY_WIDTH)
    tri, masks = _gla_constants(GLA_CHUNK)

    def fwd(c):
        return jnp.maximum(c - 1, 0)

    def bwd(c):
        return nc - 1 - jnp.maximum(c - 1, 0)

    def const(a):
        return pl.BlockSpec(a.shape, lambda b, hh, c: (0,) * a.ndim, pipeline_mode=pl.Buffered(1))

    hp = GLA_HEADS_PER_STEP
    groups = h // hp
    cq = (None, GLA_CHUNK, hp * GLA_DK)
    cv = (None, GLA_CHUNK, hp * GLA_DV)
    ca = (None, GLA_CHUNK, hp * 2 * GLA_DK)
    cca = (None, ctx_len, hp * 2 * GLA_DK)
    in_specs = [
        pl.BlockSpec(cq, lambda b, g, c: (b, fwd(c), g)),
        pl.BlockSpec(cq, lambda b, g, c: (b, fwd(c), groups + g)),
        pl.BlockSpec(cv, lambda b, g, c: (b, fwd(c), g)),
        pl.BlockSpec(ca, lambda b, g, c: (b, fwd(c), g)),
        pl.BlockSpec(cq, lambda b, g, c: (b, bwd(c), g)),
        pl.BlockSpec(cq, lambda b, g, c: (b, bwd(c), groups + g)),
        pl.BlockSpec(cv, lambda b, g, c: (b, bwd(c), g)),
        pl.BlockSpec(ca, lambda b, g, c: (b, bwd(c), groups + g)),
        pl.BlockSpec((None, ctx_len, hp * GLA_DK), lambda b, g, c: (b, 0, g)),
        pl.BlockSpec((None, ctx_len, hp * GLA_DV), lambda b, g, c: (b, 0, g)),
        pl.BlockSpec(cca, lambda b, g, c: (b, 0, g)),
        pl.BlockSpec(cca, lambda b, g, c: (b, 0, groups + g)),
        const(tri), const(masks),
    ]
    out_specs = [pl.BlockSpec(cv, lambda b, g, c: (b, fwd(c), g)),
                 pl.BlockSpec(cv, lambda b, g, c: (b, bwd(c), g))]
    out_shape = [jax.ShapeDtypeStruct((batch, t_len, GLA_V_WIDTH), BF16)] * 2
    return pl.pallas_call(
        _gla_kernel,
        out_shape=out_shape,
        grid=(batch, groups, nc + 1),
        in_specs=in_specs,
        out_specs=out_specs,
        scratch_shapes=[pltpu.VMEM((hp, GLA_DV, GLA_DK), F32), pltpu.VMEM((hp, GLA_DV, GLA_DK), F32)],
        compiler_params=_params(3),
        name="gla",
    )(qk, qk, v3, a3, qk, qk, v3, a3, kc3, vc3, ac3, ac3, tri, masks)


def _decay_weights(w_f, b_f, w_b, b_b):
    r, width = w_f.shape
    w2 = jnp.zeros((LANES, 2 * width), F32).at[:r, :width].set(w_f).at[r:2 * r, width:].set(w_b)
    return w2.astype(BF16), jnp.concatenate([b_f, b_b]).reshape(1, 2 * width)


def _layer_norm(v, g, b):
    mu = jnp.mean(v, axis=1, keepdims=True)
    var = jnp.mean(jnp.square(v - mu), axis=1, keepdims=True)
    return (v - mu) * lax.rsqrt(var + EPS) * g + b


def _merge_kernel(oa_ref, of_ref, ob_ref, gb_ref, gates_ref, x_ref, g1_ref, sc2_ref, sh2_ref,
                  wa_ref, wb_ref, wo_ref, nw_ref, lg_ref, lb_ref, rw_ref,
                  x1_ref, h2_ref, sc_ref):
    d = x_ref.shape[1]
    o = of_ref[...].astype(F32) + ob_ref[...].astype(F32)
    pieces = []
    for hh in range(GLA_HEADS):
        oh = o[:, hh * GLA_DV:(hh + 1) * GLA_DV]
        pieces.append(oh * lax.rsqrt(jnp.mean(jnp.square(oh), axis=1, keepdims=True) + EPS))
    out_b = jnp.concatenate(pieces, axis=1) * nw_ref[...] * _silu(gb_ref[...].astype(F32))
    ya = jnp.dot(oa_ref[...], wa_ref[...], preferred_element_type=F32)
    yb = jnp.dot(out_b.astype(BF16), wb_ref[...], preferred_element_type=F32)
    y = _sigmoid(gates_ref[:, :d].astype(F32)) * ya + _sigmoid(gates_ref[:, d:].astype(F32)) * yb
    y2 = jnp.dot(y.astype(BF16), wo_ref[...], preferred_element_type=F32)
    x1 = _layer_norm(DEEPNORM_ALPHA * x_ref[...] + g1_ref[0] * y2, lg_ref[...], lb_ref[...])
    x1_ref[...] = x1
    h2 = x1 * (1.0 + sc2_ref[0]) + sh2_ref[0]
    h2_ref[...] = _pack_pairs(h2)
    logits_t = lax.dot_general(rw_ref[...], h2.astype(BF16), (((1,), (1,)), ((), ())), preferred_element_type=F32)
    sc_ref[...] = _sigmoid(logits_t)


def _merge(out_a, o_f, o_b, gb, gates, x2d, g1, sc2, sh2, wa, wb, wo, nw, lg, lb, rw_t, tile, tiles_per_batch):
    n, d = x2d.shape
    row = lambda i: (i, 0)
    mod = lambda i: (i // tiles_per_batch, 0, 0)
    full = lambda i: (0, 0)

    def const(a):
        return pl.BlockSpec(a.shape, full, pipeline_mode=pl.Buffered(1))

    in_specs = [pl.BlockSpec((tile, NA_WIDTH), row), pl.BlockSpec((tile, GLA_V_WIDTH), row),
                pl.BlockSpec((tile, GLA_V_WIDTH), row), pl.BlockSpec((tile, GLA_V_WIDTH), row),
                pl.BlockSpec((tile, 2 * d), row), pl.BlockSpec((tile, d), row),
                pl.BlockSpec((1, 1, d), mod), pl.BlockSpec((1, 1, d), mod), pl.BlockSpec((1, 1, d), mod),
                const(wa), const(wb), const(wo), const(nw), const(lg), const(lb), const(rw_t)]
    out_shape = [jax.ShapeDtypeStruct((n, d), F32), jax.ShapeDtypeStruct((n, d // 2), U32),
                 jax.ShapeDtypeStruct((N_EXPERTS, n), F32)]
    out_specs = [pl.BlockSpec((tile, d), row), pl.BlockSpec((tile, d // 2), row),
                 pl.BlockSpec((N_EXPERTS, tile), lambda i: (0, i))]
    return pl.pallas_call(
        _merge_kernel, out_shape=out_shape, grid=(n // tile,), in_specs=in_specs, out_specs=out_specs,
        compiler_params=_params(1), name="merge_ln1_router",
    )(out_a, o_f, o_b, gb, gates, x2d, g1, sc2, sh2, wa, wb, wo, nw, lg, lb, rw_t)


def _first_argmax(vals, idx, n):
    m = jnp.max(vals, axis=0, keepdims=True)
    first = jnp.min(jnp.where(vals == m, idx, float(n)), axis=0, keepdims=True)
    return m, first


def _route_kernel(sc_ref, bias_ref, e_ref, w_ref, rank_ref, cnt_ref, carry_ref):
    step = pl.program_id(0)
    tr = sc_ref.shape[1]

    @pl.when(step == 0)
    def _():
        carry_ref[...] = jnp.zeros(carry_ref.shape, F32)

    scores = sc_ref[...]
    biased = scores + bias_ref[...]
    eidx = lax.broadcasted_iota(jnp.int32, (N_EXPERTS, tr), 0).astype(F32)
    lidx = lax.broadcasted_iota(jnp.int32, (GROUP_SIZE, tr), 0).astype(F32)
    gidx = lax.broadcasted_iota(jnp.int32, (N_GROUPS, tr), 0).astype(F32)
    gs = []
    for g in range(N_GROUPS):
        blk = biased[g * GROUP_SIZE:(g + 1) * GROUP_SIZE]
        m1, first = _first_argmax(blk, lidx, GROUP_SIZE)
        m2 = jnp.max(jnp.where(lidx == first, -jnp.inf, blk), axis=0, keepdims=True)
        gs.append(m1 + m2)
    cur = jnp.concatenate(gs, axis=0)
    keep = jnp.zeros((N_GROUPS, tr), F32)
    for _ in range(TOPK_GROUPS):
        _, first = _first_argmax(cur, gidx, N_GROUPS)
        sel = gidx == first
        keep = jnp.where(sel, 1.0, keep)
        cur = jnp.where(sel, -jnp.inf, cur)
    keep_e = jnp.concatenate([jnp.broadcast_to(keep[g:g + 1], (GROUP_SIZE, tr)) for g in range(N_GROUPS)], axis=0)
    masked = jnp.where(keep_e > 0.5, biased, -jnp.inf)
    chosen = jnp.zeros((N_EXPERTS, tr), F32)
    tops, topi = [], []
    for _ in range(TOP_K):
        _, first = _first_argmax(masked, eidx, N_EXPERTS)
        sel = eidx == first
        tops.append(jnp.sum(jnp.where(sel, scores, 0.0), axis=0, keepdims=True))
        topi.append(first)
        chosen = jnp.where(sel, 1.0, chosen)
        masked = jnp.where(sel, -jnp.inf, masked)
    top_s = jnp.concatenate(tops, axis=0)
    top_i = jnp.concatenate(topi, axis=0)
    e_ref[...] = top_i.astype(jnp.int32)
    w_ref[...] = top_s / jnp.sum(top_s, axis=0, keepdims=True) * ROUTED_SCALE
    r = lax.broadcasted_iota(jnp.int32, (tr, tr), 0)
    cidx = lax.broadcasted_iota(jnp.int32, (tr, tr), 1)
    before = jnp.where(r < cidx, 1.0, 0.0).astype(BF16)
    prior = jnp.dot(chosen.astype(BF16), before, preferred_element_type=F32) + carry_ref[...]
    ranks = [jnp.sum(jnp.where(eidx == topi[kk], prior, 0.0), axis=0, keepdims=True) for kk in range(TOP_K)]
    rank_ref[...] = jnp.concatenate(ranks, axis=0).astype(jnp.int32)
    carry_ref[...] = carry_ref[...] + jnp.sum(chosen, axis=1, keepdims=True)
    cnt_ref[...] = jnp.broadcast_to(carry_ref[...], cnt_ref.shape).astype(jnp.int32)


def _route(scores_t, router_bias, tile):
    n = scores_t.shape[1]
    col = lambda i: (0, i)
    out_shape = [jax.ShapeDtypeStruct((TOP_K, n), jnp.int32), jax.ShapeDtypeStruct((TOP_K, n), F32),
                 jax.ShapeDtypeStruct((TOP_K, n), jnp.int32), jax.ShapeDtypeStruct((N_EXPERTS, LANES), jnp.int32)]
    return pl.pallas_call(
        _route_kernel, out_shape=out_shape, grid=(n // tile,),
        in_specs=[pl.BlockSpec((N_EXPERTS, tile), col), pl.BlockSpec((N_EXPERTS, 1), lambda i: (0, 0))],
        out_specs=[pl.BlockSpec((TOP_K, tile), col), pl.BlockSpec((TOP_K, tile), col),
                   pl.BlockSpec((TOP_K, tile), col), pl.BlockSpec((N_EXPERTS, LANES), lambda i: (0, 0))],
        scratch_shapes=[pltpu.VMEM((N_EXPERTS, 1), F32)],
        compiler_params=_params(1), name="route",
    )(scores_t, router_bias.reshape(N_EXPERTS, 1))


def _slots_kernel(e_ref, rank_ref, start_ref, dest_ref):
    tr = e_ref.shape[1]
    eidx = lax.broadcasted_iota(jnp.int32, (N_EXPERTS, tr), 0)
    e = e_ref[...]
    start = start_ref[...]
    rows = [jnp.sum(jnp.where(eidx == e[kk:kk + 1], start, 0.0), axis=0, keepdims=True) for kk in range(TOP_K)]
    dest_ref[...] = jnp.concatenate(rows, axis=0).astype(jnp.int32) + rank_ref[...]


def _slots(top_e, rank, start_rows, tile):
    n = top_e.shape[1]
    col = lambda i: (0, i)
    return pl.pallas_call(
        _slots_kernel, out_shape=jax.ShapeDtypeStruct((TOP_K, n), jnp.int32), grid=(n // tile,),
        in_specs=[pl.BlockSpec((TOP_K, tile), col), pl.BlockSpec((TOP_K, tile), col),
                  pl.BlockSpec((N_EXPERTS, 1), lambda i: (0, 0))],
        out_specs=pl.BlockSpec((TOP_K, tile), col),
        compiler_params=_params(1), name="slots",
    )(top_e, rank, start_rows)


def _sc_mesh():
    return plsc.VectorSubcoreMesh(core_axis_name="c", subcore_axis_name="s",
                                  num_cores=SC_CORES, num_subcores=SC_SUBCORES)


def _sc_scatter_rows(rows, idx, n_out):
    n, dp = rows.shape
    copies = idx.shape[0] // n
    per_worker = n // (SC_CORES * SC_SUBCORES)
    chunk = SC_GATHER_ROWS

    @functools.partial(
        pl.kernel, mesh=_sc_mesh(), out_type=jax.ShapeDtypeStruct((n_out, dp), rows.dtype),
        scratch_types=[pltpu.VMEM((chunk,), jnp.int32), pltpu.VMEM((chunk, dp), rows.dtype)],
        name="sc_scatter_rows")
    def scatter(rows_hbm, idx_hbm, out_hbm, idx_v, rows_v):
        base = (lax.axis_index("s") * SC_CORES + lax.axis_index("c")) * per_worker

        @pl.loop(0, per_worker // chunk)
        def _(j):
            off = base + j * chunk
            pltpu.sync_copy(rows_hbm.at[pl.ds(off, chunk)], rows_v)
            for k in range(copies):
                pltpu.sync_copy(idx_hbm.at[pl.ds(k * n + off, chunk)], idx_v)
                pltpu.sync_copy(rows_v, out_hbm.at[idx_v])

    return scatter(rows, idx)


def _expert_kernel(start_ref, cnt_ref, nused_ref, xs_ref, wg_ref, wu_ref, wd_ref, y_ref,
                   wgb, wub, wdb, xbuf, ybuf, sem_in, sem_out):
    e = pl.program_id(0)
    bm, ring = EXPERT_BLOCK, EXPERT_RING
    cnt = cnt_ref[e]
    nb = (cnt + bm - 1) // bm
    g0 = start_ref[e]
    n_used = nused_ref[0]

    def x_copy(g):
        slot = g % ring
        return pltpu.make_async_copy(xs_ref.at[pl.ds(g * bm, bm)], xbuf.at[slot], sem_in.at[slot])

    def y_copy(g):
        slot = g % ring
        return pltpu.make_async_copy(ybuf.at[slot], y_ref.at[pl.ds(g * bm, bm)], sem_out.at[slot])

    @pl.when(e == 0)
    def _():
        for g in range(ring):
            @pl.when(g < n_used)
            def _():
                x_copy(g).start()

    wgb[...] = wg_ref[...].astype(BF16)
    wub[...] = wu_ref[...].astype(BF16)
    wdb[...] = wd_ref[...].astype(BF16)
    row = lax.broadcasted_iota(jnp.int32, (bm, xbuf.shape[2]), 0)

    def blocks(j, count):
        for b in range(count):
            g = g0 + j + b
            x_copy(g).wait()

            @pl.when(g >= ring)
            def _():
                y_copy(g - ring).wait()

        for b in range(count):
            slot = (g0 + j + b) % ring
            x = _unpack_pairs(jnp.where(row < cnt - (j + b) * bm, xbuf[slot], jnp.uint32(0))).astype(BF16)
            gate = jnp.dot(x, wgb[...], preferred_element_type=F32)
            up = jnp.dot(x, wub[...], preferred_element_type=F32)
            ybuf[slot] = _pack_pairs(
                jnp.dot((_silu(gate) * up).astype(BF16), wdb[...], preferred_element_type=F32))

        for b in range(count):
            g = g0 + j + b
            y_copy(g).start()

            @pl.when(g + ring < n_used)
            def _():
                x_copy(g + ring).start()

    def pair(p, carry):
        blocks(2 * p, 2)
        return carry

    lax.fori_loop(0, nb // 2, pair, 0)

    @pl.when(nb % 2 == 1)
    def _():
        blocks(nb - 1, 1)

    @pl.when(e == pl.num_programs(0) - 1)
    def _():
        for back in range(1, ring + 1):
            @pl.when(n_used - back >= 0)
            def _():
                y_copy(n_used - back).wait()


def _experts(blk_start, counts, n_used, xs, wg, wu, wd):
    n_slots, dp = xs.shape
    n_exp, d, ff = wg.shape
    bm, ring = EXPERT_BLOCK, EXPERT_RING
    wsel = lambda e, st, ct, nu: (e, 0, 0)
    grid_spec = pltpu.PrefetchScalarGridSpec(
        num_scalar_prefetch=3, grid=(n_exp,),
        in_specs=[pl.BlockSpec(memory_space=pl.ANY),
                  pl.BlockSpec((None, d, ff), wsel), pl.BlockSpec((None, d, ff), wsel),
                  pl.BlockSpec((None, ff, d), wsel)],
        out_specs=pl.BlockSpec(memory_space=pl.ANY),
        scratch_shapes=[pltpu.VMEM((d, ff), BF16), pltpu.VMEM((d, ff), BF16), pltpu.VMEM((ff, d), BF16),
                        pltpu.VMEM((ring, bm, dp), U32), pltpu.VMEM((ring, bm, dp), U32),
                        pltpu.SemaphoreType.DMA((ring,)), pltpu.SemaphoreType.DMA((ring,))])
    return pl.pallas_call(
        _expert_kernel, out_shape=jax.ShapeDtypeStruct((n_slots, dp), U32), grid_spec=grid_spec,
        compiler_params=_params(1), name="experts",
    )(blk_start, counts, n_used, xs, wg, wu, wd)


def _sc_gather_rows(table, idx):
    n_idx = idx.shape[0]
    dp = table.shape[1]
    workers = SC_CORES * SC_SUBCORES
    per_worker = n_idx // workers
    chunk = SC_GATHER_ROWS

    @functools.partial(
        pl.kernel, mesh=_sc_mesh(), out_type=jax.ShapeDtypeStruct((n_idx, dp), table.dtype),
        scratch_types=[pltpu.VMEM((chunk,), jnp.int32), pltpu.VMEM((chunk, dp), table.dtype),
                       pltpu.SemaphoreType.DMA],
        name="sc_gather_rows")
    def gather(table_hbm, idx_hbm, out_hbm, idx_v, rows_v, sem):
        base = (lax.axis_index("s") * SC_CORES + lax.axis_index("c")) * per_worker

        @pl.loop(0, per_worker // chunk)
        def _(j):
            off = base + j * chunk
            pltpu.sync_copy(idx_hbm.at[pl.ds(off, chunk)], idx_v)
            pltpu.async_copy(table_hbm.at[idx_v], rows_v, sem).wait()
            pltpu.sync_copy(rows_v, out_hbm.at[pl.ds(off, chunk)])

    return gather(table, idx)


def _combine_kernel(tile, yt_ref, w_ref, h_ref, x1_ref, g2_ref, sg_ref, su_ref, sd_ref, lg_ref, lb_ref, *rest):
    o_ref = rest[-1]
    hb = _unpack_pairs(h_ref[...]).astype(BF16)
    g = jnp.dot(hb, sg_ref[...], preferred_element_type=F32)
    u = jnp.dot(hb, su_ref[...], preferred_element_type=F32)
    f = jnp.dot((_silu(g) * u).astype(BF16), sd_ref[...], preferred_element_type=F32)
    w = w_ref[...]
    for kk in range(TOP_K):
        f = f + w[:, kk:kk + 1] * _unpack_pairs(yt_ref[kk * tile:(kk + 1) * tile, :])
    o_ref[...] = _layer_norm(DEEPNORM_ALPHA * x1_ref[...] + g2_ref[0] * f, lg_ref[...], lb_ref[...])


def _combine(y_tok, top_w, h2p, x1, g2, sg, su, sd, lg, lb, tile, tiles_per_batch, first_tile, prev_out):
    n, d = x1.shape
    dp = h2p.shape[1]
    row = lambda i: (first_tile + i, 0)
    full = lambda i: (0, 0)

    def const(a):
        return pl.BlockSpec(a.shape, full, pipeline_mode=pl.Buffered(1))

    in_specs = [pl.BlockSpec((TOP_K * tile, dp), lambda i: (i, 0)),
                pl.BlockSpec((tile, TOP_K), row), pl.BlockSpec((tile, dp), row), pl.BlockSpec((tile, d), row),
                pl.BlockSpec((1, 1, d), lambda i: ((first_tile + i) // tiles_per_batch, 0, 0)),
                const(sg), const(su), const(sd), const(lg), const(lb)]
    args = [y_tok, top_w, h2p, x1, g2, sg, su, sd, lg, lb]
    aliases = {}
    if prev_out is not None:
        in_specs.append(pl.BlockSpec(memory_space=pl.ANY))
        args.append(prev_out)
        aliases = {len(args) - 1: 0}
    return pl.pallas_call(
        functools.partial(_combine_kernel, tile),
        out_shape=jax.ShapeDtypeStruct((n, d), F32),
        grid=(y_tok.shape[0] // (TOP_K * tile),),
        in_specs=in_specs,
        out_specs=pl.BlockSpec((tile, d), row),
        input_output_aliases=aliases,
        compiler_params=_params(1), name="combine_shared_ln2",
    )(*args)


def kernel(x, c, ctx, c_ctx, w_mod, b_mod, w_in, na_rpb, gla_w_decay_f, gla_b_decay_f, gla_w_decay_b, gla_b_decay_b,
           gla_norm_w, w_branch_a, w_branch_b, w_out, ln1_g, ln1_b, router_w, router_bias, exp_w_gate, exp_w_up,
           exp_w_down, sh_w_gate, sh_w_up, sh_w_down, ln2_g, ln2_b):
    batch, t_len, d = x.shape
    ctx_len = ctx.shape[1]
    n = batch * t_len
    assert w_mod.shape[0] == DEPTH == 1
    assert t_len % GLA_CHUNK == 0 and ctx_len == GLA_CHUNK and (t_len // GRID_W) % NA_ROW_BLOCK == 0

    mod_rows = 16
    c_all = jnp.zeros((mod_rows, d), F32).at[:batch].set(c).at[batch].set(c_ctx)
    mod = _modulation(c_all, w_mod[0], b_mod[0])
    sh1, sc1, g1, sh2, sc2, g2 = [mod[:batch, j * d:(j + 1) * d].reshape(batch, 1, d) for j in range(6)]
    sh1c = mod[batch:batch + 1, 0:d].reshape(1, 1, d)
    sc1c = mod[batch:batch + 1, d:2 * d].reshape(1, 1, d)

    offs = np.cumsum((0, NA_WIDTH, NA_WIDTH, NA_WIDTH, GLA_QK_WIDTH, GLA_QK_WIDTH, GLA_V_WIDTH, GLA_V_WIDTH,
                      GLA_GATE_RANK, GLA_GATE_RANK, d, d))
    qa, ka, va, qb, kb, vbc, gbc, lrf, lrb, ga, gbt = [w_in[0][:, offs[j]:offs[j + 1]] for j in range(11)]
    lr_cols = jnp.concatenate([lrf, lrb, jnp.zeros((d, LANES - 2 * GLA_GATE_RANK), F32)], axis=1)
    w_lat = jnp.concatenate([lr_cols, qa, ka, va, qb, kb, vbc, gbc, ga, gbt], axis=1).astype(BF16)
    lat_offs = np.cumsum((0, LANES, NA_WIDTH, NA_WIDTH, NA_WIDTH, GLA_QK_WIDTH, GLA_QK_WIDTH, GLA_V_WIDTH))
    w_ctx = jnp.concatenate([w_lat[:, lat_offs[j]:lat_offs[j + 1]] for j in (0, 2, 3, 5, 6)], axis=1)
    plain = ("plain",)
    lat_plan = ((LANES, (("decay",),)),
                (3 * NA_WIDTH, (("scale", NA_HEAD_DIM ** -0.5), plain, plain)),
                (2 * GLA_QK_WIDTH, (("rope", GLA_DK ** -0.5), ("rope", 1.0))),
                (GLA_V_WIDTH, (plain, plain)), (GLA_V_WIDTH, (plain, plain)),
                (2 * d, (plain,) * 4))
    ctx_plan = ((LANES, (("decay",),)),
                (2 * NA_WIDTH, (plain, plain)), (GLA_QK_WIDTH, (plain,)), (GLA_V_WIDTH, (plain, plain)))
    w2, b2 = _decay_weights(gla_w_decay_f[0], gla_b_decay_f[0], gla_w_decay_b[0], gla_b_decay_b[0])
    tile = 256
    x2d = x.reshape(n, d)
    decay, na_qkv, gla_qk, vb, gb, gates = _projection(
        x2d, sc1, sh1, w_lat, w2, b2, lat_plan, (BF16,) * 6, PROJ_TILE, t_len // PROJ_TILE,
        rope=_rope_tables(t_len))
    decay_c, na_kv_c, k_c, v_c = _projection(
        ctx.reshape(batch * ctx_len, d), sc1c, sh1c, w_ctx, w2, b2, ctx_plan, (BF16,) * 4, tile,
        batch * ctx_len // tile)

    out_a = _na_attention(na_qkv, na_kv_c, _na_bias_tables(na_rpb[0]), batch, t_len, ctx_len)
    o_f, o_b = _gla(gla_qk, vb, decay, k_c, v_c, decay_c, batch, t_len, ctx_len)

    x1, h2p, scores_t = _merge(
        out_a.reshape(n, NA_WIDTH), o_f.reshape(n, GLA_V_WIDTH), o_b.reshape(n, GLA_V_WIDTH), gb, gates, x2d,
        g1, sc2, sh2, w_branch_a[0].astype(BF16), w_branch_b[0].astype(BF16), w_out[0].astype(BF16),
        gla_norm_w[0].reshape(1, -1), ln1_g[0].reshape(1, d), ln1_b[0].reshape(1, d),
        router_w[0].T.astype(BF16), tile, t_len // tile)

    top_e, top_w, rank, counts = _route(scores_t, router_bias[0], 512)

    counts = counts[:, 0]
    n_blocks = n * TOP_K // EXPERT_BLOCK + N_EXPERTS
    blocks_per = (counts + EXPERT_BLOCK - 1) // EXPERT_BLOCK
    blk_end = jnp.cumsum(blocks_per)
    blk_start = blk_end - blocks_per
    dest = _slots(top_e, rank, (blk_start * EXPERT_BLOCK).astype(F32).reshape(N_EXPERTS, 1), 512)

    xs = _sc_scatter_rows(h2p, dest.reshape(TOP_K * n), n_blocks * EXPERT_BLOCK)
    y = _experts(blk_start.astype(jnp.int32), counts, blk_end[-1:].astype(jnp.int32), xs,
                 exp_w_gate[0], exp_w_up[0], exp_w_down[0])
    dest_tok = dest.reshape(TOP_K, n // tile, tile).transpose(1, 0, 2).reshape(COMBINE_CHUNKS, -1)
    w_rows = top_w.T
    shared = (sh_w_gate[0].astype(BF16), sh_w_up[0].astype(BF16), sh_w_down[0].astype(BF16),
              ln2_g[0].reshape(1, d), ln2_b[0].reshape(1, d))
    tiles_per_chunk = n // tile // COMBINE_CHUNKS
    out = None
    for ci in range(COMBINE_CHUNKS):
        y_tok = _sc_gather_rows(y, dest_tok[ci])
        out = _combine(y_tok, w_rows, h2p, x1, g2, *shared, tile, t_len // tile, ci * tiles_per_chunk, out)
    return out.reshape(batch, t_len, d)
```

```python
import functools

import numpy as np
import jax
import jax.numpy as jnp
from jax import lax
from jax.experimental import pallas as pl
from jax.experimental.pallas import tpu as pltpu
from jax.experimental.pallas import tpu_sc as plsc

F32 = jnp.float32
BF16 = jnp.bfloat16
U32 = jnp.uint32
HIGHEST = lax.Precision.HIGHEST

GRID_W = 64
NA_HEADS = 8
NA_HEAD_DIM = 64
NA_WIDTH = NA_HEADS * NA_HEAD_DIM
WIN_ROWS = 8
WIN_COLS = 16
GLA_HEADS = 4
GLA_DK = 128
GLA_DV = 256
GLA_QK_WIDTH = GLA_HEADS * GLA_DK
GLA_V_WIDTH = GLA_HEADS * GLA_DV
GLA_GATE_RANK = 16
GLA_TAU = 16.0
LOG2E = 1.4426950408889634
ROPE_BASE = 10000.0
N_EXPERTS = 256
TOP_K = 8
N_GROUPS = 8
TOPK_GROUPS = 4
GROUP_SIZE = N_EXPERTS // N_GROUPS
ROUTED_SCALE = 2.5
DEPTH = 1
DEEPNORM_ALPHA = (2 * DEPTH) ** 0.25
EPS = 1e-6

LANES = 128
PROJ_TILE = 512
NA_ROW_BLOCK = 4
NA_UNION_ROWS = NA_ROW_BLOCK + WIN_ROWS - 1
GLA_CHUNK = 256
GLA_DIAG = 16
GLA_HEADS_PER_STEP = 4
EXPERT_BLOCK = 256
EXPERT_GROUP = 4
EXPERT_RING = 8
COMBINE_CHUNKS = 8
SC_CORES = 2
SC_SUBCORES = 16
SC_GATHER_ROWS = 128
NEG_BIG = -1e30
VMEM_LIMIT = 56 * 1024 * 1024


def _params(n_axes, vmem=VMEM_LIMIT):
    return pltpu.CompilerParams(dimension_semantics=("arbitrary",) * n_axes, vmem_limit_bytes=vmem)


def _sigmoid(v):
    return 1.0 / (1.0 + jnp.exp(-v))


def _silu(v):
    return v * _sigmoid(v)


def _pack_pairs(v):
    m = v.shape[1] // 2
    lo = lax.bitcast_convert_type(v[:, :m].astype(BF16).astype(F32), U32) >> 16
    hi = lax.bitcast_convert_type(v[:, m:].astype(BF16).astype(F32), U32) & jnp.uint32(0xFFFF0000)
    return lo | hi


def _unpack_pairs(p):
    lo = lax.bitcast_convert_type(p << 16, F32)
    hi = lax.bitcast_convert_type(p & jnp.uint32(0xFFFF0000), F32)
    return jnp.concatenate([lo, hi], axis=1)


def _mod_kernel(c_ref, w_ref, b_ref, o_ref):
    o_ref[...] = jnp.dot(_silu(c_ref[...]), w_ref[...], preferred_element_type=F32, precision=HIGHEST) + b_ref[...]


def _modulation(c_all, w_mod, b_mod):
    rows, d = c_all.shape
    n = w_mod.shape[1]
    bn = 512
    return pl.pallas_call(
        _mod_kernel,
        out_shape=jax.ShapeDtypeStruct((rows, n), F32),
        grid=(n // bn,),
        in_specs=[pl.BlockSpec((rows, d), lambda j: (0, 0)),
                  pl.BlockSpec((d, bn), lambda j: (0, j)),
                  pl.BlockSpec((1, bn), lambda j: (0, j))],
        out_specs=pl.BlockSpec((rows, bn), lambda j: (0, j)),
        compiler_params=_params(1),
        name="modulation",
    )(c_all, w_mod, b_mod.reshape(1, n))


def _swap32(v):
    lane = lax.broadcasted_iota(jnp.int32, v.shape, 1)
    return jnp.where((lane % 64) < 32, pltpu.roll(v, 96, 1), pltpu.roll(v, 32, 1))


GLA_DECAY_WIDTH = 2 * GLA_HEADS * 2 * GLA_DK


def _log2_decay_split(lr, w2, b2, out_ref):
    z = (jnp.dot(lr.astype(BF16), w2, preferred_element_type=F32) + b2) * LOG2E
    a = (jnp.minimum(z, 0.0) - jnp.log2(1.0 + jnp.exp2(-jnp.abs(z)))) * (1.0 / GLA_TAU)
    hi = a.astype(BF16)
    lo = (a - hi.astype(F32)).astype(BF16)
    for p in range(a.shape[1] // GLA_DK):
        src = slice(p * GLA_DK, (p + 1) * GLA_DK)
        out_ref[:, 2 * p * GLA_DK:(2 * p + 1) * GLA_DK] = hi[:, src]
        out_ref[:, (2 * p + 1) * GLA_DK:(2 * p + 2) * GLA_DK] = lo[:, src]


def _proj_kernel(plan, has_rope, *refs):
    x_ref, sc_ref, sh_ref, w_ref, w2_ref, b2_ref = refs[:6]
    pos = 6
    if has_rope:
        cos_ref, sin_ref = refs[6:8]
        pos = 8
    out_refs = refs[pos:]
    h = (x_ref[...] * (1.0 + sc_ref[0]) + sh_ref[0]).astype(BF16)
    col = 0
    for out_ref, (width, kinds) in zip(out_refs, plan):
        for j, kind in enumerate(kinds):
            cw = width // len(kinds)
            c0 = j * cw
            acc = jnp.dot(h, w_ref[:, col + c0:col + c0 + cw], preferred_element_type=F32)
            if kind[0] == "decay":
                _log2_decay_split(acc, w2_ref[...], b2_ref[...], out_ref)
                continue
            if kind[0] == "scale":
                acc = acc * kind[1]
            elif kind[0] == "rope":
                cos, sin = cos_ref[...], sin_ref[...]
                pieces = []
                for p in range(cw // LANES):
                    v = acc[:, p * LANES:(p + 1) * LANES]
                    pieces.append((v * cos + _swap32(v) * sin) * kind[1])
                acc = jnp.concatenate(pieces, axis=1)
            out_ref[:, c0:c0 + cw] = acc.astype(out_ref.dtype)
        col += width


def _projection(x2d, sc, sh, w, w2, b2, plan, out_dtypes, tile, tiles_per_mod, rope=None):
    n, d = x2d.shape
    const = lambda a: pl.BlockSpec(a.shape, lambda i: (0, 0), pipeline_mode=pl.Buffered(1))
    in_specs = [pl.BlockSpec((tile, d), lambda i: (i, 0)),
                pl.BlockSpec((1, 1, d), lambda i: (i // tiles_per_mod, 0, 0)),
                pl.BlockSpec((1, 1, d), lambda i: (i // tiles_per_mod, 0, 0)),
                const(w), const(w2), const(b2)]
    args = [x2d, sc, sh, w, w2, b2]
    if rope is not None:
        in_specs += [pl.BlockSpec((tile, LANES), lambda i: (i % tiles_per_mod, 0))] * 2
        args += list(rope)
    widths = [GLA_DECAY_WIDTH if kinds[0][0] == "decay" else wd for wd, kinds in plan]
    out_shape = [jax.ShapeDtypeStruct((n, wd), dt) for wd, dt in zip(widths, out_dtypes)]
    out_specs = [pl.BlockSpec((tile, wd), lambda i: (i, 0)) for wd in widths]
    return pl.pallas_call(
        functools.partial(_proj_kernel, plan, rope is not None),
        out_shape=out_shape,
        grid=(n // tile,),
        in_specs=in_specs,
        out_specs=out_specs,
        compiler_params=_params(1),
        name="in_proj" if rope is not None else "ctx_proj",
    )(*args)


def _rope_tables(t_len):
    half = GLA_DK // 2
    quarter = half // 2
    f32 = np.float32
    inv_freq = f32(ROPE_BASE) ** (-np.arange(quarter, dtype=f32) / f32(quarter))
    pos = np.arange(t_len)
    row_ang = (pos // GRID_W).astype(f32)[:, None] * inv_freq[None, :]
    col_ang = (pos % GRID_W).astype(f32)[:, None] * inv_freq[None, :]
    cr, sr, cc, sn = np.cos(row_ang), np.sin(row_ang), np.cos(col_ang), np.sin(col_ang)
    cos = np.concatenate([cr, cr, cc, cc], axis=1).astype(f32)
    sin = np.concatenate([-sr, sr, -sn, sn], axis=1).astype(f32)
    return jnp.asarray(cos), jnp.asarray(sin)


def _na_bias_tables(rpb):
    rb, ur, w = NA_ROW_BLOCK, NA_UNION_ROWS, GRID_W
    heads = rpb.shape[0]
    pad = jnp.pad(rpb, ((0, 0), (0, 0), (w, w)))
    toep = jnp.stack([pad[:, :, w + WIN_COLS - 1 - c:2 * w + WIN_COLS - 1 - c] for c in range(w)], axis=2)
    c = np.arange(w)[:, None]
    kc = np.arange(w)[None, :]
    col_start = np.clip(c - WIN_COLS // 2, 0, w - WIN_COLS)
    col_ok = (kc >= col_start) & (kc < col_start + WIN_COLS)
    toep = jnp.where(col_ok[None, None], toep, NEG_BIG)
    neg = jnp.full((heads, w, w), NEG_BIG, F32)
    half = WIN_ROWS // 2
    tables = []
    for lo, off in ((lambda i: 0, WIN_ROWS - 1), (lambda i: i, WIN_ROWS - 1 - half), (lambda i: ur - WIN_ROWS, 0)):
        rows_ = []
        for i in range(rb):
            blocks = [toep[:, j - i + off] if lo(i) <= j < lo(i) + WIN_ROWS else neg for j in range(ur)]
            rows_.append(jnp.concatenate(blocks, axis=2))
        tables.append(jnp.concatenate(rows_, axis=1))
    return jnp.stack(tables)


def _na_kernel(rows, q_ref, k_ref, v_ref, kc_ref, vc_ref, bias_ref, o_ref):
    rb = pl.program_id(1)
    ustart = jnp.clip(rb * NA_ROW_BLOCK - WIN_ROWS // 2, 0, rows - NA_UNION_ROWS)
    k0 = pl.multiple_of(ustart * GRID_W, GRID_W)
    nk = NA_UNION_ROWS * GRID_W
    nt = (((1,), (1,)), ((), ()))
    lane = lax.broadcasted_iota(jnp.int32, (NA_ROW_BLOCK * GRID_W, LANES), 1)
    for p in range(NA_WIDTH // LANES):
        ls = slice(p * LANES, (p + 1) * LANES)
        q = q_ref[:, ls]
        ku = k_ref[pl.ds(k0, nk), ls]
        vu = v_ref[pl.ds(k0, nk), ls]
        kc = kc_ref[:, ls]
        vc = vc_ref[:, ls]
        out = jnp.zeros(q.shape, F32)
        for hh in range(LANES // NA_HEAD_DIM):
            mine = (lane // NA_HEAD_DIM) == hh
            qm = jnp.where(mine, q, jnp.zeros_like(q))
            s_loc = lax.dot_general(qm, ku, nt, preferred_element_type=F32) + bias_ref[p * 2 + hh]
            s_ctx = lax.dot_general(qm, kc, nt, preferred_element_type=F32)
            m = jnp.maximum(jnp.max(s_loc, axis=1, keepdims=True), jnp.max(s_ctx, axis=1, keepdims=True))
            p_loc = jnp.exp(s_loc - m)
            p_ctx = jnp.exp(s_ctx - m)
            denom = jnp.sum(p_loc, axis=1, keepdims=True) + jnp.sum(p_ctx, axis=1, keepdims=True)
            o = (jnp.dot(p_loc.astype(BF16), vu, preferred_element_type=F32)
                 + jnp.dot(p_ctx.astype(BF16), vc, preferred_element_type=F32)) / denom
            out = jnp.where(mine, o, out)
        o_ref[:, ls] = out.astype(o_ref.dtype)


def _na_attention(na_qkv, na_kv_ctx, bias_tables, batch, t_len, ctx_len):
    rows = t_len // GRID_W
    n_rb = rows // NA_ROW_BLOCK
    tq = NA_ROW_BLOCK * GRID_W
    qkv = na_qkv.reshape(batch, t_len, 3 * NA_WIDTH)
    kvc = na_kv_ctx.reshape(batch, ctx_len, 2 * NA_WIDTH)

    def bias_idx(b, r):
        return (jnp.where(r == 0, 0, jnp.where(r == n_rb - 1, 2, 1)), 0, 0, 0)

    return pl.pallas_call(
        functools.partial(_na_kernel, rows),
        out_shape=jax.ShapeDtypeStruct((batch, t_len, NA_WIDTH), BF16),
        grid=(batch, n_rb),
        in_specs=[pl.BlockSpec((None, tq, NA_WIDTH), lambda b, r: (b, r, 0)),
                  pl.BlockSpec((None, t_len, NA_WIDTH), lambda b, r: (b, 0, 1)),
                  pl.BlockSpec((None, t_len, NA_WIDTH), lambda b, r: (b, 0, 2)),
                  pl.BlockSpec((None, ctx_len, NA_WIDTH), lambda b, r: (b, 0, 0)),
                  pl.BlockSpec((None, ctx_len, NA_WIDTH), lambda b, r: (b, 0, 1)),
                  pl.BlockSpec((None,) + bias_tables.shape[1:], bias_idx)],
        out_specs=pl.BlockSpec((None, tq, NA_WIDTH), lambda b, r: (b, r, 0)),
        compiler_params=_params(2),
        name="na_attention",
    )(qkv, qkv, qkv, kvc, kvc, bias_tables)


def _gla_constants(c):
    tris, masks = [], []
    for reverse in (False, True):
        i = np.arange(c)[:, None]
        j = np.arange(c)[None, :]
        tris.append((j >= i) if reverse else (j <= i))
        i = np.arange(c // 2)[:, None]
        j = np.arange(c // 2)[None, :]
        if reverse:
            i, j = j, i
        level = []
        s = c // 4
        while s >= GLA_DIAG:
            level.append(((i // (2 * s)) == (j // (2 * s))) & ((i % (2 * s)) >= s) & ((j % (2 * s)) < s))
            s //= 2
        level.append(((i // GLA_DIAG) == (j // GLA_DIAG)) & (j <= i))
        masks.append(np.stack(level))
    return jnp.asarray(np.stack(tris), BF16), jnp.asarray(np.stack(masks), F32)


def _block_refs(cum, s, reverse, diag):
    c = cum.shape[0]
    span = s if diag else 2 * s
    parts = []
    for p in range(c // span):
        if diag:
            r = p * span + (span - 1 if reverse else 0)
        else:
            r = p * span + (s - 1 if reverse else s)
        parts.append(jnp.broadcast_to(cum[r:r + 1, :], (span, cum.shape[1])))
    return jnp.concatenate(parts, axis=0)


def _cumsum_rows(a_hl, tri):
    parts = jnp.dot(tri, a_hl, preferred_element_type=F32)
    w = a_hl.shape[1] // 2
    return parts[:, :w] + parts[:, w:]


def _gla_chunk(q, k, v, a, state_t, tri, mask_ref, reverse, want_out):
    c = k.shape[0]
    hc = c // 2
    nt = (((1,), (1,)), ((), ()))
    k = k.astype(F32)
    q = q.astype(F32) if want_out else None
    cum = _cumsum_rows(a, tri)
    last = 0 if reverse else c - 1
    total = cum[last:last + 1, :]

    def scale(x, log2_factor):
        return (x * jnp.exp2(log2_factor)).astype(BF16)

    out = None
    if want_out:
        out = lax.dot_general(scale(q, cum), state_t.astype(BF16), nt, preferred_element_type=F32)
        halves = (slice(hc, c), slice(0, hc)) if reverse else (slice(0, hc), slice(hc, c))
        early, late = halves
        r = hc - 1 if reverse else hc
        g = cum[r:r + 1, :]
        cross = lax.dot_general(scale(q[late], cum[late] - g), scale(k[early], g - cum[early]), nt,
                                preferred_element_type=F32)
        inner = []
        for rows in halves:
            cx, qx, kx = cum[rows], q[rows], k[rows]
            acc = jnp.zeros((hc, hc), F32)
            s = hc // 2
            level = 0
            while True:
                diag = s < GLA_DIAG
                d = cx - _block_refs(cx, GLA_DIAG if diag else s, reverse, diag)
                if diag:
                    qs, ks = scale(qx, d), scale(kx, -d)
                else:
                    e = jnp.exp2(-jnp.abs(d))
                    qs, ks = (qx * e).astype(BF16), (kx * e).astype(BF16)
                acc = acc + lax.dot_general(qs, ks, nt, preferred_element_type=F32) * mask_ref[level]
                if diag:
                    break
                s //= 2
                level += 1
            inner.append(acc)
        out_early = jnp.dot(inner[0].astype(BF16), v[early], preferred_element_type=F32)
        out_late = jnp.dot(jnp.concatenate([cross, inner[1]], axis=1).astype(BF16),
                           jnp.concatenate([v[early], v[late]], axis=0), preferred_element_type=F32)
        intra = [out_late, out_early] if reverse else [out_early, out_late]
        out = out + jnp.concatenate(intra, axis=0)
    upd = lax.dot_general(v, scale(k, total - cum), (((0,), (0,)), ((), ())), preferred_element_type=F32)
    return out, jnp.exp2(total) * state_t + upd


def _gla_kernel(qf_ref, kf_ref, vf_ref, af_ref, qb_ref, kb_ref, vb_ref, ab_ref,
                kc_ref, vc_ref, acf_ref, acb_ref, tri_ref, mask_ref,
                of_ref, ob_ref, sf_ref, sb_ref):
    c = pl.program_id(2)
    fm, bm = mask_ref.at[0], mask_ref.at[1]
    heads = sf_ref.shape[0]

    @pl.when(c == 0)
    def _():
        zero = jnp.zeros(sf_ref.shape[1:], F32)
        for h in range(heads):
            kc = kc_ref[:, h * GLA_DK:(h + 1) * GLA_DK]
            vc = vc_ref[:, h * GLA_DV:(h + 1) * GLA_DV]
            hl = slice(2 * h * GLA_DK, 2 * (h + 1) * GLA_DK)
            _, sf_ref[h] = _gla_chunk(None, kc, vc, acf_ref[:, hl], zero, tri_ref[0], fm, False, False)
            _, sb_ref[h] = _gla_chunk(None, kc, vc, acb_ref[:, hl], zero, tri_ref[1], bm, True, False)

    @pl.when(c > 0)
    def _():
        for h in range(heads):
            ks = slice(h * GLA_DK, (h + 1) * GLA_DK)
            vs = slice(h * GLA_DV, (h + 1) * GLA_DV)
            hl = slice(2 * h * GLA_DK, 2 * (h + 1) * GLA_DK)
            o, sf_ref[h] = _gla_chunk(qf_ref[:, ks], kf_ref[:, ks], vf_ref[:, vs], af_ref[:, hl], sf_ref[h],
                                      tri_ref[0], fm, False, True)
            of_ref[:, vs] = o.astype(of_ref.dtype)
            o, sb_ref[h] = _gla_chunk(qb_ref[:, ks], kb_ref[:, ks], vb_ref[:, vs], ab_ref[:, hl], sb_ref[h],
                                      tri_ref[1], bm, True, True)
            ob_ref[:, vs] = o.astype(ob_ref.dtype)


def _gla(gla_qk, vb, decay, k_ctx, v_ctx, decay_ctx, batch, t_len, ctx_len):
    nc = t_len // GLA_CHUNK
    h = GLA_HEADS
    qk = gla_qk.reshape(batch, t_len, 2 * GLA_QK_WIDTH)
    v3 = vb.reshape(batch, t_len, GLA_V_WIDTH)
    a3 = decay.reshape(batch, t_len, GLA_DECAY_WIDTH)
    kc3 = k_ctx.reshape(batch, ctx_len, GLA_QK_WIDTH)
    vc3 = v_ctx.reshape(batch, ctx_len, GLA_V_WIDTH)
    ac3 = decay_ctx.reshape(batch, ctx_len, GLA_DECAY_WIDTH)
    tri, masks = _gla_constants(GLA_CHUNK)

    def fwd(c):
        return jnp.maximum(c - 1, 0)

    def bwd(c):
        return nc - 1 - jnp.maximum(c - 1, 0)

    def const(a):
        return pl.BlockSpec(a.shape, lambda b, hh, c: (0,) * a.ndim, pipeline_mode=pl.Buffered(1))

    hp = GLA_HEADS_PER_STEP
    groups = h // hp
    cq = (None, GLA_CHUNK, hp * GLA_DK)
    cv = (None, GLA_CHUNK, hp * GLA_DV)
    ca = (None, GLA_CHUNK, hp * 2 * GLA_DK)
    cca = (None, ctx_len, hp * 2 * GLA_DK)
    in_specs = [
        pl.BlockSpec(cq, lambda b, g, c: (b, fwd(c), g)),
        pl.BlockSpec(cq, lambda b, g, c: (b, fwd(c), groups + g)),
        pl.BlockSpec(cv, lambda b, g, c: (b, fwd(c), g)),
        pl.BlockSpec(ca, lambda b, g, c: (b, fwd(c), g)),
        pl.BlockSpec(cq, lambda b, g, c: (b, bwd(c), g)),
        pl.BlockSpec(cq, lambda b, g, c: (b, bwd(c), groups + g)),
        pl.BlockSpec(cv, lambda b, g, c: (b, bwd(c), g)),
        pl.BlockSpec(ca, lambda b, g, c: (b, bwd(c), groups + g)),
        pl.BlockSpec((None, ctx_len, hp * GLA_DK), lambda b, g, c: (b, 0, g)),
        pl.BlockSpec((None, ctx_len, hp * GLA_DV), lambda b, g, c: (b, 0, g)),
        pl.BlockSpec(cca, lambda b, g, c: (b, 0, g)),
        pl.BlockSpec(cca, lambda b, g, c: (b, 0, groups + g)),
        const(tri), const(masks),
    ]
    out_specs = [pl.BlockSpec(cv, lambda b, g, c: (b, fwd(c), g)),
                 pl.BlockSpec(cv, lambda b, g, c: (b, bwd(c), g))]
    out_shape = [jax.ShapeDtypeStruct((batch, t_len, GLA_V_WIDTH), BF16)] * 2
    return pl.pallas_call(
        _gla_kernel,
        out_shape=out_shape,
        grid=(batch, groups, nc + 1),
        in_specs=in_specs,
        out_specs=out_specs,
        scratch_shapes=[pltpu.VMEM((hp, GLA_DV, GLA_DK), F32), pltpu.VMEM((hp, GLA_DV, GLA_DK), F32)],
        compiler_params=_params(3),
        name="gla",
    )(qk, qk, v3, a3, qk, qk, v3, a3, kc3, vc3, ac3, ac3, tri, masks)


def _decay_weights(w_f, b_f, w_b, b_b):
    r, width = w_f.shape
    w2 = jnp.zeros((LANES, 2 * width), F32).at[:r, :width].set(w_f).at[r:2 * r, width:].set(w_b)
    return w2.astype(BF16), jnp.concatenate([b_f, b_b]).reshape(1, 2 * width)


def _layer_norm(v, g, b):
    mu = jnp.mean(v, axis=1, keepdims=True)
    var = jnp.mean(jnp.square(v - mu), axis=1, keepdims=True)
    return (v - mu) * lax.rsqrt(var + EPS) * g + b


def _merge_kernel(oa_ref, of_ref, ob_ref, gb_ref, gates_ref, x_ref, g1_ref, sc2_ref, sh2_ref,
                  wa_ref, wb_ref, wo_ref, nw_ref, lg_ref, lb_ref, rw_ref,
                  x1_ref, h2_ref, sc_ref):
    d = x_ref.shape[1]
    o = of_ref[...].astype(F32) + ob_ref[...].astype(F32)
    pieces = []
    for hh in range(GLA_HEADS):
        oh = o[:, hh * GLA_DV:(hh + 1) * GLA_DV]
        pieces.append(oh * lax.rsqrt(jnp.mean(jnp.square(oh), axis=1, keepdims=True) + EPS))
    out_b = jnp.concatenate(pieces, axis=1) * nw_ref[...] * _silu(gb_ref[...].astype(F32))
    ya = jnp.dot(oa_ref[...], wa_ref[...], preferred_element_type=F32)
    yb = jnp.dot(out_b.astype(BF16), wb_ref[...], preferred_element_type=F32)
    y = _sigmoid(gates_ref[:, :d].astype(F32)) * ya + _sigmoid(gates_ref[:, d:].astype(F32)) * yb
    y2 = jnp.dot(y.astype(BF16), wo_ref[...], preferred_element_type=F32)
    x1 = _layer_norm(DEEPNORM_ALPHA * x_ref[...] + g1_ref[0] * y2, lg_ref[...], lb_ref[...])
    x1_ref[...] = x1
    h2 = x1 * (1.0 + sc2_ref[0]) + sh2_ref[0]
    h2_ref[...] = _pack_pairs(h2)
    logits_t = lax.dot_general(rw_ref[...], h2.astype(BF16), (((1,), (1,)), ((), ())), preferred_element_type=F32)
    sc_ref[...] = _sigmoid(logits_t)


def _merge(out_a, o_f, o_b, gb, gates, x2d, g1, sc2, sh2, wa, wb, wo, nw, lg, lb, rw_t, tile, tiles_per_batch):
    n, d = x2d.shape
    row = lambda i: (i, 0)
    mod = lambda i: (i // tiles_per_batch, 0, 0)
    full = lambda i: (0, 0)

    def const(a):
        return pl.BlockSpec(a.shape, full, pipeline_mode=pl.Buffered(1))

    in_specs = [pl.BlockSpec((tile, NA_WIDTH), row), pl.BlockSpec((tile, GLA_V_WIDTH), row),
                pl.BlockSpec((tile, GLA_V_WIDTH), row), pl.BlockSpec((tile, GLA_V_WIDTH), row),
                pl.BlockSpec((tile, 2 * d), row), pl.BlockSpec((tile, d), row),
                pl.BlockSpec((1, 1, d), mod), pl.BlockSpec((1, 1, d), mod), pl.BlockSpec((1, 1, d), mod),
                const(wa), const(wb), const(wo), const(nw), const(lg), const(lb), const(rw_t)]
    out_shape = [jax.ShapeDtypeStruct((n, d), F32), jax.ShapeDtypeStruct((n, d // 2), U32),
                 jax.ShapeDtypeStruct((N_EXPERTS, n), F32)]
    out_specs = [pl.BlockSpec((tile, d), row), pl.BlockSpec((tile, d // 2), row),
                 pl.BlockSpec((N_EXPERTS, tile), lambda i: (0, i))]
    return pl.pallas_call(
        _merge_kernel, out_shape=out_shape, grid=(n // tile,), in_specs=in_specs, out_specs=out_specs,
        compiler_params=_params(1), name="merge_ln1_router",
    )(out_a, o_f, o_b, gb, gates, x2d, g1, sc2, sh2, wa, wb, wo, nw, lg, lb, rw_t)


def _first_argmax(vals, idx, n):
    m = jnp.max(vals, axis=0, keepdims=True)
    first = jnp.min(jnp.where(vals == m, idx, float(n)), axis=0, keepdims=True)
    return m, first


def _route_kernel(sc_ref, bias_ref, e_ref, w_ref, rank_ref, cnt_ref, carry_ref):
    step = pl.program_id(0)
    tr = sc_ref.shape[1]

    @pl.when(step == 0)
    def _():
        carry_ref[...] = jnp.zeros(carry_ref.shape, F32)

    scores = sc_ref[...]
    biased = scores + bias_ref[...]
    eidx = lax.broadcasted_iota(jnp.int32, (N_EXPERTS, tr), 0).astype(F32)
    lidx = lax.broadcasted_iota(jnp.int32, (GROUP_SIZE, tr), 0).astype(F32)
    gidx = lax.broadcasted_iota(jnp.int32, (N_GROUPS, tr), 0).astype(F32)
    gs = []
    for g in range(N_GROUPS):
        blk = biased[g * GROUP_SIZE:(g + 1) * GROUP_SIZE]
        m1, first = _first_argmax(blk, lidx, GROUP_SIZE)
        m2 = jnp.max(jnp.where(lidx == first, -jnp.inf, blk), axis=0, keepdims=True)
        gs.append(m1 + m2)
    cur = jnp.concatenate(gs, axis=0)
    keep = jnp.zeros((N_GROUPS, tr), F32)
    for _ in range(TOPK_GROUPS):
        _, first = _first_argmax(cur, gidx, N_GROUPS)
        sel = gidx == first
        keep = jnp.where(sel, 1.0, keep)
        cur = jnp.where(sel, -jnp.inf, cur)
    keep_e = jnp.concatenate([jnp.broadcast_to(keep[g:g + 1], (GROUP_SIZE, tr)) for g in range(N_GROUPS)], axis=0)
    masked = jnp.where(keep_e > 0.5, biased, -jnp.inf)
    chosen = jnp.zeros((N_EXPERTS, tr), F32)
    tops, topi = [], []
    for _ in range(TOP_K):
        _, first = _first_argmax(masked, eidx, N_EXPERTS)
        sel = eidx == first
        tops.append(jnp.sum(jnp.where(sel, scores, 0.0), axis=0, keepdims=True))
        topi.append(first)
        chosen = jnp.where(sel, 1.0, chosen)
        masked = jnp.where(sel, -jnp.inf, masked)
    top_s = jnp.concatenate(tops, axis=0)
    top_i = jnp.concatenate(topi, axis=0)
    e_ref[...] = top_i.astype(jnp.int32)
    w_ref[...] = top_s / jnp.sum(top_s, axis=0, keepdims=True) * ROUTED_SCALE
    r = lax.broadcasted_iota(jnp.int32, (tr, tr), 0)
    cidx = lax.broadcasted_iota(jnp.int32, (tr, tr), 1)
    before = jnp.where(r < cidx, 1.0, 0.0).astype(BF16)
    prior = jnp.dot(chosen.astype(BF16), before, preferred_element_type=F32) + carry_ref[...]
    ranks = [jnp.sum(jnp.where(eidx == topi[kk], prior, 0.0), axis=0, keepdims=True) for kk in range(TOP_K)]
    rank_ref[...] = jnp.concatenate(ranks, axis=0).astype(jnp.int32)
    carry_ref[...] = carry_ref[...] + jnp.sum(chosen, axis=1, keepdims=True)
    cnt_ref[...] = jnp.broadcast_to(carry_ref[...], cnt_ref.shape).astype(jnp.int32)


def _route(scores_t, router_bias, tile):
    n = scores_t.shape[1]
    col = lambda i: (0, i)
    out_shape = [jax.ShapeDtypeStruct((TOP_K, n), jnp.int32), jax.ShapeDtypeStruct((TOP_K, n), F32),
                 jax.ShapeDtypeStruct((TOP_K, n), jnp.int32), jax.ShapeDtypeStruct((N_EXPERTS, LANES), jnp.int32)]
    return pl.pallas_call(
        _route_kernel, out_shape=out_shape, grid=(n // tile,),
        in_specs=[pl.BlockSpec((N_EXPERTS, tile), col), pl.BlockSpec((N_EXPERTS, 1), lambda i: (0, 0))],
        out_specs=[pl.BlockSpec((TOP_K, tile), col), pl.BlockSpec((TOP_K, tile), col),
                   pl.BlockSpec((TOP_K, tile), col), pl.BlockSpec((N_EXPERTS, LANES), lambda i: (0, 0))],
        scratch_shapes=[pltpu.VMEM((N_EXPERTS, 1), F32)],
        compiler_params=_params(1), name="route",
    )(scores_t, router_bias.reshape(N_EXPERTS, 1))


def _slots_kernel(e_ref, rank_ref, start_ref, dest_ref):
    tr = e_ref.shape[1]
    eidx = lax.broadcasted_iota(jnp.int32, (N_EXPERTS, tr), 0)
    e = e_ref[...]
    start = start_ref[...]
    rows = [jnp.sum(jnp.where(eidx == e[kk:kk + 1], start, 0.0), axis=0, keepdims=True) for kk in range(TOP_K)]
    dest_ref[...] = jnp.concatenate(rows, axis=0).astype(jnp.int32) + rank_ref[...]


def _slots(top_e, rank, start_rows, tile):
    n = top_e.shape[1]
    col = lambda i: (0, i)
    return pl.pallas_call(
        _slots_kernel, out_shape=jax.ShapeDtypeStruct((TOP_K, n), jnp.int32), grid=(n // tile,),
        in_specs=[pl.BlockSpec((TOP_K, tile), col), pl.BlockSpec((TOP_K, tile), col),
                  pl.BlockSpec((N_EXPERTS, 1), lambda i: (0, 0))],
        out_specs=pl.BlockSpec((TOP_K, tile), col),
        compiler_params=_params(1), name="slots",
    )(top_e, rank, start_rows)


def _sc_mesh():
    return plsc.VectorSubcoreMesh(core_axis_name="c", subcore_axis_name="s",
                                  num_cores=SC_CORES, num_subcores=SC_SUBCORES)


def _sc_scatter_rows(rows, idx, n_out):
    n, dp = rows.shape
    copies = idx.shape[0] // n
    per_worker = n // (SC_CORES * SC_SUBCORES)
    chunk = SC_GATHER_ROWS

    @functools.partial(
        pl.kernel, mesh=_sc_mesh(), out_type=jax.ShapeDtypeStruct((n_out, dp), rows.dtype),
        scratch_types=[pltpu.VMEM((chunk,), jnp.int32), pltpu.VMEM((chunk, dp), rows.dtype)],
        name="sc_scatter_rows")
    def scatter(rows_hbm, idx_hbm, out_hbm, idx_v, rows_v):
        base = (lax.axis_index("s") * SC_CORES + lax.axis_index("c")) * per_worker

        @pl.loop(0, per_worker // chunk)
        def _(j):
            off = base + j * chunk
            pltpu.sync_copy(rows_hbm.at[pl.ds(off, chunk)], rows_v)
            for k in range(copies):
                pltpu.sync_copy(idx_hbm.at[pl.ds(k * n + off, chunk)], idx_v)
                pltpu.sync_copy(rows_v, out_hbm.at[idx_v])

    return scatter(rows, idx)


def _expert_kernel(start_ref, cnt_ref, nused_ref, xs_ref, wg_ref, wu_ref, wd_ref, y_ref,
                   wgb, wub, wdb, xbuf, ybuf, sem_in, sem_out):
    e = pl.program_id(0)
    bm, ring = EXPERT_BLOCK, EXPERT_RING
    cnt = cnt_ref[e]
    nb = (cnt + bm - 1) // bm
    g0 = start_ref[e]
    n_used = nused_ref[0]

    def x_copy(g):
        slot = g % ring
        return pltpu.make_async_copy(xs_ref.at[pl.ds(g * bm, bm)], xbuf.at[slot], sem_in.at[slot])

    def y_copy(g):
        slot = g % ring
        return pltpu.make_async_copy(ybuf.at[slot], y_ref.at[pl.ds(g * bm, bm)], sem_out.at[slot])

    @pl.when(e == 0)
    def _():
        for g in range(ring):
            @pl.when(g < n_used)
            def _():
                x_copy(g).start()

    wgb[...] = wg_ref[...].astype(BF16)
    wub[...] = wu_ref[...].astype(BF16)
    wdb[...] = wd_ref[...].astype(BF16)
    row = lax.broadcasted_iota(jnp.int32, (bm, xbuf.shape[2]), 0)

    def blocks(j, count):
        for b in range(count):
            g = g0 + j + b
            x_copy(g).wait()

            @pl.when(g >= ring)
            def _():
                y_copy(g - ring).wait()

        for b in range(count):
            slot = (g0 + j + b) % ring
            x = _unpack_pairs(jnp.where(row < cnt - (j + b) * bm, xbuf[slot], jnp.uint32(0))).astype(BF16)
            gate = jnp.dot(x, wgb[...], preferred_element_type=F32)
            up = jnp.dot(x, wub[...], preferred_element_type=F32)
            ybuf[slot] = _pack_pairs(
                jnp.dot((_silu(gate) * up).astype(BF16), wdb[...], preferred_element_type=F32))

        for b in range(count):
            g = g0 + j + b
            y_copy(g).start()

            @pl.when(g + ring < n_used)
            def _():
                x_copy(g + ring).start()

    group = EXPERT_GROUP

    def full_group(p, carry):
        blocks(group * p, group)
        return carry

    lax.fori_loop(0, nb // group, full_group, 0)
    done = (nb // group) * group
    size = group // 2
    while size >= 1:
        @pl.when((nb - done) % (2 * size) >= size)
        def _(done=done, size=size):
            blocks(done, size)

        done = done + jnp.where((nb - done) % (2 * size) >= size, size, 0)
        size //= 2

    @pl.when(e == pl.num_programs(0) - 1)
    def _():
        for back in range(1, ring + 1):
            @pl.when(n_used - back >= 0)
            def _():
                y_copy(n_used - back).wait()


def _experts(blk_start, counts, n_used, xs, wg, wu, wd):
    n_slots, dp = xs.shape
    n_exp, d, ff = wg.shape
    bm, ring = EXPERT_BLOCK, EXPERT_RING
    wsel = lambda e, st, ct, nu: (e, 0, 0)
    grid_spec = pltpu.PrefetchScalarGridSpec(
        num_scalar_prefetch=3, grid=(n_exp,),
        in_specs=[pl.BlockSpec(memory_space=pl.ANY),
                  pl.BlockSpec((None, d, ff), wsel), pl.BlockSpec((None, d, ff), wsel),
                  pl.BlockSpec((None, ff, d), wsel)],
        out_specs=pl.BlockSpec(memory_space=pl.ANY),
        scratch_shapes=[pltpu.VMEM((d, ff), BF16), pltpu.VMEM((d, ff), BF16), pltpu.VMEM((ff, d), BF16),
                        pltpu.VMEM((ring, bm, dp), U32), pltpu.VMEM((ring, bm, dp), U32),
                        pltpu.SemaphoreType.DMA((ring,)), pltpu.SemaphoreType.DMA((ring,))])
    return pl.pallas_call(
        _expert_kernel, out_shape=jax.ShapeDtypeStruct((n_slots, dp), U32), grid_spec=grid_spec,
        compiler_params=_params(1), name="experts",
    )(blk_start, counts, n_used, xs, wg, wu, wd)


def _sc_gather_rows(table, idx):
    n_idx = idx.shape[0]
    dp = table.shape[1]
    workers = SC_CORES * SC_SUBCORES
    per_worker = n_idx // workers
    chunk = SC_GATHER_ROWS

    @functools.partial(
        pl.kernel, mesh=_sc_mesh(), out_type=jax.ShapeDtypeStruct((n_idx, dp), table.dtype),
        scratch_types=[pltpu.VMEM((chunk,), jnp.int32), pltpu.VMEM((chunk, dp), table.dtype),
                       pltpu.SemaphoreType.DMA],
        name="sc_gather_rows")
    def gather(table_hbm, idx_hbm, out_hbm, idx_v, rows_v, sem):
        base = (lax.axis_index("s") * SC_CORES + lax.axis_index("c")) * per_worker

        @pl.loop(0, per_worker // chunk)
        def _(j):
            off = base + j * chunk
            pltpu.sync_copy(idx_hbm.at[pl.ds(off, chunk)], idx_v)
            pltpu.async_copy(table_hbm.at[idx_v], rows_v, sem).wait()
            pltpu.sync_copy(rows_v, out_hbm.at[pl.ds(off, chunk)])

    return gather(table, idx)


def _combine_kernel(tile, yt_ref, w_ref, h_ref, x1_ref, g2_ref, sg_ref, su_ref, sd_ref, lg_ref, lb_ref, *rest):
    o_ref = rest[-1]
    hb = _unpack_pairs(h_ref[...]).astype(BF16)
    g = jnp.dot(hb, sg_ref[...], preferred_element_type=F32)
    u = jnp.dot(hb, su_ref[...], preferred_element_type=F32)
    f = jnp.dot((_silu(g) * u).astype(BF16), sd_ref[...], preferred_element_type=F32)
    w = w_ref[...]
    for kk in range(TOP_K):
        f = f + w[:, kk:kk + 1] * _unpack_pairs(yt_ref[kk * tile:(kk + 1) * tile, :])
    o_ref[...] = _layer_norm(DEEPNORM_ALPHA * x1_ref[...] + g2_ref[0] * f, lg_ref[...], lb_ref[...])


def _combine(y_tok, top_w, h2p, x1, g2, sg, su, sd, lg, lb, tile, tiles_per_batch, first_tile, prev_out):
    n, d = x1.shape
    dp = h2p.shape[1]
    row = lambda i: (first_tile + i, 0)
    full = lambda i: (0, 0)

    def const(a):
        return pl.BlockSpec(a.shape, full, pipeline_mode=pl.Buffered(1))

    in_specs = [pl.BlockSpec((TOP_K * tile, dp), lambda i: (i, 0)),
                pl.BlockSpec((tile, TOP_K), row), pl.BlockSpec((tile, dp), row), pl.BlockSpec((tile, d), row),
                pl.BlockSpec((1, 1, d), lambda i: ((first_tile + i) // tiles_per_batch, 0, 0)),
                const(sg), const(su), const(sd), const(lg), const(lb)]
    args = [y_tok, top_w, h2p, x1, g2, sg, su, sd, lg, lb]
    aliases = {}
    if prev_out is not None:
        in_specs.append(pl.BlockSpec(memory_space=pl.ANY))
        args.append(prev_out)
        aliases = {len(args) - 1: 0}
    return pl.pallas_call(
        functools.partial(_combine_kernel, tile),
        out_shape=jax.ShapeDtypeStruct((n, d), F32),
        grid=(y_tok.shape[0] // (TOP_K * tile),),
        in_specs=in_specs,
        out_specs=pl.BlockSpec((tile, d), row),
        input_output_aliases=aliases,
        compiler_params=_params(1), name="combine_shared_ln2",
    )(*args)


def kernel(x, c, ctx, c_ctx, w_mod, b_mod, w_in, na_rpb, gla_w_decay_f, gla_b_decay_f, gla_w_decay_b, gla_b_decay_b,
           gla_norm_w, w_branch_a, w_branch_b, w_out, ln1_g, ln1_b, router_w, router_bias, exp_w_gate, exp_w_up,
           exp_w_down, sh_w_gate, sh_w_up, sh_w_down, ln2_g, ln2_b):
    batch, t_len, d = x.shape
    ctx_len = ctx.shape[1]
    n = batch * t_len
    assert w_mod.shape[0] == DEPTH == 1
    assert t_len % GLA_CHUNK == 0 and ctx_len == GLA_CHUNK and (t_len // GRID_W) % NA_ROW_BLOCK == 0

    mod_rows = 16
    c_all = jnp.zeros((mod_rows, d), F32).at[:batch].set(c).at[batch].set(c_ctx)
    mod = _modulation(c_all, w_mod[0], b_mod[0])
    sh1, sc1, g1, sh2, sc2, g2 = [mod[:batch, j * d:(j + 1) * d].reshape(batch, 1, d) for j in range(6)]
    sh1c = mod[batch:batch + 1, 0:d].reshape(1, 1, d)
    sc1c = mod[batch:batch + 1, d:2 * d].reshape(1, 1, d)

    offs = np.cumsum((0, NA_WIDTH, NA_WIDTH, NA_WIDTH, GLA_QK_WIDTH, GLA_QK_WIDTH, GLA_V_WIDTH, GLA_V_WIDTH,
                      GLA_GATE_RANK, GLA_GATE_RANK, d, d))
    qa, ka, va, qb, kb, vbc, gbc, lrf, lrb, ga, gbt = [w_in[0][:, offs[j]:offs[j + 1]] for j in range(11)]
    lr_cols = jnp.concatenate([lrf, lrb, jnp.zeros((d, LANES - 2 * GLA_GATE_RANK), F32)], axis=1)
    w_lat = jnp.concatenate([lr_cols, qa, ka, va, qb, kb, vbc, gbc, ga, gbt], axis=1).astype(BF16)
    lat_offs = np.cumsum((0, LANES, NA_WIDTH, NA_WIDTH, NA_WIDTH, GLA_QK_WIDTH, GLA_QK_WIDTH, GLA_V_WIDTH))
    w_ctx = jnp.concatenate([w_lat[:, lat_offs[j]:lat_offs[j + 1]] for j in (0, 2, 3, 5, 6)], axis=1)
    plain = ("plain",)
    lat_plan = ((LANES, (("decay",),)),
                (3 * NA_WIDTH, (("scale", NA_HEAD_DIM ** -0.5), plain, plain)),
                (2 * GLA_QK_WIDTH, (("rope", GLA_DK ** -0.5), ("rope", 1.0))),
                (GLA_V_WIDTH, (plain, plain)), (GLA_V_WIDTH, (plain, plain)),
                (2 * d, (plain,) * 4))
    ctx_plan = ((LANES, (("decay",),)),
                (2 * NA_WIDTH, (plain, plain)), (GLA_QK_WIDTH, (plain,)), (GLA_V_WIDTH, (plain, plain)))
    w2, b2 = _decay_weights(gla_w_decay_f[0], gla_b_decay_f[0], gla_w_decay_b[0], gla_b_decay_b[0])
    tile = 256
    x2d = x.reshape(n, d)
    decay, na_qkv, gla_qk, vb, gb, gates = _projection(
        x2d, sc1, sh1, w_lat, w2, b2, lat_plan, (BF16,) * 6, PROJ_TILE, t_len // PROJ_TILE,
        rope=_rope_tables(t_len))
    decay_c, na_kv_c, k_c, v_c = _projection(
        ctx.reshape(batch * ctx_len, d), sc1c, sh1c, w_ctx, w2, b2, ctx_plan, (BF16,) * 4, tile,
        batch * ctx_len // tile)

    out_a = _na_attention(na_qkv, na_kv_c, _na_bias_tables(na_rpb[0]), batch, t_len, ctx_len)
    o_f, o_b = _gla(gla_qk, vb, decay, k_c, v_c, decay_c, batch, t_len, ctx_len)

    x1, h2p, scores_t = _merge(
        out_a.reshape(n, NA_WIDTH), o_f.reshape(n, GLA_V_WIDTH), o_b.reshape(n, GLA_V_WIDTH), gb, gates, x2d,
        g1, sc2, sh2, w_branch_a[0].astype(BF16), w_branch_b[0].astype(BF16), w_out[0].astype(BF16),
        gla_norm_w[0].reshape(1, -1), ln1_g[0].reshape(1, d), ln1_b[0].reshape(1, d),
        router_w[0].T.astype(BF16), tile, t_len // tile)

    top_e, top_w, rank, counts = _route(scores_t, router_bias[0], 512)

    counts = counts[:, 0]
    n_blocks = n * TOP_K // EXPERT_BLOCK + N_EXPERTS
    blocks_per = (counts + EXPERT_BLOCK - 1) // EXPERT_BLOCK
    blk_end = jnp.cumsum(blocks_per)
    blk_start = blk_end - blocks_per
    dest = _slots(top_e, rank, (blk_start * EXPERT_BLOCK).astype(F32).reshape(N_EXPERTS, 1), 512)

    xs = _sc_scatter_rows(h2p, dest.reshape(TOP_K * n), n_blocks * EXPERT_BLOCK)
    y = _experts(blk_start.astype(jnp.int32), counts, blk_end[-1:].astype(jnp.int32), xs,
                 exp_w_gate[0], exp_w_up[0], exp_w_down[0])
    dest_tok = dest.reshape(TOP_K, n // tile, tile).transpose(1, 0, 2).reshape(COMBINE_CHUNKS, -1)
    w_rows = top_w.T
    shared = (sh_w_gate[0].astype(BF16), sh_w_up[0].astype(BF16), sh_w_down[0].astype(BF16),
              ln2_g[0].reshape(1, d), ln2_b[0].reshape(1, d))
    tiles_per_chunk = n // tile // COMBINE_CHUNKS
    out = None
    for ci in range(COMBINE_CHUNKS):
        y_tok = _sc_gather_rows(y, dest_tok[ci])
        out = _combine(y_tok, w_rows, h2p, x1, g2, *shared, tile, t_len // tile, ci * tiles_per_chunk, out)
    return out.reshape(batch, t_len, d)
```

```python
import functools

import numpy as np
import jax
import jax.numpy as jnp
from jax import lax
from jax.experimental import pallas as pl
from jax.experimental.pallas import tpu as pltpu
from jax.experimental.pallas import tpu_sc as plsc

F32 = jnp.float32
BF16 = jnp.bfloat16
U32 = jnp.uint32
HIGHEST = lax.Precision.HIGHEST

GRID_W = 64
NA_HEADS = 8
NA_HEAD_DIM = 64
NA_WIDTH = NA_HEADS * NA_HEAD_DIM
WIN_ROWS = 8
WIN_COLS = 16
GLA_HEADS = 4
GLA_DK = 128
GLA_DV = 256
GLA_QK_WIDTH = GLA_HEADS * GLA_DK
GLA_V_WIDTH = GLA_HEADS * GLA_DV
GLA_GATE_RANK = 16
GLA_TAU = 16.0
LOG2E = 1.4426950408889634
ROPE_BASE = 10000.0
N_EXPERTS = 256
TOP_K = 8
N_GROUPS = 8
TOPK_GROUPS = 4
GROUP_SIZE = N_EXPERTS // N_GROUPS
ROUTED_SCALE = 2.5
DEPTH = 1
DEEPNORM_ALPHA = (2 * DEPTH) ** 0.25
EPS = 1e-6

LANES = 128
PROJ_TILE = 512
NA_ROW_BLOCK = 4
NA_UNION_ROWS = NA_ROW_BLOCK + WIN_ROWS - 1
GLA_CHUNK = 256
GLA_DIAG = 16
GLA_HEADS_PER_STEP = 4
EXPERT_BLOCK = 256
EXPERT_GROUP = 4
EXPERT_RING = 8
COMBINE_CHUNKS = 8
SC_CORES = 2
SC_SUBCORES = 16
SC_GATHER_ROWS = 128
NEG_BIG = -1e30
VMEM_LIMIT = 56 * 1024 * 1024


def _params(n_axes, vmem=VMEM_LIMIT):
    return pltpu.CompilerParams(dimension_semantics=("arbitrary",) * n_axes, vmem_limit_bytes=vmem)


def _sigmoid(v):
    return 1.0 / (1.0 + jnp.exp(-v))


def _silu(v):
    return v * _sigmoid(v)


def _pack_pairs(v):
    m = v.shape[1] // 2
    lo = lax.bitcast_convert_type(v[:, :m].astype(BF16).astype(F32), U32) >> 16
    hi = lax.bitcast_convert_type(v[:, m:].astype(BF16).astype(F32), U32) & jnp.uint32(0xFFFF0000)
    return lo | hi


def _unpack_pairs(p):
    lo = lax.bitcast_convert_type(p << 16, F32)
    hi = lax.bitcast_convert_type(p & jnp.uint32(0xFFFF0000), F32)
    return jnp.concatenate([lo, hi], axis=1)


def _mod_kernel(c_ref, w_ref, b_ref, o_ref):
    o_ref[...] = jnp.dot(_silu(c_ref[...]), w_ref[...], preferred_element_type=F32, precision=HIGHEST) + b_ref[...]


def _modulation(c_all, w_mod, b_mod):
    rows, d = c_all.shape
    n = w_mod.shape[1]
    bn = 512
    return pl.pallas_call(
        _mod_kernel,
        out_shape=jax.ShapeDtypeStruct((rows, n), F32),
        grid=(n // bn,),
        in_specs=[pl.BlockSpec((rows, d), lambda j: (0, 0)),
                  pl.BlockSpec((d, bn), lambda j: (0, j)),
                  pl.BlockSpec((1, bn), lambda j: (0, j))],
        out_specs=pl.BlockSpec((rows, bn), lambda j: (0, j)),
        compiler_params=_params(1),
        name="modulation",
    )(c_all, w_mod, b_mod.reshape(1, n))


def _swap32(v):
    lane = lax.broadcasted_iota(jnp.int32, v.shape, 1)
    return jnp.where((lane % 64) < 32, pltpu.roll(v, 96, 1), pltpu.roll(v, 32, 1))


GLA_DECAY_WIDTH = 2 * GLA_HEADS * 2 * GLA_DK


def _log2_decay_split(lr, w2, b2, out_ref):
    z = (jnp.dot(lr.astype(BF16), w2, preferred_element_type=F32) + b2) * LOG2E
    a = (jnp.minimum(z, 0.0) - jnp.log2(1.0 + jnp.exp2(-jnp.abs(z)))) * (1.0 / GLA_TAU)
    hi = a.astype(BF16)
    lo = (a - hi.astype(F32)).astype(BF16)
    for p in range(a.shape[1] // GLA_DK):
        src = slice(p * GLA_DK, (p + 1) * GLA_DK)
        out_ref[:, 2 * p * GLA_DK:(2 * p + 1) * GLA_DK] = hi[:, src]
        out_ref[:, (2 * p + 1) * GLA_DK:(2 * p + 2) * GLA_DK] = lo[:, src]


def _proj_kernel(plan, has_rope, *refs):
    x_ref, sc_ref, sh_ref, w_ref, w2_ref, b2_ref = refs[:6]
    pos = 6
    if has_rope:
        cos_ref, sin_ref = refs[6:8]
        pos = 8
    out_refs = refs[pos:]
    h = (x_ref[...] * (1.0 + sc_ref[0]) + sh_ref[0]).astype(BF16)
    col = 0
    for out_ref, (width, kinds) in zip(out_refs, plan):
        for j, kind in enumerate(kinds):
            cw = width // len(kinds)
            c0 = j * cw
            acc = jnp.dot(h, w_ref[:, col + c0:col + c0 + cw], preferred_element_type=F32)
            if kind[0] == "decay":
                _log2_decay_split(acc, w2_ref[...], b2_ref[...], out_ref)
                continue
            if kind[0] == "scale":
                acc = acc * kind[1]
            elif kind[0] == "rope":
                cos, sin = cos_ref[...], sin_ref[...]
                pieces = []
                for p in range(cw // LANES):
                    v = acc[:, p * LANES:(p + 1) * LANES]
                    pieces.append((v * cos + _swap32(v) * sin) * kind[1])
                acc = jnp.concatenate(pieces, axis=1)
            out_ref[:, c0:c0 + cw] = acc.astype(out_ref.dtype)
        col += width


def _projection(x2d, sc, sh, w, w2, b2, plan, out_dtypes, tile, tiles_per_mod, rope=None):
    n, d = x2d.shape
    const = lambda a: pl.BlockSpec(a.shape, lambda i: (0, 0), pipeline_mode=pl.Buffered(1))
    in_specs = [pl.BlockSpec((tile, d), lambda i: (i, 0)),
                pl.BlockSpec((1, 1, d), lambda i: (i // tiles_per_mod, 0, 0)),
                pl.BlockSpec((1, 1, d), lambda i: (i // tiles_per_mod, 0, 0)),
                const(w), const(w2), const(b2)]
    args = [x2d, sc, sh, w, w2, b2]
    if rope is not None:
        in_specs += [pl.BlockSpec((tile, LANES), lambda i: (i % tiles_per_mod, 0))] * 2
        args += list(rope)
    widths = [GLA_DECAY_WIDTH if kinds[0][0] == "decay" else wd for wd, kinds in plan]
    out_shape = [jax.ShapeDtypeStruct((n, wd), dt) for wd, dt in zip(widths, out_dtypes)]
    out_specs = [pl.BlockSpec((tile, wd), lambda i: (i, 0)) for wd in widths]
    return pl.pallas_call(
        functools.partial(_proj_kernel, plan, rope is not None),
        out_shape=out_shape,
        grid=(n // tile,),
        in_specs=in_specs,
        out_specs=out_specs,
        compiler_params=_params(1),
        name="in_proj" if rope is not None else "ctx_proj",
    )(*args)


def _rope_tables(t_len):
    half = GLA_DK // 2
    quarter = half // 2
    f32 = np.float32
    inv_freq = f32(ROPE_BASE) ** (-np.arange(quarter, dtype=f32) / f32(quarter))
    pos = np.arange(t_len)
    row_ang = (pos // GRID_W).astype(f32)[:, None] * inv_freq[None, :]
    col_ang = (pos % GRID_W).astype(f32)[:, None] * inv_freq[None, :]
    cr, sr, cc, sn = np.cos(row_ang), np.sin(row_ang), np.cos(col_ang), np.sin(col_ang)
    cos = np.concatenate([cr, cr, cc, cc], axis=1).astype(f32)
    sin = np.concatenate([-sr, sr, -sn, sn], axis=1).astype(f32)
    return jnp.asarray(cos), jnp.asarray(sin)


def _na_bias_tables(rpb):
    rb, ur, w = NA_ROW_BLOCK, NA_UNION_ROWS, GRID_W
    heads = rpb.shape[0]
    pad = jnp.pad(rpb, ((0, 0), (0, 0), (w, w)))
    toep = jnp.stack([pad[:, :, w + WIN_COLS - 1 - c:2 * w + WIN_COLS - 1 - c] for c in range(w)], axis=2)
    c = np.arange(w)[:, None]
    kc = np.arange(w)[None, :]
    col_start = np.clip(c - WIN_COLS // 2, 0, w - WIN_COLS)
    col_ok = (kc >= col_start) & (kc < col_start + WIN_COLS)
    toep = jnp.where(col_ok[None, None], toep, NEG_BIG)
    neg = jnp.full((heads, w, w), NEG_BIG, F32)
    half = WIN_ROWS // 2
    tables = []
    for lo, off in ((lambda i: 0, WIN_ROWS - 1), (lambda i: i, WIN_ROWS - 1 - half), (lambda i: ur - WIN_ROWS, 0)):
        rows_ = []
        for i in range(rb):
            blocks = [toep[:, j - i + off] if lo(i) <= j < lo(i) + WIN_ROWS else neg for j in range(ur)]
            rows_.append(jnp.concatenate(blocks, axis=2))
        tables.append(jnp.concatenate(rows_, axis=1))
    return jnp.stack(tables)


def _na_kernel(rows, q_ref, k_ref, v_ref, kc_ref, vc_ref, bias_ref, o_ref):
    rb = pl.program_id(1)
    ustart = jnp.clip(rb * NA_ROW_BLOCK - WIN_ROWS // 2, 0, rows - NA_UNION_ROWS)
    k0 = pl.multiple_of(ustart * GRID_W, GRID_W)
    nk = NA_UNION_ROWS * GRID_W
    nt = (((1,), (1,)), ((), ()))
    lane = lax.broadcasted_iota(jnp.int32, (NA_ROW_BLOCK * GRID_W, LANES), 1)
    for p in range(NA_WIDTH // LANES):
        ls = slice(p * LANES, (p + 1) * LANES)
        q = q_ref[:, ls]
        ku = k_ref[pl.ds(k0, nk), ls]
        vu = v_ref[pl.ds(k0, nk), ls]
        kc = kc_ref[:, ls]
        vc = vc_ref[:, ls]
        out = jnp.zeros(q.shape, F32)
        for hh in range(LANES // NA_HEAD_DIM):
            mine = (lane // NA_HEAD_DIM) == hh
            qm = jnp.where(mine, q, jnp.zeros_like(q))
            s_loc = lax.dot_general(qm, ku, nt, preferred_element_type=F32) + bias_ref[p * 2 + hh]
            s_ctx = lax.dot_general(qm, kc, nt, preferred_element_type=F32)
            m = jnp.maximum(jnp.max(s_loc, axis=1, keepdims=True), jnp.max(s_ctx, axis=1, keepdims=True))
            p_loc = jnp.exp(s_loc - m)
            p_ctx = jnp.exp(s_ctx - m)
            denom = jnp.sum(p_loc, axis=1, keepdims=True) + jnp.sum(p_ctx, axis=1, keepdims=True)
            o = (jnp.dot(p_loc.astype(BF16), vu, preferred_element_type=F32)
                 + jnp.dot(p_ctx.astype(BF16), vc, preferred_element_type=F32)) / denom
            out = jnp.where(mine, o, out)
        o_ref[:, ls] = out.astype(o_ref.dtype)


def _na_attention(na_qkv, na_kv_ctx, bias_tables, batch, t_len, ctx_len):
    rows = t_len // GRID_W
    n_rb = rows // NA_ROW_BLOCK
    tq = NA_ROW_BLOCK * GRID_W
    qkv = na_qkv.reshape(batch, t_len, 3 * NA_WIDTH)
    kvc = na_kv_ctx.reshape(batch, ctx_len, 2 * NA_WIDTH)

    def bias_idx(b, r):
        return (jnp.where(r == 0, 0, jnp.where(r == n_rb - 1, 2, 1)), 0, 0, 0)

    return pl.pallas_call(
        functools.partial(_na_kernel, rows),
        out_shape=jax.ShapeDtypeStruct((batch, t_len, NA_WIDTH), BF16),
        grid=(batch, n_rb),
        in_specs=[pl.BlockSpec((None, tq, NA_WIDTH), lambda b, r: (b, r, 0)),
                  pl.BlockSpec((None, t_len, NA_WIDTH), lambda b, r: (b, 0, 1)),
                  pl.BlockSpec((None, t_len, NA_WIDTH), lambda b, r: (b, 0, 2)),
                  pl.BlockSpec((None, ctx_len, NA_WIDTH), lambda b, r: (b, 0, 0)),
                  pl.BlockSpec((None, ctx_len, NA_WIDTH), lambda b, r: (b, 0, 1)),
                  pl.BlockSpec((None,) + bias_tables.shape[1:], bias_idx)],
        out_specs=pl.BlockSpec((None, tq, NA_WIDTH), lambda b, r: (b, r, 0)),
        compiler_params=_params(2),
        name="na_attention",
    )(qkv, qkv, qkv, kvc, kvc, bias_tables)


def _gla_constants(c):
    tris, masks = [], []
    for reverse in (False, True):
        i = np.arange(c)[:, None]
        j = np.arange(c)[None, :]
        tris.append((j >= i) if reverse else (j <= i))
        i = np.arange(c // 2)[:, None]
        j = np.arange(c // 2)[None, :]
        if reverse:
            i, j = j, i
        level = []
        s = c // 4
        while s >= GLA_DIAG:
            level.append(((i // (2 * s)) == (j // (2 * s))) & ((i % (2 * s)) >= s) & ((j % (2 * s)) < s))
            s //= 2
        level.append(((i // GLA_DIAG) == (j // GLA_DIAG)) & (j <= i))
        masks.append(np.stack(level))
    return jnp.asarray(np.stack(tris), BF16), jnp.asarray(np.stack(masks), F32)


def _block_refs(cum, s, reverse, diag):
    c = cum.shape[0]
    span = s if diag else 2 * s
    parts = []
    for p in range(c // span):
        if diag:
            r = p * span + (span - 1 if reverse else 0)
        else:
            r = p * span + (s - 1 if reverse else s)
        parts.append(jnp.broadcast_to(cum[r:r + 1, :], (span, cum.shape[1])))
    return jnp.concatenate(parts, axis=0)


def _cumsum_rows(a_hl, tri):
    parts = jnp.dot(tri, a_hl, preferred_element_type=F32)
    w = a_hl.shape[1] // 2
    return parts[:, :w] + parts[:, w:]


def _gla_chunk(q, k, v, a, state_t, tri, mask_ref, reverse, want_out):
    c = k.shape[0]
    hc = c // 2
    nt = (((1,), (1,)), ((), ()))
    k = k.astype(F32)
    q = q.astype(F32) if want_out else None
    cum = _cumsum_rows(a, tri)
    last = 0 if reverse else c - 1
    total = cum[last:last + 1, :]

    def scale(x, log2_factor):
        return (x * jnp.exp2(log2_factor)).astype(BF16)

    out = None
    if want_out:
        out = lax.dot_general(scale(q, cum), state_t.astype(BF16), nt, preferred_element_type=F32)
        halves = (slice(hc, c), slice(0, hc)) if reverse else (slice(0, hc), slice(hc, c))
        early, late = halves
        r = hc - 1 if reverse else hc
        g = cum[r:r + 1, :]
        cross = lax.dot_general(scale(q[late], cum[late] - g), scale(k[early], g - cum[early]), nt,
                                preferred_element_type=F32)
        inner = []
        for rows in halves:
            cx, qx, kx = cum[rows], q[rows], k[rows]
            acc = jnp.zeros((hc, hc), F32)
            s = hc // 2
            level = 0
            while True:
                diag = s < GLA_DIAG
                d = cx - _block_refs(cx, GLA_DIAG if diag else s, reverse, diag)
                if diag:
                    qs, ks = scale(qx, d), scale(kx, -d)
                else:
                    e = jnp.exp2(-jnp.abs(d))
                    qs, ks = (qx * e).astype(BF16), (kx * e).astype(BF16)
                acc = acc + lax.dot_general(qs, ks, nt, preferred_element_type=F32) * mask_ref[level]
                if diag:
                    break
                s //= 2
                level += 1
            inner.append(acc)
        out_early = jnp.dot(inner[0].astype(BF16), v[early], preferred_element_type=F32)
        out_late = jnp.dot(jnp.concatenate([cross, inner[1]], axis=1).astype(BF16),
                           jnp.concatenate([v[early], v[late]], axis=0), preferred_element_type=F32)
        intra = [out_late, out_early] if reverse else [out_early, out_late]
        out = out + jnp.concatenate(intra, axis=0)
    upd = lax.dot_general(v, scale(k, total - cum), (((0,), (0,)), ((), ())), preferred_element_type=F32)
    return out, jnp.exp2(total) * state_t + upd


def _gla_kernel(qf_ref, kf_ref, vf_ref, af_ref, qb_ref, kb_ref, vb_ref, ab_ref,
                kc_ref, vc_ref, acf_ref, acb_ref, tri_ref, mask_ref,
                of_ref, ob_ref, sf_ref, sb_ref):
    c = pl.program_id(2)
    fm, bm = mask_ref.at[0], mask_ref.at[1]
    heads = sf_ref.shape[0]

    @pl.when(c == 0)
    def _():
        zero = jnp.zeros(sf_ref.shape[1:], F32)
        for h in range(heads):
            kc = kc_ref[:, h * GLA_DK:(h + 1) * GLA_DK]
            vc = vc_ref[:, h * GLA_DV:(h + 1) * GLA_DV]
            hl = slice(2 * h * GLA_DK, 2 * (h + 1) * GLA_DK)
            _, sf_ref[h] = _gla_chunk(None, kc, vc, acf_ref[:, hl], zero, tri_ref[0], fm, False, False)
            _, sb_ref[h] = _gla_chunk(None, kc, vc, acb_ref[:, hl], zero, tri_ref[1], bm, True, False)

    @pl.when(c > 0)
    def _():
        for h in range(heads):
            ks = slice(h * GLA_DK, (h + 1) * GLA_DK)
            vs = slice(h * GLA_DV, (h + 1) * GLA_DV)
            hl = slice(2 * h * GLA_DK, 2 * (h + 1) * GLA_DK)
            o, sf_ref[h] = _gla_chunk(qf_ref[:, ks], kf_ref[:, ks], vf_ref[:, vs], af_ref[:, hl], sf_ref[h],
                                      tri_ref[0], fm, False, True)
            of_ref[:, vs] = o.astype(of_ref.dtype)
            o, sb_ref[h] = _gla_chunk(qb_ref[:, ks], kb_ref[:, ks], vb_ref[:, vs], ab_ref[:, hl], sb_ref[h],
                                      tri_ref[1], bm, True, True)
            ob_ref[:, vs] = o.astype(ob_ref.dtype)


def _gla(gla_qk, vb, decay, k_ctx, v_ctx, decay_ctx, batch, t_len, ctx_len):
    nc = t_len // GLA_CHUNK
    h = GLA_HEADS
    qk = gla_qk.reshape(batch, t_len, 2 * GLA_QK_WIDTH)
    v3 = vb.reshape(batch, t_len, GLA_V_WIDTH)
    a3 = decay.reshape(batch, t_len, GLA_DECAY_WIDTH)
    kc3 = k_ctx.reshape(batch, ctx_len, GLA_QK_WIDTH)
    vc3 = v_ctx.reshape(batch, ctx_len, GLA_V_WIDTH)
    ac3 = decay_ctx.reshape(batch, ctx_len, GLA_DECAY_WIDTH)
    tri, masks = _gla_constants(GLA_CHUNK)

    def fwd(c):
        return jnp.maximum(c - 1, 0)

    def bwd(c):
        return nc - 1 - jnp.maximum(c - 1, 0)

    def const(a):
        return pl.BlockSpec(a.shape, lambda b, hh, c: (0,) * a.ndim, pipeline_mode=pl.Buffered(1))

    hp = GLA_HEADS_PER_STEP
    groups = h // hp
    cq = (None, GLA_CHUNK, hp * GLA_DK)
    cv = (None, GLA_CHUNK, hp * GLA_DV)
    ca = (None, GLA_CHUNK, hp * 2 * GLA_DK)
    cca = (None, ctx_len, hp * 2 * GLA_DK)
    in_specs = [
        pl.BlockSpec(cq, lambda b, g, c: (b, fwd(c), g)),
        pl.BlockSpec(cq, lambda b, g, c: (b, fwd(c), groups + g)),
        pl.BlockSpec(cv, lambda b, g, c: (b, fwd(c), g)),
        pl.BlockSpec(ca, lambda b, g, c: (b, fwd(c), g)),
        pl.BlockSpec(cq, lambda b, g, c: (b, bwd(c), g)),
        pl.BlockSpec(cq, lambda b, g, c: (b, bwd(c), groups + g)),
        pl.BlockSpec(cv, lambda b, g, c: (b, bwd(c), g)),
        pl.BlockSpec(ca, lambda b, g, c: (b, bwd(c), groups + g)),
        pl.BlockSpec((None, ctx_len, hp * GLA_DK), lambda b, g, c: (b, 0, g)),
        pl.BlockSpec((None, ctx_len, hp * GLA_DV), lambda b, g, c: (b, 0, g)),
        pl.BlockSpec(cca, lambda b, g, c: (b, 0, g)),
        pl.BlockSpec(cca, lambda b, g, c: (b, 0, groups + g)),
        const(tri), const(masks),
    ]
    out_specs = [pl.BlockSpec(cv, lambda b, g, c: (b, fwd(c), g)),
                 pl.BlockSpec(cv, lambda b, g, c: (b, bwd(c), g))]
    out_shape = [jax.ShapeDtypeStruct((batch, t_len, GLA_V_WIDTH), BF16)] * 2
    return pl.pallas_call(
        _gla_kernel,
        out_shape=out_shape,
        grid=(batch, groups, nc + 1),
        in_specs=in_specs,
        out_specs=out_specs,
        scratch_shapes=[pltpu.VMEM((hp, GLA_DV, GLA_DK), F32), pltpu.VMEM((hp, GLA_DV, GLA_DK), F32)],
        compiler_params=_params(3),
        name="gla",
    )(qk, qk, v3, a3, qk, qk, v3, a3, kc3, vc3, ac3, ac3, tri, masks)


def _decay_weights(w_f, b_f, w_b, b_b):
    r, width = w_f.shape
    w2 = jnp.zeros((LANES, 2 * width), F32).at[:r, :width].set(w_f).at[r:2 * r, width:].set(w_b)
    return w2.astype(BF16), jnp.concatenate([b_f, b_b]).reshape(1, 2 * width)


def _layer_norm(v, g, b):
    mu = jnp.mean(v, axis=1, keepdims=True)
    var = jnp.mean(jnp.square(v - mu), axis=1, keepdims=True)
    return (v - mu) * lax.rsqrt(var + EPS) * g + b


def _merge_kernel(oa_ref, of_ref, ob_ref, gb_ref, gates_ref, x_ref, g1_ref, sc2_ref, sh2_ref,
                  wa_ref, wb_ref, wo_ref, nw_ref, lg_ref, lb_ref, rw_ref,
                  x1_ref, h2_ref, sc_ref):
    d = x_ref.shape[1]
    o = of_ref[...].astype(F32) + ob_ref[...].astype(F32)
    pieces = []
    for hh in range(GLA_HEADS):
        oh = o[:, hh * GLA_DV:(hh + 1) * GLA_DV]
        pieces.append(oh * lax.rsqrt(jnp.mean(jnp.square(oh), axis=1, keepdims=True) + EPS))
    out_b = jnp.concatenate(pieces, axis=1) * nw_ref[...] * _silu(gb_ref[...].astype(F32))
    ya = jnp.dot(oa_ref[...], wa_ref[...], preferred_element_type=F32)
    yb = jnp.dot(out_b.astype(BF16), wb_ref[...], preferred_element_type=F32)
    y = _sigmoid(gates_ref[:, :d].astype(F32)) * ya + _sigmoid(gates_ref[:, d:].astype(F32)) * yb
    y2 = jnp.dot(y.astype(BF16), wo_ref[...], preferred_element_type=F32)
    x1 = _layer_norm(DEEPNORM_ALPHA * x_ref[...] + g1_ref[0] * y2, lg_ref[...], lb_ref[...])
    x1_ref[...] = x1
    h2 = x1 * (1.0 + sc2_ref[0]) + sh2_ref[0]
    h2_ref[...] = _pack_pairs(h2)
    logits_t = lax.dot_general(rw_ref[...], h2.astype(BF16), (((1,), (1,)), ((), ())), preferred_element_type=F32)
    sc_ref[...] = _sigmoid(logits_t)


def _merge(out_a, o_f, o_b, gb, gates, x2d, g1, sc2, sh2, wa, wb, wo, nw, lg, lb, rw_t, tile, tiles_per_batch):
    n, d = x2d.shape
    row = lambda i: (i, 0)
    mod = lambda i: (i // tiles_per_batch, 0, 0)
    full = lambda i: (0, 0)

    def const(a):
        return pl.BlockSpec(a.shape, full, pipeline_mode=pl.Buffered(1))

    in_specs = [pl.BlockSpec((tile, NA_WIDTH), row), pl.BlockSpec((tile, GLA_V_WIDTH), row),
                pl.BlockSpec((tile, GLA_V_WIDTH), row), pl.BlockSpec((tile, GLA_V_WIDTH), row),
                pl.BlockSpec((tile, 2 * d), row), pl.BlockSpec((tile, d), row),
                pl.BlockSpec((1, 1, d), mod), pl.BlockSpec((1, 1, d), mod), pl.BlockSpec((1, 1, d), mod),
                const(wa), const(wb), const(wo), const(nw), const(lg), const(lb), const(rw_t)]
    out_shape = [jax.ShapeDtypeStruct((n, d), F32), jax.ShapeDtypeStruct((n, d // 2), U32),
                 jax.ShapeDtypeStruct((N_EXPERTS, n), F32)]
    out_specs = [pl.BlockSpec((tile, d), row), pl.BlockSpec((tile, d // 2), row),
                 pl.BlockSpec((N_EXPERTS, tile), lambda i: (0, i))]
    return pl.pallas_call(
        _merge_kernel, out_shape=out_shape, grid=(n // tile,), in_specs=in_specs, out_specs=out_specs,
        compiler_params=_params(1), name="merge_ln1_router",
    )(out_a, o_f, o_b, gb, gates, x2d, g1, sc2, sh2, wa, wb, wo, nw, lg, lb, rw_t)


def _first_argmax(vals, idx, n):
    m = jnp.max(vals, axis=0, keepdims=True)
    first = jnp.min(jnp.where(vals == m, idx, float(n)), axis=0, keepdims=True)
    return m, first


def _route_kernel(sc_ref, bias_ref, before_ref, e_ref, w_ref, rank_ref, cnt_ref, carry_ref):
    step = pl.program_id(0)
    tr = sc_ref.shape[1]

    @pl.when(step == 0)
    def _():
        carry_ref[...] = jnp.zeros(carry_ref.shape, F32)

    scores = sc_ref[...]
    biased = scores + bias_ref[...]
    eidx = lax.broadcasted_iota(jnp.int32, (N_EXPERTS, tr), 0).astype(F32)
    lidx = lax.broadcasted_iota(jnp.int32, (GROUP_SIZE, tr), 0).astype(F32)
    gidx = lax.broadcasted_iota(jnp.int32, (N_GROUPS, tr), 0).astype(F32)
    gs = []
    for g in range(N_GROUPS):
        blk = biased[g * GROUP_SIZE:(g + 1) * GROUP_SIZE]
        m1, first = _first_argmax(blk, lidx, GROUP_SIZE)
        m2 = jnp.max(jnp.where(lidx == first, -jnp.inf, blk), axis=0, keepdims=True)
        gs.append(m1 + m2)
    cur = jnp.concatenate(gs, axis=0)
    keep = jnp.zeros((N_GROUPS, tr), F32)
    for _ in range(TOPK_GROUPS):
        _, first = _first_argmax(cur, gidx, N_GROUPS)
        sel = gidx == first
        keep = jnp.where(sel, 1.0, keep)
        cur = jnp.where(sel, -jnp.inf, cur)
    keep_e = jnp.concatenate([jnp.broadcast_to(keep[g:g + 1], (GROUP_SIZE, tr)) for g in range(N_GROUPS)], axis=0)
    masked = jnp.where(keep_e > 0.5, biased, -jnp.inf)
    chosen = jnp.zeros((N_EXPERTS, tr), F32)
    tops, topi = [], []
    for _ in range(TOP_K):
        _, first = _first_argmax(masked, eidx, N_EXPERTS)
        sel = eidx == first
        tops.append(jnp.sum(jnp.where(sel, scores, 0.0), axis=0, keepdims=True))
        topi.append(first)
        chosen = jnp.where(sel, 1.0, chosen)
        masked = jnp.where(sel, -jnp.inf, masked)
    top_s = jnp.concatenate(tops, axis=0)
    top_i = jnp.concatenate(topi, axis=0)
    e_ref[...] = top_i.astype(jnp.int32)
    w_ref[...] = top_s / jnp.sum(top_s, axis=0, keepdims=True) * ROUTED_SCALE
    prior = jnp.dot(chosen.astype(BF16), before_ref[...], preferred_element_type=F32) + carry_ref[...]
    ranks = [jnp.sum(jnp.where(eidx == topi[kk], prior, 0.0), axis=0, keepdims=True) for kk in range(TOP_K)]
    rank_ref[...] = jnp.concatenate(ranks, axis=0).astype(jnp.int32)
    carry_ref[...] = carry_ref[...] + jnp.sum(chosen, axis=1, keepdims=True)
    cnt_ref[...] = jnp.broadcast_to(carry_ref[...], cnt_ref.shape).astype(jnp.int32)


def _route(scores_t, router_bias, tile):
    n = scores_t.shape[1]
    col = lambda i: (0, i)
    out_shape = [jax.ShapeDtypeStruct((TOP_K, n), jnp.int32), jax.ShapeDtypeStruct((TOP_K, n), F32),
                 jax.ShapeDtypeStruct((TOP_K, n), jnp.int32), jax.ShapeDtypeStruct((N_EXPERTS, LANES), jnp.int32)]
    before = jnp.asarray(np.arange(tile)[:, None] < np.arange(tile)[None, :], BF16)
    return pl.pallas_call(
        _route_kernel, out_shape=out_shape, grid=(n // tile,),
        in_specs=[pl.BlockSpec((N_EXPERTS, tile), col), pl.BlockSpec((N_EXPERTS, 1), lambda i: (0, 0)),
                  pl.BlockSpec((tile, tile), lambda i: (0, 0), pipeline_mode=pl.Buffered(1))],
        out_specs=[pl.BlockSpec((TOP_K, tile), col), pl.BlockSpec((TOP_K, tile), col),
                   pl.BlockSpec((TOP_K, tile), col), pl.BlockSpec((N_EXPERTS, LANES), lambda i: (0, 0))],
        scratch_shapes=[pltpu.VMEM((N_EXPERTS, 1), F32)],
        compiler_params=_params(1), name="route",
    )(scores_t, router_bias.reshape(N_EXPERTS, 1), before)


def _slots_kernel(e_ref, rank_ref, start_ref, dest_ref):
    tr = e_ref.shape[1]
    eidx = lax.broadcasted_iota(jnp.int32, (N_EXPERTS, tr), 0)
    e = e_ref[...]
    start = start_ref[...]
    rows = [jnp.sum(jnp.where(eidx == e[kk:kk + 1], start, 0.0), axis=0, keepdims=True) for kk in range(TOP_K)]
    dest_ref[...] = jnp.concatenate(rows, axis=0).astype(jnp.int32) + rank_ref[...]


def _slots(top_e, rank, start_rows, tile):
    n = top_e.shape[1]
    col = lambda i: (0, i)
    return pl.pallas_call(
        _slots_kernel, out_shape=jax.ShapeDtypeStruct((TOP_K, n), jnp.int32), grid=(n // tile,),
        in_specs=[pl.BlockSpec((TOP_K, tile), col), pl.BlockSpec((TOP_K, tile), col),
                  pl.BlockSpec((N_EXPERTS, 1), lambda i: (0, 0))],
        out_specs=pl.BlockSpec((TOP_K, tile), col),
        compiler_params=_params(1), name="slots",
    )(top_e, rank, start_rows)


def _sc_mesh():
    return plsc.VectorSubcoreMesh(core_axis_name="c", subcore_axis_name="s",
                                  num_cores=SC_CORES, num_subcores=SC_SUBCORES)


def _sc_scatter_rows(rows, idx, n_out):
    n, dp = rows.shape
    copies = idx.shape[0] // n
    per_worker = n // (SC_CORES * SC_SUBCORES)
    chunk = SC_GATHER_ROWS

    @functools.partial(
        pl.kernel, mesh=_sc_mesh(), out_type=jax.ShapeDtypeStruct((n_out, dp), rows.dtype),
        scratch_types=[pltpu.VMEM((chunk,), jnp.int32), pltpu.VMEM((chunk, dp), rows.dtype)],
        name="sc_scatter_rows")
    def scatter(rows_hbm, idx_hbm, out_hbm, idx_v, rows_v):
        base = (lax.axis_index("s") * SC_CORES + lax.axis_index("c")) * per_worker

        @pl.loop(0, per_worker // chunk)
        def _(j):
            off = base + j * chunk
            pltpu.sync_copy(rows_hbm.at[pl.ds(off, chunk)], rows_v)
            for k in range(copies):
                pltpu.sync_copy(idx_hbm.at[pl.ds(k * n + off, chunk)], idx_v)
                pltpu.sync_copy(rows_v, out_hbm.at[idx_v])

    return scatter(rows, idx)


def _expert_kernel(start_ref, cnt_ref, nused_ref, xs_ref, wg_ref, wu_ref, wd_ref, y_ref,
                   wgb, wub, wdb, xbuf, ybuf, sem_in, sem_out):
    e = pl.program_id(0)
    bm, ring = EXPERT_BLOCK, EXPERT_RING
    cnt = cnt_ref[e]
    nb = (cnt + bm - 1) // bm
    g0 = start_ref[e]
    n_used = nused_ref[0]

    def x_copy(g):
        slot = g % ring
        return pltpu.make_async_copy(xs_ref.at[pl.ds(g * bm, bm)], xbuf.at[slot], sem_in.at[slot])

    def y_copy(g):
        slot = g % ring
        return pltpu.make_async_copy(ybuf.at[slot], y_ref.at[pl.ds(g * bm, bm)], sem_out.at[slot])

    @pl.when(e == 0)
    def _():
        for g in range(ring):
            @pl.when(g < n_used)
            def _():
                x_copy(g).start()

    wgb[...] = wg_ref[...].astype(BF16)
    wub[...] = wu_ref[...].astype(BF16)
    wdb[...] = wd_ref[...].astype(BF16)
    row = lax.broadcasted_iota(jnp.int32, (bm, xbuf.shape[2]), 0)

    def blocks(j, count):
        for b in range(count):
            g = g0 + j + b
            x_copy(g).wait()

            @pl.when(g >= ring)
            def _():
                y_copy(g - ring).wait()

        for b in range(count):
            slot = (g0 + j + b) % ring
            x = _unpack_pairs(jnp.where(row < cnt - (j + b) * bm, xbuf[slot], jnp.uint32(0))).astype(BF16)
            gate = jnp.dot(x, wgb[...], preferred_element_type=F32)
            up = jnp.dot(x, wub[...], preferred_element_type=F32)
            ybuf[slot] = _pack_pairs(
                jnp.dot((_silu(gate) * up).astype(BF16), wdb[...], preferred_element_type=F32))

        for b in range(count):
            g = g0 + j + b
            y_copy(g).start()

            @pl.when(g + ring < n_used)
            def _():
                x_copy(g + ring).start()

    group = EXPERT_GROUP

    def full_group(p, carry):
        blocks(group * p, group)
        return carry

    lax.fori_loop(0, nb // group, full_group, 0)
    done = (nb // group) * group
    size = group // 2
    while size >= 1:
        @pl.when((nb - done) % (2 * size) >= size)
        def _(done=done, size=size):
            blocks(done, size)

        done = done + jnp.where((nb - done) % (2 * size) >= size, size, 0)
        size //= 2

    @pl.when(e == pl.num_programs(0) - 1)
    def _():
        for back in range(1, ring + 1):
            @pl.when(n_used - back >= 0)
            def _():
                y_copy(n_used - back).wait()


def _experts(blk_start, counts, n_used, xs, wg, wu, wd):
    n_slots, dp = xs.shape
    n_exp, d, ff = wg.shape
    bm, ring = EXPERT_BLOCK, EXPERT_RING
    wsel = lambda e, st, ct, nu: (e, 0, 0)
    grid_spec = pltpu.PrefetchScalarGridSpec(
        num_scalar_prefetch=3, grid=(n_exp,),
        in_specs=[pl.BlockSpec(memory_space=pl.ANY),
                  pl.BlockSpec((None, d, ff), wsel), pl.BlockSpec((None, d, ff), wsel),
                  pl.BlockSpec((None, ff, d), wsel)],
        out_specs=pl.BlockSpec(memory_space=pl.ANY),
        scratch_shapes=[pltpu.VMEM((d, ff), BF16), pltpu.VMEM((d, ff), BF16), pltpu.VMEM((ff, d), BF16),
                        pltpu.VMEM((ring, bm, dp), U32), pltpu.VMEM((ring, bm, dp), U32),
                        pltpu.SemaphoreType.DMA((ring,)), pltpu.SemaphoreType.DMA((ring,))])
    return pl.pallas_call(
        _expert_kernel, out_shape=jax.ShapeDtypeStruct((n_slots, dp), U32), grid_spec=grid_spec,
        compiler_params=_params(1), name="experts",
    )(blk_start, counts, n_used, xs, wg, wu, wd)


def _sc_gather_rows(table, idx):
    n_idx = idx.shape[0]
    dp = table.shape[1]
    workers = SC_CORES * SC_SUBCORES
    per_worker = n_idx // workers
    chunk = SC_GATHER_ROWS

    @functools.partial(
        pl.kernel, mesh=_sc_mesh(), out_type=jax.ShapeDtypeStruct((n_idx, dp), table.dtype),
        scratch_types=[pltpu.VMEM((chunk,), jnp.int32), pltpu.VMEM((chunk, dp), table.dtype),
                       pltpu.SemaphoreType.DMA],
        name="sc_gather_rows")
    def gather(table_hbm, idx_hbm, out_hbm, idx_v, rows_v, sem):
        base = (lax.axis_index("s") * SC_CORES + lax.axis_index("c")) * per_worker

        @pl.loop(0, per_worker // chunk)
        def _(j):
            off = base + j * chunk
            pltpu.sync_copy(idx_hbm.at[pl.ds(off, chunk)], idx_v)
            pltpu.async_copy(table_hbm.at[idx_v], rows_v, sem).wait()
            pltpu.sync_copy(rows_v, out_hbm.at[pl.ds(off, chunk)])

    return gather(table, idx)


def _combine_kernel(tile, yt_ref, w_ref, h_ref, x1_ref, g2_ref, sg_ref, su_ref, sd_ref, lg_ref, lb_ref, *rest):
    o_ref = rest[-1]
    hb = _unpack_pairs(h_ref[...]).astype(BF16)
    g = jnp.dot(hb, sg_ref[...], preferred_element_type=F32)
    u = jnp.dot(hb, su_ref[...], preferred_element_type=F32)
    f = jnp.dot((_silu(g) * u).astype(BF16), sd_ref[...], preferred_element_type=F32)
    w = w_ref[...]
    for kk in range(TOP_K):
        f = f + w[:, kk:kk + 1] * _unpack_pairs(yt_ref[kk * tile:(kk + 1) * tile, :])
    o_ref[...] = _layer_norm(DEEPNORM_ALPHA * x1_ref[...] + g2_ref[0] * f, lg_ref[...], lb_ref[...])


def _combine(y_tok, top_w, h2p, x1, g2, sg, su, sd, lg, lb, tile, tiles_per_batch, first_tile, prev_out):
    n, d = x1.shape
    dp = h2p.shape[1]
    row = lambda i: (first_tile + i, 0)
    full = lambda i: (0, 0)

    def const(a):
        return pl.BlockSpec(a.shape, full, pipeline_mode=pl.Buffered(1))

    in_specs = [pl.BlockSpec((TOP_K * tile, dp), lambda i: (i, 0)),
                pl.BlockSpec((tile, TOP_K), row), pl.BlockSpec((tile, dp), row), pl.BlockSpec((tile, d), row),
                pl.BlockSpec((1, 1, d), lambda i: ((first_tile + i) // tiles_per_batch, 0, 0)),
                const(sg), const(su), const(sd), const(lg), const(lb)]
    args = [y_tok, top_w, h2p, x1, g2, sg, su, sd, lg, lb]
    aliases = {}
    if prev_out is not None:
        in_specs.append(pl.BlockSpec(memory_space=pl.ANY))
        args.append(prev_out)
        aliases = {len(args) - 1: 0}
    return pl.pallas_call(
        functools.partial(_combine_kernel, tile),
        out_shape=jax.ShapeDtypeStruct((n, d), F32),
        grid=(y_tok.shape[0] // (TOP_K * tile),),
        in_specs=in_specs,
        out_specs=pl.BlockSpec((tile, d), row),
        input_output_aliases=aliases,
        compiler_params=_params(1), name="combine_shared_ln2",
    )(*args)


def kernel(x, c, ctx, c_ctx, w_mod, b_mod, w_in, na_rpb, gla_w_decay_f, gla_b_decay_f, gla_w_decay_b, gla_b_decay_b,
           gla_norm_w, w_branch_a, w_branch_b, w_out, ln1_g, ln1_b, router_w, router_bias, exp_w_gate, exp_w_up,
           exp_w_down, sh_w_gate, sh_w_up, sh_w_down, ln2_g, ln2_b):
    batch, t_len, d = x.shape
    ctx_len = ctx.shape[1]
    n = batch * t_len
    assert w_mod.shape[0] == DEPTH == 1
    assert t_len % GLA_CHUNK == 0 and ctx_len == GLA_CHUNK and (t_len // GRID_W) % NA_ROW_BLOCK == 0

    mod_rows = 16
    c_all = jnp.zeros((mod_rows, d), F32).at[:batch].set(c).at[batch].set(c_ctx)
    mod = _modulation(c_all, w_mod[0], b_mod[0])
    sh1, sc1, g1, sh2, sc2, g2 = [mod[:batch, j * d:(j + 1) * d].reshape(batch, 1, d) for j in range(6)]
    sh1c = mod[batch:batch + 1, 0:d].reshape(1, 1, d)
    sc1c = mod[batch:batch + 1, d:2 * d].reshape(1, 1, d)

    offs = np.cumsum((0, NA_WIDTH, NA_WIDTH, NA_WIDTH, GLA_QK_WIDTH, GLA_QK_WIDTH, GLA_V_WIDTH, GLA_V_WIDTH,
                      GLA_GATE_RANK, GLA_GATE_RANK, d, d))
    qa, ka, va, qb, kb, vbc, gbc, lrf, lrb, ga, gbt = [w_in[0][:, offs[j]:offs[j + 1]] for j in range(11)]
    lr_cols = jnp.concatenate([lrf, lrb, jnp.zeros((d, LANES - 2 * GLA_GATE_RANK), F32)], axis=1)
    w_lat = jnp.concatenate([lr_cols, qa, ka, va, qb, kb, vbc, gbc, ga, gbt], axis=1).astype(BF16)
    lat_offs = np.cumsum((0, LANES, NA_WIDTH, NA_WIDTH, NA_WIDTH, GLA_QK_WIDTH, GLA_QK_WIDTH, GLA_V_WIDTH))
    w_ctx = jnp.concatenate([w_lat[:, lat_offs[j]:lat_offs[j + 1]] for j in (0, 2, 3, 5, 6)], axis=1)
    plain = ("plain",)
    lat_plan = ((LANES, (("decay",),)),
                (3 * NA_WIDTH, (("scale", NA_HEAD_DIM ** -0.5), plain, plain)),
                (2 * GLA_QK_WIDTH, (("rope", GLA_DK ** -0.5), ("rope", 1.0))),
                (GLA_V_WIDTH, (plain, plain)), (GLA_V_WIDTH, (plain, plain)),
                (2 * d, (plain,) * 4))
    ctx_plan = ((LANES, (("decay",),)),
                (2 * NA_WIDTH, (plain, plain)), (GLA_QK_WIDTH, (plain,)), (GLA_V_WIDTH, (plain, plain)))
    w2, b2 = _decay_weights(gla_w_decay_f[0], gla_b_decay_f[0], gla_w_decay_b[0], gla_b_decay_b[0])
    tile = 256
    x2d = x.reshape(n, d)
    decay, na_qkv, gla_qk, vb, gb, gates = _projection(
        x2d, sc1, sh1, w_lat, w2, b2, lat_plan, (BF16,) * 6, PROJ_TILE, t_len // PROJ_TILE,
        rope=_rope_tables(t_len))
    decay_c, na_kv_c, k_c, v_c = _projection(
        ctx.reshape(batch * ctx_len, d), sc1c, sh1c, w_ctx, w2, b2, ctx_plan, (BF16,) * 4, tile,
        batch * ctx_len // tile)

    out_a = _na_attention(na_qkv, na_kv_c, _na_bias_tables(na_rpb[0]), batch, t_len, ctx_len)
    o_f, o_b = _gla(gla_qk, vb, decay, k_c, v_c, decay_c, batch, t_len, ctx_len)

    x1, h2p, scores_t = _merge(
        out_a.reshape(n, NA_WIDTH), o_f.reshape(n, GLA_V_WIDTH), o_b.reshape(n, GLA_V_WIDTH), gb, gates, x2d,
        g1, sc2, sh2, w_branch_a[0].astype(BF16), w_branch_b[0].astype(BF16), w_out[0].astype(BF16),
        gla_norm_w[0].reshape(1, -1), ln1_g[0].reshape(1, d), ln1_b[0].reshape(1, d),
        router_w[0].T.astype(BF16), PROJ_TILE, t_len // PROJ_TILE)

    top_e, top_w, rank, counts = _route(scores_t, router_bias[0], 512)

    counts = counts[:, 0]
    n_blocks = n * TOP_K // EXPERT_BLOCK + N_EXPERTS
    blocks_per = (counts + EXPERT_BLOCK - 1) // EXPERT_BLOCK
    blk_end = jnp.cumsum(blocks_per)
    blk_start = blk_end - blocks_per
    dest = _slots(top_e, rank, (blk_start * EXPERT_BLOCK).astype(F32).reshape(N_EXPERTS, 1), 512)

    xs = _sc_scatter_rows(h2p, dest.reshape(TOP_K * n), n_blocks * EXPERT_BLOCK)
    y = _experts(blk_start.astype(jnp.int32), counts, blk_end[-1:].astype(jnp.int32), xs,
                 exp_w_gate[0], exp_w_up[0], exp_w_down[0])
    dest_tok = dest.reshape(TOP_K, n // tile, tile).transpose(1, 0, 2).reshape(COMBINE_CHUNKS, -1)
    w_rows = top_w.T
    shared = (sh_w_gate[0].astype(BF16), sh_w_up[0].astype(BF16), sh_w_down[0].astype(BF16),
              ln2_g[0].reshape(1, d), ln2_b[0].reshape(1, d))
    tiles_per_chunk = n // tile // COMBINE_CHUNKS
    out = None
    for ci in range(COMBINE_CHUNKS):
        y_tok = _sc_gather_rows(y, dest_tok[ci])
        out = _combine(y_tok, w_rows, h2p, x1, g2, *shared, tile, t_len // tile, ci * tiles_per_chunk, out)
    return out.reshape(batch, t_len, d)
```

```python
import functools

import numpy as np
import jax
import jax.numpy as jnp
from jax import lax
from jax.experimental import pallas as pl
from jax.experimental.pallas import tpu as pltpu
from jax.experimental.pallas import tpu_sc as plsc

F32 = jnp.float32
BF16 = jnp.bfloat16
U32 = jnp.uint32
HIGHEST = lax.Precision.HIGHEST

GRID_W = 64
NA_HEADS = 8
NA_HEAD_DIM = 64
NA_WIDTH = NA_HEADS * NA_HEAD_DIM
WIN_ROWS = 8
WIN_COLS = 16
GLA_HEADS = 4
GLA_DK = 128
GLA_DV = 256
GLA_QK_WIDTH = GLA_HEADS * GLA_DK
GLA_V_WIDTH = GLA_HEADS * GLA_DV
GLA_GATE_RANK = 16
GLA_TAU = 16.0
LOG2E = 1.4426950408889634
ROPE_BASE = 10000.0
N_EXPERTS = 256
TOP_K = 8
N_GROUPS = 8
TOPK_GROUPS = 4
GROUP_SIZE = N_EXPERTS // N_GROUPS
ROUTED_SCALE = 2.5
DEPTH = 1
DEEPNORM_ALPHA = (2 * DEPTH) ** 0.25
EPS = 1e-6

LANES = 128
PROJ_TILE = 512
NA_ROW_BLOCK = 4
NA_BLOCKS_PER_STEP = 2
NA_UNION_ROWS = NA_ROW_BLOCK + WIN_ROWS - 1
GLA_CHUNK = 256
GLA_DIAG = 16
GLA_HEADS_PER_STEP = 4
EXPERT_BLOCK = 256
EXPERT_GROUP = 4
EXPERT_RING = 8
COMBINE_CHUNKS = 8
COMBINE_TILE = 512
SC_CORES = 2
SC_SUBCORES = 16
SC_GATHER_ROWS = 128
NEG_BIG = -1e30
VMEM_LIMIT = 56 * 1024 * 1024
NA_VMEM_LIMIT = 58 * 1024 * 1024


def _params(n_axes, vmem=VMEM_LIMIT):
    return pltpu.CompilerParams(dimension_semantics=("arbitrary",) * n_axes, vmem_limit_bytes=vmem)


def _sigmoid(v):
    return 1.0 / (1.0 + jnp.exp(-v))


def _silu(v):
    return v * _sigmoid(v)


def _pack_pairs(v):
    m = v.shape[1] // 2
    lo = lax.bitcast_convert_type(v[:, :m].astype(BF16).astype(F32), U32) >> 16
    hi = lax.bitcast_convert_type(v[:, m:].astype(BF16).astype(F32), U32) & jnp.uint32(0xFFFF0000)
    return lo | hi


def _unpack_pairs(p):
    lo = lax.bitcast_convert_type(p << 16, F32)
    hi = lax.bitcast_convert_type(p & jnp.uint32(0xFFFF0000), F32)
    return jnp.concatenate([lo, hi], axis=1)


def _mod_kernel(c_ref, w_ref, b_ref, o_ref):
    o_ref[...] = jnp.dot(_silu(c_ref[...]), w_ref[...], preferred_element_type=F32, precision=HIGHEST) + b_ref[...]


def _modulation(c_all, w_mod, b_mod):
    rows, d = c_all.shape
    n = w_mod.shape[1]
    bn = 512
    return pl.pallas_call(
        _mod_kernel,
        out_shape=jax.ShapeDtypeStruct((rows, n), F32),
        grid=(n // bn,),
        in_specs=[pl.BlockSpec((rows, d), lambda j: (0, 0)),
                  pl.BlockSpec((d, bn), lambda j: (0, j)),
                  pl.BlockSpec((1, bn), lambda j: (0, j))],
        out_specs=pl.BlockSpec((rows, bn), lambda j: (0, j)),
        compiler_params=_params(1),
        name="modulation",
    )(c_all, w_mod, b_mod.reshape(1, n))


def _swap32(v):
    lane = lax.broadcasted_iota(jnp.int32, v.shape, 1)
    return jnp.where((lane % 64) < 32, pltpu.roll(v, 96, 1), pltpu.roll(v, 32, 1))


GLA_DECAY_WIDTH = 2 * GLA_HEADS * 2 * GLA_DK


def _log2_decay_split(lr, w2, b2, out_ref):
    z = (jnp.dot(lr.astype(BF16), w2, preferred_element_type=F32) + b2) * LOG2E
    a = (jnp.minimum(z, 0.0) - jnp.log2(1.0 + jnp.exp2(-jnp.abs(z)))) * (1.0 / GLA_TAU)
    hi = a.astype(BF16)
    lo = (a - hi.astype(F32)).astype(BF16)
    for p in range(a.shape[1] // GLA_DK):
        src = slice(p * GLA_DK, (p + 1) * GLA_DK)
        out_ref[:, 2 * p * GLA_DK:(2 * p + 1) * GLA_DK] = hi[:, src]
        out_ref[:, (2 * p + 1) * GLA_DK:(2 * p + 2) * GLA_DK] = lo[:, src]


def _proj_kernel(plan, has_rope, *refs):
    x_ref, sc_ref, sh_ref, w_ref, w2_ref, b2_ref = refs[:6]
    pos = 6
    if has_rope:
        cos_ref, sin_ref = refs[6:8]
        pos = 8
    out_refs = refs[pos:]
    h = (x_ref[...] * (1.0 + sc_ref[0]) + sh_ref[0]).astype(BF16)
    col = 0
    for out_ref, (width, kinds) in zip(out_refs, plan):
        for j, kind in enumerate(kinds):
            cw = width // len(kinds)
            c0 = j * cw
            acc = jnp.dot(h, w_ref[:, col + c0:col + c0 + cw], preferred_element_type=F32)
            if kind[0] == "decay":
                _log2_decay_split(acc, w2_ref[...], b2_ref[...], out_ref)
                continue
            if kind[0] == "scale":
                acc = acc * kind[1]
            elif kind[0] == "rope":
                cos, sin = cos_ref[...], sin_ref[...]
                pieces = []
                for p in range(cw // LANES):
                    v = acc[:, p * LANES:(p + 1) * LANES]
                    pieces.append((v * cos + _swap32(v) * sin) * kind[1])
                acc = jnp.concatenate(pieces, axis=1)
            out_ref[:, c0:c0 + cw] = acc.astype(out_ref.dtype)
        col += width


def _projection(x2d, sc, sh, w, w2, b2, plan, out_dtypes, tile, tiles_per_mod, rope=None):
    n, d = x2d.shape
    const = lambda a: pl.BlockSpec(a.shape, lambda i: (0, 0), pipeline_mode=pl.Buffered(1))
    in_specs = [pl.BlockSpec((tile, d), lambda i: (i, 0)),
                pl.BlockSpec((1, 1, d), lambda i: (i // tiles_per_mod, 0, 0)),
                pl.BlockSpec((1, 1, d), lambda i: (i // tiles_per_mod, 0, 0)),
                const(w), const(w2), const(b2)]
    args = [x2d, sc, sh, w, w2, b2]
    if rope is not None:
        in_specs += [pl.BlockSpec((tile, LANES), lambda i: (i % tiles_per_mod, 0))] * 2
        args += list(rope)
    widths = [GLA_DECAY_WIDTH if kinds[0][0] == "decay" else wd for wd, kinds in plan]
    out_shape = [jax.ShapeDtypeStruct((n, wd), dt) for wd, dt in zip(widths, out_dtypes)]
    out_specs = [pl.BlockSpec((tile, wd), lambda i: (i, 0)) for wd in widths]
    return pl.pallas_call(
        functools.partial(_proj_kernel, plan, rope is not None),
        out_shape=out_shape,
        grid=(n // tile,),
        in_specs=in_specs,
        out_specs=out_specs,
        compiler_params=_params(1),
        name="in_proj" if rope is not None else "ctx_proj",
    )(*args)


def _rope_tables(t_len):
    half = GLA_DK // 2
    quarter = half // 2
    f32 = np.float32
    inv_freq = f32(ROPE_BASE) ** (-np.arange(quarter, dtype=f32) / f32(quarter))
    pos = np.arange(t_len)
    row_ang = (pos // GRID_W).astype(f32)[:, None] * inv_freq[None, :]
    col_ang = (pos % GRID_W).astype(f32)[:, None] * inv_freq[None, :]
    cr, sr, cc, sn = np.cos(row_ang), np.sin(row_ang), np.cos(col_ang), np.sin(col_ang)
    cos = np.concatenate([cr, cr, cc, cc], axis=1).astype(f32)
    sin = np.concatenate([-sr, sr, -sn, sn], axis=1).astype(f32)
    return jnp.asarray(cos), jnp.asarray(sin)


def _na_bias_tables(rpb):
    rb, ur, w = NA_ROW_BLOCK, NA_UNION_ROWS, GRID_W
    heads = rpb.shape[0]
    pad = jnp.pad(rpb, ((0, 0), (0, 0), (w, w)))
    toep = jnp.stack([pad[:, :, w + WIN_COLS - 1 - c:2 * w + WIN_COLS - 1 - c] for c in range(w)], axis=2)
    c = np.arange(w)[:, None]
    kc = np.arange(w)[None, :]
    col_start = np.clip(c - WIN_COLS // 2, 0, w - WIN_COLS)
    col_ok = (kc >= col_start) & (kc < col_start + WIN_COLS)
    toep = jnp.where(col_ok[None, None], toep, NEG_BIG)
    neg = jnp.full((heads, w, w), NEG_BIG, F32)
    half = WIN_ROWS // 2
    tables = []
    for lo, off in ((lambda i: 0, WIN_ROWS - 1), (lambda i: i, WIN_ROWS - 1 - half), (lambda i: ur - WIN_ROWS, 0)):
        rows_ = []
        for i in range(rb):
            blocks = [toep[:, j - i + off] if lo(i) <= j < lo(i) + WIN_ROWS else neg for j in range(ur)]
            rows_.append(jnp.concatenate(blocks, axis=2))
        tables.append(jnp.concatenate(rows_, axis=1))
    return jnp.stack(tables)


def _na_kernel(rows, q_ref, k_ref, v_ref, kc_ref, vc_ref, *rest):
    bias_refs, o_ref = rest[:-1], rest[-1]
    tq = NA_ROW_BLOCK * GRID_W
    nk = NA_UNION_ROWS * GRID_W
    nt = (((1,), (1,)), ((), ()))
    lane = lax.broadcasted_iota(jnp.int32, (tq, LANES), 1)
    for sub, bias_ref in enumerate(bias_refs):
        rb = pl.program_id(1) * len(bias_refs) + sub
        ustart = jnp.clip(rb * NA_ROW_BLOCK - WIN_ROWS // 2, 0, rows - NA_UNION_ROWS)
        k0 = pl.multiple_of(ustart * GRID_W, GRID_W)
        qrows = slice(sub * tq, (sub + 1) * tq)
        for p in range(NA_WIDTH // LANES):
            ls = slice(p * LANES, (p + 1) * LANES)
            q = q_ref[qrows, ls]
            ku = k_ref[pl.ds(k0, nk), ls]
            vu = v_ref[pl.ds(k0, nk), ls]
            kc = kc_ref[:, ls]
            vc = vc_ref[:, ls]
            out = jnp.zeros(q.shape, F32)
            for hh in range(LANES // NA_HEAD_DIM):
                mine = (lane // NA_HEAD_DIM) == hh
                qm = jnp.where(mine, q, jnp.zeros_like(q))
                s_loc = lax.dot_general(qm, ku, nt, preferred_element_type=F32) + bias_ref[p * 2 + hh]
                s_ctx = lax.dot_general(qm, kc, nt, preferred_element_type=F32)
                m = jnp.maximum(jnp.max(s_loc, axis=1, keepdims=True), jnp.max(s_ctx, axis=1, keepdims=True))
                p_loc = jnp.exp(s_loc - m)
                p_ctx = jnp.exp(s_ctx - m)
                denom = jnp.sum(p_loc, axis=1, keepdims=True) + jnp.sum(p_ctx, axis=1, keepdims=True)
                o = (jnp.dot(p_loc.astype(BF16), vu, preferred_element_type=F32)
                     + jnp.dot(p_ctx.astype(BF16), vc, preferred_element_type=F32)) / denom
                out = jnp.where(mine, o, out)
            o_ref[qrows, ls] = out.astype(o_ref.dtype)


def _na_attention(na_qkv, na_kv_ctx, bias_tables, batch, t_len, ctx_len):
    rows = t_len // GRID_W
    n_rb = rows // NA_ROW_BLOCK
    per = NA_BLOCKS_PER_STEP
    tq = per * NA_ROW_BLOCK * GRID_W
    qkv = na_qkv.reshape(batch, t_len, 3 * NA_WIDTH)
    kvc = na_kv_ctx.reshape(batch, ctx_len, 2 * NA_WIDTH)

    def bias_spec(sub):
        def idx(b, r):
            rb = r * per + sub
            return (jnp.where(rb == 0, 0, jnp.where(rb == n_rb - 1, 2, 1)), 0, 0, 0)
        return pl.BlockSpec((None,) + bias_tables.shape[1:], idx)

    return pl.pallas_call(
        functools.partial(_na_kernel, rows),
        out_shape=jax.ShapeDtypeStruct((batch, t_len, NA_WIDTH), BF16),
        grid=(batch, n_rb // per),
        in_specs=[pl.BlockSpec((None, tq, NA_WIDTH), lambda b, r: (b, r, 0)),
                  pl.BlockSpec((None, t_len, NA_WIDTH), lambda b, r: (b, 0, 1)),
                  pl.BlockSpec((None, t_len, NA_WIDTH), lambda b, r: (b, 0, 2)),
                  pl.BlockSpec((None, ctx_len, NA_WIDTH), lambda b, r: (b, 0, 0)),
                  pl.BlockSpec((None, ctx_len, NA_WIDTH), lambda b, r: (b, 0, 1))]
                 + [bias_spec(sub) for sub in range(per)],
        out_specs=pl.BlockSpec((None, tq, NA_WIDTH), lambda b, r: (b, r, 0)),
        compiler_params=_params(2, vmem=NA_VMEM_LIMIT),
        name="na_attention",
    )(qkv, qkv, qkv, kvc, kvc, *([bias_tables] * per))


def _gla_constants(c):
    tris, masks = [], []
    for reverse in (False, True):
        i = np.arange(c)[:, None]
        j = np.arange(c)[None, :]
        tris.append((j >= i) if reverse else (j <= i))
        i = np.arange(c // 2)[:, None]
        j = np.arange(c // 2)[None, :]
        if reverse:
            i, j = j, i
        level = []
        s = c // 4
        while s >= GLA_DIAG:
            level.append(((i // (2 * s)) == (j // (2 * s))) & ((i % (2 * s)) >= s) & ((j % (2 * s)) < s))
            s //= 2
        level.append(((i // GLA_DIAG) == (j // GLA_DIAG)) & (j <= i))
        masks.append(np.stack(level))
    return jnp.asarray(np.stack(tris), BF16), jnp.asarray(np.stack(masks), F32)


def _block_refs(cum, s, reverse, diag):
    c = cum.shape[0]
    span = s if diag else 2 * s
    parts = []
    for p in range(c // span):
        if diag:
            r = p * span + (span - 1 if reverse else 0)
        else:
            r = p * span + (s - 1 if reverse else s)
        parts.append(jnp.broadcast_to(cum[r:r + 1, :], (span, cum.shape[1])))
    return jnp.concatenate(parts, axis=0)


def _cumsum_rows(a_hl, tri):
    parts = jnp.dot(tri, a_hl, preferred_element_type=F32)
    w = a_hl.shape[1] // 2
    return parts[:, :w] + parts[:, w:]


def _gla_chunk(q, k, v, a, state_t, tri, mask_ref, reverse, want_out):
    c = k.shape[0]
    hc = c // 2
    nt = (((1,), (1,)), ((), ()))
    k = k.astype(F32)
    q = q.astype(F32) if want_out else None
    cum = _cumsum_rows(a, tri)
    last = 0 if reverse else c - 1
    total = cum[last:last + 1, :]

    def scale(x, log2_factor):
        return (x * jnp.exp2(log2_factor)).astype(BF16)

    out = None
    if want_out:
        out = lax.dot_general(scale(q, cum), state_t.astype(BF16), nt, preferred_element_type=F32)
        halves = (slice(hc, c), slice(0, hc)) if reverse else (slice(0, hc), slice(hc, c))
        early, late = halves
        r = hc - 1 if reverse else hc
        g = cum[r:r + 1, :]
        cross = lax.dot_general(scale(q[late], cum[late] - g), scale(k[early], g - cum[early]), nt,
                                preferred_element_type=F32)
        inner = []
        for rows in halves:
            cx, qx, kx = cum[rows], q[rows], k[rows]
            acc = jnp.zeros((hc, hc), F32)
            s = hc // 2
            level = 0
            while True:
                diag = s < GLA_DIAG
                d = cx - _block_refs(cx, GLA_DIAG if diag else s, reverse, diag)
                if diag:
                    qs, ks = scale(qx, d), scale(kx, -d)
                else:
                    e = jnp.exp2(-jnp.abs(d))
                    qs, ks = (qx * e).astype(BF16), (kx * e).astype(BF16)
                acc = acc + lax.dot_general(qs, ks, nt, preferred_element_type=F32) * mask_ref[level]
                if diag:
                    break
                s //= 2
                level += 1
            inner.append(acc)
        out_early = jnp.dot(inner[0].astype(BF16), v[early], preferred_element_type=F32)
        out_late = jnp.dot(jnp.concatenate([cross, inner[1]], axis=1).astype(BF16),
                           jnp.concatenate([v[early], v[late]], axis=0), preferred_element_type=F32)
        intra = [out_late, out_early] if reverse else [out_early, out_late]
        out = out + jnp.concatenate(intra, axis=0)
    upd = lax.dot_general(v, scale(k, total - cum), (((0,), (0,)), ((), ())), preferred_element_type=F32)
    return out, jnp.exp2(total) * state_t + upd


def _gla_kernel(qf_ref, kf_ref, vf_ref, af_ref, qb_ref, kb_ref, vb_ref, ab_ref,
                kc_ref, vc_ref, acf_ref, acb_ref, tri_ref, mask_ref,
                of_ref, ob_ref, sf_ref, sb_ref):
    c = pl.program_id(2)
    fm, bm = mask_ref.at[0], mask_ref.at[1]
    heads = sf_ref.shape[0]

    @pl.when(c == 0)
    def _():
        zero = jnp.zeros(sf_ref.shape[1:], F32)
        for h in range(heads):
            kc = kc_ref[:, h * GLA_DK:(h + 1) * GLA_DK]
            vc = vc_ref[:, h * GLA_DV:(h + 1) * GLA_DV]
            hl = slice(2 * h * GLA_DK, 2 * (h + 1) * GLA_DK)
            _, sf_ref[h] = _gla_chunk(None, kc, vc, acf_ref[:, hl], zero, tri_ref[0], fm, False, False)
            _, sb_ref[h] = _gla_chunk(None, kc, vc, acb_ref[:, hl], zero, tri_ref[1], bm, True, False)

    @pl.when(c > 0)
    def _():
        for h in range(heads):
            ks = slice(h * GLA_DK, (h + 1) * GLA_DK)
            vs = slice(h * GLA_DV, (h + 1) * GLA_DV)
            hl = slice(2 * h * GLA_DK, 2 * (h + 1) * GLA_DK)
            o, sf_ref[h] = _gla_chunk(qf_ref[:, ks], kf_ref[:, ks], vf_ref[:, vs], af_ref[:, hl], sf_ref[h],
                                      tri_ref[0], fm, False, True)
            of_ref[:, vs] = o.astype(of_ref.dtype)
            o, sb_ref[h] = _gla_chunk(qb_ref[:, ks], kb_ref[:, ks], vb_ref[:, vs], ab_ref[:, hl], sb_ref[h],
                                      tri_ref[1], bm, True, True)
            ob_ref[:, vs] = o.astype(ob_ref.dtype)


def _gla(gla_qk, vb, decay, k_ctx, v_ctx, decay_ctx, batch, t_len, ctx_len):
    nc = t_len // GLA_CHUNK
    h = GLA_HEADS
    qk = gla_qk.reshape(batch, t_len, 2 * GLA_QK_WIDTH)
    v3 = vb.reshape(batch, t_len, GLA_V_WIDTH)
    a3 = decay.reshape(batch, t_len, GLA_DECAY_WIDTH)
    kc3 = k_ctx.reshape(batch, ctx_len, GLA_QK_WIDTH)
    vc3 = v_ctx.reshape(batch, ctx_len, GLA_V_WIDTH)
    ac3 = decay_ctx.reshape(batch, ctx_len, GLA_DECAY_WIDTH)
    tri, masks = _gla_constants(GLA_CHUNK)

    def fwd(c):
        return jnp.maximum(c - 1, 0)

    def bwd(c):
        return nc - 1 - jnp.maximum(c - 1, 0)

    def const(a):
        return pl.BlockSpec(a.shape, lambda b, hh, c: (0,) * a.ndim, pipeline_mode=pl.Buffered(1))

    hp = GLA_HEADS_PER_STEP
    groups = h // hp
    cq = (None, GLA_CHUNK, hp * GLA_DK)
    cv = (None, GLA_CHUNK, hp * GLA_DV)
    ca = (None, GLA_CHUNK, hp * 2 * GLA_DK)
    cca = (None, ctx_len, hp * 2 * GLA_DK)
    in_specs = [
        pl.BlockSpec(cq, lambda b, g, c: (b, fwd(c), g)),
        pl.BlockSpec(cq, lambda b, g, c: (b, fwd(c), groups + g)),
        pl.BlockSpec(cv, lambda b, g, c: (b, fwd(c), g)),
        pl.BlockSpec(ca, lambda b, g, c: (b, fwd(c), g)),
        pl.BlockSpec(cq, lambda b, g, c: (b, bwd(c), g)),
        pl.BlockSpec(cq, lambda b, g, c: (b, bwd(c), groups + g)),
        pl.BlockSpec(cv, lambda b, g, c: (b, bwd(c), g)),
        pl.BlockSpec(ca, lambda b, g, c: (b, bwd(c), groups + g)),
        pl.BlockSpec((None, ctx_len, hp * GLA_DK), lambda b, g, c: (b, 0, g)),
        pl.BlockSpec((None, ctx_len, hp * GLA_DV), lambda b, g, c: (b, 0, g)),
        pl.BlockSpec(cca, lambda b, g, c: (b, 0, g)),
        pl.BlockSpec(cca, lambda b, g, c: (b, 0, groups + g)),
        const(tri), const(masks),
    ]
    out_specs = [pl.BlockSpec(cv, lambda b, g, c: (b, fwd(c), g)),
                 pl.BlockSpec(cv, lambda b, g, c: (b, bwd(c), g))]
    out_shape = [jax.ShapeDtypeStruct((batch, t_len, GLA_V_WIDTH), BF16)] * 2
    return pl.pallas_call(
        _gla_kernel,
        out_shape=out_shape,
        grid=(batch, groups, nc + 1),
        in_specs=in_specs,
        out_specs=out_specs,
        scratch_shapes=[pltpu.VMEM((hp, GLA_DV, GLA_DK), F32), pltpu.VMEM((hp, GLA_DV, GLA_DK), F32)],
        compiler_params=_params(3),
        name="gla",
    )(qk, qk, v3, a3, qk, qk, v3, a3, kc3, vc3, ac3, ac3, tri, masks)


def _decay_weights(w_f, b_f, w_b, b_b):
    r, width = w_f.shape
    w2 = jnp.zeros((LANES, 2 * width), F32).at[:r, :width].set(w_f).at[r:2 * r, width:].set(w_b)
    return w2.astype(BF16), jnp.concatenate([b_f, b_b]).reshape(1, 2 * width)


def _layer_norm(v, g, b):
    mu = jnp.mean(v, axis=1, keepdims=True)
    var = jnp.mean(jnp.square(v - mu), axis=1, keepdims=True)
    return (v - mu) * lax.rsqrt(var + EPS) * g + b


def _merge_kernel(oa_ref, of_ref, ob_ref, gb_ref, gates_ref, x_ref, g1_ref, sc2_ref, sh2_ref,
                  wa_ref, wb_ref, wo_ref, nw_ref, lg_ref, lb_ref, rw_ref,
                  x1_ref, h2_ref, sc_ref):
    d = x_ref.shape[1]
    o = of_ref[...].astype(F32) + ob_ref[...].astype(F32)
    pieces = []
    for hh in range(GLA_HEADS):
        oh = o[:, hh * GLA_DV:(hh + 1) * GLA_DV]
        pieces.append(oh * lax.rsqrt(jnp.mean(jnp.square(oh), axis=1, keepdims=True) + EPS))
    out_b = jnp.concatenate(pieces, axis=1) * nw_ref[...] * _silu(gb_ref[...].astype(F32))
    ya = jnp.dot(oa_ref[...], wa_ref[...], preferred_element_type=F32)
    yb = jnp.dot(out_b.astype(BF16), wb_ref[...], preferred_element_type=F32)
    y = _sigmoid(gates_ref[:, :d].astype(F32)) * ya + _sigmoid(gates_ref[:, d:].astype(F32)) * yb
    y2 = jnp.dot(y.astype(BF16), wo_ref[...], preferred_element_type=F32)
    x1 = _layer_norm(DEEPNORM_ALPHA * x_ref[...] + g1_ref[0] * y2, lg_ref[...], lb_ref[...])
    x1_ref[...] = x1
    h2 = x1 * (1.0 + sc2_ref[0]) + sh2_ref[0]
    h2_ref[...] = _pack_pairs(h2)
    logits_t = lax.dot_general(rw_ref[...], h2.astype(BF16), (((1,), (1,)), ((), ())), preferred_element_type=F32)
    sc_ref[...] = _sigmoid(logits_t)


def _merge(out_a, o_f, o_b, gb, gates, x2d, g1, sc2, sh2, wa, wb, wo, nw, lg, lb, rw_t, tile, tiles_per_batch):
    n, d = x2d.shape
    row = lambda i: (i, 0)
    mod = lambda i: (i // tiles_per_batch, 0, 0)
    full = lambda i: (0, 0)

    def const(a):
        return pl.BlockSpec(a.shape, full, pipeline_mode=pl.Buffered(1))

    in_specs = [pl.BlockSpec((tile, NA_WIDTH), row), pl.BlockSpec((tile, GLA_V_WIDTH), row),
                pl.BlockSpec((tile, GLA_V_WIDTH), row), pl.BlockSpec((tile, GLA_V_WIDTH), row),
                pl.BlockSpec((tile, 2 * d), row), pl.BlockSpec((tile, d), row),
                pl.BlockSpec((1, 1, d), mod), pl.BlockSpec((1, 1, d), mod), pl.BlockSpec((1, 1, d), mod),
                const(wa), const(wb), const(wo), const(nw), const(lg), const(lb), const(rw_t)]
    out_shape = [jax.ShapeDtypeStruct((n, d), F32), jax.ShapeDtypeStruct((n, d // 2), U32),
                 jax.ShapeDtypeStruct((N_EXPERTS, n), F32)]
    out_specs = [pl.BlockSpec((tile, d), row), pl.BlockSpec((tile, d // 2), row),
                 pl.BlockSpec((N_EXPERTS, tile), lambda i: (0, i))]
    return pl.pallas_call(
        _merge_kernel, out_shape=out_shape, grid=(n // tile,), in_specs=in_specs, out_specs=out_specs,
        compiler_params=_params(1), name="merge_ln1_router",
    )(out_a, o_f, o_b, gb, gates, x2d, g1, sc2, sh2, wa, wb, wo, nw, lg, lb, rw_t)


def _first_argmax(vals, idx, n):
    m = jnp.max(vals, axis=0, keepdims=True)
    first = jnp.min(jnp.where(vals == m, idx, float(n)), axis=0, keepdims=True)
    return m, first


def _route_kernel(sc_ref, bias_ref, before_ref, e_ref, w_ref, rank_ref, cnt_ref, carry_ref):
    step = pl.program_id(0)
    tr = sc_ref.shape[1]

    @pl.when(step == 0)
    def _():
        carry_ref[...] = jnp.zeros(carry_ref.shape, F32)

    scores = sc_ref[...]
    biased = scores + bias_ref[...]
    eidx = lax.broadcasted_iota(jnp.int32, (N_EXPERTS, tr), 0).astype(F32)
    lidx = lax.broadcasted_iota(jnp.int32, (GROUP_SIZE, tr), 0).astype(F32)
    gidx = lax.broadcasted_iota(jnp.int32, (N_GROUPS, tr), 0).astype(F32)
    gs = []
    for g in range(N_GROUPS):
        blk = biased[g * GROUP_SIZE:(g + 1) * GROUP_SIZE]
        m1, first = _first_argmax(blk, lidx, GROUP_SIZE)
        m2 = jnp.max(jnp.where(lidx == first, -jnp.inf, blk), axis=0, keepdims=True)
        gs.append(m1 + m2)
    cur = jnp.concatenate(gs, axis=0)
    keep = jnp.zeros((N_GROUPS, tr), F32)
    for _ in range(TOPK_GROUPS):
        _, first = _first_argmax(cur, gidx, N_GROUPS)
        sel = gidx == first
        keep = jnp.where(sel, 1.0, keep)
        cur = jnp.where(sel, -jnp.inf, cur)
    keep_e = jnp.concatenate([jnp.broadcast_to(keep[g:g + 1], (GROUP_SIZE, tr)) for g in range(N_GROUPS)], axis=0)
    masked = jnp.where(keep_e > 0.5, biased, -jnp.inf)
    chosen = jnp.zeros((N_EXPERTS, tr), F32)
    tops, topi = [], []
    for _ in range(TOP_K):
        _, first = _first_argmax(masked, eidx, N_EXPERTS)
        sel = eidx == first
        tops.append(jnp.sum(jnp.where(sel, scores, 0.0), axis=0, keepdims=True))
        topi.append(first)
        chosen = jnp.where(sel, 1.0, chosen)
        masked = jnp.where(sel, -jnp.inf, masked)
    top_s = jnp.concatenate(tops, axis=0)
    top_i = jnp.concatenate(topi, axis=0)
    e_ref[...] = top_i.astype(jnp.int32)
    w_ref[...] = top_s / jnp.sum(top_s, axis=0, keepdims=True) * ROUTED_SCALE
    prior = jnp.dot(chosen.astype(BF16), before_ref[...], preferred_element_type=F32) + carry_ref[...]
    ranks = [jnp.sum(jnp.where(eidx == topi[kk], prior, 0.0), axis=0, keepdims=True) for kk in range(TOP_K)]
    rank_ref[...] = jnp.concatenate(ranks, axis=0).astype(jnp.int32)
    carry_ref[...] = carry_ref[...] + jnp.sum(chosen, axis=1, keepdims=True)
    cnt_ref[...] = jnp.broadcast_to(carry_ref[...], cnt_ref.shape).astype(jnp.int32)


def _route(scores_t, router_bias, tile):
    n = scores_t.shape[1]
    col = lambda i: (0, i)
    out_shape = [jax.ShapeDtypeStruct((TOP_K, n), jnp.int32), jax.ShapeDtypeStruct((TOP_K, n), F32),
                 jax.ShapeDtypeStruct((TOP_K, n), jnp.int32), jax.ShapeDtypeStruct((N_EXPERTS, LANES), jnp.int32)]
    before = jnp.asarray(np.arange(tile)[:, None] < np.arange(tile)[None, :], BF16)
    return pl.pallas_call(
        _route_kernel, out_shape=out_shape, grid=(n // tile,),
        in_specs=[pl.BlockSpec((N_EXPERTS, tile), col), pl.BlockSpec((N_EXPERTS, 1), lambda i: (0, 0)),
                  pl.BlockSpec((tile, tile), lambda i: (0, 0), pipeline_mode=pl.Buffered(1))],
        out_specs=[pl.BlockSpec((TOP_K, tile), col), pl.BlockSpec((TOP_K, tile), col),
                   pl.BlockSpec((TOP_K, tile), col), pl.BlockSpec((N_EXPERTS, LANES), lambda i: (0, 0))],
        scratch_shapes=[pltpu.VMEM((N_EXPERTS, 1), F32)],
        compiler_params=_params(1), name="route",
    )(scores_t, router_bias.reshape(N_EXPERTS, 1), before)


def _slots_kernel(e_ref, rank_ref, start_ref, dest_ref):
    tr = e_ref.shape[1]
    eidx = lax.broadcasted_iota(jnp.int32, (N_EXPERTS, tr), 0)
    e = e_ref[...]
    start = start_ref[...]
    rows = [jnp.sum(jnp.where(eidx == e[kk:kk + 1], start, 0.0), axis=0, keepdims=True) for kk in range(TOP_K)]
    dest_ref[...] = jnp.concatenate(rows, axis=0).astype(jnp.int32) + rank_ref[...]


def _slots(top_e, rank, start_rows, tile):
    n = top_e.shape[1]
    col = lambda i: (0, i)
    return pl.pallas_call(
        _slots_kernel, out_shape=jax.ShapeDtypeStruct((TOP_K, n), jnp.int32), grid=(n // tile,),
        in_specs=[pl.BlockSpec((TOP_K, tile), col), pl.BlockSpec((TOP_K, tile), col),
                  pl.BlockSpec((N_EXPERTS, 1), lambda i: (0, 0))],
        out_specs=pl.BlockSpec((TOP_K, tile), col),
        compiler_params=_params(1), name="slots",
    )(top_e, rank, start_rows)


def _sc_mesh():
    return plsc.VectorSubcoreMesh(core_axis_name="c", subcore_axis_name="s",
                                  num_cores=SC_CORES, num_subcores=SC_SUBCORES)


def _sc_scatter_rows(rows, idx, n_out):
    n, dp = rows.shape
    copies = idx.shape[0] // n
    per_worker = n // (SC_CORES * SC_SUBCORES)
    chunk = SC_GATHER_ROWS

    @functools.partial(
        pl.kernel, mesh=_sc_mesh(), out_type=jax.ShapeDtypeStruct((n_out, dp), rows.dtype),
        scratch_types=[pltpu.VMEM((chunk,), jnp.int32), pltpu.VMEM((chunk, dp), rows.dtype)],
        name="sc_scatter_rows")
    def scatter(rows_hbm, idx_hbm, out_hbm, idx_v, rows_v):
        base = (lax.axis_index("s") * SC_CORES + lax.axis_index("c")) * per_worker

        @pl.loop(0, per_worker // chunk)
        def _(j):
            off = base + j * chunk
            pltpu.sync_copy(rows_hbm.at[pl.ds(off, chunk)], rows_v)
            for k in range(copies):
                pltpu.sync_copy(idx_hbm.at[pl.ds(k * n + off, chunk)], idx_v)
                pltpu.sync_copy(rows_v, out_hbm.at[idx_v])

    return scatter(rows, idx)


def _expert_kernel(start_ref, cnt_ref, nused_ref, xs_ref, wg_ref, wu_ref, wd_ref, y_ref,
                   wgb, wub, wdb, xbuf, ybuf, sem_in, sem_out):
    e = pl.program_id(0)
    bm, ring = EXPERT_BLOCK, EXPERT_RING
    cnt = cnt_ref[e]
    nb = (cnt + bm - 1) // bm
    g0 = start_ref[e]
    n_used = nused_ref[0]

    def x_copy(g):
        slot = g % ring
        return pltpu.make_async_copy(xs_ref.at[pl.ds(g * bm, bm)], xbuf.at[slot], sem_in.at[slot])

    def y_copy(g):
        slot = g % ring
        return pltpu.make_async_copy(ybuf.at[slot], y_ref.at[pl.ds(g * bm, bm)], sem_out.at[slot])

    @pl.when(e == 0)
    def _():
        for g in range(ring):
            @pl.when(g < n_used)
            def _():
                x_copy(g).start()

    wgb[...] = wg_ref[...].astype(BF16)
    wub[...] = wu_ref[...].astype(BF16)
    wdb[...] = wd_ref[...].astype(BF16)
    row = lax.broadcasted_iota(jnp.int32, (bm, xbuf.shape[2]), 0)

    def blocks(j, count):
        for b in range(count):
            g = g0 + j + b
            x_copy(g).wait()

            @pl.when(g >= ring)
            def _():
                y_copy(g - ring).wait()

        for b in range(count):
            slot = (g0 + j + b) % ring
            x = _unpack_pairs(jnp.where(row < cnt - (j + b) * bm, xbuf[slot], jnp.uint32(0))).astype(BF16)
            gate = jnp.dot(x, wgb[...], preferred_element_type=F32)
            up = jnp.dot(x, wub[...], preferred_element_type=F32)
            ybuf[slot] = _pack_pairs(
                jnp.dot((_silu(gate) * up).astype(BF16), wdb[...], preferred_element_type=F32))

        for b in range(count):
            g = g0 + j + b
            y_copy(g).start()

            @pl.when(g + ring < n_used)
            def _():
                x_copy(g + ring).start()

    group = EXPERT_GROUP

    def full_group(p, carry):
        blocks(group * p, group)
        return carry

    lax.fori_loop(0, nb // group, full_group, 0)
    done = (nb // group) * group
    size = group // 2
    while size >= 1:
        @pl.when((nb - done) % (2 * size) >= size)
        def _(done=done, size=size):
            blocks(done, size)

        done = done + jnp.where((nb - done) % (2 * size) >= size, size, 0)
        size //= 2

    @pl.when(e == pl.num_programs(0) - 1)
    def _():
        for back in range(1, ring + 1):
            @pl.when(n_used - back >= 0)
            def _():
                y_copy(n_used - back).wait()


def _experts(blk_start, counts, n_used, xs, wg, wu, wd):
    n_slots, dp = xs.shape
    n_exp, d, ff = wg.shape
    bm, ring = EXPERT_BLOCK, EXPERT_RING
    wsel = lambda e, st, ct, nu: (e, 0, 0)
    grid_spec = pltpu.PrefetchScalarGridSpec(
        num_scalar_prefetch=3, grid=(n_exp,),
        in_specs=[pl.BlockSpec(memory_space=pl.ANY),
                  pl.BlockSpec((None, d, ff), wsel), pl.BlockSpec((None, d, ff), wsel),
                  pl.BlockSpec((None, ff, d), wsel)],
        out_specs=pl.BlockSpec(memory_space=pl.ANY),
        scratch_shapes=[pltpu.VMEM((d, ff), BF16), pltpu.VMEM((d, ff), BF16), pltpu.VMEM((ff, d), BF16),
                        pltpu.VMEM((ring, bm, dp), U32), pltpu.VMEM((ring, bm, dp), U32),
                        pltpu.SemaphoreType.DMA((ring,)), pltpu.SemaphoreType.DMA((ring,))])
    return pl.pallas_call(
        _expert_kernel, out_shape=jax.ShapeDtypeStruct((n_slots, dp), U32), grid_spec=grid_spec,
        compiler_params=_params(1), name="experts",
    )(blk_start, counts, n_used, xs, wg, wu, wd)


def _sc_gather_rows(table, idx):
    n_idx = idx.shape[0]
    dp = table.shape[1]
    workers = SC_CORES * SC_SUBCORES
    per_worker = n_idx // workers
    chunk = SC_GATHER_ROWS

    @functools.partial(
        pl.kernel, mesh=_sc_mesh(), out_type=jax.ShapeDtypeStruct((n_idx, dp), table.dtype),
        scratch_types=[pltpu.VMEM((chunk,), jnp.int32), pltpu.VMEM((chunk, dp), table.dtype),
                       pltpu.SemaphoreType.DMA],
        name="sc_gather_rows")
    def gather(table_hbm, idx_hbm, out_hbm, idx_v, rows_v, sem):
        base = (lax.axis_index("s") * SC_CORES + lax.axis_index("c")) * per_worker

        @pl.loop(0, per_worker // chunk)
        def _(j):
            off = base + j * chunk
            pltpu.sync_copy(idx_hbm.at[pl.ds(off, chunk)], idx_v)
            pltpu.async_copy(table_hbm.at[idx_v], rows_v, sem).wait()
            pltpu.sync_copy(rows_v, out_hbm.at[pl.ds(off, chunk)])

    return gather(table, idx)


def _combine_kernel(tile, yt_ref, w_ref, h_ref, x1_ref, g2_ref, sg_ref, su_ref, sd_ref, lg_ref, lb_ref, *rest):
    o_ref = rest[-1]
    hb = _unpack_pairs(h_ref[...]).astype(BF16)
    g = jnp.dot(hb, sg_ref[...], preferred_element_type=F32)
    u = jnp.dot(hb, su_ref[...], preferred_element_type=F32)
    f = jnp.dot((_silu(g) * u).astype(BF16), sd_ref[...], preferred_element_type=F32)
    w = w_ref[...]
    for kk in range(TOP_K):
        f = f + w[:, kk:kk + 1] * _unpack_pairs(yt_ref[kk * tile:(kk + 1) * tile, :])
    o_ref[...] = _layer_norm(DEEPNORM_ALPHA * x1_ref[...] + g2_ref[0] * f, lg_ref[...], lb_ref[...])


def _combine(y_tok, top_w, h2p, x1, g2, sg, su, sd, lg, lb, tile, tiles_per_batch, first_tile, prev_out):
    n, d = x1.shape
    dp = h2p.shape[1]
    row = lambda i: (first_tile + i, 0)
    full = lambda i: (0, 0)

    def const(a):
        return pl.BlockSpec(a.shape, full, pipeline_mode=pl.Buffered(1))

    in_specs = [pl.BlockSpec((TOP_K * tile, dp), lambda i: (i, 0)),
                pl.BlockSpec((tile, TOP_K), row), pl.BlockSpec((tile, dp), row), pl.BlockSpec((tile, d), row),
                pl.BlockSpec((1, 1, d), lambda i: ((first_tile + i) // tiles_per_batch, 0, 0)),
                const(sg), const(su), const(sd), const(lg), const(lb)]
    args = [y_tok, top_w, h2p, x1, g2, sg, su, sd, lg, lb]
    aliases = {}
    if prev_out is not None:
        in_specs.append(pl.BlockSpec(memory_space=pl.ANY))
        args.append(prev_out)
        aliases = {len(args) - 1: 0}
    return pl.pallas_call(
        functools.partial(_combine_kernel, tile),
        out_shape=jax.ShapeDtypeStruct((n, d), F32),
        grid=(y_tok.shape[0] // (TOP_K * tile),),
        in_specs=in_specs,
        out_specs=pl.BlockSpec((tile, d), row),
        input_output_aliases=aliases,
        compiler_params=_params(1), name="combine_shared_ln2",
    )(*args)


def kernel(x, c, ctx, c_ctx, w_mod, b_mod, w_in, na_rpb, gla_w_decay_f, gla_b_decay_f, gla_w_decay_b, gla_b_decay_b,
           gla_norm_w, w_branch_a, w_branch_b, w_out, ln1_g, ln1_b, router_w, router_bias, exp_w_gate, exp_w_up,
           exp_w_down, sh_w_gate, sh_w_up, sh_w_down, ln2_g, ln2_b):
    batch, t_len, d = x.shape
    ctx_len = ctx.shape[1]
    n = batch * t_len
    assert w_mod.shape[0] == DEPTH == 1
    assert t_len % GLA_CHUNK == 0 and ctx_len == GLA_CHUNK and (t_len // GRID_W) % (NA_ROW_BLOCK * NA_BLOCKS_PER_STEP) == 0

    mod_rows = 16
    c_all = jnp.zeros((mod_rows, d), F32).at[:batch].set(c).at[batch].set(c_ctx)
    mod = _modulation(c_all, w_mod[0], b_mod[0])
    sh1, sc1, g1, sh2, sc2, g2 = [mod[:batch, j * d:(j + 1) * d].reshape(batch, 1, d) for j in range(6)]
    sh1c = mod[batch:batch + 1, 0:d].reshape(1, 1, d)
    sc1c = mod[batch:batch + 1, d:2 * d].reshape(1, 1, d)

    offs = np.cumsum((0, NA_WIDTH, NA_WIDTH, NA_WIDTH, GLA_QK_WIDTH, GLA_QK_WIDTH, GLA_V_WIDTH, GLA_V_WIDTH,
                      GLA_GATE_RANK, GLA_GATE_RANK, d, d))
    qa, ka, va, qb, kb, vbc, gbc, lrf, lrb, ga, gbt = [w_in[0][:, offs[j]:offs[j + 1]] for j in range(11)]
    lr_cols = jnp.concatenate([lrf, lrb, jnp.zeros((d, LANES - 2 * GLA_GATE_RANK), F32)], axis=1)
    w_lat = jnp.concatenate([lr_cols, qa, ka, va, qb, kb, vbc, gbc, ga, gbt], axis=1).astype(BF16)
    lat_offs = np.cumsum((0, LANES, NA_WIDTH, NA_WIDTH, NA_WIDTH, GLA_QK_WIDTH, GLA_QK_WIDTH, GLA_V_WIDTH))
    w_ctx = jnp.concatenate([w_lat[:, lat_offs[j]:lat_offs[j + 1]] for j in (0, 2, 3, 5, 6)], axis=1)
    plain = ("plain",)
    lat_plan = ((LANES, (("decay",),)),
                (3 * NA_WIDTH, (("scale", NA_HEAD_DIM ** -0.5), plain, plain)),
                (2 * GLA_QK_WIDTH, (("rope", GLA_DK ** -0.5), ("rope", 1.0))),
                (GLA_V_WIDTH, (plain, plain)), (GLA_V_WIDTH, (plain, plain)),
                (2 * d, (plain,) * 4))
    ctx_plan = ((LANES, (("decay",),)),
                (2 * NA_WIDTH, (plain, plain)), (GLA_QK_WIDTH, (plain,)), (GLA_V_WIDTH, (plain, plain)))
    w2, b2 = _decay_weights(gla_w_decay_f[0], gla_b_decay_f[0], gla_w_decay_b[0], gla_b_decay_b[0])
    tile = COMBINE_TILE
    x2d = x.reshape(n, d)
    decay, na_qkv, gla_qk, vb, gb, gates = _projection(
        x2d, sc1, sh1, w_lat, w2, b2, lat_plan, (BF16,) * 6, PROJ_TILE, t_len // PROJ_TILE,
        rope=_rope_tables(t_len))
    decay_c, na_kv_c, k_c, v_c = _projection(
        ctx.reshape(batch * ctx_len, d), sc1c, sh1c, w_ctx, w2, b2, ctx_plan, (BF16,) * 4, tile,
        batch * ctx_len // tile)

    out_a = _na_attention(na_qkv, na_kv_c, _na_bias_tables(na_rpb[0]), batch, t_len, ctx_len)
    o_f, o_b = _gla(gla_qk, vb, decay, k_c, v_c, decay_c, batch, t_len, ctx_len)

    x1, h2p, scores_t = _merge(
        out_a.reshape(n, NA_WIDTH), o_f.reshape(n, GLA_V_WIDTH), o_b.reshape(n, GLA_V_WIDTH), gb, gates, x2d,
        g1, sc2, sh2, w_branch_a[0].astype(BF16), w_branch_b[0].astype(BF16), w_out[0].astype(BF16),
        gla_norm_w[0].reshape(1, -1), ln1_g[0].reshape(1, d), ln1_b[0].reshape(1, d),
        router_w[0].T.astype(BF16), PROJ_TILE, t_len // PROJ_TILE)

    top_e, top_w, rank, counts = _route(scores_t, router_bias[0], 512)

    counts = counts[:, 0]
    n_blocks = n * TOP_K // EXPERT_BLOCK + N_EXPERTS
    blocks_per = (counts + EXPERT_BLOCK - 1) // EXPERT_BLOCK
    blk_end = jnp.cumsum(blocks_per)
    blk_start = blk_end - blocks_per
    dest = _slots(top_e, rank, (blk_start * EXPERT_BLOCK).astype(F32).reshape(N_EXPERTS, 1), 2048)

    xs = _sc_scatter_rows(h2p, dest.reshape(TOP_K * n), n_blocks * EXPERT_BLOCK)
    y = _experts(blk_start.astype(jnp.int32), counts, blk_end[-1:].astype(jnp.int32), xs,
                 exp_w_gate[0], exp_w_up[0], exp_w_down[0])
    dest_tok = dest.reshape(TOP_K, n // tile, tile).transpose(1, 0, 2).reshape(COMBINE_CHUNKS, -1)
    w_rows = top_w.T
    shared = (sh_w_gate[0].astype(BF16), sh_w_up[0].astype(BF16), sh_w_down[0].astype(BF16),
              ln2_g[0].reshape(1, d), ln2_b[0].reshape(1, d))
    tiles_per_chunk = n // tile // COMBINE_CHUNKS
    out = None
    for ci in range(COMBINE_CHUNKS):
        y_tok = _sc_gather_rows(y, dest_tok[ci])
        out = _combine(y_tok, w_rows, h2p, x1, g2, *shared, tile, t_len // tile, ci * tiles_per_chunk, out)
    return out.reshape(batch, t_len, d)
```

```python
import functools

import numpy as np
import jax
import jax.numpy as jnp
from jax import lax
from jax.experimental import pallas as pl
from jax.experimental.pallas import tpu as pltpu
from jax.experimental.pallas import tpu_sc as plsc

F32 = jnp.float32
BF16 = jnp.bfloat16
U32 = jnp.uint32
HIGHEST = lax.Precision.HIGHEST

GRID_W = 64
NA_HEADS = 8
NA_HEAD_DIM = 64
NA_WIDTH = NA_HEADS * NA_HEAD_DIM
WIN_ROWS = 8
WIN_COLS = 16
GLA_HEADS = 4
GLA_DK = 128
GLA_DV = 256
GLA_QK_WIDTH = GLA_HEADS * GLA_DK
GLA_V_WIDTH = GLA_HEADS * GLA_DV
GLA_GATE_RANK = 16
GLA_TAU = 16.0
LOG2E = 1.4426950408889634
ROPE_BASE = 10000.0
N_EXPERTS = 256
TOP_K = 8
N_GROUPS = 8
TOPK_GROUPS = 4
GROUP_SIZE = N_EXPERTS // N_GROUPS
ROUTED_SCALE = 2.5
DEPTH = 1
DEEPNORM_ALPHA = (2 * DEPTH) ** 0.25
EPS = 1e-6

LANES = 128
PROJ_TILE = 512
NA_ROW_BLOCK = 4
NA_BLOCKS_PER_STEP = 2
NA_UNION_ROWS = NA_ROW_BLOCK + WIN_ROWS - 1
GLA_CHUNK = 256
GLA_DIAG = 16
GLA_HEADS_PER_STEP = 4
EXPERT_BLOCK = 256
EXPERT_GROUP = 4
EXPERT_RING = 8
COMBINE_CHUNKS = 8
COMBINE_TILE = 512
SC_CORES = 2
SC_SUBCORES = 16
SC_GATHER_ROWS = 128
NEG_BIG = -1e30
VMEM_LIMIT = 56 * 1024 * 1024
NA_VMEM_LIMIT = 58 * 1024 * 1024


def _params(n_axes, vmem=VMEM_LIMIT):
    return pltpu.CompilerParams(dimension_semantics=("arbitrary",) * n_axes, vmem_limit_bytes=vmem)


def _sigmoid(v):
    return 1.0 / (1.0 + jnp.exp2(v * (-LOG2E)))


def _silu(v):
    return v * _sigmoid(v)


def _pack_pairs(v):
    m = v.shape[1] // 2
    lo = lax.bitcast_convert_type(v[:, :m].astype(BF16).astype(F32), U32) >> 16
    hi = lax.bitcast_convert_type(v[:, m:].astype(BF16).astype(F32), U32) & jnp.uint32(0xFFFF0000)
    return lo | hi


def _unpack_pairs(p):
    lo = lax.bitcast_convert_type(p << 16, F32)
    hi = lax.bitcast_convert_type(p & jnp.uint32(0xFFFF0000), F32)
    return jnp.concatenate([lo, hi], axis=1)


def _mod_kernel(c_ref, w_ref, b_ref, o_ref):
    o_ref[...] = jnp.dot(_silu(c_ref[...]), w_ref[...], preferred_element_type=F32, precision=HIGHEST) + b_ref[...]


def _modulation(c_all, w_mod, b_mod):
    rows, d = c_all.shape
    n = w_mod.shape[1]
    bn = 512
    return pl.pallas_call(
        _mod_kernel,
        out_shape=jax.ShapeDtypeStruct((rows, n), F32),
        grid=(n // bn,),
        in_specs=[pl.BlockSpec((rows, d), lambda j: (0, 0)),
                  pl.BlockSpec((d, bn), lambda j: (0, j)),
                  pl.BlockSpec((1, bn), lambda j: (0, j))],
        out_specs=pl.BlockSpec((rows, bn), lambda j: (0, j)),
        compiler_params=_params(1),
        name="modulation",
    )(c_all, w_mod, b_mod.reshape(1, n))


def _swap32(v):
    lane = lax.broadcasted_iota(jnp.int32, v.shape, 1)
    return jnp.where((lane % 64) < 32, pltpu.roll(v, 96, 1), pltpu.roll(v, 32, 1))


GLA_DECAY_WIDTH = 2 * GLA_HEADS * 2 * GLA_DK


def _log2_decay_split(lr, w2, b2, out_ref):
    z = (jnp.dot(lr.astype(BF16), w2, preferred_element_type=F32) + b2) * LOG2E
    a = (jnp.minimum(z, 0.0) - jnp.log2(1.0 + jnp.exp2(-jnp.abs(z)))) * (1.0 / GLA_TAU)
    hi = a.astype(BF16)
    lo = (a - hi.astype(F32)).astype(BF16)
    for p in range(a.shape[1] // GLA_DK):
        src = slice(p * GLA_DK, (p + 1) * GLA_DK)
        out_ref[:, 2 * p * GLA_DK:(2 * p + 1) * GLA_DK] = hi[:, src]
        out_ref[:, (2 * p + 1) * GLA_DK:(2 * p + 2) * GLA_DK] = lo[:, src]


def _proj_kernel(plan, has_rope, *refs):
    x_ref, sc_ref, sh_ref, w_ref, w2_ref, b2_ref = refs[:6]
    pos = 6
    if has_rope:
        cos_ref, sin_ref = refs[6:8]
        pos = 8
    out_refs = refs[pos:]
    h = (x_ref[...] * (1.0 + sc_ref[0]) + sh_ref[0]).astype(BF16)
    col = 0
    for out_ref, (width, kinds) in zip(out_refs, plan):
        for j, kind in enumerate(kinds):
            cw = width // len(kinds)
            c0 = j * cw
            acc = jnp.dot(h, w_ref[:, col + c0:col + c0 + cw], preferred_element_type=F32)
            if kind[0] == "decay":
                _log2_decay_split(acc, w2_ref[...], b2_ref[...], out_ref)
                continue
            if kind[0] == "scale":
                acc = acc * kind[1]
            elif kind[0] == "rope":
                cos, sin = cos_ref[...], sin_ref[...]
                pieces = []
                for p in range(cw // LANES):
                    v = acc[:, p * LANES:(p + 1) * LANES]
                    pieces.append((v * cos + _swap32(v) * sin) * kind[1])
                acc = jnp.concatenate(pieces, axis=1)
            out_ref[:, c0:c0 + cw] = acc.astype(out_ref.dtype)
        col += width


def _projection(x2d, sc, sh, w, w2, b2, plan, out_dtypes, tile, tiles_per_mod, rope=None):
    n, d = x2d.shape
    const = lambda a: pl.BlockSpec(a.shape, lambda i: (0, 0), pipeline_mode=pl.Buffered(1))
    in_specs = [pl.BlockSpec((tile, d), lambda i: (i, 0)),
                pl.BlockSpec((1, 1, d), lambda i: (i // tiles_per_mod, 0, 0)),
                pl.BlockSpec((1, 1, d), lambda i: (i // tiles_per_mod, 0, 0)),
                const(w), const(w2), const(b2)]
    args = [x2d, sc, sh, w, w2, b2]
    if rope is not None:
        in_specs += [pl.BlockSpec((tile, LANES), lambda i: (i % tiles_per_mod, 0))] * 2
        args += list(rope)
    widths = [GLA_DECAY_WIDTH if kinds[0][0] == "decay" else wd for wd, kinds in plan]
    out_shape = [jax.ShapeDtypeStruct((n, wd), dt) for wd, dt in zip(widths, out_dtypes)]
    out_specs = [pl.BlockSpec((tile, wd), lambda i: (i, 0)) for wd in widths]
    return pl.pallas_call(
        functools.partial(_proj_kernel, plan, rope is not None),
        out_shape=out_shape,
        grid=(n // tile,),
        in_specs=in_specs,
        out_specs=out_specs,
        compiler_params=_params(1),
        name="in_proj" if rope is not None else "ctx_proj",
    )(*args)


def _rope_tables(t_len):
    half = GLA_DK // 2
    quarter = half // 2
    f32 = np.float32
    inv_freq = f32(ROPE_BASE) ** (-np.arange(quarter, dtype=f32) / f32(quarter))
    pos = np.arange(t_len)
    row_ang = (pos // GRID_W).astype(f32)[:, None] * inv_freq[None, :]
    col_ang = (pos % GRID_W).astype(f32)[:, None] * inv_freq[None, :]
    cr, sr, cc, sn = np.cos(row_ang), np.sin(row_ang), np.cos(col_ang), np.sin(col_ang)
    cos = np.concatenate([cr, cr, cc, cc], axis=1).astype(f32)
    sin = np.concatenate([-sr, sr, -sn, sn], axis=1).astype(f32)
    return jnp.asarray(cos), jnp.asarray(sin)


def _na_bias_tables(rpb):
    rb, ur, w = NA_ROW_BLOCK, NA_UNION_ROWS, GRID_W
    heads = rpb.shape[0]
    pad = jnp.pad(rpb, ((0, 0), (0, 0), (w, w)))
    toep = jnp.stack([pad[:, :, w + WIN_COLS - 1 - c:2 * w + WIN_COLS - 1 - c] for c in range(w)], axis=2)
    c = np.arange(w)[:, None]
    kc = np.arange(w)[None, :]
    col_start = np.clip(c - WIN_COLS // 2, 0, w - WIN_COLS)
    col_ok = (kc >= col_start) & (kc < col_start + WIN_COLS)
    toep = jnp.where(col_ok[None, None], toep, NEG_BIG)
    neg = jnp.full((heads, w, w), NEG_BIG, F32)
    half = WIN_ROWS // 2
    tables = []
    for lo, off in ((lambda i: 0, WIN_ROWS - 1), (lambda i: i, WIN_ROWS - 1 - half), (lambda i: ur - WIN_ROWS, 0)):
        rows_ = []
        for i in range(rb):
            blocks = [toep[:, j - i + off] if lo(i) <= j < lo(i) + WIN_ROWS else neg for j in range(ur)]
            rows_.append(jnp.concatenate(blocks, axis=2))
        tables.append(jnp.concatenate(rows_, axis=1))
    return jnp.stack(tables)


def _na_kernel(rows, q_ref, k_ref, v_ref, kc_ref, vc_ref, *rest):
    bias_refs, o_ref = rest[:-1], rest[-1]
    tq = NA_ROW_BLOCK * GRID_W
    nk = NA_UNION_ROWS * GRID_W
    nt = (((1,), (1,)), ((), ()))
    lane = lax.broadcasted_iota(jnp.int32, (tq, LANES), 1)
    for sub, bias_ref in enumerate(bias_refs):
        rb = pl.program_id(1) * len(bias_refs) + sub
        ustart = jnp.clip(rb * NA_ROW_BLOCK - WIN_ROWS // 2, 0, rows - NA_UNION_ROWS)
        k0 = pl.multiple_of(ustart * GRID_W, GRID_W)
        qrows = slice(sub * tq, (sub + 1) * tq)
        for p in range(NA_WIDTH // LANES):
            ls = slice(p * LANES, (p + 1) * LANES)
            q = q_ref[qrows, ls]
            ku = k_ref[pl.ds(k0, nk), ls]
            vu = v_ref[pl.ds(k0, nk), ls]
            kc = kc_ref[:, ls]
            vc = vc_ref[:, ls]
            out = jnp.zeros(q.shape, F32)
            for hh in range(LANES // NA_HEAD_DIM):
                mine = (lane // NA_HEAD_DIM) == hh
                qm = jnp.where(mine, q, jnp.zeros_like(q))
                s_loc = lax.dot_general(qm, ku, nt, preferred_element_type=F32) + bias_ref[p * 2 + hh]
                s_ctx = lax.dot_general(qm, kc, nt, preferred_element_type=F32)
                m = jnp.maximum(jnp.max(s_loc, axis=1, keepdims=True), jnp.max(s_ctx, axis=1, keepdims=True))
                p_loc = jnp.exp(s_loc - m)
                p_ctx = jnp.exp(s_ctx - m)
                denom = jnp.sum(p_loc, axis=1, keepdims=True) + jnp.sum(p_ctx, axis=1, keepdims=True)
                o = (jnp.dot(p_loc.astype(BF16), vu, preferred_element_type=F32)
                     + jnp.dot(p_ctx.astype(BF16), vc, preferred_element_type=F32)) / denom
                out = jnp.where(mine, o, out)
            o_ref[qrows, ls] = out.astype(o_ref.dtype)


def _na_attention(na_qkv, na_kv_ctx, bias_tables, batch, t_len, ctx_len):
    rows = t_len // GRID_W
    n_rb = rows // NA_ROW_BLOCK
    per = NA_BLOCKS_PER_STEP
    tq = per * NA_ROW_BLOCK * GRID_W
    qkv = na_qkv.reshape(batch, t_len, 3 * NA_WIDTH)
    kvc = na_kv_ctx.reshape(batch, ctx_len, 2 * NA_WIDTH)

    def bias_spec(sub):
        def idx(b, r):
            rb = r * per + sub
            return (jnp.where(rb == 0, 0, jnp.where(rb == n_rb - 1, 2, 1)), 0, 0, 0)
        return pl.BlockSpec((None,) + bias_tables.shape[1:], idx)

    return pl.pallas_call(
        functools.partial(_na_kernel, rows),
        out_shape=jax.ShapeDtypeStruct((batch, t_len, NA_WIDTH), BF16),
        grid=(batch, n_rb // per),
        in_specs=[pl.BlockSpec((None, tq, NA_WIDTH), lambda b, r: (b, r, 0)),
                  pl.BlockSpec((None, t_len, NA_WIDTH), lambda b, r: (b, 0, 1)),
                  pl.BlockSpec((None, t_len, NA_WIDTH), lambda b, r: (b, 0, 2)),
                  pl.BlockSpec((None, ctx_len, NA_WIDTH), lambda b, r: (b, 0, 0)),
                  pl.BlockSpec((None, ctx_len, NA_WIDTH), lambda b, r: (b, 0, 1))]
                 + [bias_spec(sub) for sub in range(per)],
        out_specs=pl.BlockSpec((None, tq, NA_WIDTH), lambda b, r: (b, r, 0)),
        compiler_params=_params(2, vmem=NA_VMEM_LIMIT),
        name="na_attention",
    )(qkv, qkv, qkv, kvc, kvc, *([bias_tables] * per))


def _gla_constants(c):
    tris, masks = [], []
    for reverse in (False, True):
        i = np.arange(c)[:, None]
        j = np.arange(c)[None, :]
        tris.append((j >= i) if reverse else (j <= i))
        i = np.arange(c // 2)[:, None]
        j = np.arange(c // 2)[None, :]
        if reverse:
            i, j = j, i
        level = []
        s = c // 4
        while s >= GLA_DIAG:
            level.append(((i // (2 * s)) == (j // (2 * s))) & ((i % (2 * s)) >= s) & ((j % (2 * s)) < s))
            s //= 2
        level.append(((i // GLA_DIAG) == (j // GLA_DIAG)) & (j <= i))
        masks.append(np.stack(level))
    return jnp.asarray(np.stack(tris), BF16), jnp.asarray(np.stack(masks), F32)


def _block_refs(cum, s, reverse, diag):
    c = cum.shape[0]
    span = s if diag else 2 * s
    parts = []
    for p in range(c // span):
        if diag:
            r = p * span + (span - 1 if reverse else 0)
        else:
            r = p * span + (s - 1 if reverse else s)
        parts.append(jnp.broadcast_to(cum[r:r + 1, :], (span, cum.shape[1])))
    return jnp.concatenate(parts, axis=0)


def _cumsum_rows(a_hl, tri):
    parts = jnp.dot(tri, a_hl, preferred_element_type=F32)
    w = a_hl.shape[1] // 2
    return parts[:, :w] + parts[:, w:]


def _gla_chunk(q, k, v, a, state_t, tri, mask_ref, reverse, want_out):
    c = k.shape[0]
    hc = c // 2
    nt = (((1,), (1,)), ((), ()))
    k = k.astype(F32)
    q = q.astype(F32) if want_out else None
    cum = _cumsum_rows(a, tri)
    last = 0 if reverse else c - 1
    total = cum[last:last + 1, :]

    def scale(x, log2_factor):
        return (x * jnp.exp2(log2_factor)).astype(BF16)

    out = None
    if want_out:
        out = lax.dot_general(scale(q, cum), state_t.astype(BF16), nt, preferred_element_type=F32)
        halves = (slice(hc, c), slice(0, hc)) if reverse else (slice(0, hc), slice(hc, c))
        early, late = halves
        r = hc - 1 if reverse else hc
        g = cum[r:r + 1, :]
        cross = lax.dot_general(scale(q[late], cum[late] - g), scale(k[early], g - cum[early]), nt,
                                preferred_element_type=F32)
        inner = []
        for rows in halves:
            cx, qx, kx = cum[rows], q[rows], k[rows]
            acc = jnp.zeros((hc, hc), F32)
            s = hc // 2
            level = 0
            while True:
                diag = s < GLA_DIAG
                d = cx - _block_refs(cx, GLA_DIAG if diag else s, reverse, diag)
                if diag:
                    qs, ks = scale(qx, d), scale(kx, -d)
                else:
                    e = jnp.exp2(-jnp.abs(d))
                    qs, ks = (qx * e).astype(BF16), (kx * e).astype(BF16)
                acc = acc + lax.dot_general(qs, ks, nt, preferred_element_type=F32) * mask_ref[level]
                if diag:
                    break
                s //= 2
                level += 1
            inner.append(acc)
        out_early = jnp.dot(inner[0].astype(BF16), v[early], preferred_element_type=F32)
        out_late = jnp.dot(jnp.concatenate([cross, inner[1]], axis=1).astype(BF16),
                           jnp.concatenate([v[early], v[late]], axis=0), preferred_element_type=F32)
        intra = [out_late, out_early] if reverse else [out_early, out_late]
        out = out + jnp.concatenate(intra, axis=0)
    upd = lax.dot_general(v, scale(k, total - cum), (((0,), (0,)), ((), ())), preferred_element_type=F32)
    return out, jnp.exp2(total) * state_t + upd


def _gla_kernel(qf_ref, kf_ref, vf_ref, af_ref, qb_ref, kb_ref, vb_ref, ab_ref,
                kc_ref, vc_ref, acf_ref, acb_ref, tri_ref, mask_ref,
                of_ref, ob_ref, sf_ref, sb_ref):
    c = pl.program_id(2)
    fm, bm = mask_ref.at[0], mask_ref.at[1]
    heads = sf_ref.shape[0]

    @pl.when(c == 0)
    def _():
        zero = jnp.zeros(sf_ref.shape[1:], F32)
        for h in range(heads):
            kc = kc_ref[:, h * GLA_DK:(h + 1) * GLA_DK]
            vc = vc_ref[:, h * GLA_DV:(h + 1) * GLA_DV]
            hl = slice(2 * h * GLA_DK, 2 * (h + 1) * GLA_DK)
            _, sf_ref[h] = _gla_chunk(None, kc, vc, acf_ref[:, hl], zero, tri_ref[0], fm, False, False)
            _, sb_ref[h] = _gla_chunk(None, kc, vc, acb_ref[:, hl], zero, tri_ref[1], bm, True, False)

    @pl.when(c > 0)
    def _():
        for h in range(heads):
            ks = slice(h * GLA_DK, (h + 1) * GLA_DK)
            vs = slice(h * GLA_DV, (h + 1) * GLA_DV)
            hl = slice(2 * h * GLA_DK, 2 * (h + 1) * GLA_DK)
            o, sf_ref[h] = _gla_chunk(qf_ref[:, ks], kf_ref[:, ks], vf_ref[:, vs], af_ref[:, hl], sf_ref[h],
                                      tri_ref[0], fm, False, True)
            of_ref[:, vs] = o.astype(of_ref.dtype)
            o, sb_ref[h] = _gla_chunk(qb_ref[:, ks], kb_ref[:, ks], vb_ref[:, vs], ab_ref[:, hl], sb_ref[h],
                                      tri_ref[1], bm, True, True)
            ob_ref[:, vs] = o.astype(ob_ref.dtype)


def _gla(gla_qk, vb, decay, k_ctx, v_ctx, decay_ctx, batch, t_len, ctx_len):
    nc = t_len // GLA_CHUNK
    h = GLA_HEADS
    qk = gla_qk.reshape(batch, t_len, 2 * GLA_QK_WIDTH)
    v3 = vb.reshape(batch, t_len, GLA_V_WIDTH)
    a3 = decay.reshape(batch, t_len, GLA_DECAY_WIDTH)
    kc3 = k_ctx.reshape(batch, ctx_len, GLA_QK_WIDTH)
    vc3 = v_ctx.reshape(batch, ctx_len, GLA_V_WIDTH)
    ac3 = decay_ctx.reshape(batch, ctx_len, GLA_DECAY_WIDTH)
    tri, masks = _gla_constants(GLA_CHUNK)

    def fwd(c):
        return jnp.maximum(c - 1, 0)

    def bwd(c):
        return nc - 1 - jnp.maximum(c - 1, 0)

    def const(a):
        return pl.BlockSpec(a.shape, lambda b, hh, c: (0,) * a.ndim, pipeline_mode=pl.Buffered(1))

    hp = GLA_HEADS_PER_STEP
    groups = h // hp
    cq = (None, GLA_CHUNK, hp * GLA_DK)
    cv = (None, GLA_CHUNK, hp * GLA_DV)
    ca = (None, GLA_CHUNK, hp * 2 * GLA_DK)
    cca = (None, ctx_len, hp * 2 * GLA_DK)
    in_specs = [
        pl.BlockSpec(cq, lambda b, g, c: (b, fwd(c), g)),
        pl.BlockSpec(cq, lambda b, g, c: (b, fwd(c), groups + g)),
        pl.BlockSpec(cv, lambda b, g, c: (b, fwd(c), g)),
        pl.BlockSpec(ca, lambda b, g, c: (b, fwd(c), g)),
        pl.BlockSpec(cq, lambda b, g, c: (b, bwd(c), g)),
        pl.BlockSpec(cq, lambda b, g, c: (b, bwd(c), groups + g)),
        pl.BlockSpec(cv, lambda b, g, c: (b, bwd(c), g)),
        pl.BlockSpec(ca, lambda b, g, c: (b, bwd(c), groups + g)),
        pl.BlockSpec((None, ctx_len, hp * GLA_DK), lambda b, g, c: (b, 0, g)),
        pl.BlockSpec((None, ctx_len, hp * GLA_DV), lambda b, g, c: (b, 0, g)),
        pl.BlockSpec(cca, lambda b, g, c: (b, 0, g)),
        pl.BlockSpec(cca, lambda b, g, c: (b, 0, groups + g)),
        const(tri), const(masks),
    ]
    out_specs = [pl.BlockSpec(cv, lambda b, g, c: (b, fwd(c), g)),
                 pl.BlockSpec(cv, lambda b, g, c: (b, bwd(c), g))]
    out_shape = [jax.ShapeDtypeStruct((batch, t_len, GLA_V_WIDTH), BF16)] * 2
    return pl.pallas_call(
        _gla_kernel,
        out_shape=out_shape,
        grid=(batch, groups, nc + 1),
        in_specs=in_specs,
        out_specs=out_specs,
        scratch_shapes=[pltpu.VMEM((hp, GLA_DV, GLA_DK), F32), pltpu.VMEM((hp, GLA_DV, GLA_DK), F32)],
        compiler_params=_params(3),
        name="gla",
    )(qk, qk, v3, a3, qk, qk, v3, a3, kc3, vc3, ac3, ac3, tri, masks)


def _decay_weights(w_f, b_f, w_b, b_b):
    r, width = w_f.shape
    w2 = jnp.zeros((LANES, 2 * width), F32).at[:r, :width].set(w_f).at[r:2 * r, width:].set(w_b)
    return w2.astype(BF16), jnp.concatenate([b_f, b_b]).reshape(1, 2 * width)


def _layer_norm(v, g, b):
    mu = jnp.mean(v, axis=1, keepdims=True)
    var = jnp.mean(jnp.square(v - mu), axis=1, keepdims=True)
    return (v - mu) * lax.rsqrt(var + EPS) * g + b


def _merge_kernel(oa_ref, of_ref, ob_ref, gb_ref, gates_ref, x_ref, g1_ref, sc2_ref, sh2_ref,
                  wa_ref, wb_ref, wo_ref, nw_ref, lg_ref, lb_ref, rw_ref,
                  x1_ref, h2_ref, sc_ref):
    d = x_ref.shape[1]
    o = of_ref[...].astype(F32) + ob_ref[...].astype(F32)
    pieces = []
    for hh in range(GLA_HEADS):
        oh = o[:, hh * GLA_DV:(hh + 1) * GLA_DV]
        pieces.append(oh * lax.rsqrt(jnp.mean(jnp.square(oh), axis=1, keepdims=True) + EPS))
    out_b = jnp.concatenate(pieces, axis=1) * nw_ref[...] * _silu(gb_ref[...].astype(F32))
    ya = jnp.dot(oa_ref[...], wa_ref[...], preferred_element_type=F32)
    yb = jnp.dot(out_b.astype(BF16), wb_ref[...], preferred_element_type=F32)
    y = _sigmoid(gates_ref[:, :d].astype(F32)) * ya + _sigmoid(gates_ref[:, d:].astype(F32)) * yb
    y2 = jnp.dot(y.astype(BF16), wo_ref[...], preferred_element_type=F32)
    x1 = _layer_norm(DEEPNORM_ALPHA * x_ref[...] + g1_ref[0] * y2, lg_ref[...], lb_ref[...])
    x1_ref[...] = x1
    h2 = x1 * (1.0 + sc2_ref[0]) + sh2_ref[0]
    h2_ref[...] = _pack_pairs(h2)
    logits_t = lax.dot_general(rw_ref[...], h2.astype(BF16), (((1,), (1,)), ((), ())), preferred_element_type=F32)
    sc_ref[...] = _sigmoid(logits_t)


def _merge(out_a, o_f, o_b, gb, gates, x2d, g1, sc2, sh2, wa, wb, wo, nw, lg, lb, rw_t, tile, tiles_per_batch):
    n, d = x2d.shape
    row = lambda i: (i, 0)
    mod = lambda i: (i // tiles_per_batch, 0, 0)
    full = lambda i: (0, 0)

    def const(a):
        return pl.BlockSpec(a.shape, full, pipeline_mode=pl.Buffered(1))

    in_specs = [pl.BlockSpec((tile, NA_WIDTH), row), pl.BlockSpec((tile, GLA_V_WIDTH), row),
                pl.BlockSpec((tile, GLA_V_WIDTH), row), pl.BlockSpec((tile, GLA_V_WIDTH), row),
                pl.BlockSpec((tile, 2 * d), row), pl.BlockSpec((tile, d), row),
                pl.BlockSpec((1, 1, d), mod), pl.BlockSpec((1, 1, d), mod), pl.BlockSpec((1, 1, d), mod),
                const(wa), const(wb), const(wo), const(nw), const(lg), const(lb), const(rw_t)]
    out_shape = [jax.ShapeDtypeStruct((n, d), F32), jax.ShapeDtypeStruct((n, d // 2), U32),
                 jax.ShapeDtypeStruct((N_EXPERTS, n), F32)]
    out_specs = [pl.BlockSpec((tile, d), row), pl.BlockSpec((tile, d // 2), row),
                 pl.BlockSpec((N_EXPERTS, tile), lambda i: (0, i))]
    return pl.pallas_call(
        _merge_kernel, out_shape=out_shape, grid=(n // tile,), in_specs=in_specs, out_specs=out_specs,
        compiler_params=_params(1), name="merge_ln1_router",
    )(out_a, o_f, o_b, gb, gates, x2d, g1, sc2, sh2, wa, wb, wo, nw, lg, lb, rw_t)


def _first_argmax(vals, idx, n):
    m = jnp.max(vals, axis=0, keepdims=True)
    first = jnp.min(jnp.where(vals == m, idx, float(n)), axis=0, keepdims=True)
    return m, first


def _route_kernel(sc_ref, bias_ref, before_ref, e_ref, w_ref, rank_ref, cnt_ref, carry_ref):
    step = pl.program_id(0)
    tr = sc_ref.shape[1]

    @pl.when(step == 0)
    def _():
        carry_ref[...] = jnp.zeros(carry_ref.shape, F32)

    scores = sc_ref[...]
    biased = scores + bias_ref[...]
    eidx = lax.broadcasted_iota(jnp.int32, (N_EXPERTS, tr), 0).astype(F32)
    lidx = lax.broadcasted_iota(jnp.int32, (GROUP_SIZE, tr), 0).astype(F32)
    gidx = lax.broadcasted_iota(jnp.int32, (N_GROUPS, tr), 0).astype(F32)
    gs = []
    for g in range(N_GROUPS):
        blk = biased[g * GROUP_SIZE:(g + 1) * GROUP_SIZE]
        m1, first = _first_argmax(blk, lidx, GROUP_SIZE)
        m2 = jnp.max(jnp.where(lidx == first, -jnp.inf, blk), axis=0, keepdims=True)
        gs.append(m1 + m2)
    cur = jnp.concatenate(gs, axis=0)
    keep = jnp.zeros((N_GROUPS, tr), F32)
    for _ in range(TOPK_GROUPS):
        _, first = _first_argmax(cur, gidx, N_GROUPS)
        sel = gidx == first
        keep = jnp.where(sel, 1.0, keep)
        cur = jnp.where(sel, -jnp.inf, cur)
    keep_e = jnp.concatenate([jnp.broadcast_to(keep[g:g + 1], (GROUP_SIZE, tr)) for g in range(N_GROUPS)], axis=0)
    masked = jnp.where(keep_e > 0.5, biased, -jnp.inf)
    chosen = jnp.zeros((N_EXPERTS, tr), F32)
    tops, topi = [], []
    for _ in range(TOP_K):
        _, first = _first_argmax(masked, eidx, N_EXPERTS)
        sel = eidx == first
        tops.append(jnp.sum(jnp.where(sel, scores, 0.0), axis=0, keepdims=True))
        topi.append(first)
        chosen = jnp.where(sel, 1.0, chosen)
        masked = jnp.where(sel, -jnp.inf, masked)
    top_s = jnp.concatenate(tops, axis=0)
    top_i = jnp.concatenate(topi, axis=0)
    e_ref[...] = top_i.astype(jnp.int32)
    w_ref[...] = (top_s / jnp.sum(top_s, axis=0, keepdims=True) * ROUTED_SCALE).T
    prior = jnp.dot(chosen.astype(BF16), before_ref[...], preferred_element_type=F32) + carry_ref[...]
    ranks = [jnp.sum(jnp.where(eidx == topi[kk], prior, 0.0), axis=0, keepdims=True) for kk in range(TOP_K)]
    rank_ref[...] = jnp.concatenate(ranks, axis=0).astype(jnp.int32)
    carry_ref[...] = carry_ref[...] + jnp.sum(chosen, axis=1, keepdims=True)
    cnt_ref[...] = jnp.broadcast_to(carry_ref[...], cnt_ref.shape).astype(jnp.int32)


def _route(scores_t, router_bias, tile):
    n = scores_t.shape[1]
    col = lambda i: (0, i)
    out_shape = [jax.ShapeDtypeStruct((TOP_K, n), jnp.int32), jax.ShapeDtypeStruct((n, TOP_K), F32),
                 jax.ShapeDtypeStruct((TOP_K, n), jnp.int32), jax.ShapeDtypeStruct((N_EXPERTS, LANES), jnp.int32)]
    before = jnp.asarray(np.arange(tile)[:, None] < np.arange(tile)[None, :], BF16)
    return pl.pallas_call(
        _route_kernel, out_shape=out_shape, grid=(n // tile,),
        in_specs=[pl.BlockSpec((N_EXPERTS, tile), col), pl.BlockSpec((N_EXPERTS, 1), lambda i: (0, 0)),
                  pl.BlockSpec((tile, tile), lambda i: (0, 0), pipeline_mode=pl.Buffered(1))],
        out_specs=[pl.BlockSpec((TOP_K, tile), col), pl.BlockSpec((tile, TOP_K), lambda i: (i, 0)),
                   pl.BlockSpec((TOP_K, tile), col), pl.BlockSpec((N_EXPERTS, LANES), lambda i: (0, 0))],
        scratch_shapes=[pltpu.VMEM((N_EXPERTS, 1), F32)],
        compiler_params=_params(1), name="route",
    )(scores_t, router_bias.reshape(N_EXPERTS, 1), before)


def _slots_kernel(e_ref, rank_ref, start_ref, dest_ref):
    tr = e_ref.shape[1]
    eidx = lax.broadcasted_iota(jnp.int32, (N_EXPERTS, tr), 0)
    e = e_ref[...]
    start = start_ref[...]
    rows = [jnp.sum(jnp.where(eidx == e[kk:kk + 1], start, 0.0), axis=0, keepdims=True) for kk in range(TOP_K)]
    dest_ref[...] = jnp.concatenate(rows, axis=0).astype(jnp.int32) + rank_ref[...]


def _slots(top_e, rank, start_rows, tile):
    n = top_e.shape[1]
    col = lambda i: (0, i)
    return pl.pallas_call(
        _slots_kernel, out_shape=jax.ShapeDtypeStruct((TOP_K, n), jnp.int32), grid=(n // tile,),
        in_specs=[pl.BlockSpec((TOP_K, tile), col), pl.BlockSpec((TOP_K, tile), col),
                  pl.BlockSpec((N_EXPERTS, 1), lambda i: (0, 0))],
        out_specs=pl.BlockSpec((TOP_K, tile), col),
        compiler_params=_params(1), name="slots",
    )(top_e, rank, start_rows)


def _sc_mesh():
    return plsc.VectorSubcoreMesh(core_axis_name="c", subcore_axis_name="s",
                                  num_cores=SC_CORES, num_subcores=SC_SUBCORES)


def _sc_scatter_rows(rows, idx, n_out):
    n, dp = rows.shape
    copies = idx.shape[0] // n
    per_worker = n // (SC_CORES * SC_SUBCORES)
    chunk = SC_GATHER_ROWS

    @functools.partial(
        pl.kernel, mesh=_sc_mesh(), out_type=jax.ShapeDtypeStruct((n_out, dp), rows.dtype),
        scratch_types=[pltpu.VMEM((chunk,), jnp.int32), pltpu.VMEM((chunk, dp), rows.dtype)],
        name="sc_scatter_rows")
    def scatter(rows_hbm, idx_hbm, out_hbm, idx_v, rows_v):
        base = (lax.axis_index("s") * SC_CORES + lax.axis_index("c")) * per_worker

        @pl.loop(0, per_worker // chunk)
        def _(j):
            off = base + j * chunk
            pltpu.sync_copy(rows_hbm.at[pl.ds(off, chunk)], rows_v)
            for k in range(copies):
                pltpu.sync_copy(idx_hbm.at[pl.ds(k * n + off, chunk)], idx_v)
                pltpu.sync_copy(rows_v, out_hbm.at[idx_v])

    return scatter(rows, idx)


def _expert_kernel(start_ref, cnt_ref, nused_ref, xs_ref, wg_ref, wu_ref, wd_ref, y_ref,
                   wgb, wub, wdb, xbuf, ybuf, sem_in, sem_out):
    e = pl.program_id(0)
    bm, ring = EXPERT_BLOCK, EXPERT_RING
    cnt = cnt_ref[e]
    nb = (cnt + bm - 1) // bm
    g0 = start_ref[e]
    n_used = nused_ref[0]

    def x_copy(g):
        slot = g % ring
        return pltpu.make_async_copy(xs_ref.at[pl.ds(g * bm, bm)], xbuf.at[slot], sem_in.at[slot])

    def y_copy(g):
        slot = g % ring
        return pltpu.make_async_copy(ybuf.at[slot], y_ref.at[pl.ds(g * bm, bm)], sem_out.at[slot])

    @pl.when(e == 0)
    def _():
        for g in range(ring):
            @pl.when(g < n_used)
            def _():
                x_copy(g).start()

    wgb[...] = wg_ref[...].astype(BF16)
    wub[...] = wu_ref[...].astype(BF16)
    wdb[...] = wd_ref[...].astype(BF16)
    row = lax.broadcasted_iota(jnp.int32, (bm, xbuf.shape[2]), 0)

    def blocks(j, count):
        for b in range(count):
            g = g0 + j + b
            x_copy(g).wait()

            @pl.when(g >= ring)
            def _():
                y_copy(g - ring).wait()

        for b in range(count):
            slot = (g0 + j + b) % ring
            x = _unpack_pairs(jnp.where(row < cnt - (j + b) * bm, xbuf[slot], jnp.uint32(0))).astype(BF16)
            gate = jnp.dot(x, wgb[...], preferred_element_type=F32)
            up = jnp.dot(x, wub[...], preferred_element_type=F32)
            ybuf[slot] = _pack_pairs(
                jnp.dot((_silu(gate) * up).astype(BF16), wdb[...], preferred_element_type=F32))

        for b in range(count):
            g = g0 + j + b
            y_copy(g).start()

            @pl.when(g + ring < n_used)
            def _():
                x_copy(g + ring).start()

    group = EXPERT_GROUP

    def full_group(p, carry):
        blocks(group * p, group)
        return carry

    lax.fori_loop(0, nb // group, full_group, 0)
    done = (nb // group) * group
    size = group // 2
    while size >= 1:
        @pl.when((nb - done) % (2 * size) >= size)
        def _(done=done, size=size):
            blocks(done, size)

        done = done + jnp.where((nb - done) % (2 * size) >= size, size, 0)
        size //= 2

    @pl.when(e == pl.num_programs(0) - 1)
    def _():
        for back in range(1, ring + 1):
            @pl.when(n_used - back >= 0)
            def _():
                y_copy(n_used - back).wait()


def _experts(blk_start, counts, n_used, xs, wg, wu, wd):
    n_slots, dp = xs.shape
    n_exp, d, ff = wg.shape
    bm, ring = EXPERT_BLOCK, EXPERT_RING
    wsel = lambda e, st, ct, nu: (e, 0, 0)
    grid_spec = pltpu.PrefetchScalarGridSpec(
        num_scalar_prefetch=3, grid=(n_exp,),
        in_specs=[pl.BlockSpec(memory_space=pl.ANY),
                  pl.BlockSpec((None, d, ff), wsel), pl.BlockSpec((None, d, ff), wsel),
                  pl.BlockSpec((None, ff, d), wsel)],
        out_specs=pl.BlockSpec(memory_space=pl.ANY),
        scratch_shapes=[pltpu.VMEM((d, ff), BF16), pltpu.VMEM((d, ff), BF16), pltpu.VMEM((ff, d), BF16),
                        pltpu.VMEM((ring, bm, dp), U32), pltpu.VMEM((ring, bm, dp), U32),
                        pltpu.SemaphoreType.DMA((ring,)), pltpu.SemaphoreType.DMA((ring,))])
    return pl.pallas_call(
        _expert_kernel, out_shape=jax.ShapeDtypeStruct((n_slots, dp), U32), grid_spec=grid_spec,
        compiler_params=_params(1), name="experts",
    )(blk_start, counts, n_used, xs, wg, wu, wd)


def _sc_gather_rows(table, idx):
    n_idx = idx.shape[0]
    dp = table.shape[1]
    workers = SC_CORES * SC_SUBCORES
    per_worker = n_idx // workers
    chunk = SC_GATHER_ROWS

    @functools.partial(
        pl.kernel, mesh=_sc_mesh(), out_type=jax.ShapeDtypeStruct((n_idx, dp), table.dtype),
        scratch_types=[pltpu.VMEM((chunk,), jnp.int32), pltpu.VMEM((chunk, dp), table.dtype),
                       pltpu.SemaphoreType.DMA],
        name="sc_gather_rows")
    def gather(table_hbm, idx_hbm, out_hbm, idx_v, rows_v, sem):
        base = (lax.axis_index("s") * SC_CORES + lax.axis_index("c")) * per_worker

        @pl.loop(0, per_worker // chunk)
        def _(j):
            off = base + j * chunk
            pltpu.sync_copy(idx_hbm.at[pl.ds(off, chunk)], idx_v)
            pltpu.async_copy(table_hbm.at[idx_v], rows_v, sem).wait()
            pltpu.sync_copy(rows_v, out_hbm.at[pl.ds(off, chunk)])

    return gather(table, idx)


def _combine_kernel(tile, yt_ref, w_ref, h_ref, x1_ref, g2_ref, sg_ref, su_ref, sd_ref, lg_ref, lb_ref, *rest):
    o_ref = rest[-1]
    hb = _unpack_pairs(h_ref[...]).astype(BF16)
    g = jnp.dot(hb, sg_ref[...], preferred_element_type=F32)
    u = jnp.dot(hb, su_ref[...], preferred_element_type=F32)
    f = jnp.dot((_silu(g) * u).astype(BF16), sd_ref[...], preferred_element_type=F32)
    w = w_ref[...]
    for kk in range(TOP_K):
        f = f + w[:, kk:kk + 1] * _unpack_pairs(yt_ref[kk * tile:(kk + 1) * tile, :])
    o_ref[...] = _layer_norm(DEEPNORM_ALPHA * x1_ref[...] + g2_ref[0] * f, lg_ref[...], lb_ref[...])


def _combine(y_tok, top_w, h2p, x1, g2, sg, su, sd, lg, lb, tile, tiles_per_batch, first_tile, prev_out):
    n, d = x1.shape
    dp = h2p.shape[1]
    row = lambda i: (first_tile + i, 0)
    full = lambda i: (0, 0)

    def const(a):
        return pl.BlockSpec(a.shape, full, pipeline_mode=pl.Buffered(1))

    in_specs = [pl.BlockSpec((TOP_K * tile, dp), lambda i: (i, 0)),
                pl.BlockSpec((tile, TOP_K), row), pl.BlockSpec((tile, dp), row), pl.BlockSpec((tile, d), row),
                pl.BlockSpec((1, 1, d), lambda i: ((first_tile + i) // tiles_per_batch, 0, 0)),
                const(sg), const(su), const(sd), const(lg), const(lb)]
    args = [y_tok, top_w, h2p, x1, g2, sg, su, sd, lg, lb]
    aliases = {}
    if prev_out is not None:
        in_specs.append(pl.BlockSpec(memory_space=pl.ANY))
        args.append(prev_out)
        aliases = {len(args) - 1: 0}
    return pl.pallas_call(
        functools.partial(_combine_kernel, tile),
        out_shape=jax.ShapeDtypeStruct((n, d), F32),
        grid=(y_tok.shape[0] // (TOP_K * tile),),
        in_specs=in_specs,
        out_specs=pl.BlockSpec((tile, d), row),
        input_output_aliases=aliases,
        compiler_params=_params(1), name="combine_shared_ln2",
    )(*args)


def kernel(x, c, ctx, c_ctx, w_mod, b_mod, w_in, na_rpb, gla_w_decay_f, gla_b_decay_f, gla_w_decay_b, gla_b_decay_b,
           gla_norm_w, w_branch_a, w_branch_b, w_out, ln1_g, ln1_b, router_w, router_bias, exp_w_gate, exp_w_up,
           exp_w_down, sh_w_gate, sh_w_up, sh_w_down, ln2_g, ln2_b):
    batch, t_len, d = x.shape
    ctx_len = ctx.shape[1]
    n = batch * t_len
    assert w_mod.shape[0] == DEPTH == 1
    assert t_len % GLA_CHUNK == 0 and ctx_len == GLA_CHUNK and (t_len // GRID_W) % (NA_ROW_BLOCK * NA_BLOCKS_PER_STEP) == 0

    mod_rows = 16
    c_all = jnp.zeros((mod_rows, d), F32).at[:batch].set(c).at[batch].set(c_ctx)
    mod = _modulation(c_all, w_mod[0], b_mod[0])
    sh1, sc1, g1, sh2, sc2, g2 = [mod[:batch, j * d:(j + 1) * d].reshape(batch, 1, d) for j in range(6)]
    sh1c = mod[batch:batch + 1, 0:d].reshape(1, 1, d)
    sc1c = mod[batch:batch + 1, d:2 * d].reshape(1, 1, d)

    offs = np.cumsum((0, NA_WIDTH, NA_WIDTH, NA_WIDTH, GLA_QK_WIDTH, GLA_QK_WIDTH, GLA_V_WIDTH, GLA_V_WIDTH,
                      GLA_GATE_RANK, GLA_GATE_RANK, d, d))
    qa, ka, va, qb, kb, vbc, gbc, lrf, lrb, ga, gbt = [w_in[0][:, offs[j]:offs[j + 1]] for j in range(11)]
    lr_cols = jnp.concatenate([lrf, lrb, jnp.zeros((d, LANES - 2 * GLA_GATE_RANK), F32)], axis=1)
    w_lat = jnp.concatenate([lr_cols, qa, ka, va, qb, kb, vbc, gbc, ga, gbt], axis=1).astype(BF16)
    lat_offs = np.cumsum((0, LANES, NA_WIDTH, NA_WIDTH, NA_WIDTH, GLA_QK_WIDTH, GLA_QK_WIDTH, GLA_V_WIDTH))
    w_ctx = jnp.concatenate([w_lat[:, lat_offs[j]:lat_offs[j + 1]] for j in (0, 2, 3, 5, 6)], axis=1)
    plain = ("plain",)
    lat_plan = ((LANES, (("decay",),)),
                (3 * NA_WIDTH, (("scale", NA_HEAD_DIM ** -0.5), plain, plain)),
                (2 * GLA_QK_WIDTH, (("rope", GLA_DK ** -0.5), ("rope", 1.0))),
                (GLA_V_WIDTH, (plain, plain)), (GLA_V_WIDTH, (plain, plain)),
                (2 * d, (plain,) * 4))
    ctx_plan = ((LANES, (("decay",),)),
                (2 * NA_WIDTH, (plain, plain)), (GLA_QK_WIDTH, (plain,)), (GLA_V_WIDTH, (plain, plain)))
    w2, b2 = _decay_weights(gla_w_decay_f[0], gla_b_decay_f[0], gla_w_decay_b[0], gla_b_decay_b[0])
    tile = COMBINE_TILE
    x2d = x.reshape(n, d)
    decay, na_qkv, gla_qk, vb, gb, gates = _projection(
        x2d, sc1, sh1, w_lat, w2, b2, lat_plan, (BF16,) * 6, PROJ_TILE, t_len // PROJ_TILE,
        rope=_rope_tables(t_len))
    decay_c, na_kv_c, k_c, v_c = _projection(
        ctx.reshape(batch * ctx_len, d), sc1c, sh1c, w_ctx, w2, b2, ctx_plan, (BF16,) * 4, tile,
        batch * ctx_len // tile)

    out_a = _na_attention(na_qkv, na_kv_c, _na_bias_tables(na_rpb[0]), batch, t_len, ctx_len)
    o_f, o_b = _gla(gla_qk, vb, decay, k_c, v_c, decay_c, batch, t_len, ctx_len)

    x1, h2p, scores_t = _merge(
        out_a.reshape(n, NA_WIDTH), o_f.reshape(n, GLA_V_WIDTH), o_b.reshape(n, GLA_V_WIDTH), gb, gates, x2d,
        g1, sc2, sh2, w_branch_a[0].astype(BF16), w_branch_b[0].astype(BF16), w_out[0].astype(BF16),
        gla_norm_w[0].reshape(1, -1), ln1_g[0].reshape(1, d), ln1_b[0].reshape(1, d),
        router_w[0].T.astype(BF16), PROJ_TILE, t_len // PROJ_TILE)

    top_e, top_w, rank, counts = _route(scores_t, router_bias[0], 512)

    counts = counts[:, 0]
    n_blocks = n * TOP_K // EXPERT_BLOCK + N_EXPERTS
    blocks_per = (counts + EXPERT_BLOCK - 1) // EXPERT_BLOCK
    blk_end = jnp.cumsum(blocks_per)
    blk_start = blk_end - blocks_per
    dest = _slots(top_e, rank, (blk_start * EXPERT_BLOCK).astype(F32).reshape(N_EXPERTS, 1), 2048)

    xs = _sc_scatter_rows(h2p, dest.reshape(TOP_K * n), n_blocks * EXPERT_BLOCK)
    y = _experts(blk_start.astype(jnp.int32), counts, blk_end[-1:].astype(jnp.int32), xs,
                 exp_w_gate[0], exp_w_up[0], exp_w_down[0])
    dest_tok = dest.reshape(TOP_K, n // tile, tile).transpose(1, 0, 2).reshape(COMBINE_CHUNKS, -1)
    w_rows = top_w
    shared = (sh_w_gate[0].astype(BF16), sh_w_up[0].astype(BF16), sh_w_down[0].astype(BF16),
              ln2_g[0].reshape(1, d), ln2_b[0].reshape(1, d))
    tiles_per_chunk = n // tile // COMBINE_CHUNKS
    out = None
    for ci in range(COMBINE_CHUNKS):
        y_tok = _sc_gather_rows(y, dest_tok[ci])
        out = _combine(y_tok, w_rows, h2p, x1, g2, *shared, tile, t_len // tile, ci * tiles_per_chunk, out)
    return out.reshape(batch, t_len, d)
```

```python
import functools

import numpy as np
import jax
import jax.numpy as jnp
from jax import lax
from jax.experimental import pallas as pl
from jax.experimental.pallas import tpu as pltpu
from jax.experimental.pallas import tpu_sc as plsc

F32 = jnp.float32
BF16 = jnp.bfloat16
U32 = jnp.uint32
HIGHEST = lax.Precision.HIGHEST

GRID_W = 64
NA_HEADS = 8
NA_HEAD_DIM = 64
NA_WIDTH = NA_HEADS * NA_HEAD_DIM
WIN_ROWS = 8
WIN_COLS = 16
GLA_HEADS = 4
GLA_DK = 128
GLA_DV = 256
GLA_QK_WIDTH = GLA_HEADS * GLA_DK
GLA_V_WIDTH = GLA_HEADS * GLA_DV
GLA_GATE_RANK = 16
GLA_TAU = 16.0
LOG2E = 1.4426950408889634
ROPE_BASE = 10000.0
N_EXPERTS = 256
TOP_K = 8
N_GROUPS = 8
TOPK_GROUPS = 4
GROUP_SIZE = N_EXPERTS // N_GROUPS
ROUTED_SCALE = 2.5
DEPTH = 1
DEEPNORM_ALPHA = (2 * DEPTH) ** 0.25
EPS = 1e-6

LANES = 128
PROJ_TILE = 512
NA_ROW_BLOCK = 4
NA_BLOCKS_PER_STEP = 2
NA_UNION_ROWS = NA_ROW_BLOCK + WIN_ROWS - 1
GLA_CHUNK = 256
GLA_DIAG = 16
GLA_HEADS_PER_STEP = 4
EXPERT_BLOCK = 272
EXPERT_GROUP = 4
EXPERT_RING = 8
COMBINE_CHUNKS = 8
COMBINE_TILE = 512
SC_CORES = 2
SC_SUBCORES = 16
SC_GATHER_ROWS = 128
NEG_BIG = -1e30
VMEM_LIMIT = 56 * 1024 * 1024
NA_VMEM_LIMIT = 58 * 1024 * 1024


def _params(n_axes, vmem=VMEM_LIMIT):
    return pltpu.CompilerParams(dimension_semantics=("arbitrary",) * n_axes, vmem_limit_bytes=vmem)


def _sigmoid(v):
    return 1.0 / (1.0 + jnp.exp2(v * (-LOG2E)))


def _silu(v):
    return v * _sigmoid(v)


def _pack_pairs(v):
    m = v.shape[1] // 2
    lo = lax.bitcast_convert_type(v[:, :m].astype(BF16).astype(F32), U32) >> 16
    hi = lax.bitcast_convert_type(v[:, m:].astype(BF16).astype(F32), U32) & jnp.uint32(0xFFFF0000)
    return lo | hi


def _unpack_pairs(p):
    lo = lax.bitcast_convert_type(p << 16, F32)
    hi = lax.bitcast_convert_type(p & jnp.uint32(0xFFFF0000), F32)
    return jnp.concatenate([lo, hi], axis=1)


def _mod_kernel(c_ref, w_ref, b_ref, o_ref):
    o_ref[...] = jnp.dot(_silu(c_ref[...]), w_ref[...], preferred_element_type=F32, precision=HIGHEST) + b_ref[...]


def _modulation(c_all, w_mod, b_mod):
    rows, d = c_all.shape
    n = w_mod.shape[1]
    bn = 512
    return pl.pallas_call(
        _mod_kernel,
        out_shape=jax.ShapeDtypeStruct((rows, n), F32),
        grid=(n // bn,),
        in_specs=[pl.BlockSpec((rows, d), lambda j: (0, 0)),
                  pl.BlockSpec((d, bn), lambda j: (0, j)),
                  pl.BlockSpec((1, bn), lambda j: (0, j))],
        out_specs=pl.BlockSpec((rows, bn), lambda j: (0, j)),
        compiler_params=_params(1),
        name="modulation",
    )(c_all, w_mod, b_mod.reshape(1, n))


def _swap32(v):
    lane = lax.broadcasted_iota(jnp.int32, v.shape, 1)
    return jnp.where((lane % 64) < 32, pltpu.roll(v, 96, 1), pltpu.roll(v, 32, 1))


GLA_DECAY_WIDTH = 2 * GLA_HEADS * 2 * GLA_DK


def _log2_decay_split(lr, w2, b2, out_ref):
    z = (jnp.dot(lr.astype(BF16), w2, preferred_element_type=F32) + b2) * LOG2E
    a = (jnp.minimum(z, 0.0) - jnp.log2(1.0 + jnp.exp2(-jnp.abs(z)))) * (1.0 / GLA_TAU)
    hi = a.astype(BF16)
    lo = (a - hi.astype(F32)).astype(BF16)
    for p in range(a.shape[1] // GLA_DK):
        src = slice(p * GLA_DK, (p + 1) * GLA_DK)
        out_ref[:, 2 * p * GLA_DK:(2 * p + 1) * GLA_DK] = hi[:, src]
        out_ref[:, (2 * p + 1) * GLA_DK:(2 * p + 2) * GLA_DK] = lo[:, src]


def _proj_kernel(plan, has_rope, *refs):
    x_ref, sc_ref, sh_ref, w_ref, w2_ref, b2_ref = refs[:6]
    pos = 6
    if has_rope:
        cos_ref, sin_ref = refs[6:8]
        pos = 8
    out_refs = refs[pos:]
    h = (x_ref[...] * (1.0 + sc_ref[0]) + sh_ref[0]).astype(BF16)
    col = 0
    for out_ref, (width, kinds) in zip(out_refs, plan):
        for j, kind in enumerate(kinds):
            cw = width // len(kinds)
            c0 = j * cw
            acc = jnp.dot(h, w_ref[:, col + c0:col + c0 + cw], preferred_element_type=F32)
            if kind[0] == "decay":
                _log2_decay_split(acc, w2_ref[...], b2_ref[...], out_ref)
                continue
            if kind[0] == "scale":
                acc = acc * kind[1]
            elif kind[0] == "rope":
                cos, sin = cos_ref[...], sin_ref[...]
                pieces = []
                for p in range(cw // LANES):
                    v = acc[:, p * LANES:(p + 1) * LANES]
                    pieces.append((v * cos + _swap32(v) * sin) * kind[1])
                acc = jnp.concatenate(pieces, axis=1)
            out_ref[:, c0:c0 + cw] = acc.astype(out_ref.dtype)
        col += width


def _projection(x2d, sc, sh, w, w2, b2, plan, out_dtypes, tile, tiles_per_mod, rope=None):
    n, d = x2d.shape
    const = lambda a: pl.BlockSpec(a.shape, lambda i: (0, 0), pipeline_mode=pl.Buffered(1))
    in_specs = [pl.BlockSpec((tile, d), lambda i: (i, 0)),
                pl.BlockSpec((1, 1, d), lambda i: (i // tiles_per_mod, 0, 0)),
                pl.BlockSpec((1, 1, d), lambda i: (i // tiles_per_mod, 0, 0)),
                const(w), const(w2), const(b2)]
    args = [x2d, sc, sh, w, w2, b2]
    if rope is not None:
        in_specs += [pl.BlockSpec((tile, LANES), lambda i: (i % tiles_per_mod, 0))] * 2
        args += list(rope)
    widths = [GLA_DECAY_WIDTH if kinds[0][0] == "decay" else wd for wd, kinds in plan]
    out_shape = [jax.ShapeDtypeStruct((n, wd), dt) for wd, dt in zip(widths, out_dtypes)]
    out_specs = [pl.BlockSpec((tile, wd), lambda i: (i, 0)) for wd in widths]
    return pl.pallas_call(
        functools.partial(_proj_kernel, plan, rope is not None),
        out_shape=out_shape,
        grid=(n // tile,),
        in_specs=in_specs,
        out_specs=out_specs,
        compiler_params=_params(1),
        name="in_proj" if rope is not None else "ctx_proj",
    )(*args)


def _rope_tables(t_len):
    half = GLA_DK // 2
    quarter = half // 2
    f32 = np.float32
    inv_freq = f32(ROPE_BASE) ** (-np.arange(quarter, dtype=f32) / f32(quarter))
    pos = np.arange(t_len)
    row_ang = (pos // GRID_W).astype(f32)[:, None] * inv_freq[None, :]
    col_ang = (pos % GRID_W).astype(f32)[:, None] * inv_freq[None, :]
    cr, sr, cc, sn = np.cos(row_ang), np.sin(row_ang), np.cos(col_ang), np.sin(col_ang)
    cos = np.concatenate([cr, cr, cc, cc], axis=1).astype(f32)
    sin = np.concatenate([-sr, sr, -sn, sn], axis=1).astype(f32)
    return jnp.asarray(cos), jnp.asarray(sin)


def _na_bias_tables(rpb):
    rb, ur, w = NA_ROW_BLOCK, NA_UNION_ROWS, GRID_W
    heads = rpb.shape[0]
    pad = jnp.pad(rpb, ((0, 0), (0, 0), (w, w)))
    toep = jnp.stack([pad[:, :, w + WIN_COLS - 1 - c:2 * w + WIN_COLS - 1 - c] for c in range(w)], axis=2)
    c = np.arange(w)[:, None]
    kc = np.arange(w)[None, :]
    col_start = np.clip(c - WIN_COLS // 2, 0, w - WIN_COLS)
    col_ok = (kc >= col_start) & (kc < col_start + WIN_COLS)
    toep = jnp.where(col_ok[None, None], toep, NEG_BIG)
    neg = jnp.full((heads, w, w), NEG_BIG, F32)
    half = WIN_ROWS // 2
    tables = []
    for lo, off in ((lambda i: 0, WIN_ROWS - 1), (lambda i: i, WIN_ROWS - 1 - half), (lambda i: ur - WIN_ROWS, 0)):
        rows_ = []
        for i in range(rb):
            blocks = [toep[:, j - i + off] if lo(i) <= j < lo(i) + WIN_ROWS else neg for j in range(ur)]
            rows_.append(jnp.concatenate(blocks, axis=2))
        tables.append(jnp.concatenate(rows_, axis=1))
    return jnp.stack(tables)


def _na_kernel(rows, q_ref, k_ref, v_ref, kc_ref, vc_ref, *rest):
    bias_refs, o_ref = rest[:-1], rest[-1]
    tq = NA_ROW_BLOCK * GRID_W
    nk = NA_UNION_ROWS * GRID_W
    nt = (((1,), (1,)), ((), ()))
    lane = lax.broadcasted_iota(jnp.int32, (tq, LANES), 1)
    for sub, bias_ref in enumerate(bias_refs):
        rb = pl.program_id(1) * len(bias_refs) + sub
        ustart = jnp.clip(rb * NA_ROW_BLOCK - WIN_ROWS // 2, 0, rows - NA_UNION_ROWS)
        k0 = pl.multiple_of(ustart * GRID_W, GRID_W)
        qrows = slice(sub * tq, (sub + 1) * tq)
        for p in range(NA_WIDTH // LANES):
            ls = slice(p * LANES, (p + 1) * LANES)
            q = q_ref[qrows, ls]
            ku = k_ref[pl.ds(k0, nk), ls]
            vu = v_ref[pl.ds(k0, nk), ls]
            kc = kc_ref[:, ls]
            vc = vc_ref[:, ls]
            out = jnp.zeros(q.shape, F32)
            for hh in range(LANES // NA_HEAD_DIM):
                mine = (lane // NA_HEAD_DIM) == hh
                qm = jnp.where(mine, q, jnp.zeros_like(q))
                s_loc = lax.dot_general(qm, ku, nt, preferred_element_type=F32) + bias_ref[p * 2 + hh]
                s_ctx = lax.dot_general(qm, kc, nt, preferred_element_type=F32)
                m = jnp.maximum(jnp.max(s_loc, axis=1, keepdims=True), jnp.max(s_ctx, axis=1, keepdims=True))
                p_loc = jnp.exp(s_loc - m)
                p_ctx = jnp.exp(s_ctx - m)
                denom = jnp.sum(p_loc, axis=1, keepdims=True) + jnp.sum(p_ctx, axis=1, keepdims=True)
                o = (jnp.dot(p_loc.astype(BF16), vu, preferred_element_type=F32)
                     + jnp.dot(p_ctx.astype(BF16), vc, preferred_element_type=F32)) / denom
                out = jnp.where(mine, o, out)
            o_ref[qrows, ls] = out.astype(o_ref.dtype)


def _na_attention(na_qkv, na_kv_ctx, bias_tables, batch, t_len, ctx_len):
    rows = t_len // GRID_W
    n_rb = rows // NA_ROW_BLOCK
    per = NA_BLOCKS_PER_STEP
    tq = per * NA_ROW_BLOCK * GRID_W
    qkv = na_qkv.reshape(batch, t_len, 3 * NA_WIDTH)
    kvc = na_kv_ctx.reshape(batch, ctx_len, 2 * NA_WIDTH)

    def bias_spec(sub):
        def idx(b, r):
            rb = r * per + sub
            return (jnp.where(rb == 0, 0, jnp.where(rb == n_rb - 1, 2, 1)), 0, 0, 0)
        return pl.BlockSpec((None,) + bias_tables.shape[1:], idx)

    return pl.pallas_call(
        functools.partial(_na_kernel, rows),
        out_shape=jax.ShapeDtypeStruct((batch, t_len, NA_WIDTH), BF16),
        grid=(batch, n_rb // per),
        in_specs=[pl.BlockSpec((None, tq, NA_WIDTH), lambda b, r: (b, r, 0)),
                  pl.BlockSpec((None, t_len, NA_WIDTH), lambda b, r: (b, 0, 1)),
                  pl.BlockSpec((None, t_len, NA_WIDTH), lambda b, r: (b, 0, 2)),
                  pl.BlockSpec((None, ctx_len, NA_WIDTH), lambda b, r: (b, 0, 0)),
                  pl.BlockSpec((None, ctx_len, NA_WIDTH), lambda b, r: (b, 0, 1))]
                 + [bias_spec(sub) for sub in range(per)],
        out_specs=pl.BlockSpec((None, tq, NA_WIDTH), lambda b, r: (b, r, 0)),
        compiler_params=_params(2, vmem=NA_VMEM_LIMIT),
        name="na_attention",
    )(qkv, qkv, qkv, kvc, kvc, *([bias_tables] * per))


def _gla_constants(c):
    tris, masks = [], []
    for reverse in (False, True):
        i = np.arange(c)[:, None]
        j = np.arange(c)[None, :]
        tris.append((j >= i) if reverse else (j <= i))
        i = np.arange(c // 2)[:, None]
        j = np.arange(c // 2)[None, :]
        if reverse:
            i, j = j, i
        level = []
        s = c // 4
        while s >= GLA_DIAG:
            level.append(((i // (2 * s)) == (j // (2 * s))) & ((i % (2 * s)) >= s) & ((j % (2 * s)) < s))
            s //= 2
        level.append(((i // GLA_DIAG) == (j // GLA_DIAG)) & (j <= i))
        masks.append(np.stack(level))
    return jnp.asarray(np.stack(tris), BF16), jnp.asarray(np.stack(masks), F32)


def _block_refs(cum, s, reverse, diag):
    c = cum.shape[0]
    span = s if diag else 2 * s
    parts = []
    for p in range(c // span):
        if diag:
            r = p * span + (span - 1 if reverse else 0)
        else:
            r = p * span + (s - 1 if reverse else s)
        parts.append(jnp.broadcast_to(cum[r:r + 1, :], (span, cum.shape[1])))
    return jnp.concatenate(parts, axis=0)


def _cumsum_rows(a_hl, tri):
    parts = jnp.dot(tri, a_hl, preferred_element_type=F32)
    w = a_hl.shape[1] // 2
    return parts[:, :w] + parts[:, w:]


def _gla_chunk(q, k, v, a, state_t, tri, mask_ref, reverse, want_out):
    c = k.shape[0]
    hc = c // 2
    nt = (((1,), (1,)), ((), ()))
    k = k.astype(F32)
    q = q.astype(F32) if want_out else None
    cum = _cumsum_rows(a, tri)
    last = 0 if reverse else c - 1
    total = cum[last:last + 1, :]

    def scale(x, log2_factor):
        return (x * jnp.exp2(log2_factor)).astype(BF16)

    out = None
    if want_out:
        out = lax.dot_general(scale(q, cum), state_t.astype(BF16), nt, preferred_element_type=F32)
        halves = (slice(hc, c), slice(0, hc)) if reverse else (slice(0, hc), slice(hc, c))
        early, late = halves
        r = hc - 1 if reverse else hc
        g = cum[r:r + 1, :]
        cross = lax.dot_general(scale(q[late], cum[late] - g), scale(k[early], g - cum[early]), nt,
                                preferred_element_type=F32)
        inner = []
        for rows in halves:
            cx, qx, kx = cum[rows], q[rows], k[rows]
            acc = jnp.zeros((hc, hc), F32)
            s = hc // 2
            level = 0
            while True:
                diag = s < GLA_DIAG
                d = cx - _block_refs(cx, GLA_DIAG if diag else s, reverse, diag)
                if diag:
                    qs, ks = scale(qx, d), scale(kx, -d)
                else:
                    e = jnp.exp2(-jnp.abs(d))
                    qs, ks = (qx * e).astype(BF16), (kx * e).astype(BF16)
                acc = acc + lax.dot_general(qs, ks, nt, preferred_element_type=F32) * mask_ref[level]
                if diag:
                    break
                s //= 2
                level += 1
            inner.append(acc)
        out_early = jnp.dot(inner[0].astype(BF16), v[early], preferred_element_type=F32)
        out_late = jnp.dot(jnp.concatenate([cross, inner[1]], axis=1).astype(BF16),
                           jnp.concatenate([v[early], v[late]], axis=0), preferred_element_type=F32)
        intra = [out_late, out_early] if reverse else [out_early, out_late]
        out = out + jnp.concatenate(intra, axis=0)
    upd = lax.dot_general(v, scale(k, total - cum), (((0,), (0,)), ((), ())), preferred_element_type=F32)
    return out, jnp.exp2(total) * state_t + upd


def _gla_kernel(qf_ref, kf_ref, vf_ref, af_ref, qb_ref, kb_ref, vb_ref, ab_ref,
                kc_ref, vc_ref, acf_ref, acb_ref, tri_ref, mask_ref,
                of_ref, ob_ref, sf_ref, sb_ref):
    c = pl.program_id(2)
    fm, bm = mask_ref.at[0], mask_ref.at[1]
    heads = sf_ref.shape[0]

    @pl.when(c == 0)
    def _():
        zero = jnp.zeros(sf_ref.shape[1:], F32)
        for h in range(heads):
            kc = kc_ref[:, h * GLA_DK:(h + 1) * GLA_DK]
            vc = vc_ref[:, h * GLA_DV:(h + 1) * GLA_DV]
            hl = slice(2 * h * GLA_DK, 2 * (h + 1) * GLA_DK)
            _, sf_ref[h] = _gla_chunk(None, kc, vc, acf_ref[:, hl], zero, tri_ref[0], fm, False, False)
            _, sb_ref[h] = _gla_chunk(None, kc, vc, acb_ref[:, hl], zero, tri_ref[1], bm, True, False)

    @pl.when(c > 0)
    def _():
        for h in range(heads):
            ks = slice(h * GLA_DK, (h + 1) * GLA_DK)
            vs = slice(h * GLA_DV, (h + 1) * GLA_DV)
            hl = slice(2 * h * GLA_DK, 2 * (h + 1) * GLA_DK)
            o, sf_ref[h] = _gla_chunk(qf_ref[:, ks], kf_ref[:, ks], vf_ref[:, vs], af_ref[:, hl], sf_ref[h],
                                      tri_ref[0], fm, False, True)
            of_ref[:, vs] = o.astype(of_ref.dtype)
            o, sb_ref[h] = _gla_chunk(qb_ref[:, ks], kb_ref[:, ks], vb_ref[:, vs], ab_ref[:, hl], sb_ref[h],
                                      tri_ref[1], bm, True, True)
            ob_ref[:, vs] = o.astype(ob_ref.dtype)


def _gla(gla_qk, vb, decay, k_ctx, v_ctx, decay_ctx, batch, t_len, ctx_len):
    nc = t_len // GLA_CHUNK
    h = GLA_HEADS
    qk = gla_qk.reshape(batch, t_len, 2 * GLA_QK_WIDTH)
    v3 = vb.reshape(batch, t_len, GLA_V_WIDTH)
    a3 = decay.reshape(batch, t_len, GLA_DECAY_WIDTH)
    kc3 = k_ctx.reshape(batch, ctx_len, GLA_QK_WIDTH)
    vc3 = v_ctx.reshape(batch, ctx_len, GLA_V_WIDTH)
    ac3 = decay_ctx.reshape(batch, ctx_len, GLA_DECAY_WIDTH)
    tri, masks = _gla_constants(GLA_CHUNK)

    def fwd(c):
        return jnp.maximum(c - 1, 0)

    def bwd(c):
        return nc - 1 - jnp.maximum(c - 1, 0)

    def const(a):
        return pl.BlockSpec(a.shape, lambda b, hh, c: (0,) * a.ndim, pipeline_mode=pl.Buffered(1))

    hp = GLA_HEADS_PER_STEP
    groups = h // hp
    cq = (None, GLA_CHUNK, hp * GLA_DK)
    cv = (None, GLA_CHUNK, hp * GLA_DV)
    ca = (None, GLA_CHUNK, hp * 2 * GLA_DK)
    cca = (None, ctx_len, hp * 2 * GLA_DK)
    in_specs = [
        pl.BlockSpec(cq, lambda b, g, c: (b, fwd(c), g)),
        pl.BlockSpec(cq, lambda b, g, c: (b, fwd(c), groups + g)),
        pl.BlockSpec(cv, lambda b, g, c: (b, fwd(c), g)),
        pl.BlockSpec(ca, lambda b, g, c: (b, fwd(c), g)),
        pl.BlockSpec(cq, lambda b, g, c: (b, bwd(c), g)),
        pl.BlockSpec(cq, lambda b, g, c: (b, bwd(c), groups + g)),
        pl.BlockSpec(cv, lambda b, g, c: (b, bwd(c), g)),
        pl.BlockSpec(ca, lambda b, g, c: (b, bwd(c), groups + g)),
        pl.BlockSpec((None, ctx_len, hp * GLA_DK), lambda b, g, c: (b, 0, g)),
        pl.BlockSpec((None, ctx_len, hp * GLA_DV), lambda b, g, c: (b, 0, g)),
        pl.BlockSpec(cca, lambda b, g, c: (b, 0, g)),
        pl.BlockSpec(cca, lambda b, g, c: (b, 0, groups + g)),
        const(tri), const(masks),
    ]
    out_specs = [pl.BlockSpec(cv, lambda b, g, c: (b, fwd(c), g)),
                 pl.BlockSpec(cv, lambda b, g, c: (b, bwd(c), g))]
    out_shape = [jax.ShapeDtypeStruct((batch, t_len, GLA_V_WIDTH), BF16)] * 2
    return pl.pallas_call(
        _gla_kernel,
        out_shape=out_shape,
        grid=(batch, groups, nc + 1),
        in_specs=in_specs,
        out_specs=out_specs,
        scratch_shapes=[pltpu.VMEM((hp, GLA_DV, GLA_DK), F32), pltpu.VMEM((hp, GLA_DV, GLA_DK), F32)],
        compiler_params=_params(3),
        name="gla",
    )(qk, qk, v3, a3, qk, qk, v3, a3, kc3, vc3, ac3, ac3, tri, masks)


def _decay_weights(w_f, b_f, w_b, b_b):
    r, width = w_f.shape
    w2 = jnp.zeros((LANES, 2 * width), F32).at[:r, :width].set(w_f).at[r:2 * r, width:].set(w_b)
    return w2.astype(BF16), jnp.concatenate([b_f, b_b]).reshape(1, 2 * width)


def _layer_norm(v, g, b):
    mu = jnp.mean(v, axis=1, keepdims=True)
    var = jnp.mean(jnp.square(v - mu), axis=1, keepdims=True)
    return (v - mu) * lax.rsqrt(var + EPS) * g + b


def _merge_kernel(oa_ref, of_ref, ob_ref, gb_ref, gates_ref, x_ref, g1_ref, sc2_ref, sh2_ref,
                  wa_ref, wb_ref, wo_ref, nw_ref, lg_ref, lb_ref, rw_ref,
                  x1_ref, h2_ref, sc_ref):
    d = x_ref.shape[1]
    o = of_ref[...].astype(F32) + ob_ref[...].astype(F32)
    pieces = []
    for hh in range(GLA_HEADS):
        oh = o[:, hh * GLA_DV:(hh + 1) * GLA_DV]
        pieces.append(oh * lax.rsqrt(jnp.mean(jnp.square(oh), axis=1, keepdims=True) + EPS))
    out_b = jnp.concatenate(pieces, axis=1) * nw_ref[...] * _silu(gb_ref[...].astype(F32))
    ya = jnp.dot(oa_ref[...], wa_ref[...], preferred_element_type=F32)
    yb = jnp.dot(out_b.astype(BF16), wb_ref[...], preferred_element_type=F32)
    y = _sigmoid(gates_ref[:, :d].astype(F32)) * ya + _sigmoid(gates_ref[:, d:].astype(F32)) * yb
    y2 = jnp.dot(y.astype(BF16), wo_ref[...], preferred_element_type=F32)
    x1 = _layer_norm(DEEPNORM_ALPHA * x_ref[...] + g1_ref[0] * y2, lg_ref[...], lb_ref[...])
    x1_ref[...] = x1
    h2 = x1 * (1.0 + sc2_ref[0]) + sh2_ref[0]
    h2_ref[...] = _pack_pairs(h2)
    logits_t = lax.dot_general(rw_ref[...], h2.astype(BF16), (((1,), (1,)), ((), ())), preferred_element_type=F32)
    sc_ref[...] = _sigmoid(logits_t)


def _merge(out_a, o_f, o_b, gb, gates, x2d, g1, sc2, sh2, wa, wb, wo, nw, lg, lb, rw_t, tile, tiles_per_batch):
    n, d = x2d.shape
    row = lambda i: (i, 0)
    mod = lambda i: (i // tiles_per_batch, 0, 0)
    full = lambda i: (0, 0)

    def const(a):
        return pl.BlockSpec(a.shape, full, pipeline_mode=pl.Buffered(1))

    in_specs = [pl.BlockSpec((tile, NA_WIDTH), row), pl.BlockSpec((tile, GLA_V_WIDTH), row),
                pl.BlockSpec((tile, GLA_V_WIDTH), row), pl.BlockSpec((tile, GLA_V_WIDTH), row),
                pl.BlockSpec((tile, 2 * d), row), pl.BlockSpec((tile, d), row),
                pl.BlockSpec((1, 1, d), mod), pl.BlockSpec((1, 1, d), mod), pl.BlockSpec((1, 1, d), mod),
                const(wa), const(wb), const(wo), const(nw), const(lg), const(lb), const(rw_t)]
    out_shape = [jax.ShapeDtypeStruct((n, d), F32), jax.ShapeDtypeStruct((n, d // 2), U32),
                 jax.ShapeDtypeStruct((N_EXPERTS, n), F32)]
    out_specs = [pl.BlockSpec((tile, d), row), pl.BlockSpec((tile, d // 2), row),
                 pl.BlockSpec((N_EXPERTS, tile), lambda i: (0, i))]
    return pl.pallas_call(
        _merge_kernel, out_shape=out_shape, grid=(n // tile,), in_specs=in_specs, out_specs=out_specs,
        compiler_params=_params(1), name="merge_ln1_router",
    )(out_a, o_f, o_b, gb, gates, x2d, g1, sc2, sh2, wa, wb, wo, nw, lg, lb, rw_t)


def _first_argmax(vals, idx, n):
    m = jnp.max(vals, axis=0, keepdims=True)
    first = jnp.min(jnp.where(vals == m, idx, float(n)), axis=0, keepdims=True)
    return m, first


def _route_kernel(sc_ref, bias_ref, before_ref, e_ref, w_ref, rank_ref, cnt_ref, carry_ref):
    step = pl.program_id(0)
    tr = sc_ref.shape[1]

    @pl.when(step == 0)
    def _():
        carry_ref[...] = jnp.zeros(carry_ref.shape, F32)

    scores = sc_ref[...]
    biased = scores + bias_ref[...]
    eidx = lax.broadcasted_iota(jnp.int32, (N_EXPERTS, tr), 0).astype(F32)
    lidx = lax.broadcasted_iota(jnp.int32, (GROUP_SIZE, tr), 0).astype(F32)
    gidx = lax.broadcasted_iota(jnp.int32, (N_GROUPS, tr), 0).astype(F32)
    gs = []
    for g in range(N_GROUPS):
        blk = biased[g * GROUP_SIZE:(g + 1) * GROUP_SIZE]
        m1, first = _first_argmax(blk, lidx, GROUP_SIZE)
        m2 = jnp.max(jnp.where(lidx == first, -jnp.inf, blk), axis=0, keepdims=True)
        gs.append(m1 + m2)
    cur = jnp.concatenate(gs, axis=0)
    keep = jnp.zeros((N_GROUPS, tr), F32)
    for _ in range(TOPK_GROUPS):
        _, first = _first_argmax(cur, gidx, N_GROUPS)
        sel = gidx == first
        keep = jnp.where(sel, 1.0, keep)
        cur = jnp.where(sel, -jnp.inf, cur)
    keep_e = jnp.concatenate([jnp.broadcast_to(keep[g:g + 1], (GROUP_SIZE, tr)) for g in range(N_GROUPS)], axis=0)
    masked = jnp.where(keep_e > 0.5, biased, -jnp.inf)
    chosen = jnp.zeros((N_EXPERTS, tr), F32)
    tops, topi = [], []
    for _ in range(TOP_K):
        _, first = _first_argmax(masked, eidx, N_EXPERTS)
        sel = eidx == first
        tops.append(jnp.sum(jnp.where(sel, scores, 0.0), axis=0, keepdims=True))
        topi.append(first)
        chosen = jnp.where(sel, 1.0, chosen)
        masked = jnp.where(sel, -jnp.inf, masked)
    top_s = jnp.concatenate(tops, axis=0)
    top_i = jnp.concatenate(topi, axis=0)
    e_ref[...] = top_i.astype(jnp.int32)
    w_ref[...] = (top_s / jnp.sum(top_s, axis=0, keepdims=True) * ROUTED_SCALE).T
    prior = jnp.dot(chosen.astype(BF16), before_ref[...], preferred_element_type=F32) + carry_ref[...]
    ranks = [jnp.sum(jnp.where(eidx == topi[kk], prior, 0.0), axis=0, keepdims=True) for kk in range(TOP_K)]
    rank_ref[...] = jnp.concatenate(ranks, axis=0).astype(jnp.int32)
    carry_ref[...] = carry_ref[...] + jnp.sum(chosen, axis=1, keepdims=True)
    cnt_ref[...] = jnp.broadcast_to(carry_ref[...], cnt_ref.shape).astype(jnp.int32)


def _route(scores_t, router_bias, tile):
    n = scores_t.shape[1]
    col = lambda i: (0, i)
    out_shape = [jax.ShapeDtypeStruct((TOP_K, n), jnp.int32), jax.ShapeDtypeStruct((n, TOP_K), F32),
                 jax.ShapeDtypeStruct((TOP_K, n), jnp.int32), jax.ShapeDtypeStruct((N_EXPERTS, LANES), jnp.int32)]
    before = jnp.asarray(np.arange(tile)[:, None] < np.arange(tile)[None, :], BF16)
    return pl.pallas_call(
        _route_kernel, out_shape=out_shape, grid=(n // tile,),
        in_specs=[pl.BlockSpec((N_EXPERTS, tile), col), pl.BlockSpec((N_EXPERTS, 1), lambda i: (0, 0)),
                  pl.BlockSpec((tile, tile), lambda i: (0, 0), pipeline_mode=pl.Buffered(1))],
        out_specs=[pl.BlockSpec((TOP_K, tile), col), pl.BlockSpec((tile, TOP_K), lambda i: (i, 0)),
                   pl.BlockSpec((TOP_K, tile), col), pl.BlockSpec((N_EXPERTS, LANES), lambda i: (0, 0))],
        scratch_shapes=[pltpu.VMEM((N_EXPERTS, 1), F32)],
        compiler_params=_params(1), name="route",
    )(scores_t, router_bias.reshape(N_EXPERTS, 1), before)


def _slots_kernel(e_ref, rank_ref, start_ref, dest_ref):
    tr = e_ref.shape[1]
    eidx = lax.broadcasted_iota(jnp.int32, (N_EXPERTS, tr), 0)
    e = e_ref[...]
    start = start_ref[...]
    rows = [jnp.sum(jnp.where(eidx == e[kk:kk + 1], start, 0.0), axis=0, keepdims=True) for kk in range(TOP_K)]
    dest_ref[...] = jnp.concatenate(rows, axis=0).astype(jnp.int32) + rank_ref[...]


def _slots(top_e, rank, start_rows, tile):
    n = top_e.shape[1]
    col = lambda i: (0, i)
    return pl.pallas_call(
        _slots_kernel, out_shape=jax.ShapeDtypeStruct((TOP_K, n), jnp.int32), grid=(n // tile,),
        in_specs=[pl.BlockSpec((TOP_K, tile), col), pl.BlockSpec((TOP_K, tile), col),
                  pl.BlockSpec((N_EXPERTS, 1), lambda i: (0, 0))],
        out_specs=pl.BlockSpec((TOP_K, tile), col),
        compiler_params=_params(1), name="slots",
    )(top_e, rank, start_rows)


def _sc_mesh():
    return plsc.VectorSubcoreMesh(core_axis_name="c", subcore_axis_name="s",
                                  num_cores=SC_CORES, num_subcores=SC_SUBCORES)


def _sc_scatter_rows(rows, idx, n_out):
    n, dp = rows.shape
    copies = idx.shape[0] // n
    per_worker = n // (SC_CORES * SC_SUBCORES)
    chunk = SC_GATHER_ROWS

    @functools.partial(
        pl.kernel, mesh=_sc_mesh(), out_type=jax.ShapeDtypeStruct((n_out, dp), rows.dtype),
        scratch_types=[pltpu.VMEM((chunk,), jnp.int32), pltpu.VMEM((chunk, dp), rows.dtype)],
        name="sc_scatter_rows")
    def scatter(rows_hbm, idx_hbm, out_hbm, idx_v, rows_v):
        base = (lax.axis_index("s") * SC_CORES + lax.axis_index("c")) * per_worker

        @pl.loop(0, per_worker // chunk)
        def _(j):
            off = base + j * chunk
            pltpu.sync_copy(rows_hbm.at[pl.ds(off, chunk)], rows_v)
            for k in range(copies):
                pltpu.sync_copy(idx_hbm.at[pl.ds(k * n + off, chunk)], idx_v)
                pltpu.sync_copy(rows_v, out_hbm.at[idx_v])

    return scatter(rows, idx)


def _expert_kernel(start_ref, cnt_ref, nused_ref, xs_ref, wg_ref, wu_ref, wd_ref, y_ref,
                   wgb, wub, wdb, xbuf, ybuf, sem_in, sem_out):
    e = pl.program_id(0)
    bm, ring = EXPERT_BLOCK, EXPERT_RING
    cnt = cnt_ref[e]
    nb = (cnt + bm - 1) // bm
    g0 = start_ref[e]
    n_used = nused_ref[0]

    def x_copy(g):
        slot = g % ring
        return pltpu.make_async_copy(xs_ref.at[pl.ds(g * bm, bm)], xbuf.at[slot], sem_in.at[slot])

    def y_copy(g):
        slot = g % ring
        return pltpu.make_async_copy(ybuf.at[slot], y_ref.at[pl.ds(g * bm, bm)], sem_out.at[slot])

    @pl.when(e == 0)
    def _():
        for g in range(ring):
            @pl.when(g < n_used)
            def _():
                x_copy(g).start()

    wgb[...] = wg_ref[...].astype(BF16)
    wub[...] = wu_ref[...].astype(BF16)
    wdb[...] = wd_ref[...].astype(BF16)
    row = lax.broadcasted_iota(jnp.int32, (bm, xbuf.shape[2]), 0)

    def blocks(j, count):
        for b in range(count):
            g = g0 + j + b
            x_copy(g).wait()

            @pl.when(g >= ring)
            def _():
                y_copy(g - ring).wait()

        for b in range(count):
            slot = (g0 + j + b) % ring
            x = _unpack_pairs(jnp.where(row < cnt - (j + b) * bm, xbuf[slot], jnp.uint32(0))).astype(BF16)
            gate = jnp.dot(x, wgb[...], preferred_element_type=F32)
            up = jnp.dot(x, wub[...], preferred_element_type=F32)
            ybuf[slot] = _pack_pairs(
                jnp.dot((_silu(gate) * up).astype(BF16), wdb[...], preferred_element_type=F32))

        for b in range(count):
            g = g0 + j + b
            y_copy(g).start()

            @pl.when(g + ring < n_used)
            def _():
                x_copy(g + ring).start()

    group = EXPERT_GROUP

    def full_group(p, carry):
        blocks(group * p, group)
        return carry

    lax.fori_loop(0, nb // group, full_group, 0)
    done = (nb // group) * group
    size = group // 2
    while size >= 1:
        @pl.when((nb - done) % (2 * size) >= size)
        def _(done=done, size=size):
            blocks(done, size)

        done = done + jnp.where((nb - done) % (2 * size) >= size, size, 0)
        size //= 2

    @pl.when(e == pl.num_programs(0) - 1)
    def _():
        for back in range(1, ring + 1):
            @pl.when(n_used - back >= 0)
            def _():
                y_copy(n_used - back).wait()


def _experts(blk_start, counts, n_used, xs, wg, wu, wd):
    n_slots, dp = xs.shape
    n_exp, d, ff = wg.shape
    bm, ring = EXPERT_BLOCK, EXPERT_RING
    wsel = lambda e, st, ct, nu: (e, 0, 0)
    grid_spec = pltpu.PrefetchScalarGridSpec(
        num_scalar_prefetch=3, grid=(n_exp,),
        in_specs=[pl.BlockSpec(memory_space=pl.ANY),
                  pl.BlockSpec((None, d, ff), wsel), pl.BlockSpec((None, d, ff), wsel),
                  pl.BlockSpec((None, ff, d), wsel)],
        out_specs=pl.BlockSpec(memory_space=pl.ANY),
        scratch_shapes=[pltpu.VMEM((d, ff), BF16), pltpu.VMEM((d, ff), BF16), pltpu.VMEM((ff, d), BF16),
                        pltpu.VMEM((ring, bm, dp), U32), pltpu.VMEM((ring, bm, dp), U32),
                        pltpu.SemaphoreType.DMA((ring,)), pltpu.SemaphoreType.DMA((ring,))])
    return pl.pallas_call(
        _expert_kernel, out_shape=jax.ShapeDtypeStruct((n_slots, dp), U32), grid_spec=grid_spec,
        compiler_params=_params(1), name="experts",
    )(blk_start, counts, n_used, xs, wg, wu, wd)


def _sc_gather_rows(table, idx):
    n_idx = idx.shape[0]
    dp = table.shape[1]
    workers = SC_CORES * SC_SUBCORES
    per_worker = n_idx // workers
    chunk = SC_GATHER_ROWS

    @functools.partial(
        pl.kernel, mesh=_sc_mesh(), out_type=jax.ShapeDtypeStruct((n_idx, dp), table.dtype),
        scratch_types=[pltpu.VMEM((chunk,), jnp.int32), pltpu.VMEM((chunk, dp), table.dtype),
                       pltpu.SemaphoreType.DMA],
        name="sc_gather_rows")
    def gather(table_hbm, idx_hbm, out_hbm, idx_v, rows_v, sem):
        base = (lax.axis_index("s") * SC_CORES + lax.axis_index("c")) * per_worker

        @pl.loop(0, per_worker // chunk)
        def _(j):
            off = base + j * chunk
            pltpu.sync_copy(idx_hbm.at[pl.ds(off, chunk)], idx_v)
            pltpu.async_copy(table_hbm.at[idx_v], rows_v, sem).wait()
            pltpu.sync_copy(rows_v, out_hbm.at[pl.ds(off, chunk)])

    return gather(table, idx)


def _combine_kernel(tile, yt_ref, w_ref, h_ref, x1_ref, g2_ref, sg_ref, su_ref, sd_ref, lg_ref, lb_ref, *rest):
    o_ref = rest[-1]
    hb = _unpack_pairs(h_ref[...]).astype(BF16)
    g = jnp.dot(hb, sg_ref[...], preferred_element_type=F32)
    u = jnp.dot(hb, su_ref[...], preferred_element_type=F32)
    f = jnp.dot((_silu(g) * u).astype(BF16), sd_ref[...], preferred_element_type=F32)
    w = w_ref[...]
    for kk in range(TOP_K):
        f = f + w[:, kk:kk + 1] * _unpack_pairs(yt_ref[kk * tile:(kk + 1) * tile, :])
    o_ref[...] = _layer_norm(DEEPNORM_ALPHA * x1_ref[...] + g2_ref[0] * f, lg_ref[...], lb_ref[...])


def _combine(y_tok, top_w, h2p, x1, g2, sg, su, sd, lg, lb, tile, tiles_per_batch, first_tile, prev_out):
    n, d = x1.shape
    dp = h2p.shape[1]
    row = lambda i: (first_tile + i, 0)
    full = lambda i: (0, 0)

    def const(a):
        return pl.BlockSpec(a.shape, full, pipeline_mode=pl.Buffered(1))

    in_specs = [pl.BlockSpec((TOP_K * tile, dp), lambda i: (i, 0)),
                pl.BlockSpec((tile, TOP_K), row), pl.BlockSpec((tile, dp), row), pl.BlockSpec((tile, d), row),
                pl.BlockSpec((1, 1, d), lambda i: ((first_tile + i) // tiles_per_batch, 0, 0)),
                const(sg), const(su), const(sd), const(lg), const(lb)]
    args = [y_tok, top_w, h2p, x1, g2, sg, su, sd, lg, lb]
    aliases = {}
    if prev_out is not None:
        in_specs.append(pl.BlockSpec(memory_space=pl.ANY))
        args.append(prev_out)
        aliases = {len(args) - 1: 0}
    return pl.pallas_call(
        functools.partial(_combine_kernel, tile),
        out_shape=jax.ShapeDtypeStruct((n, d), F32),
        grid=(y_tok.shape[0] // (TOP_K * tile),),
        in_specs=in_specs,
        out_specs=pl.BlockSpec((tile, d), row),
        input_output_aliases=aliases,
        compiler_params=_params(1), name="combine_shared_ln2",
    )(*args)


def kernel(x, c, ctx, c_ctx, w_mod, b_mod, w_in, na_rpb, gla_w_decay_f, gla_b_decay_f, gla_w_decay_b, gla_b_decay_b,
           gla_norm_w, w_branch_a, w_branch_b, w_out, ln1_g, ln1_b, router_w, router_bias, exp_w_gate, exp_w_up,
           exp_w_down, sh_w_gate, sh_w_up, sh_w_down, ln2_g, ln2_b):
    batch, t_len, d = x.shape
    ctx_len = ctx.shape[1]
    n = batch * t_len
    assert w_mod.shape[0] == DEPTH == 1
    assert t_len % GLA_CHUNK == 0 and ctx_len == GLA_CHUNK and (t_len // GRID_W) % (NA_ROW_BLOCK * NA_BLOCKS_PER_STEP) == 0

    mod_rows = 16
    c_all = jnp.zeros((mod_rows, d), F32).at[:batch].set(c).at[batch].set(c_ctx)
    mod = _modulation(c_all, w_mod[0], b_mod[0])
    sh1, sc1, g1, sh2, sc2, g2 = [mod[:batch, j * d:(j + 1) * d].reshape(batch, 1, d) for j in range(6)]
    sh1c = mod[batch:batch + 1, 0:d].reshape(1, 1, d)
    sc1c = mod[batch:batch + 1, d:2 * d].reshape(1, 1, d)

    offs = np.cumsum((0, NA_WIDTH, NA_WIDTH, NA_WIDTH, GLA_QK_WIDTH, GLA_QK_WIDTH, GLA_V_WIDTH, GLA_V_WIDTH,
                      GLA_GATE_RANK, GLA_GATE_RANK, d, d))
    qa, ka, va, qb, kb, vbc, gbc, lrf, lrb, ga, gbt = [w_in[0][:, offs[j]:offs[j + 1]] for j in range(11)]
    lr_cols = jnp.concatenate([lrf, lrb, jnp.zeros((d, LANES - 2 * GLA_GATE_RANK), F32)], axis=1)
    w_lat = jnp.concatenate([lr_cols, qa, ka, va, qb, kb, vbc, gbc, ga, gbt], axis=1).astype(BF16)
    lat_offs = np.cumsum((0, LANES, NA_WIDTH, NA_WIDTH, NA_WIDTH, GLA_QK_WIDTH, GLA_QK_WIDTH, GLA_V_WIDTH))
    w_ctx = jnp.concatenate([w_lat[:, lat_offs[j]:lat_offs[j + 1]] for j in (0, 2, 3, 5, 6)], axis=1)
    plain = ("plain",)
    lat_plan = ((LANES, (("decay",),)),
                (3 * NA_WIDTH, (("scale", NA_HEAD_DIM ** -0.5), plain, plain)),
                (2 * GLA_QK_WIDTH, (("rope", GLA_DK ** -0.5), ("rope", 1.0))),
                (GLA_V_WIDTH, (plain, plain)), (GLA_V_WIDTH, (plain, plain)),
                (2 * d, (plain,) * 4))
    ctx_plan = ((LANES, (("decay",),)),
                (2 * NA_WIDTH, (plain, plain)), (GLA_QK_WIDTH, (plain,)), (GLA_V_WIDTH, (plain, plain)))
    w2, b2 = _decay_weights(gla_w_decay_f[0], gla_b_decay_f[0], gla_w_decay_b[0], gla_b_decay_b[0])
    tile = COMBINE_TILE
    x2d = x.reshape(n, d)
    decay, na_qkv, gla_qk, vb, gb, gates = _projection(
        x2d, sc1, sh1, w_lat, w2, b2, lat_plan, (BF16,) * 6, PROJ_TILE, t_len // PROJ_TILE,
        rope=_rope_tables(t_len))
    decay_c, na_kv_c, k_c, v_c = _projection(
        ctx.reshape(batch * ctx_len, d), sc1c, sh1c, w_ctx, w2, b2, ctx_plan, (BF16,) * 4, tile,
        batch * ctx_len // tile)

    out_a = _na_attention(na_qkv, na_kv_c, _na_bias_tables(na_rpb[0]), batch, t_len, ctx_len)
    o_f, o_b = _gla(gla_qk, vb, decay, k_c, v_c, decay_c, batch, t_len, ctx_len)

    x1, h2p, scores_t = _merge(
        out_a.reshape(n, NA_WIDTH), o_f.reshape(n, GLA_V_WIDTH), o_b.reshape(n, GLA_V_WIDTH), gb, gates, x2d,
        g1, sc2, sh2, w_branch_a[0].astype(BF16), w_branch_b[0].astype(BF16), w_out[0].astype(BF16),
        gla_norm_w[0].reshape(1, -1), ln1_g[0].reshape(1, d), ln1_b[0].reshape(1, d),
        router_w[0].T.astype(BF16), PROJ_TILE, t_len // PROJ_TILE)

    top_e, top_w, rank, counts = _route(scores_t, router_bias[0], 512)

    counts = counts[:, 0]
    n_blocks = pl.cdiv(n * TOP_K, EXPERT_BLOCK) + N_EXPERTS
    blocks_per = (counts + EXPERT_BLOCK - 1) // EXPERT_BLOCK
    blk_end = jnp.cumsum(blocks_per)
    blk_start = blk_end - blocks_per
    dest = _slots(top_e, rank, (blk_start * EXPERT_BLOCK).astype(F32).reshape(N_EXPERTS, 1), 2048)

    xs = _sc_scatter_rows(h2p, dest.reshape(TOP_K * n), n_blocks * EXPERT_BLOCK)
    y = _experts(blk_start.astype(jnp.int32), counts, blk_end[-1:].astype(jnp.int32), xs,
                 exp_w_gate[0], exp_w_up[0], exp_w_down[0])
    dest_tok = dest.reshape(TOP_K, n // tile, tile).transpose(1, 0, 2).reshape(COMBINE_CHUNKS, -1)
    w_rows = top_w
    shared = (sh_w_gate[0].astype(BF16), sh_w_up[0].astype(BF16), sh_w_down[0].astype(BF16),
              ln2_g[0].reshape(1, d), ln2_b[0].reshape(1, d))
    tiles_per_chunk = n // tile // COMBINE_CHUNKS
    out = None
    for ci in range(COMBINE_CHUNKS):
        y_tok = _sc_gather_rows(y, dest_tok[ci])
        out = _combine(y_tok, w_rows, h2p, x1, g2, *shared, tile, t_len // tile, ci * tiles_per_chunk, out)
    return out.reshape(batch, t_len, d)
```

```python
import functools

import numpy as np
import jax
import jax.numpy as jnp
from jax import lax
from jax.experimental import pallas as pl
from jax.experimental.pallas import tpu as pltpu
from jax.experimental.pallas import tpu_sc as plsc

F32 = jnp.float32
BF16 = jnp.bfloat16
U32 = jnp.uint32
HIGHEST = lax.Precision.HIGHEST

GRID_W = 64
NA_HEADS = 8
NA_HEAD_DIM = 64
NA_WIDTH = NA_HEADS * NA_HEAD_DIM
WIN_ROWS = 8
WIN_COLS = 16
GLA_HEADS = 4
GLA_DK = 128
GLA_DV = 256
GLA_QK_WIDTH = GLA_HEADS * GLA_DK
GLA_V_WIDTH = GLA_HEADS * GLA_DV
GLA_GATE_RANK = 16
GLA_TAU = 16.0
LOG2E = 1.4426950408889634
ROPE_BASE = 10000.0
N_EXPERTS = 256
TOP_K = 8
N_GROUPS = 8
TOPK_GROUPS = 4
GROUP_SIZE = N_EXPERTS // N_GROUPS
ROUTED_SCALE = 2.5
DEPTH = 1
DEEPNORM_ALPHA = (2 * DEPTH) ** 0.25
EPS = 1e-6

LANES = 128
PROJ_TILE = 512
NA_ROW_BLOCK = 4
NA_BLOCKS_PER_STEP = 2
NA_UNION_ROWS = NA_ROW_BLOCK + WIN_ROWS - 1
GLA_CHUNK = 256
GLA_DIAG = 16
GLA_HEADS_PER_STEP = 4
EXPERT_BLOCK = 272
EXPERT_GROUP = 4
EXPERT_RING = 8
COMBINE_CHUNKS = 8
COMBINE_TILE = 512
SC_CORES = 2
SC_SUBCORES = 16
SC_GATHER_ROWS = 128
SC_LANES = 16
SC_PACK_WORDS = 32 * 1024
NEG_BIG = -1e30
VMEM_LIMIT = 56 * 1024 * 1024
NA_VMEM_LIMIT = 58 * 1024 * 1024


def _params(n_axes, vmem=VMEM_LIMIT):
    return pltpu.CompilerParams(dimension_semantics=("arbitrary",) * n_axes, vmem_limit_bytes=vmem)


def _sigmoid(v):
    return 1.0 / (1.0 + jnp.exp2(v * (-LOG2E)))


def _silu(v):
    return v * _sigmoid(v)


def _pack_pairs(v):
    m = v.shape[1] // 2
    lo = lax.bitcast_convert_type(v[:, :m].astype(BF16).astype(F32), U32) >> 16
    hi = lax.bitcast_convert_type(v[:, m:].astype(BF16).astype(F32), U32) & jnp.uint32(0xFFFF0000)
    return lo | hi


def _unpack_pairs(p):
    lo = lax.bitcast_convert_type(p << 16, F32)
    hi = lax.bitcast_convert_type(p & jnp.uint32(0xFFFF0000), F32)
    return jnp.concatenate([lo, hi], axis=1)


def _mod_kernel(c_ref, w_ref, b_ref, o_ref):
    o_ref[...] = jnp.dot(_silu(c_ref[...]), w_ref[...], preferred_element_type=F32, precision=HIGHEST) + b_ref[...]


def _modulation(c_all, w_mod, b_mod):
    rows, d = c_all.shape
    n = w_mod.shape[1]
    bn = 512
    return pl.pallas_call(
        _mod_kernel,
        out_shape=jax.ShapeDtypeStruct((rows, n), F32),
        grid=(n // bn,),
        in_specs=[pl.BlockSpec((rows, d), lambda j: (0, 0)),
                  pl.BlockSpec((d, bn), lambda j: (0, j)),
                  pl.BlockSpec((1, bn), lambda j: (0, j))],
        out_specs=pl.BlockSpec((rows, bn), lambda j: (0, j)),
        compiler_params=_params(1),
        name="modulation",
    )(c_all, w_mod, b_mod.reshape(1, n))


def _swap32(v):
    lane = lax.broadcasted_iota(jnp.int32, v.shape, 1)
    return jnp.where((lane % 64) < 32, pltpu.roll(v, 96, 1), pltpu.roll(v, 32, 1))


GLA_DECAY_WIDTH = 2 * GLA_HEADS * 2 * GLA_DK


def _log2_decay_split(lr, w2, b2, out_ref):
    z = (jnp.dot(lr.astype(BF16), w2, preferred_element_type=F32) + b2) * LOG2E
    a = (jnp.minimum(z, 0.0) - jnp.log2(1.0 + jnp.exp2(-jnp.abs(z)))) * (1.0 / GLA_TAU)
    hi = a.astype(BF16)
    lo = (a - hi.astype(F32)).astype(BF16)
    for p in range(a.shape[1] // GLA_DK):
        src = slice(p * GLA_DK, (p + 1) * GLA_DK)
        out_ref[:, 2 * p * GLA_DK:(2 * p + 1) * GLA_DK] = hi[:, src]
        out_ref[:, (2 * p + 1) * GLA_DK:(2 * p + 2) * GLA_DK] = lo[:, src]


def _proj_kernel(plan, has_rope, *refs):
    x_ref, sc_ref, sh_ref, w_ref, w2_ref, b2_ref = refs[:6]
    pos = 6
    if has_rope:
        cos_ref, sin_ref = refs[6:8]
        pos = 8
    out_refs = refs[pos:]
    h = (x_ref[...] * (1.0 + sc_ref[0]) + sh_ref[0]).astype(BF16)
    col = 0
    for out_ref, (width, kinds) in zip(out_refs, plan):
        for j, kind in enumerate(kinds):
            cw = width // len(kinds)
            c0 = j * cw
            acc = jnp.dot(h, w_ref[:, col + c0:col + c0 + cw], preferred_element_type=F32)
            if kind[0] == "decay":
                _log2_decay_split(acc, w2_ref[...], b2_ref[...], out_ref)
                continue
            if kind[0] == "scale":
                acc = acc * kind[1]
            elif kind[0] == "rope":
                cos, sin = cos_ref[...], sin_ref[...]
                pieces = []
                for p in range(cw // LANES):
                    v = acc[:, p * LANES:(p + 1) * LANES]
                    pieces.append((v * cos + _swap32(v) * sin) * kind[1])
                acc = jnp.concatenate(pieces, axis=1)
            out_ref[:, c0:c0 + cw] = acc.astype(out_ref.dtype)
        col += width


def _projection(x2d, sc, sh, w, w2, b2, plan, out_dtypes, tile, tiles_per_mod, rope=None):
    n, d = x2d.shape
    const = lambda a: pl.BlockSpec(a.shape, lambda i: (0, 0), pipeline_mode=pl.Buffered(1))
    in_specs = [pl.BlockSpec((tile, d), lambda i: (i, 0)),
                pl.BlockSpec((1, 1, d), lambda i: (i // tiles_per_mod, 0, 0)),
                pl.BlockSpec((1, 1, d), lambda i: (i // tiles_per_mod, 0, 0)),
                const(w), const(w2), const(b2)]
    args = [x2d, sc, sh, w, w2, b2]
    if rope is not None:
        in_specs += [pl.BlockSpec((tile, LANES), lambda i: (i % tiles_per_mod, 0))] * 2
        args += list(rope)
    widths = [GLA_DECAY_WIDTH if kinds[0][0] == "decay" else wd for wd, kinds in plan]
    out_shape = [jax.ShapeDtypeStruct((n, wd), dt) for wd, dt in zip(widths, out_dtypes)]
    out_specs = [pl.BlockSpec((tile, wd), lambda i: (i, 0)) for wd in widths]
    return pl.pallas_call(
        functools.partial(_proj_kernel, plan, rope is not None),
        out_shape=out_shape,
        grid=(n // tile,),
        in_specs=in_specs,
        out_specs=out_specs,
        compiler_params=_params(1),
        name="in_proj" if rope is not None else "ctx_proj",
    )(*args)


def _rope_tables(t_len):
    half = GLA_DK // 2
    quarter = half // 2
    f32 = np.float32
    inv_freq = f32(ROPE_BASE) ** (-np.arange(quarter, dtype=f32) / f32(quarter))
    pos = np.arange(t_len)
    row_ang = (pos // GRID_W).astype(f32)[:, None] * inv_freq[None, :]
    col_ang = (pos % GRID_W).astype(f32)[:, None] * inv_freq[None, :]
    cr, sr, cc, sn = np.cos(row_ang), np.sin(row_ang), np.cos(col_ang), np.sin(col_ang)
    cos = np.concatenate([cr, cr, cc, cc], axis=1).astype(f32)
    sin = np.concatenate([-sr, sr, -sn, sn], axis=1).astype(f32)
    return jnp.asarray(cos), jnp.asarray(sin)


def _na_bias_tables(rpb):
    rb, ur, w = NA_ROW_BLOCK, NA_UNION_ROWS, GRID_W
    heads = rpb.shape[0]
    pad = jnp.pad(rpb, ((0, 0), (0, 0), (w, w)))
    toep = jnp.stack([pad[:, :, w + WIN_COLS - 1 - c:2 * w + WIN_COLS - 1 - c] for c in range(w)], axis=2)
    c = np.arange(w)[:, None]
    kc = np.arange(w)[None, :]
    col_start = np.clip(c - WIN_COLS // 2, 0, w - WIN_COLS)
    col_ok = (kc >= col_start) & (kc < col_start + WIN_COLS)
    toep = jnp.where(col_ok[None, None], toep, NEG_BIG)
    neg = jnp.full((heads, w, w), NEG_BIG, F32)
    half = WIN_ROWS // 2
    tables = []
    for lo, off in ((lambda i: 0, WIN_ROWS - 1), (lambda i: i, WIN_ROWS - 1 - half), (lambda i: ur - WIN_ROWS, 0)):
        rows_ = []
        for i in range(rb):
            blocks = [toep[:, j - i + off] if lo(i) <= j < lo(i) + WIN_ROWS else neg for j in range(ur)]
            rows_.append(jnp.concatenate(blocks, axis=2))
        tables.append(jnp.concatenate(rows_, axis=1))
    return jnp.stack(tables)


def _na_kernel(rows, q_ref, k_ref, v_ref, kc_ref, vc_ref, *rest):
    bias_refs, o_ref = rest[:-1], rest[-1]
    tq = NA_ROW_BLOCK * GRID_W
    nk = NA_UNION_ROWS * GRID_W
    nt = (((1,), (1,)), ((), ()))
    lane = lax.broadcasted_iota(jnp.int32, (tq, LANES), 1)
    for sub, bias_ref in enumerate(bias_refs):
        rb = pl.program_id(1) * len(bias_refs) + sub
        ustart = jnp.clip(rb * NA_ROW_BLOCK - WIN_ROWS // 2, 0, rows - NA_UNION_ROWS)
        k0 = pl.multiple_of(ustart * GRID_W, GRID_W)
        qrows = slice(sub * tq, (sub + 1) * tq)
        for p in range(NA_WIDTH // LANES):
            ls = slice(p * LANES, (p + 1) * LANES)
            q = q_ref[qrows, ls]
            ku = k_ref[pl.ds(k0, nk), ls]
            vu = v_ref[pl.ds(k0, nk), ls]
            kc = kc_ref[:, ls]
            vc = vc_ref[:, ls]
            out = jnp.zeros(q.shape, F32)
            for hh in range(LANES // NA_HEAD_DIM):
                mine = (lane // NA_HEAD_DIM) == hh
                qm = jnp.where(mine, q, jnp.zeros_like(q))
                s_loc = lax.dot_general(qm, ku, nt, preferred_element_type=F32) + bias_ref[p * 2 + hh]
                s_ctx = lax.dot_general(qm, kc, nt, preferred_element_type=F32)
                m = jnp.maximum(jnp.max(s_loc, axis=1, keepdims=True), jnp.max(s_ctx, axis=1, keepdims=True))
                p_loc = jnp.exp(s_loc - m)
                p_ctx = jnp.exp(s_ctx - m)
                denom = jnp.sum(p_loc, axis=1, keepdims=True) + jnp.sum(p_ctx, axis=1, keepdims=True)
                o = (jnp.dot(p_loc.astype(BF16), vu, preferred_element_type=F32)
                     + jnp.dot(p_ctx.astype(BF16), vc, preferred_element_type=F32)) / denom
                out = jnp.where(mine, o, out)
            o_ref[qrows, ls] = out.astype(o_ref.dtype)


def _na_attention(na_qkv, na_kv_ctx, bias_tables, batch, t_len, ctx_len):
    rows = t_len // GRID_W
    n_rb = rows // NA_ROW_BLOCK
    per = NA_BLOCKS_PER_STEP
    tq = per * NA_ROW_BLOCK * GRID_W
    qkv = na_qkv.reshape(batch, t_len, 3 * NA_WIDTH)
    kvc = na_kv_ctx.reshape(batch, ctx_len, 2 * NA_WIDTH)

    def bias_spec(sub):
        def idx(b, r):
            rb = r * per + sub
            return (jnp.where(rb == 0, 0, jnp.where(rb == n_rb - 1, 2, 1)), 0, 0, 0)
        return pl.BlockSpec((None,) + bias_tables.shape[1:], idx)

    return pl.pallas_call(
        functools.partial(_na_kernel, rows),
        out_shape=jax.ShapeDtypeStruct((batch, t_len, NA_WIDTH), BF16),
        grid=(batch, n_rb // per),
        in_specs=[pl.BlockSpec((None, tq, NA_WIDTH), lambda b, r: (b, r, 0)),
                  pl.BlockSpec((None, t_len, NA_WIDTH), lambda b, r: (b, 0, 1)),
                  pl.BlockSpec((None, t_len, NA_WIDTH), lambda b, r: (b, 0, 2)),
                  pl.BlockSpec((None, ctx_len, NA_WIDTH), lambda b, r: (b, 0, 0)),
                  pl.BlockSpec((None, ctx_len, NA_WIDTH), lambda b, r: (b, 0, 1))]
                 + [bias_spec(sub) for sub in range(per)],
        out_specs=pl.BlockSpec((None, tq, NA_WIDTH), lambda b, r: (b, r, 0)),
        compiler_params=_params(2, vmem=NA_VMEM_LIMIT),
        name="na_attention",
    )(qkv, qkv, qkv, kvc, kvc, *([bias_tables] * per))


def _gla_constants(c):
    tris, masks = [], []
    for reverse in (False, True):
        i = np.arange(c)[:, None]
        j = np.arange(c)[None, :]
        tris.append((j >= i) if reverse else (j <= i))
        i = np.arange(c // 2)[:, None]
        j = np.arange(c // 2)[None, :]
        if reverse:
            i, j = j, i
        level = []
        s = c // 4
        while s >= GLA_DIAG:
            level.append(((i // (2 * s)) == (j // (2 * s))) & ((i % (2 * s)) >= s) & ((j % (2 * s)) < s))
            s //= 2
        level.append(((i // GLA_DIAG) == (j // GLA_DIAG)) & (j <= i))
        masks.append(np.stack(level))
    return jnp.asarray(np.stack(tris), BF16), jnp.asarray(np.stack(masks), F32)


def _block_refs(cum, s, reverse, diag):
    c = cum.shape[0]
    span = s if diag else 2 * s
    parts = []
    for p in range(c // span):
        if diag:
            r = p * span + (span - 1 if reverse else 0)
        else:
            r = p * span + (s - 1 if reverse else s)
        parts.append(jnp.broadcast_to(cum[r:r + 1, :], (span, cum.shape[1])))
    return jnp.concatenate(parts, axis=0)


def _cumsum_rows(a_hl, tri):
    parts = jnp.dot(tri, a_hl, preferred_element_type=F32)
    w = a_hl.shape[1] // 2
    return parts[:, :w] + parts[:, w:]


def _gla_chunk(q, k, v, a, state_t, tri, mask_ref, reverse, want_out):
    c = k.shape[0]
    hc = c // 2
    nt = (((1,), (1,)), ((), ()))
    k = k.astype(F32)
    q = q.astype(F32) if want_out else None
    cum = _cumsum_rows(a, tri)
    last = 0 if reverse else c - 1
    total = cum[last:last + 1, :]

    def scale(x, log2_factor):
        return (x * jnp.exp2(log2_factor)).astype(BF16)

    out = None
    if want_out:
        out = lax.dot_general(scale(q, cum), state_t.astype(BF16), nt, preferred_element_type=F32)
        halves = (slice(hc, c), slice(0, hc)) if reverse else (slice(0, hc), slice(hc, c))
        early, late = halves
        r = hc - 1 if reverse else hc
        g = cum[r:r + 1, :]
        cross = lax.dot_general(scale(q[late], cum[late] - g), scale(k[early], g - cum[early]), nt,
                                preferred_element_type=F32)
        inner = []
        for rows in halves:
            cx, qx, kx = cum[rows], q[rows], k[rows]
            acc = jnp.zeros((hc, hc), F32)
            s = hc // 2
            level = 0
            while True:
                diag = s < GLA_DIAG
                d = cx - _block_refs(cx, GLA_DIAG if diag else s, reverse, diag)
                if diag:
                    qs, ks = scale(qx, d), scale(kx, -d)
                else:
                    e = jnp.exp2(-jnp.abs(d))
                    qs, ks = (qx * e).astype(BF16), (kx * e).astype(BF16)
                acc = acc + lax.dot_general(qs, ks, nt, preferred_element_type=F32) * mask_ref[level]
                if diag:
                    break
                s //= 2
                level += 1
            inner.append(acc)
        out_early = jnp.dot(inner[0].astype(BF16), v[early], preferred_element_type=F32)
        out_late = jnp.dot(jnp.concatenate([cross, inner[1]], axis=1).astype(BF16),
                           jnp.concatenate([v[early], v[late]], axis=0), preferred_element_type=F32)
        intra = [out_late, out_early] if reverse else [out_early, out_late]
        out = out + jnp.concatenate(intra, axis=0)
    upd = lax.dot_general(v, scale(k, total - cum), (((0,), (0,)), ((), ())), preferred_element_type=F32)
    return out, jnp.exp2(total) * state_t + upd


def _gla_kernel(qf_ref, kf_ref, vf_ref, af_ref, qb_ref, kb_ref, vb_ref, ab_ref,
                kc_ref, vc_ref, acf_ref, acb_ref, tri_ref, mask_ref,
                of_ref, ob_ref, sf_ref, sb_ref):
    c = pl.program_id(2)
    fm, bm = mask_ref.at[0], mask_ref.at[1]
    heads = sf_ref.shape[0]

    @pl.when(c == 0)
    def _():
        zero = jnp.zeros(sf_ref.shape[1:], F32)
        for h in range(heads):
            kc = kc_ref[:, h * GLA_DK:(h + 1) * GLA_DK]
            vc = vc_ref[:, h * GLA_DV:(h + 1) * GLA_DV]
            hl = slice(2 * h * GLA_DK, 2 * (h + 1) * GLA_DK)
            _, sf_ref[h] = _gla_chunk(None, kc, vc, acf_ref[:, hl], zero, tri_ref[0], fm, False, False)
            _, sb_ref[h] = _gla_chunk(None, kc, vc, acb_ref[:, hl], zero, tri_ref[1], bm, True, False)

    @pl.when(c > 0)
    def _():
        for h in range(heads):
            ks = slice(h * GLA_DK, (h + 1) * GLA_DK)
            vs = slice(h * GLA_DV, (h + 1) * GLA_DV)
            hl = slice(2 * h * GLA_DK, 2 * (h + 1) * GLA_DK)
            o, sf_ref[h] = _gla_chunk(qf_ref[:, ks], kf_ref[:, ks], vf_ref[:, vs], af_ref[:, hl], sf_ref[h],
                                      tri_ref[0], fm, False, True)
            of_ref[:, vs] = o.astype(of_ref.dtype)
            o, sb_ref[h] = _gla_chunk(qb_ref[:, ks], kb_ref[:, ks], vb_ref[:, vs], ab_ref[:, hl], sb_ref[h],
                                      tri_ref[1], bm, True, True)
            ob_ref[:, vs] = o.astype(ob_ref.dtype)


def _gla(gla_qk, vb, decay, k_ctx, v_ctx, decay_ctx, batch, t_len, ctx_len):
    nc = t_len // GLA_CHUNK
    h = GLA_HEADS
    qk = gla_qk.reshape(batch, t_len, 2 * GLA_QK_WIDTH)
    v3 = vb.reshape(batch, t_len, GLA_V_WIDTH)
    a3 = decay.reshape(batch, t_len, GLA_DECAY_WIDTH)
    kc3 = k_ctx.reshape(batch, ctx_len, GLA_QK_WIDTH)
    vc3 = v_ctx.reshape(batch, ctx_len, GLA_V_WIDTH)
    ac3 = decay_ctx.reshape(batch, ctx_len, GLA_DECAY_WIDTH)
    tri, masks = _gla_constants(GLA_CHUNK)

    def fwd(c):
        return jnp.maximum(c - 1, 0)

    def bwd(c):
        return nc - 1 - jnp.maximum(c - 1, 0)

    def const(a):
        return pl.BlockSpec(a.shape, lambda b, hh, c: (0,) * a.ndim, pipeline_mode=pl.Buffered(1))

    hp = GLA_HEADS_PER_STEP
    groups = h // hp
    cq = (None, GLA_CHUNK, hp * GLA_DK)
    cv = (None, GLA_CHUNK, hp * GLA_DV)
    ca = (None, GLA_CHUNK, hp * 2 * GLA_DK)
    cca = (None, ctx_len, hp * 2 * GLA_DK)
    in_specs = [
        pl.BlockSpec(cq, lambda b, g, c: (b, fwd(c), g)),
        pl.BlockSpec(cq, lambda b, g, c: (b, fwd(c), groups + g)),
        pl.BlockSpec(cv, lambda b, g, c: (b, fwd(c), g)),
        pl.BlockSpec(ca, lambda b, g, c: (b, fwd(c), g)),
        pl.BlockSpec(cq, lambda b, g, c: (b, bwd(c), g)),
        pl.BlockSpec(cq, lambda b, g, c: (b, bwd(c), groups + g)),
        pl.BlockSpec(cv, lambda b, g, c: (b, bwd(c), g)),
        pl.BlockSpec(ca, lambda b, g, c: (b, bwd(c), groups + g)),
        pl.BlockSpec((None, ctx_len, hp * GLA_DK), lambda b, g, c: (b, 0, g)),
        pl.BlockSpec((None, ctx_len, hp * GLA_DV), lambda b, g, c: (b, 0, g)),
        pl.BlockSpec(cca, lambda b, g, c: (b, 0, g)),
        pl.BlockSpec(cca, lambda b, g, c: (b, 0, groups + g)),
        const(tri), const(masks),
    ]
    out_specs = [pl.BlockSpec(cv, lambda b, g, c: (b, fwd(c), g)),
                 pl.BlockSpec(cv, lambda b, g, c: (b, bwd(c), g))]
    out_shape = [jax.ShapeDtypeStruct((batch, t_len, GLA_V_WIDTH), BF16)] * 2
    return pl.pallas_call(
        _gla_kernel,
        out_shape=out_shape,
        grid=(batch, groups, nc + 1),
        in_specs=in_specs,
        out_specs=out_specs,
        scratch_shapes=[pltpu.VMEM((hp, GLA_DV, GLA_DK), F32), pltpu.VMEM((hp, GLA_DV, GLA_DK), F32)],
        compiler_params=_params(3),
        name="gla",
    )(qk, qk, v3, a3, qk, qk, v3, a3, kc3, vc3, ac3, ac3, tri, masks)


def _decay_weights(w_f, b_f, w_b, b_b):
    r, width = w_f.shape
    w2 = jnp.zeros((LANES, 2 * width), F32).at[:r, :width].set(w_f).at[r:2 * r, width:].set(w_b)
    return w2.astype(BF16), jnp.concatenate([b_f, b_b]).reshape(1, 2 * width)


def _layer_norm(v, g, b):
    mu = jnp.mean(v, axis=1, keepdims=True)
    var = jnp.mean(jnp.square(v - mu), axis=1, keepdims=True)
    return (v - mu) * lax.rsqrt(var + EPS) * g + b


def _merge_kernel(oa_ref, of_ref, ob_ref, gb_ref, gates_ref, x_ref, g1_ref, sc2_ref, sh2_ref,
                  wa_ref, wb_ref, wo_ref, nw_ref, lg_ref, lb_ref, rw_ref,
                  x1_ref, h2_ref, sc_ref):
    d = x_ref.shape[1]
    o = of_ref[...].astype(F32) + ob_ref[...].astype(F32)
    pieces = []
    for hh in range(GLA_HEADS):
        oh = o[:, hh * GLA_DV:(hh + 1) * GLA_DV]
        pieces.append(oh * lax.rsqrt(jnp.mean(jnp.square(oh), axis=1, keepdims=True) + EPS))
    out_b = jnp.concatenate(pieces, axis=1) * nw_ref[...] * _silu(gb_ref[...].astype(F32))
    ya = jnp.dot(oa_ref[...], wa_ref[...], preferred_element_type=F32)
    yb = jnp.dot(out_b.astype(BF16), wb_ref[...], preferred_element_type=F32)
    y = _sigmoid(gates_ref[:, :d].astype(F32)) * ya + _sigmoid(gates_ref[:, d:].astype(F32)) * yb
    y2 = jnp.dot(y.astype(BF16), wo_ref[...], preferred_element_type=F32)
    x1 = _layer_norm(DEEPNORM_ALPHA * x_ref[...] + g1_ref[0] * y2, lg_ref[...], lb_ref[...])
    x1_ref[...] = x1
    h2 = x1 * (1.0 + sc2_ref[0]) + sh2_ref[0]
    h2_ref[...] = _pack_pairs(h2)
    logits_t = lax.dot_general(rw_ref[...], h2.astype(BF16), (((1,), (1,)), ((), ())), preferred_element_type=F32)
    sc_ref[...] = _sigmoid(logits_t)


def _merge(out_a, o_f, o_b, gb, gates, x2d, g1, sc2, sh2, wa, wb, wo, nw, lg, lb, rw_t, tile, tiles_per_batch):
    n, d = x2d.shape
    row = lambda i: (i, 0)
    mod = lambda i: (i // tiles_per_batch, 0, 0)
    full = lambda i: (0, 0)

    def const(a):
        return pl.BlockSpec(a.shape, full, pipeline_mode=pl.Buffered(1))

    in_specs = [pl.BlockSpec((tile, NA_WIDTH), row), pl.BlockSpec((tile, GLA_V_WIDTH), row),
                pl.BlockSpec((tile, GLA_V_WIDTH), row), pl.BlockSpec((tile, GLA_V_WIDTH), row),
                pl.BlockSpec((tile, 2 * d), row), pl.BlockSpec((tile, d), row),
                pl.BlockSpec((1, 1, d), mod), pl.BlockSpec((1, 1, d), mod), pl.BlockSpec((1, 1, d), mod),
                const(wa), const(wb), const(wo), const(nw), const(lg), const(lb), const(rw_t)]
    out_shape = [jax.ShapeDtypeStruct((n, d), F32), jax.ShapeDtypeStruct((n, d // 2), U32),
                 jax.ShapeDtypeStruct((N_EXPERTS, n), F32)]
    out_specs = [pl.BlockSpec((tile, d), row), pl.BlockSpec((tile, d // 2), row),
                 pl.BlockSpec((N_EXPERTS, tile), lambda i: (0, i))]
    return pl.pallas_call(
        _merge_kernel, out_shape=out_shape, grid=(n // tile,), in_specs=in_specs, out_specs=out_specs,
        compiler_params=_params(1), name="merge_ln1_router",
    )(out_a, o_f, o_b, gb, gates, x2d, g1, sc2, sh2, wa, wb, wo, nw, lg, lb, rw_t)


def _first_argmax(vals, idx, n):
    m = jnp.max(vals, axis=0, keepdims=True)
    first = jnp.min(jnp.where(vals == m, idx, float(n)), axis=0, keepdims=True)
    return m, first


def _route_kernel(sc_ref, bias_ref, before_ref, e_ref, w_ref, rank_ref, cnt_ref, carry_ref):
    step = pl.program_id(0)
    tr = sc_ref.shape[1]

    @pl.when(step == 0)
    def _():
        carry_ref[...] = jnp.zeros(carry_ref.shape, F32)

    scores = sc_ref[...]
    biased = scores + bias_ref[...]
    eidx = lax.broadcasted_iota(jnp.int32, (N_EXPERTS, tr), 0).astype(F32)
    lidx = lax.broadcasted_iota(jnp.int32, (GROUP_SIZE, tr), 0).astype(F32)
    gidx = lax.broadcasted_iota(jnp.int32, (N_GROUPS, tr), 0).astype(F32)
    gs = []
    for g in range(N_GROUPS):
        blk = biased[g * GROUP_SIZE:(g + 1) * GROUP_SIZE]
        m1, first = _first_argmax(blk, lidx, GROUP_SIZE)
        m2 = jnp.max(jnp.where(lidx == first, -jnp.inf, blk), axis=0, keepdims=True)
        gs.append(m1 + m2)
    cur = jnp.concatenate(gs, axis=0)
    keep = jnp.zeros((N_GROUPS, tr), F32)
    for _ in range(TOPK_GROUPS):
        _, first = _first_argmax(cur, gidx, N_GROUPS)
        sel = gidx == first
        keep = jnp.where(sel, 1.0, keep)
        cur = jnp.where(sel, -jnp.inf, cur)
    keep_e = jnp.concatenate([jnp.broadcast_to(keep[g:g + 1], (GROUP_SIZE, tr)) for g in range(N_GROUPS)], axis=0)
    masked = jnp.where(keep_e > 0.5, biased, -jnp.inf)
    chosen = jnp.zeros((N_EXPERTS, tr), F32)
    tops, topi = [], []
    for _ in range(TOP_K):
        _, first = _first_argmax(masked, eidx, N_EXPERTS)
        sel = eidx == first
        tops.append(jnp.sum(jnp.where(sel, scores, 0.0), axis=0, keepdims=True))
        topi.append(first)
        chosen = jnp.where(sel, 1.0, chosen)
        masked = jnp.where(sel, -jnp.inf, masked)
    top_s = jnp.concatenate(tops, axis=0)
    top_i = jnp.concatenate(topi, axis=0)
    e_ref[...] = top_i.astype(jnp.int32)
    w_ref[...] = (top_s / jnp.sum(top_s, axis=0, keepdims=True) * ROUTED_SCALE).T
    prior = jnp.dot(chosen.astype(BF16), before_ref[...], preferred_element_type=F32) + carry_ref[...]
    ranks = [jnp.sum(jnp.where(eidx == topi[kk], prior, 0.0), axis=0, keepdims=True) for kk in range(TOP_K)]
    rank_ref[...] = jnp.concatenate(ranks, axis=0).astype(jnp.int32)
    carry_ref[...] = carry_ref[...] + jnp.sum(chosen, axis=1, keepdims=True)
    cnt_ref[...] = jnp.broadcast_to(carry_ref[...], cnt_ref.shape).astype(jnp.int32)


def _route(scores_t, router_bias, tile):
    n = scores_t.shape[1]
    col = lambda i: (0, i)
    out_shape = [jax.ShapeDtypeStruct((TOP_K, n), jnp.int32), jax.ShapeDtypeStruct((n, TOP_K), F32),
                 jax.ShapeDtypeStruct((TOP_K, n), jnp.int32), jax.ShapeDtypeStruct((N_EXPERTS, LANES), jnp.int32)]
    before = jnp.asarray(np.arange(tile)[:, None] < np.arange(tile)[None, :], BF16)
    return pl.pallas_call(
        _route_kernel, out_shape=out_shape, grid=(n // tile,),
        in_specs=[pl.BlockSpec((N_EXPERTS, tile), col), pl.BlockSpec((N_EXPERTS, 1), lambda i: (0, 0)),
                  pl.BlockSpec((tile, tile), lambda i: (0, 0), pipeline_mode=pl.Buffered(1))],
        out_specs=[pl.BlockSpec((TOP_K, tile), col), pl.BlockSpec((tile, TOP_K), lambda i: (i, 0)),
                   pl.BlockSpec((TOP_K, tile), col), pl.BlockSpec((N_EXPERTS, LANES), lambda i: (0, 0))],
        scratch_shapes=[pltpu.VMEM((N_EXPERTS, 1), F32)],
        compiler_params=_params(1), name="route",
    )(scores_t, router_bias.reshape(N_EXPERTS, 1), before)


def _slots_kernel(e_ref, rank_ref, start_ref, dest_ref):
    tr = e_ref.shape[1]
    eidx = lax.broadcasted_iota(jnp.int32, (N_EXPERTS, tr), 0)
    e = e_ref[...]
    start = start_ref[...]
    rows = [jnp.sum(jnp.where(eidx == e[kk:kk + 1], start, 0.0), axis=0, keepdims=True) for kk in range(TOP_K)]
    dest_ref[...] = jnp.concatenate(rows, axis=0).astype(jnp.int32) + rank_ref[...]


def _slots(top_e, rank, start_rows, tile):
    n = top_e.shape[1]
    col = lambda i: (0, i)
    return pl.pallas_call(
        _slots_kernel, out_shape=jax.ShapeDtypeStruct((TOP_K, n), jnp.int32), grid=(n // tile,),
        in_specs=[pl.BlockSpec((TOP_K, tile), col), pl.BlockSpec((TOP_K, tile), col),
                  pl.BlockSpec((N_EXPERTS, 1), lambda i: (0, 0))],
        out_specs=pl.BlockSpec((TOP_K, tile), col),
        compiler_params=_params(1), name="slots",
    )(top_e, rank, start_rows)


def _sc_mesh():
    return plsc.VectorSubcoreMesh(core_axis_name="c", subcore_axis_name="s",
                                  num_cores=SC_CORES, num_subcores=SC_SUBCORES)


def _sc_pack_pairs(w):
    r, c = w.shape
    half = c // 2
    lanes = SC_LANES
    rows_per_worker = r // (SC_CORES * SC_SUBCORES)
    chunk = SC_PACK_WORDS // c

    def rne_bits(v):
        u = lax.bitcast_convert_type(v, U32)
        return u + jnp.uint32(0x7FFF) + ((u >> 16) & jnp.uint32(1))

    @functools.partial(
        pl.kernel, mesh=_sc_mesh(), out_type=jax.ShapeDtypeStruct((r, half), U32),
        scratch_types=[pltpu.VMEM((chunk, c), F32), pltpu.VMEM((chunk, half), U32)],
        compiler_params=pltpu.CompilerParams(needs_layout_passes=False),
        name="sc_pack_weights")
    def pack(w_hbm, o_hbm, buf, obuf):
        base = (lax.axis_index("s") * SC_CORES + lax.axis_index("c")) * rows_per_worker

        @pl.loop(0, rows_per_worker // chunk)
        def _(j):
            off = base + j * chunk
            pltpu.sync_copy(w_hbm.at[pl.ds(off, chunk)], buf)

            @pl.loop(0, chunk)
            def _(row):
                @pl.loop(0, half, step=lanes)
                def _(col):
                    lo = rne_bits(buf[row, pl.ds(col, lanes)]) >> 16
                    hi = rne_bits(buf[row, pl.ds(half + col, lanes)]) & jnp.uint32(0xFFFF0000)
                    obuf[row, pl.ds(col, lanes)] = lo | hi

            pltpu.sync_copy(obuf, o_hbm.at[pl.ds(off, chunk)])

    return pack(w)


def _sc_scatter_rows(rows, idx, n_out):
    n, dp = rows.shape
    copies = idx.shape[0] // n
    per_worker = n // (SC_CORES * SC_SUBCORES)
    chunk = SC_GATHER_ROWS

    @functools.partial(
        pl.kernel, mesh=_sc_mesh(), out_type=jax.ShapeDtypeStruct((n_out, dp), rows.dtype),
        scratch_types=[pltpu.VMEM((chunk,), jnp.int32), pltpu.VMEM((chunk, dp), rows.dtype)],
        name="sc_scatter_rows")
    def scatter(rows_hbm, idx_hbm, out_hbm, idx_v, rows_v):
        base = (lax.axis_index("s") * SC_CORES + lax.axis_index("c")) * per_worker

        @pl.loop(0, per_worker // chunk)
        def _(j):
            off = base + j * chunk
            pltpu.sync_copy(rows_hbm.at[pl.ds(off, chunk)], rows_v)
            for k in range(copies):
                pltpu.sync_copy(idx_hbm.at[pl.ds(k * n + off, chunk)], idx_v)
                pltpu.sync_copy(rows_v, out_hbm.at[idx_v])

    return scatter(rows, idx)


def _expert_kernel(start_ref, cnt_ref, nused_ref, xs_ref, wg_ref, wu_ref, wd_ref, y_ref,
                   wgb, wub, wdb, xbuf, ybuf, sem_in, sem_out):
    e = pl.program_id(0)
    bm, ring = EXPERT_BLOCK, EXPERT_RING
    cnt = cnt_ref[e]
    nb = (cnt + bm - 1) // bm
    g0 = start_ref[e]
    n_used = nused_ref[0]

    def x_copy(g):
        slot = g % ring
        return pltpu.make_async_copy(xs_ref.at[pl.ds(g * bm, bm)], xbuf.at[slot], sem_in.at[slot])

    def y_copy(g):
        slot = g % ring
        return pltpu.make_async_copy(ybuf.at[slot], y_ref.at[pl.ds(g * bm, bm)], sem_out.at[slot])

    @pl.when(e == 0)
    def _():
        for g in range(ring):
            @pl.when(g < n_used)
            def _():
                x_copy(g).start()

    wgb[...] = _unpack_pairs(wg_ref[...]).astype(BF16)
    wub[...] = _unpack_pairs(wu_ref[...]).astype(BF16)
    wdb[...] = _unpack_pairs(wd_ref[...]).astype(BF16)
    row = lax.broadcasted_iota(jnp.int32, (bm, xbuf.shape[2]), 0)

    def blocks(j, count):
        for b in range(count):
            g = g0 + j + b
            x_copy(g).wait()

            @pl.when(g >= ring)
            def _():
                y_copy(g - ring).wait()

        for b in range(count):
            slot = (g0 + j + b) % ring
            x = _unpack_pairs(jnp.where(row < cnt - (j + b) * bm, xbuf[slot], jnp.uint32(0))).astype(BF16)
            gate = jnp.dot(x, wgb[...], preferred_element_type=F32)
            up = jnp.dot(x, wub[...], preferred_element_type=F32)
            ybuf[slot] = _pack_pairs(
                jnp.dot((_silu(gate) * up).astype(BF16), wdb[...], preferred_element_type=F32))

        for b in range(count):
            g = g0 + j + b
            y_copy(g).start()

            @pl.when(g + ring < n_used)
            def _():
                x_copy(g + ring).start()

    group = EXPERT_GROUP

    def full_group(p, carry):
        blocks(group * p, group)
        return carry

    lax.fori_loop(0, nb // group, full_group, 0)
    done = (nb // group) * group
    size = group // 2
    while size >= 1:
        @pl.when((nb - done) % (2 * size) >= size)
        def _(done=done, size=size):
            blocks(done, size)

        done = done + jnp.where((nb - done) % (2 * size) >= size, size, 0)
        size //= 2

    @pl.when(e == pl.num_programs(0) - 1)
    def _():
        for back in range(1, ring + 1):
            @pl.when(n_used - back >= 0)
            def _():
                y_copy(n_used - back).wait()


def _experts(blk_start, counts, n_used, xs, wg, wu, wd):
    n_slots, dp = xs.shape
    n_exp, d, ff = wg.shape[0], wg.shape[1], 2 * wg.shape[2]
    bm, ring = EXPERT_BLOCK, EXPERT_RING
    wsel = lambda e, st, ct, nu: (e, 0, 0)
    grid_spec = pltpu.PrefetchScalarGridSpec(
        num_scalar_prefetch=3, grid=(n_exp,),
        in_specs=[pl.BlockSpec(memory_space=pl.ANY),
                  pl.BlockSpec((None, d, ff // 2), wsel), pl.BlockSpec((None, d, ff // 2), wsel),
                  pl.BlockSpec((None, ff, d // 2), wsel)],
        out_specs=pl.BlockSpec(memory_space=pl.ANY),
        scratch_shapes=[pltpu.VMEM((d, ff), BF16), pltpu.VMEM((d, ff), BF16), pltpu.VMEM((ff, d), BF16),
                        pltpu.VMEM((ring, bm, dp), U32), pltpu.VMEM((ring, bm, dp), U32),
                        pltpu.SemaphoreType.DMA((ring,)), pltpu.SemaphoreType.DMA((ring,))])
    return pl.pallas_call(
        _expert_kernel, out_shape=jax.ShapeDtypeStruct((n_slots, dp), U32), grid_spec=grid_spec,
        compiler_params=_params(1), name="experts",
    )(blk_start, counts, n_used, xs, wg, wu, wd)


def _sc_gather_rows(table, idx):
    n_idx = idx.shape[0]
    dp = table.shape[1]
    workers = SC_CORES * SC_SUBCORES
    per_worker = n_idx // workers
    chunk = SC_GATHER_ROWS

    @functools.partial(
        pl.kernel, mesh=_sc_mesh(), out_type=jax.ShapeDtypeStruct((n_idx, dp), table.dtype),
        scratch_types=[pltpu.VMEM((chunk,), jnp.int32), pltpu.VMEM((chunk, dp), table.dtype),
                       pltpu.SemaphoreType.DMA],
        name="sc_gather_rows")
    def gather(table_hbm, idx_hbm, out_hbm, idx_v, rows_v, sem):
        base = (lax.axis_index("s") * SC_CORES + lax.axis_index("c")) * per_worker

        @pl.loop(0, per_worker // chunk)
        def _(j):
            off = base + j * chunk
            pltpu.sync_copy(idx_hbm.at[pl.ds(off, chunk)], idx_v)
            pltpu.async_copy(table_hbm.at[idx_v], rows_v, sem).wait()
            pltpu.sync_copy(rows_v, out_hbm.at[pl.ds(off, chunk)])

    return gather(table, idx)


def _combine_kernel(tile, yt_ref, w_ref, h_ref, x1_ref, g2_ref, sg_ref, su_ref, sd_ref, lg_ref, lb_ref, *rest):
    o_ref = rest[-1]
    hb = _unpack_pairs(h_ref[...]).astype(BF16)
    g = jnp.dot(hb, sg_ref[...], preferred_element_type=F32)
    u = jnp.dot(hb, su_ref[...], preferred_element_type=F32)
    f = jnp.dot((_silu(g) * u).astype(BF16), sd_ref[...], preferred_element_type=F32)
    w = w_ref[...]
    for kk in range(TOP_K):
        f = f + w[:, kk:kk + 1] * _unpack_pairs(yt_ref[kk * tile:(kk + 1) * tile, :])
    o_ref[...] = _layer_norm(DEEPNORM_ALPHA * x1_ref[...] + g2_ref[0] * f, lg_ref[...], lb_ref[...])


def _combine(y_tok, top_w, h2p, x1, g2, sg, su, sd, lg, lb, tile, tiles_per_batch, first_tile, prev_out):
    n, d = x1.shape
    dp = h2p.shape[1]
    row = lambda i: (first_tile + i, 0)
    full = lambda i: (0, 0)

    def const(a):
        return pl.BlockSpec(a.shape, full, pipeline_mode=pl.Buffered(1))

    in_specs = [pl.BlockSpec((TOP_K * tile, dp), lambda i: (i, 0)),
                pl.BlockSpec((tile, TOP_K), row), pl.BlockSpec((tile, dp), row), pl.BlockSpec((tile, d), row),
                pl.BlockSpec((1, 1, d), lambda i: ((first_tile + i) // tiles_per_batch, 0, 0)),
                const(sg), const(su), const(sd), const(lg), const(lb)]
    args = [y_tok, top_w, h2p, x1, g2, sg, su, sd, lg, lb]
    aliases = {}
    if prev_out is not None:
        in_specs.append(pl.BlockSpec(memory_space=pl.ANY))
        args.append(prev_out)
        aliases = {len(args) - 1: 0}
    return pl.pallas_call(
        functools.partial(_combine_kernel, tile),
        out_shape=jax.ShapeDtypeStruct((n, d), F32),
        grid=(y_tok.shape[0] // (TOP_K * tile),),
        in_specs=in_specs,
        out_specs=pl.BlockSpec((tile, d), row),
        input_output_aliases=aliases,
        compiler_params=_params(1), name="combine_shared_ln2",
    )(*args)


def kernel(x, c, ctx, c_ctx, w_mod, b_mod, w_in, na_rpb, gla_w_decay_f, gla_b_decay_f, gla_w_decay_b, gla_b_decay_b,
           gla_norm_w, w_branch_a, w_branch_b, w_out, ln1_g, ln1_b, router_w, router_bias, exp_w_gate, exp_w_up,
           exp_w_down, sh_w_gate, sh_w_up, sh_w_down, ln2_g, ln2_b):
    batch, t_len, d = x.shape
    ctx_len = ctx.shape[1]
    n = batch * t_len
    assert w_mod.shape[0] == DEPTH == 1
    assert t_len % GLA_CHUNK == 0 and ctx_len == GLA_CHUNK and (t_len // GRID_W) % (NA_ROW_BLOCK * NA_BLOCKS_PER_STEP) == 0

    packed_experts = [_sc_pack_pairs(w[0].reshape(-1, w.shape[-1])).reshape(w.shape[1], w.shape[2], -1)
                      for w in (exp_w_gate, exp_w_up, exp_w_down)]

    mod_rows = 16
    c_all = jnp.zeros((mod_rows, d), F32).at[:batch].set(c).at[batch].set(c_ctx)
    mod = _modulation(c_all, w_mod[0], b_mod[0])
    sh1, sc1, g1, sh2, sc2, g2 = [mod[:batch, j * d:(j + 1) * d].reshape(batch, 1, d) for j in range(6)]
    sh1c = mod[batch:batch + 1, 0:d].reshape(1, 1, d)
    sc1c = mod[batch:batch + 1, d:2 * d].reshape(1, 1, d)

    offs = np.cumsum((0, NA_WIDTH, NA_WIDTH, NA_WIDTH, GLA_QK_WIDTH, GLA_QK_WIDTH, GLA_V_WIDTH, GLA_V_WIDTH,
                      GLA_GATE_RANK, GLA_GATE_RANK, d, d))
    qa, ka, va, qb, kb, vbc, gbc, lrf, lrb, ga, gbt = [w_in[0][:, offs[j]:offs[j + 1]] for j in range(11)]
    lr_cols = jnp.concatenate([lrf, lrb, jnp.zeros((d, LANES - 2 * GLA_GATE_RANK), F32)], axis=1)
    w_lat = jnp.concatenate([lr_cols, qa, ka, va, qb, kb, vbc, gbc, ga, gbt], axis=1).astype(BF16)
    lat_offs = np.cumsum((0, LANES, NA_WIDTH, NA_WIDTH, NA_WIDTH, GLA_QK_WIDTH, GLA_QK_WIDTH, GLA_V_WIDTH))
    w_ctx = jnp.concatenate([w_lat[:, lat_offs[j]:lat_offs[j + 1]] for j in (0, 2, 3, 5, 6)], axis=1)
    plain = ("plain",)
    lat_plan = ((LANES, (("decay",),)),
                (3 * NA_WIDTH, (("scale", NA_HEAD_DIM ** -0.5), plain, plain)),
                (2 * GLA_QK_WIDTH, (("rope", GLA_DK ** -0.5), ("rope", 1.0))),
                (GLA_V_WIDTH, (plain, plain)), (GLA_V_WIDTH, (plain, plain)),
                (2 * d, (plain,) * 4))
    ctx_plan = ((LANES, (("decay",),)),
                (2 * NA_WIDTH, (plain, plain)), (GLA_QK_WIDTH, (plain,)), (GLA_V_WIDTH, (plain, plain)))
    w2, b2 = _decay_weights(gla_w_decay_f[0], gla_b_decay_f[0], gla_w_decay_b[0], gla_b_decay_b[0])
    tile = COMBINE_TILE
    x2d = x.reshape(n, d)
    decay, na_qkv, gla_qk, vb, gb, gates = _projection(
        x2d, sc1, sh1, w_lat, w2, b2, lat_plan, (BF16,) * 6, PROJ_TILE, t_len // PROJ_TILE,
        rope=_rope_tables(t_len))
    decay_c, na_kv_c, k_c, v_c = _projection(
        ctx.reshape(batch * ctx_len, d), sc1c, sh1c, w_ctx, w2, b2, ctx_plan, (BF16,) * 4, tile,
        batch * ctx_len // tile)

    out_a = _na_attention(na_qkv, na_kv_c, _na_bias_tables(na_rpb[0]), batch, t_len, ctx_len)
    o_f, o_b = _gla(gla_qk, vb, decay, k_c, v_c, decay_c, batch, t_len, ctx_len)

    x1, h2p, scores_t = _merge(
        out_a.reshape(n, NA_WIDTH), o_f.reshape(n, GLA_V_WIDTH), o_b.reshape(n, GLA_V_WIDTH), gb, gates, x2d,
        g1, sc2, sh2, w_branch_a[0].astype(BF16), w_branch_b[0].astype(BF16), w_out[0].astype(BF16),
        gla_norm_w[0].reshape(1, -1), ln1_g[0].reshape(1, d), ln1_b[0].reshape(1, d),
        router_w[0].T.astype(BF16), PROJ_TILE, t_len // PROJ_TILE)

    top_e, top_w, rank, counts = _route(scores_t, router_bias[0], 512)

    counts = counts[:, 0]
    n_blocks = pl.cdiv(n * TOP_K, EXPERT_BLOCK) + N_EXPERTS
    blocks_per = (counts + EXPERT_BLOCK - 1) // EXPERT_BLOCK
    blk_end = jnp.cumsum(blocks_per)
    blk_start = blk_end - blocks_per
    dest = _slots(top_e, rank, (blk_start * EXPERT_BLOCK).astype(F32).reshape(N_EXPERTS, 1), 2048)

    xs = _sc_scatter_rows(h2p, dest.reshape(TOP_K * n), n_blocks * EXPERT_BLOCK)
    y = _experts(blk_start.astype(jnp.int32), counts, blk_end[-1:].astype(jnp.int32), xs, *packed_experts)
    dest_tok = dest.reshape(TOP_K, n // tile, tile).transpose(1, 0, 2).reshape(COMBINE_CHUNKS, -1)
    w_rows = top_w
    shared = (sh_w_gate[0].astype(BF16), sh_w_up[0].astype(BF16), sh_w_down[0].astype(BF16),
              ln2_g[0].reshape(1, d), ln2_b[0].reshape(1, d))
    tiles_per_chunk = n // tile // COMBINE_CHUNKS
    out = None
    for ci in range(COMBINE_CHUNKS):
        y_tok = _sc_gather_rows(y, dest_tok[ci])
        out = _combine(y_tok, w_rows, h2p, x1, g2, *shared, tile, t_len // tile, ci * tiles_per_chunk, out)
    return out.reshape(batch, t_len, d)
```

```python
import functools

import numpy as np
import jax
import jax.numpy as jnp
from jax import lax
from jax.experimental import pallas as pl
from jax.experimental.pallas import tpu as pltpu
from jax.experimental.pallas import tpu_sc as plsc

F32 = jnp.float32
BF16 = jnp.bfloat16
U32 = jnp.uint32
HIGHEST = lax.Precision.HIGHEST

GRID_W = 64
NA_HEADS = 8
NA_HEAD_DIM = 64
NA_WIDTH = NA_HEADS * NA_HEAD_DIM
WIN_ROWS = 8
WIN_COLS = 16
GLA_HEADS = 4
GLA_DK = 128
GLA_DV = 256
GLA_QK_WIDTH = GLA_HEADS * GLA_DK
GLA_V_WIDTH = GLA_HEADS * GLA_DV
GLA_GATE_RANK = 16
GLA_TAU = 16.0
LOG2E = 1.4426950408889634
ROPE_BASE = 10000.0
N_EXPERTS = 256
TOP_K = 8
N_GROUPS = 8
TOPK_GROUPS = 4
GROUP_SIZE = N_EXPERTS // N_GROUPS
ROUTED_SCALE = 2.5
DEPTH = 1
DEEPNORM_ALPHA = (2 * DEPTH) ** 0.25
EPS = 1e-6

LANES = 128
PROJ_TILE = 512
NA_ROW_BLOCK = 4
NA_BLOCKS_PER_STEP = 2
NA_UNION_ROWS = NA_ROW_BLOCK + WIN_ROWS - 1
GLA_CHUNK = 256
GLA_DIAG = 16
GLA_HEADS_PER_STEP = 4
EXPERT_BLOCK = 272
EXPERT_GROUP = 4
EXPERT_RING = 8
COMBINE_CHUNKS = 8
COMBINE_TILE = 512
SC_CORES = 2
SC_SUBCORES = 16
SC_GATHER_ROWS = 128
NEG_BIG = -1e30
VMEM_LIMIT = 56 * 1024 * 1024
NA_VMEM_LIMIT = 58 * 1024 * 1024


def _params(n_axes, vmem=VMEM_LIMIT):
    return pltpu.CompilerParams(dimension_semantics=("arbitrary",) * n_axes, vmem_limit_bytes=vmem)


def _sigmoid(v):
    return 1.0 / (1.0 + jnp.exp2(v * (-LOG2E)))


def _silu(v):
    return v * _sigmoid(v)


def _interleave(chains):
    results = [None] * len(chains)
    active = list(enumerate(chains))
    while active:
        still = []
        for i, chain in active:
            try:
                next(chain)
                still.append((i, chain))
            except StopIteration as stop:
                results[i] = stop.value
        active = still
    return results


def _pack_pairs(v):
    m = v.shape[1] // 2
    lo = lax.bitcast_convert_type(v[:, :m].astype(BF16).astype(F32), U32) >> 16
    hi = lax.bitcast_convert_type(v[:, m:].astype(BF16).astype(F32), U32) & jnp.uint32(0xFFFF0000)
    return lo | hi


def _unpack_pairs(p):
    lo = lax.bitcast_convert_type(p << 16, F32)
    hi = lax.bitcast_convert_type(p & jnp.uint32(0xFFFF0000), F32)
    return jnp.concatenate([lo, hi], axis=1)


def _mod_kernel(c_ref, w_ref, b_ref, o_ref):
    o_ref[...] = jnp.dot(_silu(c_ref[...]), w_ref[...], preferred_element_type=F32, precision=HIGHEST) + b_ref[...]


def _modulation(c_all, w_mod, b_mod):
    rows, d = c_all.shape
    n = w_mod.shape[1]
    bn = 512
    return pl.pallas_call(
        _mod_kernel,
        out_shape=jax.ShapeDtypeStruct((rows, n), F32),
        grid=(n // bn,),
        in_specs=[pl.BlockSpec((rows, d), lambda j: (0, 0)),
                  pl.BlockSpec((d, bn), lambda j: (0, j)),
                  pl.BlockSpec((1, bn), lambda j: (0, j))],
        out_specs=pl.BlockSpec((rows, bn), lambda j: (0, j)),
        compiler_params=_params(1),
        name="modulation",
    )(c_all, w_mod, b_mod.reshape(1, n))


def _swap32(v):
    lane = lax.broadcasted_iota(jnp.int32, v.shape, 1)
    return jnp.where((lane % 64) < 32, pltpu.roll(v, 96, 1), pltpu.roll(v, 32, 1))


GLA_DECAY_WIDTH = 2 * GLA_HEADS * 2 * GLA_DK


def _log2_decay_split(lr, w2, b2, out_ref):
    z = (jnp.dot(lr.astype(BF16), w2, preferred_element_type=F32) + b2) * LOG2E
    a = (jnp.minimum(z, 0.0) - jnp.log2(1.0 + jnp.exp2(-jnp.abs(z)))) * (1.0 / GLA_TAU)
    hi = a.astype(BF16)
    lo = (a - hi.astype(F32)).astype(BF16)
    for p in range(a.shape[1] // GLA_DK):
        src = slice(p * GLA_DK, (p + 1) * GLA_DK)
        out_ref[:, 2 * p * GLA_DK:(2 * p + 1) * GLA_DK] = hi[:, src]
        out_ref[:, (2 * p + 1) * GLA_DK:(2 * p + 2) * GLA_DK] = lo[:, src]


def _proj_kernel(plan, has_rope, *refs):
    x_ref, sc_ref, sh_ref, w_ref, w2_ref, b2_ref = refs[:6]
    pos = 6
    if has_rope:
        cos_ref, sin_ref = refs[6:8]
        pos = 8
    out_refs = refs[pos:]
    h = (x_ref[...] * (1.0 + sc_ref[0]) + sh_ref[0]).astype(BF16)
    col = 0
    for out_ref, (width, kinds) in zip(out_refs, plan):
        for j, kind in enumerate(kinds):
            cw = width // len(kinds)
            c0 = j * cw
            acc = jnp.dot(h, w_ref[:, col + c0:col + c0 + cw], preferred_element_type=F32)
            if kind[0] == "decay":
                _log2_decay_split(acc, w2_ref[...], b2_ref[...], out_ref)
                continue
            if kind[0] == "scale":
                acc = acc * kind[1]
            elif kind[0] == "rope":
                cos, sin = cos_ref[...], sin_ref[...]
                pieces = []
                for p in range(cw // LANES):
                    v = acc[:, p * LANES:(p + 1) * LANES]
                    pieces.append((v * cos + _swap32(v) * sin) * kind[1])
                acc = jnp.concatenate(pieces, axis=1)
            out_ref[:, c0:c0 + cw] = acc.astype(out_ref.dtype)
        col += width


def _projection(x2d, sc, sh, w, w2, b2, plan, out_dtypes, tile, tiles_per_mod, rope=None):
    n, d = x2d.shape
    const = lambda a: pl.BlockSpec(a.shape, lambda i: (0, 0), pipeline_mode=pl.Buffered(1))
    in_specs = [pl.BlockSpec((tile, d), lambda i: (i, 0)),
                pl.BlockSpec((1, 1, d), lambda i: (i // tiles_per_mod, 0, 0)),
                pl.BlockSpec((1, 1, d), lambda i: (i // tiles_per_mod, 0, 0)),
                const(w), const(w2), const(b2)]
    args = [x2d, sc, sh, w, w2, b2]
    if rope is not None:
        in_specs += [pl.BlockSpec((tile, LANES), lambda i: (i % tiles_per_mod, 0))] * 2
        args += list(rope)
    widths = [GLA_DECAY_WIDTH if kinds[0][0] == "decay" else wd for wd, kinds in plan]
    out_shape = [jax.ShapeDtypeStruct((n, wd), dt) for wd, dt in zip(widths, out_dtypes)]
    out_specs = [pl.BlockSpec((tile, wd), lambda i: (i, 0)) for wd in widths]
    return pl.pallas_call(
        functools.partial(_proj_kernel, plan, rope is not None),
        out_shape=out_shape,
        grid=(n // tile,),
        in_specs=in_specs,
        out_specs=out_specs,
        compiler_params=_params(1),
        name="in_proj" if rope is not None else "ctx_proj",
    )(*args)


def _rope_tables(t_len):
    half = GLA_DK // 2
    quarter = half // 2
    f32 = np.float32
    inv_freq = f32(ROPE_BASE) ** (-np.arange(quarter, dtype=f32) / f32(quarter))
    pos = np.arange(t_len)
    row_ang = (pos // GRID_W).astype(f32)[:, None] * inv_freq[None, :]
    col_ang = (pos % GRID_W).astype(f32)[:, None] * inv_freq[None, :]
    cr, sr, cc, sn = np.cos(row_ang), np.sin(row_ang), np.cos(col_ang), np.sin(col_ang)
    cos = np.concatenate([cr, cr, cc, cc], axis=1).astype(f32)
    sin = np.concatenate([-sr, sr, -sn, sn], axis=1).astype(f32)
    return jnp.asarray(cos), jnp.asarray(sin)


def _na_bias_tables(rpb):
    rb, ur, w = NA_ROW_BLOCK, NA_UNION_ROWS, GRID_W
    heads = rpb.shape[0]
    pad = jnp.pad(rpb, ((0, 0), (0, 0), (w, w)))
    toep = jnp.stack([pad[:, :, w + WIN_COLS - 1 - c:2 * w + WIN_COLS - 1 - c] for c in range(w)], axis=2)
    c = np.arange(w)[:, None]
    kc = np.arange(w)[None, :]
    col_start = np.clip(c - WIN_COLS // 2, 0, w - WIN_COLS)
    col_ok = (kc >= col_start) & (kc < col_start + WIN_COLS)
    toep = jnp.where(col_ok[None, None], toep, NEG_BIG)
    neg = jnp.full((heads, w, w), NEG_BIG, F32)
    half = WIN_ROWS // 2
    tables = []
    for lo, off in ((lambda i: 0, WIN_ROWS - 1), (lambda i: i, WIN_ROWS - 1 - half), (lambda i: ur - WIN_ROWS, 0)):
        rows_ = []
        for i in range(rb):
            blocks = [toep[:, j - i + off] if lo(i) <= j < lo(i) + WIN_ROWS else neg for j in range(ur)]
            rows_.append(jnp.concatenate(blocks, axis=2))
        tables.append(jnp.concatenate(rows_, axis=1))
    return jnp.stack(tables)


def _na_kernel(rows, q_ref, k_ref, v_ref, kc_ref, vc_ref, *rest):
    bias_refs, o_ref = rest[:-1], rest[-1]
    tq = NA_ROW_BLOCK * GRID_W
    nk = NA_UNION_ROWS * GRID_W
    nt = (((1,), (1,)), ((), ()))
    lane = lax.broadcasted_iota(jnp.int32, (tq, LANES), 1)
    def head(qm, ku, vu, kc, vc, bias):
        s_loc = lax.dot_general(qm, ku, nt, preferred_element_type=F32) + bias
        s_ctx = lax.dot_general(qm, kc, nt, preferred_element_type=F32)
        yield
        m = jnp.maximum(jnp.max(s_loc, axis=1, keepdims=True), jnp.max(s_ctx, axis=1, keepdims=True))
        p_loc = jnp.exp(s_loc - m)
        p_ctx = jnp.exp(s_ctx - m)
        yield
        denom = jnp.sum(p_loc, axis=1, keepdims=True) + jnp.sum(p_ctx, axis=1, keepdims=True)
        o = (jnp.dot(p_loc.astype(BF16), vu, preferred_element_type=F32)
             + jnp.dot(p_ctx.astype(BF16), vc, preferred_element_type=F32))
        yield
        return o / denom

    for sub, bias_ref in enumerate(bias_refs):
        rb = pl.program_id(1) * len(bias_refs) + sub
        ustart = jnp.clip(rb * NA_ROW_BLOCK - WIN_ROWS // 2, 0, rows - NA_UNION_ROWS)
        k0 = pl.multiple_of(ustart * GRID_W, GRID_W)
        qrows = slice(sub * tq, (sub + 1) * tq)
        chains = []
        for p in range(NA_WIDTH // LANES):
            ls = slice(p * LANES, (p + 1) * LANES)
            q = q_ref[qrows, ls]
            ku = k_ref[pl.ds(k0, nk), ls]
            vu = v_ref[pl.ds(k0, nk), ls]
            for hh in range(LANES // NA_HEAD_DIM):
                qm = jnp.where((lane // NA_HEAD_DIM) == hh, q, jnp.zeros_like(q))
                chains.append(head(qm, ku, vu, kc_ref[:, ls], vc_ref[:, ls], bias_ref[p * 2 + hh]))
        outs = _interleave(chains)
        for p in range(NA_WIDTH // LANES):
            out = jnp.where((lane // NA_HEAD_DIM) == 0, outs[2 * p], outs[2 * p + 1])
            o_ref[qrows, p * LANES:(p + 1) * LANES] = out.astype(o_ref.dtype)


def _na_attention(na_qkv, na_kv_ctx, bias_tables, batch, t_len, ctx_len):
    rows = t_len // GRID_W
    n_rb = rows // NA_ROW_BLOCK
    per = NA_BLOCKS_PER_STEP
    tq = per * NA_ROW_BLOCK * GRID_W
    qkv = na_qkv.reshape(batch, t_len, 3 * NA_WIDTH)
    kvc = na_kv_ctx.reshape(batch, ctx_len, 2 * NA_WIDTH)

    def bias_spec(sub):
        def idx(b, r):
            rb = r * per + sub
            return (jnp.where(rb == 0, 0, jnp.where(rb == n_rb - 1, 2, 1)), 0, 0, 0)
        return pl.BlockSpec((None,) + bias_tables.shape[1:], idx)

    return pl.pallas_call(
        functools.partial(_na_kernel, rows),
        out_shape=jax.ShapeDtypeStruct((batch, t_len, NA_WIDTH), BF16),
        grid=(batch, n_rb // per),
        in_specs=[pl.BlockSpec((None, tq, NA_WIDTH), lambda b, r: (b, r, 0)),
                  pl.BlockSpec((None, t_len, NA_WIDTH), lambda b, r: (b, 0, 1)),
                  pl.BlockSpec((None, t_len, NA_WIDTH), lambda b, r: (b, 0, 2)),
                  pl.BlockSpec((None, ctx_len, NA_WIDTH), lambda b, r: (b, 0, 0)),
                  pl.BlockSpec((None, ctx_len, NA_WIDTH), lambda b, r: (b, 0, 1))]
                 + [bias_spec(sub) for sub in range(per)],
        out_specs=pl.BlockSpec((None, tq, NA_WIDTH), lambda b, r: (b, r, 0)),
        compiler_params=_params(2, vmem=NA_VMEM_LIMIT),
        name="na_attention",
    )(qkv, qkv, qkv, kvc, kvc, *([bias_tables] * per))


def _gla_constants(c):
    tris, masks = [], []
    for reverse in (False, True):
        i = np.arange(c)[:, None]
        j = np.arange(c)[None, :]
        tris.append((j >= i) if reverse else (j <= i))
        i = np.arange(c // 2)[:, None]
        j = np.arange(c // 2)[None, :]
        if reverse:
            i, j = j, i
        level = []
        s = c // 4
        while s >= GLA_DIAG:
            level.append(((i // (2 * s)) == (j // (2 * s))) & ((i % (2 * s)) >= s) & ((j % (2 * s)) < s))
            s //= 2
        level.append(((i // GLA_DIAG) == (j // GLA_DIAG)) & (j <= i))
        masks.append(np.stack(level))
    return jnp.asarray(np.stack(tris), BF16), jnp.asarray(np.stack(masks), F32)


def _block_refs(cum, s, reverse, diag):
    c = cum.shape[0]
    span = s if diag else 2 * s
    parts = []
    for p in range(c // span):
        if diag:
            r = p * span + (span - 1 if reverse else 0)
        else:
            r = p * span + (s - 1 if reverse else s)
        parts.append(jnp.broadcast_to(cum[r:r + 1, :], (span, cum.shape[1])))
    return jnp.concatenate(parts, axis=0)


def _cumsum_rows(a_hl, tri):
    parts = jnp.dot(tri, a_hl, preferred_element_type=F32)
    w = a_hl.shape[1] // 2
    return parts[:, :w] + parts[:, w:]


def _gla_chunk(q, k, v, a, state_t, tri, mask_ref, reverse, want_out):
    c = k.shape[0]
    hc = c // 2
    nt = (((1,), (1,)), ((), ()))
    k = k.astype(F32)
    q = q.astype(F32) if want_out else None
    cum = _cumsum_rows(a, tri)
    last = 0 if reverse else c - 1
    total = cum[last:last + 1, :]
    yield

    def scale(x, log2_factor):
        return (x * jnp.exp2(log2_factor)).astype(BF16)

    out = None
    if want_out:
        out = lax.dot_general(scale(q, cum), state_t.astype(BF16), nt, preferred_element_type=F32)
        halves = (slice(hc, c), slice(0, hc)) if reverse else (slice(0, hc), slice(hc, c))
        early, late = halves
        r = hc - 1 if reverse else hc
        g = cum[r:r + 1, :]
        cross = lax.dot_general(scale(q[late], cum[late] - g), scale(k[early], g - cum[early]), nt,
                                preferred_element_type=F32)
        yield
        inner = []
        for rows in halves:
            cx, qx, kx = cum[rows], q[rows], k[rows]
            acc = jnp.zeros((hc, hc), F32)
            s = hc // 2
            level = 0
            while True:
                diag = s < GLA_DIAG
                d = cx - _block_refs(cx, GLA_DIAG if diag else s, reverse, diag)
                if diag:
                    qs, ks = scale(qx, d), scale(kx, -d)
                else:
                    e = jnp.exp2(-jnp.abs(d))
                    qs, ks = (qx * e).astype(BF16), (kx * e).astype(BF16)
                acc = acc + lax.dot_general(qs, ks, nt, preferred_element_type=F32) * mask_ref[level]
                yield
                if diag:
                    break
                s //= 2
                level += 1
            inner.append(acc)
        out_early = jnp.dot(inner[0].astype(BF16), v[early], preferred_element_type=F32)
        out_late = jnp.dot(jnp.concatenate([cross, inner[1]], axis=1).astype(BF16),
                           jnp.concatenate([v[early], v[late]], axis=0), preferred_element_type=F32)
        intra = [out_late, out_early] if reverse else [out_early, out_late]
        out = out + jnp.concatenate(intra, axis=0)
        yield
    upd = lax.dot_general(v, scale(k, total - cum), (((0,), (0,)), ((), ())), preferred_element_type=F32)
    return out, jnp.exp2(total) * state_t + upd


def _gla_kernel(qf_ref, kf_ref, vf_ref, af_ref, qb_ref, kb_ref, vb_ref, ab_ref,
                kc_ref, vc_ref, acf_ref, acb_ref, tri_ref, mask_ref,
                of_ref, ob_ref, sf_ref, sb_ref):
    c = pl.program_id(2)
    fm, bm = mask_ref.at[0], mask_ref.at[1]
    heads = sf_ref.shape[0]

    @pl.when(c == 0)
    def _():
        zero = jnp.zeros(sf_ref.shape[1:], F32)
        chains = []
        for h in range(heads):
            kc = kc_ref[:, h * GLA_DK:(h + 1) * GLA_DK]
            vc = vc_ref[:, h * GLA_DV:(h + 1) * GLA_DV]
            hl = slice(2 * h * GLA_DK, 2 * (h + 1) * GLA_DK)
            chains.append(_gla_chunk(None, kc, vc, acf_ref[:, hl], zero, tri_ref[0], fm, False, False))
            chains.append(_gla_chunk(None, kc, vc, acb_ref[:, hl], zero, tri_ref[1], bm, True, False))
        for i, (_, state) in enumerate(_interleave(chains)):
            (sb_ref if i % 2 else sf_ref)[i // 2] = state

    @pl.when(c > 0)
    def _():
        chains = []
        for h in range(heads):
            ks = slice(h * GLA_DK, (h + 1) * GLA_DK)
            vs = slice(h * GLA_DV, (h + 1) * GLA_DV)
            hl = slice(2 * h * GLA_DK, 2 * (h + 1) * GLA_DK)
            chains.append(_gla_chunk(qf_ref[:, ks], kf_ref[:, ks], vf_ref[:, vs], af_ref[:, hl], sf_ref[h],
                                     tri_ref[0], fm, False, True))
            chains.append(_gla_chunk(qb_ref[:, ks], kb_ref[:, ks], vb_ref[:, vs], ab_ref[:, hl], sb_ref[h],
                                     tri_ref[1], bm, True, True))
        for i, (o, state) in enumerate(_interleave(chains)):
            h = i // 2
            vs = slice(h * GLA_DV, (h + 1) * GLA_DV)
            if i % 2:
                ob_ref[:, vs] = o.astype(ob_ref.dtype)
                sb_ref[h] = state
            else:
                of_ref[:, vs] = o.astype(of_ref.dtype)
                sf_ref[h] = state


def _gla(gla_qk, vb, decay, k_ctx, v_ctx, decay_ctx, batch, t_len, ctx_len):
    nc = t_len // GLA_CHUNK
    h = GLA_HEADS
    qk = gla_qk.reshape(batch, t_len, 2 * GLA_QK_WIDTH)
    v3 = vb.reshape(batch, t_len, GLA_V_WIDTH)
    a3 = decay.reshape(batch, t_len, GLA_DECAY_WIDTH)
    kc3 = k_ctx.reshape(batch, ctx_len, GLA_QK_WIDTH)
    vc3 = v_ctx.reshape(batch, ctx_len, GLA_V_WIDTH)
    ac3 = decay_ctx.reshape(batch, ctx_len, GLA_DECAY_WIDTH)
    tri, masks = _gla_constants(GLA_CHUNK)

    def fwd(c):
        return jnp.maximum(c - 1, 0)

    def bwd(c):
        return nc - 1 - jnp.maximum(c - 1, 0)

    def const(a):
        return pl.BlockSpec(a.shape, lambda b, hh, c: (0,) * a.ndim, pipeline_mode=pl.Buffered(1))

    hp = GLA_HEADS_PER_STEP
    groups = h // hp
    cq = (None, GLA_CHUNK, hp * GLA_DK)
    cv = (None, GLA_CHUNK, hp * GLA_DV)
    ca = (None, GLA_CHUNK, hp * 2 * GLA_DK)
    cca = (None, ctx_len, hp * 2 * GLA_DK)
    in_specs = [
        pl.BlockSpec(cq, lambda b, g, c: (b, fwd(c), g)),
        pl.BlockSpec(cq, lambda b, g, c: (b, fwd(c), groups + g)),
        pl.BlockSpec(cv, lambda b, g, c: (b, fwd(c), g)),
        pl.BlockSpec(ca, lambda b, g, c: (b, fwd(c), g)),
        pl.BlockSpec(cq, lambda b, g, c: (b, bwd(c), g)),
        pl.BlockSpec(cq, lambda b, g, c: (b, bwd(c), groups + g)),
        pl.BlockSpec(cv, lambda b, g, c: (b, bwd(c), g)),
        pl.BlockSpec(ca, lambda b, g, c: (b, bwd(c), groups + g)),
        pl.BlockSpec((None, ctx_len, hp * GLA_DK), lambda b, g, c: (b, 0, g)),
        pl.BlockSpec((None, ctx_len, hp * GLA_DV), lambda b, g, c: (b, 0, g)),
        pl.BlockSpec(cca, lambda b, g, c: (b, 0, g)),
        pl.BlockSpec(cca, lambda b, g, c: (b, 0, groups + g)),
        const(tri), const(masks),
    ]
    out_specs = [pl.BlockSpec(cv, lambda b, g, c: (b, fwd(c), g)),
                 pl.BlockSpec(cv, lambda b, g, c: (b, bwd(c), g))]
    out_shape = [jax.ShapeDtypeStruct((batch, t_len, GLA_V_WIDTH), BF16)] * 2
    return pl.pallas_call(
        _gla_kernel,
        out_shape=out_shape,
        grid=(batch, groups, nc + 1),
        in_specs=in_specs,
        out_specs=out_specs,
        scratch_shapes=[pltpu.VMEM((hp, GLA_DV, GLA_DK), F32), pltpu.VMEM((hp, GLA_DV, GLA_DK), F32)],
        compiler_params=_params(3),
        name="gla",
    )(qk, qk, v3, a3, qk, qk, v3, a3, kc3, vc3, ac3, ac3, tri, masks)


def _decay_weights(w_f, b_f, w_b, b_b):
    r, width = w_f.shape
    w2 = jnp.zeros((LANES, 2 * width), F32).at[:r, :width].set(w_f).at[r:2 * r, width:].set(w_b)
    return w2.astype(BF16), jnp.concatenate([b_f, b_b]).reshape(1, 2 * width)


def _layer_norm(v, g, b):
    mu = jnp.mean(v, axis=1, keepdims=True)
    var = jnp.mean(jnp.square(v - mu), axis=1, keepdims=True)
    return (v - mu) * lax.rsqrt(var + EPS) * g + b


def _merge_kernel(oa_ref, of_ref, ob_ref, gb_ref, gates_ref, x_ref, g1_ref, sc2_ref, sh2_ref,
                  wa_ref, wb_ref, wo_ref, nw_ref, lg_ref, lb_ref, rw_ref,
                  x1_ref, h2_ref, sc_ref):
    d = x_ref.shape[1]
    o = of_ref[...].astype(F32) + ob_ref[...].astype(F32)
    pieces = []
    for hh in range(GLA_HEADS):
        oh = o[:, hh * GLA_DV:(hh + 1) * GLA_DV]
        pieces.append(oh * lax.rsqrt(jnp.mean(jnp.square(oh), axis=1, keepdims=True) + EPS))
    out_b = jnp.concatenate(pieces, axis=1) * nw_ref[...] * _silu(gb_ref[...].astype(F32))
    ya = jnp.dot(oa_ref[...], wa_ref[...], preferred_element_type=F32)
    yb = jnp.dot(out_b.astype(BF16), wb_ref[...], preferred_element_type=F32)
    y = _sigmoid(gates_ref[:, :d].astype(F32)) * ya + _sigmoid(gates_ref[:, d:].astype(F32)) * yb
    y2 = jnp.dot(y.astype(BF16), wo_ref[...], preferred_element_type=F32)
    x1 = _layer_norm(DEEPNORM_ALPHA * x_ref[...] + g1_ref[0] * y2, lg_ref[...], lb_ref[...])
    x1_ref[...] = x1
    h2 = x1 * (1.0 + sc2_ref[0]) + sh2_ref[0]
    h2_ref[...] = _pack_pairs(h2)
    logits_t = lax.dot_general(rw_ref[...], h2.astype(BF16), (((1,), (1,)), ((), ())), preferred_element_type=F32)
    sc_ref[...] = _sigmoid(logits_t)


def _merge(out_a, o_f, o_b, gb, gates, x2d, g1, sc2, sh2, wa, wb, wo, nw, lg, lb, rw_t, tile, tiles_per_batch):
    n, d = x2d.shape
    row = lambda i: (i, 0)
    mod = lambda i: (i // tiles_per_batch, 0, 0)
    full = lambda i: (0, 0)

    def const(a):
        return pl.BlockSpec(a.shape, full, pipeline_mode=pl.Buffered(1))

    in_specs = [pl.BlockSpec((tile, NA_WIDTH), row), pl.BlockSpec((tile, GLA_V_WIDTH), row),
                pl.BlockSpec((tile, GLA_V_WIDTH), row), pl.BlockSpec((tile, GLA_V_WIDTH), row),
                pl.BlockSpec((tile, 2 * d), row), pl.BlockSpec((tile, d), row),
                pl.BlockSpec((1, 1, d), mod), pl.BlockSpec((1, 1, d), mod), pl.BlockSpec((1, 1, d), mod),
                const(wa), const(wb), const(wo), const(nw), const(lg), const(lb), const(rw_t)]
    out_shape = [jax.ShapeDtypeStruct((n, d), F32), jax.ShapeDtypeStruct((n, d // 2), U32),
                 jax.ShapeDtypeStruct((N_EXPERTS, n), F32)]
    out_specs = [pl.BlockSpec((tile, d), row), pl.BlockSpec((tile, d // 2), row),
                 pl.BlockSpec((N_EXPERTS, tile), lambda i: (0, i))]
    return pl.pallas_call(
        _merge_kernel, out_shape=out_shape, grid=(n // tile,), in_specs=in_specs, out_specs=out_specs,
        compiler_params=_params(1), name="merge_ln1_router",
    )(out_a, o_f, o_b, gb, gates, x2d, g1, sc2, sh2, wa, wb, wo, nw, lg, lb, rw_t)


def _first_argmax(vals, idx, n):
    m = jnp.max(vals, axis=0, keepdims=True)
    first = jnp.min(jnp.where(vals == m, idx, float(n)), axis=0, keepdims=True)
    return m, first


def _route_kernel(sc_ref, bias_ref, before_ref, e_ref, w_ref, rank_ref, cnt_ref, carry_ref):
    step = pl.program_id(0)
    tr = sc_ref.shape[1]

    @pl.when(step == 0)
    def _():
        carry_ref[...] = jnp.zeros(carry_ref.shape, F32)

    scores = sc_ref[...]
    biased = scores + bias_ref[...]
    eidx = lax.broadcasted_iota(jnp.int32, (N_EXPERTS, tr), 0).astype(F32)
    lidx = lax.broadcasted_iota(jnp.int32, (GROUP_SIZE, tr), 0).astype(F32)
    gidx = lax.broadcasted_iota(jnp.int32, (N_GROUPS, tr), 0).astype(F32)
    gs = []
    for g in range(N_GROUPS):
        blk = biased[g * GROUP_SIZE:(g + 1) * GROUP_SIZE]
        m1, first = _first_argmax(blk, lidx, GROUP_SIZE)
        m2 = jnp.max(jnp.where(lidx == first, -jnp.inf, blk), axis=0, keepdims=True)
        gs.append(m1 + m2)
    cur = jnp.concatenate(gs, axis=0)
    keep = jnp.zeros((N_GROUPS, tr), F32)
    for _ in range(TOPK_GROUPS):
        _, first = _first_argmax(cur, gidx, N_GROUPS)
        sel = gidx == first
        keep = jnp.where(sel, 1.0, keep)
        cur = jnp.where(sel, -jnp.inf, cur)
    keep_e = jnp.concatenate([jnp.broadcast_to(keep[g:g + 1], (GROUP_SIZE, tr)) for g in range(N_GROUPS)], axis=0)
    masked = jnp.where(keep_e > 0.5, biased, -jnp.inf)
    chosen = jnp.zeros((N_EXPERTS, tr), F32)
    tops, topi = [], []
    for _ in range(TOP_K):
        _, first = _first_argmax(masked, eidx, N_EXPERTS)
        sel = eidx == first
        tops.append(jnp.sum(jnp.where(sel, scores, 0.0), axis=0, keepdims=True))
        topi.append(first)
        chosen = jnp.where(sel, 1.0, chosen)
        masked = jnp.where(sel, -jnp.inf, masked)
    top_s = jnp.concatenate(tops, axis=0)
    top_i = jnp.concatenate(topi, axis=0)
    e_ref[...] = top_i.astype(jnp.int32)
    w_ref[...] = (top_s / jnp.sum(top_s, axis=0, keepdims=True) * ROUTED_SCALE).T
    prior = jnp.dot(chosen.astype(BF16), before_ref[...], preferred_element_type=F32) + carry_ref[...]
    ranks = [jnp.sum(jnp.where(eidx == topi[kk], prior, 0.0), axis=0, keepdims=True) for kk in range(TOP_K)]
    rank_ref[...] = jnp.concatenate(ranks, axis=0).astype(jnp.int32)
    carry_ref[...] = carry_ref[...] + jnp.sum(chosen, axis=1, keepdims=True)
    cnt_ref[...] = jnp.broadcast_to(carry_ref[...], cnt_ref.shape).astype(jnp.int32)


def _route(scores_t, router_bias, tile):
    n = scores_t.shape[1]
    col = lambda i: (0, i)
    out_shape = [jax.ShapeDtypeStruct((TOP_K, n), jnp.int32), jax.ShapeDtypeStruct((n, TOP_K), F32),
                 jax.ShapeDtypeStruct((TOP_K, n), jnp.int32), jax.ShapeDtypeStruct((N_EXPERTS, LANES), jnp.int32)]
    before = jnp.asarray(np.arange(tile)[:, None] < np.arange(tile)[None, :], BF16)
    return pl.pallas_call(
        _route_kernel, out_shape=out_shape, grid=(n // tile,),
        in_specs=[pl.BlockSpec((N_EXPERTS, tile), col), pl.BlockSpec((N_EXPERTS, 1), lambda i: (0, 0)),
                  pl.BlockSpec((tile, tile), lambda i: (0, 0), pipeline_mode=pl.Buffered(1))],
        out_specs=[pl.BlockSpec((TOP_K, tile), col), pl.BlockSpec((tile, TOP_K), lambda i: (i, 0)),
                   pl.BlockSpec((TOP_K, tile), col), pl.BlockSpec((N_EXPERTS, LANES), lambda i: (0, 0))],
        scratch_shapes=[pltpu.VMEM((N_EXPERTS, 1), F32)],
        compiler_params=_params(1), name="route",
    )(scores_t, router_bias.reshape(N_EXPERTS, 1), before)


def _slots_kernel(e_ref, rank_ref, start_ref, dest_ref):
    tr = e_ref.shape[1]
    eidx = lax.broadcasted_iota(jnp.int32, (N_EXPERTS, tr), 0)
    e = e_ref[...]
    start = start_ref[...]
    rows = [jnp.sum(jnp.where(eidx == e[kk:kk + 1], start, 0.0), axis=0, keepdims=True) for kk in range(TOP_K)]
    dest_ref[...] = jnp.concatenate(rows, axis=0).astype(jnp.int32) + rank_ref[...]


def _slots(top_e, rank, start_rows, tile):
    n = top_e.shape[1]
    col = lambda i: (0, i)
    return pl.pallas_call(
        _slots_kernel, out_shape=jax.ShapeDtypeStruct((TOP_K, n), jnp.int32), grid=(n // tile,),
        in_specs=[pl.BlockSpec((TOP_K, tile), col), pl.BlockSpec((TOP_K, tile), col),
                  pl.BlockSpec((N_EXPERTS, 1), lambda i: (0, 0))],
        out_specs=pl.BlockSpec((TOP_K, tile), col),
        compiler_params=_params(1), name="slots",
    )(top_e, rank, start_rows)


def _sc_mesh():
    return plsc.VectorSubcoreMesh(core_axis_name="c", subcore_axis_name="s",
                                  num_cores=SC_CORES, num_subcores=SC_SUBCORES)


def _sc_scatter_rows(rows, idx, n_out):
    n, dp = rows.shape
    copies = idx.shape[0] // n
    per_worker = n // (SC_CORES * SC_SUBCORES)
    chunk = SC_GATHER_ROWS

    @functools.partial(
        pl.kernel, mesh=_sc_mesh(), out_type=jax.ShapeDtypeStruct((n_out, dp), rows.dtype),
        scratch_types=[pltpu.VMEM((chunk,), jnp.int32), pltpu.VMEM((chunk, dp), rows.dtype)],
        name="sc_scatter_rows")
    def scatter(rows_hbm, idx_hbm, out_hbm, idx_v, rows_v):
        base = (lax.axis_index("s") * SC_CORES + lax.axis_index("c")) * per_worker

        @pl.loop(0, per_worker // chunk)
        def _(j):
            off = base + j * chunk
            pltpu.sync_copy(rows_hbm.at[pl.ds(off, chunk)], rows_v)
            for k in range(copies):
                pltpu.sync_copy(idx_hbm.at[pl.ds(k * n + off, chunk)], idx_v)
                pltpu.sync_copy(rows_v, out_hbm.at[idx_v])

    return scatter(rows, idx)


def _expert_kernel(start_ref, cnt_ref, nused_ref, xs_ref, wg_ref, wu_ref, wd_ref, y_ref,
                   wgb, wub, wdb, xbuf, ybuf, sem_in, sem_out):
    e = pl.program_id(0)
    bm, ring = EXPERT_BLOCK, EXPERT_RING
    cnt = cnt_ref[e]
    nb = (cnt + bm - 1) // bm
    g0 = start_ref[e]
    n_used = nused_ref[0]

    def x_copy(g):
        slot = g % ring
        return pltpu.make_async_copy(xs_ref.at[pl.ds(g * bm, bm)], xbuf.at[slot], sem_in.at[slot])

    def y_copy(g):
        slot = g % ring
        return pltpu.make_async_copy(ybuf.at[slot], y_ref.at[pl.ds(g * bm, bm)], sem_out.at[slot])

    @pl.when(e == 0)
    def _():
        for g in range(ring):
            @pl.when(g < n_used)
            def _():
                x_copy(g).start()

    wgb[...] = wg_ref[...].astype(BF16)
    wub[...] = wu_ref[...].astype(BF16)
    wdb[...] = wd_ref[...].astype(BF16)
    row = lax.broadcasted_iota(jnp.int32, (bm, xbuf.shape[2]), 0)

    def blocks(j, count):
        for b in range(count):
            g = g0 + j + b
            x_copy(g).wait()

            @pl.when(g >= ring)
            def _():
                y_copy(g - ring).wait()

        def swiglu(b):
            slot = (g0 + j + b) % ring
            x = _unpack_pairs(jnp.where(row < cnt - (j + b) * bm, xbuf[slot], jnp.uint32(0))).astype(BF16)
            yield
            gate = jnp.dot(x, wgb[...], preferred_element_type=F32)
            up = jnp.dot(x, wub[...], preferred_element_type=F32)
            yield
            act = (_silu(gate) * up).astype(BF16)
            yield
            y = jnp.dot(act, wdb[...], preferred_element_type=F32)
            yield
            ybuf[slot] = _pack_pairs(y)

        _interleave([swiglu(b) for b in range(count)])

        for b in range(count):
            g = g0 + j + b
            y_copy(g).start()

            @pl.when(g + ring < n_used)
            def _():
                x_copy(g + ring).start()

    group = EXPERT_GROUP

    def full_group(p, carry):
        blocks(group * p, group)
        return carry

    lax.fori_loop(0, nb // group, full_group, 0)
    done = (nb // group) * group
    size = group // 2
    while size >= 1:
        @pl.when((nb - done) % (2 * size) >= size)
        def _(done=done, size=size):
            blocks(done, size)

        done = done + jnp.where((nb - done) % (2 * size) >= size, size, 0)
        size //= 2

    @pl.when(e == pl.num_programs(0) - 1)
    def _():
        for back in range(1, ring + 1):
            @pl.when(n_used - back >= 0)
            def _():
                y_copy(n_used - back).wait()


def _experts(blk_start, counts, n_used, xs, wg, wu, wd):
    n_slots, dp = xs.shape
    n_exp, d, ff = wg.shape
    bm, ring = EXPERT_BLOCK, EXPERT_RING
    wsel = lambda e, st, ct, nu: (e, 0, 0)
    grid_spec = pltpu.PrefetchScalarGridSpec(
        num_scalar_prefetch=3, grid=(n_exp,),
        in_specs=[pl.BlockSpec(memory_space=pl.ANY),
                  pl.BlockSpec((None, d, ff), wsel), pl.BlockSpec((None, d, ff), wsel),
                  pl.BlockSpec((None, ff, d), wsel)],
        out_specs=pl.BlockSpec(memory_space=pl.ANY),
        scratch_shapes=[pltpu.VMEM((d, ff), BF16), pltpu.VMEM((d, ff), BF16), pltpu.VMEM((ff, d), BF16),
                        pltpu.VMEM((ring, bm, dp), U32), pltpu.VMEM((ring, bm, dp), U32),
                        pltpu.SemaphoreType.DMA((ring,)), pltpu.SemaphoreType.DMA((ring,))])
    return pl.pallas_call(
        _expert_kernel, out_shape=jax.ShapeDtypeStruct((n_slots, dp), U32), grid_spec=grid_spec,
        compiler_params=_params(1), name="experts",
    )(blk_start, counts, n_used, xs, wg, wu, wd)


def _sc_gather_rows(table, idx):
    n_idx = idx.shape[0]
    dp = table.shape[1]
    workers = SC_CORES * SC_SUBCORES
    per_worker = n_idx // workers
    chunk = SC_GATHER_ROWS

    @functools.partial(
        pl.kernel, mesh=_sc_mesh(), out_type=jax.ShapeDtypeStruct((n_idx, dp), table.dtype),
        scratch_types=[pltpu.VMEM((chunk,), jnp.int32), pltpu.VMEM((chunk, dp), table.dtype),
                       pltpu.SemaphoreType.DMA],
        name="sc_gather_rows")
    def gather(table_hbm, idx_hbm, out_hbm, idx_v, rows_v, sem):
        base = (lax.axis_index("s") * SC_CORES + lax.axis_index("c")) * per_worker

        @pl.loop(0, per_worker // chunk)
        def _(j):
            off = base + j * chunk
            pltpu.sync_copy(idx_hbm.at[pl.ds(off, chunk)], idx_v)
            pltpu.async_copy(table_hbm.at[idx_v], rows_v, sem).wait()
            pltpu.sync_copy(rows_v, out_hbm.at[pl.ds(off, chunk)])

    return gather(table, idx)


def _combine_kernel(tile, yt_ref, w_ref, h_ref, x1_ref, g2_ref, sg_ref, su_ref, sd_ref, lg_ref, lb_ref, *rest):
    o_ref = rest[-1]
    hb = _unpack_pairs(h_ref[...]).astype(BF16)
    g = jnp.dot(hb, sg_ref[...], preferred_element_type=F32)
    u = jnp.dot(hb, su_ref[...], preferred_element_type=F32)
    f = jnp.dot((_silu(g) * u).astype(BF16), sd_ref[...], preferred_element_type=F32)
    w = w_ref[...]
    for kk in range(TOP_K):
        f = f + w[:, kk:kk + 1] * _unpack_pairs(yt_ref[kk * tile:(kk + 1) * tile, :])
    o_ref[...] = _layer_norm(DEEPNORM_ALPHA * x1_ref[...] + g2_ref[0] * f, lg_ref[...], lb_ref[...])


def _combine(y_tok, top_w, h2p, x1, g2, sg, su, sd, lg, lb, tile, tiles_per_batch, first_tile, prev_out):
    n, d = x1.shape
    dp = h2p.shape[1]
    row = lambda i: (first_tile + i, 0)
    full = lambda i: (0, 0)

    def const(a):
        return pl.BlockSpec(a.shape, full, pipeline_mode=pl.Buffered(1))

    in_specs = [pl.BlockSpec((TOP_K * tile, dp), lambda i: (i, 0)),
                pl.BlockSpec((tile, TOP_K), row), pl.BlockSpec((tile, dp), row), pl.BlockSpec((tile, d), row),
                pl.BlockSpec((1, 1, d), lambda i: ((first_tile + i) // tiles_per_batch, 0, 0)),
                const(sg), const(su), const(sd), const(lg), const(lb)]
    args = [y_tok, top_w, h2p, x1, g2, sg, su, sd, lg, lb]
    aliases = {}
    if prev_out is not None:
        in_specs.append(pl.BlockSpec(memory_space=pl.ANY))
        args.append(prev_out)
        aliases = {len(args) - 1: 0}
    return pl.pallas_call(
        functools.partial(_combine_kernel, tile),
        out_shape=jax.ShapeDtypeStruct((n, d), F32),
        grid=(y_tok.shape[0] // (TOP_K * tile),),
        in_specs=in_specs,
        out_specs=pl.BlockSpec((tile, d), row),
        input_output_aliases=aliases,
        compiler_params=_params(1), name="combine_shared_ln2",
    )(*args)


def kernel(x, c, ctx, c_ctx, w_mod, b_mod, w_in, na_rpb, gla_w_decay_f, gla_b_decay_f, gla_w_decay_b, gla_b_decay_b,
           gla_norm_w, w_branch_a, w_branch_b, w_out, ln1_g, ln1_b, router_w, router_bias, exp_w_gate, exp_w_up,
           exp_w_down, sh_w_gate, sh_w_up, sh_w_down, ln2_g, ln2_b):
    batch, t_len, d = x.shape
    ctx_len = ctx.shape[1]
    n = batch * t_len
    assert w_mod.shape[0] == DEPTH == 1
    assert t_len % GLA_CHUNK == 0 and ctx_len == GLA_CHUNK and (t_len // GRID_W) % (NA_ROW_BLOCK * NA_BLOCKS_PER_STEP) == 0

    mod_rows = 16
    c_all = jnp.zeros((mod_rows, d), F32).at[:batch].set(c).at[batch].set(c_ctx)
    mod = _modulation(c_all, w_mod[0], b_mod[0])
    sh1, sc1, g1, sh2, sc2, g2 = [mod[:batch, j * d:(j + 1) * d].reshape(batch, 1, d) for j in range(6)]
    sh1c = mod[batch:batch + 1, 0:d].reshape(1, 1, d)
    sc1c = mod[batch:batch + 1, d:2 * d].reshape(1, 1, d)

    offs = np.cumsum((0, NA_WIDTH, NA_WIDTH, NA_WIDTH, GLA_QK_WIDTH, GLA_QK_WIDTH, GLA_V_WIDTH, GLA_V_WIDTH,
                      GLA_GATE_RANK, GLA_GATE_RANK, d, d))
    qa, ka, va, qb, kb, vbc, gbc, lrf, lrb, ga, gbt = [w_in[0][:, offs[j]:offs[j + 1]] for j in range(11)]
    lr_cols = jnp.concatenate([lrf, lrb, jnp.zeros((d, LANES - 2 * GLA_GATE_RANK), F32)], axis=1)
    w_lat = jnp.concatenate([lr_cols, qa, ka, va, qb, kb, vbc, gbc, ga, gbt], axis=1).astype(BF16)
    lat_offs = np.cumsum((0, LANES, NA_WIDTH, NA_WIDTH, NA_WIDTH, GLA_QK_WIDTH, GLA_QK_WIDTH, GLA_V_WIDTH))
    w_ctx = jnp.concatenate([w_lat[:, lat_offs[j]:lat_offs[j + 1]] for j in (0, 2, 3, 5, 6)], axis=1)
    plain = ("plain",)
    lat_plan = ((LANES, (("decay",),)),
                (3 * NA_WIDTH, (("scale", NA_HEAD_DIM ** -0.5), plain, plain)),
                (2 * GLA_QK_WIDTH, (("rope", GLA_DK ** -0.5), ("rope", 1.0))),
                (GLA_V_WIDTH, (plain, plain)), (GLA_V_WIDTH, (plain, plain)),
                (2 * d, (plain,) * 4))
    ctx_plan = ((LANES, (("decay",),)),
                (2 * NA_WIDTH, (plain, plain)), (GLA_QK_WIDTH, (plain,)), (GLA_V_WIDTH, (plain, plain)))
    w2, b2 = _decay_weights(gla_w_decay_f[0], gla_b_decay_f[0], gla_w_decay_b[0], gla_b_decay_b[0])
    tile = COMBINE_TILE
    x2d = x.reshape(n, d)
    decay, na_qkv, gla_qk, vb, gb, gates = _projection(
        x2d, sc1, sh1, w_lat, w2, b2, lat_plan, (BF16,) * 6, PROJ_TILE, t_len // PROJ_TILE,
        rope=_rope_tables(t_len))
    decay_c, na_kv_c, k_c, v_c = _projection(
        ctx.reshape(batch * ctx_len, d), sc1c, sh1c, w_ctx, w2, b2, ctx_plan, (BF16,) * 4, tile,
        batch * ctx_len // tile)

    out_a = _na_attention(na_qkv, na_kv_c, _na_bias_tables(na_rpb[0]), batch, t_len, ctx_len)
    o_f, o_b = _gla(gla_qk, vb, decay, k_c, v_c, decay_c, batch, t_len, ctx_len)

    x1, h2p, scores_t = _merge(
        out_a.reshape(n, NA_WIDTH), o_f.reshape(n, GLA_V_WIDTH), o_b.reshape(n, GLA_V_WIDTH), gb, gates, x2d,
        g1, sc2, sh2, w_branch_a[0].astype(BF16), w_branch_b[0].astype(BF16), w_out[0].astype(BF16),
        gla_norm_w[0].reshape(1, -1), ln1_g[0].reshape(1, d), ln1_b[0].reshape(1, d),
        router_w[0].T.astype(BF16), PROJ_TILE, t_len // PROJ_TILE)

    top_e, top_w, rank, counts = _route(scores_t, router_bias[0], 512)

    counts = counts[:, 0]
    n_blocks = pl.cdiv(n * TOP_K, EXPERT_BLOCK) + N_EXPERTS
    blocks_per = (counts + EXPERT_BLOCK - 1) // EXPERT_BLOCK
    blk_end = jnp.cumsum(blocks_per)
    blk_start = blk_end - blocks_per
    dest = _slots(top_e, rank, (blk_start * EXPERT_BLOCK).astype(F32).reshape(N_EXPERTS, 1), 2048)

    xs = _sc_scatter_rows(h2p, dest.reshape(TOP_K * n), n_blocks * EXPERT_BLOCK)
    y = _experts(blk_start.astype(jnp.int32), counts, blk_end[-1:].astype(jnp.int32), xs,
                 exp_w_gate[0], exp_w_up[0], exp_w_down[0])
    dest_tok = dest.reshape(TOP_K, n // tile, tile).transpose(1, 0, 2).reshape(COMBINE_CHUNKS, -1)
    w_rows = top_w
    shared = (sh_w_gate[0].astype(BF16), sh_w_up[0].astype(BF16), sh_w_down[0].astype(BF16),
              ln2_g[0].reshape(1, d), ln2_b[0].reshape(1, d))
    tiles_per_chunk = n // tile // COMBINE_CHUNKS
    out = None
    for ci in range(COMBINE_CHUNKS):
        y_tok = _sc_gather_rows(y, dest_tok[ci])
        out = _combine(y_tok, w_rows, h2p, x1, g2, *shared, tile, t_len // tile, ci * tiles_per_chunk, out)
    return out.reshape(batch, t_len, d)
```

```python
import functools

import numpy as np
import jax
import jax.numpy as jnp
from jax import lax
from jax.experimental import pallas as pl
from jax.experimental.pallas import tpu as pltpu
from jax.experimental.pallas import tpu_sc as plsc

F32 = jnp.float32
BF16 = jnp.bfloat16
U32 = jnp.uint32
HIGHEST = lax.Precision.HIGHEST

GRID_W = 64
NA_HEADS = 8
NA_HEAD_DIM = 64
NA_WIDTH = NA_HEADS * NA_HEAD_DIM
WIN_ROWS = 8
WIN_COLS = 16
GLA_HEADS = 4
GLA_DK = 128
GLA_DV = 256
GLA_QK_WIDTH = GLA_HEADS * GLA_DK
GLA_V_WIDTH = GLA_HEADS * GLA_DV
GLA_GATE_RANK = 16
GLA_TAU = 16.0
LOG2E = 1.4426950408889634
ROPE_BASE = 10000.0
N_EXPERTS = 256
TOP_K = 8
N_GROUPS = 8
TOPK_GROUPS = 4
GROUP_SIZE = N_EXPERTS // N_GROUPS
ROUTED_SCALE = 2.5
DEPTH = 1
DEEPNORM_ALPHA = (2 * DEPTH) ** 0.25
EPS = 1e-6

LANES = 128
PROJ_TILE = 512
PROJ_INTERLEAVE = 2
NA_ROW_BLOCK = 4
NA_BLOCKS_PER_STEP = 2
NA_INTERLEAVE = 4
NA_UNION_ROWS = NA_ROW_BLOCK + WIN_ROWS - 1
GLA_CHUNK = 256
GLA_DIAG = 16
GLA_HEADS_PER_STEP = 4
GLA_INTERLEAVE = 8
EXPERT_BLOCK = 272
EXPERT_GROUP = 4
EXPERT_RING = 8
COMBINE_CHUNKS = 8
COMBINE_TILE = 512
SC_CORES = 2
SC_SUBCORES = 16
SC_GATHER_ROWS = 128
NEG_BIG = -1e30
VMEM_LIMIT = 56 * 1024 * 1024
NA_VMEM_LIMIT = 58 * 1024 * 1024


def _params(n_axes, vmem=VMEM_LIMIT):
    return pltpu.CompilerParams(dimension_semantics=("arbitrary",) * n_axes, vmem_limit_bytes=vmem)


def _sigmoid(v):
    return 1.0 / (1.0 + jnp.exp2(v * (-LOG2E)))


def _silu(v):
    return v * _sigmoid(v)


def _interleave(chains, group=None):
    if group is not None and group < len(chains):
        return [r for i in range(0, len(chains), group) for r in _interleave(chains[i:i + group])]
    results = [None] * len(chains)
    active = list(enumerate(chains))
    while active:
        still = []
        for i, chain in active:
            try:
                next(chain)
                still.append((i, chain))
            except StopIteration as stop:
                results[i] = stop.value
        active = still
    return results


def _pack_pairs(v):
    m = v.shape[1] // 2
    lo = lax.bitcast_convert_type(v[:, :m].astype(BF16).astype(F32), U32) >> 16
    hi = lax.bitcast_convert_type(v[:, m:].astype(BF16).astype(F32), U32) & jnp.uint32(0xFFFF0000)
    return lo | hi


def _unpack_pairs(p):
    lo = lax.bitcast_convert_type(p << 16, F32)
    hi = lax.bitcast_convert_type(p & jnp.uint32(0xFFFF0000), F32)
    return jnp.concatenate([lo, hi], axis=1)


def _mod_kernel(c_ref, w_ref, b_ref, o_ref):
    o_ref[...] = jnp.dot(_silu(c_ref[...]), w_ref[...], preferred_element_type=F32, precision=HIGHEST) + b_ref[...]


def _modulation(c_all, w_mod, b_mod):
    rows, d = c_all.shape
    n = w_mod.shape[1]
    bn = 512
    return pl.pallas_call(
        _mod_kernel,
        out_shape=jax.ShapeDtypeStruct((rows, n), F32),
        grid=(n // bn,),
        in_specs=[pl.BlockSpec((rows, d), lambda j: (0, 0)),
                  pl.BlockSpec((d, bn), lambda j: (0, j)),
                  pl.BlockSpec((1, bn), lambda j: (0, j))],
        out_specs=pl.BlockSpec((rows, bn), lambda j: (0, j)),
        compiler_params=_params(1),
        name="modulation",
    )(c_all, w_mod, b_mod.reshape(1, n))


def _swap32(v):
    lane = lax.broadcasted_iota(jnp.int32, v.shape, 1)
    return jnp.where((lane % 64) < 32, pltpu.roll(v, 96, 1), pltpu.roll(v, 32, 1))


GLA_DECAY_WIDTH = 2 * GLA_HEADS * 2 * GLA_DK


def _log2_decay_split(lr, w2, b2, out_ref):
    z = (jnp.dot(lr.astype(BF16), w2, preferred_element_type=F32) + b2) * LOG2E
    a = (jnp.minimum(z, 0.0) - jnp.log2(1.0 + jnp.exp2(-jnp.abs(z)))) * (1.0 / GLA_TAU)
    hi = a.astype(BF16)
    lo = (a - hi.astype(F32)).astype(BF16)
    for p in range(a.shape[1] // GLA_DK):
        src = slice(p * GLA_DK, (p + 1) * GLA_DK)
        out_ref[:, 2 * p * GLA_DK:(2 * p + 1) * GLA_DK] = hi[:, src]
        out_ref[:, (2 * p + 1) * GLA_DK:(2 * p + 2) * GLA_DK] = lo[:, src]


def _proj_kernel(plan, has_rope, *refs):
    x_ref, sc_ref, sh_ref, w_ref, w2_ref, b2_ref = refs[:6]
    pos = 6
    if has_rope:
        cos_ref, sin_ref = refs[6:8]
        pos = 8
    out_refs = refs[pos:]
    h = (x_ref[...] * (1.0 + sc_ref[0]) + sh_ref[0]).astype(BF16)

    def piece(out_ref, kind, w0, c0, cw):
        acc = jnp.dot(h, w_ref[:, w0:w0 + cw], preferred_element_type=F32)
        yield
        if kind[0] == "decay":
            _log2_decay_split(acc, w2_ref[...], b2_ref[...], out_ref)
            return
        if kind[0] == "scale":
            acc = acc * kind[1]
        elif kind[0] == "rope":
            cos, sin = cos_ref[...], sin_ref[...]
            parts = []
            for p in range(cw // LANES):
                v = acc[:, p * LANES:(p + 1) * LANES]
                parts.append((v * cos + _swap32(v) * sin) * kind[1])
            acc = jnp.concatenate(parts, axis=1)
        out_ref[:, c0:c0 + cw] = acc.astype(out_ref.dtype)

    chains = []
    col = 0
    for out_ref, (width, kinds) in zip(out_refs, plan):
        cw = width // len(kinds)
        for j, kind in enumerate(kinds):
            chains.append(piece(out_ref, kind, col + j * cw, j * cw, cw))
        col += width
    for i in range(0, len(chains), PROJ_INTERLEAVE):
        _interleave(chains[i:i + PROJ_INTERLEAVE])


def _projection(x2d, sc, sh, w, w2, b2, plan, out_dtypes, tile, tiles_per_mod, rope=None):
    n, d = x2d.shape
    const = lambda a: pl.BlockSpec(a.shape, lambda i: (0, 0), pipeline_mode=pl.Buffered(1))
    in_specs = [pl.BlockSpec((tile, d), lambda i: (i, 0)),
                pl.BlockSpec((1, 1, d), lambda i: (i // tiles_per_mod, 0, 0)),
                pl.BlockSpec((1, 1, d), lambda i: (i // tiles_per_mod, 0, 0)),
                const(w), const(w2), const(b2)]
    args = [x2d, sc, sh, w, w2, b2]
    if rope is not None:
        in_specs += [pl.BlockSpec((tile, LANES), lambda i: (i % tiles_per_mod, 0))] * 2
        args += list(rope)
    widths = [GLA_DECAY_WIDTH if kinds[0][0] == "decay" else wd for wd, kinds in plan]
    out_shape = [jax.ShapeDtypeStruct((n, wd), dt) for wd, dt in zip(widths, out_dtypes)]
    out_specs = [pl.BlockSpec((tile, wd), lambda i: (i, 0)) for wd in widths]
    return pl.pallas_call(
        functools.partial(_proj_kernel, plan, rope is not None),
        out_shape=out_shape,
        grid=(n // tile,),
        in_specs=in_specs,
        out_specs=out_specs,
        compiler_params=_params(1),
        name="in_proj" if rope is not None else "ctx_proj",
    )(*args)


def _rope_tables(t_len):
    half = GLA_DK // 2
    quarter = half // 2
    f32 = np.float32
    inv_freq = f32(ROPE_BASE) ** (-np.arange(quarter, dtype=f32) / f32(quarter))
    pos = np.arange(t_len)
    row_ang = (pos // GRID_W).astype(f32)[:, None] * inv_freq[None, :]
    col_ang = (pos % GRID_W).astype(f32)[:, None] * inv_freq[None, :]
    cr, sr, cc, sn = np.cos(row_ang), np.sin(row_ang), np.cos(col_ang), np.sin(col_ang)
    cos = np.concatenate([cr, cr, cc, cc], axis=1).astype(f32)
    sin = np.concatenate([-sr, sr, -sn, sn], axis=1).astype(f32)
    return jnp.asarray(cos), jnp.asarray(sin)


def _na_bias_tables(rpb):
    rb, ur, w = NA_ROW_BLOCK, NA_UNION_ROWS, GRID_W
    heads = rpb.shape[0]
    pad = jnp.pad(rpb, ((0, 0), (0, 0), (w, w)))
    toep = jnp.stack([pad[:, :, w + WIN_COLS - 1 - c:2 * w + WIN_COLS - 1 - c] for c in range(w)], axis=2)
    c = np.arange(w)[:, None]
    kc = np.arange(w)[None, :]
    col_start = np.clip(c - WIN_COLS // 2, 0, w - WIN_COLS)
    col_ok = (kc >= col_start) & (kc < col_start + WIN_COLS)
    toep = jnp.where(col_ok[None, None], toep, NEG_BIG)
    neg = jnp.full((heads, w, w), NEG_BIG, F32)
    half = WIN_ROWS // 2
    tables = []
    for lo, off in ((lambda i: 0, WIN_ROWS - 1), (lambda i: i, WIN_ROWS - 1 - half), (lambda i: ur - WIN_ROWS, 0)):
        rows_ = []
        for i in range(rb):
            blocks = [toep[:, j - i + off] if lo(i) <= j < lo(i) + WIN_ROWS else neg for j in range(ur)]
            rows_.append(jnp.concatenate(blocks, axis=2))
        tables.append(jnp.concatenate(rows_, axis=1))
    return jnp.stack(tables)


def _na_kernel(rows, q_ref, k_ref, v_ref, kc_ref, vc_ref, *rest):
    bias_refs, o_ref = rest[:-1], rest[-1]
    tq = NA_ROW_BLOCK * GRID_W
    nk = NA_UNION_ROWS * GRID_W
    nt = (((1,), (1,)), ((), ()))
    lane = lax.broadcasted_iota(jnp.int32, (tq, LANES), 1)
    def head(qm, ku, vu, kc, vc, bias):
        s_loc = lax.dot_general(qm, ku, nt, preferred_element_type=F32) + bias
        s_ctx = lax.dot_general(qm, kc, nt, preferred_element_type=F32)
        yield
        m = jnp.maximum(jnp.max(s_loc, axis=1, keepdims=True), jnp.max(s_ctx, axis=1, keepdims=True))
        p_loc = jnp.exp(s_loc - m)
        p_ctx = jnp.exp(s_ctx - m)
        yield
        denom = jnp.sum(p_loc, axis=1, keepdims=True) + jnp.sum(p_ctx, axis=1, keepdims=True)
        o = (jnp.dot(p_loc.astype(BF16), vu, preferred_element_type=F32)
             + jnp.dot(p_ctx.astype(BF16), vc, preferred_element_type=F32))
        yield
        return o / denom

    chains = []
    for sub, bias_ref in enumerate(bias_refs):
        rb = pl.program_id(1) * len(bias_refs) + sub
        ustart = jnp.clip(rb * NA_ROW_BLOCK - WIN_ROWS // 2, 0, rows - NA_UNION_ROWS)
        k0 = pl.multiple_of(ustart * GRID_W, GRID_W)
        qrows = slice(sub * tq, (sub + 1) * tq)
        for p in range(NA_WIDTH // LANES):
            ls = slice(p * LANES, (p + 1) * LANES)
            q = q_ref[qrows, ls]
            ku = k_ref[pl.ds(k0, nk), ls]
            vu = v_ref[pl.ds(k0, nk), ls]
            for hh in range(LANES // NA_HEAD_DIM):
                qm = jnp.where((lane // NA_HEAD_DIM) == hh, q, jnp.zeros_like(q))
                chains.append(head(qm, ku, vu, kc_ref[:, ls], vc_ref[:, ls], bias_ref[p * 2 + hh]))
    outs = _interleave(chains, NA_INTERLEAVE)
    pairs = NA_WIDTH // LANES
    for sub in range(len(bias_refs)):
        for p in range(pairs):
            i = 2 * (sub * pairs + p)
            out = jnp.where((lane // NA_HEAD_DIM) == 0, outs[i], outs[i + 1])
            o_ref[sub * tq:(sub + 1) * tq, p * LANES:(p + 1) * LANES] = out.astype(o_ref.dtype)


def _na_attention(na_qkv, na_kv_ctx, bias_tables, batch, t_len, ctx_len):
    rows = t_len // GRID_W
    n_rb = rows // NA_ROW_BLOCK
    per = NA_BLOCKS_PER_STEP
    tq = per * NA_ROW_BLOCK * GRID_W
    qkv = na_qkv.reshape(batch, t_len, 3 * NA_WIDTH)
    kvc = na_kv_ctx.reshape(batch, ctx_len, 2 * NA_WIDTH)

    def bias_spec(sub):
        def idx(b, r):
            rb = r * per + sub
            return (jnp.where(rb == 0, 0, jnp.where(rb == n_rb - 1, 2, 1)), 0, 0, 0)
        return pl.BlockSpec((None,) + bias_tables.shape[1:], idx)

    return pl.pallas_call(
        functools.partial(_na_kernel, rows),
        out_shape=jax.ShapeDtypeStruct((batch, t_len, NA_WIDTH), BF16),
        grid=(batch, n_rb // per),
        in_specs=[pl.BlockSpec((None, tq, NA_WIDTH), lambda b, r: (b, r, 0)),
                  pl.BlockSpec((None, t_len, NA_WIDTH), lambda b, r: (b, 0, 1)),
                  pl.BlockSpec((None, t_len, NA_WIDTH), lambda b, r: (b, 0, 2)),
                  pl.BlockSpec((None, ctx_len, NA_WIDTH), lambda b, r: (b, 0, 0)),
                  pl.BlockSpec((None, ctx_len, NA_WIDTH), lambda b, r: (b, 0, 1))]
                 + [bias_spec(sub) for sub in range(per)],
        out_specs=pl.BlockSpec((None, tq, NA_WIDTH), lambda b, r: (b, r, 0)),
        compiler_params=_params(2, vmem=NA_VMEM_LIMIT),
        name="na_attention",
    )(qkv, qkv, qkv, kvc, kvc, *([bias_tables] * per))


def _gla_constants(c):
    tris, masks = [], []
    for reverse in (False, True):
        i = np.arange(c)[:, None]
        j = np.arange(c)[None, :]
        tris.append((j >= i) if reverse else (j <= i))
        i = np.arange(c // 2)[:, None]
        j = np.arange(c // 2)[None, :]
        if reverse:
            i, j = j, i
        level = []
        s = c // 4
        while s >= GLA_DIAG:
            level.append(((i // (2 * s)) == (j // (2 * s))) & ((i % (2 * s)) >= s) & ((j % (2 * s)) < s))
            s //= 2
        level.append(((i // GLA_DIAG) == (j // GLA_DIAG)) & (j <= i))
        masks.append(np.stack(level))
    return jnp.asarray(np.stack(tris), BF16), jnp.asarray(np.stack(masks), F32)


def _block_refs(cum, s, reverse, diag):
    c = cum.shape[0]
    span = s if diag else 2 * s
    parts = []
    for p in range(c // span):
        if diag:
            r = p * span + (span - 1 if reverse else 0)
        else:
            r = p * span + (s - 1 if reverse else s)
        parts.append(jnp.broadcast_to(cum[r:r + 1, :], (span, cum.shape[1])))
    return jnp.concatenate(parts, axis=0)


def _cumsum_rows(a_hl, tri):
    parts = jnp.dot(tri, a_hl, preferred_element_type=F32)
    w = a_hl.shape[1] // 2
    return parts[:, :w] + parts[:, w:]


def _gla_chunk(q, k, v, a, state_t, tri, mask_ref, reverse, want_out):
    c = k.shape[0]
    hc = c // 2
    nt = (((1,), (1,)), ((), ()))
    k = k.astype(F32)
    q = q.astype(F32) if want_out else None
    cum = _cumsum_rows(a, tri)
    last = 0 if reverse else c - 1
    total = cum[last:last + 1, :]
    yield

    def scale(x, log2_factor):
        return (x * jnp.exp2(log2_factor)).astype(BF16)

    out = None
    if want_out:
        out = lax.dot_general(scale(q, cum), state_t.astype(BF16), nt, preferred_element_type=F32)
        halves = (slice(hc, c), slice(0, hc)) if reverse else (slice(0, hc), slice(hc, c))
        early, late = halves
        r = hc - 1 if reverse else hc
        g = cum[r:r + 1, :]
        cross = lax.dot_general(scale(q[late], cum[late] - g), scale(k[early], g - cum[early]), nt,
                                preferred_element_type=F32)
        yield
        inner = []
        for rows in halves:
            cx, qx, kx = cum[rows], q[rows], k[rows]
            acc = jnp.zeros((hc, hc), F32)
            s = hc // 2
            level = 0
            while True:
                diag = s < GLA_DIAG
                d = cx - _block_refs(cx, GLA_DIAG if diag else s, reverse, diag)
                if diag:
                    qs, ks = scale(qx, d), scale(kx, -d)
                else:
                    e = jnp.exp2(-jnp.abs(d))
                    qs, ks = (qx * e).astype(BF16), (kx * e).astype(BF16)
                acc = acc + lax.dot_general(qs, ks, nt, preferred_element_type=F32) * mask_ref[level]
                yield
                if diag:
                    break
                s //= 2
                level += 1
            inner.append(acc)
        out_early = jnp.dot(inner[0].astype(BF16), v[early], preferred_element_type=F32)
        out_late = jnp.dot(jnp.concatenate([cross, inner[1]], axis=1).astype(BF16),
                           jnp.concatenate([v[early], v[late]], axis=0), preferred_element_type=F32)
        intra = [out_late, out_early] if reverse else [out_early, out_late]
        out = out + jnp.concatenate(intra, axis=0)
        yield
    upd = lax.dot_general(v, scale(k, total - cum), (((0,), (0,)), ((), ())), preferred_element_type=F32)
    return out, jnp.exp2(total) * state_t + upd


def _gla_kernel(qf_ref, kf_ref, vf_ref, af_ref, qb_ref, kb_ref, vb_ref, ab_ref,
                kc_ref, vc_ref, acf_ref, acb_ref, tri_ref, mask_ref,
                of_ref, ob_ref, sf_ref, sb_ref):
    c = pl.program_id(2)
    fm, bm = mask_ref.at[0], mask_ref.at[1]
    heads = sf_ref.shape[0]

    @pl.when(c == 0)
    def _():
        zero = jnp.zeros(sf_ref.shape[1:], F32)
        chains = []
        for h in range(heads):
            kc = kc_ref[:, h * GLA_DK:(h + 1) * GLA_DK]
            vc = vc_ref[:, h * GLA_DV:(h + 1) * GLA_DV]
            hl = slice(2 * h * GLA_DK, 2 * (h + 1) * GLA_DK)
            chains.append(_gla_chunk(None, kc, vc, acf_ref[:, hl], zero, tri_ref[0], fm, False, False))
            chains.append(_gla_chunk(None, kc, vc, acb_ref[:, hl], zero, tri_ref[1], bm, True, False))
        for i, (_, state) in enumerate(_interleave(chains)):
            (sb_ref if i % 2 else sf_ref)[i // 2] = state

    @pl.when(c > 0)
    def _():
        chains = []
        for h in range(heads):
            ks = slice(h * GLA_DK, (h + 1) * GLA_DK)
            vs = slice(h * GLA_DV, (h + 1) * GLA_DV)
            hl = slice(2 * h * GLA_DK, 2 * (h + 1) * GLA_DK)
            chains.append(_gla_chunk(qf_ref[:, ks], kf_ref[:, ks], vf_ref[:, vs], af_ref[:, hl], sf_ref[h],
                                     tri_ref[0], fm, False, True))
            chains.append(_gla_chunk(qb_ref[:, ks], kb_ref[:, ks], vb_ref[:, vs], ab_ref[:, hl], sb_ref[h],
                                     tri_ref[1], bm, True, True))
        for i, (o, state) in enumerate(_interleave(chains, GLA_INTERLEAVE)):
            h = i // 2
            vs = slice(h * GLA_DV, (h + 1) * GLA_DV)
            if i % 2:
                ob_ref[:, vs] = o.astype(ob_ref.dtype)
                sb_ref[h] = state
            else:
                of_ref[:, vs] = o.astype(of_ref.dtype)
                sf_ref[h] = state


def _gla(gla_qk, vb, decay, k_ctx, v_ctx, decay_ctx, batch, t_len, ctx_len):
    nc = t_len // GLA_CHUNK
    h = GLA_HEADS
    qk = gla_qk.reshape(batch, t_len, 2 * GLA_QK_WIDTH)
    v3 = vb.reshape(batch, t_len, GLA_V_WIDTH)
    a3 = decay.reshape(batch, t_len, GLA_DECAY_WIDTH)
    kc3 = k_ctx.reshape(batch, ctx_len, GLA_QK_WIDTH)
    vc3 = v_ctx.reshape(batch, ctx_len, GLA_V_WIDTH)
    ac3 = decay_ctx.reshape(batch, ctx_len, GLA_DECAY_WIDTH)
    tri, masks = _gla_constants(GLA_CHUNK)

    def fwd(c):
        return jnp.maximum(c - 1, 0)

    def bwd(c):
        return nc - 1 - jnp.maximum(c - 1, 0)

    def const(a):
        return pl.BlockSpec(a.shape, lambda b, hh, c: (0,) * a.ndim, pipeline_mode=pl.Buffered(1))

    hp = GLA_HEADS_PER_STEP
    groups = h // hp
    cq = (None, GLA_CHUNK, hp * GLA_DK)
    cv = (None, GLA_CHUNK, hp * GLA_DV)
    ca = (None, GLA_CHUNK, hp * 2 * GLA_DK)
    cca = (None, ctx_len, hp * 2 * GLA_DK)
    in_specs = [
        pl.BlockSpec(cq, lambda b, g, c: (b, fwd(c), g)),
        pl.BlockSpec(cq, lambda b, g, c: (b, fwd(c), groups + g)),
        pl.BlockSpec(cv, lambda b, g, c: (b, fwd(c), g)),
        pl.BlockSpec(ca, lambda b, g, c: (b, fwd(c), g)),
        pl.BlockSpec(cq, lambda b, g, c: (b, bwd(c), g)),
        pl.BlockSpec(cq, lambda b, g, c: (b, bwd(c), groups + g)),
        pl.BlockSpec(cv, lambda b, g, c: (b, bwd(c), g)),
        pl.BlockSpec(ca, lambda b, g, c: (b, bwd(c), groups + g)),
        pl.BlockSpec((None, ctx_len, hp * GLA_DK), lambda b, g, c: (b, 0, g)),
        pl.BlockSpec((None, ctx_len, hp * GLA_DV), lambda b, g, c: (b, 0, g)),
        pl.BlockSpec(cca, lambda b, g, c: (b, 0, g)),
        pl.BlockSpec(cca, lambda b, g, c: (b, 0, groups + g)),
        const(tri), const(masks),
    ]
    out_specs = [pl.BlockSpec(cv, lambda b, g, c: (b, fwd(c), g)),
                 pl.BlockSpec(cv, lambda b, g, c: (b, bwd(c), g))]
    out_shape = [jax.ShapeDtypeStruct((batch, t_len, GLA_V_WIDTH), BF16)] * 2
    return pl.pallas_call(
        _gla_kernel,
        out_shape=out_shape,
        grid=(batch, groups, nc + 1),
        in_specs=in_specs,
        out_specs=out_specs,
        scratch_shapes=[pltpu.VMEM((hp, GLA_DV, GLA_DK), F32), pltpu.VMEM((hp, GLA_DV, GLA_DK), F32)],
        compiler_params=_params(3),
        name="gla",
    )(qk, qk, v3, a3, qk, qk, v3, a3, kc3, vc3, ac3, ac3, tri, masks)


def _decay_weights(w_f, b_f, w_b, b_b):
    r, width = w_f.shape
    w2 = jnp.zeros((LANES, 2 * width), F32).at[:r, :width].set(w_f).at[r:2 * r, width:].set(w_b)
    return w2.astype(BF16), jnp.concatenate([b_f, b_b]).reshape(1, 2 * width)


def _layer_norm(v, g, b):
    mu = jnp.mean(v, axis=1, keepdims=True)
    var = jnp.mean(jnp.square(v - mu), axis=1, keepdims=True)
    return (v - mu) * lax.rsqrt(var + EPS) * g + b


def _merge_kernel(oa_ref, of_ref, ob_ref, gb_ref, gates_ref, x_ref, g1_ref, sc2_ref, sh2_ref,
                  wa_ref, wb_ref, wo_ref, nw_ref, lg_ref, lb_ref, rw_ref,
                  x1_ref, h2_ref, sc_ref):
    d = x_ref.shape[1]
    o = of_ref[...].astype(F32) + ob_ref[...].astype(F32)
    pieces = []
    for hh in range(GLA_HEADS):
        oh = o[:, hh * GLA_DV:(hh + 1) * GLA_DV]
        pieces.append(oh * lax.rsqrt(jnp.mean(jnp.square(oh), axis=1, keepdims=True) + EPS))
    out_b = jnp.concatenate(pieces, axis=1) * nw_ref[...] * _silu(gb_ref[...].astype(F32))
    ya = jnp.dot(oa_ref[...], wa_ref[...], preferred_element_type=F32)
    yb = jnp.dot(out_b.astype(BF16), wb_ref[...], preferred_element_type=F32)
    y = _sigmoid(gates_ref[:, :d].astype(F32)) * ya + _sigmoid(gates_ref[:, d:].astype(F32)) * yb
    y2 = jnp.dot(y.astype(BF16), wo_ref[...], preferred_element_type=F32)
    x1 = _layer_norm(DEEPNORM_ALPHA * x_ref[...] + g1_ref[0] * y2, lg_ref[...], lb_ref[...])
    x1_ref[...] = x1
    h2 = x1 * (1.0 + sc2_ref[0]) + sh2_ref[0]
    h2_ref[...] = _pack_pairs(h2)
    logits_t = lax.dot_general(rw_ref[...], h2.astype(BF16), (((1,), (1,)), ((), ())), preferred_element_type=F32)
    sc_ref[...] = _sigmoid(logits_t)


def _merge(out_a, o_f, o_b, gb, gates, x2d, g1, sc2, sh2, wa, wb, wo, nw, lg, lb, rw_t, tile, tiles_per_batch):
    n, d = x2d.shape
    row = lambda i: (i, 0)
    mod = lambda i: (i // tiles_per_batch, 0, 0)
    full = lambda i: (0, 0)

    def const(a):
        return pl.BlockSpec(a.shape, full, pipeline_mode=pl.Buffered(1))

    in_specs = [pl.BlockSpec((tile, NA_WIDTH), row), pl.BlockSpec((tile, GLA_V_WIDTH), row),
                pl.BlockSpec((tile, GLA_V_WIDTH), row), pl.BlockSpec((tile, GLA_V_WIDTH), row),
                pl.BlockSpec((tile, 2 * d), row), pl.BlockSpec((tile, d), row),
                pl.BlockSpec((1, 1, d), mod), pl.BlockSpec((1, 1, d), mod), pl.BlockSpec((1, 1, d), mod),
                const(wa), const(wb), const(wo), const(nw), const(lg), const(lb), const(rw_t)]
    out_shape = [jax.ShapeDtypeStruct((n, d), F32), jax.ShapeDtypeStruct((n, d // 2), U32),
                 jax.ShapeDtypeStruct((N_EXPERTS, n), F32)]
    out_specs = [pl.BlockSpec((tile, d), row), pl.BlockSpec((tile, d // 2), row),
                 pl.BlockSpec((N_EXPERTS, tile), lambda i: (0, i))]
    return pl.pallas_call(
        _merge_kernel, out_shape=out_shape, grid=(n // tile,), in_specs=in_specs, out_specs=out_specs,
        compiler_params=_params(1), name="merge_ln1_router",
    )(out_a, o_f, o_b, gb, gates, x2d, g1, sc2, sh2, wa, wb, wo, nw, lg, lb, rw_t)


def _first_argmax(vals, idx, n):
    m = jnp.max(vals, axis=0, keepdims=True)
    first = jnp.min(jnp.where(vals == m, idx, float(n)), axis=0, keepdims=True)
    return m, first


def _route_kernel(sc_ref, bias_ref, before_ref, e_ref, w_ref, rank_ref, cnt_ref, carry_ref):
    step = pl.program_id(0)
    tr = sc_ref.shape[1]

    @pl.when(step == 0)
    def _():
        carry_ref[...] = jnp.zeros(carry_ref.shape, F32)

    scores = sc_ref[...]
    biased = scores + bias_ref[...]
    eidx = lax.broadcasted_iota(jnp.int32, (N_EXPERTS, tr), 0).astype(F32)
    lidx = lax.broadcasted_iota(jnp.int32, (GROUP_SIZE, tr), 0).astype(F32)
    gidx = lax.broadcasted_iota(jnp.int32, (N_GROUPS, tr), 0).astype(F32)
    gs = []
    for g in range(N_GROUPS):
        blk = biased[g * GROUP_SIZE:(g + 1) * GROUP_SIZE]
        m1, first = _first_argmax(blk, lidx, GROUP_SIZE)
        m2 = jnp.max(jnp.where(lidx == first, -jnp.inf, blk), axis=0, keepdims=True)
        gs.append(m1 + m2)
    cur = jnp.concatenate(gs, axis=0)
    keep = jnp.zeros((N_GROUPS, tr), F32)
    for _ in range(TOPK_GROUPS):
        _, first = _first_argmax(cur, gidx, N_GROUPS)
        sel = gidx == first
        keep = jnp.where(sel, 1.0, keep)
        cur = jnp.where(sel, -jnp.inf, cur)
    keep_e = jnp.concatenate([jnp.broadcast_to(keep[g:g + 1], (GROUP_SIZE, tr)) for g in range(N_GROUPS)], axis=0)
    masked = jnp.where(keep_e > 0.5, biased, -jnp.inf)
    chosen = jnp.zeros((N_EXPERTS, tr), F32)
    tops, topi = [], []
    for _ in range(TOP_K):
        _, first = _first_argmax(masked, eidx, N_EXPERTS)
        sel = eidx == first
        tops.append(jnp.sum(jnp.where(sel, scores, 0.0), axis=0, keepdims=True))
        topi.append(first)
        chosen = jnp.where(sel, 1.0, chosen)
        masked = jnp.where(sel, -jnp.inf, masked)
    top_s = jnp.concatenate(tops, axis=0)
    top_i = jnp.concatenate(topi, axis=0)
    e_ref[...] = top_i.astype(jnp.int32)
    w_ref[...] = (top_s / jnp.sum(top_s, axis=0, keepdims=True) * ROUTED_SCALE).T
    prior = jnp.dot(chosen.astype(BF16), before_ref[...], preferred_element_type=F32) + carry_ref[...]
    ranks = [jnp.sum(jnp.where(eidx == topi[kk], prior, 0.0), axis=0, keepdims=True) for kk in range(TOP_K)]
    rank_ref[...] = jnp.concatenate(ranks, axis=0).astype(jnp.int32)
    carry_ref[...] = carry_ref[...] + jnp.sum(chosen, axis=1, keepdims=True)
    cnt_ref[...] = jnp.broadcast_to(carry_ref[...], cnt_ref.shape).astype(jnp.int32)


def _route(scores_t, router_bias, tile):
    n = scores_t.shape[1]
    col = lambda i: (0, i)
    out_shape = [jax.ShapeDtypeStruct((TOP_K, n), jnp.int32), jax.ShapeDtypeStruct((n, TOP_K), F32),
                 jax.ShapeDtypeStruct((TOP_K, n), jnp.int32), jax.ShapeDtypeStruct((N_EXPERTS, LANES), jnp.int32)]
    before = jnp.asarray(np.arange(tile)[:, None] < np.arange(tile)[None, :], BF16)
    return pl.pallas_call(
        _route_kernel, out_shape=out_shape, grid=(n // tile,),
        in_specs=[pl.BlockSpec((N_EXPERTS, tile), col), pl.BlockSpec((N_EXPERTS, 1), lambda i: (0, 0)),
                  pl.BlockSpec((tile, tile), lambda i: (0, 0), pipeline_mode=pl.Buffered(1))],
        out_specs=[pl.BlockSpec((TOP_K, tile), col), pl.BlockSpec((tile, TOP_K), lambda i: (i, 0)),
                   pl.BlockSpec((TOP_K, tile), col), pl.BlockSpec((N_EXPERTS, LANES), lambda i: (0, 0))],
        scratch_shapes=[pltpu.VMEM((N_EXPERTS, 1), F32)],
        compiler_params=_params(1), name="route",
    )(scores_t, router_bias.reshape(N_EXPERTS, 1), before)


def _slots_kernel(e_ref, rank_ref, start_ref, dest_ref):
    tr = e_ref.shape[1]
    eidx = lax.broadcasted_iota(jnp.int32, (N_EXPERTS, tr), 0)
    e = e_ref[...]
    start = start_ref[...]
    rows = [jnp.sum(jnp.where(eidx == e[kk:kk + 1], start, 0.0), axis=0, keepdims=True) for kk in range(TOP_K)]
    dest_ref[...] = jnp.concatenate(rows, axis=0).astype(jnp.int32) + rank_ref[...]


def _slots(top_e, rank, start_rows, tile):
    n = top_e.shape[1]
    col = lambda i: (0, i)
    return pl.pallas_call(
        _slots_kernel, out_shape=jax.ShapeDtypeStruct((TOP_K, n), jnp.int32), grid=(n // tile,),
        in_specs=[pl.BlockSpec((TOP_K, tile), col), pl.BlockSpec((TOP_K, tile), col),
                  pl.BlockSpec((N_EXPERTS, 1), lambda i: (0, 0))],
        out_specs=pl.BlockSpec((TOP_K, tile), col),
        compiler_params=_params(1), name="slots",
    )(top_e, rank, start_rows)


def _sc_mesh():
    return plsc.VectorSubcoreMesh(core_axis_name="c", subcore_axis_name="s",
                                  num_cores=SC_CORES, num_subcores=SC_SUBCORES)


def _sc_scatter_rows(rows, idx, n_out):
    n, dp = rows.shape
    copies = idx.shape[0] // n
    per_worker = n // (SC_CORES * SC_SUBCORES)
    chunk = SC_GATHER_ROWS

    @functools.partial(
        pl.kernel, mesh=_sc_mesh(), out_type=jax.ShapeDtypeStruct((n_out, dp), rows.dtype),
        scratch_types=[pltpu.VMEM((chunk,), jnp.int32), pltpu.VMEM((chunk, dp), rows.dtype)],
        name="sc_scatter_rows")
    def scatter(rows_hbm, idx_hbm, out_hbm, idx_v, rows_v):
        base = (lax.axis_index("s") * SC_CORES + lax.axis_index("c")) * per_worker

        @pl.loop(0, per_worker // chunk)
        def _(j):
            off = base + j * chunk
            pltpu.sync_copy(rows_hbm.at[pl.ds(off, chunk)], rows_v)
            for k in range(copies):
                pltpu.sync_copy(idx_hbm.at[pl.ds(k * n + off, chunk)], idx_v)
                pltpu.sync_copy(rows_v, out_hbm.at[idx_v])

    return scatter(rows, idx)


def _expert_kernel(start_ref, cnt_ref, nused_ref, xs_ref, wg_ref, wu_ref, wd_ref, y_ref,
                   wgb, wub, wdb, xbuf, ybuf, sem_in, sem_out):
    e = pl.program_id(0)
    bm, ring = EXPERT_BLOCK, EXPERT_RING
    cnt = cnt_ref[e]
    nb = (cnt + bm - 1) // bm
    g0 = start_ref[e]
    n_used = nused_ref[0]

    def x_copy(g):
        slot = g % ring
        return pltpu.make_async_copy(xs_ref.at[pl.ds(g * bm, bm)], xbuf.at[slot], sem_in.at[slot])

    def y_copy(g):
        slot = g % ring
        return pltpu.make_async_copy(ybuf.at[slot], y_ref.at[pl.ds(g * bm, bm)], sem_out.at[slot])

    @pl.when(e == 0)
    def _():
        for g in range(ring):
            @pl.when(g < n_used)
            def _():
                x_copy(g).start()

    wgb[...] = wg_ref[...].astype(BF16)
    wub[...] = wu_ref[...].astype(BF16)
    wdb[...] = wd_ref[...].astype(BF16)
    row = lax.broadcasted_iota(jnp.int32, (bm, xbuf.shape[2]), 0)

    def blocks(j, count):
        for b in range(count):
            g = g0 + j + b
            x_copy(g).wait()

            @pl.when(g >= ring)
            def _():
                y_copy(g - ring).wait()

        def swiglu(b):
            slot = (g0 + j + b) % ring
            x = _unpack_pairs(jnp.where(row < cnt - (j + b) * bm, xbuf[slot], jnp.uint32(0))).astype(BF16)
            yield
            gate = jnp.dot(x, wgb[...], preferred_element_type=F32)
            up = jnp.dot(x, wub[...], preferred_element_type=F32)
            yield
            act = (_silu(gate) * up).astype(BF16)
            yield
            y = jnp.dot(act, wdb[...], preferred_element_type=F32)
            yield
            ybuf[slot] = _pack_pairs(y)

        _interleave([swiglu(b) for b in range(count)])

        for b in range(count):
            g = g0 + j + b
            y_copy(g).start()

            @pl.when(g + ring < n_used)
            def _():
                x_copy(g + ring).start()

    group = EXPERT_GROUP

    def full_group(p, carry):
        blocks(group * p, group)
        return carry

    lax.fori_loop(0, nb // group, full_group, 0)
    done = (nb // group) * group
    size = group // 2
    while size >= 1:
        @pl.when((nb - done) % (2 * size) >= size)
        def _(done=done, size=size):
            blocks(done, size)

        done = done + jnp.where((nb - done) % (2 * size) >= size, size, 0)
        size //= 2

    @pl.when(e == pl.num_programs(0) - 1)
    def _():
        for back in range(1, ring + 1):
            @pl.when(n_used - back >= 0)
            def _():
                y_copy(n_used - back).wait()


def _experts(blk_start, counts, n_used, xs, wg, wu, wd):
    n_slots, dp = xs.shape
    n_exp, d, ff = wg.shape
    bm, ring = EXPERT_BLOCK, EXPERT_RING
    wsel = lambda e, st, ct, nu: (e, 0, 0)
    grid_spec = pltpu.PrefetchScalarGridSpec(
        num_scalar_prefetch=3, grid=(n_exp,),
        in_specs=[pl.BlockSpec(memory_space=pl.ANY),
                  pl.BlockSpec((None, d, ff), wsel), pl.BlockSpec((None, d, ff), wsel),
                  pl.BlockSpec((None, ff, d), wsel)],
        out_specs=pl.BlockSpec(memory_space=pl.ANY),
        scratch_shapes=[pltpu.VMEM((d, ff), BF16), pltpu.VMEM((d, ff), BF16), pltpu.VMEM((ff, d), BF16),
                        pltpu.VMEM((ring, bm, dp), U32), pltpu.VMEM((ring, bm, dp), U32),
                        pltpu.SemaphoreType.DMA((ring,)), pltpu.SemaphoreType.DMA((ring,))])
    return pl.pallas_call(
        _expert_kernel, out_shape=jax.ShapeDtypeStruct((n_slots, dp), U32), grid_spec=grid_spec,
        compiler_params=_params(1), name="experts",
    )(blk_start, counts, n_used, xs, wg, wu, wd)


def _sc_gather_rows(table, idx):
    n_idx = idx.shape[0]
    dp = table.shape[1]
    workers = SC_CORES * SC_SUBCORES
    per_worker = n_idx // workers
    chunk = SC_GATHER_ROWS

    @functools.partial(
        pl.kernel, mesh=_sc_mesh(), out_type=jax.ShapeDtypeStruct((n_idx, dp), table.dtype),
        scratch_types=[pltpu.VMEM((chunk,), jnp.int32), pltpu.VMEM((chunk, dp), table.dtype),
                       pltpu.SemaphoreType.DMA],
        name="sc_gather_rows")
    def gather(table_hbm, idx_hbm, out_hbm, idx_v, rows_v, sem):
        base = (lax.axis_index("s") * SC_CORES + lax.axis_index("c")) * per_worker

        @pl.loop(0, per_worker // chunk)
        def _(j):
            off = base + j * chunk
            pltpu.sync_copy(idx_hbm.at[pl.ds(off, chunk)], idx_v)
            pltpu.async_copy(table_hbm.at[idx_v], rows_v, sem).wait()
            pltpu.sync_copy(rows_v, out_hbm.at[pl.ds(off, chunk)])

    return gather(table, idx)


def _combine_kernel(tile, yt_ref, w_ref, h_ref, x1_ref, g2_ref, sg_ref, su_ref, sd_ref, lg_ref, lb_ref, *rest):
    o_ref = rest[-1]
    hb = _unpack_pairs(h_ref[...]).astype(BF16)
    g = jnp.dot(hb, sg_ref[...], preferred_element_type=F32)
    u = jnp.dot(hb, su_ref[...], preferred_element_type=F32)
    f = jnp.dot((_silu(g) * u).astype(BF16), sd_ref[...], preferred_element_type=F32)
    w = w_ref[...]
    for kk in range(TOP_K):
        f = f + w[:, kk:kk + 1] * _unpack_pairs(yt_ref[kk * tile:(kk + 1) * tile, :])
    o_ref[...] = _layer_norm(DEEPNORM_ALPHA * x1_ref[...] + g2_ref[0] * f, lg_ref[...], lb_ref[...])


def _combine(y_tok, top_w, h2p, x1, g2, sg, su, sd, lg, lb, tile, tiles_per_batch, first_tile, prev_out):
    n, d = x1.shape
    dp = h2p.shape[1]
    row = lambda i: (first_tile + i, 0)
    full = lambda i: (0, 0)

    def const(a):
        return pl.BlockSpec(a.shape, full, pipeline_mode=pl.Buffered(1))

    in_specs = [pl.BlockSpec((TOP_K * tile, dp), lambda i: (i, 0)),
                pl.BlockSpec((tile, TOP_K), row), pl.BlockSpec((tile, dp), row), pl.BlockSpec((tile, d), row),
                pl.BlockSpec((1, 1, d), lambda i: ((first_tile + i) // tiles_per_batch, 0, 0)),
                const(sg), const(su), const(sd), const(lg), const(lb)]
    args = [y_tok, top_w, h2p, x1, g2, sg, su, sd, lg, lb]
    aliases = {}
    if prev_out is not None:
        in_specs.append(pl.BlockSpec(memory_space=pl.ANY))
        args.append(prev_out)
        aliases = {len(args) - 1: 0}
    return pl.pallas_call(
        functools.partial(_combine_kernel, tile),
        out_shape=jax.ShapeDtypeStruct((n, d), F32),
        grid=(y_tok.shape[0] // (TOP_K * tile),),
        in_specs=in_specs,
        out_specs=pl.BlockSpec((tile, d), row),
        input_output_aliases=aliases,
        compiler_params=_params(1), name="combine_shared_ln2",
    )(*args)


def kernel(x, c, ctx, c_ctx, w_mod, b_mod, w_in, na_rpb, gla_w_decay_f, gla_b_decay_f, gla_w_decay_b, gla_b_decay_b,
           gla_norm_w, w_branch_a, w_branch_b, w_out, ln1_g, ln1_b, router_w, router_bias, exp_w_gate, exp_w_up,
           exp_w_down, sh_w_gate, sh_w_up, sh_w_down, ln2_g, ln2_b):
    batch, t_len, d = x.shape
    ctx_len = ctx.shape[1]
    n = batch * t_len
    assert w_mod.shape[0] == DEPTH == 1
    assert t_len % GLA_CHUNK == 0 and ctx_len == GLA_CHUNK and (t_len // GRID_W) % (NA_ROW_BLOCK * NA_BLOCKS_PER_STEP) == 0

    mod_rows = 16
    c_all = jnp.zeros((mod_rows, d), F32).at[:batch].set(c).at[batch].set(c_ctx)
    mod = _modulation(c_all, w_mod[0], b_mod[0])
    sh1, sc1, g1, sh2, sc2, g2 = [mod[:batch, j * d:(j + 1) * d].reshape(batch, 1, d) for j in range(6)]
    sh1c = mod[batch:batch + 1, 0:d].reshape(1, 1, d)
    sc1c = mod[batch:batch + 1, d:2 * d].reshape(1, 1, d)

    offs = np.cumsum((0, NA_WIDTH, NA_WIDTH, NA_WIDTH, GLA_QK_WIDTH, GLA_QK_WIDTH, GLA_V_WIDTH, GLA_V_WIDTH,
                      GLA_GATE_RANK, GLA_GATE_RANK, d, d))
    qa, ka, va, qb, kb, vbc, gbc, lrf, lrb, ga, gbt = [w_in[0][:, offs[j]:offs[j + 1]] for j in range(11)]
    lr_cols = jnp.concatenate([lrf, lrb, jnp.zeros((d, LANES - 2 * GLA_GATE_RANK), F32)], axis=1)
    w_lat = jnp.concatenate([lr_cols, qa, ka, va, qb, kb, vbc, gbc, ga, gbt], axis=1).astype(BF16)
    lat_offs = np.cumsum((0, LANES, NA_WIDTH, NA_WIDTH, NA_WIDTH, GLA_QK_WIDTH, GLA_QK_WIDTH, GLA_V_WIDTH))
    w_ctx = jnp.concatenate([w_lat[:, lat_offs[j]:lat_offs[j + 1]] for j in (0, 2, 3, 5, 6)], axis=1)
    plain = ("plain",)
    lat_plan = ((LANES, (("decay",),)),
                (3 * NA_WIDTH, (("scale", NA_HEAD_DIM ** -0.5), plain, plain)),
                (2 * GLA_QK_WIDTH, (("rope", GLA_DK ** -0.5), ("rope", 1.0))),
                (GLA_V_WIDTH, (plain, plain)), (GLA_V_WIDTH, (plain, plain)),
                (2 * d, (plain,) * 4))
    ctx_plan = ((LANES, (("decay",),)),
                (2 * NA_WIDTH, (plain, plain)), (GLA_QK_WIDTH, (plain,)), (GLA_V_WIDTH, (plain, plain)))
    w2, b2 = _decay_weights(gla_w_decay_f[0], gla_b_decay_f[0], gla_w_decay_b[0], gla_b_decay_b[0])
    tile = COMBINE_TILE
    x2d = x.reshape(n, d)
    decay, na_qkv, gla_qk, vb, gb, gates = _projection(
        x2d, sc1, sh1, w_lat, w2, b2, lat_plan, (BF16,) * 6, PROJ_TILE, t_len // PROJ_TILE,
        rope=_rope_tables(t_len))
    decay_c, na_kv_c, k_c, v_c = _projection(
        ctx.reshape(batch * ctx_len, d), sc1c, sh1c, w_ctx, w2, b2, ctx_plan, (BF16,) * 4, tile,
        batch * ctx_len // tile)

    out_a = _na_attention(na_qkv, na_kv_c, _na_bias_tables(na_rpb[0]), batch, t_len, ctx_len)
    o_f, o_b = _gla(gla_qk, vb, decay, k_c, v_c, decay_c, batch, t_len, ctx_len)

    x1, h2p, scores_t = _merge(
        out_a.reshape(n, NA_WIDTH), o_f.reshape(n, GLA_V_WIDTH), o_b.reshape(n, GLA_V_WIDTH), gb, gates, x2d,
        g1, sc2, sh2, w_branch_a[0].astype(BF16), w_branch_b[0].astype(BF16), w_out[0].astype(BF16),
        gla_norm_w[0].reshape(1, -1), ln1_g[0].reshape(1, d), ln1_b[0].reshape(1, d),
        router_w[0].T.astype(BF16), PROJ_TILE, t_len // PROJ_TILE)

    top_e, top_w, rank, counts = _route(scores_t, router_bias[0], 512)

    counts = counts[:, 0]
    n_blocks = pl.cdiv(n * TOP_K, EXPERT_BLOCK) + N_EXPERTS
    blocks_per = (counts + EXPERT_BLOCK - 1) // EXPERT_BLOCK
    blk_end = jnp.cumsum(blocks_per)
    blk_start = blk_end - blocks_per
    dest = _slots(top_e, rank, (blk_start * EXPERT_BLOCK).astype(F32).reshape(N_EXPERTS, 1), 2048)

    xs = _sc_scatter_rows(h2p, dest.reshape(TOP_K * n), n_blocks * EXPERT_BLOCK)
    y = _experts(blk_start.astype(jnp.int32), counts, blk_end[-1:].astype(jnp.int32), xs,
                 exp_w_gate[0], exp_w_up[0], exp_w_down[0])
    dest_tok = dest.reshape(TOP_K, n // tile, tile).transpose(1, 0, 2).reshape(COMBINE_CHUNKS, -1)
    w_rows = top_w
    shared = (sh_w_gate[0].astype(BF16), sh_w_up[0].astype(BF16), sh_w_down[0].astype(BF16),
              ln2_g[0].reshape(1, d), ln2_b[0].reshape(1, d))
    tiles_per_chunk = n // tile // COMBINE_CHUNKS
    out = None
    for ci in range(COMBINE_CHUNKS):
        y_tok = _sc_gather_rows(y, dest_tok[ci])
        out = _combine(y_tok, w_rows, h2p, x1, g2, *shared, tile, t_len // tile, ci * tiles_per_chunk, out)
    return out.reshape(batch, t_len, d)
```

```python
import functools

import numpy as np
import jax
import jax.numpy as jnp
from jax import lax
from jax.experimental import pallas as pl
from jax.experimental.pallas import tpu as pltpu
from jax.experimental.pallas import tpu_sc as plsc

F32 = jnp.float32
BF16 = jnp.bfloat16
U32 = jnp.uint32
HIGHEST = lax.Precision.HIGHEST

GRID_W = 64
NA_HEADS = 8
NA_HEAD_DIM = 64
NA_WIDTH = NA_HEADS * NA_HEAD_DIM
WIN_ROWS = 8
WIN_COLS = 16
GLA_HEADS = 4
GLA_DK = 128
GLA_DV = 256
GLA_QK_WIDTH = GLA_HEADS * GLA_DK
GLA_V_WIDTH = GLA_HEADS * GLA_DV
GLA_GATE_RANK = 16
GLA_TAU = 16.0
LOG2E = 1.4426950408889634
ROPE_BASE = 10000.0
N_EXPERTS = 256
TOP_K = 8
N_GROUPS = 8
TOPK_GROUPS = 4
GROUP_SIZE = N_EXPERTS // N_GROUPS
ROUTED_SCALE = 2.5
DEPTH = 1
DEEPNORM_ALPHA = (2 * DEPTH) ** 0.25
EPS = 1e-6

LANES = 128
PROJ_TILE = 512
PROJ_INTERLEAVE = 2
NA_ROW_BLOCK = 4
NA_BLOCKS_PER_STEP = 2
NA_INTERLEAVE = 4
NA_UNION_ROWS = NA_ROW_BLOCK + WIN_ROWS - 1
GLA_CHUNK = 256
GLA_DIAG = 16
GLA_HEADS_PER_STEP = 4
GLA_INTERLEAVE = 8
EXPERT_BLOCK = 272
EXPERT_GROUP = 4
EXPERT_RING = 8
COMBINE_CHUNKS = 8
COMBINE_TILE = 512
SC_CORES = 2
SC_SUBCORES = 16
SC_GATHER_ROWS = 128
NEG_BIG = -1e30
VMEM_LIMIT = 56 * 1024 * 1024
NA_VMEM_LIMIT = 58 * 1024 * 1024


def _params(n_axes, vmem=VMEM_LIMIT):
    return pltpu.CompilerParams(dimension_semantics=("arbitrary",) * n_axes, vmem_limit_bytes=vmem)


def _sigmoid(v):
    return 1.0 / (1.0 + jnp.exp2(v * (-LOG2E)))


def _silu(v):
    return v * _sigmoid(v)


def _interleave(chains, group=None):
    if group is not None and group < len(chains):
        return [r for i in range(0, len(chains), group) for r in _interleave(chains[i:i + group])]
    results = [None] * len(chains)
    active = list(enumerate(chains))
    while active:
        still = []
        for i, chain in active:
            try:
                next(chain)
                still.append((i, chain))
            except StopIteration as stop:
                results[i] = stop.value
        active = still
    return results


def _pack_pairs(v):
    m = v.shape[1] // 2
    lo = lax.bitcast_convert_type(v[:, :m].astype(BF16).astype(F32), U32) >> 16
    hi = lax.bitcast_convert_type(v[:, m:].astype(BF16).astype(F32), U32) & jnp.uint32(0xFFFF0000)
    return lo | hi


def _unpack_pairs(p):
    lo = lax.bitcast_convert_type(p << 16, F32)
    hi = lax.bitcast_convert_type(p & jnp.uint32(0xFFFF0000), F32)
    return jnp.concatenate([lo, hi], axis=1)


def _mod_kernel(c_ref, w_ref, b_ref, o_ref):
    o_ref[...] = jnp.dot(_silu(c_ref[...]), w_ref[...], preferred_element_type=F32, precision=HIGHEST) + b_ref[...]


def _modulation(c_all, w_mod, b_mod):
    rows, d = c_all.shape
    n = w_mod.shape[1]
    bn = 512
    return pl.pallas_call(
        _mod_kernel,
        out_shape=jax.ShapeDtypeStruct((rows, n), F32),
        grid=(n // bn,),
        in_specs=[pl.BlockSpec((rows, d), lambda j: (0, 0)),
                  pl.BlockSpec((d, bn), lambda j: (0, j)),
                  pl.BlockSpec((1, bn), lambda j: (0, j))],
        out_specs=pl.BlockSpec((rows, bn), lambda j: (0, j)),
        compiler_params=_params(1),
        name="modulation",
    )(c_all, w_mod, b_mod.reshape(1, n))


def _swap32(v):
    lane = lax.broadcasted_iota(jnp.int32, v.shape, 1)
    return jnp.where((lane % 64) < 32, pltpu.roll(v, 96, 1), pltpu.roll(v, 32, 1))


GLA_DECAY_WIDTH = 2 * GLA_HEADS * 2 * GLA_DK


def _log2_decay_split(lr, w2, b2, out_ref):
    z = (jnp.dot(lr.astype(BF16), w2, preferred_element_type=F32) + b2) * LOG2E
    a = (jnp.minimum(z, 0.0) - jnp.log2(1.0 + jnp.exp2(-jnp.abs(z)))) * (1.0 / GLA_TAU)
    hi = a.astype(BF16)
    lo = (a - hi.astype(F32)).astype(BF16)
    for p in range(a.shape[1] // GLA_DK):
        src = slice(p * GLA_DK, (p + 1) * GLA_DK)
        out_ref[:, 2 * p * GLA_DK:(2 * p + 1) * GLA_DK] = hi[:, src]
        out_ref[:, (2 * p + 1) * GLA_DK:(2 * p + 2) * GLA_DK] = lo[:, src]


def _proj_kernel(plan, has_rope, *refs):
    x_ref, sc_ref, sh_ref, w_ref, w2_ref, b2_ref = refs[:6]
    pos = 6
    if has_rope:
        cos_ref, sin_ref = refs[6:8]
        pos = 8
    out_refs = refs[pos:]
    h = (x_ref[...] * (1.0 + sc_ref[0]) + sh_ref[0]).astype(BF16)

    def piece(out_ref, kind, w0, c0, cw):
        acc = jnp.dot(h, w_ref[:, w0:w0 + cw], preferred_element_type=F32)
        yield
        if kind[0] == "decay":
            _log2_decay_split(acc, w2_ref[...], b2_ref[...], out_ref)
            return
        if kind[0] == "scale":
            acc = acc * kind[1]
        elif kind[0] == "rope":
            cos, sin = cos_ref[...], sin_ref[...]
            parts = []
            for p in range(cw // LANES):
                v = acc[:, p * LANES:(p + 1) * LANES]
                parts.append((v * cos + _swap32(v) * sin) * kind[1])
            acc = jnp.concatenate(parts, axis=1)
        out_ref[:, c0:c0 + cw] = acc.astype(out_ref.dtype)

    chains = []
    col = 0
    for out_ref, (width, kinds) in zip(out_refs, plan):
        cw = width // len(kinds)
        for j, kind in enumerate(kinds):
            chains.append(piece(out_ref, kind, col + j * cw, j * cw, cw))
        col += width
    for i in range(0, len(chains), PROJ_INTERLEAVE):
        _interleave(chains[i:i + PROJ_INTERLEAVE])


def _projection(x2d, sc, sh, w, w2, b2, plan, out_dtypes, tile, tiles_per_mod, rope=None):
    n, d = x2d.shape
    const = lambda a: pl.BlockSpec(a.shape, lambda i: (0, 0), pipeline_mode=pl.Buffered(1))
    in_specs = [pl.BlockSpec((tile, d), lambda i: (i, 0)),
                pl.BlockSpec((1, 1, d), lambda i: (i // tiles_per_mod, 0, 0)),
                pl.BlockSpec((1, 1, d), lambda i: (i // tiles_per_mod, 0, 0)),
                const(w), const(w2), const(b2)]
    args = [x2d, sc, sh, w, w2, b2]
    if rope is not None:
        in_specs += [pl.BlockSpec((tile, LANES), lambda i: (i % tiles_per_mod, 0))] * 2
        args += list(rope)
    widths = [GLA_DECAY_WIDTH if kinds[0][0] == "decay" else wd for wd, kinds in plan]
    out_shape = [jax.ShapeDtypeStruct((n, wd), dt) for wd, dt in zip(widths, out_dtypes)]
    out_specs = [pl.BlockSpec((tile, wd), lambda i: (i, 0)) for wd in widths]
    return pl.pallas_call(
        functools.partial(_proj_kernel, plan, rope is not None),
        out_shape=out_shape,
        grid=(n // tile,),
        in_specs=in_specs,
        out_specs=out_specs,
        compiler_params=_params(1),
        name="in_proj" if rope is not None else "ctx_proj",
    )(*args)


def _rope_tables(t_len):
    half = GLA_DK // 2
    quarter = half // 2
    f32 = np.float32
    inv_freq = f32(ROPE_BASE) ** (-np.arange(quarter, dtype=f32) / f32(quarter))
    pos = np.arange(t_len)
    row_ang = (pos // GRID_W).astype(f32)[:, None] * inv_freq[None, :]
    col_ang = (pos % GRID_W).astype(f32)[:, None] * inv_freq[None, :]
    cr, sr, cc, sn = np.cos(row_ang), np.sin(row_ang), np.cos(col_ang), np.sin(col_ang)
    cos = np.concatenate([cr, cr, cc, cc], axis=1).astype(f32)
    sin = np.concatenate([-sr, sr, -sn, sn], axis=1).astype(f32)
    return jnp.asarray(cos), jnp.asarray(sin)


def _na_bias_tables(rpb):
    rb, ur, w = NA_ROW_BLOCK, NA_UNION_ROWS, GRID_W
    heads = rpb.shape[0]
    pad = jnp.pad(rpb, ((0, 0), (0, 0), (w, w)))
    toep = jnp.stack([pad[:, :, w + WIN_COLS - 1 - c:2 * w + WIN_COLS - 1 - c] for c in range(w)], axis=2)
    c = np.arange(w)[:, None]
    kc = np.arange(w)[None, :]
    col_start = np.clip(c - WIN_COLS // 2, 0, w - WIN_COLS)
    col_ok = (kc >= col_start) & (kc < col_start + WIN_COLS)
    toep = jnp.where(col_ok[None, None], toep, NEG_BIG)
    neg = jnp.full((heads, w, w), NEG_BIG, F32)
    half = WIN_ROWS // 2
    tables = []
    for lo, off in ((lambda i: 0, WIN_ROWS - 1), (lambda i: i, WIN_ROWS - 1 - half), (lambda i: ur - WIN_ROWS, 0)):
        rows_ = []
        for i in range(rb):
            blocks = [toep[:, j - i + off] if lo(i) <= j < lo(i) + WIN_ROWS else neg for j in range(ur)]
            rows_.append(jnp.concatenate(blocks, axis=2))
        tables.append(jnp.concatenate(rows_, axis=1))
    return jnp.stack(tables)


def _na_kernel(rows, q_ref, k_ref, v_ref, kc_ref, vc_ref, *rest):
    bias_refs, o_ref = rest[:-1], rest[-1]
    tq = NA_ROW_BLOCK * GRID_W
    nk = NA_UNION_ROWS * GRID_W
    nt = (((1,), (1,)), ((), ()))
    lane = lax.broadcasted_iota(jnp.int32, (tq, LANES), 1)
    def head(qm, ku, vu, kc, vc, bias):
        s_loc = lax.dot_general(qm, ku, nt, preferred_element_type=F32) + bias
        s_ctx = lax.dot_general(qm, kc, nt, preferred_element_type=F32)
        yield
        m = jnp.maximum(jnp.max(s_loc, axis=1, keepdims=True), jnp.max(s_ctx, axis=1, keepdims=True))
        p_loc = jnp.exp(s_loc - m)
        p_ctx = jnp.exp(s_ctx - m)
        yield
        denom = jnp.sum(p_loc, axis=1, keepdims=True) + jnp.sum(p_ctx, axis=1, keepdims=True)
        o = (jnp.dot(p_loc.astype(BF16), vu, preferred_element_type=F32)
             + jnp.dot(p_ctx.astype(BF16), vc, preferred_element_type=F32))
        yield
        return o / denom

    chains = []
    for sub, bias_ref in enumerate(bias_refs):
        rb = pl.program_id(1) * len(bias_refs) + sub
        ustart = jnp.clip(rb * NA_ROW_BLOCK - WIN_ROWS // 2, 0, rows - NA_UNION_ROWS)
        k0 = pl.multiple_of(ustart * GRID_W, GRID_W)
        qrows = slice(sub * tq, (sub + 1) * tq)
        for p in range(NA_WIDTH // LANES):
            ls = slice(p * LANES, (p + 1) * LANES)
            q = q_ref[qrows, ls]
            ku = k_ref[pl.ds(k0, nk), ls]
            vu = v_ref[pl.ds(k0, nk), ls]
            for hh in range(LANES // NA_HEAD_DIM):
                qm = jnp.where((lane // NA_HEAD_DIM) == hh, q, jnp.zeros_like(q))
                chains.append(head(qm, ku, vu, kc_ref[:, ls], vc_ref[:, ls], bias_ref[p * 2 + hh]))
    outs = _interleave(chains, NA_INTERLEAVE)
    pairs = NA_WIDTH // LANES
    for sub in range(len(bias_refs)):
        for p in range(pairs):
            i = 2 * (sub * pairs + p)
            out = jnp.where((lane // NA_HEAD_DIM) == 0, outs[i], outs[i + 1])
            o_ref[sub * tq:(sub + 1) * tq, p * LANES:(p + 1) * LANES] = out.astype(o_ref.dtype)


def _na_attention(na_qkv, na_kv_ctx, bias_tables, batch, t_len, ctx_len):
    rows = t_len // GRID_W
    n_rb = rows // NA_ROW_BLOCK
    per = NA_BLOCKS_PER_STEP
    tq = per * NA_ROW_BLOCK * GRID_W
    qkv = na_qkv.reshape(batch, t_len, 3 * NA_WIDTH)
    kvc = na_kv_ctx.reshape(batch, ctx_len, 2 * NA_WIDTH)

    def bias_spec(sub):
        def idx(b, r):
            rb = r * per + sub
            return (jnp.where(rb == 0, 0, jnp.where(rb == n_rb - 1, 2, 1)), 0, 0, 0)
        return pl.BlockSpec((None,) + bias_tables.shape[1:], idx)

    return pl.pallas_call(
        functools.partial(_na_kernel, rows),
        out_shape=jax.ShapeDtypeStruct((batch, t_len, NA_WIDTH), BF16),
        grid=(batch, n_rb // per),
        in_specs=[pl.BlockSpec((None, tq, NA_WIDTH), lambda b, r: (b, r, 0)),
                  pl.BlockSpec((None, t_len, NA_WIDTH), lambda b, r: (b, 0, 1)),
                  pl.BlockSpec((None, t_len, NA_WIDTH), lambda b, r: (b, 0, 2)),
                  pl.BlockSpec((None, ctx_len, NA_WIDTH), lambda b, r: (b, 0, 0)),
                  pl.BlockSpec((None, ctx_len, NA_WIDTH), lambda b, r: (b, 0, 1))]
                 + [bias_spec(sub) for sub in range(per)],
        out_specs=pl.BlockSpec((None, tq, NA_WIDTH), lambda b, r: (b, r, 0)),
        compiler_params=_params(2, vmem=NA_VMEM_LIMIT),
        name="na_attention",
    )(qkv, qkv, qkv, kvc, kvc, *([bias_tables] * per))


def _gla_constants(c):
    tris, masks = [], []
    for reverse in (False, True):
        i = np.arange(c)[:, None]
        j = np.arange(c)[None, :]
        tris.append((j >= i) if reverse else (j <= i))
        i = np.arange(c // 2)[:, None]
        j = np.arange(c // 2)[None, :]
        if reverse:
            i, j = j, i
        level = []
        s = c // 4
        while s >= GLA_DIAG:
            level.append(((i // (2 * s)) == (j // (2 * s))) & ((i % (2 * s)) >= s) & ((j % (2 * s)) < s))
            s //= 2
        level.append(((i // GLA_DIAG) == (j // GLA_DIAG)) & (j <= i))
        masks.append(np.stack(level))
    return jnp.asarray(np.stack(tris), BF16), jnp.asarray(np.stack(masks), F32)


def _block_refs(cum, s, reverse, diag):
    c = cum.shape[0]
    span = s if diag else 2 * s
    parts = []
    for p in range(c // span):
        if diag:
            r = p * span + (span - 1 if reverse else 0)
        else:
            r = p * span + (s - 1 if reverse else s)
        parts.append(jnp.broadcast_to(cum[r:r + 1, :], (span, cum.shape[1])))
    return jnp.concatenate(parts, axis=0)


def _cumsum_rows(a_hl, tri):
    parts = jnp.dot(tri, a_hl, preferred_element_type=F32)
    w = a_hl.shape[1] // 2
    return parts[:, :w] + parts[:, w:]


def _gla_chunk(q, k, v, a, state_t, tri, mask_ref, reverse, want_out):
    c = k.shape[0]
    hc = c // 2
    nt = (((1,), (1,)), ((), ()))
    k = k.astype(F32)
    q = q.astype(F32) if want_out else None
    cum = _cumsum_rows(a, tri)
    last = 0 if reverse else c - 1
    total = cum[last:last + 1, :]
    yield

    def scale(x, log2_factor):
        return (x * jnp.exp2(log2_factor)).astype(BF16)

    out = None
    if want_out:
        out = lax.dot_general(scale(q, cum), state_t.astype(BF16), nt, preferred_element_type=F32)
        halves = (slice(hc, c), slice(0, hc)) if reverse else (slice(0, hc), slice(hc, c))
        early, late = halves
        r = hc - 1 if reverse else hc
        g = cum[r:r + 1, :]
        cross = lax.dot_general(scale(q[late], cum[late] - g), scale(k[early], g - cum[early]), nt,
                                preferred_element_type=F32)
        yield
        inner = [jnp.zeros((hc, hc), F32), jnp.zeros((hc, hc), F32)]
        s = hc // 2
        level = 0
        while True:
            diag = s < GLA_DIAG
            for x, rows in enumerate(halves):
                cx, qx, kx = cum[rows], q[rows], k[rows]
                d = cx - _block_refs(cx, GLA_DIAG if diag else s, reverse, diag)
                if diag:
                    qs, ks = scale(qx, d), scale(kx, -d)
                else:
                    e = jnp.exp2(-jnp.abs(d))
                    qs, ks = (qx * e).astype(BF16), (kx * e).astype(BF16)
                inner[x] = inner[x] + lax.dot_general(qs, ks, nt, preferred_element_type=F32) * mask_ref[level]
            yield
            if diag:
                break
            s //= 2
            level += 1
        out_early = jnp.dot(inner[0].astype(BF16), v[early], preferred_element_type=F32)
        out_late = jnp.dot(jnp.concatenate([cross, inner[1]], axis=1).astype(BF16),
                           jnp.concatenate([v[early], v[late]], axis=0), preferred_element_type=F32)
        intra = [out_late, out_early] if reverse else [out_early, out_late]
        out = out + jnp.concatenate(intra, axis=0)
        yield
    upd = lax.dot_general(v, scale(k, total - cum), (((0,), (0,)), ((), ())), preferred_element_type=F32)
    return out, jnp.exp2(total) * state_t + upd


def _gla_kernel(qf_ref, kf_ref, vf_ref, af_ref, qb_ref, kb_ref, vb_ref, ab_ref,
                kc_ref, vc_ref, acf_ref, acb_ref, tri_ref, mask_ref,
                of_ref, ob_ref, sf_ref, sb_ref):
    c = pl.program_id(2)
    fm, bm = mask_ref.at[0], mask_ref.at[1]
    heads = sf_ref.shape[0]

    @pl.when(c == 0)
    def _():
        zero = jnp.zeros(sf_ref.shape[1:], F32)
        chains = []
        for h in range(heads):
            kc = kc_ref[:, h * GLA_DK:(h + 1) * GLA_DK]
            vc = vc_ref[:, h * GLA_DV:(h + 1) * GLA_DV]
            hl = slice(2 * h * GLA_DK, 2 * (h + 1) * GLA_DK)
            chains.append(_gla_chunk(None, kc, vc, acf_ref[:, hl], zero, tri_ref[0], fm, False, False))
            chains.append(_gla_chunk(None, kc, vc, acb_ref[:, hl], zero, tri_ref[1], bm, True, False))
        for i, (_, state) in enumerate(_interleave(chains)):
            (sb_ref if i % 2 else sf_ref)[i // 2] = state

    @pl.when(c > 0)
    def _():
        chains = []
        for h in range(heads):
            ks = slice(h * GLA_DK, (h + 1) * GLA_DK)
            vs = slice(h * GLA_DV, (h + 1) * GLA_DV)
            hl = slice(2 * h * GLA_DK, 2 * (h + 1) * GLA_DK)
            chains.append(_gla_chunk(qf_ref[:, ks], kf_ref[:, ks], vf_ref[:, vs], af_ref[:, hl], sf_ref[h],
                                     tri_ref[0], fm, False, True))
            chains.append(_gla_chunk(qb_ref[:, ks], kb_ref[:, ks], vb_ref[:, vs], ab_ref[:, hl], sb_ref[h],
                                     tri_ref[1], bm, True, True))
        for i, (o, state) in enumerate(_interleave(chains, GLA_INTERLEAVE)):
            h = i // 2
            vs = slice(h * GLA_DV, (h + 1) * GLA_DV)
            if i % 2:
                ob_ref[:, vs] = o.astype(ob_ref.dtype)
                sb_ref[h] = state
            else:
                of_ref[:, vs] = o.astype(of_ref.dtype)
                sf_ref[h] = state


def _gla(gla_qk, vb, decay, k_ctx, v_ctx, decay_ctx, batch, t_len, ctx_len):
    nc = t_len // GLA_CHUNK
    h = GLA_HEADS
    qk = gla_qk.reshape(batch, t_len, 2 * GLA_QK_WIDTH)
    v3 = vb.reshape(batch, t_len, GLA_V_WIDTH)
    a3 = decay.reshape(batch, t_len, GLA_DECAY_WIDTH)
    kc3 = k_ctx.reshape(batch, ctx_len, GLA_QK_WIDTH)
    vc3 = v_ctx.reshape(batch, ctx_len, GLA_V_WIDTH)
    ac3 = decay_ctx.reshape(batch, ctx_len, GLA_DECAY_WIDTH)
    tri, masks = _gla_constants(GLA_CHUNK)

    def fwd(c):
        return jnp.maximum(c - 1, 0)

    def bwd(c):
        return nc - 1 - jnp.maximum(c - 1, 0)

    def const(a):
        return pl.BlockSpec(a.shape, lambda b, hh, c: (0,) * a.ndim, pipeline_mode=pl.Buffered(1))

    hp = GLA_HEADS_PER_STEP
    groups = h // hp
    cq = (None, GLA_CHUNK, hp * GLA_DK)
    cv = (None, GLA_CHUNK, hp * GLA_DV)
    ca = (None, GLA_CHUNK, hp * 2 * GLA_DK)
    cca = (None, ctx_len, hp * 2 * GLA_DK)
    in_specs = [
        pl.BlockSpec(cq, lambda b, g, c: (b, fwd(c), g)),
        pl.BlockSpec(cq, lambda b, g, c: (b, fwd(c), groups + g)),
        pl.BlockSpec(cv, lambda b, g, c: (b, fwd(c), g)),
        pl.BlockSpec(ca, lambda b, g, c: (b, fwd(c), g)),
        pl.BlockSpec(cq, lambda b, g, c: (b, bwd(c), g)),
        pl.BlockSpec(cq, lambda b, g, c: (b, bwd(c), groups + g)),
        pl.BlockSpec(cv, lambda b, g, c: (b, bwd(c), g)),
        pl.BlockSpec(ca, lambda b, g, c: (b, bwd(c), groups + g)),
        pl.BlockSpec((None, ctx_len, hp * GLA_DK), lambda b, g, c: (b, 0, g)),
        pl.BlockSpec((None, ctx_len, hp * GLA_DV), lambda b, g, c: (b, 0, g)),
        pl.BlockSpec(cca, lambda b, g, c: (b, 0, g)),
        pl.BlockSpec(cca, lambda b, g, c: (b, 0, groups + g)),
        const(tri), const(masks),
    ]
    out_specs = [pl.BlockSpec(cv, lambda b, g, c: (b, fwd(c), g)),
                 pl.BlockSpec(cv, lambda b, g, c: (b, bwd(c), g))]
    out_shape = [jax.ShapeDtypeStruct((batch, t_len, GLA_V_WIDTH), BF16)] * 2
    return pl.pallas_call(
        _gla_kernel,
        out_shape=out_shape,
        grid=(batch, groups, nc + 1),
        in_specs=in_specs,
        out_specs=out_specs,
        scratch_shapes=[pltpu.VMEM((hp, GLA_DV, GLA_DK), F32), pltpu.VMEM((hp, GLA_DV, GLA_DK), F32)],
        compiler_params=_params(3),
        name="gla",
    )(qk, qk, v3, a3, qk, qk, v3, a3, kc3, vc3, ac3, ac3, tri, masks)


def _decay_weights(w_f, b_f, w_b, b_b):
    r, width = w_f.shape
    w2 = jnp.zeros((LANES, 2 * width), F32).at[:r, :width].set(w_f).at[r:2 * r, width:].set(w_b)
    return w2.astype(BF16), jnp.concatenate([b_f, b_b]).reshape(1, 2 * width)


def _layer_norm(v, g, b):
    mu = jnp.mean(v, axis=1, keepdims=True)
    var = jnp.mean(jnp.square(v - mu), axis=1, keepdims=True)
    return (v - mu) * lax.rsqrt(var + EPS) * g + b


def _merge_kernel(oa_ref, of_ref, ob_ref, gb_ref, gates_ref, x_ref, g1_ref, sc2_ref, sh2_ref,
                  wa_ref, wb_ref, wo_ref, nw_ref, lg_ref, lb_ref, rw_ref,
                  x1_ref, h2_ref, sc_ref):
    d = x_ref.shape[1]
    o = of_ref[...].astype(F32) + ob_ref[...].astype(F32)
    pieces = []
    for hh in range(GLA_HEADS):
        oh = o[:, hh * GLA_DV:(hh + 1) * GLA_DV]
        pieces.append(oh * lax.rsqrt(jnp.mean(jnp.square(oh), axis=1, keepdims=True) + EPS))
    out_b = jnp.concatenate(pieces, axis=1) * nw_ref[...] * _silu(gb_ref[...].astype(F32))
    ya = jnp.dot(oa_ref[...], wa_ref[...], preferred_element_type=F32)
    yb = jnp.dot(out_b.astype(BF16), wb_ref[...], preferred_element_type=F32)
    y = _sigmoid(gates_ref[:, :d].astype(F32)) * ya + _sigmoid(gates_ref[:, d:].astype(F32)) * yb
    y2 = jnp.dot(y.astype(BF16), wo_ref[...], preferred_element_type=F32)
    x1 = _layer_norm(DEEPNORM_ALPHA * x_ref[...] + g1_ref[0] * y2, lg_ref[...], lb_ref[...])
    x1_ref[...] = x1
    h2 = x1 * (1.0 + sc2_ref[0]) + sh2_ref[0]
    h2_ref[...] = _pack_pairs(h2)
    logits_t = lax.dot_general(rw_ref[...], h2.astype(BF16), (((1,), (1,)), ((), ())), preferred_element_type=F32)
    sc_ref[...] = _sigmoid(logits_t)


def _merge(out_a, o_f, o_b, gb, gates, x2d, g1, sc2, sh2, wa, wb, wo, nw, lg, lb, rw_t, tile, tiles_per_batch):
    n, d = x2d.shape
    row = lambda i: (i, 0)
    mod = lambda i: (i // tiles_per_batch, 0, 0)
    full = lambda i: (0, 0)

    def const(a):
        return pl.BlockSpec(a.shape, full, pipeline_mode=pl.Buffered(1))

    in_specs = [pl.BlockSpec((tile, NA_WIDTH), row), pl.BlockSpec((tile, GLA_V_WIDTH), row),
                pl.BlockSpec((tile, GLA_V_WIDTH), row), pl.BlockSpec((tile, GLA_V_WIDTH), row),
                pl.BlockSpec((tile, 2 * d), row), pl.BlockSpec((tile, d), row),
                pl.BlockSpec((1, 1, d), mod), pl.BlockSpec((1, 1, d), mod), pl.BlockSpec((1, 1, d), mod),
                const(wa), const(wb), const(wo), const(nw), const(lg), const(lb), const(rw_t)]
    out_shape = [jax.ShapeDtypeStruct((n, d), F32), jax.ShapeDtypeStruct((n, d // 2), U32),
                 jax.ShapeDtypeStruct((N_EXPERTS, n), F32)]
    out_specs = [pl.BlockSpec((tile, d), row), pl.BlockSpec((tile, d // 2), row),
                 pl.BlockSpec((N_EXPERTS, tile), lambda i: (0, i))]
    return pl.pallas_call(
        _merge_kernel, out_shape=out_shape, grid=(n // tile,), in_specs=in_specs, out_specs=out_specs,
        compiler_params=_params(1), name="merge_ln1_router",
    )(out_a, o_f, o_b, gb, gates, x2d, g1, sc2, sh2, wa, wb, wo, nw, lg, lb, rw_t)


def _first_argmax(vals, idx, n):
    m = jnp.max(vals, axis=0, keepdims=True)
    first = jnp.min(jnp.where(vals == m, idx, float(n)), axis=0, keepdims=True)
    return m, first


def _route_kernel(sc_ref, bias_ref, before_ref, e_ref, w_ref, rank_ref, cnt_ref, carry_ref):
    step = pl.program_id(0)
    tr = sc_ref.shape[1]

    @pl.when(step == 0)
    def _():
        carry_ref[...] = jnp.zeros(carry_ref.shape, F32)

    scores = sc_ref[...]
    biased = scores + bias_ref[...]
    eidx = lax.broadcasted_iota(jnp.int32, (N_EXPERTS, tr), 0).astype(F32)
    lidx = lax.broadcasted_iota(jnp.int32, (GROUP_SIZE, tr), 0).astype(F32)
    gidx = lax.broadcasted_iota(jnp.int32, (N_GROUPS, tr), 0).astype(F32)
    gs = []
    for g in range(N_GROUPS):
        blk = biased[g * GROUP_SIZE:(g + 1) * GROUP_SIZE]
        m1, first = _first_argmax(blk, lidx, GROUP_SIZE)
        m2 = jnp.max(jnp.where(lidx == first, -jnp.inf, blk), axis=0, keepdims=True)
        gs.append(m1 + m2)
    cur = jnp.concatenate(gs, axis=0)
    keep = jnp.zeros((N_GROUPS, tr), F32)
    for _ in range(TOPK_GROUPS):
        _, first = _first_argmax(cur, gidx, N_GROUPS)
        sel = gidx == first
        keep = jnp.where(sel, 1.0, keep)
        cur = jnp.where(sel, -jnp.inf, cur)
    keep_e = jnp.concatenate([jnp.broadcast_to(keep[g:g + 1], (GROUP_SIZE, tr)) for g in range(N_GROUPS)], axis=0)
    masked = jnp.where(keep_e > 0.5, biased, -jnp.inf)
    chosen = jnp.zeros((N_EXPERTS, tr), F32)
    tops, topi = [], []
    for _ in range(TOP_K):
        _, first = _first_argmax(masked, eidx, N_EXPERTS)
        sel = eidx == first
        tops.append(jnp.sum(jnp.where(sel, scores, 0.0), axis=0, keepdims=True))
        topi.append(first)
        chosen = jnp.where(sel, 1.0, chosen)
        masked = jnp.where(sel, -jnp.inf, masked)
    top_s = jnp.concatenate(tops, axis=0)
    top_i = jnp.concatenate(topi, axis=0)
    e_ref[...] = top_i.astype(jnp.int32)
    w_ref[...] = (top_s / jnp.sum(top_s, axis=0, keepdims=True) * ROUTED_SCALE).T
    prior = jnp.dot(chosen.astype(BF16), before_ref[...], preferred_element_type=F32) + carry_ref[...]
    ranks = [jnp.sum(jnp.where(eidx == topi[kk], prior, 0.0), axis=0, keepdims=True) for kk in range(TOP_K)]
    rank_ref[...] = jnp.concatenate(ranks, axis=0).astype(jnp.int32)
    carry_ref[...] = carry_ref[...] + jnp.sum(chosen, axis=1, keepdims=True)
    cnt_ref[...] = jnp.broadcast_to(carry_ref[...], cnt_ref.shape).astype(jnp.int32)


def _route(scores_t, router_bias, tile):
    n = scores_t.shape[1]
    col = lambda i: (0, i)
    out_shape = [jax.ShapeDtypeStruct((TOP_K, n), jnp.int32), jax.ShapeDtypeStruct((n, TOP_K), F32),
                 jax.ShapeDtypeStruct((TOP_K, n), jnp.int32), jax.ShapeDtypeStruct((N_EXPERTS, LANES), jnp.int32)]
    before = jnp.asarray(np.arange(tile)[:, None] < np.arange(tile)[None, :], BF16)
    return pl.pallas_call(
        _route_kernel, out_shape=out_shape, grid=(n // tile,),
        in_specs=[pl.BlockSpec((N_EXPERTS, tile), col), pl.BlockSpec((N_EXPERTS, 1), lambda i: (0, 0)),
                  pl.BlockSpec((tile, tile), lambda i: (0, 0), pipeline_mode=pl.Buffered(1))],
        out_specs=[pl.BlockSpec((TOP_K, tile), col), pl.BlockSpec((tile, TOP_K), lambda i: (i, 0)),
                   pl.BlockSpec((TOP_K, tile), col), pl.BlockSpec((N_EXPERTS, LANES), lambda i: (0, 0))],
        scratch_shapes=[pltpu.VMEM((N_EXPERTS, 1), F32)],
        compiler_params=_params(1), name="route",
    )(scores_t, router_bias.reshape(N_EXPERTS, 1), before)


def _slots_kernel(e_ref, rank_ref, start_ref, dest_ref):
    tr = e_ref.shape[1]
    eidx = lax.broadcasted_iota(jnp.int32, (N_EXPERTS, tr), 0)
    e = e_ref[...]
    start = start_ref[...]
    rows = [jnp.sum(jnp.where(eidx == e[kk:kk + 1], start, 0.0), axis=0, keepdims=True) for kk in range(TOP_K)]
    dest_ref[...] = jnp.concatenate(rows, axis=0).astype(jnp.int32) + rank_ref[...]


def _slots(top_e, rank, start_rows, tile):
    n = top_e.shape[1]
    col = lambda i: (0, i)
    return pl.pallas_call(
        _slots_kernel, out_shape=jax.ShapeDtypeStruct((TOP_K, n), jnp.int32), grid=(n // tile,),
        in_specs=[pl.BlockSpec((TOP_K, tile), col), pl.BlockSpec((TOP_K, tile), col),
                  pl.BlockSpec((N_EXPERTS, 1), lambda i: (0, 0))],
        out_specs=pl.BlockSpec((TOP_K, tile), col),
        compiler_params=_params(1), name="slots",
    )(top_e, rank, start_rows)


def _sc_mesh():
    return plsc.VectorSubcoreMesh(core_axis_name="c", subcore_axis_name="s",
                                  num_cores=SC_CORES, num_subcores=SC_SUBCORES)


def _sc_scatter_rows(rows, idx, n_out):
    n, dp = rows.shape
    copies = idx.shape[0] // n
    per_worker = n // (SC_CORES * SC_SUBCORES)
    chunk = SC_GATHER_ROWS

    @functools.partial(
        pl.kernel, mesh=_sc_mesh(), out_type=jax.ShapeDtypeStruct((n_out, dp), rows.dtype),
        scratch_types=[pltpu.VMEM((chunk,), jnp.int32), pltpu.VMEM((chunk, dp), rows.dtype)],
        name="sc_scatter_rows")
    def scatter(rows_hbm, idx_hbm, out_hbm, idx_v, rows_v):
        base = (lax.axis_index("s") * SC_CORES + lax.axis_index("c")) * per_worker

        @pl.loop(0, per_worker // chunk)
        def _(j):
            off = base + j * chunk
            pltpu.sync_copy(rows_hbm.at[pl.ds(off, chunk)], rows_v)
            for k in range(copies):
                pltpu.sync_copy(idx_hbm.at[pl.ds(k * n + off, chunk)], idx_v)
                pltpu.sync_copy(rows_v, out_hbm.at[idx_v])

    return scatter(rows, idx)


def _expert_kernel(start_ref, cnt_ref, nused_ref, xs_ref, wg_ref, wu_ref, wd_ref, y_ref,
                   wgb, wub, wdb, xbuf, ybuf, sem_in, sem_out):
    e = pl.program_id(0)
    bm, ring = EXPERT_BLOCK, EXPERT_RING
    cnt = cnt_ref[e]
    nb = (cnt + bm - 1) // bm
    g0 = start_ref[e]
    n_used = nused_ref[0]

    def x_copy(g):
        slot = g % ring
        return pltpu.make_async_copy(xs_ref.at[pl.ds(g * bm, bm)], xbuf.at[slot], sem_in.at[slot])

    def y_copy(g):
        slot = g % ring
        return pltpu.make_async_copy(ybuf.at[slot], y_ref.at[pl.ds(g * bm, bm)], sem_out.at[slot])

    @pl.when(e == 0)
    def _():
        for g in range(ring):
            @pl.when(g < n_used)
            def _():
                x_copy(g).start()

    wgb[...] = wg_ref[...].astype(BF16)
    wub[...] = wu_ref[...].astype(BF16)
    wdb[...] = wd_ref[...].astype(BF16)
    row = lax.broadcasted_iota(jnp.int32, (bm, xbuf.shape[2]), 0)

    def blocks(j, count):
        for b in range(count):
            g = g0 + j + b
            x_copy(g).wait()

            @pl.when(g >= ring)
            def _():
                y_copy(g - ring).wait()

        def swiglu(b):
            slot = (g0 + j + b) % ring
            x = _unpack_pairs(jnp.where(row < cnt - (j + b) * bm, xbuf[slot], jnp.uint32(0))).astype(BF16)
            yield
            gate = jnp.dot(x, wgb[...], preferred_element_type=F32)
            up = jnp.dot(x, wub[...], preferred_element_type=F32)
            yield
            act = (_silu(gate) * up).astype(BF16)
            yield
            y = jnp.dot(act, wdb[...], preferred_element_type=F32)
            yield
            ybuf[slot] = _pack_pairs(y)

        _interleave([swiglu(b) for b in range(count)])

        for b in range(count):
            g = g0 + j + b
            y_copy(g).start()

            @pl.when(g + ring < n_used)
            def _():
                x_copy(g + ring).start()

    group = EXPERT_GROUP

    def full_group(p, carry):
        blocks(group * p, group)
        return carry

    lax.fori_loop(0, nb // group, full_group, 0)
    done = (nb // group) * group
    size = group // 2
    while size >= 1:
        @pl.when((nb - done) % (2 * size) >= size)
        def _(done=done, size=size):
            blocks(done, size)

        done = done + jnp.where((nb - done) % (2 * size) >= size, size, 0)
        size //= 2

    @pl.when(e == pl.num_programs(0) - 1)
    def _():
        for back in range(1, ring + 1):
            @pl.when(n_used - back >= 0)
            def _():
                y_copy(n_used - back).wait()


def _experts(blk_start, counts, n_used, xs, wg, wu, wd):
    n_slots, dp = xs.shape
    n_exp, d, ff = wg.shape
    bm, ring = EXPERT_BLOCK, EXPERT_RING
    wsel = lambda e, st, ct, nu: (e, 0, 0)
    grid_spec = pltpu.PrefetchScalarGridSpec(
        num_scalar_prefetch=3, grid=(n_exp,),
        in_specs=[pl.BlockSpec(memory_space=pl.ANY),
                  pl.BlockSpec((None, d, ff), wsel), pl.BlockSpec((None, d, ff), wsel),
                  pl.BlockSpec((None, ff, d), wsel)],
        out_specs=pl.BlockSpec(memory_space=pl.ANY),
        scratch_shapes=[pltpu.VMEM((d, ff), BF16), pltpu.VMEM((d, ff), BF16), pltpu.VMEM((ff, d), BF16),
                        pltpu.VMEM((ring, bm, dp), U32), pltpu.VMEM((ring, bm, dp), U32),
                        pltpu.SemaphoreType.DMA((ring,)), pltpu.SemaphoreType.DMA((ring,))])
    return pl.pallas_call(
        _expert_kernel, out_shape=jax.ShapeDtypeStruct((n_slots, dp), U32), grid_spec=grid_spec,
        compiler_params=_params(1), name="experts",
    )(blk_start, counts, n_used, xs, wg, wu, wd)


def _sc_gather_rows(table, idx):
    n_idx = idx.shape[0]
    dp = table.shape[1]
    workers = SC_CORES * SC_SUBCORES
    per_worker = n_idx // workers
    chunk = SC_GATHER_ROWS

    @functools.partial(
        pl.kernel, mesh=_sc_mesh(), out_type=jax.ShapeDtypeStruct((n_idx, dp), table.dtype),
        scratch_types=[pltpu.VMEM((chunk,), jnp.int32), pltpu.VMEM((chunk, dp), table.dtype),
                       pltpu.SemaphoreType.DMA],
        name="sc_gather_rows")
    def gather(table_hbm, idx_hbm, out_hbm, idx_v, rows_v, sem):
        base = (lax.axis_index("s") * SC_CORES + lax.axis_index("c")) * per_worker

        @pl.loop(0, per_worker // chunk)
        def _(j):
            off = base + j * chunk
            pltpu.sync_copy(idx_hbm.at[pl.ds(off, chunk)], idx_v)
            pltpu.async_copy(table_hbm.at[idx_v], rows_v, sem).wait()
            pltpu.sync_copy(rows_v, out_hbm.at[pl.ds(off, chunk)])

    return gather(table, idx)


def _combine_kernel(tile, yt_ref, w_ref, h_ref, x1_ref, g2_ref, sg_ref, su_ref, sd_ref, lg_ref, lb_ref, *rest):
    o_ref = rest[-1]
    hb = _unpack_pairs(h_ref[...]).astype(BF16)
    g = jnp.dot(hb, sg_ref[...], preferred_element_type=F32)
    u = jnp.dot(hb, su_ref[...], preferred_element_type=F32)
    f = jnp.dot((_silu(g) * u).astype(BF16), sd_ref[...], preferred_element_type=F32)
    w = w_ref[...]
    for kk in range(TOP_K):
        f = f + w[:, kk:kk + 1] * _unpack_pairs(yt_ref[kk * tile:(kk + 1) * tile, :])
    o_ref[...] = _layer_norm(DEEPNORM_ALPHA * x1_ref[...] + g2_ref[0] * f, lg_ref[...], lb_ref[...])


def _combine(y_tok, top_w, h2p, x1, g2, sg, su, sd, lg, lb, tile, tiles_per_batch, first_tile, prev_out):
    n, d = x1.shape
    dp = h2p.shape[1]
    row = lambda i: (first_tile + i, 0)
    full = lambda i: (0, 0)

    def const(a):
        return pl.BlockSpec(a.shape, full, pipeline_mode=pl.Buffered(1))

    in_specs = [pl.BlockSpec((TOP_K * tile, dp), lambda i: (i, 0)),
                pl.BlockSpec((tile, TOP_K), row), pl.BlockSpec((tile, dp), row), pl.BlockSpec((tile, d), row),
                pl.BlockSpec((1, 1, d), lambda i: ((first_tile + i) // tiles_per_batch, 0, 0)),
                const(sg), const(su), const(sd), const(lg), const(lb)]
    args = [y_tok, top_w, h2p, x1, g2, sg, su, sd, lg, lb]
    aliases = {}
    if prev_out is not None:
        in_specs.append(pl.BlockSpec(memory_space=pl.ANY))
        args.append(prev_out)
        aliases = {len(args) - 1: 0}
    return pl.pallas_call(
        functools.partial(_combine_kernel, tile),
        out_shape=jax.ShapeDtypeStruct((n, d), F32),
        grid=(y_tok.shape[0] // (TOP_K * tile),),
        in_specs=in_specs,
        out_specs=pl.BlockSpec((tile, d), row),
        input_output_aliases=aliases,
        compiler_params=_params(1), name="combine_shared_ln2",
    )(*args)


def kernel(x, c, ctx, c_ctx, w_mod, b_mod, w_in, na_rpb, gla_w_decay_f, gla_b_decay_f, gla_w_decay_b, gla_b_decay_b,
           gla_norm_w, w_branch_a, w_branch_b, w_out, ln1_g, ln1_b, router_w, router_bias, exp_w_gate, exp_w_up,
           exp_w_down, sh_w_gate, sh_w_up, sh_w_down, ln2_g, ln2_b):
    batch, t_len, d = x.shape
    ctx_len = ctx.shape[1]
    n = batch * t_len
    assert w_mod.shape[0] == DEPTH == 1
    assert t_len % GLA_CHUNK == 0 and ctx_len == GLA_CHUNK and (t_len // GRID_W) % (NA_ROW_BLOCK * NA_BLOCKS_PER_STEP) == 0

    mod_rows = 16
    c_all = jnp.zeros((mod_rows, d), F32).at[:batch].set(c).at[batch].set(c_ctx)
    mod = _modulation(c_all, w_mod[0], b_mod[0])
    sh1, sc1, g1, sh2, sc2, g2 = [mod[:batch, j * d:(j + 1) * d].reshape(batch, 1, d) for j in range(6)]
    sh1c = mod[batch:batch + 1, 0:d].reshape(1, 1, d)
    sc1c = mod[batch:batch + 1, d:2 * d].reshape(1, 1, d)

    offs = np.cumsum((0, NA_WIDTH, NA_WIDTH, NA_WIDTH, GLA_QK_WIDTH, GLA_QK_WIDTH, GLA_V_WIDTH, GLA_V_WIDTH,
                      GLA_GATE_RANK, GLA_GATE_RANK, d, d))
    qa, ka, va, qb, kb, vbc, gbc, lrf, lrb, ga, gbt = [w_in[0][:, offs[j]:offs[j + 1]] for j in range(11)]
    lr_cols = jnp.concatenate([lrf, lrb, jnp.zeros((d, LANES - 2 * GLA_GATE_RANK), F32)], axis=1)
    w_lat = jnp.concatenate([lr_cols, qa, ka, va, qb, kb, vbc, gbc, ga, gbt], axis=1).astype(BF16)
    lat_offs = np.cumsum((0, LANES, NA_WIDTH, NA_WIDTH, NA_WIDTH, GLA_QK_WIDTH, GLA_QK_WIDTH, GLA_V_WIDTH))
    w_ctx = jnp.concatenate([w_lat[:, lat_offs[j]:lat_offs[j + 1]] for j in (0, 2, 3, 5, 6)], axis=1)
    plain = ("plain",)
    lat_plan = ((LANES, (("decay",),)),
                (3 * NA_WIDTH, (("scale", NA_HEAD_DIM ** -0.5), plain, plain)),
                (2 * GLA_QK_WIDTH, (("rope", GLA_DK ** -0.5), ("rope", 1.0))),
                (GLA_V_WIDTH, (plain, plain)), (GLA_V_WIDTH, (plain, plain)),
                (2 * d, (plain,) * 4))
    ctx_plan = ((LANES, (("decay",),)),
                (2 * NA_WIDTH, (plain, plain)), (GLA_QK_WIDTH, (plain,)), (GLA_V_WIDTH, (plain, plain)))
    w2, b2 = _decay_weights(gla_w_decay_f[0], gla_b_decay_f[0], gla_w_decay_b[0], gla_b_decay_b[0])
    tile = COMBINE_TILE
    x2d = x.reshape(n, d)
    decay, na_qkv, gla_qk, vb, gb, gates = _projection(
        x2d, sc1, sh1, w_lat, w2, b2, lat_plan, (BF16,) * 6, PROJ_TILE, t_len // PROJ_TILE,
        rope=_rope_tables(t_len))
    decay_c, na_kv_c, k_c, v_c = _projection(
        ctx.reshape(batch * ctx_len, d), sc1c, sh1c, w_ctx, w2, b2, ctx_plan, (BF16,) * 4, tile,
        batch * ctx_len // tile)

    out_a = _na_attention(na_qkv, na_kv_c, _na_bias_tables(na_rpb[0]), batch, t_len, ctx_len)
    o_f, o_b = _gla(gla_qk, vb, decay, k_c, v_c, decay_c, batch, t_len, ctx_len)

    x1, h2p, scores_t = _merge(
        out_a.reshape(n, NA_WIDTH), o_f.reshape(n, GLA_V_WIDTH), o_b.reshape(n, GLA_V_WIDTH), gb, gates, x2d,
        g1, sc2, sh2, w_branch_a[0].astype(BF16), w_branch_b[0].astype(BF16), w_out[0].astype(BF16),
        gla_norm_w[0].reshape(1, -1), ln1_g[0].reshape(1, d), ln1_b[0].reshape(1, d),
        router_w[0].T.astype(BF16), PROJ_TILE, t_len // PROJ_TILE)

    top_e, top_w, rank, counts = _route(scores_t, router_bias[0], 512)

    counts = counts[:, 0]
    n_blocks = pl.cdiv(n * TOP_K, EXPERT_BLOCK) + N_EXPERTS
    blocks_per = (counts + EXPERT_BLOCK - 1) // EXPERT_BLOCK
    blk_end = jnp.cumsum(blocks_per)
    blk_start = blk_end - blocks_per
    dest = _slots(top_e, rank, (blk_start * EXPERT_BLOCK).astype(F32).reshape(N_EXPERTS, 1), 2048)

    xs = _sc_scatter_rows(h2p, dest.reshape(TOP_K * n), n_blocks * EXPERT_BLOCK)
    y = _experts(blk_start.astype(jnp.int32), counts, blk_end[-1:].astype(jnp.int32), xs,
                 exp_w_gate[0], exp_w_up[0], exp_w_down[0])
    dest_tok = dest.reshape(TOP_K, n // tile, tile).transpose(1, 0, 2).reshape(COMBINE_CHUNKS, -1)
    w_rows = top_w
    shared = (sh_w_gate[0].astype(BF16), sh_w_up[0].astype(BF16), sh_w_down[0].astype(BF16),
              ln2_g[0].reshape(1, d), ln2_b[0].reshape(1, d))
    tiles_per_chunk = n // tile // COMBINE_CHUNKS
    out = None
    for ci in range(COMBINE_CHUNKS):
        y_tok = _sc_gather_rows(y, dest_tok[ci])
        out = _combine(y_tok, w_rows, h2p, x1, g2, *shared, tile, t_len // tile, ci * tiles_per_chunk, out)
    return out.reshape(batch, t_len, d)
```

```python
import functools

import numpy as np
import jax
import jax.numpy as jnp
from jax import lax
from jax.experimental import pallas as pl
from jax.experimental.pallas import tpu as pltpu
from jax.experimental.pallas import tpu_sc as plsc

F32 = jnp.float32
BF16 = jnp.bfloat16
U32 = jnp.uint32
HIGHEST = lax.Precision.HIGHEST

GRID_W = 64
NA_HEADS = 8
NA_HEAD_DIM = 64
NA_WIDTH = NA_HEADS * NA_HEAD_DIM
WIN_ROWS = 8
WIN_COLS = 16
GLA_HEADS = 4
GLA_DK = 128
GLA_DV = 256
GLA_QK_WIDTH = GLA_HEADS * GLA_DK
GLA_V_WIDTH = GLA_HEADS * GLA_DV
GLA_GATE_RANK = 16
GLA_TAU = 16.0
LOG2E = 1.4426950408889634
ROPE_BASE = 10000.0
N_EXPERTS = 256
TOP_K = 8
N_GROUPS = 8
TOPK_GROUPS = 4
GROUP_SIZE = N_EXPERTS // N_GROUPS
ROUTED_SCALE = 2.5
DEPTH = 1
DEEPNORM_ALPHA = (2 * DEPTH) ** 0.25
EPS = 1e-6

LANES = 128
PROJ_TILE = 512
PROJ_INTERLEAVE = 2
NA_ROW_BLOCK = 4
NA_BLOCKS_PER_STEP = 2
NA_INTERLEAVE = 4
NA_UNION_ROWS = NA_ROW_BLOCK + WIN_ROWS - 1
GLA_CHUNK = 256
GLA_DIAG = 16
GLA_HEADS_PER_STEP = 4
GLA_INTERLEAVE = 8
EXPERT_BLOCK = 272
EXPERT_GROUP = 4
EXPERT_RING = 8
COMBINE_CHUNKS = 8
COMBINE_TILE = 512
SC_CORES = 2
SC_SUBCORES = 16
SC_GATHER_ROWS = 128
NEG_BIG = -1e30
VMEM_LIMIT = 56 * 1024 * 1024
NA_VMEM_LIMIT = 58 * 1024 * 1024


def _params(n_axes, vmem=VMEM_LIMIT):
    return pltpu.CompilerParams(dimension_semantics=("arbitrary",) * n_axes, vmem_limit_bytes=vmem)


def _sigmoid(v):
    return 1.0 / (1.0 + jnp.exp2(v * (-LOG2E)))


def _silu(v):
    return v * _sigmoid(v)


def _interleave(chains, group=None):
    if group is not None and group < len(chains):
        return [r for i in range(0, len(chains), group) for r in _interleave(chains[i:i + group])]
    results = [None] * len(chains)
    active = list(enumerate(chains))
    while active:
        still = []
        for i, chain in active:
            try:
                next(chain)
                still.append((i, chain))
            except StopIteration as stop:
                results[i] = stop.value
        active = still
    return results


def _pack_pairs(v):
    m = v.shape[1] // 2
    lo = lax.bitcast_convert_type(v[:, :m].astype(BF16).astype(F32), U32) >> 16
    hi = lax.bitcast_convert_type(v[:, m:].astype(BF16).astype(F32), U32) & jnp.uint32(0xFFFF0000)
    return lo | hi


def _unpack_pairs(p):
    lo = lax.bitcast_convert_type(p << 16, F32)
    hi = lax.bitcast_convert_type(p & jnp.uint32(0xFFFF0000), F32)
    return jnp.concatenate([lo, hi], axis=1)


def _mod_kernel(c_ref, w_ref, b_ref, o_ref):
    o_ref[...] = jnp.dot(_silu(c_ref[...]), w_ref[...], preferred_element_type=F32, precision=HIGHEST) + b_ref[...]


def _modulation(c_all, w_mod, b_mod):
    rows, d = c_all.shape
    n = w_mod.shape[1]
    bn = 512
    return pl.pallas_call(
        _mod_kernel,
        out_shape=jax.ShapeDtypeStruct((rows, n), F32),
        grid=(n // bn,),
        in_specs=[pl.BlockSpec((rows, d), lambda j: (0, 0)),
                  pl.BlockSpec((d, bn), lambda j: (0, j)),
                  pl.BlockSpec((1, bn), lambda j: (0, j))],
        out_specs=pl.BlockSpec((rows, bn), lambda j: (0, j)),
        compiler_params=_params(1),
        name="modulation",
    )(c_all, w_mod, b_mod.reshape(1, n))


def _swap32(v):
    lane = lax.broadcasted_iota(jnp.int32, v.shape, 1)
    return jnp.where((lane % 64) < 32, pltpu.roll(v, 96, 1), pltpu.roll(v, 32, 1))


GLA_DECAY_WIDTH = 2 * GLA_HEADS * 2 * GLA_DK


def _log2_decay_split(lr, w2, b2, out_ref):
    z = (jnp.dot(lr.astype(BF16), w2, preferred_element_type=F32) + b2) * LOG2E
    a = (jnp.minimum(z, 0.0) - jnp.log2(1.0 + jnp.exp2(-jnp.abs(z)))) * (1.0 / GLA_TAU)
    hi = a.astype(BF16)
    lo = (a - hi.astype(F32)).astype(BF16)
    for p in range(a.shape[1] // GLA_DK):
        src = slice(p * GLA_DK, (p + 1) * GLA_DK)
        out_ref[:, 2 * p * GLA_DK:(2 * p + 1) * GLA_DK] = hi[:, src]
        out_ref[:, (2 * p + 1) * GLA_DK:(2 * p + 2) * GLA_DK] = lo[:, src]


def _proj_kernel(plan, has_rope, *refs):
    x_ref, sc_ref, sh_ref, w_ref, w2_ref, b2_ref = refs[:6]
    pos = 6
    if has_rope:
        cos_ref, sin_ref = refs[6:8]
        pos = 8
    out_refs = refs[pos:]
    h = (x_ref[...] * (1.0 + sc_ref[0]) + sh_ref[0]).astype(BF16)

    def piece(out_ref, kind, w0, c0, cw):
        acc = jnp.dot(h, w_ref[:, w0:w0 + cw], preferred_element_type=F32)
        yield
        if kind[0] == "decay":
            _log2_decay_split(acc, w2_ref[...], b2_ref[...], out_ref)
            return
        if kind[0] == "scale":
            acc = acc * kind[1]
        elif kind[0] == "rope":
            cos, sin = cos_ref[...], sin_ref[...]
            parts = []
            for p in range(cw // LANES):
                v = acc[:, p * LANES:(p + 1) * LANES]
                parts.append((v * cos + _swap32(v) * sin) * kind[1])
            acc = jnp.concatenate(parts, axis=1)
        out_ref[:, c0:c0 + cw] = acc.astype(out_ref.dtype)

    chains = []
    col = 0
    for out_ref, (width, kinds) in zip(out_refs, plan):
        cw = width // len(kinds)
        for j, kind in enumerate(kinds):
            chains.append(piece(out_ref, kind, col + j * cw, j * cw, cw))
        col += width
    for i in range(0, len(chains), PROJ_INTERLEAVE):
        _interleave(chains[i:i + PROJ_INTERLEAVE])


def _projection(x2d, sc, sh, w, w2, b2, plan, out_dtypes, tile, tiles_per_mod, rope=None):
    n, d = x2d.shape
    const = lambda a: pl.BlockSpec(a.shape, lambda i: (0, 0), pipeline_mode=pl.Buffered(1))
    in_specs = [pl.BlockSpec((tile, d), lambda i: (i, 0)),
                pl.BlockSpec((1, 1, d), lambda i: (i // tiles_per_mod, 0, 0)),
                pl.BlockSpec((1, 1, d), lambda i: (i // tiles_per_mod, 0, 0)),
                const(w), const(w2), const(b2)]
    args = [x2d, sc, sh, w, w2, b2]
    if rope is not None:
        in_specs += [pl.BlockSpec((tile, LANES), lambda i: (i % tiles_per_mod, 0))] * 2
        args += list(rope)
    widths = [GLA_DECAY_WIDTH if kinds[0][0] == "decay" else wd for wd, kinds in plan]
    out_shape = [jax.ShapeDtypeStruct((n, wd), dt) for wd, dt in zip(widths, out_dtypes)]
    out_specs = [pl.BlockSpec((tile, wd), lambda i: (i, 0)) for wd in widths]
    return pl.pallas_call(
        functools.partial(_proj_kernel, plan, rope is not None),
        out_shape=out_shape,
        grid=(n // tile,),
        in_specs=in_specs,
        out_specs=out_specs,
        compiler_params=_params(1),
        name="in_proj" if rope is not None else "ctx_proj",
    )(*args)


def _rope_tables(t_len):
    half = GLA_DK // 2
    quarter = half // 2
    f32 = np.float32
    inv_freq = f32(ROPE_BASE) ** (-np.arange(quarter, dtype=f32) / f32(quarter))
    pos = np.arange(t_len)
    row_ang = (pos // GRID_W).astype(f32)[:, None] * inv_freq[None, :]
    col_ang = (pos % GRID_W).astype(f32)[:, None] * inv_freq[None, :]
    cr, sr, cc, sn = np.cos(row_ang), np.sin(row_ang), np.cos(col_ang), np.sin(col_ang)
    cos = np.concatenate([cr, cr, cc, cc], axis=1).astype(f32)
    sin = np.concatenate([-sr, sr, -sn, sn], axis=1).astype(f32)
    return jnp.asarray(cos), jnp.asarray(sin)


def _na_bias_tables(rpb):
    rb, ur, w = NA_ROW_BLOCK, NA_UNION_ROWS, GRID_W
    heads = rpb.shape[0]
    pad = jnp.pad(rpb, ((0, 0), (0, 0), (w, w)))
    toep = jnp.stack([pad[:, :, w + WIN_COLS - 1 - c:2 * w + WIN_COLS - 1 - c] for c in range(w)], axis=2)
    c = np.arange(w)[:, None]
    kc = np.arange(w)[None, :]
    col_start = np.clip(c - WIN_COLS // 2, 0, w - WIN_COLS)
    col_ok = (kc >= col_start) & (kc < col_start + WIN_COLS)
    toep = jnp.where(col_ok[None, None], toep, NEG_BIG)
    neg = jnp.full((heads, w, w), NEG_BIG, F32)
    half = WIN_ROWS // 2
    tables = []
    for lo, off in ((lambda i: 0, WIN_ROWS - 1), (lambda i: i, WIN_ROWS - 1 - half), (lambda i: ur - WIN_ROWS, 0)):
        rows_ = []
        for i in range(rb):
            blocks = [toep[:, j - i + off] if lo(i) <= j < lo(i) + WIN_ROWS else neg for j in range(ur)]
            rows_.append(jnp.concatenate(blocks, axis=2))
        tables.append(jnp.concatenate(rows_, axis=1))
    return jnp.stack(tables)


def _na_kernel(rows, q_ref, k_ref, v_ref, kc_ref, vc_ref, *rest):
    bias_refs, o_ref = rest[:-1], rest[-1]
    tq = NA_ROW_BLOCK * GRID_W
    nk = NA_UNION_ROWS * GRID_W
    nt = (((1,), (1,)), ((), ()))
    lane = lax.broadcasted_iota(jnp.int32, (tq, LANES), 1)
    def head(qm, ku, vu, kc, vc, bias):
        s_loc = lax.dot_general(qm, ku, nt, preferred_element_type=F32) + bias
        s_ctx = lax.dot_general(qm, kc, nt, preferred_element_type=F32)
        yield
        m = jnp.maximum(jnp.max(s_loc, axis=1, keepdims=True), jnp.max(s_ctx, axis=1, keepdims=True))
        p_loc = jnp.exp(s_loc - m)
        p_ctx = jnp.exp(s_ctx - m)
        yield
        denom = jnp.sum(p_loc, axis=1, keepdims=True) + jnp.sum(p_ctx, axis=1, keepdims=True)
        o = (jnp.dot(p_loc.astype(BF16), vu, preferred_element_type=F32)
             + jnp.dot(p_ctx.astype(BF16), vc, preferred_element_type=F32))
        yield
        return o / denom

    chains = []
    for sub, bias_ref in enumerate(bias_refs):
        rb = pl.program_id(1) * len(bias_refs) + sub
        ustart = jnp.clip(rb * NA_ROW_BLOCK - WIN_ROWS // 2, 0, rows - NA_UNION_ROWS)
        k0 = pl.multiple_of(ustart * GRID_W, GRID_W)
        qrows = slice(sub * tq, (sub + 1) * tq)
        for p in range(NA_WIDTH // LANES):
            ls = slice(p * LANES, (p + 1) * LANES)
            q = q_ref[qrows, ls]
            ku = k_ref[pl.ds(k0, nk), ls]
            vu = v_ref[pl.ds(k0, nk), ls]
            for hh in range(LANES // NA_HEAD_DIM):
                qm = jnp.where((lane // NA_HEAD_DIM) == hh, q, jnp.zeros_like(q))
                chains.append(head(qm, ku, vu, kc_ref[:, ls], vc_ref[:, ls], bias_ref[p * 2 + hh]))
    outs = _interleave(chains, NA_INTERLEAVE)
    pairs = NA_WIDTH // LANES
    for sub in range(len(bias_refs)):
        for p in range(pairs):
            i = 2 * (sub * pairs + p)
            out = jnp.where((lane // NA_HEAD_DIM) == 0, outs[i], outs[i + 1])
            o_ref[sub * tq:(sub + 1) * tq, p * LANES:(p + 1) * LANES] = out.astype(o_ref.dtype)


def _na_attention(na_qkv, na_kv_ctx, bias_tables, batch, t_len, ctx_len):
    rows = t_len // GRID_W
    n_rb = rows // NA_ROW_BLOCK
    per = NA_BLOCKS_PER_STEP
    tq = per * NA_ROW_BLOCK * GRID_W
    qkv = na_qkv.reshape(batch, t_len, 3 * NA_WIDTH)
    kvc = na_kv_ctx.reshape(batch, ctx_len, 2 * NA_WIDTH)

    def bias_spec(sub):
        def idx(b, r):
            rb = r * per + sub
            return (jnp.where(rb == 0, 0, jnp.where(rb == n_rb - 1, 2, 1)), 0, 0, 0)
        return pl.BlockSpec((None,) + bias_tables.shape[1:], idx)

    return pl.pallas_call(
        functools.partial(_na_kernel, rows),
        out_shape=jax.ShapeDtypeStruct((batch, t_len, NA_WIDTH), BF16),
        grid=(batch, n_rb // per),
        in_specs=[pl.BlockSpec((None, tq, NA_WIDTH), lambda b, r: (b, r, 0)),
                  pl.BlockSpec((None, t_len, NA_WIDTH), lambda b, r: (b, 0, 1)),
                  pl.BlockSpec((None, t_len, NA_WIDTH), lambda b, r: (b, 0, 2)),
                  pl.BlockSpec((None, ctx_len, NA_WIDTH), lambda b, r: (b, 0, 0)),
                  pl.BlockSpec((None, ctx_len, NA_WIDTH), lambda b, r: (b, 0, 1))]
                 + [bias_spec(sub) for sub in range(per)],
        out_specs=pl.BlockSpec((None, tq, NA_WIDTH), lambda b, r: (b, r, 0)),
        compiler_params=_params(2, vmem=NA_VMEM_LIMIT),
        name="na_attention",
    )(qkv, qkv, qkv, kvc, kvc, *([bias_tables] * per))


def _gla_constants(c):
    tris, masks = [], []
    for reverse in (False, True):
        i = np.arange(c)[:, None]
        j = np.arange(c)[None, :]
        tris.append((j >= i) if reverse else (j <= i))
        i = np.arange(c // 2)[:, None]
        j = np.arange(c // 2)[None, :]
        if reverse:
            i, j = j, i
        level = []
        s = c // 4
        while s >= GLA_DIAG:
            level.append(((i // (2 * s)) == (j // (2 * s))) & ((i % (2 * s)) >= s) & ((j % (2 * s)) < s))
            s //= 2
        level.append(((i // GLA_DIAG) == (j // GLA_DIAG)) & (j <= i))
        masks.append(np.stack(level))
    return jnp.asarray(np.stack(tris), BF16), jnp.asarray(np.stack(masks), F32)


def _block_refs(cum, s, reverse, diag):
    c = cum.shape[0]
    span = s if diag else 2 * s
    parts = []
    for p in range(c // span):
        if diag:
            r = p * span + (span - 1 if reverse else 0)
        else:
            r = p * span + (s - 1 if reverse else s)
        parts.append(jnp.broadcast_to(cum[r:r + 1, :], (span, cum.shape[1])))
    return jnp.concatenate(parts, axis=0)


def _cumsum_rows(a_hl, tri):
    parts = jnp.dot(tri, a_hl, preferred_element_type=F32)
    w = a_hl.shape[1] // 2
    return parts[:, :w] + parts[:, w:]


def _gla_chunk(q, k, v, a, state_t, tri, mask_ref, reverse, want_out):
    c = k.shape[0]
    hc = c // 2
    nt = (((1,), (1,)), ((), ()))
    k = k.astype(F32)
    q = q.astype(F32) if want_out else None
    cum = _cumsum_rows(a, tri)
    last = 0 if reverse else c - 1
    total = cum[last:last + 1, :]
    yield

    def scale(x, log2_factor):
        return (x * jnp.exp2(log2_factor)).astype(BF16)

    out = None
    if want_out:
        out = lax.dot_general(scale(q, cum), state_t.astype(BF16), nt, preferred_element_type=F32)
        halves = (slice(hc, c), slice(0, hc)) if reverse else (slice(0, hc), slice(hc, c))
        early, late = halves
        r = hc - 1 if reverse else hc
        g = cum[r:r + 1, :]
        cross = lax.dot_general(scale(q[late], cum[late] - g), scale(k[early], g - cum[early]), nt,
                                preferred_element_type=F32)
        yield
        inner = [jnp.zeros((hc, hc), F32), jnp.zeros((hc, hc), F32)]
        s = hc // 2
        level = 0
        while True:
            diag = s < GLA_DIAG
            for x, rows in enumerate(halves):
                cx, qx, kx = cum[rows], q[rows], k[rows]
                d = cx - _block_refs(cx, GLA_DIAG if diag else s, reverse, diag)
                if diag:
                    qs, ks = scale(qx, d), scale(kx, -d)
                else:
                    e = jnp.exp2(-jnp.abs(d))
                    qs, ks = (qx * e).astype(BF16), (kx * e).astype(BF16)
                inner[x] = inner[x] + lax.dot_general(qs, ks, nt, preferred_element_type=F32) * mask_ref[level]
            yield
            if diag:
                break
            s //= 2
            level += 1
        out_early = jnp.dot(inner[0].astype(BF16), v[early], preferred_element_type=F32)
        out_late = jnp.dot(jnp.concatenate([cross, inner[1]], axis=1).astype(BF16),
                           jnp.concatenate([v[early], v[late]], axis=0), preferred_element_type=F32)
        intra = [out_late, out_early] if reverse else [out_early, out_late]
        out = out + jnp.concatenate(intra, axis=0)
        yield
    upd = lax.dot_general(v, scale(k, total - cum), (((0,), (0,)), ((), ())), preferred_element_type=F32)
    return out, jnp.exp2(total) * state_t + upd


def _gla_kernel(qf_ref, kf_ref, vf_ref, af_ref, qb_ref, kb_ref, vb_ref, ab_ref,
                kc_ref, vc_ref, acf_ref, acb_ref, tri_ref, mask_ref,
                of_ref, ob_ref, sf_ref, sb_ref):
    c = pl.program_id(2)
    fm, bm = mask_ref.at[0], mask_ref.at[1]
    heads = sf_ref.shape[0]

    @pl.when(c == 0)
    def _():
        zero = jnp.zeros(sf_ref.shape[1:], F32)
        chains = []
        for h in range(heads):
            kc = kc_ref[:, h * GLA_DK:(h + 1) * GLA_DK]
            vc = vc_ref[:, h * GLA_DV:(h + 1) * GLA_DV]
            hl = slice(2 * h * GLA_DK, 2 * (h + 1) * GLA_DK)
            chains.append(_gla_chunk(None, kc, vc, acf_ref[:, hl], zero, tri_ref[0], fm, False, False))
            chains.append(_gla_chunk(None, kc, vc, acb_ref[:, hl], zero, tri_ref[1], bm, True, False))
        for i, (_, state) in enumerate(_interleave(chains)):
            (sb_ref if i % 2 else sf_ref)[i // 2] = state

    @pl.when(c > 0)
    def _():
        chains = []
        for h in range(heads):
            ks = slice(h * GLA_DK, (h + 1) * GLA_DK)
            vs = slice(h * GLA_DV, (h + 1) * GLA_DV)
            hl = slice(2 * h * GLA_DK, 2 * (h + 1) * GLA_DK)
            chains.append(_gla_chunk(qf_ref[:, ks], kf_ref[:, ks], vf_ref[:, vs], af_ref[:, hl], sf_ref[h],
                                     tri_ref[0], fm, False, True))
            chains.append(_gla_chunk(qb_ref[:, ks], kb_ref[:, ks], vb_ref[:, vs], ab_ref[:, hl], sb_ref[h],
                                     tri_ref[1], bm, True, True))
        for i, (o, state) in enumerate(_interleave(chains, GLA_INTERLEAVE)):
            h = i // 2
            vs = slice(h * GLA_DV, (h + 1) * GLA_DV)
            if i % 2:
                ob_ref[:, vs] = o.astype(ob_ref.dtype)
                sb_ref[h] = state
            else:
                of_ref[:, vs] = o.astype(of_ref.dtype)
                sf_ref[h] = state


def _gla(gla_qk, vb, decay, k_ctx, v_ctx, decay_ctx, batch, t_len, ctx_len):
    nc = t_len // GLA_CHUNK
    h = GLA_HEADS
    qk = gla_qk.reshape(batch, t_len, 2 * GLA_QK_WIDTH)
    v3 = vb.reshape(batch, t_len, GLA_V_WIDTH)
    a3 = decay.reshape(batch, t_len, GLA_DECAY_WIDTH)
    kc3 = k_ctx.reshape(batch, ctx_len, GLA_QK_WIDTH)
    vc3 = v_ctx.reshape(batch, ctx_len, GLA_V_WIDTH)
    ac3 = decay_ctx.reshape(batch, ctx_len, GLA_DECAY_WIDTH)
    tri, masks = _gla_constants(GLA_CHUNK)

    def fwd(c):
        return jnp.maximum(c - 1, 0)

    def bwd(c):
        return nc - 1 - jnp.maximum(c - 1, 0)

    def const(a):
        return pl.BlockSpec(a.shape, lambda b, hh, c: (0,) * a.ndim, pipeline_mode=pl.Buffered(1))

    hp = GLA_HEADS_PER_STEP
    groups = h // hp
    cq = (None, GLA_CHUNK, hp * GLA_DK)
    cv = (None, GLA_CHUNK, hp * GLA_DV)
    ca = (None, GLA_CHUNK, hp * 2 * GLA_DK)
    cca = (None, ctx_len, hp * 2 * GLA_DK)
    in_specs = [
        pl.BlockSpec(cq, lambda b, g, c: (b, fwd(c), g)),
        pl.BlockSpec(cq, lambda b, g, c: (b, fwd(c), groups + g)),
        pl.BlockSpec(cv, lambda b, g, c: (b, fwd(c), g)),
        pl.BlockSpec(ca, lambda b, g, c: (b, fwd(c), g)),
        pl.BlockSpec(cq, lambda b, g, c: (b, bwd(c), g)),
        pl.BlockSpec(cq, lambda b, g, c: (b, bwd(c), groups + g)),
        pl.BlockSpec(cv, lambda b, g, c: (b, bwd(c), g)),
        pl.BlockSpec(ca, lambda b, g, c: (b, bwd(c), groups + g)),
        pl.BlockSpec((None, ctx_len, hp * GLA_DK), lambda b, g, c: (b, 0, g)),
        pl.BlockSpec((None, ctx_len, hp * GLA_DV), lambda b, g, c: (b, 0, g)),
        pl.BlockSpec(cca, lambda b, g, c: (b, 0, g)),
        pl.BlockSpec(cca, lambda b, g, c: (b, 0, groups + g)),
        const(tri), const(masks),
    ]
    out_specs = [pl.BlockSpec(cv, lambda b, g, c: (b, fwd(c), g)),
                 pl.BlockSpec(cv, lambda b, g, c: (b, bwd(c), g))]
    out_shape = [jax.ShapeDtypeStruct((batch, t_len, GLA_V_WIDTH), BF16)] * 2
    return pl.pallas_call(
        _gla_kernel,
        out_shape=out_shape,
        grid=(batch, groups, nc + 1),
        in_specs=in_specs,
        out_specs=out_specs,
        scratch_shapes=[pltpu.VMEM((hp, GLA_DV, GLA_DK), F32), pltpu.VMEM((hp, GLA_DV, GLA_DK), F32)],
        compiler_params=_params(3),
        name="gla",
    )(qk, qk, v3, a3, qk, qk, v3, a3, kc3, vc3, ac3, ac3, tri, masks)


def _decay_weights(w_f, b_f, w_b, b_b):
    r, width = w_f.shape
    w2 = jnp.zeros((LANES, 2 * width), F32).at[:r, :width].set(w_f).at[r:2 * r, width:].set(w_b)
    return w2.astype(BF16), jnp.concatenate([b_f, b_b]).reshape(1, 2 * width)


def _layer_norm(v, g, b):
    mu = jnp.mean(v, axis=1, keepdims=True)
    var = jnp.mean(jnp.square(v - mu), axis=1, keepdims=True)
    return (v - mu) * lax.rsqrt(var + EPS) * g + b


def _merge_kernel(oa_ref, of_ref, ob_ref, gb_ref, gates_ref, x_ref, g1_ref, sc2_ref, sh2_ref,
                  wa_ref, wb_ref, wo_ref, nw_ref, lg_ref, lb_ref, rw_ref,
                  x1_ref, h2_ref, sc_ref):
    d = x_ref.shape[1]
    o = of_ref[...].astype(F32) + ob_ref[...].astype(F32)
    pieces = []
    for hh in range(GLA_HEADS):
        oh = o[:, hh * GLA_DV:(hh + 1) * GLA_DV]
        pieces.append(oh * lax.rsqrt(jnp.mean(jnp.square(oh), axis=1, keepdims=True) + EPS))
    out_b = jnp.concatenate(pieces, axis=1) * nw_ref[...] * _silu(gb_ref[...].astype(F32))
    ya = jnp.dot(oa_ref[...], wa_ref[...], preferred_element_type=F32)
    yb = jnp.dot(out_b.astype(BF16), wb_ref[...], preferred_element_type=F32)
    y = _sigmoid(gates_ref[:, :d].astype(F32)) * ya + _sigmoid(gates_ref[:, d:].astype(F32)) * yb
    y2 = jnp.dot(y.astype(BF16), wo_ref[...], preferred_element_type=F32)
    x1 = _layer_norm(DEEPNORM_ALPHA * x_ref[...] + g1_ref[0] * y2, lg_ref[...], lb_ref[...])
    x1_ref[...] = x1
    h2 = x1 * (1.0 + sc2_ref[0]) + sh2_ref[0]
    h2_ref[...] = _pack_pairs(h2)
    logits_t = lax.dot_general(rw_ref[...], h2.astype(BF16), (((1,), (1,)), ((), ())), preferred_element_type=F32)
    sc_ref[...] = _sigmoid(logits_t)


def _merge(out_a, o_f, o_b, gb, gates, x2d, g1, sc2, sh2, wa, wb, wo, nw, lg, lb, rw_t, tile, tiles_per_batch):
    n, d = x2d.shape
    row = lambda i: (i, 0)
    mod = lambda i: (i // tiles_per_batch, 0, 0)
    full = lambda i: (0, 0)

    def const(a):
        return pl.BlockSpec(a.shape, full, pipeline_mode=pl.Buffered(1))

    in_specs = [pl.BlockSpec((tile, NA_WIDTH), row), pl.BlockSpec((tile, GLA_V_WIDTH), row),
                pl.BlockSpec((tile, GLA_V_WIDTH), row), pl.BlockSpec((tile, GLA_V_WIDTH), row),
                pl.BlockSpec((tile, 2 * d), row), pl.BlockSpec((tile, d), row),
                pl.BlockSpec((1, 1, d), mod), pl.BlockSpec((1, 1, d), mod), pl.BlockSpec((1, 1, d), mod),
                const(wa), const(wb), const(wo), const(nw), const(lg), const(lb), const(rw_t)]
    out_shape = [jax.ShapeDtypeStruct((n, d), F32), jax.ShapeDtypeStruct((n, d // 2), U32),
                 jax.ShapeDtypeStruct((N_EXPERTS, n), F32)]
    out_specs = [pl.BlockSpec((tile, d), row), pl.BlockSpec((tile, d // 2), row),
                 pl.BlockSpec((N_EXPERTS, tile), lambda i: (0, i))]
    return pl.pallas_call(
        _merge_kernel, out_shape=out_shape, grid=(n // tile,), in_specs=in_specs, out_specs=out_specs,
        compiler_params=_params(1), name="merge_ln1_router",
    )(out_a, o_f, o_b, gb, gates, x2d, g1, sc2, sh2, wa, wb, wo, nw, lg, lb, rw_t)


def _first_argmax(vals, idx, n):
    m = jnp.max(vals, axis=0, keepdims=True)
    first = jnp.min(jnp.where(vals == m, idx, float(n)), axis=0, keepdims=True)
    return m, first


def _route_kernel(sc_ref, bias_ref, before_ref, e_ref, w_ref, rank_ref, cnt_ref, carry_ref):
    step = pl.program_id(0)
    tr = sc_ref.shape[1]

    @pl.when(step == 0)
    def _():
        carry_ref[...] = jnp.zeros(carry_ref.shape, F32)

    scores = sc_ref[...]
    biased = scores + bias_ref[...]
    eidx = lax.broadcasted_iota(jnp.int32, (N_EXPERTS, tr), 0).astype(F32)
    lidx = lax.broadcasted_iota(jnp.int32, (GROUP_SIZE, tr), 0).astype(F32)
    gidx = lax.broadcasted_iota(jnp.int32, (N_GROUPS, tr), 0).astype(F32)
    gs = []
    for g in range(N_GROUPS):
        blk = biased[g * GROUP_SIZE:(g + 1) * GROUP_SIZE]
        m1, first = _first_argmax(blk, lidx, GROUP_SIZE)
        m2 = jnp.max(jnp.where(lidx == first, -jnp.inf, blk), axis=0, keepdims=True)
        gs.append(m1 + m2)
    cur = jnp.concatenate(gs, axis=0)
    keep = jnp.zeros((N_GROUPS, tr), F32)
    for _ in range(TOPK_GROUPS):
        _, first = _first_argmax(cur, gidx, N_GROUPS)
        sel = gidx == first
        keep = jnp.where(sel, 1.0, keep)
        cur = jnp.where(sel, -jnp.inf, cur)
    keep_e = jnp.concatenate([jnp.broadcast_to(keep[g:g + 1], (GROUP_SIZE, tr)) for g in range(N_GROUPS)], axis=0)
    masked = jnp.where(keep_e > 0.5, biased, -jnp.inf)
    chosen = jnp.zeros((N_EXPERTS, tr), F32)
    tops, topi = [], []
    for _ in range(TOP_K):
        _, first = _first_argmax(masked, eidx, N_EXPERTS)
        sel = eidx == first
        tops.append(jnp.sum(jnp.where(sel, scores, 0.0), axis=0, keepdims=True))
        topi.append(first)
        chosen = jnp.where(sel, 1.0, chosen)
        masked = jnp.where(sel, -jnp.inf, masked)
    top_s = jnp.concatenate(tops, axis=0)
    top_i = jnp.concatenate(topi, axis=0)
    e_ref[...] = top_i.astype(jnp.int32)
    w_ref[...] = (top_s / jnp.sum(top_s, axis=0, keepdims=True) * ROUTED_SCALE).T
    prior = jnp.dot(chosen.astype(BF16), before_ref[...], preferred_element_type=F32) + carry_ref[...]
    ranks = [jnp.sum(jnp.where(eidx == topi[kk], prior, 0.0), axis=0, keepdims=True) for kk in range(TOP_K)]
    rank_ref[...] = jnp.concatenate(ranks, axis=0).astype(jnp.int32)
    carry_ref[...] = carry_ref[...] + jnp.sum(chosen, axis=1, keepdims=True)
    cnt_ref[...] = jnp.broadcast_to(carry_ref[...], cnt_ref.shape).astype(jnp.int32)


def _route(scores_t, router_bias, tile):
    n = scores_t.shape[1]
    col = lambda i: (0, i)
    out_shape = [jax.ShapeDtypeStruct((TOP_K, n), jnp.int32), jax.ShapeDtypeStruct((n, TOP_K), F32),
                 jax.ShapeDtypeStruct((TOP_K, n), jnp.int32), jax.ShapeDtypeStruct((N_EXPERTS, LANES), jnp.int32)]
    before = jnp.asarray(np.arange(tile)[:, None] < np.arange(tile)[None, :], BF16)
    return pl.pallas_call(
        _route_kernel, out_shape=out_shape, grid=(n // tile,),
        in_specs=[pl.BlockSpec((N_EXPERTS, tile), col), pl.BlockSpec((N_EXPERTS, 1), lambda i: (0, 0)),
                  pl.BlockSpec((tile, tile), lambda i: (0, 0), pipeline_mode=pl.Buffered(1))],
        out_specs=[pl.BlockSpec((TOP_K, tile), col), pl.BlockSpec((tile, TOP_K), lambda i: (i, 0)),
                   pl.BlockSpec((TOP_K, tile), col), pl.BlockSpec((N_EXPERTS, LANES), lambda i: (0, 0))],
        scratch_shapes=[pltpu.VMEM((N_EXPERTS, 1), F32)],
        compiler_params=_params(1), name="route",
    )(scores_t, router_bias.reshape(N_EXPERTS, 1), before)


def _slots_kernel(e_ref, rank_ref, start_ref, dest_ref):
    tr = e_ref.shape[1]
    eidx = lax.broadcasted_iota(jnp.int32, (N_EXPERTS, tr), 0)
    e = e_ref[...]
    start = start_ref[...]
    rows = [jnp.sum(jnp.where(eidx == e[kk:kk + 1], start, 0.0), axis=0, keepdims=True) for kk in range(TOP_K)]
    dest_ref[...] = jnp.concatenate(rows, axis=0).astype(jnp.int32) + rank_ref[...]


def _slots(top_e, rank, start_rows, tile):
    n = top_e.shape[1]
    col = lambda i: (0, i)
    return pl.pallas_call(
        _slots_kernel, out_shape=jax.ShapeDtypeStruct((TOP_K, n), jnp.int32), grid=(n // tile,),
        in_specs=[pl.BlockSpec((TOP_K, tile), col), pl.BlockSpec((TOP_K, tile), col),
                  pl.BlockSpec((N_EXPERTS, 1), lambda i: (0, 0))],
        out_specs=pl.BlockSpec((TOP_K, tile), col),
        compiler_params=_params(1), name="slots",
    )(top_e, rank, start_rows)


def _sc_mesh():
    return plsc.VectorSubcoreMesh(core_axis_name="c", subcore_axis_name="s",
                                  num_cores=SC_CORES, num_subcores=SC_SUBCORES)


def _sc_scatter_rows(rows, idx, n_out):
    n, dp = rows.shape
    copies = idx.shape[0] // n
    per_worker = n // (SC_CORES * SC_SUBCORES)
    chunk = SC_GATHER_ROWS

    @functools.partial(
        pl.kernel, mesh=_sc_mesh(), out_type=jax.ShapeDtypeStruct((n_out, dp), rows.dtype),
        scratch_types=[pltpu.VMEM((copies, chunk), jnp.int32), pltpu.VMEM((chunk, dp), rows.dtype),
                       pltpu.SemaphoreType.DMA, pltpu.SemaphoreType.DMA],
        name="sc_scatter_rows")
    def scatter(rows_hbm, idx_hbm, out_hbm, idx_v, rows_v, sem_idx, sem_out):
        base = (lax.axis_index("s") * SC_CORES + lax.axis_index("c")) * per_worker

        @pl.loop(0, per_worker // chunk)
        def _(j):
            off = base + j * chunk
            loads = [pltpu.async_copy(idx_hbm.at[pl.ds(k * n + off, chunk)], idx_v.at[k], sem_idx)
                     for k in range(copies)]
            pltpu.sync_copy(rows_hbm.at[pl.ds(off, chunk)], rows_v)
            for cp in loads:
                cp.wait()
            stores = [pltpu.async_copy(rows_v, out_hbm.at[idx_v.at[k]], sem_out) for k in range(copies)]
            for cp in stores:
                cp.wait()

    return scatter(rows, idx)


def _expert_kernel(start_ref, cnt_ref, nused_ref, xs_ref, wg_ref, wu_ref, wd_ref, y_ref,
                   wgb, wub, wdb, xbuf, ybuf, sem_in, sem_out):
    e = pl.program_id(0)
    bm, ring = EXPERT_BLOCK, EXPERT_RING
    cnt = cnt_ref[e]
    nb = (cnt + bm - 1) // bm
    g0 = start_ref[e]
    n_used = nused_ref[0]

    def x_copy(g):
        slot = g % ring
        return pltpu.make_async_copy(xs_ref.at[pl.ds(g * bm, bm)], xbuf.at[slot], sem_in.at[slot])

    def y_copy(g):
        slot = g % ring
        return pltpu.make_async_copy(ybuf.at[slot], y_ref.at[pl.ds(g * bm, bm)], sem_out.at[slot])

    @pl.when(e == 0)
    def _():
        for g in range(ring):
            @pl.when(g < n_used)
            def _():
                x_copy(g).start()

    wgb[...] = wg_ref[...].astype(BF16)
    wub[...] = wu_ref[...].astype(BF16)
    wdb[...] = wd_ref[...].astype(BF16)
    row = lax.broadcasted_iota(jnp.int32, (bm, xbuf.shape[2]), 0)

    def blocks(j, count):
        for b in range(count):
            g = g0 + j + b
            x_copy(g).wait()

            @pl.when(g >= ring)
            def _():
                y_copy(g - ring).wait()

        def swiglu(b):
            slot = (g0 + j + b) % ring
            x = _unpack_pairs(jnp.where(row < cnt - (j + b) * bm, xbuf[slot], jnp.uint32(0))).astype(BF16)
            yield
            gate = jnp.dot(x, wgb[...], preferred_element_type=F32)
            up = jnp.dot(x, wub[...], preferred_element_type=F32)
            yield
            act = (_silu(gate) * up).astype(BF16)
            yield
            y = jnp.dot(act, wdb[...], preferred_element_type=F32)
            yield
            ybuf[slot] = _pack_pairs(y)

        _interleave([swiglu(b) for b in range(count)])

        for b in range(count):
            g = g0 + j + b
            y_copy(g).start()

            @pl.when(g + ring < n_used)
            def _():
                x_copy(g + ring).start()

    group = EXPERT_GROUP

    def full_group(p, carry):
        blocks(group * p, group)
        return carry

    lax.fori_loop(0, nb // group, full_group, 0)
    done = (nb // group) * group
    size = group // 2
    while size >= 1:
        @pl.when((nb - done) % (2 * size) >= size)
        def _(done=done, size=size):
            blocks(done, size)

        done = done + jnp.where((nb - done) % (2 * size) >= size, size, 0)
        size //= 2

    @pl.when(e == pl.num_programs(0) - 1)
    def _():
        for back in range(1, ring + 1):
            @pl.when(n_used - back >= 0)
            def _():
                y_copy(n_used - back).wait()


def _experts(blk_start, counts, n_used, xs, wg, wu, wd):
    n_slots, dp = xs.shape
    n_exp, d, ff = wg.shape
    bm, ring = EXPERT_BLOCK, EXPERT_RING
    wsel = lambda e, st, ct, nu: (e, 0, 0)
    grid_spec = pltpu.PrefetchScalarGridSpec(
        num_scalar_prefetch=3, grid=(n_exp,),
        in_specs=[pl.BlockSpec(memory_space=pl.ANY),
                  pl.BlockSpec((None, d, ff), wsel), pl.BlockSpec((None, d, ff), wsel),
                  pl.BlockSpec((None, ff, d), wsel)],
        out_specs=pl.BlockSpec(memory_space=pl.ANY),
        scratch_shapes=[pltpu.VMEM((d, ff), BF16), pltpu.VMEM((d, ff), BF16), pltpu.VMEM((ff, d), BF16),
                        pltpu.VMEM((ring, bm, dp), U32), pltpu.VMEM((ring, bm, dp), U32),
                        pltpu.SemaphoreType.DMA((ring,)), pltpu.SemaphoreType.DMA((ring,))])
    return pl.pallas_call(
        _expert_kernel, out_shape=jax.ShapeDtypeStruct((n_slots, dp), U32), grid_spec=grid_spec,
        compiler_params=_params(1), name="experts",
    )(blk_start, counts, n_used, xs, wg, wu, wd)


def _sc_gather_rows(table, idx):
    n_idx = idx.shape[0]
    dp = table.shape[1]
    workers = SC_CORES * SC_SUBCORES
    per_worker = n_idx // workers
    chunk = SC_GATHER_ROWS

    @functools.partial(
        pl.kernel, mesh=_sc_mesh(), out_type=jax.ShapeDtypeStruct((n_idx, dp), table.dtype),
        scratch_types=[pltpu.VMEM((chunk,), jnp.int32), pltpu.VMEM((chunk, dp), table.dtype),
                       pltpu.SemaphoreType.DMA],
        name="sc_gather_rows")
    def gather(table_hbm, idx_hbm, out_hbm, idx_v, rows_v, sem):
        base = (lax.axis_index("s") * SC_CORES + lax.axis_index("c")) * per_worker

        @pl.loop(0, per_worker // chunk)
        def _(j):
            off = base + j * chunk
            pltpu.sync_copy(idx_hbm.at[pl.ds(off, chunk)], idx_v)
            pltpu.async_copy(table_hbm.at[idx_v], rows_v, sem).wait()
            pltpu.sync_copy(rows_v, out_hbm.at[pl.ds(off, chunk)])

    return gather(table, idx)


def _combine_kernel(tile, yt_ref, w_ref, h_ref, x1_ref, g2_ref, sg_ref, su_ref, sd_ref, lg_ref, lb_ref, *rest):
    o_ref = rest[-1]
    hb = _unpack_pairs(h_ref[...]).astype(BF16)
    g = jnp.dot(hb, sg_ref[...], preferred_element_type=F32)
    u = jnp.dot(hb, su_ref[...], preferred_element_type=F32)
    f = jnp.dot((_silu(g) * u).astype(BF16), sd_ref[...], preferred_element_type=F32)
    w = w_ref[...]
    for kk in range(TOP_K):
        f = f + w[:, kk:kk + 1] * _unpack_pairs(yt_ref[kk * tile:(kk + 1) * tile, :])
    o_ref[...] = _layer_norm(DEEPNORM_ALPHA * x1_ref[...] + g2_ref[0] * f, lg_ref[...], lb_ref[...])


def _combine(y_tok, top_w, h2p, x1, g2, sg, su, sd, lg, lb, tile, tiles_per_batch, first_tile, prev_out):
    n, d = x1.shape
    dp = h2p.shape[1]
    row = lambda i: (first_tile + i, 0)
    full = lambda i: (0, 0)

    def const(a):
        return pl.BlockSpec(a.shape, full, pipeline_mode=pl.Buffered(1))

    in_specs = [pl.BlockSpec((TOP_K * tile, dp), lambda i: (i, 0)),
                pl.BlockSpec((tile, TOP_K), row), pl.BlockSpec((tile, dp), row), pl.BlockSpec((tile, d), row),
                pl.BlockSpec((1, 1, d), lambda i: ((first_tile + i) // tiles_per_batch, 0, 0)),
                const(sg), const(su), const(sd), const(lg), const(lb)]
    args = [y_tok, top_w, h2p, x1, g2, sg, su, sd, lg, lb]
    aliases = {}
    if prev_out is not None:
        in_specs.append(pl.BlockSpec(memory_space=pl.ANY))
        args.append(prev_out)
        aliases = {len(args) - 1: 0}
    return pl.pallas_call(
        functools.partial(_combine_kernel, tile),
        out_shape=jax.ShapeDtypeStruct((n, d), F32),
        grid=(y_tok.shape[0] // (TOP_K * tile),),
        in_specs=in_specs,
        out_specs=pl.BlockSpec((tile, d), row),
        input_output_aliases=aliases,
        compiler_params=_params(1), name="combine_shared_ln2",
    )(*args)


def kernel(x, c, ctx, c_ctx, w_mod, b_mod, w_in, na_rpb, gla_w_decay_f, gla_b_decay_f, gla_w_decay_b, gla_b_decay_b,
           gla_norm_w, w_branch_a, w_branch_b, w_out, ln1_g, ln1_b, router_w, router_bias, exp_w_gate, exp_w_up,
           exp_w_down, sh_w_gate, sh_w_up, sh_w_down, ln2_g, ln2_b):
    batch, t_len, d = x.shape
    ctx_len = ctx.shape[1]
    n = batch * t_len
    assert w_mod.shape[0] == DEPTH == 1
    assert t_len % GLA_CHUNK == 0 and ctx_len == GLA_CHUNK and (t_len // GRID_W) % (NA_ROW_BLOCK * NA_BLOCKS_PER_STEP) == 0

    mod_rows = 16
    c_all = jnp.zeros((mod_rows, d), F32).at[:batch].set(c).at[batch].set(c_ctx)
    mod = _modulation(c_all, w_mod[0], b_mod[0])
    sh1, sc1, g1, sh2, sc2, g2 = [mod[:batch, j * d:(j + 1) * d].reshape(batch, 1, d) for j in range(6)]
    sh1c = mod[batch:batch + 1, 0:d].reshape(1, 1, d)
    sc1c = mod[batch:batch + 1, d:2 * d].reshape(1, 1, d)

    offs = np.cumsum((0, NA_WIDTH, NA_WIDTH, NA_WIDTH, GLA_QK_WIDTH, GLA_QK_WIDTH, GLA_V_WIDTH, GLA_V_WIDTH,
                      GLA_GATE_RANK, GLA_GATE_RANK, d, d))
    qa, ka, va, qb, kb, vbc, gbc, lrf, lrb, ga, gbt = [w_in[0][:, offs[j]:offs[j + 1]] for j in range(11)]
    lr_cols = jnp.concatenate([lrf, lrb, jnp.zeros((d, LANES - 2 * GLA_GATE_RANK), F32)], axis=1)
    w_lat = jnp.concatenate([lr_cols, qa, ka, va, qb, kb, vbc, gbc, ga, gbt], axis=1).astype(BF16)
    lat_offs = np.cumsum((0, LANES, NA_WIDTH, NA_WIDTH, NA_WIDTH, GLA_QK_WIDTH, GLA_QK_WIDTH, GLA_V_WIDTH))
    w_ctx = jnp.concatenate([w_lat[:, lat_offs[j]:lat_offs[j + 1]] for j in (0, 2, 3, 5, 6)], axis=1)
    plain = ("plain",)
    lat_plan = ((LANES, (("decay",),)),
                (3 * NA_WIDTH, (("scale", NA_HEAD_DIM ** -0.5), plain, plain)),
                (2 * GLA_QK_WIDTH, (("rope", GLA_DK ** -0.5), ("rope", 1.0))),
                (GLA_V_WIDTH, (plain, plain)), (GLA_V_WIDTH, (plain, plain)),
                (2 * d, (plain,) * 4))
    ctx_plan = ((LANES, (("decay",),)),
                (2 * NA_WIDTH, (plain, plain)), (GLA_QK_WIDTH, (plain,)), (GLA_V_WIDTH, (plain, plain)))
    w2, b2 = _decay_weights(gla_w_decay_f[0], gla_b_decay_f[0], gla_w_decay_b[0], gla_b_decay_b[0])
    tile = COMBINE_TILE
    x2d = x.reshape(n, d)
    decay, na_qkv, gla_qk, vb, gb, gates = _projection(
        x2d, sc1, sh1, w_lat, w2, b2, lat_plan, (BF16,) * 6, PROJ_TILE, t_len // PROJ_TILE,
        rope=_rope_tables(t_len))
    decay_c, na_kv_c, k_c, v_c = _projection(
        ctx.reshape(batch * ctx_len, d), sc1c, sh1c, w_ctx, w2, b2, ctx_plan, (BF16,) * 4, tile,
        batch * ctx_len // tile)

    out_a = _na_attention(na_qkv, na_kv_c, _na_bias_tables(na_rpb[0]), batch, t_len, ctx_len)
    o_f, o_b = _gla(gla_qk, vb, decay, k_c, v_c, decay_c, batch, t_len, ctx_len)

    x1, h2p, scores_t = _merge(
        out_a.reshape(n, NA_WIDTH), o_f.reshape(n, GLA_V_WIDTH), o_b.reshape(n, GLA_V_WIDTH), gb, gates, x2d,
        g1, sc2, sh2, w_branch_a[0].astype(BF16), w_branch_b[0].astype(BF16), w_out[0].astype(BF16),
        gla_norm_w[0].reshape(1, -1), ln1_g[0].reshape(1, d), ln1_b[0].reshape(1, d),
        router_w[0].T.astype(BF16), PROJ_TILE, t_len // PROJ_TILE)

    top_e, top_w, rank, counts = _route(scores_t, router_bias[0], 512)

    counts = counts[:, 0]
    n_blocks = pl.cdiv(n * TOP_K, EXPERT_BLOCK) + N_EXPERTS
    blocks_per = (counts + EXPERT_BLOCK - 1) // EXPERT_BLOCK
    blk_end = jnp.cumsum(blocks_per)
    blk_start = blk_end - blocks_per
    dest = _slots(top_e, rank, (blk_start * EXPERT_BLOCK).astype(F32).reshape(N_EXPERTS, 1), 2048)

    xs = _sc_scatter_rows(h2p, dest.reshape(TOP_K * n), n_blocks * EXPERT_BLOCK)
    y = _experts(blk_start.astype(jnp.int32), counts, blk_end[-1:].astype(jnp.int32), xs,
                 exp_w_gate[0], exp_w_up[0], exp_w_down[0])
    dest_tok = dest.reshape(TOP_K, n // tile, tile).transpose(1, 0, 2).reshape(COMBINE_CHUNKS, -1)
    w_rows = top_w
    shared = (sh_w_gate[0].astype(BF16), sh_w_up[0].astype(BF16), sh_w_down[0].astype(BF16),
              ln2_g[0].reshape(1, d), ln2_b[0].reshape(1, d))
    tiles_per_chunk = n // tile // COMBINE_CHUNKS
    out = None
    for ci in range(COMBINE_CHUNKS):
        y_tok = _sc_gather_rows(y, dest_tok[ci])
        out = _combine(y_tok, w_rows, h2p, x1, g2, *shared, tile, t_len // tile, ci * tiles_per_chunk, out)
    return out.reshape(batch, t_len, d)
```

```python
import functools

import numpy as np
import jax
import jax.numpy as jnp
from jax import lax
from jax.experimental import pallas as pl
from jax.experimental.pallas import tpu as pltpu
from jax.experimental.pallas import tpu_sc as plsc

F32 = jnp.float32
BF16 = jnp.bfloat16
U32 = jnp.uint32
HIGHEST = lax.Precision.HIGHEST

GRID_W = 64
NA_HEADS = 8
NA_HEAD_DIM = 64
NA_WIDTH = NA_HEADS * NA_HEAD_DIM
WIN_ROWS = 8
WIN_COLS = 16
GLA_HEADS = 4
GLA_DK = 128
GLA_DV = 256
GLA_QK_WIDTH = GLA_HEADS * GLA_DK
GLA_V_WIDTH = GLA_HEADS * GLA_DV
GLA_GATE_RANK = 16
GLA_TAU = 16.0
LOG2E = 1.4426950408889634
ROPE_BASE = 10000.0
N_EXPERTS = 256
TOP_K = 8
N_GROUPS = 8
TOPK_GROUPS = 4
GROUP_SIZE = N_EXPERTS // N_GROUPS
ROUTED_SCALE = 2.5
DEPTH = 1
DEEPNORM_ALPHA = (2 * DEPTH) ** 0.25
EPS = 1e-6

LANES = 128
PROJ_TILE = 512
PROJ_INTERLEAVE = 2
NA_ROW_BLOCK = 4
NA_BLOCKS_PER_STEP = 2
NA_INTERLEAVE = 4
NA_UNION_ROWS = NA_ROW_BLOCK + WIN_ROWS - 1
GLA_CHUNK = 256
GLA_DIAG = 16
GLA_HEADS_PER_STEP = 4
GLA_INTERLEAVE = 8
EXPERT_BLOCK = 272
EXPERT_GROUP = 4
EXPERT_RING = 12
COMBINE_CHUNKS = 8
COMBINE_TILE = 512
SC_CORES = 2
SC_SUBCORES = 16
SC_GATHER_ROWS = 128
NEG_BIG = -1e30
VMEM_LIMIT = 56 * 1024 * 1024
NA_VMEM_LIMIT = 58 * 1024 * 1024


def _params(n_axes, vmem=VMEM_LIMIT):
    return pltpu.CompilerParams(dimension_semantics=("arbitrary",) * n_axes, vmem_limit_bytes=vmem)


def _sigmoid(v):
    return 1.0 / (1.0 + jnp.exp2(v * (-LOG2E)))


def _silu(v):
    return v * _sigmoid(v)


def _interleave(chains, group=None):
    if group is not None and group < len(chains):
        return [r for i in range(0, len(chains), group) for r in _interleave(chains[i:i + group])]
    results = [None] * len(chains)
    active = list(enumerate(chains))
    while active:
        still = []
        for i, chain in active:
            try:
                next(chain)
                still.append((i, chain))
            except StopIteration as stop:
                results[i] = stop.value
        active = still
    return results


def _pack_pairs(v):
    m = v.shape[1] // 2
    lo = lax.bitcast_convert_type(v[:, :m].astype(BF16).astype(F32), U32) >> 16
    hi = lax.bitcast_convert_type(v[:, m:].astype(BF16).astype(F32), U32) & jnp.uint32(0xFFFF0000)
    return lo | hi


def _unpack_pairs(p):
    lo = lax.bitcast_convert_type(p << 16, F32)
    hi = lax.bitcast_convert_type(p & jnp.uint32(0xFFFF0000), F32)
    return jnp.concatenate([lo, hi], axis=1)


def _mod_kernel(c_ref, w_ref, b_ref, o_ref):
    o_ref[...] = jnp.dot(_silu(c_ref[...]), w_ref[...], preferred_element_type=F32, precision=HIGHEST) + b_ref[...]


def _modulation(c_all, w_mod, b_mod):
    rows, d = c_all.shape
    n = w_mod.shape[1]
    bn = 512
    return pl.pallas_call(
        _mod_kernel,
        out_shape=jax.ShapeDtypeStruct((rows, n), F32),
        grid=(n // bn,),
        in_specs=[pl.BlockSpec((rows, d), lambda j: (0, 0)),
                  pl.BlockSpec((d, bn), lambda j: (0, j)),
                  pl.BlockSpec((1, bn), lambda j: (0, j))],
        out_specs=pl.BlockSpec((rows, bn), lambda j: (0, j)),
        compiler_params=_params(1),
        name="modulation",
    )(c_all, w_mod, b_mod.reshape(1, n))


def _swap32(v):
    lane = lax.broadcasted_iota(jnp.int32, v.shape, 1)
    return jnp.where((lane % 64) < 32, pltpu.roll(v, 96, 1), pltpu.roll(v, 32, 1))


GLA_DECAY_WIDTH = 2 * GLA_HEADS * 2 * GLA_DK


def _log2_decay_split(lr, w2, b2, out_ref):
    z = (jnp.dot(lr.astype(BF16), w2, preferred_element_type=F32) + b2) * LOG2E
    a = (jnp.minimum(z, 0.0) - jnp.log2(1.0 + jnp.exp2(-jnp.abs(z)))) * (1.0 / GLA_TAU)
    hi = a.astype(BF16)
    lo = (a - hi.astype(F32)).astype(BF16)
    for p in range(a.shape[1] // GLA_DK):
        src = slice(p * GLA_DK, (p + 1) * GLA_DK)
        out_ref[:, 2 * p * GLA_DK:(2 * p + 1) * GLA_DK] = hi[:, src]
        out_ref[:, (2 * p + 1) * GLA_DK:(2 * p + 2) * GLA_DK] = lo[:, src]


def _proj_kernel(plan, has_rope, *refs):
    x_ref, sc_ref, sh_ref, w_ref, w2_ref, b2_ref = refs[:6]
    pos = 6
    if has_rope:
        cos_ref, sin_ref = refs[6:8]
        pos = 8
    out_refs = refs[pos:]
    h = (x_ref[...] * (1.0 + sc_ref[0]) + sh_ref[0]).astype(BF16)

    def piece(out_ref, kind, w0, c0, cw):
        acc = jnp.dot(h, w_ref[:, w0:w0 + cw], preferred_element_type=F32)
        yield
        if kind[0] == "decay":
            _log2_decay_split(acc, w2_ref[...], b2_ref[...], out_ref)
            return
        if kind[0] == "scale":
            acc = acc * kind[1]
        elif kind[0] == "rope":
            cos, sin = cos_ref[...], sin_ref[...]
            parts = []
            for p in range(cw // LANES):
                v = acc[:, p * LANES:(p + 1) * LANES]
                parts.append((v * cos + _swap32(v) * sin) * kind[1])
            acc = jnp.concatenate(parts, axis=1)
        out_ref[:, c0:c0 + cw] = acc.astype(out_ref.dtype)

    chains = []
    col = 0
    for out_ref, (width, kinds) in zip(out_refs, plan):
        cw = width // len(kinds)
        for j, kind in enumerate(kinds):
            chains.append(piece(out_ref, kind, col + j * cw, j * cw, cw))
        col += width
    for i in range(0, len(chains), PROJ_INTERLEAVE):
        _interleave(chains[i:i + PROJ_INTERLEAVE])


def _projection(x2d, sc, sh, w, w2, b2, plan, out_dtypes, tile, tiles_per_mod, rope=None):
    n, d = x2d.shape
    const = lambda a: pl.BlockSpec(a.shape, lambda i: (0, 0), pipeline_mode=pl.Buffered(1))
    in_specs = [pl.BlockSpec((tile, d), lambda i: (i, 0)),
                pl.BlockSpec((1, 1, d), lambda i: (i // tiles_per_mod, 0, 0)),
                pl.BlockSpec((1, 1, d), lambda i: (i // tiles_per_mod, 0, 0)),
                const(w), const(w2), const(b2)]
    args = [x2d, sc, sh, w, w2, b2]
    if rope is not None:
        in_specs += [pl.BlockSpec((tile, LANES), lambda i: (i % tiles_per_mod, 0))] * 2
        args += list(rope)
    widths = [GLA_DECAY_WIDTH if kinds[0][0] == "decay" else wd for wd, kinds in plan]
    out_shape = [jax.ShapeDtypeStruct((n, wd), dt) for wd, dt in zip(widths, out_dtypes)]
    out_specs = [pl.BlockSpec((tile, wd), lambda i: (i, 0)) for wd in widths]
    return pl.pallas_call(
        functools.partial(_proj_kernel, plan, rope is not None),
        out_shape=out_shape,
        grid=(n // tile,),
        in_specs=in_specs,
        out_specs=out_specs,
        compiler_params=_params(1),
        name="in_proj" if rope is not None else "ctx_proj",
    )(*args)


def _rope_tables(t_len):
    half = GLA_DK // 2
    quarter = half // 2
    f32 = np.float32
    inv_freq = f32(ROPE_BASE) ** (-np.arange(quarter, dtype=f32) / f32(quarter))
    pos = np.arange(t_len)
    row_ang = (pos // GRID_W).astype(f32)[:, None] * inv_freq[None, :]
    col_ang = (pos % GRID_W).astype(f32)[:, None] * inv_freq[None, :]
    cr, sr, cc, sn = np.cos(row_ang), np.sin(row_ang), np.cos(col_ang), np.sin(col_ang)
    cos = np.concatenate([cr, cr, cc, cc], axis=1).astype(f32)
    sin = np.concatenate([-sr, sr, -sn, sn], axis=1).astype(f32)
    return jnp.asarray(cos), jnp.asarray(sin)


def _na_bias_tables(rpb):
    rb, ur, w = NA_ROW_BLOCK, NA_UNION_ROWS, GRID_W
    heads = rpb.shape[0]
    pad = jnp.pad(rpb, ((0, 0), (0, 0), (w, w)))
    toep = jnp.stack([pad[:, :, w + WIN_COLS - 1 - c:2 * w + WIN_COLS - 1 - c] for c in range(w)], axis=2)
    c = np.arange(w)[:, None]
    kc = np.arange(w)[None, :]
    col_start = np.clip(c - WIN_COLS // 2, 0, w - WIN_COLS)
    col_ok = (kc >= col_start) & (kc < col_start + WIN_COLS)
    toep = jnp.where(col_ok[None, None], toep, NEG_BIG)
    neg = jnp.full((heads, w, w), NEG_BIG, F32)
    half = WIN_ROWS // 2
    tables = []
    for lo, off in ((lambda i: 0, WIN_ROWS - 1), (lambda i: i, WIN_ROWS - 1 - half), (lambda i: ur - WIN_ROWS, 0)):
        rows_ = []
        for i in range(rb):
            blocks = [toep[:, j - i + off] if lo(i) <= j < lo(i) + WIN_ROWS else neg for j in range(ur)]
            rows_.append(jnp.concatenate(blocks, axis=2))
        tables.append(jnp.concatenate(rows_, axis=1))
    return jnp.stack(tables)


def _na_kernel(rows, q_ref, k_ref, v_ref, kc_ref, vc_ref, *rest):
    bias_refs, o_ref = rest[:-1], rest[-1]
    tq = NA_ROW_BLOCK * GRID_W
    nk = NA_UNION_ROWS * GRID_W
    nt = (((1,), (1,)), ((), ()))
    lane = lax.broadcasted_iota(jnp.int32, (tq, LANES), 1)
    def head(qm, ku, vu, kc, vc, bias):
        s_loc = lax.dot_general(qm, ku, nt, preferred_element_type=F32) + bias
        s_ctx = lax.dot_general(qm, kc, nt, preferred_element_type=F32)
        yield
        m = jnp.maximum(jnp.max(s_loc, axis=1, keepdims=True), jnp.max(s_ctx, axis=1, keepdims=True))
        p_loc = jnp.exp(s_loc - m)
        p_ctx = jnp.exp(s_ctx - m)
        yield
        denom = jnp.sum(p_loc, axis=1, keepdims=True) + jnp.sum(p_ctx, axis=1, keepdims=True)
        o = (jnp.dot(p_loc.astype(BF16), vu, preferred_element_type=F32)
             + jnp.dot(p_ctx.astype(BF16), vc, preferred_element_type=F32))
        yield
        return o / denom

    chains = []
    for sub, bias_ref in enumerate(bias_refs):
        rb = pl.program_id(1) * len(bias_refs) + sub
        ustart = jnp.clip(rb * NA_ROW_BLOCK - WIN_ROWS // 2, 0, rows - NA_UNION_ROWS)
        k0 = pl.multiple_of(ustart * GRID_W, GRID_W)
        qrows = slice(sub * tq, (sub + 1) * tq)
        for p in range(NA_WIDTH // LANES):
            ls = slice(p * LANES, (p + 1) * LANES)
            q = q_ref[qrows, ls]
            ku = k_ref[pl.ds(k0, nk), ls]
            vu = v_ref[pl.ds(k0, nk), ls]
            for hh in range(LANES // NA_HEAD_DIM):
                qm = jnp.where((lane // NA_HEAD_DIM) == hh, q, jnp.zeros_like(q))
                chains.append(head(qm, ku, vu, kc_ref[:, ls], vc_ref[:, ls], bias_ref[p * 2 + hh]))
    outs = _interleave(chains, NA_INTERLEAVE)
    pairs = NA_WIDTH // LANES
    for sub in range(len(bias_refs)):
        for p in range(pairs):
            i = 2 * (sub * pairs + p)
            out = jnp.where((lane // NA_HEAD_DIM) == 0, outs[i], outs[i + 1])
            o_ref[sub * tq:(sub + 1) * tq, p * LANES:(p + 1) * LANES] = out.astype(o_ref.dtype)


def _na_attention(na_qkv, na_kv_ctx, bias_tables, batch, t_len, ctx_len):
    rows = t_len // GRID_W
    n_rb = rows // NA_ROW_BLOCK
    per = NA_BLOCKS_PER_STEP
    tq = per * NA_ROW_BLOCK * GRID_W
    qkv = na_qkv.reshape(batch, t_len, 3 * NA_WIDTH)
    kvc = na_kv_ctx.reshape(batch, ctx_len, 2 * NA_WIDTH)

    def bias_spec(sub):
        def idx(b, r):
            rb = r * per + sub
            return (jnp.where(rb == 0, 0, jnp.where(rb == n_rb - 1, 2, 1)), 0, 0, 0)
        return pl.BlockSpec((None,) + bias_tables.shape[1:], idx)

    return pl.pallas_call(
        functools.partial(_na_kernel, rows),
        out_shape=jax.ShapeDtypeStruct((batch, t_len, NA_WIDTH), BF16),
        grid=(batch, n_rb // per),
        in_specs=[pl.BlockSpec((None, tq, NA_WIDTH), lambda b, r: (b, r, 0)),
                  pl.BlockSpec((None, t_len, NA_WIDTH), lambda b, r: (b, 0, 1)),
                  pl.BlockSpec((None, t_len, NA_WIDTH), lambda b, r: (b, 0, 2)),
                  pl.BlockSpec((None, ctx_len, NA_WIDTH), lambda b, r: (b, 0, 0)),
                  pl.BlockSpec((None, ctx_len, NA_WIDTH), lambda b, r: (b, 0, 1))]
                 + [bias_spec(sub) for sub in range(per)],
        out_specs=pl.BlockSpec((None, tq, NA_WIDTH), lambda b, r: (b, r, 0)),
        compiler_params=_params(2, vmem=NA_VMEM_LIMIT),
        name="na_attention",
    )(qkv, qkv, qkv, kvc, kvc, *([bias_tables] * per))


def _gla_constants(c):
    tris, masks = [], []
    for reverse in (False, True):
        i = np.arange(c)[:, None]
        j = np.arange(c)[None, :]
        tris.append((j >= i) if reverse else (j <= i))
        i = np.arange(c // 2)[:, None]
        j = np.arange(c // 2)[None, :]
        if reverse:
            i, j = j, i
        level = []
        s = c // 4
        while s >= GLA_DIAG:
            level.append(((i // (2 * s)) == (j // (2 * s))) & ((i % (2 * s)) >= s) & ((j % (2 * s)) < s))
            s //= 2
        level.append(((i // GLA_DIAG) == (j // GLA_DIAG)) & (j <= i))
        masks.append(np.stack(level))
    return jnp.asarray(np.stack(tris), BF16), jnp.asarray(np.stack(masks), F32)


def _block_refs(cum, s, reverse, diag):
    c = cum.shape[0]
    span = s if diag else 2 * s
    parts = []
    for p in range(c // span):
        if diag:
            r = p * span + (span - 1 if reverse else 0)
        else:
            r = p * span + (s - 1 if reverse else s)
        parts.append(jnp.broadcast_to(cum[r:r + 1, :], (span, cum.shape[1])))
    return jnp.concatenate(parts, axis=0)


def _cumsum_rows(a_hl, tri):
    parts = jnp.dot(tri, a_hl, preferred_element_type=F32)
    w = a_hl.shape[1] // 2
    return parts[:, :w] + parts[:, w:]


def _gla_chunk(q, k, v, a, state_t, tri, mask_ref, reverse, want_out):
    c = k.shape[0]
    hc = c // 2
    nt = (((1,), (1,)), ((), ()))
    k = k.astype(F32)
    q = q.astype(F32) if want_out else None
    cum = _cumsum_rows(a, tri)
    last = 0 if reverse else c - 1
    total = cum[last:last + 1, :]
    yield

    def scale(x, log2_factor):
        return (x * jnp.exp2(log2_factor)).astype(BF16)

    out = None
    if want_out:
        out = lax.dot_general(scale(q, cum), state_t.astype(BF16), nt, preferred_element_type=F32)
        halves = (slice(hc, c), slice(0, hc)) if reverse else (slice(0, hc), slice(hc, c))
        early, late = halves
        r = hc - 1 if reverse else hc
        g = cum[r:r + 1, :]
        cross = lax.dot_general(scale(q[late], cum[late] - g), scale(k[early], g - cum[early]), nt,
                                preferred_element_type=F32)
        yield
        inner = [jnp.zeros((hc, hc), F32), jnp.zeros((hc, hc), F32)]
        s = hc // 2
        level = 0
        while True:
            diag = s < GLA_DIAG
            for x, rows in enumerate(halves):
                cx, qx, kx = cum[rows], q[rows], k[rows]
                d = cx - _block_refs(cx, GLA_DIAG if diag else s, reverse, diag)
                if diag:
                    qs, ks = scale(qx, d), scale(kx, -d)
                else:
                    e = jnp.exp2(-jnp.abs(d))
                    qs, ks = (qx * e).astype(BF16), (kx * e).astype(BF16)
                inner[x] = inner[x] + lax.dot_general(qs, ks, nt, preferred_element_type=F32) * mask_ref[level]
            yield
            if diag:
                break
            s //= 2
            level += 1
        out_early = jnp.dot(inner[0].astype(BF16), v[early], preferred_element_type=F32)
        out_late = jnp.dot(jnp.concatenate([cross, inner[1]], axis=1).astype(BF16),
                           jnp.concatenate([v[early], v[late]], axis=0), preferred_element_type=F32)
        intra = [out_late, out_early] if reverse else [out_early, out_late]
        out = out + jnp.concatenate(intra, axis=0)
        yield
    upd = lax.dot_general(v, scale(k, total - cum), (((0,), (0,)), ((), ())), preferred_element_type=F32)
    return out, jnp.exp2(total) * state_t + upd


def _gla_kernel(qf_ref, kf_ref, vf_ref, af_ref, qb_ref, kb_ref, vb_ref, ab_ref,
                kc_ref, vc_ref, acf_ref, acb_ref, tri_ref, mask_ref,
                of_ref, ob_ref, sf_ref, sb_ref):
    c = pl.program_id(2)
    fm, bm = mask_ref.at[0], mask_ref.at[1]
    heads = sf_ref.shape[0]

    @pl.when(c == 0)
    def _():
        zero = jnp.zeros(sf_ref.shape[1:], F32)
        chains = []
        for h in range(heads):
            kc = kc_ref[:, h * GLA_DK:(h + 1) * GLA_DK]
            vc = vc_ref[:, h * GLA_DV:(h + 1) * GLA_DV]
            hl = slice(2 * h * GLA_DK, 2 * (h + 1) * GLA_DK)
            chains.append(_gla_chunk(None, kc, vc, acf_ref[:, hl], zero, tri_ref[0], fm, False, False))
            chains.append(_gla_chunk(None, kc, vc, acb_ref[:, hl], zero, tri_ref[1], bm, True, False))
        for i, (_, state) in enumerate(_interleave(chains)):
            (sb_ref if i % 2 else sf_ref)[i // 2] = state

    @pl.when(c > 0)
    def _():
        chains = []
        for h in range(heads):
            ks = slice(h * GLA_DK, (h + 1) * GLA_DK)
            vs = slice(h * GLA_DV, (h + 1) * GLA_DV)
            hl = slice(2 * h * GLA_DK, 2 * (h + 1) * GLA_DK)
            chains.append(_gla_chunk(qf_ref[:, ks], kf_ref[:, ks], vf_ref[:, vs], af_ref[:, hl], sf_ref[h],
                                     tri_ref[0], fm, False, True))
            chains.append(_gla_chunk(qb_ref[:, ks], kb_ref[:, ks], vb_ref[:, vs], ab_ref[:, hl], sb_ref[h],
                                     tri_ref[1], bm, True, True))
        for i, (o, state) in enumerate(_interleave(chains, GLA_INTERLEAVE)):
            h = i // 2
            vs = slice(h * GLA_DV, (h + 1) * GLA_DV)
            if i % 2:
                ob_ref[:, vs] = o.astype(ob_ref.dtype)
                sb_ref[h] = state
            else:
                of_ref[:, vs] = o.astype(of_ref.dtype)
                sf_ref[h] = state


def _gla(gla_qk, vb, decay, k_ctx, v_ctx, decay_ctx, batch, t_len, ctx_len):
    nc = t_len // GLA_CHUNK
    h = GLA_HEADS
    qk = gla_qk.reshape(batch, t_len, 2 * GLA_QK_WIDTH)
    v3 = vb.reshape(batch, t_len, GLA_V_WIDTH)
    a3 = decay.reshape(batch, t_len, GLA_DECAY_WIDTH)
    kc3 = k_ctx.reshape(batch, ctx_len, GLA_QK_WIDTH)
    vc3 = v_ctx.reshape(batch, ctx_len, GLA_V_WIDTH)
    ac3 = decay_ctx.reshape(batch, ctx_len, GLA_DECAY_WIDTH)
    tri, masks = _gla_constants(GLA_CHUNK)

    def fwd(c):
        return jnp.maximum(c - 1, 0)

    def bwd(c):
        return nc - 1 - jnp.maximum(c - 1, 0)

    def const(a):
        return pl.BlockSpec(a.shape, lambda b, hh, c: (0,) * a.ndim, pipeline_mode=pl.Buffered(1))

    hp = GLA_HEADS_PER_STEP
    groups = h // hp
    cq = (None, GLA_CHUNK, hp * GLA_DK)
    cv = (None, GLA_CHUNK, hp * GLA_DV)
    ca = (None, GLA_CHUNK, hp * 2 * GLA_DK)
    cca = (None, ctx_len, hp * 2 * GLA_DK)
    in_specs = [
        pl.BlockSpec(cq, lambda b, g, c: (b, fwd(c), g)),
        pl.BlockSpec(cq, lambda b, g, c: (b, fwd(c), groups + g)),
        pl.BlockSpec(cv, lambda b, g, c: (b, fwd(c), g)),
        pl.BlockSpec(ca, lambda b, g, c: (b, fwd(c), g)),
        pl.BlockSpec(cq, lambda b, g, c: (b, bwd(c), g)),
        pl.BlockSpec(cq, lambda b, g, c: (b, bwd(c), groups + g)),
        pl.BlockSpec(cv, lambda b, g, c: (b, bwd(c), g)),
        pl.BlockSpec(ca, lambda b, g, c: (b, bwd(c), groups + g)),
        pl.BlockSpec((None, ctx_len, hp * GLA_DK), lambda b, g, c: (b, 0, g)),
        pl.BlockSpec((None, ctx_len, hp * GLA_DV), lambda b, g, c: (b, 0, g)),
        pl.BlockSpec(cca, lambda b, g, c: (b, 0, g)),
        pl.BlockSpec(cca, lambda b, g, c: (b, 0, groups + g)),
        const(tri), const(masks),
    ]
    out_specs = [pl.BlockSpec(cv, lambda b, g, c: (b, fwd(c), g)),
                 pl.BlockSpec(cv, lambda b, g, c: (b, bwd(c), g))]
    out_shape = [jax.ShapeDtypeStruct((batch, t_len, GLA_V_WIDTH), BF16)] * 2
    return pl.pallas_call(
        _gla_kernel,
        out_shape=out_shape,
        grid=(batch, groups, nc + 1),
        in_specs=in_specs,
        out_specs=out_specs,
        scratch_shapes=[pltpu.VMEM((hp, GLA_DV, GLA_DK), F32), pltpu.VMEM((hp, GLA_DV, GLA_DK), F32)],
        compiler_params=_params(3),
        name="gla",
    )(qk, qk, v3, a3, qk, qk, v3, a3, kc3, vc3, ac3, ac3, tri, masks)


def _decay_weights(w_f, b_f, w_b, b_b):
    r, width = w_f.shape
    w2 = jnp.zeros((LANES, 2 * width), F32).at[:r, :width].set(w_f).at[r:2 * r, width:].set(w_b)
    return w2.astype(BF16), jnp.concatenate([b_f, b_b]).reshape(1, 2 * width)


def _layer_norm(v, g, b):
    mu = jnp.mean(v, axis=1, keepdims=True)
    var = jnp.mean(jnp.square(v - mu), axis=1, keepdims=True)
    return (v - mu) * lax.rsqrt(var + EPS) * g + b


def _merge_kernel(oa_ref, of_ref, ob_ref, gb_ref, gates_ref, x_ref, g1_ref, sc2_ref, sh2_ref,
                  wa_ref, wb_ref, wo_ref, nw_ref, lg_ref, lb_ref, rw_ref,
                  x1_ref, h2_ref, sc_ref):
    d = x_ref.shape[1]
    o = of_ref[...].astype(F32) + ob_ref[...].astype(F32)
    pieces = []
    for hh in range(GLA_HEADS):
        oh = o[:, hh * GLA_DV:(hh + 1) * GLA_DV]
        pieces.append(oh * lax.rsqrt(jnp.mean(jnp.square(oh), axis=1, keepdims=True) + EPS))
    out_b = jnp.concatenate(pieces, axis=1) * nw_ref[...] * _silu(gb_ref[...].astype(F32))
    ya = jnp.dot(oa_ref[...], wa_ref[...], preferred_element_type=F32)
    yb = jnp.dot(out_b.astype(BF16), wb_ref[...], preferred_element_type=F32)
    y = _sigmoid(gates_ref[:, :d].astype(F32)) * ya + _sigmoid(gates_ref[:, d:].astype(F32)) * yb
    y2 = jnp.dot(y.astype(BF16), wo_ref[...], preferred_element_type=F32)
    x1 = _layer_norm(DEEPNORM_ALPHA * x_ref[...] + g1_ref[0] * y2, lg_ref[...], lb_ref[...])
    x1_ref[...] = x1
    h2 = x1 * (1.0 + sc2_ref[0]) + sh2_ref[0]
    h2_ref[...] = _pack_pairs(h2)
    logits_t = lax.dot_general(rw_ref[...], h2.astype(BF16), (((1,), (1,)), ((), ())), preferred_element_type=F32)
    sc_ref[...] = _sigmoid(logits_t)


def _merge(out_a, o_f, o_b, gb, gates, x2d, g1, sc2, sh2, wa, wb, wo, nw, lg, lb, rw_t, tile, tiles_per_batch):
    n, d = x2d.shape
    row = lambda i: (i, 0)
    mod = lambda i: (i // tiles_per_batch, 0, 0)
    full = lambda i: (0, 0)

    def const(a):
        return pl.BlockSpec(a.shape, full, pipeline_mode=pl.Buffered(1))

    in_specs = [pl.BlockSpec((tile, NA_WIDTH), row), pl.BlockSpec((tile, GLA_V_WIDTH), row),
                pl.BlockSpec((tile, GLA_V_WIDTH), row), pl.BlockSpec((tile, GLA_V_WIDTH), row),
                pl.BlockSpec((tile, 2 * d), row), pl.BlockSpec((tile, d), row),
                pl.BlockSpec((1, 1, d), mod), pl.BlockSpec((1, 1, d), mod), pl.BlockSpec((1, 1, d), mod),
                const(wa), const(wb), const(wo), const(nw), const(lg), const(lb), const(rw_t)]
    out_shape = [jax.ShapeDtypeStruct((n, d), F32), jax.ShapeDtypeStruct((n, d // 2), U32),
                 jax.ShapeDtypeStruct((N_EXPERTS, n), F32)]
    out_specs = [pl.BlockSpec((tile, d), row), pl.BlockSpec((tile, d // 2), row),
                 pl.BlockSpec((N_EXPERTS, tile), lambda i: (0, i))]
    return pl.pallas_call(
        _merge_kernel, out_shape=out_shape, grid=(n // tile,), in_specs=in_specs, out_specs=out_specs,
        compiler_params=_params(1), name="merge_ln1_router",
    )(out_a, o_f, o_b, gb, gates, x2d, g1, sc2, sh2, wa, wb, wo, nw, lg, lb, rw_t)


def _first_argmax(vals, idx, n):
    m = jnp.max(vals, axis=0, keepdims=True)
    first = jnp.min(jnp.where(vals == m, idx, float(n)), axis=0, keepdims=True)
    return m, first


def _route_kernel(sc_ref, bias_ref, before_ref, e_ref, w_ref, rank_ref, cnt_ref, carry_ref):
    step = pl.program_id(0)
    tr = sc_ref.shape[1]

    @pl.when(step == 0)
    def _():
        carry_ref[...] = jnp.zeros(carry_ref.shape, F32)

    scores = sc_ref[...]
    biased = scores + bias_ref[...]
    eidx = lax.broadcasted_iota(jnp.int32, (N_EXPERTS, tr), 0).astype(F32)
    lidx = lax.broadcasted_iota(jnp.int32, (GROUP_SIZE, tr), 0).astype(F32)
    gidx = lax.broadcasted_iota(jnp.int32, (N_GROUPS, tr), 0).astype(F32)
    gs = []
    for g in range(N_GROUPS):
        blk = biased[g * GROUP_SIZE:(g + 1) * GROUP_SIZE]
        m1, first = _first_argmax(blk, lidx, GROUP_SIZE)
        m2 = jnp.max(jnp.where(lidx == first, -jnp.inf, blk), axis=0, keepdims=True)
        gs.append(m1 + m2)
    cur = jnp.concatenate(gs, axis=0)
    keep = jnp.zeros((N_GROUPS, tr), F32)
    for _ in range(TOPK_GROUPS):
        _, first = _first_argmax(cur, gidx, N_GROUPS)
        sel = gidx == first
        keep = jnp.where(sel, 1.0, keep)
        cur = jnp.where(sel, -jnp.inf, cur)
    keep_e = jnp.concatenate([jnp.broadcast_to(keep[g:g + 1], (GROUP_SIZE, tr)) for g in range(N_GROUPS)], axis=0)
    masked = jnp.where(keep_e > 0.5, biased, -jnp.inf)
    chosen = jnp.zeros((N_EXPERTS, tr), F32)
    tops, topi = [], []
    for _ in range(TOP_K):
        _, first = _first_argmax(masked, eidx, N_EXPERTS)
        sel = eidx == first
        tops.append(jnp.sum(jnp.where(sel, scores, 0.0), axis=0, keepdims=True))
        topi.append(first)
        chosen = jnp.where(sel, 1.0, chosen)
        masked = jnp.where(sel, -jnp.inf, masked)
    top_s = jnp.concatenate(tops, axis=0)
    top_i = jnp.concatenate(topi, axis=0)
    e_ref[...] = top_i.astype(jnp.int32)
    w_ref[...] = (top_s / jnp.sum(top_s, axis=0, keepdims=True) * ROUTED_SCALE).T
    prior = jnp.dot(chosen.astype(BF16), before_ref[...], preferred_element_type=F32) + carry_ref[...]
    ranks = [jnp.sum(jnp.where(eidx == topi[kk], prior, 0.0), axis=0, keepdims=True) for kk in range(TOP_K)]
    rank_ref[...] = jnp.concatenate(ranks, axis=0).astype(jnp.int32)
    carry_ref[...] = carry_ref[...] + jnp.sum(chosen, axis=1, keepdims=True)
    cnt_ref[...] = jnp.broadcast_to(carry_ref[...], cnt_ref.shape).astype(jnp.int32)


def _route(scores_t, router_bias, tile):
    n = scores_t.shape[1]
    col = lambda i: (0, i)
    out_shape = [jax.ShapeDtypeStruct((TOP_K, n), jnp.int32), jax.ShapeDtypeStruct((n, TOP_K), F32),
                 jax.ShapeDtypeStruct((TOP_K, n), jnp.int32), jax.ShapeDtypeStruct((N_EXPERTS, LANES), jnp.int32)]
    before = jnp.asarray(np.arange(tile)[:, None] < np.arange(tile)[None, :], BF16)
    return pl.pallas_call(
        _route_kernel, out_shape=out_shape, grid=(n // tile,),
        in_specs=[pl.BlockSpec((N_EXPERTS, tile), col), pl.BlockSpec((N_EXPERTS, 1), lambda i: (0, 0)),
                  pl.BlockSpec((tile, tile), lambda i: (0, 0), pipeline_mode=pl.Buffered(1))],
        out_specs=[pl.BlockSpec((TOP_K, tile), col), pl.BlockSpec((tile, TOP_K), lambda i: (i, 0)),
                   pl.BlockSpec((TOP_K, tile), col), pl.BlockSpec((N_EXPERTS, LANES), lambda i: (0, 0))],
        scratch_shapes=[pltpu.VMEM((N_EXPERTS, 1), F32)],
        compiler_params=_params(1), name="route",
    )(scores_t, router_bias.reshape(N_EXPERTS, 1), before)


def _slots_kernel(e_ref, rank_ref, start_ref, dest_ref):
    tr = e_ref.shape[1]
    eidx = lax.broadcasted_iota(jnp.int32, (N_EXPERTS, tr), 0)
    e = e_ref[...]
    start = start_ref[...]
    rows = [jnp.sum(jnp.where(eidx == e[kk:kk + 1], start, 0.0), axis=0, keepdims=True) for kk in range(TOP_K)]
    dest_ref[...] = jnp.concatenate(rows, axis=0).astype(jnp.int32) + rank_ref[...]


def _slots(top_e, rank, start_rows, tile):
    n = top_e.shape[1]
    col = lambda i: (0, i)
    return pl.pallas_call(
        _slots_kernel, out_shape=jax.ShapeDtypeStruct((TOP_K, n), jnp.int32), grid=(n // tile,),
        in_specs=[pl.BlockSpec((TOP_K, tile), col), pl.BlockSpec((TOP_K, tile), col),
                  pl.BlockSpec((N_EXPERTS, 1), lambda i: (0, 0))],
        out_specs=pl.BlockSpec((TOP_K, tile), col),
        compiler_params=_params(1), name="slots",
    )(top_e, rank, start_rows)


def _sc_mesh():
    return plsc.VectorSubcoreMesh(core_axis_name="c", subcore_axis_name="s",
                                  num_cores=SC_CORES, num_subcores=SC_SUBCORES)


def _sc_scatter_rows(rows, idx, n_out):
    n, dp = rows.shape
    copies = idx.shape[0] // n
    per_worker = n // (SC_CORES * SC_SUBCORES)
    chunk = SC_GATHER_ROWS

    @functools.partial(
        pl.kernel, mesh=_sc_mesh(), out_type=jax.ShapeDtypeStruct((n_out, dp), rows.dtype),
        scratch_types=[pltpu.VMEM((copies, chunk), jnp.int32), pltpu.VMEM((chunk, dp), rows.dtype),
                       pltpu.SemaphoreType.DMA, pltpu.SemaphoreType.DMA],
        name="sc_scatter_rows")
    def scatter(rows_hbm, idx_hbm, out_hbm, idx_v, rows_v, sem_idx, sem_out):
        base = (lax.axis_index("s") * SC_CORES + lax.axis_index("c")) * per_worker

        @pl.loop(0, per_worker // chunk)
        def _(j):
            off = base + j * chunk
            loads = [pltpu.async_copy(idx_hbm.at[pl.ds(k * n + off, chunk)], idx_v.at[k], sem_idx)
                     for k in range(copies)]
            pltpu.sync_copy(rows_hbm.at[pl.ds(off, chunk)], rows_v)
            for cp in loads:
                cp.wait()
            stores = [pltpu.async_copy(rows_v, out_hbm.at[idx_v.at[k]], sem_out) for k in range(copies)]
            for cp in stores:
                cp.wait()

    return scatter(rows, idx)


def _expert_kernel(start_ref, cnt_ref, nused_ref, xs_ref, wg_ref, wu_ref, wd_ref, y_ref,
                   wgb, wub, wdb, xbuf, ybuf, sem_in, sem_out):
    e = pl.program_id(0)
    bm, ring = EXPERT_BLOCK, EXPERT_RING
    cnt = cnt_ref[e]
    nb = (cnt + bm - 1) // bm
    g0 = start_ref[e]
    n_used = nused_ref[0]

    def x_copy(g):
        slot = g % ring
        return pltpu.make_async_copy(xs_ref.at[pl.ds(g * bm, bm)], xbuf.at[slot], sem_in.at[slot])

    def y_copy(g):
        slot = g % ring
        return pltpu.make_async_copy(ybuf.at[slot], y_ref.at[pl.ds(g * bm, bm)], sem_out.at[slot])

    @pl.when(e == 0)
    def _():
        for g in range(ring):
            @pl.when(g < n_used)
            def _():
                x_copy(g).start()

    wgb[...] = wg_ref[...].astype(BF16)
    wub[...] = wu_ref[...].astype(BF16)
    wdb[...] = wd_ref[...].astype(BF16)
    row = lax.broadcasted_iota(jnp.int32, (bm, xbuf.shape[2]), 0)

    def blocks(j, count):
        for b in range(count):
            g = g0 + j + b
            x_copy(g).wait()

            @pl.when(g >= ring)
            def _():
                y_copy(g - ring).wait()

        def swiglu(b):
            slot = (g0 + j + b) % ring
            x = _unpack_pairs(jnp.where(row < cnt - (j + b) * bm, xbuf[slot], jnp.uint32(0))).astype(BF16)
            yield
            gate = jnp.dot(x, wgb[...], preferred_element_type=F32)
            up = jnp.dot(x, wub[...], preferred_element_type=F32)
            yield
            act = (_silu(gate) * up).astype(BF16)
            yield
            y = jnp.dot(act, wdb[...], preferred_element_type=F32)
            yield
            ybuf[slot] = _pack_pairs(y)

        _interleave([swiglu(b) for b in range(count)])

        for b in range(count):
            g = g0 + j + b
            y_copy(g).start()

            @pl.when(g + ring < n_used)
            def _():
                x_copy(g + ring).start()

    group = EXPERT_GROUP

    def full_group(p, carry):
        blocks(group * p, group)
        return carry

    lax.fori_loop(0, nb // group, full_group, 0)
    done = (nb // group) * group
    size = group // 2
    while size >= 1:
        @pl.when((nb - done) % (2 * size) >= size)
        def _(done=done, size=size):
            blocks(done, size)

        done = done + jnp.where((nb - done) % (2 * size) >= size, size, 0)
        size //= 2

    @pl.when(e == pl.num_programs(0) - 1)
    def _():
        for back in range(1, ring + 1):
            @pl.when(n_used - back >= 0)
            def _():
                y_copy(n_used - back).wait()


def _experts(blk_start, counts, n_used, xs, wg, wu, wd):
    n_slots, dp = xs.shape
    n_exp, d, ff = wg.shape
    bm, ring = EXPERT_BLOCK, EXPERT_RING
    wsel = lambda e, st, ct, nu: (e, 0, 0)
    grid_spec = pltpu.PrefetchScalarGridSpec(
        num_scalar_prefetch=3, grid=(n_exp,),
        in_specs=[pl.BlockSpec(memory_space=pl.ANY),
                  pl.BlockSpec((None, d, ff), wsel), pl.BlockSpec((None, d, ff), wsel),
                  pl.BlockSpec((None, ff, d), wsel)],
        out_specs=pl.BlockSpec(memory_space=pl.ANY),
        scratch_shapes=[pltpu.VMEM((d, ff), BF16), pltpu.VMEM((d, ff), BF16), pltpu.VMEM((ff, d), BF16),
                        pltpu.VMEM((ring, bm, dp), U32), pltpu.VMEM((ring, bm, dp), U32),
                        pltpu.SemaphoreType.DMA((ring,)), pltpu.SemaphoreType.DMA((ring,))])
    return pl.pallas_call(
        _expert_kernel, out_shape=jax.ShapeDtypeStruct((n_slots, dp), U32), grid_spec=grid_spec,
        compiler_params=_params(1), name="experts",
    )(blk_start, counts, n_used, xs, wg, wu, wd)


def _sc_gather_rows(table, idx):
    n_idx = idx.shape[0]
    dp = table.shape[1]
    workers = SC_CORES * SC_SUBCORES
    per_worker = n_idx // workers
    chunk = SC_GATHER_ROWS

    @functools.partial(
        pl.kernel, mesh=_sc_mesh(), out_type=jax.ShapeDtypeStruct((n_idx, dp), table.dtype),
        scratch_types=[pltpu.VMEM((chunk,), jnp.int32), pltpu.VMEM((chunk, dp), table.dtype),
                       pltpu.SemaphoreType.DMA],
        name="sc_gather_rows")
    def gather(table_hbm, idx_hbm, out_hbm, idx_v, rows_v, sem):
        base = (lax.axis_index("s") * SC_CORES + lax.axis_index("c")) * per_worker

        @pl.loop(0, per_worker // chunk)
        def _(j):
            off = base + j * chunk
            pltpu.sync_copy(idx_hbm.at[pl.ds(off, chunk)], idx_v)
            pltpu.async_copy(table_hbm.at[idx_v], rows_v, sem).wait()
            pltpu.sync_copy(rows_v, out_hbm.at[pl.ds(off, chunk)])

    return gather(table, idx)


def _combine_kernel(tile, yt_ref, w_ref, h_ref, x1_ref, g2_ref, sg_ref, su_ref, sd_ref, lg_ref, lb_ref, *rest):
    o_ref = rest[-1]
    hb = _unpack_pairs(h_ref[...]).astype(BF16)
    g = jnp.dot(hb, sg_ref[...], preferred_element_type=F32)
    u = jnp.dot(hb, su_ref[...], preferred_element_type=F32)
    f = jnp.dot((_silu(g) * u).astype(BF16), sd_ref[...], preferred_element_type=F32)
    w = w_ref[...]
    for kk in range(TOP_K):
        f = f + w[:, kk:kk + 1] * _unpack_pairs(yt_ref[kk * tile:(kk + 1) * tile, :])
    o_ref[...] = _layer_norm(DEEPNORM_ALPHA * x1_ref[...] + g2_ref[0] * f, lg_ref[...], lb_ref[...])


def _combine(y_tok, top_w, h2p, x1, g2, sg, su, sd, lg, lb, tile, tiles_per_batch, first_tile, prev_out):
    n, d = x1.shape
    dp = h2p.shape[1]
    row = lambda i: (first_tile + i, 0)
    full = lambda i: (0, 0)

    def const(a):
        return pl.BlockSpec(a.shape, full, pipeline_mode=pl.Buffered(1))

    in_specs = [pl.BlockSpec((TOP_K * tile, dp), lambda i: (i, 0)),
                pl.BlockSpec((tile, TOP_K), row), pl.BlockSpec((tile, dp), row), pl.BlockSpec((tile, d), row),
                pl.BlockSpec((1, 1, d), lambda i: ((first_tile + i) // tiles_per_batch, 0, 0)),
                const(sg), const(su), const(sd), const(lg), const(lb)]
    args = [y_tok, top_w, h2p, x1, g2, sg, su, sd, lg, lb]
    aliases = {}
    if prev_out is not None:
        in_specs.append(pl.BlockSpec(memory_space=pl.ANY))
        args.append(prev_out)
        aliases = {len(args) - 1: 0}
    return pl.pallas_call(
        functools.partial(_combine_kernel, tile),
        out_shape=jax.ShapeDtypeStruct((n, d), F32),
        grid=(y_tok.shape[0] // (TOP_K * tile),),
        in_specs=in_specs,
        out_specs=pl.BlockSpec((tile, d), row),
        input_output_aliases=aliases,
        compiler_params=_params(1), name="combine_shared_ln2",
    )(*args)


def kernel(x, c, ctx, c_ctx, w_mod, b_mod, w_in, na_rpb, gla_w_decay_f, gla_b_decay_f, gla_w_decay_b, gla_b_decay_b,
           gla_norm_w, w_branch_a, w_branch_b, w_out, ln1_g, ln1_b, router_w, router_bias, exp_w_gate, exp_w_up,
           exp_w_down, sh_w_gate, sh_w_up, sh_w_down, ln2_g, ln2_b):
    batch, t_len, d = x.shape
    ctx_len = ctx.shape[1]
    n = batch * t_len
    assert w_mod.shape[0] == DEPTH == 1
    assert t_len % GLA_CHUNK == 0 and ctx_len == GLA_CHUNK and (t_len // GRID_W) % (NA_ROW_BLOCK * NA_BLOCKS_PER_STEP) == 0

    mod_rows = 16
    c_all = jnp.zeros((mod_rows, d), F32).at[:batch].set(c).at[batch].set(c_ctx)
    mod = _modulation(c_all, w_mod[0], b_mod[0])
    sh1, sc1, g1, sh2, sc2, g2 = [mod[:batch, j * d:(j + 1) * d].reshape(batch, 1, d) for j in range(6)]
    sh1c = mod[batch:batch + 1, 0:d].reshape(1, 1, d)
    sc1c = mod[batch:batch + 1, d:2 * d].reshape(1, 1, d)

    offs = np.cumsum((0, NA_WIDTH, NA_WIDTH, NA_WIDTH, GLA_QK_WIDTH, GLA_QK_WIDTH, GLA_V_WIDTH, GLA_V_WIDTH,
                      GLA_GATE_RANK, GLA_GATE_RANK, d, d))
    qa, ka, va, qb, kb, vbc, gbc, lrf, lrb, ga, gbt = [w_in[0][:, offs[j]:offs[j + 1]] for j in range(11)]
    lr_cols = jnp.concatenate([lrf, lrb, jnp.zeros((d, LANES - 2 * GLA_GATE_RANK), F32)], axis=1)
    w_lat = jnp.concatenate([lr_cols, qa, ka, va, qb, kb, vbc, gbc, ga, gbt], axis=1).astype(BF16)
    lat_offs = np.cumsum((0, LANES, NA_WIDTH, NA_WIDTH, NA_WIDTH, GLA_QK_WIDTH, GLA_QK_WIDTH, GLA_V_WIDTH))
    w_ctx = jnp.concatenate([w_lat[:, lat_offs[j]:lat_offs[j + 1]] for j in (0, 2, 3, 5, 6)], axis=1)
    plain = ("plain",)
    lat_plan = ((LANES, (("decay",),)),
                (3 * NA_WIDTH, (("scale", NA_HEAD_DIM ** -0.5), plain, plain)),
                (2 * GLA_QK_WIDTH, (("rope", GLA_DK ** -0.5), ("rope", 1.0))),
                (GLA_V_WIDTH, (plain, plain)), (GLA_V_WIDTH, (plain, plain)),
                (2 * d, (plain,) * 4))
    ctx_plan = ((LANES, (("decay",),)),
                (2 * NA_WIDTH, (plain, plain)), (GLA_QK_WIDTH, (plain,)), (GLA_V_WIDTH, (plain, plain)))
    w2, b2 = _decay_weights(gla_w_decay_f[0], gla_b_decay_f[0], gla_w_decay_b[0], gla_b_decay_b[0])
    tile = COMBINE_TILE
    x2d = x.reshape(n, d)
    decay, na_qkv, gla_qk, vb, gb, gates = _projection(
        x2d, sc1, sh1, w_lat, w2, b2, lat_plan, (BF16,) * 6, PROJ_TILE, t_len // PROJ_TILE,
        rope=_rope_tables(t_len))
    decay_c, na_kv_c, k_c, v_c = _projection(
        ctx.reshape(batch * ctx_len, d), sc1c, sh1c, w_ctx, w2, b2, ctx_plan, (BF16,) * 4, tile,
        batch * ctx_len // tile)

    out_a = _na_attention(na_qkv, na_kv_c, _na_bias_tables(na_rpb[0]), batch, t_len, ctx_len)
    o_f, o_b = _gla(gla_qk, vb, decay, k_c, v_c, decay_c, batch, t_len, ctx_len)

    x1, h2p, scores_t = _merge(
        out_a.reshape(n, NA_WIDTH), o_f.reshape(n, GLA_V_WIDTH), o_b.reshape(n, GLA_V_WIDTH), gb, gates, x2d,
        g1, sc2, sh2, w_branch_a[0].astype(BF16), w_branch_b[0].astype(BF16), w_out[0].astype(BF16),
        gla_norm_w[0].reshape(1, -1), ln1_g[0].reshape(1, d), ln1_b[0].reshape(1, d),
        router_w[0].T.astype(BF16), PROJ_TILE, t_len // PROJ_TILE)

    top_e, top_w, rank, counts = _route(scores_t, router_bias[0], 512)

    counts = counts[:, 0]
    n_blocks = pl.cdiv(n * TOP_K, EXPERT_BLOCK) + N_EXPERTS
    blocks_per = (counts + EXPERT_BLOCK - 1) // EXPERT_BLOCK
    blk_end = jnp.cumsum(blocks_per)
    blk_start = blk_end - blocks_per
    dest = _slots(top_e, rank, (blk_start * EXPERT_BLOCK).astype(F32).reshape(N_EXPERTS, 1), 2048)

    xs = _sc_scatter_rows(h2p, dest.reshape(TOP_K * n), n_blocks * EXPERT_BLOCK)
    y = _experts(blk_start.astype(jnp.int32), counts, blk_end[-1:].astype(jnp.int32), xs,
                 exp_w_gate[0], exp_w_up[0], exp_w_down[0])
    dest_tok = dest.reshape(TOP_K, n // tile, tile).transpose(1, 0, 2).reshape(COMBINE_CHUNKS, -1)
    w_rows = top_w
    shared = (sh_w_gate[0].astype(BF16), sh_w_up[0].astype(BF16), sh_w_down[0].astype(BF16),
              ln2_g[0].reshape(1, d), ln2_b[0].reshape(1, d))
    tiles_per_chunk = n // tile // COMBINE_CHUNKS
    out = None
    for ci in range(COMBINE_CHUNKS):
        y_tok = _sc_gather_rows(y, dest_tok[ci])
        out = _combine(y_tok, w_rows, h2p, x1, g2, *shared, tile, t_len // tile, ci * tiles_per_chunk, out)
    return out.reshape(batch, t_len, d)
```
